```python
import jax, jax.numpy as jnp
from jax import lax
import numpy as np

D_MODEL = 1024
BATCH = 16
SEQ = 256
DEPTH = 2
DEC_BATCH = 4
DEC_SEQ = 1024
PAST_LEN = 256

GRID_W = 64
N_EVEN = (DEPTH + 1) // 2
N_ODD = DEPTH // 2
H_A = 4
DK_A = 128
DV_A = 128
H_B = 4
DK_B = 128
DV_B = 128
SHORT_CONV = 5
C_CONV = H_B * (2 * DK_B + DV_B)
CHUNK = 32
H_C = 16
HD_C = 64
KH_MAX = 8
KW = 16
QBW = 16
KBW = QBW + KW
Q_BLOCK = 128
D_FF = 4 * D_MODEL
EPS = 1e-6
NEG_INF = -1e30

_AB_SIZES = (H_A * DK_A, H_A * DK_A, H_A * DK_A, H_A * DV_A, H_A * DV_A,
             C_CONV, H_B * DV_B, H_B, H_B, H_B, H_B)
D_IN_AB = 3 * H_A * DK_A + 2 * H_A * DV_A + C_CONV + H_B * DV_B + 4 * H_B
D_MIX_AB = H_A * DV_A + H_B * DV_B

kernel_name = 'hybrid_hgrn2_gdn_natten_prefix_dit'


def _split(x, sizes):
    offs, acc = [], 0
    for s in sizes[:-1]:
        acc += s
        offs.append(acc)
    return jnp.split(x, offs, axis=-1)


def _rmsnorm(x, g):
    x32 = x.astype(jnp.float32)
    y = x32 * lax.rsqrt(jnp.mean(x32 * x32, axis=-1, keepdims=True) + EPS)
    return (y * g.astype(jnp.float32)).astype(x.dtype)


def _l2norm(x):
    return x * lax.rsqrt(jnp.sum(x * x, axis=-1, keepdims=True) + EPS)


def _heads(x, h):
    b, t, _ = x.shape
    return x.reshape(b, t, h, -1).transpose(0, 2, 1, 3)


def _merge(x):
    b, h, t, d = x.shape
    return x.transpose(0, 2, 1, 3).reshape(b, t, h * d)


def _modulation(cond, w, b):
    m = jax.nn.silu(cond) @ w + b
    return jnp.split(m[:, None, :], 6, axis=-1)


def _short_conv(x, w):
    return lax.conv_general_dilated(
        x, w.astype(x.dtype), window_strides=(1,),
        padding=[(SHORT_CONV // 2, SHORT_CONV // 2)],
        dimension_numbers=('NWC', 'WIO', 'NWC'), feature_group_count=x.shape[-1])


def _gla_chunked(q, k, v, log_f, s0):
    b, h, t, _ = q.shape
    n = t // CHUNK
    rs = lambda a: a.reshape(b, h, n, CHUNK, a.shape[-1])
    q, k, v, log_f = rs(q), rs(k), rs(v), rs(log_f)
    g = jnp.cumsum(log_f, axis=3)
    tri = jnp.tril(jnp.ones((CHUNK, CHUNK), bool))[:, :, None]
    diff = g[:, :, :, :, None, :] - g[:, :, :, None, :, :]
    decay = jnp.where(tri, jnp.exp(jnp.where(tri, diff, 0.0)), 0.0)
    attn = jnp.einsum('bhntd,bhnsd,bhntsd->bhnts', q, k, decay)
    o_intra = jnp.einsum('bhnts,bhnsv->bhntv', attn, v)
    g_last = g[:, :, :, -1:, :]
    u = jnp.einsum('bhnsd,bhnsv->bhndv', k * jnp.exp(g_last - g), v)
    a_chunk = jnp.exp(g_last[:, :, :, 0])

    def step(s, inp):
        a_n, u_n = inp
        return a_n[..., None] * s + u_n, s

    s_final, s_before = lax.scan(step, s0, (jnp.moveaxis(a_chunk, 2, 0), jnp.moveaxis(u, 2, 0)))
    s_before = jnp.moveaxis(s_before, 0, 2)
    o_inter = jnp.einsum('bhntd,bhndv->bhntv', q * jnp.exp(g), s_before)
    return (o_intra + o_inter).reshape(b, h, t, -1), s_final


def _gated_delta_chunked(q, k, v, glog, beta, s0):
    b, h, t, _ = q.shape
    n = t // CHUNK
    rs = lambda a: a.reshape(b, h, n, CHUNK, a.shape[-1])
    q, k, v = rs(q), rs(k), rs(v)
    g = jnp.cumsum(glog.reshape(b, h, n, CHUNK), axis=-1)
    beta = beta.reshape(b, h, n, CHUNK)
    tri = jnp.tril(jnp.ones((CHUNK, CHUNK), bool))
    strict = jnp.tril(jnp.ones((CHUNK, CHUNK), bool), -1)
    diff = g[..., :, None] - g[..., None, :]
    L = jnp.where(tri, jnp.exp(jnp.where(tri, diff, 0.0)), 0.0)
    kb = k * beta[..., None]
    A = jnp.where(strict, jnp.einsum('bhntd,bhnsd->bhnts', kb, k) * L, 0.0)
    T = jnp.eye(CHUNK, dtype=A.dtype) + A
    u = lax.linalg.triangular_solve(T, v * beta[..., None], left_side=True, lower=True, unit_diagonal=True)
    w = lax.linalg.triangular_solve(T, kb * jnp.exp(g)[..., None], left_side=True, lower=True, unit_diagonal=True)
    attn = jnp.where(tri, jnp.einsum('bhntd,bhnsd->bhnts', q, k) * L, 0.0)
    qg = q * jnp.exp(g)[..., None]
    g_last = g[..., -1]
    kdec = k * jnp.exp(g_last[..., None] - g)[..., None]

    def step(s, inp):
        u_n, w_n, attn_n, qg_n, kdec_n, gl_n = inp
        v_new = u_n - jnp.einsum('bhtd,bhdv->bhtv', w_n, s)
        o_n = jnp.einsum('bhtd,bhdv->bhtv', qg_n, s) + jnp.einsum('bhts,bhsv->bhtv', attn_n, v_new)
        s = jnp.exp(gl_n)[..., None, None] * s + jnp.einsum('bhsd,bhsv->bhdv', kdec_n, v_new)
        return s, o_n

    xs = tuple(jnp.moveaxis(a, 2, 0) for a in (u, w, attn, qg, kdec, g_last))
    s_final, o = lax.scan(step, s0, xs)
    return jnp.moveaxis(o, 0, 2).reshape(b, h, t, -1), s_final


def _mixer_ab(h, w_in, w_out, lb, conv_w, a_log, dt_bias, gn_a, gn_b, s_hgrn, s_gdn):
    f32 = jnp.float32
    q_a, f_fw, f_bw, i_a, g_a, qkv_b, g_b, a_fw, a_bw, b_fw, b_bw = _split(h @ w_in, _AB_SIZES)
    lbh = lb.reshape(H_A, 1, DK_A)
    qa = _heads(jax.nn.silu(q_a), H_A).astype(f32) * DK_A ** -0.5
    va = _heads(i_a, H_A).astype(f32)
    s_h = s_hgrn.astype(f32)

    def hgrn_dir(f_raw, s0, flip):
        f = lbh + (1.0 - lbh) * jax.nn.sigmoid(_heads(f_raw, H_A).astype(f32))
        args = [qa, 1.0 - f, va, jnp.log(f)]
        if flip:
            args = [jnp.flip(a, axis=2) for a in args]
        o, s = _gla_chunked(*args, s0)
        return (jnp.flip(o, axis=2) if flip else o), s

    oa_f, sa_f = hgrn_dir(f_fw, s_h[:, 0], False)
    oa_b, sa_b = hgrn_dir(f_bw, s_h[:, 1], True)
    o_a = _rmsnorm(oa_f + oa_b, gn_a) * jax.nn.silu(_heads(g_a, H_A).astype(f32))
    qkv = jax.nn.silu(_short_conv(qkv_b, conv_w))
    q_b, k_b, v_b = _split(qkv, (H_B * DK_B, H_B * DK_B, H_B * DV_B))
    qb = _l2norm(_heads(q_b, H_B).astype(f32)) * DK_B ** -0.5
    kb = _l2norm(_heads(k_b, H_B).astype(f32))
    vb = _heads(v_b, H_B).astype(f32)
    s_g = s_gdn.astype(f32)

    def gdn_dir(a_raw, b_raw, d, s0, flip):
        beta = jax.nn.sigmoid(b_raw.astype(f32)).transpose(0, 2, 1)
        glog = (-jnp.exp(a_log[d].astype(f32))
                * jax.nn.softplus(a_raw.astype(f32) + dt_bias[d].astype(f32))).transpose(0, 2, 1)
        args = [qb, kb, vb, glog, beta]
        if flip:
            args = [jnp.flip(a, axis=2) for a in args]
        o, s = _gated_delta_chunked(*args, s0)
        return (jnp.flip(o, axis=2) if flip else o), s

    ob_f, sb_f = gdn_dir(a_fw, b_fw, 0, s_g[:, 0], False)
    ob_b, sb_b = gdn_dir(a_bw, b_bw, 1, s_g[:, 1], True)
    o_b = _rmsnorm(ob_f + ob_b, gn_b) * jax.nn.silu(_heads(g_b, H_B).astype(f32))
    o = jnp.concatenate([_merge(o_a), _merge(o_b)], axis=-1).astype(h.dtype)
    return (o @ w_out,
            jnp.stack([sa_f, sa_b], axis=1).astype(h.dtype),
            jnp.stack([sb_f, sb_b], axis=1).astype(h.dtype))


def _na_qkv(h, w_qkv, qn, kn):
    q, k, v = jnp.split(h @ w_qkv, 3, axis=-1)
    return _rmsnorm(_heads(q, H_C), qn), _rmsnorm(_heads(k, H_C), kn), _heads(v, H_C)


def _ctx_attention(q, k, v):
    b, h, t, d = q.shape
    qb = jnp.moveaxis(q.reshape(b, h, t // Q_BLOCK, Q_BLOCK, d), 2, 0)

    def blk(qi):
        s = jnp.einsum('bhqd,bhkd->bhqk', qi, k).astype(jnp.float32) * HD_C ** -0.5
        p = jax.nn.softmax(s, axis=-1)
        return jnp.einsum('bhqk,bhkd->bhqd', p.astype(v.dtype), v)

    o = lax.map(blk, qb)
    return jnp.moveaxis(o, 0, 2).reshape(b, h, t, d)


def _na_latent(q, k, v, k_ctx, v_ctx, rpb):
    b, h, t, d = q.shape
    rows = t // GRID_W
    kh = min(KH_MAX, rows)
    ncb = GRID_W // QBW
    qg = q.reshape(b, h, rows, GRID_W, d)
    kg = k.reshape(b, h, rows, GRID_W, d)
    vg = v.reshape(b, h, rows, GRID_W, d)
    qc = np.arange(GRID_W).reshape(ncb, QBW)
    kc0 = np.clip(np.arange(ncb) * QBW - KW // 2, 0, GRID_W - KBW)
    kcols = kc0[:, None] + np.arange(KBW)
    cstart = np.clip(qc - KW // 2, 0, GRID_W - KW)
    col_valid = (kcols[:, None, :] >= cstart[..., None]) & (kcols[:, None, :] < cstart[..., None] + KW)
    col_idx = np.clip(kcols[:, None, :] - qc[:, :, None] + KW - 1, 0, 2 * KW - 2)
    rpb_c = rpb.astype(jnp.float32)[:, :, col_idx]
    scale = HD_C ** -0.5

    def row_block(r):
        rs = jnp.clip(r - kh // 2, 0, rows - kh)
        q_r = lax.dynamic_index_in_dim(qg, r, axis=2, keepdims=False).reshape(b, h, ncb, QBW, d)
        k_r = lax.dynamic_slice_in_dim(kg, rs, kh, axis=2)[:, :, :, kcols]
        v_r = lax.dynamic_slice_in_dim(vg, rs, kh, axis=2)[:, :, :, kcols]
        s_win = jnp.einsum('bhjqd,bhajkd->bhjqak', q_r, k_r).astype(jnp.float32) * scale
        dr_idx = rs + jnp.arange(kh) - r + (KH_MAX - 1)
        bias = jnp.transpose(rpb_c[:, dr_idx], (0, 2, 3, 1, 4))
        s_win = jnp.where(col_valid[:, :, None, :], s_win + bias, NEG_INF).reshape(b, h, ncb, QBW, kh * KBW)
        s_ctx = jnp.einsum('bhjqd,bhld->bhjql', q_r, k_ctx).astype(jnp.float32) * scale
        p = jax.nn.softmax(jnp.concatenate([s_win, s_ctx], axis=-1), axis=-1)
        p_win = p[..., :kh * KBW].reshape(b, h, ncb, QBW, kh, KBW).astype(v.dtype)
        p_ctx = p[..., kh * KBW:].astype(v.dtype)
        o = (jnp.einsum('bhjqak,bhajkd->bhjqd', p_win, v_r)
             + jnp.einsum('bhjql,bhld->bhjqd', p_ctx, v_ctx.astype(v.dtype)))
        return o.reshape(b, h, GRID_W, d)

    o = lax.map(row_block, jnp.arange(rows))
    return jnp.moveaxis(o, 0, 2).reshape(b, h, t, d)


def _mlp(h, w1, w2):
    return jnp.square(jax.nn.relu(h @ w1)) @ w2


def setup_inputs(seed: int = 0) -> dict:
    key = jax.random.key(seed)
    ks = jax.random.split(key, 32)
    nrm = lambda k, shape, s: jax.random.normal(k, shape, jnp.float32) * s
    d = D_MODEL
    return {
        'x_prompt': nrm(ks[0], (BATCH, SEQ, d), 1.0),
        'x_sample': nrm(ks[1], (DEC_BATCH, DEC_SEQ, d), 1.0),
        'state_hgrn': nrm(ks[2], (DEC_BATCH, N_EVEN, 2, H_A, DK_A, DV_A), 0.5),
        'state_gdn': nrm(ks[3], (DEC_BATCH, N_EVEN, 2, H_B, DK_B, DV_B), 0.1),
        'cache_na_k': nrm(ks[4], (DEC_BATCH, N_ODD, H_C, PAST_LEN, HD_C), 1.0),
        'cache_na_v': nrm(ks[5], (DEC_BATCH, N_ODD, H_C, PAST_LEN, HD_C), 1.0),
        'c': nrm(ks[6], (DEC_BATCH, d), 1.0),
        'c_ctx': nrm(ks[7], (d,), 1.0),
        'ada_w': nrm(ks[8], (DEPTH, d, 6 * d), 0.5 * d ** -0.5),
        'ada_b': nrm(ks[9], (DEPTH, 6 * d), 0.02),
        'norm_g': 1.0 + nrm(ks[10], (DEPTH, 2, d), 0.01),
        'w_in_ab': nrm(ks[11], (N_EVEN, d, D_IN_AB), d ** -0.5),
        'w_out_ab': nrm(ks[12], (N_EVEN, D_MIX_AB, d), D_MIX_AB ** -0.5),
        'hgrn_lb': nrm(ks[13], (DEPTH + 1, H_A * DK_A), 0.1),
        'gdn_conv': nrm(ks[14], (N_EVEN, SHORT_CONV, 1, C_CONV), SHORT_CONV ** -0.5),
        'gdn_a_log': jnp.log(jax.random.uniform(ks[15], (N_EVEN, 2, H_B), jnp.float32, 1.0, 8.0)),
        'gdn_dt_bias': nrm(ks[16], (N_EVEN, 2, H_B), 0.1),
        'gn_hgrn': 1.0 + nrm(ks[17], (N_EVEN, DV_A), 0.01),
        'gn_gdn': 1.0 + nrm(ks[18], (N_EVEN, DV_B), 0.01),
        'w_qkv_na': nrm(ks[19], (N_ODD, d, 3 * H_C * HD_C), d ** -0.5),
        'qn_na': 1.0 + nrm(ks[20], (N_ODD, HD_C), 0.01),
        'kn_na': 1.0 + nrm(ks[21], (N_ODD, HD_C), 0.01),
        'rpb_na': nrm(ks[22], (N_ODD, H_C, 2 * KH_MAX - 1, 2 * KW - 1), 0.02),
        'w_out_na': nrm(ks[23], (N_ODD, H_C * HD_C, d), (H_C * HD_C) ** -0.5),
        'w_mlp1': nrm(ks[24], (DEPTH, d, D_FF), d ** -0.5),
        'w_mlp2': nrm(ks[25], (DEPTH, D_FF, d), D_FF ** -0.5),
    }


def reference(x_prompt, x_sample, state_hgrn, state_gdn, cache_na_k, cache_na_v, c,
              c_ctx, ada_w, ada_b, norm_g, w_in_ab, w_out_ab, hgrn_lb, gdn_conv,
              gdn_a_log, gdn_dt_bias, gn_hgrn, gn_gdn, w_qkv_na, qn_na, kn_na, rpb_na,
              w_out_na, w_mlp1, w_mlp2):
    xp, xs = x_prompt, x_sample
    bp = xp.shape[0]
    lbs = jnp.cumsum(jax.nn.softmax(hgrn_lb.astype(jnp.float32), axis=0), axis=0)
    new_hgrn, new_gdn, new_k, new_v = [], [], [], []
    for l in range(DEPTH):
        p_sh1, p_sc1, p_g1, p_sh2, p_sc2, p_g2 = _modulation(c_ctx[None, :], ada_w[l], ada_b[l])
        s_sh1, s_sc1, s_g1, s_sh2, s_sc2, s_g2 = _modulation(c, ada_w[l], ada_b[l])
        hp = _rmsnorm(xp, norm_g[l, 0]) * (1.0 + p_sc1) + p_sh1
        hs = _rmsnorm(xs, norm_g[l, 0]) * (1.0 + s_sc1) + s_sh1
        if l % 2 == 0:
            e = l // 2
            w = (w_in_ab[e], w_out_ab[e], lbs[l], gdn_conv[e], gdn_a_log[e], gdn_dt_bias[e], gn_hgrn[e], gn_gdn[e])
            zh = jnp.zeros((bp, 2, H_A, DK_A, DV_A), jnp.float32)
            zg = jnp.zeros((bp, 2, H_B, DK_B, DV_B), jnp.float32)
            mp, sh_new, sg_new = _mixer_ab(hp, *w, zh, zg)
            ms, _, _ = _mixer_ab(hs, *w, state_hgrn[:, e], state_gdn[:, e])
            new_hgrn.append(sh_new)
            new_gdn.append(sg_new)
        else:
            o = l // 2
            qp, kp, vp = _na_qkv(hp, w_qkv_na[o], qn_na[o], kn_na[o])
            mp = _merge(_ctx_attention(qp, kp, vp)) @ w_out_na[o]
            qs, ks_, vs = _na_qkv(hs, w_qkv_na[o], qn_na[o], kn_na[o])
            ms = _merge(_na_latent(qs, ks_, vs, cache_na_k[:, o], cache_na_v[:, o], rpb_na[o])) @ w_out_na[o]
            new_k.append(kp)
            new_v.append(vp)
        xp = xp + p_g1 * mp
        xs = xs + s_g1 * ms
        hp = _rmsnorm(xp, norm_g[l, 1]) * (1.0 + p_sc2) + p_sh2
        hs = _rmsnorm(xs, norm_g[l, 1]) * (1.0 + s_sc2) + s_sh2
        xp = xp + p_g2 * _mlp(hp, w_mlp1[l], w_mlp2[l])
        xs = xs + s_g2 * _mlp(hs, w_mlp1[l], w_mlp2[l])
    return (xp, xs, jnp.stack(new_hgrn, axis=1), jnp.stack(new_gdn, axis=1),
            jnp.stack(new_k, axis=1), jnp.stack(new_v, axis=1))
```

```python
import functools

import numpy as np
import jax
import jax.numpy as jnp
from jax import lax
from jax.experimental import pallas as pl
from jax.experimental.pallas import tpu as pltpu

F32 = jnp.float32
BF16 = jnp.bfloat16

D_MODEL = 1024
BATCH = 16
SEQ = 256
DEC_BATCH = 4
DEC_SEQ = 1024
PAST_LEN = 256
N_PROMPT = BATCH * SEQ
N_SAMPLE = DEC_BATCH * DEC_SEQ
N_TOK = N_PROMPT + N_SAMPLE
GRID_W = 64
GRID_ROWS = DEC_SEQ // GRID_W
H_A = 4
DK_A = 128
DV_A = 128
H_B = 4
DK_B = 128
DV_B = 128
SHORT_CONV = 5
H_C = 16
HD_C = 64
KH = 8
KW = 16
D_FF = 4 * D_MODEL
EPS = 1e-6
NEG_INF = -1e30
N_MOD_ROWS = 8
D_MAIN_AB = 4608
N_GATE_AB = 16
CHUNK = 32
GBLK = 128
VMEM_LIMIT = 56 * 1024 * 1024


def _cparams(*sem):
    return pltpu.CompilerParams(dimension_semantics=sem, vmem_limit_bytes=VMEM_LIMIT)


def _sigmoid(x):
    return 1.0 / (1.0 + jnp.exp(-x))


def _silu(x):
    return x * _sigmoid(x)


def _dot(a, b):
    return jnp.dot(a, b, preferred_element_type=F32)


def _dot_nt(a, b):
    return lax.dot_general(a, b, (((1,), (1,)), ((), ())), preferred_element_type=F32)


def _dot_tn(a, b):
    return lax.dot_general(a, b, (((0,), (0,)), ((), ())), preferred_element_type=F32)


def _split2(x):
    hi = x.astype(BF16)
    lo = (x - hi.astype(F32)).astype(BF16)
    return hi, lo


def _dot_const(m2, x):
    hi, lo = _split2(x)
    return _dot(m2, jnp.concatenate([hi, lo], axis=0))


def _dot3(a, b):
    ah, al = _split2(a)
    bh, bl = _split2(b)
    return _dot(ah, bh) + (_dot(ah, bl) + _dot(al, bh))


def _mod_row(i, tm):
    start = i * tm
    return jnp.where(start < N_PROMPT, 0, 1 + (start - N_PROMPT) // DEC_SEQ)


def _mod_slice(mod_ref, row, k):
    return mod_ref[pl.ds(row, 1), k * D_MODEL:(k + 1) * D_MODEL]


def _norm_mod(x, g, sc, sh):
    ms = jnp.mean(x * x, axis=-1, keepdims=True)
    return (x * lax.rsqrt(ms + EPS) * g) * (1.0 + sc) + sh


def _mod_kernel(cond_ref, w_ref, b_ref, o_ref):
    s = _silu(cond_ref[...]).astype(BF16)
    o_ref[0] = _dot(s, w_ref[0].astype(BF16)) + b_ref[0]


def _modulation(cond8, ada_w, ada_b):
    depth = ada_w.shape[0]
    tn = 1024
    nj = ada_w.shape[2] // tn
    return pl.pallas_call(
        _mod_kernel,
        grid=(depth, nj),
        in_specs=[
            pl.BlockSpec((N_MOD_ROWS, D_MODEL), lambda l, j: (0, 0)),
            pl.BlockSpec((1, D_MODEL, tn), lambda l, j: (l, 0, j)),
            pl.BlockSpec((1, 1, tn), lambda l, j: (l, 0, j)),
        ],
        out_specs=pl.BlockSpec((1, N_MOD_ROWS, tn), lambda l, j: (l, 0, j)),
        out_shape=jax.ShapeDtypeStruct((depth, N_MOD_ROWS, ada_w.shape[2]), F32),
        compiler_params=_cparams("arbitrary", "arbitrary"),
        name="modulation",
    )(cond8, ada_w, ada_b.reshape(depth, 1, -1))


def _norm_proj_kernel(x_ref, mod_ref, g_ref, *refs, tm, n_w):
    w_refs, o_refs = refs[:n_w], refs[n_w:]
    row = _mod_row(pl.program_id(0), tm)
    h = _norm_mod(x_ref[...], g_ref[...], _mod_slice(mod_ref, row, 1), _mod_slice(mod_ref, row, 0)).astype(BF16)
    for w_ref, o_ref in zip(w_refs, o_refs):
        o_ref[...] = _dot(h, w_ref[...])


def _norm_proj(x, mod, g, ws, tm=256):
    n_w = len(ws)
    const = lambda i: (0, 0)
    return pl.pallas_call(
        functools.partial(_norm_proj_kernel, tm=tm, n_w=n_w),
        grid=(N_TOK // tm,),
        in_specs=[
            pl.BlockSpec((tm, D_MODEL), lambda i: (i, 0)),
            pl.BlockSpec(mod.shape, const),
            pl.BlockSpec((1, D_MODEL), const),
        ] + [pl.BlockSpec(w.shape, const, pipeline_mode=pl.Buffered(1)) for w in ws],
        out_specs=[pl.BlockSpec((tm, w.shape[1]), lambda i: (i, 0)) for w in ws],
        out_shape=[jax.ShapeDtypeStruct((N_TOK, w.shape[1]), F32) for w in ws],
        compiler_params=_cparams("arbitrary"),
        name="norm_proj",
    )(x, mod, g.reshape(1, D_MODEL), *ws)


def _post_kernel(x_ref, m_ref, mod_ref, g_ref, wo_ref, w1_ref, w2_ref, y_ref, *, tm, ff_chunk):
    row = _mod_row(pl.program_id(0), tm)
    mix = _dot(m_ref[...].astype(BF16), wo_ref[...])
    x1 = x_ref[...] + _mod_slice(mod_ref, row, 2) * mix
    h = _norm_mod(x1, g_ref[...], _mod_slice(mod_ref, row, 4), _mod_slice(mod_ref, row, 3)).astype(BF16)
    acc = jnp.zeros((tm, D_MODEL), F32)
    for k in range(0, D_FF, ff_chunk):
        a = jnp.maximum(_dot(h, w1_ref[:, k:k + ff_chunk]), 0.0)
        acc = acc + _dot((a * a).astype(BF16), w2_ref[k:k + ff_chunk, :])
    y_ref[...] = x1 + _mod_slice(mod_ref, row, 5) * acc


def _post_mixer(x, mixed, mod, g, wo, w1, w2, tm=256, ff_chunk=1024):
    const = lambda i: (0, 0)
    row_spec = pl.BlockSpec((tm, D_MODEL), lambda i: (i, 0))
    return pl.pallas_call(
        functools.partial(_post_kernel, tm=tm, ff_chunk=ff_chunk),
        grid=(N_TOK // tm,),
        in_specs=[
            row_spec,
            row_spec,
            pl.BlockSpec(mod.shape, const),
            pl.BlockSpec((1, D_MODEL), const),
            pl.BlockSpec(wo.shape, const, pipeline_mode=pl.Buffered(1)),
            pl.BlockSpec(w1.shape, const, pipeline_mode=pl.Buffered(1)),
            pl.BlockSpec(w2.shape, const, pipeline_mode=pl.Buffered(1)),
        ],
        out_specs=row_spec,
        out_shape=jax.ShapeDtypeStruct((N_TOK, D_MODEL), F32),
        compiler_params=_cparams("arbitrary"),
        name="post_mixer",
    )(x, mixed, mod, g.reshape(1, D_MODEL), wo, w1, w2)


def _head_norm(x, w):
    ms = jnp.mean(x * x, axis=-1, keepdims=True)
    return x * lax.rsqrt(ms + EPS) * w


def _softmax_pv(scores, values):
    m = functools.reduce(jnp.maximum, [jnp.max(s, axis=-1, keepdims=True) for s in scores])
    ps = [jnp.exp(s - m) for s in scores]
    l = functools.reduce(lambda a, b: a + b, [jnp.sum(p, axis=-1, keepdims=True) for p in ps])
    o = functools.reduce(lambda a, b: a + b, [_dot(p.astype(BF16), v) for p, v in zip(ps, values)])
    return o / l


def _ctx_attn_kernel(q_ref, k_ref, v_ref, qn_ref, kn_ref, o_ref, kc_ref, vc_ref):
    scale = HD_C ** -0.5
    outs = []
    for j in range(2):
        sl = slice(j * HD_C, (j + 1) * HD_C)
        q = _head_norm(q_ref[:, sl], qn_ref[...])
        k = _head_norm(k_ref[:, sl], kn_ref[...])
        v = v_ref[:, sl]
        kc_ref[0, 0, j] = k
        vc_ref[0, 0, j] = v
        s = _dot_nt(q.astype(BF16), k.astype(BF16)) * scale
        outs.append(_softmax_pv([s], [v.astype(BF16)]))
    o_ref[...] = jnp.concatenate(outs, axis=-1)


def _ctx_attention(qkv, qn, kn):
    nhp = H_C // 2
    blk = lambda off: pl.BlockSpec((SEQ, 2 * HD_C), lambda b, p: (b, off + p))
    cache_spec = pl.BlockSpec((1, 1, 2, SEQ, HD_C), lambda b, p: (b, 0, p, 0, 0))
    cache_shape = jax.ShapeDtypeStruct((BATCH, 1, H_C, SEQ, HD_C), F32)
    return pl.pallas_call(
        _ctx_attn_kernel,
        grid=(BATCH, nhp),
        in_specs=[blk(0), blk(nhp), blk(2 * nhp),
                  pl.BlockSpec((1, HD_C), lambda b, p: (0, 0)),
                  pl.BlockSpec((1, HD_C), lambda b, p: (0, 0))],
        out_specs=[pl.BlockSpec((SEQ, 2 * HD_C), lambda b, p: (b, p)), cache_spec, cache_spec],
        out_shape=[jax.ShapeDtypeStruct((N_TOK, D_MODEL), F32), cache_shape, cache_shape],
        compiler_params=_cparams("arbitrary", "arbitrary"),
        name="ctx_attention",
    )(qkv, qkv, qkv, qn.reshape(1, HD_C), kn.reshape(1, HD_C))


def _na_row_start(r):
    return min(max(r - KH // 2, 0), GRID_ROWS - KH)


def _na_attn_kernel(q_ref, k_ref, v_ref, kc_ref, vc_ref, qn_ref, kn_ref, bias_ref, mixed_in_ref, o_ref, qs, ks, vs):
    del mixed_in_ref
    scale = HD_C ** -0.5
    for j in range(2):
        sl = slice(j * HD_C, (j + 1) * HD_C)
        qs[j] = _head_norm(q_ref[:, sl], qn_ref[...]).astype(BF16)
        ks[j] = _head_norm(k_ref[:, sl], kn_ref[...]).astype(BF16)
        vs[j] = v_ref[:, sl].astype(BF16)
        k_ctx = kc_ref[0, 0, j].astype(BF16)
        v_ctx = vc_ref[0, 0, j].astype(BF16)
        for r in range(GRID_ROWS):
            rs = _na_row_start(r)
            q_r = qs[j, r * GRID_W:(r + 1) * GRID_W, :]
            win = slice(rs * GRID_W, (rs + KH) * GRID_W)
            s_win = _dot_nt(q_r, ks[j, win, :]) * scale + bias_ref[j, r - rs]
            s_ctx = _dot_nt(q_r, k_ctx) * scale
            o_ref[r * GRID_W:(r + 1) * GRID_W, sl] = _softmax_pv([s_win, s_ctx], [vs[j, win, :], v_ctx])


def _na_bias_table(rpb):
    off = np.arange(KH)[:, None]
    a = np.arange(KH)[None, :]
    a_idx = a - off + (KH - 1)
    qc = np.arange(GRID_W)[:, None]
    kc = np.arange(GRID_W)[None, :]
    cstart = np.clip(qc - KW // 2, 0, GRID_W - KW)
    valid = (kc >= cstart) & (kc < cstart + KW)
    c_idx = np.clip(kc - qc + KW - 1, 0, 2 * KW - 2)
    t = rpb.astype(F32)[:, a_idx[:, :, None, None], c_idx[None, None, :, :]]
    t = jnp.where(valid[None, None, None], t, NEG_INF)
    return jnp.transpose(t, (0, 1, 3, 2, 4)).reshape(H_C, KH, GRID_W, KH * GRID_W)


def _na_attention(qkv, cache_k, cache_v, qn, kn, bias, mixed):
    nhp = H_C // 2
    row0 = N_PROMPT // DEC_SEQ
    blk = lambda off: pl.BlockSpec((DEC_SEQ, 2 * HD_C), lambda p, b: (row0 + b, off + p))
    cache_spec = pl.BlockSpec((1, 1, 2, PAST_LEN, HD_C), lambda p, b: (b, 0, p, 0, 0))
    return pl.pallas_call(
        _na_attn_kernel,
        grid=(nhp, DEC_BATCH),
        in_specs=[blk(0), blk(nhp), blk(2 * nhp), cache_spec, cache_spec,
                  pl.BlockSpec((1, HD_C), lambda p, b: (0, 0)),
                  pl.BlockSpec((1, HD_C), lambda p, b: (0, 0)),
                  pl.BlockSpec((2, KH, GRID_W, KH * GRID_W), lambda p, b: (p, 0, 0, 0)),
                  pl.BlockSpec(memory_space=pl.ANY)],
        out_specs=pl.BlockSpec((DEC_SEQ, 2 * HD_C), lambda p, b: (row0 + b, p)),
        out_shape=jax.ShapeDtypeStruct((N_TOK, D_MODEL), F32),
        input_output_aliases={8: 0},
        scratch_shapes=[pltpu.VMEM((2, DEC_SEQ, HD_C), BF16)] * 3,
        compiler_params=_cparams("arbitrary", "arbitrary"),
        name="na_attention",
    )(qkv, qkv, qkv, cache_k, cache_v, qn.reshape(1, HD_C), kn.reshape(1, HD_C), bias, mixed)


def _seq_layout(prompt):
    return (SEQ, BATCH, 0) if prompt else (DEC_SEQ, DEC_BATCH, N_PROMPT // DEC_SEQ)


def _flip_blocks(m, c):
    r, s = m.shape
    return m.reshape(r // c, c, s // c, c)[:, ::-1, :, ::-1].reshape(r, s)


def _rms_gate(x, gn, gate):
    ms = jnp.mean(x * x, axis=-1, keepdims=True)
    return x * lax.rsqrt(ms + EPS) * gn * _silu(gate)


HG_LEVELS = tuple(CHUNK >> (i + 1) for i in range(CHUNK.bit_length() - 1))
HG_NL = len(HG_LEVELS)
HG_STACK = (HG_NL + 1) * CHUNK
TOT_ROWS = 16
HG_ROWS = (HG_NL + 2) * CHUNK + TOT_ROWS


def _hgrn_consts():
    c = CHUNK
    level_rows = []
    mask = np.zeros((HG_STACK, HG_STACK), np.float32)
    mask[:c, :c] = np.eye(c)
    for li, b in enumerate(HG_LEVELS):
        m = np.zeros((c, c), np.float32)
        blk = np.zeros((c, c), np.float32)
        for t in range(c):
            mid = (t // (2 * b)) * 2 * b + b
            if t >= mid:
                m[t, mid:t + 1] = 1.0
                blk[t, mid - b:mid] = 1.0
            else:
                m[t, t + 1:mid] = 1.0
        level_rows.append(m)
        mask[(li + 1) * c:(li + 2) * c, (li + 1) * c:(li + 2) * c] = blk
    dq = np.tril(np.ones((c, c), np.float32))
    dk = np.triu(np.ones((c, c), np.float32), 1)
    body = np.concatenate(level_rows + [dq, dk], axis=0)
    tot = np.ones((TOT_ROWS, c), np.float32)
    mcs, masks = [], []
    for reverse in (False, True):
        bm = _flip_blocks(body, c) if reverse else body
        mk = _flip_blocks(mask, c) if reverse else mask
        mc = np.concatenate([bm, tot], axis=0)
        mcs.append(np.concatenate([mc, mc], axis=1))
        masks.append(mk)
    return jnp.asarray(np.stack(mcs), BF16), jnp.asarray(np.stack(masks), F32)


def _hgrn_kernel(*refs, seq, has_s0, emit_state, aliased):
    it = iter(refs)
    qa_ref, ff_ref, fb_ref, ia_ref, ga_ref, lb_ref, gn_ref, mc_ref, mask_ref = [next(it) for _ in range(9)]
    s0_ref = next(it) if has_s0 else None
    if aliased:
        next(it)
    o_ref = next(it)
    st_ref = next(it) if emit_state else None
    s_scr, acc = next(it), next(it)
    c = CHUNK
    n_chunks = seq // c

    lb_raw = lb_ref[...]
    lb_e = jnp.exp(lb_raw - jnp.max(lb_raw, axis=0, keepdims=True))
    lb_all = lb_e[0:1] / jnp.sum(lb_e, axis=0, keepdims=True)

    for d in range(2):
        for h in range(H_A):
            s_scr[d, h] = s0_ref[0, 0, d, h].T if has_s0 else jnp.zeros((DV_A, DK_A), F32)
    acc[...] = jnp.zeros(acc.shape, F32)

    def body(n, carry):
        for d in range(2):
            cn = n if d == 0 else n_chunks - 1 - n
            rows = pl.ds(pl.multiple_of(cn * c, c), c)
            fr_ref = ff_ref if d == 0 else fb_ref
            mc2 = mc_ref[d]
            mask = mask_ref[d]
            for h in range(H_A):
                ln = slice(h * DK_A, (h + 1) * DK_A)
                lb = lb_all[:, ln]
                f = lb + (1.0 - lb) * _sigmoid(fr_ref[rows, ln])
                k = 1.0 - f
                q = _silu(qa_ref[rows, ln]) * DK_A ** -0.5
                v = ia_ref[rows, ln].astype(BF16)
                e = jnp.exp(_dot_const(mc2, jnp.log(f)))
                lvl = [e[i * c:(i + 1) * c] for i in range(HG_NL + 2)]
                qst = jnp.concatenate([q] + [q * lvl[i] for i in range(HG_NL)], axis=0).astype(BF16)
                kst = jnp.concatenate([k] + [k * lvl[i] for i in range(HG_NL)], axis=0).astype(BF16)
                r = (_dot_nt(qst, kst) * mask).astype(BF16)
                ost = _dot(r, jnp.concatenate([v] * (HG_NL + 1), axis=0))
                o = functools.reduce(lambda a, b: a + b, [ost[i * c:(i + 1) * c] for i in range(HG_NL + 1)])
                st = s_scr[d, h]
                o = o + _dot_nt((q * lvl[HG_NL]).astype(BF16), st.astype(BF16))
                acc[rows, ln] += o
                e_tot = e[(HG_NL + 2) * c:(HG_NL + 2) * c + 1]
                s_scr[d, h] = st * e_tot + _dot_tn(v, (k * lvl[HG_NL + 1]).astype(BF16))
        return carry

    lax.fori_loop(0, n_chunks, body, 0)

    for h in range(H_A):
        ln = slice(h * DV_A, (h + 1) * DV_A)
        o_ref[:, ln] = _rms_gate(acc[:, ln], gn_ref[...], ga_ref[:, ln])
    if emit_state:
        for d in range(2):
            for h in range(H_A):
                st_ref[0, 0, d, h] = s_scr[d, h].T


def _hgrn(proj, hgrn_lb, gn, consts, prompt, s0=None, layer=0, mixed=None):
    seq, nb, rb0 = _seq_layout(prompt)
    mc2, mask = consts
    wa = H_A * DK_A
    blk = lambda j: pl.BlockSpec((seq, wa), lambda b: (rb0 + b, j))
    const2 = lambda b: (0, 0)
    const3 = lambda b: (0, 0, 0)
    st_block = (1, 1, 2, H_A, DK_A, DV_A)
    in_specs = [blk(0), blk(1), blk(2), blk(3), blk(4),
                pl.BlockSpec(hgrn_lb.shape, const2), pl.BlockSpec((1, DV_A), const2),
                pl.BlockSpec(mc2.shape, const3), pl.BlockSpec(mask.shape, const3)]
    args = [proj] * 5 + [hgrn_lb, gn.reshape(1, DV_A), mc2, mask]
    if s0 is not None:
        in_specs.append(pl.BlockSpec(st_block, lambda b: (b, layer, 0, 0, 0, 0)))
        args.append(s0)
    aliases = {}
    if mixed is not None:
        aliases = {len(args): 0}
        in_specs.append(pl.BlockSpec(memory_space=pl.ANY))
        args.append(mixed)
    out_specs = [pl.BlockSpec((seq, wa), lambda b: (rb0 + b, 0))]
    out_shape = [jax.ShapeDtypeStruct((N_TOK, D_MODEL), F32)]
    if prompt:
        out_specs.append(pl.BlockSpec(st_block, lambda b: (b, 0, 0, 0, 0, 0)))
        out_shape.append(jax.ShapeDtypeStruct((nb, 1, 2, H_A, DK_A, DV_A), F32))
    return pl.pallas_call(
        functools.partial(_hgrn_kernel, seq=seq, has_s0=s0 is not None, emit_state=prompt, aliased=mixed is not None),
        grid=(nb,),
        in_specs=in_specs,
        out_specs=out_specs,
        out_shape=out_shape,
        input_output_aliases=aliases,
        scratch_shapes=[pltpu.VMEM((2, H_A, DV_A, DK_A), F32), pltpu.VMEM((seq, wa), F32)],
        compiler_params=_cparams("arbitrary"),
        name="hgrn_prompt" if prompt else "hgrn_sample",
    )(*args)


GD_SUB = GBLK // CHUNK
GD_ROWS = 2 * GBLK + TOT_ROWS


def _gdn_consts():
    n, c = GBLK, CHUNK
    same = (np.arange(n)[:, None] // c) == (np.arange(n)[None, :] // c)
    tri = (same & (np.arange(n)[None, :] <= np.arange(n)[:, None])).astype(np.float32)
    sup = (same & (np.arange(n)[None, :] > np.arange(n)[:, None])).astype(np.float32)
    tot = np.zeros((TOT_ROWS, n), np.float32)
    for s in range(GD_SUB):
        tot[s, s * c:(s + 1) * c] = 1.0
    mgs, mds, tris, trits = [], [], [], []
    for reverse in (False, True):
        t = _flip_blocks(tri, c) if reverse else tri
        s = _flip_blocks(sup, c) if reverse else sup
        mg = np.concatenate([t, s, tot], axis=0)
        md = np.concatenate([t, -same.astype(np.float32)], axis=1)
        mgs.append(np.concatenate([mg, mg], axis=1))
        mds.append(np.concatenate([md, md], axis=1))
        tris.append(t)
        trits.append(t.T)
    return (jnp.asarray(np.stack(mgs), BF16), jnp.asarray(np.stack(mds), BF16),
            jnp.asarray(np.stack(tris), F32), jnp.asarray(np.stack(trits), F32))


def _softplus(x):
    return jnp.maximum(x, 0.0) + jnp.log(1.0 + jnp.exp(-jnp.abs(x)))


def _conv_silu(x, w, seq):
    t_idx = lax.broadcasted_iota(jnp.int32, (seq, 1), 0)
    half = SHORT_CONV // 2
    acc = x * w[half:half + 1]
    for j in range(SHORT_CONV):
        shift = half - j
        if shift == 0:
            continue
        src = t_idx - shift
        xr = pltpu.roll(x, shift % seq, axis=0)
        acc = acc + jnp.where((src >= 0) & (src < seq), xr, 0.0) * w[j:j + 1]
    return _silu(acc)


def _l2norm_heads(x, n_heads, width, scale):
    outs = []
    for h in range(n_heads):
        xh = x[:, h * width:(h + 1) * width]
        outs.append(xh * (lax.rsqrt(jnp.sum(xh * xh, axis=-1, keepdims=True) + EPS) * scale))
    return jnp.concatenate(outs, axis=-1)


def _gdn_kernel(*refs, seq, has_s0, emit_state, aliased):
    it = iter(refs)
    (q_ref, k_ref, v_ref, gb_ref, gate_ref, cw_ref, alog_ref, dt_ref, gn_ref,
     mg_ref, md_ref, tri_ref, trit_ref) = [next(it) for _ in range(13)]
    s0_ref = next(it) if has_s0 else None
    if aliased:
        next(it)
    o_ref = next(it)
    st_ref = next(it) if emit_state else None
    qn, kn, vn, u_s, w_s, qg_s, kd_s, at_s, et_s, s_scr, acc = [next(it) for _ in range(11)]
    c = CHUNK
    n_chunks = seq // c
    n_blocks = seq // GBLK
    wq = H_B * DK_B

    qn[...] = _l2norm_heads(_conv_silu(q_ref[...], cw_ref[:, 0:wq], seq), H_B, DK_B, DK_B ** -0.5)
    kn[...] = _l2norm_heads(_conv_silu(k_ref[...], cw_ref[:, wq:2 * wq], seq), H_B, DK_B, 1.0)
    vn[...] = _conv_silu(v_ref[...], cw_ref[:, 2 * wq:3 * wq], seq)
    for i in range(2 * H_B):
        s_scr[i] = s0_ref[0, 0, i // H_B, i % H_B] if has_s0 else jnp.zeros((DK_B, DV_B), F32)
    acc[...] = jnp.zeros(acc.shape, F32)

    eye = (lax.broadcasted_iota(jnp.int32, (GBLK, GBLK), 0)
           == lax.broadcasted_iota(jnp.int32, (GBLK, GBLK), 1)).astype(F32)

    def block_body(blk, carry):
        rows = pl.ds(pl.multiple_of(blk * GBLK, GBLK), GBLK)
        gates = gate_ref[rows, :]
        glog_all = -jnp.exp(alog_ref[...]) * _softplus(gates + dt_ref[...])
        beta_all = _sigmoid(gates)
        for d in range(2):
            tri = tri_ref[d]
            inside = tri > 0.0
            for h in range(H_B):
                i = d * H_B + h
                ln = slice(h * DK_B, (h + 1) * DK_B)
                q, k, v = qn[rows, ln], kn[rows, ln], vn[rows, ln]
                glog = jnp.broadcast_to(glog_all[:, i:i + 1], (GBLK, GBLK))
                beta = jnp.broadcast_to(beta_all[:, 2 * H_B + i:2 * H_B + i + 1], (GBLK, DK_B))
                eg = jnp.exp(_dot_const(mg_ref[d], glog))
                gd = _dot_const(md_ref[d], jnp.concatenate([glog, glog * trit_ref[d]], axis=0))
                decay = jnp.where(inside, jnp.exp(jnp.where(inside, gd, 0.0)), 0.0)
                kb = k * beta
                kk = _dot_nt(jnp.concatenate([q, kb], axis=0).astype(BF16), k.astype(BF16))
                attn = kk[:GBLK] * decay
                a = kk[GBLK:] * decay * (1.0 - eye)
                x = eye - a
                p = a
                for _ in range(CHUNK.bit_length() - 2):
                    p = _dot3(p, p)
                    x = x + _dot3(x, p)
                uw = _dot3(x, jnp.concatenate([v * beta, kb * eg[:GBLK]], axis=1))
                qg = q * eg[:GBLK]
                kd = k * eg[GBLK:2 * GBLK]
                for s in range(GD_SUB):
                    cn = blk * GD_SUB + s
                    r = slice(s * c, (s + 1) * c)
                    u_s[i, cn] = uw[r, :DV_B]
                    w_s[i, cn] = uw[r, DV_B:].astype(BF16)
                    qg_s[i, cn] = qg[r].astype(BF16)
                    kd_s[i, cn] = kd[r].astype(BF16)
                    at_s[i, cn] = attn[r, r].astype(BF16)
                    et_s[i, cn] = jnp.broadcast_to(eg[2 * GBLK + s:2 * GBLK + s + 1], (8, DV_B))
        return carry

    lax.fori_loop(0, n_blocks, block_body, 0)

    def chunk_body(n, carry):
        for d in range(2):
            cn = n if d == 0 else n_chunks - 1 - n
            rows = pl.ds(pl.multiple_of(cn * c, c), c)
            for h in range(H_B):
                i = d * H_B + h
                ln = slice(h * DV_B, (h + 1) * DV_B)
                st = s_scr[i]
                ws = _dot(jnp.concatenate([w_s[i, cn], qg_s[i, cn]], axis=0), st.astype(BF16))
                v_new = (u_s[i, cn] - ws[:c]).astype(BF16)
                acc[rows, ln] += ws[c:] + _dot(at_s[i, cn], v_new)
                s_scr[i] = st * et_s[i, cn][0:1] + _dot_tn(kd_s[i, cn], v_new)
        return carry

    lax.fori_loop(0, n_chunks, chunk_body, 0)

    for h in range(H_B):
        ln = slice(h * DV_B, (h + 1) * DV_B)
        o_ref[:, ln] = _rms_gate(acc[:, ln], gn_ref[...], gb_ref[:, ln])
    if emit_state:
        for i in range(2 * H_B):
            st_ref[0, 0, i // H_B, i % H_B] = s_scr[i]


def _gdn(proj, gates, conv_w, a_log, dt_bias, gn, consts, prompt, mixed, s0=None, layer=0):
    seq, nb, rb0 = _seq_layout(prompt)
    n_chunks = seq // CHUNK
    wq = H_B * DK_B
    blk = lambda j: pl.BlockSpec((seq, wq), lambda b: (rb0 + b, j))
    const2 = lambda b: (0, 0)
    const3 = lambda b: (0, 0, 0)
    st_block = (1, 1, 2, H_B, DK_B, DV_B)
    pad_row = lambda p: jnp.pad(p.reshape(1, -1).astype(F32), ((0, 0), (0, 128 - p.size)))
    in_specs = [blk(5), blk(6), blk(7), blk(8),
                pl.BlockSpec((seq, 128), lambda b: (rb0 + b, 0)),
                pl.BlockSpec((SHORT_CONV, 3 * wq), const2),
                pl.BlockSpec((1, 128), const2), pl.BlockSpec((1, 128), const2), pl.BlockSpec((1, DV_B), const2)]
    in_specs += [pl.BlockSpec(m.shape, const3) for m in consts]
    args = [proj] * 4 + [gates, conv_w.reshape(SHORT_CONV, 3 * wq), pad_row(a_log), pad_row(dt_bias),
                         gn.reshape(1, DV_B)] + list(consts)
    if s0 is not None:
        in_specs.append(pl.BlockSpec(st_block, lambda b: (b, layer, 0, 0, 0, 0)))
        args.append(s0)
    aliases = {len(args): 0}
    in_specs.append(pl.BlockSpec(memory_space=pl.ANY))
    args.append(mixed)
    out_specs = [pl.BlockSpec((seq, wq), lambda b: (rb0 + b, 1))]
    out_shape = [jax.ShapeDtypeStruct((N_TOK, D_MODEL), F32)]
    if prompt:
        out_specs.append(pl.BlockSpec(st_block, lambda b: (b, 0, 0, 0, 0, 0)))
        out_shape.append(jax.ShapeDtypeStruct((nb, 1, 2, H_B, DK_B, DV_B), F32))
    n_dh = 2 * H_B
    scratch = ([pltpu.VMEM((seq, wq), F32)] * 3
               + [pltpu.VMEM((n_dh, n_chunks, CHUNK, DV_B), F32)]
               + [pltpu.VMEM((n_dh, n_chunks, CHUNK, DK_B), BF16)] * 3
               + [pltpu.VMEM((n_dh, n_chunks, CHUNK, CHUNK), BF16),
                  pltpu.VMEM((n_dh, n_chunks, 8, DV_B), F32),
                  pltpu.VMEM((n_dh, DK_B, DV_B), F32),
                  pltpu.VMEM((seq, wq), F32)])
    return pl.pallas_call(
        functools.partial(_gdn_kernel, seq=seq, has_s0=s0 is not None, emit_state=prompt, aliased=True),
        grid=(nb,),
        in_specs=in_specs,
        out_specs=out_specs,
        out_shape=out_shape,
        input_output_aliases=aliases,
        scratch_shapes=scratch,
        compiler_params=_cparams("arbitrary"),
        name="gdn_prompt" if prompt else "gdn_sample",
    )(*args)


def kernel(x_prompt, x_sample, state_hgrn, state_gdn, cache_na_k, cache_na_v, c, c_ctx, ada_w, ada_b, norm_g, w_in_ab, w_out_ab, hgrn_lb, gdn_conv, gdn_a_log, gdn_dt_bias, gn_hgrn, gn_gdn, w_qkv_na, qn_na, kn_na, rpb_na, w_out_na, w_mlp1, w_mlp2):
    cond = jnp.concatenate([c_ctx[None, :], c, jnp.zeros((N_MOD_ROWS - 1 - DEC_BATCH, D_MODEL), F32)], axis=0)
    mods = _modulation(cond, ada_w, ada_b)
    x = jnp.concatenate([x_prompt.reshape(N_PROMPT, D_MODEL), x_sample.reshape(N_SAMPLE, D_MODEL)], axis=0)

    w_in = w_in_ab[0]
    w_main = w_in[:, :D_MAIN_AB].astype(BF16)
    w_gate = jnp.pad(w_in[:, D_MAIN_AB:], ((0, 0), (0, 128 - N_GATE_AB))).astype(BF16)
    proj, gates = _norm_proj(x, mods[0], norm_g[0, 0], [w_main, w_gate])
    hg_consts = _hgrn_consts()
    gd_consts = _gdn_consts()
    mixed, new_hgrn = _hgrn(proj, hgrn_lb, gn_hgrn[0], hg_consts, True)
    mixed, = _hgrn(proj, hgrn_lb, gn_hgrn[0], hg_consts, False, s0=state_hgrn, mixed=mixed)
    gd_args = (gdn_conv[0], gdn_a_log[0], gdn_dt_bias[0], gn_gdn[0], gd_consts)
    mixed, new_gdn = _gdn(proj, gates, *gd_args, True, mixed)
    mixed, = _gdn(proj, gates, *gd_args, False, mixed, s0=state_gdn)
    x = _post_mixer(x, mixed, mods[0], norm_g[0, 1], w_out_ab[0].astype(BF16),
                    w_mlp1[0].astype(BF16), w_mlp2[0].astype(BF16))

    qkv, = _norm_proj(x, mods[1], norm_g[1, 0], [w_qkv_na[0].astype(BF16)])
    mixed, new_k, new_v = _ctx_attention(qkv, qn_na[0], kn_na[0])
    mixed = _na_attention(qkv, cache_na_k, cache_na_v, qn_na[0], kn_na[0], _na_bias_table(rpb_na[0]), mixed)
    x = _post_mixer(x, mixed, mods[1], norm_g[1, 1], w_out_na[0].astype(BF16),
                    w_mlp1[1].astype(BF16), w_mlp2[1].astype(BF16))

    return (x[:N_PROMPT].reshape(BATCH, SEQ, D_MODEL), x[N_PROMPT:].reshape(DEC_BATCH, DEC_SEQ, D_MODEL),
            new_hgrn, new_gdn, new_k, new_v)
```

```python
import functools

import numpy as np
import jax
import jax.numpy as jnp
from jax import lax
from jax.experimental import pallas as pl
from jax.experimental.pallas import tpu as pltpu

F32 = jnp.float32
BF16 = jnp.bfloat16

D_MODEL = 1024
BATCH = 16
SEQ = 256
DEC_BATCH = 4
DEC_SEQ = 1024
PAST_LEN = 256
N_PROMPT = BATCH * SEQ
N_SAMPLE = DEC_BATCH * DEC_SEQ
N_TOK = N_PROMPT + N_SAMPLE
GRID_W = 64
GRID_ROWS = DEC_SEQ // GRID_W
H_A = 4
DK_A = 128
DV_A = 128
H_B = 4
DK_B = 128
DV_B = 128
SHORT_CONV = 5
H_C = 16
HD_C = 64
KH = 8
KW = 16
D_FF = 4 * D_MODEL
EPS = 1e-6
NEG_INF = -1e30
N_MOD_ROWS = 8
D_MAIN_AB = 4608
N_GATE_AB = 16
CHUNK = 32
GBLK = 128
VMEM_LIMIT = 56 * 1024 * 1024


def _cparams(*sem):
    return pltpu.CompilerParams(dimension_semantics=sem, vmem_limit_bytes=VMEM_LIMIT)


def _sigmoid(x):
    return 1.0 / (1.0 + jnp.exp(-x))


def _silu(x):
    return x * _sigmoid(x)


def _dot(a, b):
    return jnp.dot(a, b, preferred_element_type=F32)


def _dot_nt(a, b):
    return lax.dot_general(a, b, (((1,), (1,)), ((), ())), preferred_element_type=F32)


def _dot_tn(a, b):
    return lax.dot_general(a, b, (((0,), (0,)), ((), ())), preferred_element_type=F32)


def _split2(x):
    hi = x.astype(BF16)
    lo = (x - hi.astype(F32)).astype(BF16)
    return hi, lo


def _dot_const(m2, x):
    hi, lo = _split2(x)
    return _dot(m2, jnp.concatenate([hi, lo], axis=0))


def _dot3(a, b):
    ah, al = _split2(a)
    bh, bl = _split2(b)
    return _dot(ah, bh) + (_dot(ah, bl) + _dot(al, bh))


def _mod_row(i, tm):
    start = i * tm
    return jnp.where(start < N_PROMPT, 0, 1 + (start - N_PROMPT) // DEC_SEQ)


def _mod_slice(mod_ref, row, k):
    return mod_ref[pl.ds(row, 1), k * D_MODEL:(k + 1) * D_MODEL]


def _norm_mod(x, g, sc, sh):
    ms = jnp.mean(x * x, axis=-1, keepdims=True)
    return (x * lax.rsqrt(ms + EPS) * g) * (1.0 + sc) + sh


def _mod_kernel(cond_ref, w_ref, b_ref, o_ref):
    s = _silu(cond_ref[...]).astype(BF16)
    o_ref[0] = _dot(s, w_ref[0].astype(BF16)) + b_ref[0]


def _modulation(cond8, ada_w, ada_b):
    depth = ada_w.shape[0]
    tn = 1024
    nj = ada_w.shape[2] // tn
    return pl.pallas_call(
        _mod_kernel,
        grid=(depth, nj),
        in_specs=[
            pl.BlockSpec((N_MOD_ROWS, D_MODEL), lambda l, j: (0, 0)),
            pl.BlockSpec((1, D_MODEL, tn), lambda l, j: (l, 0, j)),
            pl.BlockSpec((1, 1, tn), lambda l, j: (l, 0, j)),
        ],
        out_specs=pl.BlockSpec((1, N_MOD_ROWS, tn), lambda l, j: (l, 0, j)),
        out_shape=jax.ShapeDtypeStruct((depth, N_MOD_ROWS, ada_w.shape[2]), F32),
        compiler_params=_cparams("arbitrary", "arbitrary"),
        name="modulation",
    )(cond8, ada_w, ada_b.reshape(depth, 1, -1))


def _norm_proj_kernel(x_ref, mod_ref, g_ref, *refs, tm, n_w):
    w_refs, o_refs = refs[:n_w], refs[n_w:]
    row = _mod_row(pl.program_id(0), tm)
    h = _norm_mod(x_ref[...], g_ref[...], _mod_slice(mod_ref, row, 1), _mod_slice(mod_ref, row, 0)).astype(BF16)
    for w_ref, o_ref in zip(w_refs, o_refs):
        o_ref[...] = _dot(h, w_ref[...])


def _norm_proj(x, mod, g, ws, tm=256):
    n_w = len(ws)
    const = lambda i: (0, 0)
    return pl.pallas_call(
        functools.partial(_norm_proj_kernel, tm=tm, n_w=n_w),
        grid=(N_TOK // tm,),
        in_specs=[
            pl.BlockSpec((tm, D_MODEL), lambda i: (i, 0)),
            pl.BlockSpec(mod.shape, const),
            pl.BlockSpec((1, D_MODEL), const),
        ] + [pl.BlockSpec(w.shape, const, pipeline_mode=pl.Buffered(1)) for w in ws],
        out_specs=[pl.BlockSpec((tm, w.shape[1]), lambda i: (i, 0)) for w in ws],
        out_shape=[jax.ShapeDtypeStruct((N_TOK, w.shape[1]), F32) for w in ws],
        compiler_params=_cparams("arbitrary"),
        name="norm_proj",
    )(x, mod, g.reshape(1, D_MODEL), *ws)


def _post_kernel(x_ref, m_ref, mod_ref, g_ref, wo_ref, w1_ref, w2_ref, y_ref, *, tm, ff_chunk):
    row = _mod_row(pl.program_id(0), tm)
    mix = _dot(m_ref[...].astype(BF16), wo_ref[...])
    x1 = x_ref[...] + _mod_slice(mod_ref, row, 2) * mix
    h = _norm_mod(x1, g_ref[...], _mod_slice(mod_ref, row, 4), _mod_slice(mod_ref, row, 3)).astype(BF16)
    acc = jnp.zeros((tm, D_MODEL), F32)
    for k in range(0, D_FF, ff_chunk):
        a = jnp.maximum(_dot(h, w1_ref[:, k:k + ff_chunk]), 0.0)
        acc = acc + _dot((a * a).astype(BF16), w2_ref[k:k + ff_chunk, :])
    y_ref[...] = x1 + _mod_slice(mod_ref, row, 5) * acc


def _post_mixer(x, mixed, mod, g, wo, w1, w2, tm=256, ff_chunk=1024):
    const = lambda i: (0, 0)
    row_spec = pl.BlockSpec((tm, D_MODEL), lambda i: (i, 0))
    return pl.pallas_call(
        functools.partial(_post_kernel, tm=tm, ff_chunk=ff_chunk),
        grid=(N_TOK // tm,),
        in_specs=[
            row_spec,
            row_spec,
            pl.BlockSpec(mod.shape, const),
            pl.BlockSpec((1, D_MODEL), const),
            pl.BlockSpec(wo.shape, const, pipeline_mode=pl.Buffered(1)),
            pl.BlockSpec(w1.shape, const, pipeline_mode=pl.Buffered(1)),
            pl.BlockSpec(w2.shape, const, pipeline_mode=pl.Buffered(1)),
        ],
        out_specs=row_spec,
        out_shape=jax.ShapeDtypeStruct((N_TOK, D_MODEL), F32),
        compiler_params=_cparams("arbitrary"),
        name="post_mixer",
    )(x, mixed, mod, g.reshape(1, D_MODEL), wo, w1, w2)


def _head_norm(x, w):
    ms = jnp.mean(x * x, axis=-1, keepdims=True)
    return x * lax.rsqrt(ms + EPS) * w


def _softmax_pv(scores, values):
    m = functools.reduce(jnp.maximum, [jnp.max(s, axis=-1, keepdims=True) for s in scores])
    ps = [jnp.exp(s - m) for s in scores]
    l = functools.reduce(lambda a, b: a + b, [jnp.sum(p, axis=-1, keepdims=True) for p in ps])
    o = functools.reduce(lambda a, b: a + b, [_dot(p.astype(BF16), v) for p, v in zip(ps, values)])
    return o / l


def _ctx_attn_kernel(q_ref, k_ref, v_ref, qn_ref, kn_ref, o_ref, kc_ref, vc_ref):
    scale = HD_C ** -0.5
    outs = []
    for j in range(2):
        sl = slice(j * HD_C, (j + 1) * HD_C)
        q = _head_norm(q_ref[:, sl], qn_ref[...])
        k = _head_norm(k_ref[:, sl], kn_ref[...])
        v = v_ref[:, sl]
        kc_ref[0, 0, j] = k
        vc_ref[0, 0, j] = v
        s = _dot_nt(q.astype(BF16), k.astype(BF16)) * scale
        outs.append(_softmax_pv([s], [v.astype(BF16)]))
    o_ref[...] = jnp.concatenate(outs, axis=-1)


def _ctx_attention(qkv, qn, kn):
    nhp = H_C // 2
    blk = lambda off: pl.BlockSpec((SEQ, 2 * HD_C), lambda b, p: (b, off + p))
    cache_spec = pl.BlockSpec((1, 1, 2, SEQ, HD_C), lambda b, p: (b, 0, p, 0, 0))
    cache_shape = jax.ShapeDtypeStruct((BATCH, 1, H_C, SEQ, HD_C), F32)
    return pl.pallas_call(
        _ctx_attn_kernel,
        grid=(BATCH, nhp),
        in_specs=[blk(0), blk(nhp), blk(2 * nhp),
                  pl.BlockSpec((1, HD_C), lambda b, p: (0, 0)),
                  pl.BlockSpec((1, HD_C), lambda b, p: (0, 0))],
        out_specs=[pl.BlockSpec((SEQ, 2 * HD_C), lambda b, p: (b, p)), cache_spec, cache_spec],
        out_shape=[jax.ShapeDtypeStruct((N_TOK, D_MODEL), F32), cache_shape, cache_shape],
        compiler_params=_cparams("arbitrary", "arbitrary"),
        name="ctx_attention",
    )(qkv, qkv, qkv, qn.reshape(1, HD_C), kn.reshape(1, HD_C))


def _na_row_start(r):
    return min(max(r - KH // 2, 0), GRID_ROWS - KH)


def _na_attn_kernel(q_ref, k_ref, v_ref, kc_ref, vc_ref, qn_ref, kn_ref, bias_ref, mixed_in_ref, o_ref, qs, ks, vs):
    del mixed_in_ref
    scale = HD_C ** -0.5
    for j in range(2):
        sl = slice(j * HD_C, (j + 1) * HD_C)
        qs[j] = _head_norm(q_ref[:, sl], qn_ref[...]).astype(BF16)
        ks[j] = _head_norm(k_ref[:, sl], kn_ref[...]).astype(BF16)
        vs[j] = v_ref[:, sl].astype(BF16)
        k_ctx = kc_ref[0, 0, j].astype(BF16)
        v_ctx = vc_ref[0, 0, j].astype(BF16)
        for r in range(GRID_ROWS):
            rs = _na_row_start(r)
            q_r = qs[j, r * GRID_W:(r + 1) * GRID_W, :]
            win = slice(rs * GRID_W, (rs + KH) * GRID_W)
            dr0 = KH - 1 - (r - rs)
            lane0 = (dr0 - dr0 % 2) * GRID_W
            s_win = _dot_nt(q_r, ks[j, win, :]) * scale + bias_ref[j, dr0 % 2, :, lane0:lane0 + KH * GRID_W]
            s_ctx = _dot_nt(q_r, k_ctx) * scale
            o_ref[r * GRID_W:(r + 1) * GRID_W, sl] = _softmax_pv([s_win, s_ctx], [vs[j, win, :], v_ctx])


NA_BIAS_LANES = 2 * KH * GRID_W


def _na_bias_table(rpb):
    qc = np.arange(GRID_W)[:, None]
    kc = np.arange(GRID_W)[None, :]
    cstart = np.clip(qc - KW // 2, 0, GRID_W - KW)
    valid = (kc >= cstart) & (kc < cstart + KW)
    onehot = ((kc - qc + KW - 1)[None] == np.arange(2 * KW - 1)[:, None, None]) & valid[None]
    t = jnp.einsum('hrd,dqk->hqrk', rpb.astype(F32), jnp.asarray(onehot, F32), precision=lax.Precision.HIGHEST)
    t = jnp.where(valid[None, :, None, :], t, NEG_INF)
    t = jnp.pad(t.reshape(H_C, GRID_W, (2 * KH - 1) * GRID_W), ((0, 0), (0, 0), (0, 2 * GRID_W)))
    return jnp.stack([t[:, :, :NA_BIAS_LANES], t[:, :, GRID_W:GRID_W + NA_BIAS_LANES]], axis=1)


def _na_attention(qkv, cache_k, cache_v, qn, kn, bias, mixed):
    nhp = H_C // 2
    row0 = N_PROMPT // DEC_SEQ
    blk = lambda off: pl.BlockSpec((DEC_SEQ, 2 * HD_C), lambda p, b: (row0 + b, off + p))
    cache_spec = pl.BlockSpec((1, 1, 2, PAST_LEN, HD_C), lambda p, b: (b, 0, p, 0, 0))
    return pl.pallas_call(
        _na_attn_kernel,
        grid=(nhp, DEC_BATCH),
        in_specs=[blk(0), blk(nhp), blk(2 * nhp), cache_spec, cache_spec,
                  pl.BlockSpec((1, HD_C), lambda p, b: (0, 0)),
                  pl.BlockSpec((1, HD_C), lambda p, b: (0, 0)),
                  pl.BlockSpec((2, 2, GRID_W, NA_BIAS_LANES), lambda p, b: (p, 0, 0, 0)),
                  pl.BlockSpec(memory_space=pl.ANY)],
        out_specs=pl.BlockSpec((DEC_SEQ, 2 * HD_C), lambda p, b: (row0 + b, p)),
        out_shape=jax.ShapeDtypeStruct((N_TOK, D_MODEL), F32),
        input_output_aliases={8: 0},
        scratch_shapes=[pltpu.VMEM((2, DEC_SEQ, HD_C), BF16)] * 3,
        compiler_params=_cparams("arbitrary", "arbitrary"),
        name="na_attention",
    )(qkv, qkv, qkv, cache_k, cache_v, qn.reshape(1, HD_C), kn.reshape(1, HD_C), bias, mixed)


def _seq_layout(prompt):
    return (SEQ, BATCH, 0) if prompt else (DEC_SEQ, DEC_BATCH, N_PROMPT // DEC_SEQ)


def _flip_blocks(m, c):
    r, s = m.shape
    return m.reshape(r // c, c, s // c, c)[:, ::-1, :, ::-1].reshape(r, s)


def _rms_gate(x, gn, gate):
    ms = jnp.mean(x * x, axis=-1, keepdims=True)
    return x * lax.rsqrt(ms + EPS) * gn * _silu(gate)


HG_LEVELS = tuple(CHUNK >> (i + 1) for i in range(CHUNK.bit_length() - 1))
HG_NL = len(HG_LEVELS)
HG_STACK = (HG_NL + 1) * CHUNK
TOT_ROWS = 16
HG_ROWS = (HG_NL + 2) * CHUNK + TOT_ROWS


def _hgrn_consts():
    c = CHUNK
    level_rows = []
    mask = np.zeros((HG_STACK, HG_STACK), np.float32)
    mask[:c, :c] = np.eye(c)
    for li, b in enumerate(HG_LEVELS):
        m = np.zeros((c, c), np.float32)
        blk = np.zeros((c, c), np.float32)
        for t in range(c):
            mid = (t // (2 * b)) * 2 * b + b
            if t >= mid:
                m[t, mid:t + 1] = 1.0
                blk[t, mid - b:mid] = 1.0
            else:
                m[t, t + 1:mid] = 1.0
        level_rows.append(m)
        mask[(li + 1) * c:(li + 2) * c, (li + 1) * c:(li + 2) * c] = blk
    dq = np.tril(np.ones((c, c), np.float32))
    dk = np.triu(np.ones((c, c), np.float32), 1)
    body = np.concatenate(level_rows + [dq, dk], axis=0)
    tot = np.ones((TOT_ROWS, c), np.float32)
    mcs, masks = [], []
    for reverse in (False, True):
        bm = _flip_blocks(body, c) if reverse else body
        mk = _flip_blocks(mask, c) if reverse else mask
        mc = np.concatenate([bm, tot], axis=0)
        mcs.append(np.concatenate([mc, mc], axis=1))
        masks.append(mk)
    return jnp.asarray(np.stack(mcs), BF16), jnp.asarray(np.stack(masks), F32)


def _hgrn_kernel(*refs, seq, has_s0, emit_state, aliased):
    it = iter(refs)
    qa_ref, ff_ref, fb_ref, ia_ref, ga_ref, lb_ref, gn_ref, mc_ref, mask_ref = [next(it) for _ in range(9)]
    s0_ref = next(it) if has_s0 else None
    if aliased:
        next(it)
    o_ref = next(it)
    st_ref = next(it) if emit_state else None
    s_scr, acc = next(it), next(it)
    c = CHUNK
    n_chunks = seq // c

    lb_raw = lb_ref[...]
    lb_e = jnp.exp(lb_raw - jnp.max(lb_raw, axis=0, keepdims=True))
    lb_all = lb_e[0:1] / jnp.sum(lb_e, axis=0, keepdims=True)

    for d in range(2):
        for h in range(H_A):
            s_scr[d, h] = s0_ref[0, 0, d, h].T if has_s0 else jnp.zeros((DV_A, DK_A), F32)
    acc[...] = jnp.zeros(acc.shape, F32)

    def body(n, carry):
        for d in range(2):
            cn = n if d == 0 else n_chunks - 1 - n
            rows = pl.ds(pl.multiple_of(cn * c, c), c)
            fr_ref = ff_ref if d == 0 else fb_ref
            mc2 = mc_ref[d]
            mask = mask_ref[d]
            for h in range(H_A):
                ln = slice(h * DK_A, (h + 1) * DK_A)
                lb = lb_all[:, ln]
                f = lb + (1.0 - lb) * _sigmoid(fr_ref[rows, ln])
                k = 1.0 - f
                q = _silu(qa_ref[rows, ln]) * DK_A ** -0.5
                v = ia_ref[rows, ln].astype(BF16)
                e = jnp.exp(_dot_const(mc2, jnp.log(f)))
                lvl = [e[i * c:(i + 1) * c] for i in range(HG_NL + 2)]
                qst = jnp.concatenate([q] + [q * lvl[i] for i in range(HG_NL)], axis=0).astype(BF16)
                kst = jnp.concatenate([k] + [k * lvl[i] for i in range(HG_NL)], axis=0).astype(BF16)
                r = (_dot_nt(qst, kst) * mask).astype(BF16)
                ost = _dot(r, jnp.concatenate([v] * (HG_NL + 1), axis=0))
                o = functools.reduce(lambda a, b: a + b, [ost[i * c:(i + 1) * c] for i in range(HG_NL + 1)])
                st = s_scr[d, h]
                o = o + _dot_nt((q * lvl[HG_NL]).astype(BF16), st.astype(BF16))
                acc[rows, ln] += o
                e_tot = e[(HG_NL + 2) * c:(HG_NL + 2) * c + 1]
                s_scr[d, h] = st * e_tot + _dot_tn(v, (k * lvl[HG_NL + 1]).astype(BF16))
        return carry

    lax.fori_loop(0, n_chunks, body, 0)

    for h in range(H_A):
        ln = slice(h * DV_A, (h + 1) * DV_A)
        o_ref[:, ln] = _rms_gate(acc[:, ln], gn_ref[...], ga_ref[:, ln])
    if emit_state:
        for d in range(2):
            for h in range(H_A):
                st_ref[0, 0, d, h] = s_scr[d, h].T


def _hgrn(proj, hgrn_lb, gn, consts, prompt, s0=None, layer=0, mixed=None):
    seq, nb, rb0 = _seq_layout(prompt)
    mc2, mask = consts
    wa = H_A * DK_A
    blk = lambda j: pl.BlockSpec((seq, wa), lambda b: (rb0 + b, j))
    const2 = lambda b: (0, 0)
    const3 = lambda b: (0, 0, 0)
    st_block = (1, 1, 2, H_A, DK_A, DV_A)
    in_specs = [blk(0), blk(1), blk(2), blk(3), blk(4),
                pl.BlockSpec(hgrn_lb.shape, const2), pl.BlockSpec((1, DV_A), const2),
                pl.BlockSpec(mc2.shape, const3), pl.BlockSpec(mask.shape, const3)]
    args = [proj] * 5 + [hgrn_lb, gn.reshape(1, DV_A), mc2, mask]
    if s0 is not None:
        in_specs.append(pl.BlockSpec(st_block, lambda b: (b, layer, 0, 0, 0, 0)))
        args.append(s0)
    aliases = {}
    if mixed is not None:
        aliases = {len(args): 0}
        in_specs.append(pl.BlockSpec(memory_space=pl.ANY))
        args.append(mixed)
    out_specs = [pl.BlockSpec((seq, wa), lambda b: (rb0 + b, 0))]
    out_shape = [jax.ShapeDtypeStruct((N_TOK, D_MODEL), F32)]
    if prompt:
        out_specs.append(pl.BlockSpec(st_block, lambda b: (b, 0, 0, 0, 0, 0)))
        out_shape.append(jax.ShapeDtypeStruct((nb, 1, 2, H_A, DK_A, DV_A), F32))
    return pl.pallas_call(
        functools.partial(_hgrn_kernel, seq=seq, has_s0=s0 is not None, emit_state=prompt, aliased=mixed is not None),
        grid=(nb,),
        in_specs=in_specs,
        out_specs=out_specs,
        out_shape=out_shape,
        input_output_aliases=aliases,
        scratch_shapes=[pltpu.VMEM((2, H_A, DV_A, DK_A), F32), pltpu.VMEM((seq, wa), F32)],
        compiler_params=_cparams("arbitrary"),
        name="hgrn_prompt" if prompt else "hgrn_sample",
    )(*args)


GD_SUB = GBLK // CHUNK
GD_ROWS = 2 * GBLK + TOT_ROWS


def _gdn_consts():
    n, c = GBLK, CHUNK
    same = (np.arange(n)[:, None] // c) == (np.arange(n)[None, :] // c)
    tri = (same & (np.arange(n)[None, :] <= np.arange(n)[:, None])).astype(np.float32)
    sup = (same & (np.arange(n)[None, :] > np.arange(n)[:, None])).astype(np.float32)
    tot = np.zeros((TOT_ROWS, n), np.float32)
    for s in range(GD_SUB):
        tot[s, s * c:(s + 1) * c] = 1.0
    mgs, mds, tris, trits = [], [], [], []
    for reverse in (False, True):
        t = _flip_blocks(tri, c) if reverse else tri
        s = _flip_blocks(sup, c) if reverse else sup
        mg = np.concatenate([t, s, tot], axis=0)
        md = np.concatenate([t, -same.astype(np.float32)], axis=1)
        mgs.append(np.concatenate([mg, mg], axis=1))
        mds.append(np.concatenate([md, md], axis=1))
        tris.append(t)
        trits.append(t.T)
    return (jnp.asarray(np.stack(mgs), BF16), jnp.asarray(np.stack(mds), BF16),
            jnp.asarray(np.stack(tris), F32), jnp.asarray(np.stack(trits), F32))


def _softplus(x):
    return jnp.maximum(x, 0.0) + jnp.log(1.0 + jnp.exp(-jnp.abs(x)))


def _conv_silu(x, w, seq):
    t_idx = lax.broadcasted_iota(jnp.int32, (seq, 1), 0)
    half = SHORT_CONV // 2
    acc = x * w[half:half + 1]
    for j in range(SHORT_CONV):
        shift = half - j
        if shift == 0:
            continue
        src = t_idx - shift
        xr = pltpu.roll(x, shift % seq, axis=0)
        acc = acc + jnp.where((src >= 0) & (src < seq), xr, 0.0) * w[j:j + 1]
    return _silu(acc)


def _l2norm_heads(x, n_heads, width, scale):
    outs = []
    for h in range(n_heads):
        xh = x[:, h * width:(h + 1) * width]
        outs.append(xh * (lax.rsqrt(jnp.sum(xh * xh, axis=-1, keepdims=True) + EPS) * scale))
    return jnp.concatenate(outs, axis=-1)


def _gdn_kernel(*refs, seq, has_s0, emit_state, aliased):
    it = iter(refs)
    (q_ref, k_ref, v_ref, gb_ref, gate_ref, cw_ref, alog_ref, dt_ref, gn_ref,
     mg_ref, md_ref, tri_ref, trit_ref) = [next(it) for _ in range(13)]
    s0_ref = next(it) if has_s0 else None
    if aliased:
        next(it)
    o_ref = next(it)
    st_ref = next(it) if emit_state else None
    qn, kn, vn, u_s, w_s, qg_s, kd_s, at_s, et_s, s_scr, acc = [next(it) for _ in range(11)]
    c = CHUNK
    n_chunks = seq // c
    n_blocks = seq // GBLK
    wq = H_B * DK_B

    qn[...] = _l2norm_heads(_conv_silu(q_ref[...], cw_ref[:, 0:wq], seq), H_B, DK_B, DK_B ** -0.5)
    kn[...] = _l2norm_heads(_conv_silu(k_ref[...], cw_ref[:, wq:2 * wq], seq), H_B, DK_B, 1.0)
    vn[...] = _conv_silu(v_ref[...], cw_ref[:, 2 * wq:3 * wq], seq)
    for i in range(2 * H_B):
        s_scr[i] = s0_ref[0, 0, i // H_B, i % H_B] if has_s0 else jnp.zeros((DK_B, DV_B), F32)
    acc[...] = jnp.zeros(acc.shape, F32)

    eye = (lax.broadcasted_iota(jnp.int32, (GBLK, GBLK), 0)
           == lax.broadcasted_iota(jnp.int32, (GBLK, GBLK), 1)).astype(F32)

    def block_body(blk, carry):
        rows = pl.ds(pl.multiple_of(blk * GBLK, GBLK), GBLK)
        gates = gate_ref[rows, :]
        glog_all = -jnp.exp(alog_ref[...]) * _softplus(gates + dt_ref[...])
        beta_all = _sigmoid(gates)
        for d in range(2):
            tri = tri_ref[d]
            inside = tri > 0.0
            for h in range(H_B):
                i = d * H_B + h
                ln = slice(h * DK_B, (h + 1) * DK_B)
                q, k, v = qn[rows, ln], kn[rows, ln], vn[rows, ln]
                glog = jnp.broadcast_to(glog_all[:, i:i + 1], (GBLK, GBLK))
                beta = jnp.broadcast_to(beta_all[:, 2 * H_B + i:2 * H_B + i + 1], (GBLK, DK_B))
                eg = jnp.exp(_dot_const(mg_ref[d], glog))
                gd = _dot_const(md_ref[d], jnp.concatenate([glog, glog * trit_ref[d]], axis=0))
                decay = jnp.where(inside, jnp.exp(jnp.where(inside, gd, 0.0)), 0.0)
                kb = k * beta
                kk = _dot_nt(jnp.concatenate([q, kb], axis=0).astype(BF16), k.astype(BF16))
                attn = kk[:GBLK] * decay
                a = kk[GBLK:] * decay * (1.0 - eye)
                x = eye - a
                p = a
                for _ in range(CHUNK.bit_length() - 2):
                    p = _dot3(p, p)
                    x = x + _dot3(x, p)
                uw = _dot3(x, jnp.concatenate([v * beta, kb * eg[:GBLK]], axis=1))
                qg = q * eg[:GBLK]
                kd = k * eg[GBLK:2 * GBLK]
                for s in range(GD_SUB):
                    cn = blk * GD_SUB + s
                    r = slice(s * c, (s + 1) * c)
                    u_s[i, cn] = uw[r, :DV_B]
                    w_s[i, cn] = uw[r, DV_B:].astype(BF16)
                    qg_s[i, cn] = qg[r].astype(BF16)
                    kd_s[i, cn] = kd[r].astype(BF16)
                    at_s[i, cn] = attn[r, r].astype(BF16)
                    et_s[i, cn] = jnp.broadcast_to(eg[2 * GBLK + s:2 * GBLK + s + 1], (8, DV_B))
        return carry

    lax.fori_loop(0, n_blocks, block_body, 0)

    def chunk_body(n, carry):
        for d in range(2):
            cn = n if d == 0 else n_chunks - 1 - n
            rows = pl.ds(pl.multiple_of(cn * c, c), c)
            for h in range(H_B):
                i = d * H_B + h
                ln = slice(h * DV_B, (h + 1) * DV_B)
                st = s_scr[i]
                ws = _dot(jnp.concatenate([w_s[i, cn], qg_s[i, cn]], axis=0), st.astype(BF16))
                v_new = (u_s[i, cn] - ws[:c]).astype(BF16)
                acc[rows, ln] += ws[c:] + _dot(at_s[i, cn], v_new)
                s_scr[i] = st * et_s[i, cn][0:1] + _dot_tn(kd_s[i, cn], v_new)
        return carry

    lax.fori_loop(0, n_chunks, chunk_body, 0)

    for h in range(H_B):
        ln = slice(h * DV_B, (h + 1) * DV_B)
        o_ref[:, ln] = _rms_gate(acc[:, ln], gn_ref[...], gb_ref[:, ln])
    if emit_state:
        for i in range(2 * H_B):
            st_ref[0, 0, i // H_B, i % H_B] = s_scr[i]


def _gdn(proj, gates, conv_w, a_log, dt_bias, gn, consts, prompt, mixed, s0=None, layer=0):
    seq, nb, rb0 = _seq_layout(prompt)
    n_chunks = seq // CHUNK
    wq = H_B * DK_B
    blk = lambda j: pl.BlockSpec((seq, wq), lambda b: (rb0 + b, j))
    const2 = lambda b: (0, 0)
    const3 = lambda b: (0, 0, 0)
    st_block = (1, 1, 2, H_B, DK_B, DV_B)
    pad_row = lambda p: jnp.pad(p.reshape(1, -1).astype(F32), ((0, 0), (0, 128 - p.size)))
    in_specs = [blk(5), blk(6), blk(7), blk(8),
                pl.BlockSpec((seq, 128), lambda b: (rb0 + b, 0)),
                pl.BlockSpec((SHORT_CONV, 3 * wq), const2),
                pl.BlockSpec((1, 128), const2), pl.BlockSpec((1, 128), const2), pl.BlockSpec((1, DV_B), const2)]
    in_specs += [pl.BlockSpec(m.shape, const3) for m in consts]
    args = [proj] * 4 + [gates, conv_w.reshape(SHORT_CONV, 3 * wq), pad_row(a_log), pad_row(dt_bias),
                         gn.reshape(1, DV_B)] + list(consts)
    if s0 is not None:
        in_specs.append(pl.BlockSpec(st_block, lambda b: (b, layer, 0, 0, 0, 0)))
        args.append(s0)
    aliases = {len(args): 0}
    in_specs.append(pl.BlockSpec(memory_space=pl.ANY))
    args.append(mixed)
    out_specs = [pl.BlockSpec((seq, wq), lambda b: (rb0 + b, 1))]
    out_shape = [jax.ShapeDtypeStruct((N_TOK, D_MODEL), F32)]
    if prompt:
        out_specs.append(pl.BlockSpec(st_block, lambda b: (b, 0, 0, 0, 0, 0)))
        out_shape.append(jax.ShapeDtypeStruct((nb, 1, 2, H_B, DK_B, DV_B), F32))
    n_dh = 2 * H_B
    scratch = ([pltpu.VMEM((seq, wq), F32)] * 3
               + [pltpu.VMEM((n_dh, n_chunks, CHUNK, DV_B), F32)]
               + [pltpu.VMEM((n_dh, n_chunks, CHUNK, DK_B), BF16)] * 3
               + [pltpu.VMEM((n_dh, n_chunks, CHUNK, CHUNK), BF16),
                  pltpu.VMEM((n_dh, n_chunks, 8, DV_B), F32),
                  pltpu.VMEM((n_dh, DK_B, DV_B), F32),
                  pltpu.VMEM((seq, wq), F32)])
    return pl.pallas_call(
        functools.partial(_gdn_kernel, seq=seq, has_s0=s0 is not None, emit_state=prompt, aliased=True),
        grid=(nb,),
        in_specs=in_specs,
        out_specs=out_specs,
        out_shape=out_shape,
        input_output_aliases=aliases,
        scratch_shapes=scratch,
        compiler_params=_cparams("arbitrary"),
        name="gdn_prompt" if prompt else "gdn_sample",
    )(*args)


def kernel(x_prompt, x_sample, state_hgrn, state_gdn, cache_na_k, cache_na_v, c, c_ctx, ada_w, ada_b, norm_g, w_in_ab, w_out_ab, hgrn_lb, gdn_conv, gdn_a_log, gdn_dt_bias, gn_hgrn, gn_gdn, w_qkv_na, qn_na, kn_na, rpb_na, w_out_na, w_mlp1, w_mlp2):
    cond = jnp.concatenate([c_ctx[None, :], c, jnp.zeros((N_MOD_ROWS - 1 - DEC_BATCH, D_MODEL), F32)], axis=0)
    mods = _modulation(cond, ada_w, ada_b)
    x = jnp.concatenate([x_prompt.reshape(N_PROMPT, D_MODEL), x_sample.reshape(N_SAMPLE, D_MODEL)], axis=0)

    w_in = w_in_ab[0]
    w_main = w_in[:, :D_MAIN_AB].astype(BF16)
    w_gate = jnp.pad(w_in[:, D_MAIN_AB:], ((0, 0), (0, 128 - N_GATE_AB))).astype(BF16)
    proj, gates = _norm_proj(x, mods[0], norm_g[0, 0], [w_main, w_gate])
    hg_consts = _hgrn_consts()
    gd_consts = _gdn_consts()
    mixed, new_hgrn = _hgrn(proj, hgrn_lb, gn_hgrn[0], hg_consts, True)
    mixed, = _hgrn(proj, hgrn_lb, gn_hgrn[0], hg_consts, False, s0=state_hgrn, mixed=mixed)
    gd_args = (gdn_conv[0], gdn_a_log[0], gdn_dt_bias[0], gn_gdn[0], gd_consts)
    mixed, new_gdn = _gdn(proj, gates, *gd_args, True, mixed)
    mixed, = _gdn(proj, gates, *gd_args, False, mixed, s0=state_gdn)
    x = _post_mixer(x, mixed, mods[0], norm_g[0, 1], w_out_ab[0].astype(BF16),
                    w_mlp1[0].astype(BF16), w_mlp2[0].astype(BF16))

    qkv, = _norm_proj(x, mods[1], norm_g[1, 0], [w_qkv_na[0].astype(BF16)])
    mixed, new_k, new_v = _ctx_attention(qkv, qn_na[0], kn_na[0])
    mixed = _na_attention(qkv, cache_na_k, cache_na_v, qn_na[0], kn_na[0], _na_bias_table(rpb_na[0]), mixed)
    x = _post_mixer(x, mixed, mods[1], norm_g[1, 1], w_out_na[0].astype(BF16),
                    w_mlp1[1].astype(BF16), w_mlp2[1].astype(BF16))

    return (x[:N_PROMPT].reshape(BATCH, SEQ, D_MODEL), x[N_PROMPT:].reshape(DEC_BATCH, DEC_SEQ, D_MODEL),
            new_hgrn, new_gdn, new_k, new_v)
```

```python
import functools

import numpy as np
import jax
import jax.numpy as jnp
from jax import lax
from jax.experimental import pallas as pl
from jax.experimental.pallas import tpu as pltpu

F32 = jnp.float32
BF16 = jnp.bfloat16

D_MODEL = 1024
BATCH = 16
SEQ = 256
DEC_BATCH = 4
DEC_SEQ = 1024
PAST_LEN = 256
N_PROMPT = BATCH * SEQ
N_SAMPLE = DEC_BATCH * DEC_SEQ
N_TOK = N_PROMPT + N_SAMPLE
GRID_W = 64
GRID_ROWS = DEC_SEQ // GRID_W
H_A = 4
DK_A = 128
DV_A = 128
H_B = 4
DK_B = 128
DV_B = 128
SHORT_CONV = 5
H_C = 16
HD_C = 64
KH = 8
KW = 16
D_FF = 4 * D_MODEL
EPS = 1e-6
NEG_INF = -1e30
N_MOD_ROWS = 8
D_MAIN_AB = 4608
N_GATE_AB = 16
CHUNK = 32
GBLK = 128
VMEM_LIMIT = 56 * 1024 * 1024


def _cparams(*sem):
    return pltpu.CompilerParams(dimension_semantics=sem, vmem_limit_bytes=VMEM_LIMIT)


def _sigmoid(x):
    return 1.0 / (1.0 + jnp.exp(-x))


def _silu(x):
    return x * _sigmoid(x)


def _dot(a, b):
    return jnp.dot(a, b, preferred_element_type=F32)


def _dot_nt(a, b):
    return lax.dot_general(a, b, (((1,), (1,)), ((), ())), preferred_element_type=F32)


def _dot_tn(a, b):
    return lax.dot_general(a, b, (((0,), (0,)), ((), ())), preferred_element_type=F32)


def _split2(x):
    hi = x.astype(BF16)
    lo = (x - hi.astype(F32)).astype(BF16)
    return hi, lo


def _dot_const(m2, x):
    hi, lo = _split2(x)
    return _dot(m2, jnp.concatenate([hi, lo], axis=0))


def _dot3(a, b):
    ah, al = _split2(a)
    bh, bl = _split2(b)
    return _dot(ah, bh) + (_dot(ah, bl) + _dot(al, bh))


def _mod_row(i, tm):
    start = i * tm
    return jnp.where(start < N_PROMPT, 0, 1 + (start - N_PROMPT) // DEC_SEQ)


def _mod_slice(mod_ref, row, k):
    return mod_ref[pl.ds(row, 1), k * D_MODEL:(k + 1) * D_MODEL]


def _norm_mod(x, g, sc, sh):
    ms = jnp.mean(x * x, axis=-1, keepdims=True)
    return (x * lax.rsqrt(ms + EPS) * g) * (1.0 + sc) + sh


def _mod_kernel(cond_ref, w_ref, b_ref, o_ref):
    s = _silu(cond_ref[...]).astype(BF16)
    o_ref[0] = _dot(s, w_ref[0].astype(BF16)) + b_ref[0]


def _modulation(cond8, ada_w, ada_b):
    depth = ada_w.shape[0]
    tn = 1024
    nj = ada_w.shape[2] // tn
    return pl.pallas_call(
        _mod_kernel,
        grid=(depth, nj),
        in_specs=[
            pl.BlockSpec((N_MOD_ROWS, D_MODEL), lambda l, j: (0, 0)),
            pl.BlockSpec((1, D_MODEL, tn), lambda l, j: (l, 0, j)),
            pl.BlockSpec((1, 1, tn), lambda l, j: (l, 0, j)),
        ],
        out_specs=pl.BlockSpec((1, N_MOD_ROWS, tn), lambda l, j: (l, 0, j)),
        out_shape=jax.ShapeDtypeStruct((depth, N_MOD_ROWS, ada_w.shape[2]), F32),
        compiler_params=_cparams("arbitrary", "arbitrary"),
        name="modulation",
    )(cond8, ada_w, ada_b.reshape(depth, 1, -1))


def _norm_proj_kernel(x_ref, mod_ref, g_ref, *refs, tm, n_w):
    w_refs, o_refs = refs[:n_w], refs[n_w:]
    row = _mod_row(pl.program_id(0), tm)
    h = _norm_mod(x_ref[...], g_ref[...], _mod_slice(mod_ref, row, 1), _mod_slice(mod_ref, row, 0)).astype(BF16)
    for w_ref, o_ref in zip(w_refs, o_refs):
        o_ref[...] = _dot(h, w_ref[...])


def _norm_proj(x, mod, g, ws, tm=256):
    n_w = len(ws)
    const = lambda i: (0, 0)
    return pl.pallas_call(
        functools.partial(_norm_proj_kernel, tm=tm, n_w=n_w),
        grid=(N_TOK // tm,),
        in_specs=[
            pl.BlockSpec((tm, D_MODEL), lambda i: (i, 0)),
            pl.BlockSpec(mod.shape, const),
            pl.BlockSpec((1, D_MODEL), const),
        ] + [pl.BlockSpec(w.shape, const, pipeline_mode=pl.Buffered(1)) for w in ws],
        out_specs=[pl.BlockSpec((tm, w.shape[1]), lambda i: (i, 0)) for w in ws],
        out_shape=[jax.ShapeDtypeStruct((N_TOK, w.shape[1]), F32) for w in ws],
        compiler_params=_cparams("arbitrary"),
        name="norm_proj",
    )(x, mod, g.reshape(1, D_MODEL), *ws)


def _post_kernel(x_ref, m_ref, mod_ref, g_ref, wo_ref, w1_ref, w2_ref, y_ref, *, tm, ff_chunk):
    row = _mod_row(pl.program_id(0), tm)
    mix = _dot(m_ref[...].astype(BF16), wo_ref[...])
    x1 = x_ref[...] + _mod_slice(mod_ref, row, 2) * mix
    h = _norm_mod(x1, g_ref[...], _mod_slice(mod_ref, row, 4), _mod_slice(mod_ref, row, 3)).astype(BF16)
    acc = jnp.zeros((tm, D_MODEL), F32)
    for k in range(0, D_FF, ff_chunk):
        a = jnp.maximum(_dot(h, w1_ref[:, k:k + ff_chunk]), 0.0)
        acc = acc + _dot((a * a).astype(BF16), w2_ref[k:k + ff_chunk, :])
    y_ref[...] = x1 + _mod_slice(mod_ref, row, 5) * acc


def _post_mixer(x, mixed, mod, g, wo, w1, w2, tm=256, ff_chunk=1024):
    const = lambda i: (0, 0)
    row_spec = pl.BlockSpec((tm, D_MODEL), lambda i: (i, 0))
    return pl.pallas_call(
        functools.partial(_post_kernel, tm=tm, ff_chunk=ff_chunk),
        grid=(N_TOK // tm,),
        in_specs=[
            row_spec,
            row_spec,
            pl.BlockSpec(mod.shape, const),
            pl.BlockSpec((1, D_MODEL), const),
            pl.BlockSpec(wo.shape, const, pipeline_mode=pl.Buffered(1)),
            pl.BlockSpec(w1.shape, const, pipeline_mode=pl.Buffered(1)),
            pl.BlockSpec(w2.shape, const, pipeline_mode=pl.Buffered(1)),
        ],
        out_specs=row_spec,
        out_shape=jax.ShapeDtypeStruct((N_TOK, D_MODEL), F32),
        compiler_params=_cparams("arbitrary"),
        name="post_mixer",
    )(x, mixed, mod, g.reshape(1, D_MODEL), wo, w1, w2)


def _head_norm(x, w):
    ms = jnp.mean(x * x, axis=-1, keepdims=True)
    return x * lax.rsqrt(ms + EPS) * w


def _softmax_pv(scores, values):
    m = functools.reduce(jnp.maximum, [jnp.max(s, axis=-1, keepdims=True) for s in scores])
    ps = [jnp.exp(s - m) for s in scores]
    l = functools.reduce(lambda a, b: a + b, [jnp.sum(p, axis=-1, keepdims=True) for p in ps])
    o = functools.reduce(lambda a, b: a + b, [_dot(p.astype(BF16), v) for p, v in zip(ps, values)])
    return o / l


def _ctx_attn_kernel(q_ref, k_ref, v_ref, qn_ref, kn_ref, o_ref, kc_ref, vc_ref):
    scale = HD_C ** -0.5
    outs = []
    for j in range(2):
        sl = slice(j * HD_C, (j + 1) * HD_C)
        q = _head_norm(q_ref[:, sl], qn_ref[...])
        k = _head_norm(k_ref[:, sl], kn_ref[...])
        v = v_ref[:, sl]
        kc_ref[0, 0, j] = k
        vc_ref[0, 0, j] = v
        s = _dot_nt(q.astype(BF16), k.astype(BF16)) * scale
        outs.append(_softmax_pv([s], [v.astype(BF16)]))
    o_ref[...] = jnp.concatenate(outs, axis=-1)


def _ctx_attention(qkv, qn, kn):
    nhp = H_C // 2
    blk = lambda off: pl.BlockSpec((SEQ, 2 * HD_C), lambda b, p: (b, off + p))
    cache_spec = pl.BlockSpec((1, 1, 2, SEQ, HD_C), lambda b, p: (b, 0, p, 0, 0))
    cache_shape = jax.ShapeDtypeStruct((BATCH, 1, H_C, SEQ, HD_C), F32)
    return pl.pallas_call(
        _ctx_attn_kernel,
        grid=(BATCH, nhp),
        in_specs=[blk(0), blk(nhp), blk(2 * nhp),
                  pl.BlockSpec((1, HD_C), lambda b, p: (0, 0)),
                  pl.BlockSpec((1, HD_C), lambda b, p: (0, 0))],
        out_specs=[pl.BlockSpec((SEQ, 2 * HD_C), lambda b, p: (b, p)), cache_spec, cache_spec],
        out_shape=[jax.ShapeDtypeStruct((N_TOK, D_MODEL), F32), cache_shape, cache_shape],
        compiler_params=_cparams("arbitrary", "arbitrary"),
        name="ctx_attention",
    )(qkv, qkv, qkv, qn.reshape(1, HD_C), kn.reshape(1, HD_C))


def _na_row_start(r):
    return min(max(r - KH // 2, 0), GRID_ROWS - KH)


def _na_attn_kernel(q_ref, k_ref, v_ref, kc_ref, vc_ref, qn_ref, kn_ref, bias_ref, mixed_in_ref, o_ref, qs, ks, vs):
    del mixed_in_ref
    scale = HD_C ** -0.5
    for j in range(2):
        sl = slice(j * HD_C, (j + 1) * HD_C)
        qs[j] = _head_norm(q_ref[:, sl], qn_ref[...]).astype(BF16)
        ks[j] = _head_norm(k_ref[:, sl], kn_ref[...]).astype(BF16)
        vs[j] = v_ref[:, sl].astype(BF16)
        k_ctx = kc_ref[0, 0, j].astype(BF16)
        v_ctx = vc_ref[0, 0, j].astype(BF16)
        for r in range(GRID_ROWS):
            rs = _na_row_start(r)
            q_r = qs[j, r * GRID_W:(r + 1) * GRID_W, :]
            win = slice(rs * GRID_W, (rs + KH) * GRID_W)
            dr0 = KH - 1 - (r - rs)
            lane0 = (dr0 - dr0 % 2) * GRID_W
            s_win = _dot_nt(q_r, ks[j, win, :]) * scale + bias_ref[j, dr0 % 2, :, lane0:lane0 + KH * GRID_W]
            s_ctx = _dot_nt(q_r, k_ctx) * scale
            o_ref[r * GRID_W:(r + 1) * GRID_W, sl] = _softmax_pv([s_win, s_ctx], [vs[j, win, :], v_ctx])


NA_BIAS_LANES = 2 * KH * GRID_W


def _na_bias_table(rpb):
    qc = np.arange(GRID_W)[:, None]
    kc = np.arange(GRID_W)[None, :]
    cstart = np.clip(qc - KW // 2, 0, GRID_W - KW)
    valid = (kc >= cstart) & (kc < cstart + KW)
    onehot = ((kc - qc + KW - 1)[None] == np.arange(2 * KW - 1)[:, None, None]) & valid[None]
    t = jnp.einsum('hrd,dqk->hqrk', rpb.astype(F32), jnp.asarray(onehot, F32), precision=lax.Precision.HIGHEST)
    t = jnp.where(valid[None, :, None, :], t, NEG_INF)
    t = jnp.pad(t.reshape(H_C, GRID_W, (2 * KH - 1) * GRID_W), ((0, 0), (0, 0), (0, 2 * GRID_W)))
    return jnp.stack([t[:, :, :NA_BIAS_LANES], t[:, :, GRID_W:GRID_W + NA_BIAS_LANES]], axis=1)


def _na_attention(qkv, cache_k, cache_v, qn, kn, bias, mixed):
    nhp = H_C // 2
    row0 = N_PROMPT // DEC_SEQ
    blk = lambda off: pl.BlockSpec((DEC_SEQ, 2 * HD_C), lambda p, b: (row0 + b, off + p))
    cache_spec = pl.BlockSpec((1, 1, 2, PAST_LEN, HD_C), lambda p, b: (b, 0, p, 0, 0))
    return pl.pallas_call(
        _na_attn_kernel,
        grid=(nhp, DEC_BATCH),
        in_specs=[blk(0), blk(nhp), blk(2 * nhp), cache_spec, cache_spec,
                  pl.BlockSpec((1, HD_C), lambda p, b: (0, 0)),
                  pl.BlockSpec((1, HD_C), lambda p, b: (0, 0)),
                  pl.BlockSpec((2, 2, GRID_W, NA_BIAS_LANES), lambda p, b: (p, 0, 0, 0)),
                  pl.BlockSpec(memory_space=pl.ANY)],
        out_specs=pl.BlockSpec((DEC_SEQ, 2 * HD_C), lambda p, b: (row0 + b, p)),
        out_shape=jax.ShapeDtypeStruct((N_TOK, D_MODEL), F32),
        input_output_aliases={8: 0},
        scratch_shapes=[pltpu.VMEM((2, DEC_SEQ, HD_C), BF16)] * 3,
        compiler_params=_cparams("arbitrary", "arbitrary"),
        name="na_attention",
    )(qkv, qkv, qkv, cache_k, cache_v, qn.reshape(1, HD_C), kn.reshape(1, HD_C), bias, mixed)


def _seq_layout(prompt):
    return (SEQ, BATCH, 0) if prompt else (DEC_SEQ, DEC_BATCH, N_PROMPT // DEC_SEQ)


def _flip_blocks(m, c):
    r, s = m.shape
    return m.reshape(r // c, c, s // c, c)[:, ::-1, :, ::-1].reshape(r, s)


def _rms_gate(x, gn, gate):
    ms = jnp.mean(x * x, axis=-1, keepdims=True)
    return x * lax.rsqrt(ms + EPS) * gn * _silu(gate)


HG_LEVELS = tuple(CHUNK >> (i + 1) for i in range(CHUNK.bit_length() - 1))
HG_NL = len(HG_LEVELS)
HG_STACK = (HG_NL + 1) * CHUNK
TOT_ROWS = 16
HG_ROWS = (HG_NL + 2) * CHUNK + TOT_ROWS


def _hgrn_consts():
    c = CHUNK
    level_rows = []
    mask = np.zeros((HG_STACK, HG_STACK), np.float32)
    mask[:c, :c] = np.eye(c)
    for li, b in enumerate(HG_LEVELS):
        m = np.zeros((c, c), np.float32)
        blk = np.zeros((c, c), np.float32)
        for t in range(c):
            mid = (t // (2 * b)) * 2 * b + b
            if t >= mid:
                m[t, mid:t + 1] = 1.0
                blk[t, mid - b:mid] = 1.0
            else:
                m[t, t + 1:mid] = 1.0
        level_rows.append(m)
        mask[(li + 1) * c:(li + 2) * c, (li + 1) * c:(li + 2) * c] = blk
    dq = np.tril(np.ones((c, c), np.float32))
    dk = np.triu(np.ones((c, c), np.float32), 1)
    body = np.concatenate(level_rows + [dq, dk], axis=0)
    tot = np.ones((TOT_ROWS, c), np.float32)
    mcs, masks = [], []
    for reverse in (False, True):
        bm = _flip_blocks(body, c) if reverse else body
        mk = _flip_blocks(mask, c) if reverse else mask
        mc = np.concatenate([bm, tot], axis=0)
        mcs.append(np.concatenate([mc, mc], axis=1))
        masks.append(mk)
    return jnp.asarray(np.stack(mcs), BF16), jnp.asarray(np.stack(masks), F32)


def _hgrn_kernel(*refs, seq, has_s0, emit_state, aliased):
    it = iter(refs)
    qa_ref, ff_ref, fb_ref, ia_ref, ga_ref, lb_ref, gn_ref, mc_ref, mask_ref = [next(it) for _ in range(9)]
    s0_ref = next(it) if has_s0 else None
    if aliased:
        next(it)
    o_ref = next(it)
    st_ref = next(it) if emit_state else None
    s_scr, acc = next(it), next(it)
    c = CHUNK
    n_chunks = seq // c

    lb_raw = lb_ref[...]
    lb_e = jnp.exp(lb_raw - jnp.max(lb_raw, axis=0, keepdims=True))
    lb_all = lb_e[0:1] / jnp.sum(lb_e, axis=0, keepdims=True)

    for d in range(2):
        for h in range(H_A):
            s_scr[d, h] = s0_ref[0, 0, d, h].T if has_s0 else jnp.zeros((DV_A, DK_A), F32)
    acc[...] = jnp.zeros(acc.shape, F32)

    def body(n, carry):
        combos = [(d, h) for d in range(2) for h in range(H_A)]
        lanes = [slice(h * DK_A, (h + 1) * DK_A) for h in range(H_A)]
        rows = [pl.ds(pl.multiple_of((n if d == 0 else n_chunks - 1 - n) * c, c), c) for d in range(2)]
        f_all = [lb_all + (1.0 - lb_all) * _sigmoid((ff_ref, fb_ref)[d][rows[d], :]) for d in range(2)]
        e_all = [jnp.exp(_dot_const(mc_ref[d], jnp.log(f_all[d]))) for d in range(2)]
        q_all = [_silu(qa_ref[rows[d], :]) * DK_A ** -0.5 for d in range(2)]
        v_all = [ia_ref[rows[d], :].astype(BF16) for d in range(2)]
        st = [s_scr[d, h] for d, h in combos]
        qs, ks, vs, es = [], [], [], []
        for d, h in combos:
            qs.append(q_all[d][:, lanes[h]])
            ks.append(1.0 - f_all[d][:, lanes[h]])
            vs.append(v_all[d][:, lanes[h]])
            es.append(e_all[d][:, lanes[h]])
        lvl = [[e[i * c:(i + 1) * c] for i in range(HG_NL + 2)] for e in es]
        qst = [jnp.concatenate([q] + [q * l[i] for i in range(HG_NL)], axis=0).astype(BF16) for q, l in zip(qs, lvl)]
        kst = [jnp.concatenate([k] + [k * l[i] for i in range(HG_NL)], axis=0).astype(BF16) for k, l in zip(ks, lvl)]
        r = [(_dot_nt(qst[i], kst[i]) * mask_ref[d]).astype(BF16) for i, (d, h) in enumerate(combos)]
        ost = [_dot(r[i], jnp.concatenate([vs[i]] * (HG_NL + 1), axis=0)) for i in range(len(combos))]
        inter = [_dot_nt((qs[i] * lvl[i][HG_NL]).astype(BF16), st[i].astype(BF16)) for i in range(len(combos))]
        upd = [_dot_tn(vs[i], (ks[i] * lvl[i][HG_NL + 1]).astype(BF16)) for i in range(len(combos))]
        o = [functools.reduce(lambda a, b: a + b, [ost[i][j * c:(j + 1) * c] for j in range(HG_NL + 1)]) + inter[i]
             for i in range(len(combos))]
        for d in range(2):
            acc[rows[d], :] += jnp.concatenate(o[d * H_A:(d + 1) * H_A], axis=1)
        for i, (d, h) in enumerate(combos):
            e_tot = es[i][(HG_NL + 2) * c:(HG_NL + 2) * c + 1]
            s_scr[d, h] = st[i] * e_tot + upd[i]
        return carry

    lax.fori_loop(0, n_chunks, body, 0)

    for h in range(H_A):
        ln = slice(h * DV_A, (h + 1) * DV_A)
        o_ref[:, ln] = _rms_gate(acc[:, ln], gn_ref[...], ga_ref[:, ln])
    if emit_state:
        for d in range(2):
            for h in range(H_A):
                st_ref[0, 0, d, h] = s_scr[d, h].T


def _hgrn(proj, hgrn_lb, gn, consts, prompt, s0=None, layer=0, mixed=None):
    seq, nb, rb0 = _seq_layout(prompt)
    mc2, mask = consts
    wa = H_A * DK_A
    blk = lambda j: pl.BlockSpec((seq, wa), lambda b: (rb0 + b, j))
    const2 = lambda b: (0, 0)
    const3 = lambda b: (0, 0, 0)
    st_block = (1, 1, 2, H_A, DK_A, DV_A)
    in_specs = [blk(0), blk(1), blk(2), blk(3), blk(4),
                pl.BlockSpec(hgrn_lb.shape, const2), pl.BlockSpec((1, DV_A), const2),
                pl.BlockSpec(mc2.shape, const3), pl.BlockSpec(mask.shape, const3)]
    args = [proj] * 5 + [hgrn_lb, gn.reshape(1, DV_A), mc2, mask]
    if s0 is not None:
        in_specs.append(pl.BlockSpec(st_block, lambda b: (b, layer, 0, 0, 0, 0)))
        args.append(s0)
    aliases = {}
    if mixed is not None:
        aliases = {len(args): 0}
        in_specs.append(pl.BlockSpec(memory_space=pl.ANY))
        args.append(mixed)
    out_specs = [pl.BlockSpec((seq, wa), lambda b: (rb0 + b, 0))]
    out_shape = [jax.ShapeDtypeStruct((N_TOK, D_MODEL), F32)]
    if prompt:
        out_specs.append(pl.BlockSpec(st_block, lambda b: (b, 0, 0, 0, 0, 0)))
        out_shape.append(jax.ShapeDtypeStruct((nb, 1, 2, H_A, DK_A, DV_A), F32))
    return pl.pallas_call(
        functools.partial(_hgrn_kernel, seq=seq, has_s0=s0 is not None, emit_state=prompt, aliased=mixed is not None),
        grid=(nb,),
        in_specs=in_specs,
        out_specs=out_specs,
        out_shape=out_shape,
        input_output_aliases=aliases,
        scratch_shapes=[pltpu.VMEM((2, H_A, DV_A, DK_A), F32), pltpu.VMEM((seq, wa), F32)],
        compiler_params=_cparams("arbitrary"),
        name="hgrn_prompt" if prompt else "hgrn_sample",
    )(*args)


GD_SUB = GBLK // CHUNK
GD_ROWS = 2 * GBLK + TOT_ROWS


def _gdn_consts():
    n, c = GBLK, CHUNK
    same = (np.arange(n)[:, None] // c) == (np.arange(n)[None, :] // c)
    tri = (same & (np.arange(n)[None, :] <= np.arange(n)[:, None])).astype(np.float32)
    sup = (same & (np.arange(n)[None, :] > np.arange(n)[:, None])).astype(np.float32)
    tot = np.zeros((TOT_ROWS, n), np.float32)
    for s in range(GD_SUB):
        tot[s, s * c:(s + 1) * c] = 1.0
    mgs, tts, tris = [], [], []
    for reverse in (False, True):
        t = _flip_blocks(tri, c) if reverse else tri
        s = _flip_blocks(sup, c) if reverse else sup
        mg = np.concatenate([t, s, tot], axis=0)
        mgs.append(np.concatenate([mg, mg], axis=1))
        tts.append(np.concatenate([t.T, t.T], axis=0))
        tris.append(t)
    tris.append(same.astype(np.float32))
    return jnp.asarray(np.stack(mgs), BF16), jnp.asarray(np.stack(tts), BF16), jnp.asarray(np.stack(tris), F32)


def _softplus(x):
    return jnp.maximum(x, 0.0) + jnp.log(1.0 + jnp.exp(-jnp.abs(x)))


def _conv_silu(x, w, seq):
    t_idx = lax.broadcasted_iota(jnp.int32, (seq, 1), 0)
    half = SHORT_CONV // 2
    acc = x * w[half:half + 1]
    for j in range(SHORT_CONV):
        shift = half - j
        if shift == 0:
            continue
        src = t_idx - shift
        xr = pltpu.roll(x, shift % seq, axis=0)
        acc = acc + jnp.where((src >= 0) & (src < seq), xr, 0.0) * w[j:j + 1]
    return _silu(acc)


def _l2norm_heads(x, n_heads, width, scale):
    outs = []
    for h in range(n_heads):
        xh = x[:, h * width:(h + 1) * width]
        outs.append(xh * (lax.rsqrt(jnp.sum(xh * xh, axis=-1, keepdims=True) + EPS) * scale))
    return jnp.concatenate(outs, axis=-1)


def _gdn_kernel(*refs, seq, has_s0, emit_state, aliased):
    it = iter(refs)
    (q_ref, k_ref, v_ref, gb_ref, gate_ref, cw_ref, alog_ref, dt_ref, gn_ref,
     mg_ref, tt_ref, tri_ref) = [next(it) for _ in range(12)]
    s0_ref = next(it) if has_s0 else None
    if aliased:
        next(it)
    o_ref = next(it)
    st_ref = next(it) if emit_state else None
    qn, kn, vn, u_s, w_s, qg_s, kdt_s, at_s, et_s, s_scr, acc = [next(it) for _ in range(11)]
    c = CHUNK
    n_chunks = seq // c
    n_blocks = seq // GBLK
    wq = H_B * DK_B
    n_dh = 2 * H_B
    combos = [(d, h) for d in range(2) for h in range(H_B)]
    lanes = [slice(h * DK_B, (h + 1) * DK_B) for h in range(H_B)]

    qn[...] = _l2norm_heads(_conv_silu(q_ref[...], cw_ref[:, 0:wq], seq), H_B, DK_B, DK_B ** -0.5)
    kn[...] = _l2norm_heads(_conv_silu(k_ref[...], cw_ref[:, wq:2 * wq], seq), H_B, DK_B, 1.0)
    vn[...] = _conv_silu(v_ref[...], cw_ref[:, 2 * wq:3 * wq], seq)
    for i in range(2 * H_B):
        s_scr[i] = s0_ref[0, 0, i // H_B, i % H_B] if has_s0 else jnp.zeros((DK_B, DV_B), F32)
    acc[...] = jnp.zeros(acc.shape, F32)

    eye = (lax.broadcasted_iota(jnp.int32, (GBLK, GBLK), 0)
           == lax.broadcasted_iota(jnp.int32, (GBLK, GBLK), 1)).astype(F32)
    eye_pk = (lax.broadcasted_iota(jnp.int32, (c, GBLK), 0)
              == lax.broadcasted_iota(jnp.int32, (c, GBLK), 1) % c).astype(F32)
    bwd_lane = lax.broadcasted_iota(jnp.int32, (1, 128), 1) % n_dh >= H_B
    add = lambda a, b: a + b

    def expand(pk):
        return jnp.concatenate([pk] * GD_SUB, axis=0) * tri_ref[2]

    def pack(bd):
        return functools.reduce(add, [bd[s * c:(s + 1) * c] for s in range(GD_SUB)])

    def dot3_split(a, b_hi, b_lo):
        ah, al = _split2(a)
        t = _dot(jnp.concatenate([ah, al], axis=0), b_hi)
        return t[:a.shape[0]] + t[a.shape[0]:] + _dot(ah, b_lo)

    def block_body(blk, carry):
        rows = pl.ds(pl.multiple_of(blk * GBLK, GBLK), GBLK)
        gates = gate_ref[rows, :]
        glog_all = -jnp.exp(alog_ref[...]) * _softplus(gates + dt_ref[...])
        beta_all = _sigmoid(gates)
        g2 = jnp.concatenate(_split2(glog_all), axis=0)
        dg = [_dot(mg_ref[d], g2) for d in range(2)]
        dsel = jnp.where(bwd_lane, dg[1], dg[0])
        eg_all = jnp.exp(dsel)
        g_all = dsel[:GBLK]
        gt = [_dot_tn(g2, tt_ref[d]) for d in range(2)]
        qs = [qn[rows, ln] for ln in lanes]
        ks = [kn[rows, ln] for ln in lanes]
        vs = [vn[rows, ln] for ln in lanes]
        col = lambda x, j: jnp.broadcast_to(x[:, j:j + 1], (GBLK, DK_B))
        betas = [col(beta_all, n_dh + i) for i in range(n_dh)]
        kbs = [ks[h] * betas[i] for i, (d, h) in enumerate(combos)]
        kk = [_dot_nt(jnp.concatenate([qs[h], kbs[h], kbs[H_B + h]], axis=0).astype(BF16), ks[h].astype(BF16))
              for h in range(H_B)]
        decay = []
        for i, (d, h) in enumerate(combos):
            inside = tri_ref[d] > 0.0
            gd = col(g_all, i) - gt[d][i:i + 1, :]
            decay.append(jnp.where(inside, jnp.exp(jnp.where(inside, gd, 0.0)), 0.0))
        attn = [kk[h][:GBLK] * decay[i] for i, (d, h) in enumerate(combos)]
        p_pk = [pack(kk[h][(1 + d) * GBLK:(2 + d) * GBLK] * decay[i] * (1.0 - eye)) for i, (d, h) in enumerate(combos)]
        x_pk = [eye_pk - p for p in p_pk]
        p_bd = [_split2(expand(p)) for p in p_pk]
        for _ in range(CHUNK.bit_length() - 2):
            p_pk = [dot3_split(p, *b) for p, b in zip(p_pk, p_bd)]
            p_bd = [_split2(expand(p)) for p in p_pk]
            x_pk = [x + dot3_split(x, *b) for x, b in zip(x_pk, p_bd)]
        eg_col = [col(eg_all[:GBLK], i) for i in range(n_dh)]
        ekd_col = [col(eg_all[GBLK:2 * GBLK], i) for i in range(n_dh)]
        rhs = [jnp.concatenate([vs[h] * betas[i], kbs[i] * eg_col[i]], axis=1) for i, (d, h) in enumerate(combos)]
        uw = [_dot3(expand(x_pk[i]), rhs[i]) for i in range(n_dh)]
        for i, (d, h) in enumerate(combos):
            qg = (qs[h] * eg_col[i]).astype(BF16)
            kdt_s[i, blk] = (ks[h] * ekd_col[i]).T.astype(BF16)
            for s in range(GD_SUB):
                cn = blk * GD_SUB + s
                r = slice(s * c, (s + 1) * c)
                u_s[i, cn] = uw[i][r, :DV_B]
                w_s[i, cn] = uw[i][r, DV_B:].astype(BF16)
                qg_s[i, cn] = qg[r]
                at_s[i, cn] = attn[i][r].astype(BF16)
                et_s[i, cn] = jnp.broadcast_to(eg_all[2 * GBLK + s:2 * GBLK + s + 1, i:i + 1], (8, DV_B))
        return carry

    lax.fori_loop(0, n_blocks, block_body, 0)

    def chunk_body(n, carry):
        cns = [n, n_chunks - 1 - n]
        rows = [pl.ds(pl.multiple_of(cn * c, c), c) for cn in cns]
        sub_of_row = lax.broadcasted_iota(jnp.int32, (GBLK, 1), 0) // c
        in_chunk = [sub_of_row == cn % GD_SUB for cn in cns]
        st = [s_scr[i] for i in range(n_dh)]
        ws = [_dot(jnp.concatenate([w_s[i, cns[d]], qg_s[i, cns[d]]], axis=0), st[i].astype(BF16))
              for i, (d, h) in enumerate(combos)]
        vblk = [jnp.where(in_chunk[d], jnp.concatenate([u_s[i, cns[d]] - ws[i][:c]] * GD_SUB, axis=0), 0.0).astype(BF16)
                for i, (d, h) in enumerate(combos)]
        r = [_dot(jnp.concatenate([at_s[i, cns[d]], kdt_s[i, cns[d] // GD_SUB]], axis=0), vblk[i])
             for i, (d, h) in enumerate(combos)]
        for d in range(2):
            acc[rows[d], :] += jnp.concatenate([ws[i][c:] + r[i][:c] for i in range(d * H_B, (d + 1) * H_B)], axis=1)
        for i, (d, h) in enumerate(combos):
            s_scr[i] = st[i] * et_s[i, cns[d]][0:1] + r[i][c:]
        return carry

    lax.fori_loop(0, n_chunks, chunk_body, 0)

    for h in range(H_B):
        ln = slice(h * DV_B, (h + 1) * DV_B)
        o_ref[:, ln] = _rms_gate(acc[:, ln], gn_ref[...], gb_ref[:, ln])
    if emit_state:
        for i in range(2 * H_B):
            st_ref[0, 0, i // H_B, i % H_B] = s_scr[i]


def _gdn(proj, gates, conv_w, a_log, dt_bias, gn, consts, prompt, mixed, s0=None, layer=0):
    seq, nb, rb0 = _seq_layout(prompt)
    n_chunks = seq // CHUNK
    wq = H_B * DK_B
    blk = lambda j: pl.BlockSpec((seq, wq), lambda b: (rb0 + b, j))
    const2 = lambda b: (0, 0)
    const3 = lambda b: (0, 0, 0)
    st_block = (1, 1, 2, H_B, DK_B, DV_B)
    pad_row = lambda p: jnp.pad(p.reshape(1, -1).astype(F32), ((0, 0), (0, 128 - p.size)))
    in_specs = [blk(5), blk(6), blk(7), blk(8),
                pl.BlockSpec((seq, 128), lambda b: (rb0 + b, 0)),
                pl.BlockSpec((SHORT_CONV, 3 * wq), const2),
                pl.BlockSpec((1, 128), const2), pl.BlockSpec((1, 128), const2), pl.BlockSpec((1, DV_B), const2)]
    in_specs += [pl.BlockSpec(m.shape, const3) for m in consts]
    args = [proj] * 4 + [gates, conv_w.reshape(SHORT_CONV, 3 * wq), pad_row(a_log), pad_row(dt_bias),
                         gn.reshape(1, DV_B)] + list(consts)
    if s0 is not None:
        in_specs.append(pl.BlockSpec(st_block, lambda b: (b, layer, 0, 0, 0, 0)))
        args.append(s0)
    aliases = {len(args): 0}
    in_specs.append(pl.BlockSpec(memory_space=pl.ANY))
    args.append(mixed)
    out_specs = [pl.BlockSpec((seq, wq), lambda b: (rb0 + b, 1))]
    out_shape = [jax.ShapeDtypeStruct((N_TOK, D_MODEL), F32)]
    if prompt:
        out_specs.append(pl.BlockSpec(st_block, lambda b: (b, 0, 0, 0, 0, 0)))
        out_shape.append(jax.ShapeDtypeStruct((nb, 1, 2, H_B, DK_B, DV_B), F32))
    n_dh = 2 * H_B
    scratch = ([pltpu.VMEM((seq, wq), F32)] * 3
               + [pltpu.VMEM((n_dh, n_chunks, CHUNK, DV_B), F32)]
               + [pltpu.VMEM((n_dh, n_chunks, CHUNK, DK_B), BF16)] * 2
               + [pltpu.VMEM((n_dh, seq // GBLK, DK_B, GBLK), BF16),
                  pltpu.VMEM((n_dh, n_chunks, CHUNK, GBLK), BF16),
                  pltpu.VMEM((n_dh, n_chunks, 8, DV_B), F32),
                  pltpu.VMEM((n_dh, DK_B, DV_B), F32),
                  pltpu.VMEM((seq, wq), F32)])
    return pl.pallas_call(
        functools.partial(_gdn_kernel, seq=seq, has_s0=s0 is not None, emit_state=prompt, aliased=True),
        grid=(nb,),
        in_specs=in_specs,
        out_specs=out_specs,
        out_shape=out_shape,
        input_output_aliases=aliases,
        scratch_shapes=scratch,
        compiler_params=_cparams("arbitrary"),
        name="gdn_prompt" if prompt else "gdn_sample",
    )(*args)


def kernel(x_prompt, x_sample, state_hgrn, state_gdn, cache_na_k, cache_na_v, c, c_ctx, ada_w, ada_b, norm_g, w_in_ab, w_out_ab, hgrn_lb, gdn_conv, gdn_a_log, gdn_dt_bias, gn_hgrn, gn_gdn, w_qkv_na, qn_na, kn_na, rpb_na, w_out_na, w_mlp1, w_mlp2):
    cond = jnp.concatenate([c_ctx[None, :], c, jnp.zeros((N_MOD_ROWS - 1 - DEC_BATCH, D_MODEL), F32)], axis=0)
    mods = _modulation(cond, ada_w, ada_b)
    x = jnp.concatenate([x_prompt.reshape(N_PROMPT, D_MODEL), x_sample.reshape(N_SAMPLE, D_MODEL)], axis=0)

    w_in = w_in_ab[0]
    w_main = w_in[:, :D_MAIN_AB].astype(BF16)
    w_gate = jnp.pad(w_in[:, D_MAIN_AB:], ((0, 0), (0, 128 - N_GATE_AB))).astype(BF16)
    proj, gates = _norm_proj(x, mods[0], norm_g[0, 0], [w_main, w_gate])
    hg_consts = _hgrn_consts()
    gd_consts = _gdn_consts()
    mixed, new_hgrn = _hgrn(proj, hgrn_lb, gn_hgrn[0], hg_consts, True)
    mixed, = _hgrn(proj, hgrn_lb, gn_hgrn[0], hg_consts, False, s0=state_hgrn, mixed=mixed)
    gd_args = (gdn_conv[0], gdn_a_log[0], gdn_dt_bias[0], gn_gdn[0], gd_consts)
    mixed, new_gdn = _gdn(proj, gates, *gd_args, True, mixed)
    mixed, = _gdn(proj, gates, *gd_args, False, mixed, s0=state_gdn)
    x = _post_mixer(x, mixed, mods[0], norm_g[0, 1], w_out_ab[0].astype(BF16),
                    w_mlp1[0].astype(BF16), w_mlp2[0].astype(BF16))

    qkv, = _norm_proj(x, mods[1], norm_g[1, 0], [w_qkv_na[0].astype(BF16)])
    mixed, new_k, new_v = _ctx_attention(qkv, qn_na[0], kn_na[0])
    mixed = _na_attention(qkv, cache_na_k, cache_na_v, qn_na[0], kn_na[0], _na_bias_table(rpb_na[0]), mixed)
    x = _post_mixer(x, mixed, mods[1], norm_g[1, 1], w_out_na[0].astype(BF16),
                    w_mlp1[1].astype(BF16), w_mlp2[1].astype(BF16))

    return (x[:N_PROMPT].reshape(BATCH, SEQ, D_MODEL), x[N_PROMPT:].reshape(DEC_BATCH, DEC_SEQ, D_MODEL),
            new_hgrn, new_gdn, new_k, new_v)
```

```python
import functools

import numpy as np
import jax
import jax.numpy as jnp
from jax import lax
from jax.experimental import pallas as pl
from jax.experimental.pallas import tpu as pltpu

F32 = jnp.float32
BF16 = jnp.bfloat16

D_MODEL = 1024
BATCH = 16
SEQ = 256
DEC_BATCH = 4
DEC_SEQ = 1024
PAST_LEN = 256
N_PROMPT = BATCH * SEQ
N_SAMPLE = DEC_BATCH * DEC_SEQ
N_TOK = N_PROMPT + N_SAMPLE
GRID_W = 64
GRID_ROWS = DEC_SEQ // GRID_W
H_A = 4
DK_A = 128
DV_A = 128
H_B = 4
DK_B = 128
DV_B = 128
SHORT_CONV = 5
H_C = 16
HD_C = 64
KH = 8
KW = 16
D_FF = 4 * D_MODEL
EPS = 1e-6
NEG_INF = -1e30
N_MOD_ROWS = 8
D_MAIN_AB = 4608
N_GATE_AB = 16
CHUNK = 32
GBLK = 128
VMEM_LIMIT = 56 * 1024 * 1024


def _cparams(*sem):
    return pltpu.CompilerParams(dimension_semantics=sem, vmem_limit_bytes=VMEM_LIMIT)


def _sigmoid(x):
    return 1.0 / (1.0 + jnp.exp(-x))


def _silu(x):
    return x * _sigmoid(x)


def _dot(a, b):
    return jnp.dot(a, b, preferred_element_type=F32)


def _dot_nt(a, b):
    return lax.dot_general(a, b, (((1,), (1,)), ((), ())), preferred_element_type=F32)


def _dot_tn(a, b):
    return lax.dot_general(a, b, (((0,), (0,)), ((), ())), preferred_element_type=F32)


def _split2(x):
    hi = x.astype(BF16)
    lo = (x - hi.astype(F32)).astype(BF16)
    return hi, lo


def _dot_const(m2, x):
    hi, lo = _split2(x)
    return _dot(m2, jnp.concatenate([hi, lo], axis=0))


def _dot3(a, b):
    ah, al = _split2(a)
    bh, bl = _split2(b)
    return _dot(ah, bh) + (_dot(ah, bl) + _dot(al, bh))


def _mod_row(i, tm):
    start = i * tm
    return jnp.where(start < N_PROMPT, 0, 1 + (start - N_PROMPT) // DEC_SEQ)


def _mod_slice(mod_ref, row, k):
    return mod_ref[pl.ds(row, 1), k * D_MODEL:(k + 1) * D_MODEL]


def _norm_mod(x, g, sc, sh):
    ms = jnp.mean(x * x, axis=-1, keepdims=True)
    return (x * lax.rsqrt(ms + EPS) * g) * (1.0 + sc) + sh


def _mod_kernel(cond_ref, w_ref, b_ref, o_ref):
    s = _silu(cond_ref[...]).astype(BF16)
    o_ref[0] = _dot(s, w_ref[0].astype(BF16)) + b_ref[0]


def _modulation(cond8, ada_w, ada_b):
    depth = ada_w.shape[0]
    tn = 1024
    nj = ada_w.shape[2] // tn
    return pl.pallas_call(
        _mod_kernel,
        grid=(depth, nj),
        in_specs=[
            pl.BlockSpec((N_MOD_ROWS, D_MODEL), lambda l, j: (0, 0)),
            pl.BlockSpec((1, D_MODEL, tn), lambda l, j: (l, 0, j)),
            pl.BlockSpec((1, 1, tn), lambda l, j: (l, 0, j)),
        ],
        out_specs=pl.BlockSpec((1, N_MOD_ROWS, tn), lambda l, j: (l, 0, j)),
        out_shape=jax.ShapeDtypeStruct((depth, N_MOD_ROWS, ada_w.shape[2]), F32),
        compiler_params=_cparams("arbitrary", "arbitrary"),
        name="modulation",
    )(cond8, ada_w, ada_b.reshape(depth, 1, -1))


def _norm_proj_kernel(x_ref, mod_ref, g_ref, *refs, tm, n_w):
    w_refs, o_refs = refs[:n_w], refs[n_w:]
    row = _mod_row(pl.program_id(0), tm)
    h = _norm_mod(x_ref[...], g_ref[...], _mod_slice(mod_ref, row, 1), _mod_slice(mod_ref, row, 0)).astype(BF16)
    for w_ref, o_ref in zip(w_refs, o_refs):
        o_ref[...] = _dot(h, w_ref[...])


def _norm_proj(x, mod, g, ws, tm=256):
    n_w = len(ws)
    const = lambda i: (0, 0)
    return pl.pallas_call(
        functools.partial(_norm_proj_kernel, tm=tm, n_w=n_w),
        grid=(N_TOK // tm,),
        in_specs=[
            pl.BlockSpec((tm, D_MODEL), lambda i: (i, 0)),
            pl.BlockSpec(mod.shape, const),
            pl.BlockSpec((1, D_MODEL), const),
        ] + [pl.BlockSpec(w.shape, const, pipeline_mode=pl.Buffered(1)) for w in ws],
        out_specs=[pl.BlockSpec((tm, w.shape[1]), lambda i: (i, 0)) for w in ws],
        out_shape=[jax.ShapeDtypeStruct((N_TOK, w.shape[1]), F32) for w in ws],
        compiler_params=_cparams("arbitrary"),
        name="norm_proj",
    )(x, mod, g.reshape(1, D_MODEL), *ws)


def _post_kernel(x_ref, m_ref, mod_ref, g_ref, wo_ref, w1_ref, w2_ref, y_ref, *, tm, ff_chunk):
    row = _mod_row(pl.program_id(0), tm)
    mix = _dot(m_ref[...].astype(BF16), wo_ref[...])
    x1 = x_ref[...] + _mod_slice(mod_ref, row, 2) * mix
    h = _norm_mod(x1, g_ref[...], _mod_slice(mod_ref, row, 4), _mod_slice(mod_ref, row, 3)).astype(BF16)
    acc = jnp.zeros((tm, D_MODEL), F32)
    for k in range(0, D_FF, ff_chunk):
        a = jnp.maximum(_dot(h, w1_ref[:, k:k + ff_chunk]), 0.0)
        acc = acc + _dot((a * a).astype(BF16), w2_ref[k:k + ff_chunk, :])
    y_ref[...] = x1 + _mod_slice(mod_ref, row, 5) * acc


def _post_mixer(x, mixed, mod, g, wo, w1, w2, tm=256, ff_chunk=1024):
    const = lambda i: (0, 0)
    row_spec = pl.BlockSpec((tm, D_MODEL), lambda i: (i, 0))
    return pl.pallas_call(
        functools.partial(_post_kernel, tm=tm, ff_chunk=ff_chunk),
        grid=(N_TOK // tm,),
        in_specs=[
            row_spec,
            row_spec,
            pl.BlockSpec(mod.shape, const),
            pl.BlockSpec((1, D_MODEL), const),
            pl.BlockSpec(wo.shape, const, pipeline_mode=pl.Buffered(1)),
            pl.BlockSpec(w1.shape, const, pipeline_mode=pl.Buffered(1)),
            pl.BlockSpec(w2.shape, const, pipeline_mode=pl.Buffered(1)),
        ],
        out_specs=row_spec,
        out_shape=jax.ShapeDtypeStruct((N_TOK, D_MODEL), F32),
        compiler_params=_cparams("arbitrary"),
        name="post_mixer",
    )(x, mixed, mod, g.reshape(1, D_MODEL), wo, w1, w2)


PAIR = 2 * HD_C


def _pair_consts():
    lane = lax.broadcasted_iota(jnp.int32, (1, PAIR), 1)
    first = lane < HD_C
    ones_col = [jnp.where(lane == HD_C, 1.0, 0.0), jnp.where(lane == 0, 1.0, 0.0)]
    r = lax.broadcasted_iota(jnp.int32, (2 * PAIR, PAIR), 0) % PAIR
    cidx = lax.broadcasted_iota(jnp.int32, (2 * PAIR, PAIR), 1)
    mean2 = jnp.where(r // HD_C == cidx // HD_C, 1.0 / HD_C, 0.0).astype(BF16)
    return first, ones_col, mean2


def _pair_norm(x, w2, mean2):
    hi, lo = _split2(x * x)
    ms = _dot(jnp.concatenate([hi, lo], axis=1), mean2)
    return x * lax.rsqrt(ms + EPS) * w2


def _pair_queries(q, first):
    return [jnp.where(first, q, 0.0).astype(BF16), jnp.where(first, 0.0, q).astype(BF16)]


def _pair_values(v, first, ones_col):
    return [jnp.where(first, v, ones_col[0]).astype(BF16), jnp.where(first, ones_col[1], v).astype(BF16)]


def _pair_output(o_aug, first):
    den = [o_aug[0][:, HD_C:HD_C + 1], o_aug[1][:, 0:1]]
    return jnp.where(first, o_aug[0] / den[0], o_aug[1] / den[1])


def _row_max(*pieces):
    tiles = [p[:, i:i + 128] for p in pieces for i in range(0, p.shape[1], 128)]
    return jnp.max(functools.reduce(jnp.maximum, tiles), axis=-1, keepdims=True)


CTX_PAIRS = 2


def _ctx_attn_kernel(q_ref, k_ref, v_ref, qn_ref, kn_ref, o_ref, kc_ref, vc_ref):
    first, ones_col, mean2 = _pair_consts()
    lanes = [slice(p * PAIR, (p + 1) * PAIR) for p in range(CTX_PAIRS)]
    qn = [_pair_norm(q_ref[:, ln], qn_ref[...], mean2) * HD_C ** -0.5 for ln in lanes]
    kn = [_pair_norm(k_ref[:, ln], kn_ref[...], mean2) for ln in lanes]
    v = [v_ref[:, ln] for ln in lanes]
    for p in range(CTX_PAIRS):
        for j in range(2):
            kc_ref[0, 0, 2 * p + j] = kn[p][:, j * HD_C:(j + 1) * HD_C]
            vc_ref[0, 0, 2 * p + j] = v[p][:, j * HD_C:(j + 1) * HD_C]
    q = [_pair_queries(x, first) for x in qn]
    va = [_pair_values(x, first, ones_col) for x in v]
    kb = [x.astype(BF16) for x in kn]
    s = [[_dot_nt(q[p][j], kb[p]) for j in range(2)] for p in range(CTX_PAIRS)]
    pr = [[jnp.exp(x - _row_max(x)).astype(BF16) for x in sp] for sp in s]
    for p in range(CTX_PAIRS):
        o_ref[:, lanes[p]] = _pair_output([_dot(pr[p][j], va[p][j]) for j in range(2)], first)


def _ctx_attention(qkv, qn, kn):
    heads = 2 * CTX_PAIRS
    ng = H_C // heads
    wide = CTX_PAIRS * PAIR
    blk = lambda off: pl.BlockSpec((SEQ, wide), lambda b, p: (b, off + p))
    cache_spec = pl.BlockSpec((1, 1, heads, SEQ, HD_C), lambda b, p: (b, 0, p, 0, 0))
    cache_shape = jax.ShapeDtypeStruct((BATCH, 1, H_C, SEQ, HD_C), F32)
    return pl.pallas_call(
        _ctx_attn_kernel,
        grid=(BATCH, ng),
        in_specs=[blk(0), blk(ng), blk(2 * ng),
                  pl.BlockSpec((1, PAIR), lambda b, p: (0, 0)),
                  pl.BlockSpec((1, PAIR), lambda b, p: (0, 0))],
        out_specs=[pl.BlockSpec((SEQ, wide), lambda b, p: (b, p)), cache_spec, cache_spec],
        out_shape=[jax.ShapeDtypeStruct((N_TOK, D_MODEL), F32), cache_shape, cache_shape],
        compiler_params=_cparams("arbitrary", "arbitrary"),
        name="ctx_attention",
    )(qkv, qkv, qkv, jnp.tile(qn.reshape(1, HD_C), (1, 2)), jnp.tile(kn.reshape(1, HD_C), (1, 2)))


def _na_row_start(r):
    return min(max(r - KH // 2, 0), GRID_ROWS - KH)


NA_ROW_GROUP = 4


def _na_attn_kernel(q_ref, k_ref, v_ref, kc_ref, vc_ref, qn_ref, kn_ref, bias_ref, mixed_in_ref, o_ref, qs, ks, vs):
    del mixed_in_ref
    first, ones_col, mean2 = _pair_consts()
    q2 = _pair_queries(_pair_norm(q_ref[...], qn_ref[...], mean2) * HD_C ** -0.5, first)
    v2 = _pair_values(v_ref[...], first, ones_col)
    ks[...] = _pair_norm(k_ref[...], kn_ref[...], mean2).astype(BF16)
    for j in range(2):
        qs[j] = q2[j]
        vs[j] = v2[j]
    k_ctx = jnp.concatenate([kc_ref[0, 0, 0], kc_ref[0, 0, 1]], axis=1).astype(BF16)
    v_ctx = _pair_values(jnp.concatenate([vc_ref[0, 0, 0], vc_ref[0, 0, 1]], axis=1), first, ones_col)
    for r0 in range(0, GRID_ROWS, NA_ROW_GROUP):
        units = [(r, j) for r in range(r0, r0 + NA_ROW_GROUP) for j in range(2)]
        rows = {r: slice(r * GRID_W, (r + 1) * GRID_W) for r, _ in units}
        wins = {r: slice(_na_row_start(r) * GRID_W, (_na_row_start(r) + KH) * GRID_W) for r, _ in units}
        s_ctx_all = [_dot_nt(qs[j, r0 * GRID_W:(r0 + NA_ROW_GROUP) * GRID_W, :], k_ctx) for j in range(2)]
        s_ctx = [s_ctx_all[j][(r - r0) * GRID_W:(r - r0 + 1) * GRID_W] for r, j in units]
        s_win = []
        for r, j in units:
            dr0 = KH - 1 - (r - _na_row_start(r))
            lane0 = (dr0 - dr0 % 2) * GRID_W
            s_win.append(_dot_nt(qs[j, rows[r], :], ks[wins[r], :])
                         + bias_ref[j, dr0 % 2, :, lane0:lane0 + KH * GRID_W])
        m = [_row_max(a, b) for a, b in zip(s_win, s_ctx)]
        p_win = [jnp.exp(a - mm).astype(BF16) for a, mm in zip(s_win, m)]
        p_ctx = [jnp.exp(b - mm).astype(BF16) for b, mm in zip(s_ctx, m)]
        o_aug = [_dot(p_win[i], vs[j, wins[r], :]) + _dot(p_ctx[i], v_ctx[j]) for i, (r, j) in enumerate(units)]
        for i in range(0, len(units), 2):
            o_ref[rows[units[i][0]], :] = _pair_output(o_aug[i:i + 2], first)


NA_BIAS_LANES = 2 * KH * GRID_W


def _na_bias_table(rpb):
    qc = np.arange(GRID_W)[:, None]
    kc = np.arange(GRID_W)[None, :]
    cstart = np.clip(qc - KW // 2, 0, GRID_W - KW)
    valid = (kc >= cstart) & (kc < cstart + KW)
    onehot = ((kc - qc + KW - 1)[None] == np.arange(2 * KW - 1)[:, None, None]) & valid[None]
    t = jnp.einsum('hrd,dqk->hqrk', rpb.astype(F32), jnp.asarray(onehot, F32), precision=lax.Precision.HIGHEST)
    t = jnp.where(valid[None, :, None, :], t, NEG_INF)
    t = jnp.pad(t.reshape(H_C, GRID_W, (2 * KH - 1) * GRID_W), ((0, 0), (0, 0), (0, 2 * GRID_W)))
    return jnp.stack([t[:, :, :NA_BIAS_LANES], t[:, :, GRID_W:GRID_W + NA_BIAS_LANES]], axis=1)


def _na_attention(qkv, cache_k, cache_v, qn, kn, bias, mixed):
    nhp = H_C // 2
    row0 = N_PROMPT // DEC_SEQ
    blk = lambda off: pl.BlockSpec((DEC_SEQ, 2 * HD_C), lambda p, b: (row0 + b, off + p))
    cache_spec = pl.BlockSpec((1, 1, 2, PAST_LEN, HD_C), lambda p, b: (b, 0, p, 0, 0))
    return pl.pallas_call(
        _na_attn_kernel,
        grid=(nhp, DEC_BATCH),
        in_specs=[blk(0), blk(nhp), blk(2 * nhp), cache_spec, cache_spec,
                  pl.BlockSpec((1, PAIR), lambda p, b: (0, 0)),
                  pl.BlockSpec((1, PAIR), lambda p, b: (0, 0)),
                  pl.BlockSpec((2, 2, GRID_W, NA_BIAS_LANES), lambda p, b: (p, 0, 0, 0)),
                  pl.BlockSpec(memory_space=pl.ANY)],
        out_specs=pl.BlockSpec((DEC_SEQ, 2 * HD_C), lambda p, b: (row0 + b, p)),
        out_shape=jax.ShapeDtypeStruct((N_TOK, D_MODEL), F32),
        input_output_aliases={8: 0},
        scratch_shapes=[pltpu.VMEM((2, DEC_SEQ, PAIR), BF16), pltpu.VMEM((DEC_SEQ, PAIR), BF16),
                        pltpu.VMEM((2, DEC_SEQ, PAIR), BF16)],
        compiler_params=_cparams("arbitrary", "arbitrary"),
        name="na_attention",
    )(qkv, qkv, qkv, cache_k, cache_v, jnp.tile(qn.reshape(1, HD_C), (1, 2)), jnp.tile(kn.reshape(1, HD_C), (1, 2)),
      bias, mixed)


def _seq_layout(prompt):
    return (SEQ, BATCH, 0) if prompt else (DEC_SEQ, DEC_BATCH, N_PROMPT // DEC_SEQ)


def _flip_blocks(m, c):
    r, s = m.shape
    return m.reshape(r // c, c, s // c, c)[:, ::-1, :, ::-1].reshape(r, s)


def _rms_gate(x, gn, gate):
    ms = jnp.mean(x * x, axis=-1, keepdims=True)
    return x * lax.rsqrt(ms + EPS) * gn * _silu(gate)


HG_LEVELS = tuple(CHUNK >> (i + 1) for i in range(CHUNK.bit_length() - 1))
HG_NL = len(HG_LEVELS)
HG_STACK = (HG_NL + 1) * CHUNK
TOT_ROWS = 16
HG_ROWS = (HG_NL + 2) * CHUNK + TOT_ROWS


def _hgrn_consts():
    c = CHUNK
    level_rows = []
    mask = np.zeros((HG_STACK, HG_STACK), np.float32)
    mask[:c, :c] = np.eye(c)
    for li, b in enumerate(HG_LEVELS):
        m = np.zeros((c, c), np.float32)
        blk = np.zeros((c, c), np.float32)
        for t in range(c):
            mid = (t // (2 * b)) * 2 * b + b
            if t >= mid:
                m[t, mid:t + 1] = 1.0
                blk[t, mid - b:mid] = 1.0
            else:
                m[t, t + 1:mid] = 1.0
        level_rows.append(m)
        mask[(li + 1) * c:(li + 2) * c, (li + 1) * c:(li + 2) * c] = blk
    dq = np.tril(np.ones((c, c), np.float32))
    dk = np.triu(np.ones((c, c), np.float32), 1)
    body = np.concatenate(level_rows + [dq, dk], axis=0)
    tot = np.ones((TOT_ROWS, c), np.float32)
    mcs, masks = [], []
    for reverse in (False, True):
        bm = _flip_blocks(body, c) if reverse else body
        mk = _flip_blocks(mask, c) if reverse else mask
        mc = np.concatenate([bm, tot], axis=0)
        mcs.append(np.concatenate([mc, mc], axis=1))
        masks.append(mk)
    return jnp.asarray(np.stack(mcs), BF16), jnp.asarray(np.stack(masks), F32)


def _hgrn_kernel(*refs, seq, has_s0, emit_state, aliased):
    it = iter(refs)
    qa_ref, ff_ref, fb_ref, ia_ref, ga_ref, lb_ref, gn_ref, mc_ref, mask_ref = [next(it) for _ in range(9)]
    s0_ref = next(it) if has_s0 else None
    if aliased:
        next(it)
    o_ref = next(it)
    st_ref = next(it) if emit_state else None
    s_scr, acc = next(it), next(it)
    c = CHUNK
    n_chunks = seq // c

    lb_raw = lb_ref[...]
    lb_e = jnp.exp(lb_raw - jnp.max(lb_raw, axis=0, keepdims=True))
    lb_all = lb_e[0:1] / jnp.sum(lb_e, axis=0, keepdims=True)

    for d in range(2):
        for h in range(H_A):
            s_scr[d, h] = s0_ref[0, 0, d, h].T if has_s0 else jnp.zeros((DV_A, DK_A), F32)
    acc[...] = jnp.zeros(acc.shape, F32)

    def body(n, carry):
        combos = [(d, h) for d in range(2) for h in range(H_A)]
        lanes = [slice(h * DK_A, (h + 1) * DK_A) for h in range(H_A)]
        rows = [pl.ds(pl.multiple_of((n if d == 0 else n_chunks - 1 - n) * c, c), c) for d in range(2)]
        f_all = [lb_all + (1.0 - lb_all) * _sigmoid((ff_ref, fb_ref)[d][rows[d], :]) for d in range(2)]
        e_all = [jnp.exp(_dot_const(mc_ref[d], jnp.log(f_all[d]))) for d in range(2)]
        q_all = [_silu(qa_ref[rows[d], :]) * DK_A ** -0.5 for d in range(2)]
        v_all = [ia_ref[rows[d], :].astype(BF16) for d in range(2)]
        st = [s_scr[d, h] for d, h in combos]
        qs, ks, vs, es = [], [], [], []
        for d, h in combos:
            qs.append(q_all[d][:, lanes[h]])
            ks.append(1.0 - f_all[d][:, lanes[h]])
            vs.append(v_all[d][:, lanes[h]])
            es.append(e_all[d][:, lanes[h]])
        lvl = [[e[i * c:(i + 1) * c] for i in range(HG_NL + 2)] for e in es]
        qst = [jnp.concatenate([q] + [q * l[i] for i in range(HG_NL)], axis=0).astype(BF16) for q, l in zip(qs, lvl)]
        kst = [jnp.concatenate([k] + [k * l[i] for i in range(HG_NL)], axis=0).astype(BF16) for k, l in zip(ks, lvl)]
        r = [(_dot_nt(qst[i], kst[i]) * mask_ref[d]).astype(BF16) for i, (d, h) in enumerate(combos)]
        ost = [_dot(r[i], jnp.concatenate([vs[i]] * (HG_NL + 1), axis=0)) for i in range(len(combos))]
        inter = [_dot_nt((qs[i] * lvl[i][HG_NL]).astype(BF16), st[i].astype(BF16)) for i in range(len(combos))]
        upd = [_dot_tn(vs[i], (ks[i] * lvl[i][HG_NL + 1]).astype(BF16)) for i in range(len(combos))]
        o = [functools.reduce(lambda a, b: a + b, [ost[i][j * c:(j + 1) * c] for j in range(HG_NL + 1)]) + inter[i]
             for i in range(len(combos))]
        for d in range(2):
            acc[rows[d], :] += jnp.concatenate(o[d * H_A:(d + 1) * H_A], axis=1)
        for i, (d, h) in enumerate(combos):
            e_tot = es[i][(HG_NL + 2) * c:(HG_NL + 2) * c + 1]
            s_scr[d, h] = st[i] * e_tot + upd[i]
        return carry

    lax.fori_loop(0, n_chunks, body, 0)

    for h in range(H_A):
        ln = slice(h * DV_A, (h + 1) * DV_A)
        o_ref[:, ln] = _rms_gate(acc[:, ln], gn_ref[...], ga_ref[:, ln])
    if emit_state:
        for d in range(2):
            for h in range(H_A):
                st_ref[0, 0, d, h] = s_scr[d, h].T


def _hgrn(proj, hgrn_lb, gn, consts, prompt, s0=None, layer=0, mixed=None):
    seq, nb, rb0 = _seq_layout(prompt)
    mc2, mask = consts
    wa = H_A * DK_A
    blk = lambda j: pl.BlockSpec((seq, wa), lambda b: (rb0 + b, j))
    const2 = lambda b: (0, 0)
    const3 = lambda b: (0, 0, 0)
    st_block = (1, 1, 2, H_A, DK_A, DV_A)
    in_specs = [blk(0), blk(1), blk(2), blk(3), blk(4),
                pl.BlockSpec(hgrn_lb.shape, const2), pl.BlockSpec((1, DV_A), const2),
                pl.BlockSpec(mc2.shape, const3), pl.BlockSpec(mask.shape, const3)]
    args = [proj] * 5 + [hgrn_lb, gn.reshape(1, DV_A), mc2, mask]
    if s0 is not None:
        in_specs.append(pl.BlockSpec(st_block, lambda b: (b, layer, 0, 0, 0, 0)))
        args.append(s0)
    aliases = {}
    if mixed is not None:
        aliases = {len(args): 0}
        in_specs.append(pl.BlockSpec(memory_space=pl.ANY))
        args.append(mixed)
    out_specs = [pl.BlockSpec((seq, wa), lambda b: (rb0 + b, 0))]
    out_shape = [jax.ShapeDtypeStruct((N_TOK, D_MODEL), F32)]
    if prompt:
        out_specs.append(pl.BlockSpec(st_block, lambda b: (b, 0, 0, 0, 0, 0)))
        out_shape.append(jax.ShapeDtypeStruct((nb, 1, 2, H_A, DK_A, DV_A), F32))
    return pl.pallas_call(
        functools.partial(_hgrn_kernel, seq=seq, has_s0=s0 is not None, emit_state=prompt, aliased=mixed is not None),
        grid=(nb,),
        in_specs=in_specs,
        out_specs=out_specs,
        out_shape=out_shape,
        input_output_aliases=aliases,
        scratch_shapes=[pltpu.VMEM((2, H_A, DV_A, DK_A), F32), pltpu.VMEM((seq, wa), F32)],
        compiler_params=_cparams("arbitrary"),
        name="hgrn_prompt" if prompt else "hgrn_sample",
    )(*args)


GD_SUB = GBLK // CHUNK
GD_ROWS = 2 * GBLK + TOT_ROWS


def _gdn_consts():
    n, c = GBLK, CHUNK
    same = (np.arange(n)[:, None] // c) == (np.arange(n)[None, :] // c)
    tri = (same & (np.arange(n)[None, :] <= np.arange(n)[:, None])).astype(np.float32)
    sup = (same & (np.arange(n)[None, :] > np.arange(n)[:, None])).astype(np.float32)
    tot = np.zeros((TOT_ROWS, n), np.float32)
    for s in range(GD_SUB):
        tot[s, s * c:(s + 1) * c] = 1.0
    mgs, tts, tris = [], [], []
    for reverse in (False, True):
        t = _flip_blocks(tri, c) if reverse else tri
        s = _flip_blocks(sup, c) if reverse else sup
        mg = np.concatenate([t, s, tot], axis=0)
        mgs.append(np.concatenate([mg, mg], axis=1))
        tts.append(np.concatenate([t.T, t.T], axis=0))
        tris.append(t)
    tris.append(same.astype(np.float32))
    return jnp.asarray(np.stack(mgs), BF16), jnp.asarray(np.stack(tts), BF16), jnp.asarray(np.stack(tris), F32)


def _softplus(x):
    return jnp.maximum(x, 0.0) + jnp.log(1.0 + jnp.exp(-jnp.abs(x)))


def _conv_silu(x, w, seq):
    t_idx = lax.broadcasted_iota(jnp.int32, (seq, 1), 0)
    half = SHORT_CONV // 2
    acc = x * w[half:half + 1]
    for j in range(SHORT_CONV):
        shift = half - j
        if shift == 0:
            continue
        src = t_idx - shift
        xr = pltpu.roll(x, shift % seq, axis=0)
        acc = acc + jnp.where((src >= 0) & (src < seq), xr, 0.0) * w[j:j + 1]
    return _silu(acc)


def _l2norm_heads(x, n_heads, width, scale):
    outs = []
    for h in range(n_heads):
        xh = x[:, h * width:(h + 1) * width]
        outs.append(xh * (lax.rsqrt(jnp.sum(xh * xh, axis=-1, keepdims=True) + EPS) * scale))
    return jnp.concatenate(outs, axis=-1)


def _gdn_kernel(*refs, seq, has_s0, emit_state, aliased):
    it = iter(refs)
    (q_ref, k_ref, v_ref, gb_ref, gate_ref, cw_ref, alog_ref, dt_ref, gn_ref,
     mg_ref, tt_ref, tri_ref) = [next(it) for _ in range(12)]
    s0_ref = next(it) if has_s0 else None
    if aliased:
        next(it)
    o_ref = next(it)
    st_ref = next(it) if emit_state else None
    qn, kn, vn, u_s, w_s, qg_s, kdt_s, at_s, et_s, s_scr, acc = [next(it) for _ in range(11)]
    c = CHUNK
    n_chunks = seq // c
    n_blocks = seq // GBLK
    wq = H_B * DK_B
    n_dh = 2 * H_B
    combos = [(d, h) for d in range(2) for h in range(H_B)]
    lanes = [slice(h * DK_B, (h + 1) * DK_B) for h in range(H_B)]

    qn[...] = _l2norm_heads(_conv_silu(q_ref[...], cw_ref[:, 0:wq], seq), H_B, DK_B, DK_B ** -0.5)
    kn[...] = _l2norm_heads(_conv_silu(k_ref[...], cw_ref[:, wq:2 * wq], seq), H_B, DK_B, 1.0)
    vn[...] = _conv_silu(v_ref[...], cw_ref[:, 2 * wq:3 * wq], seq)
    for i in range(2 * H_B):
        s_scr[i] = s0_ref[0, 0, i // H_B, i % H_B] if has_s0 else jnp.zeros((DK_B, DV_B), F32)
    acc[...] = jnp.zeros(acc.shape, F32)

    eye = (lax.broadcasted_iota(jnp.int32, (GBLK, GBLK), 0)
           == lax.broadcasted_iota(jnp.int32, (GBLK, GBLK), 1)).astype(F32)
    eye_pk = (lax.broadcasted_iota(jnp.int32, (c, GBLK), 0)
              == lax.broadcasted_iota(jnp.int32, (c, GBLK), 1) % c).astype(F32)
    bwd_lane = lax.broadcasted_iota(jnp.int32, (1, 128), 1) % n_dh >= H_B
    add = lambda a, b: a + b

    def expand(pk):
        return jnp.concatenate([pk] * GD_SUB, axis=0) * tri_ref[2]

    def pack(bd):
        return functools.reduce(add, [bd[s * c:(s + 1) * c] for s in range(GD_SUB)])

    def dot3_split(a, b_hi, b_lo):
        ah, al = _split2(a)
        t = _dot(jnp.concatenate([ah, al], axis=0), b_hi)
        return t[:a.shape[0]] + t[a.shape[0]:] + _dot(ah, b_lo)

    def block_body(blk, carry):
        rows = pl.ds(pl.multiple_of(blk * GBLK, GBLK), GBLK)
        gates = gate_ref[rows, :]
        glog_all = -jnp.exp(alog_ref[...]) * _softplus(gates + dt_ref[...])
        beta_all = _sigmoid(gates)
        g2 = jnp.concatenate(_split2(glog_all), axis=0)
        dg = [_dot(mg_ref[d], g2) for d in range(2)]
        dsel = jnp.where(bwd_lane, dg[1], dg[0])
        eg_all = jnp.exp(dsel)
        g_all = dsel[:GBLK]
        gt = [_dot_tn(g2, tt_ref[d]) for d in range(2)]
        qs = [qn[rows, ln] for ln in lanes]
        ks = [kn[rows, ln] for ln in lanes]
        vs = [vn[rows, ln] for ln in lanes]
        col = lambda x, j: jnp.broadcast_to(x[:, j:j + 1], (GBLK, DK_B))
        betas = [col(beta_all, n_dh + i) for i in range(n_dh)]
        kbs = [ks[h] * betas[i] for i, (d, h) in enumerate(combos)]
        kk = [_dot_nt(jnp.concatenate([qs[h], kbs[h], kbs[H_B + h]], axis=0).astype(BF16), ks[h].astype(BF16))
              for h in range(H_B)]
        decay = []
        for i, (d, h) in enumerate(combos):
            inside = tri_ref[d] > 0.0
            gd = col(g_all, i) - gt[d][i:i + 1, :]
            decay.append(jnp.where(inside, jnp.exp(jnp.where(inside, gd, 0.0)), 0.0))
        attn = [kk[h][:GBLK] * decay[i] for i, (d, h) in enumerate(combos)]
        p_pk = [pack(kk[h][(1 + d) * GBLK:(2 + d) * GBLK] * decay[i] * (1.0 - eye)) for i, (d, h) in enumerate(combos)]
        x_pk = [eye_pk - p for p in p_pk]
        p_bd = [_split2(expand(p)) for p in p_pk]
        for _ in range(CHUNK.bit_length() - 2):
            p_pk = [dot3_split(p, *b) for p, b in zip(p_pk, p_bd)]
            p_bd = [_split2(expand(p)) for p in p_pk]
            x_pk = [x + dot3_split(x, *b) for x, b in zip(x_pk, p_bd)]
        eg_col = [col(eg_all[:GBLK], i) for i in range(n_dh)]
        ekd_col = [col(eg_all[GBLK:2 * GBLK], i) for i in range(n_dh)]
        rhs = [jnp.concatenate([vs[h] * betas[i], kbs[i] * eg_col[i]], axis=1) for i, (d, h) in enumerate(combos)]
        uw = [_dot3(expand(x_pk[i]), rhs[i]) for i in range(n_dh)]
        for i, (d, h) in enumerate(combos):
            qg = (qs[h] * eg_col[i]).astype(BF16)
            kdt_s[i, blk] = (ks[h] * ekd_col[i]).T.astype(BF16)
            for s in range(GD_SUB):
                cn = blk * GD_SUB + s
                r = slice(s * c, (s + 1) * c)
                u_s[i, cn] = uw[i][r, :DV_B]
                w_s[i, cn] = uw[i][r, DV_B:].astype(BF16)
                qg_s[i, cn] = qg[r]
                at_s[i, cn] = attn[i][r].astype(BF16)
                et_s[i, cn] = jnp.broadcast_to(eg_all[2 * GBLK + s:2 * GBLK + s + 1, i:i + 1], (8, DV_B))
        return carry

    lax.fori_loop(0, n_blocks, block_body, 0)

    def chunk_body(n, carry):
        cns = [n, n_chunks - 1 - n]
        rows = [pl.ds(pl.multiple_of(cn * c, c), c) for cn in cns]
        sub_of_row = lax.broadcasted_iota(jnp.int32, (GBLK, 1), 0) // c
        in_chunk = [sub_of_row == cn % GD_SUB for cn in cns]
        st = [s_scr[i] for i in range(n_dh)]
        ws = [_dot(jnp.concatenate([w_s[i, cns[d]], qg_s[i, cns[d]]], axis=0), st[i].astype(BF16))
              for i, (d, h) in enumerate(combos)]
        vblk = [jnp.where(in_chunk[d], jnp.concatenate([u_s[i, cns[d]] - ws[i][:c]] * GD_SUB, axis=0), 0.0).astype(BF16)
                for i, (d, h) in enumerate(combos)]
        r = [_dot(jnp.concatenate([at_s[i, cns[d]], kdt_s[i, cns[d] // GD_SUB]], axis=0), vblk[i])
             for i, (d, h) in enumerate(combos)]
        for d in range(2):
            acc[rows[d], :] += jnp.concatenate([ws[i][c:] + r[i][:c] for i in range(d * H_B, (d + 1) * H_B)], axis=1)
        for i, (d, h) in enumerate(combos):
            s_scr[i] = st[i] * et_s[i, cns[d]][0:1] + r[i][c:]
        return carry

    lax.fori_loop(0, n_chunks, chunk_body, 0)

    for h in range(H_B):
        ln = slice(h * DV_B, (h + 1) * DV_B)
        o_ref[:, ln] = _rms_gate(acc[:, ln], gn_ref[...], gb_ref[:, ln])
    if emit_state:
        for i in range(2 * H_B):
            st_ref[0, 0, i // H_B, i % H_B] = s_scr[i]


def _gdn(proj, gates, conv_w, a_log, dt_bias, gn, consts, prompt, mixed, s0=None, layer=0):
    seq, nb, rb0 = _seq_layout(prompt)
    n_chunks = seq // CHUNK
    wq = H_B * DK_B
    blk = lambda j: pl.BlockSpec((seq, wq), lambda b: (rb0 + b, j))
    const2 = lambda b: (0, 0)
    const3 = lambda b: (0, 0, 0)
    st_block = (1, 1, 2, H_B, DK_B, DV_B)
    pad_row = lambda p: jnp.pad(p.reshape(1, -1).astype(F32), ((0, 0), (0, 128 - p.size)))
    in_specs = [blk(5), blk(6), blk(7), blk(8),
                pl.BlockSpec((seq, 128), lambda b: (rb0 + b, 0)),
                pl.BlockSpec((SHORT_CONV, 3 * wq), const2),
                pl.BlockSpec((1, 128), const2), pl.BlockSpec((1, 128), const2), pl.BlockSpec((1, DV_B), const2)]
    in_specs += [pl.BlockSpec(m.shape, const3) for m in consts]
    args = [proj] * 4 + [gates, conv_w.reshape(SHORT_CONV, 3 * wq), pad_row(a_log), pad_row(dt_bias),
                         gn.reshape(1, DV_B)] + list(consts)
    if s0 is not None:
        in_specs.append(pl.BlockSpec(st_block, lambda b: (b, layer, 0, 0, 0, 0)))
        args.append(s0)
    aliases = {len(args): 0}
    in_specs.append(pl.BlockSpec(memory_space=pl.ANY))
    args.append(mixed)
    out_specs = [pl.BlockSpec((seq, wq), lambda b: (rb0 + b, 1))]
    out_shape = [jax.ShapeDtypeStruct((N_TOK, D_MODEL), F32)]
    if prompt:
        out_specs.append(pl.BlockSpec(st_block, lambda b: (b, 0, 0, 0, 0, 0)))
        out_shape.append(jax.ShapeDtypeStruct((nb, 1, 2, H_B, DK_B, DV_B), F32))
    n_dh = 2 * H_B
    scratch = ([pltpu.VMEM((seq, wq), F32)] * 3
               + [pltpu.VMEM((n_dh, n_chunks, CHUNK, DV_B), F32)]
               + [pltpu.VMEM((n_dh, n_chunks, CHUNK, DK_B), BF16)] * 2
               + [pltpu.VMEM((n_dh, seq // GBLK, DK_B, GBLK), BF16),
                  pltpu.VMEM((n_dh, n_chunks, CHUNK, GBLK), BF16),
                  pltpu.VMEM((n_dh, n_chunks, 8, DV_B), F32),
                  pltpu.VMEM((n_dh, DK_B, DV_B), F32),
                  pltpu.VMEM((seq, wq), F32)])
    return pl.pallas_call(
        functools.partial(_gdn_kernel, seq=seq, has_s0=s0 is not None, emit_state=prompt, aliased=True),
        grid=(nb,),
        in_specs=in_specs,
        out_specs=out_specs,
        out_shape=out_shape,
        input_output_aliases=aliases,
        scratch_shapes=scratch,
        compiler_params=_cparams("arbitrary"),
        name="gdn_prompt" if prompt else "gdn_sample",
    )(*args)


def kernel(x_prompt, x_sample, state_hgrn, state_gdn, cache_na_k, cache_na_v, c, c_ctx, ada_w, ada_b, norm_g, w_in_ab, w_out_ab, hgrn_lb, gdn_conv, gdn_a_log, gdn_dt_bias, gn_hgrn, gn_gdn, w_qkv_na, qn_na, kn_na, rpb_na, w_out_na, w_mlp1, w_mlp2):
    cond = jnp.concatenate([c_ctx[None, :], c, jnp.zeros((N_MOD_ROWS - 1 - DEC_BATCH, D_MODEL), F32)], axis=0)
    mods = _modulation(cond, ada_w, ada_b)
    x = jnp.concatenate([x_prompt.reshape(N_PROMPT, D_MODEL), x_sample.reshape(N_SAMPLE, D_MODEL)], axis=0)

    w_in = w_in_ab[0]
    w_main = w_in[:, :D_MAIN_AB].astype(BF16)
    w_gate = jnp.pad(w_in[:, D_MAIN_AB:], ((0, 0), (0, 128 - N_GATE_AB))).astype(BF16)
    proj, gates = _norm_proj(x, mods[0], norm_g[0, 0], [w_main, w_gate])
    hg_consts = _hgrn_consts()
    gd_consts = _gdn_consts()
    mixed, new_hgrn = _hgrn(proj, hgrn_lb, gn_hgrn[0], hg_consts, True)
    mixed, = _hgrn(proj, hgrn_lb, gn_hgrn[0], hg_consts, False, s0=state_hgrn, mixed=mixed)
    gd_args = (gdn_conv[0], gdn_a_log[0], gdn_dt_bias[0], gn_gdn[0], gd_consts)
    mixed, new_gdn = _gdn(proj, gates, *gd_args, True, mixed)
    mixed, = _gdn(proj, gates, *gd_args, False, mixed, s0=state_gdn)
    x = _post_mixer(x, mixed, mods[0], norm_g[0, 1], w_out_ab[0].astype(BF16),
                    w_mlp1[0].astype(BF16), w_mlp2[0].astype(BF16))

    qkv, = _norm_proj(x, mods[1], norm_g[1, 0], [w_qkv_na[0].astype(BF16)])
    mixed, new_k, new_v = _ctx_attention(qkv, qn_na[0], kn_na[0])
    mixed = _na_attention(qkv, cache_na_k, cache_na_v, qn_na[0], kn_na[0], _na_bias_table(rpb_na[0]), mixed)
    x = _post_mixer(x, mixed, mods[1], norm_g[1, 1], w_out_na[0].astype(BF16),
                    w_mlp1[1].astype(BF16), w_mlp2[1].astype(BF16))

    return (x[:N_PROMPT].reshape(BATCH, SEQ, D_MODEL), x[N_PROMPT:].reshape(DEC_BATCH, DEC_SEQ, D_MODEL),
            new_hgrn, new_gdn, new_k, new_v)
```

```python
import functools

import numpy as np
import jax
import jax.numpy as jnp
from jax import lax
from jax.experimental import pallas as pl
from jax.experimental.pallas import tpu as pltpu

F32 = jnp.float32
BF16 = jnp.bfloat16

D_MODEL = 1024
BATCH = 16
SEQ = 256
DEC_BATCH = 4
DEC_SEQ = 1024
PAST_LEN = 256
N_PROMPT = BATCH * SEQ
N_SAMPLE = DEC_BATCH * DEC_SEQ
N_TOK = N_PROMPT + N_SAMPLE
GRID_W = 64
GRID_ROWS = DEC_SEQ // GRID_W
H_A = 4
DK_A = 128
DV_A = 128
H_B = 4
DK_B = 128
DV_B = 128
SHORT_CONV = 5
H_C = 16
HD_C = 64
KH = 8
KW = 16
D_FF = 4 * D_MODEL
EPS = 1e-6
NEG_INF = -1e30
N_MOD_ROWS = 8
D_MAIN_AB = 4608
N_GATE_AB = 16
CHUNK = 32
GBLK = 128
VMEM_LIMIT = 56 * 1024 * 1024


def _cparams(*sem):
    return pltpu.CompilerParams(dimension_semantics=sem, vmem_limit_bytes=VMEM_LIMIT)


def _sigmoid(x):
    return 1.0 / (1.0 + jnp.exp(-x))


def _silu(x):
    return x * _sigmoid(x)


def _dot(a, b):
    return jnp.dot(a, b, preferred_element_type=F32)


def _dot_nt(a, b):
    return lax.dot_general(a, b, (((1,), (1,)), ((), ())), preferred_element_type=F32)


def _dot_tn(a, b):
    return lax.dot_general(a, b, (((0,), (0,)), ((), ())), preferred_element_type=F32)


def _split2(x):
    hi = x.astype(BF16)
    lo = (x - hi.astype(F32)).astype(BF16)
    return hi, lo


def _dot_const(m2, x):
    hi, lo = _split2(x)
    return _dot(m2, jnp.concatenate([hi, lo], axis=0))


def _dot3(a, b):
    ah, al = _split2(a)
    bh, bl = _split2(b)
    return _dot(ah, bh) + (_dot(ah, bl) + _dot(al, bh))


def _mod_row(i, tm):
    start = i * tm
    return jnp.where(start < N_PROMPT, 0, 1 + (start - N_PROMPT) // DEC_SEQ)


def _mod_slice(mod_ref, row, k):
    return mod_ref[pl.ds(row, 1), k * D_MODEL:(k + 1) * D_MODEL]


def _norm_mod(x, g, sc, sh):
    ms = jnp.mean(x * x, axis=-1, keepdims=True)
    return (x * lax.rsqrt(ms + EPS) * g) * (1.0 + sc) + sh


def _mod_kernel(cond_ref, w_ref, b_ref, o_ref):
    s = _silu(cond_ref[...]).astype(BF16)
    o_ref[0] = _dot(s, w_ref[0].astype(BF16)) + b_ref[0]


def _modulation(cond8, ada_w, ada_b):
    depth = ada_w.shape[0]
    tn = 1024
    nj = ada_w.shape[2] // tn
    return pl.pallas_call(
        _mod_kernel,
        grid=(depth, nj),
        in_specs=[
            pl.BlockSpec((N_MOD_ROWS, D_MODEL), lambda l, j: (0, 0)),
            pl.BlockSpec((1, D_MODEL, tn), lambda l, j: (l, 0, j)),
            pl.BlockSpec((1, 1, tn), lambda l, j: (l, 0, j)),
        ],
        out_specs=pl.BlockSpec((1, N_MOD_ROWS, tn), lambda l, j: (l, 0, j)),
        out_shape=jax.ShapeDtypeStruct((depth, N_MOD_ROWS, ada_w.shape[2]), F32),
        compiler_params=_cparams("arbitrary", "arbitrary"),
        name="modulation",
    )(cond8, ada_w, ada_b.reshape(depth, 1, -1))


def _stream_specs(n_arrays, tm):
    if n_arrays == 1:
        return [pl.BlockSpec((tm, D_MODEL), lambda i: (i, 0))]
    npt = N_PROMPT // tm
    return [pl.BlockSpec((tm, D_MODEL), lambda i: (jnp.minimum(i, npt - 1), 0)),
            pl.BlockSpec((tm, D_MODEL), lambda i: (jnp.maximum(i - npt, 0), 0))]


def _stream_load(x_refs, tm):
    if len(x_refs) == 1:
        return x_refs[0][...]
    return jnp.where(pl.program_id(0) < N_PROMPT // tm, x_refs[0][...], x_refs[1][...])


def _norm_proj_kernel(*refs, tm, n_x, n_w):
    x_refs, (mod_ref, g_ref) = refs[:n_x], refs[n_x:n_x + 2]
    w_refs, o_refs = refs[n_x + 2:n_x + 2 + n_w], refs[n_x + 2 + n_w:]
    row = _mod_row(pl.program_id(0), tm)
    h = _norm_mod(_stream_load(x_refs, tm), g_ref[...], _mod_slice(mod_ref, row, 1), _mod_slice(mod_ref, row, 0)).astype(BF16)
    for w_ref, o_ref in zip(w_refs, o_refs):
        o_ref[...] = _dot(h, w_ref[...])


def _norm_proj(xs, mod, g, ws, tm=256):
    n_w = len(ws)
    const = lambda i: (0, 0)
    return pl.pallas_call(
        functools.partial(_norm_proj_kernel, tm=tm, n_x=len(xs), n_w=n_w),
        grid=(N_TOK // tm,),
        in_specs=_stream_specs(len(xs), tm) + [
            pl.BlockSpec(mod.shape, const),
            pl.BlockSpec((1, D_MODEL), const),
        ] + [pl.BlockSpec(w.shape, const, pipeline_mode=pl.Buffered(1)) for w in ws],
        out_specs=[pl.BlockSpec((tm, w.shape[1]), lambda i: (i, 0)) for w in ws],
        out_shape=[jax.ShapeDtypeStruct((N_TOK, w.shape[1]), F32) for w in ws],
        compiler_params=_cparams("arbitrary"),
        name="norm_proj",
    )(*xs, mod, g.reshape(1, D_MODEL), *ws)


def _post_kernel(*refs, tm, ff_chunk, n_x, n_y):
    x_refs = refs[:n_x]
    m_ref, mod_ref, g_ref, wo_ref, w1_ref, w2_ref = refs[n_x:n_x + 6]
    y_refs = refs[n_x + 6:]
    row = _mod_row(pl.program_id(0), tm)
    mix = _dot(m_ref[...].astype(BF16), wo_ref[...])
    x1 = _stream_load(x_refs, tm) + _mod_slice(mod_ref, row, 2) * mix
    h = _norm_mod(x1, g_ref[...], _mod_slice(mod_ref, row, 4), _mod_slice(mod_ref, row, 3)).astype(BF16)
    acc = jnp.zeros((tm, D_MODEL), F32)
    for k in range(0, D_FF, ff_chunk):
        a = jnp.maximum(_dot(h, w1_ref[:, k:k + ff_chunk]), 0.0)
        acc = acc + _dot((a * a).astype(BF16), w2_ref[k:k + ff_chunk, :])
    y = x1 + _mod_slice(mod_ref, row, 5) * acc
    if n_y == 1:
        y_refs[0][...] = y
    else:
        is_prompt = pl.program_id(0) < N_PROMPT // tm

        @pl.when(is_prompt)
        def _():
            y_refs[0][...] = y

        @pl.when(jnp.logical_not(is_prompt))
        def _():
            y_refs[1][...] = y


def _post_mixer(xs, mixed, mod, g, wo, w1, w2, split_out=False, tm=512, ff_chunk=1024):
    const = lambda i: (0, 0)
    n_y = 2 if split_out else 1
    rows = (N_PROMPT, N_SAMPLE) if split_out else (N_TOK,)
    out = pl.pallas_call(
        functools.partial(_post_kernel, tm=tm, ff_chunk=ff_chunk, n_x=len(xs), n_y=n_y),
        grid=(N_TOK // tm,),
        in_specs=_stream_specs(len(xs), tm) + [
            pl.BlockSpec((tm, D_MODEL), lambda i: (i, 0)),
            pl.BlockSpec(mod.shape, const),
            pl.BlockSpec((1, D_MODEL), const),
            pl.BlockSpec(wo.shape, const, pipeline_mode=pl.Buffered(1)),
            pl.BlockSpec(w1.shape, const, pipeline_mode=pl.Buffered(1)),
            pl.BlockSpec(w2.shape, const, pipeline_mode=pl.Buffered(1)),
        ],
        out_specs=_stream_specs(n_y, tm),
        out_shape=[jax.ShapeDtypeStruct((r, D_MODEL), F32) for r in rows],
        compiler_params=_cparams("arbitrary"),
        name="post_mixer",
    )(*xs, mixed, mod, g.reshape(1, D_MODEL), wo, w1, w2)
    return tuple(out)


PAIR = 2 * HD_C


def _pair_consts():
    lane = lax.broadcasted_iota(jnp.int32, (1, PAIR), 1)
    first = lane < HD_C
    ones_col = [jnp.where(lane == HD_C, 1.0, 0.0), jnp.where(lane == 0, 1.0, 0.0)]
    r = lax.broadcasted_iota(jnp.int32, (2 * PAIR, PAIR), 0) % PAIR
    cidx = lax.broadcasted_iota(jnp.int32, (2 * PAIR, PAIR), 1)
    mean2 = jnp.where(r // HD_C == cidx // HD_C, 1.0 / HD_C, 0.0).astype(BF16)
    return first, ones_col, mean2


def _pair_norm(x, w2, mean2):
    hi, lo = _split2(x * x)
    ms = _dot(jnp.concatenate([hi, lo], axis=1), mean2)
    return x * lax.rsqrt(ms + EPS) * w2


def _pair_queries(q, first):
    return [jnp.where(first, q, 0.0).astype(BF16), jnp.where(first, 0.0, q).astype(BF16)]


def _pair_values(v, first, ones_col):
    return [jnp.where(first, v, ones_col[0]).astype(BF16), jnp.where(first, ones_col[1], v).astype(BF16)]


def _pair_output(o_aug, first):
    den = [o_aug[0][:, HD_C:HD_C + 1], o_aug[1][:, 0:1]]
    return jnp.where(first, o_aug[0] / den[0], o_aug[1] / den[1])


def _row_max(*pieces):
    tiles = [p[:, i:i + 128] for p in pieces for i in range(0, p.shape[1], 128)]
    return jnp.max(functools.reduce(jnp.maximum, tiles), axis=-1, keepdims=True)


CTX_PAIRS = 2


def _ctx_attn_kernel(q_ref, k_ref, v_ref, qn_ref, kn_ref, o_ref, kc_ref, vc_ref):
    first, ones_col, mean2 = _pair_consts()
    lanes = [slice(p * PAIR, (p + 1) * PAIR) for p in range(CTX_PAIRS)]
    qn = [_pair_norm(q_ref[:, ln], qn_ref[...], mean2) * HD_C ** -0.5 for ln in lanes]
    kn = [_pair_norm(k_ref[:, ln], kn_ref[...], mean2) for ln in lanes]
    v = [v_ref[:, ln] for ln in lanes]
    for p in range(CTX_PAIRS):
        for j in range(2):
            kc_ref[0, 0, 2 * p + j] = kn[p][:, j * HD_C:(j + 1) * HD_C]
            vc_ref[0, 0, 2 * p + j] = v[p][:, j * HD_C:(j + 1) * HD_C]
    q = [_pair_queries(x, first) for x in qn]
    va = [_pair_values(x, first, ones_col) for x in v]
    kb = [x.astype(BF16) for x in kn]
    s = [[_dot_nt(q[p][j], kb[p]) for j in range(2)] for p in range(CTX_PAIRS)]
    pr = [[jnp.exp(x - _row_max(x)).astype(BF16) for x in sp] for sp in s]
    for p in range(CTX_PAIRS):
        o_ref[:, lanes[p]] = _pair_output([_dot(pr[p][j], va[p][j]) for j in range(2)], first)


def _ctx_attention(qkv, qn, kn):
    heads = 2 * CTX_PAIRS
    ng = H_C // heads
    wide = CTX_PAIRS * PAIR
    blk = lambda off: pl.BlockSpec((SEQ, wide), lambda b, p: (b, off + p))
    cache_spec = pl.BlockSpec((1, 1, heads, SEQ, HD_C), lambda b, p: (b, 0, p, 0, 0))
    cache_shape = jax.ShapeDtypeStruct((BATCH, 1, H_C, SEQ, HD_C), F32)
    return pl.pallas_call(
        _ctx_attn_kernel,
        grid=(BATCH, ng),
        in_specs=[blk(0), blk(ng), blk(2 * ng),
                  pl.BlockSpec((1, PAIR), lambda b, p: (0, 0)),
                  pl.BlockSpec((1, PAIR), lambda b, p: (0, 0))],
        out_specs=[pl.BlockSpec((SEQ, wide), lambda b, p: (b, p)), cache_spec, cache_spec],
        out_shape=[jax.ShapeDtypeStruct((N_TOK, D_MODEL), F32), cache_shape, cache_shape],
        compiler_params=_cparams("arbitrary", "arbitrary"),
        name="ctx_attention",
    )(qkv, qkv, qkv, jnp.tile(qn.reshape(1, HD_C), (1, 2)), jnp.tile(kn.reshape(1, HD_C), (1, 2)))


def _na_row_start(r):
    return min(max(r - KH // 2, 0), GRID_ROWS - KH)


NA_ROW_GROUP = 4


def _na_attn_kernel(q_ref, k_ref, v_ref, kc_ref, vc_ref, qn_ref, kn_ref, bias_ref, mixed_in_ref, o_ref,
                    qs, ks, vs, bias_s):
    del mixed_in_ref
    first, ones_col, mean2 = _pair_consts()

    @pl.when(pl.program_id(1) == 0)
    def _():
        for j in range(2):
            bias_s[j, 0] = bias_ref[j]
            bias_s[j, 1] = pltpu.roll(bias_ref[j], NA_BIAS_LANES - GRID_W, axis=1)

    q2 = _pair_queries(_pair_norm(q_ref[...], qn_ref[...], mean2) * HD_C ** -0.5, first)
    v2 = _pair_values(v_ref[...], first, ones_col)
    ks[...] = _pair_norm(k_ref[...], kn_ref[...], mean2).astype(BF16)
    for j in range(2):
        qs[j] = q2[j]
        vs[j] = v2[j]
    k_ctx = jnp.concatenate([kc_ref[0, 0, 0], kc_ref[0, 0, 1]], axis=1).astype(BF16)
    v_ctx = _pair_values(jnp.concatenate([vc_ref[0, 0, 0], vc_ref[0, 0, 1]], axis=1), first, ones_col)
    for r0 in range(0, GRID_ROWS, NA_ROW_GROUP):
        units = [(r, j) for r in range(r0, r0 + NA_ROW_GROUP) for j in range(2)]
        rows = {r: slice(r * GRID_W, (r + 1) * GRID_W) for r, _ in units}
        wins = {r: slice(_na_row_start(r) * GRID_W, (_na_row_start(r) + KH) * GRID_W) for r, _ in units}
        s_ctx_all = [_dot_nt(qs[j, r0 * GRID_W:(r0 + NA_ROW_GROUP) * GRID_W, :], k_ctx) for j in range(2)]
        s_ctx = [s_ctx_all[j][(r - r0) * GRID_W:(r - r0 + 1) * GRID_W] for r, j in units]
        s_win = []
        for r, j in units:
            dr0 = KH - 1 - (r - _na_row_start(r))
            lane0 = (dr0 - dr0 % 2) * GRID_W
            s_win.append(_dot_nt(qs[j, rows[r], :], ks[wins[r], :])
                         + bias_s[j, dr0 % 2, :, lane0:lane0 + KH * GRID_W])
        m = [_row_max(a, b) for a, b in zip(s_win, s_ctx)]
        p_win = [jnp.exp(a - mm).astype(BF16) for a, mm in zip(s_win, m)]
        p_ctx = [jnp.exp(b - mm).astype(BF16) for b, mm in zip(s_ctx, m)]
        o_aug = [_dot(p_win[i], vs[j, wins[r], :]) + _dot(p_ctx[i], v_ctx[j]) for i, (r, j) in enumerate(units)]
        for i in range(0, len(units), 2):
            o_ref[rows[units[i][0]], :] = _pair_output(o_aug[i:i + 2], first)


NA_BIAS_LANES = 2 * KH * GRID_W


def _na_bias_table(rpb):
    qc = np.arange(GRID_W)[:, None]
    kc = np.arange(GRID_W)[None, :]
    cstart = np.clip(qc - KW // 2, 0, GRID_W - KW)
    valid = (kc >= cstart) & (kc < cstart + KW)
    onehot = ((kc - qc + KW - 1)[None] == np.arange(2 * KW - 1)[:, None, None]) & valid[None]
    outside = np.tile(np.where(valid, 0.0, NEG_INF).astype(np.float32), (1, 2 * KH))
    rpb16 = jnp.pad(rpb.astype(F32), ((0, 0), (0, 1), (0, 0)))
    t = jnp.einsum('hrd,dqk->hqrk', rpb16, jnp.asarray(onehot, F32), precision=lax.Precision.HIGHEST)
    return t.reshape(H_C, GRID_W, NA_BIAS_LANES) + outside[None]


def _na_attention(qkv, cache_k, cache_v, qn, kn, bias, mixed):
    nhp = H_C // 2
    row0 = N_PROMPT // DEC_SEQ
    blk = lambda off: pl.BlockSpec((DEC_SEQ, 2 * HD_C), lambda p, b: (row0 + b, off + p))
    cache_spec = pl.BlockSpec((1, 1, 2, PAST_LEN, HD_C), lambda p, b: (b, 0, p, 0, 0))
    return pl.pallas_call(
        _na_attn_kernel,
        grid=(nhp, DEC_BATCH),
        in_specs=[blk(0), blk(nhp), blk(2 * nhp), cache_spec, cache_spec,
                  pl.BlockSpec((1, PAIR), lambda p, b: (0, 0)),
                  pl.BlockSpec((1, PAIR), lambda p, b: (0, 0)),
                  pl.BlockSpec((2, GRID_W, NA_BIAS_LANES), lambda p, b: (p, 0, 0)),
                  pl.BlockSpec(memory_space=pl.ANY)],
        out_specs=pl.BlockSpec((DEC_SEQ, 2 * HD_C), lambda p, b: (row0 + b, p)),
        out_shape=jax.ShapeDtypeStruct((N_TOK, D_MODEL), F32),
        input_output_aliases={8: 0},
        scratch_shapes=[pltpu.VMEM((2, DEC_SEQ, PAIR), BF16), pltpu.VMEM((DEC_SEQ, PAIR), BF16),
                        pltpu.VMEM((2, DEC_SEQ, PAIR), BF16), pltpu.VMEM((2, 2, GRID_W, NA_BIAS_LANES), F32)],
        compiler_params=_cparams("arbitrary", "arbitrary"),
        name="na_attention",
    )(qkv, qkv, qkv, cache_k, cache_v, jnp.tile(qn.reshape(1, HD_C), (1, 2)), jnp.tile(kn.reshape(1, HD_C), (1, 2)),
      bias, mixed)


def _seq_layout(prompt):
    return (SEQ, BATCH, 0) if prompt else (DEC_SEQ, DEC_BATCH, N_PROMPT // DEC_SEQ)


def _flip_blocks(m, c):
    r, s = m.shape
    return m.reshape(r // c, c, s // c, c)[:, ::-1, :, ::-1].reshape(r, s)


def _rms_gate(x, gn, gate):
    ms = jnp.mean(x * x, axis=-1, keepdims=True)
    return x * lax.rsqrt(ms + EPS) * gn * _silu(gate)


HG_LEVELS = tuple(CHUNK >> (i + 1) for i in range(CHUNK.bit_length() - 1))
HG_NL = len(HG_LEVELS)
HG_STACK = (HG_NL + 1) * CHUNK
TOT_ROWS = 16
HG_ROWS = (HG_NL + 2) * CHUNK + TOT_ROWS


def _hgrn_consts():
    c = CHUNK
    level_rows = []
    mask = np.zeros((HG_STACK, HG_STACK), np.float32)
    mask[:c, :c] = np.eye(c)
    for li, b in enumerate(HG_LEVELS):
        m = np.zeros((c, c), np.float32)
        blk = np.zeros((c, c), np.float32)
        for t in range(c):
            mid = (t // (2 * b)) * 2 * b + b
            if t >= mid:
                m[t, mid:t + 1] = 1.0
                blk[t, mid - b:mid] = 1.0
            else:
                m[t, t + 1:mid] = 1.0
        level_rows.append(m)
        mask[(li + 1) * c:(li + 2) * c, (li + 1) * c:(li + 2) * c] = blk
    dq = np.tril(np.ones((c, c), np.float32))
    dk = np.triu(np.ones((c, c), np.float32), 1)
    body = np.concatenate(level_rows + [dq, dk], axis=0)
    tot = np.ones((TOT_ROWS, c), np.float32)
    mcs, masks = [], []
    for reverse in (False, True):
        bm = _flip_blocks(body, c) if reverse else body
        mk = _flip_blocks(mask, c) if reverse else mask
        mc = np.concatenate([bm, tot], axis=0)
        mcs.append(np.concatenate([mc, mc], axis=1))
        masks.append(mk)
    return jnp.asarray(np.stack(mcs), BF16), jnp.asarray(np.stack(masks), F32)


def _hgrn_kernel(*refs, seq, has_s0, emit_state, aliased):
    it = iter(refs)
    qa_ref, ff_ref, fb_ref, ia_ref, ga_ref, lb_ref, gn_ref, mc_ref, mask_ref = [next(it) for _ in range(9)]
    s0_ref = next(it) if has_s0 else None
    if aliased:
        next(it)
    o_ref = next(it)
    st_ref = next(it) if emit_state else None
    s_scr, acc = next(it), next(it)
    c = CHUNK
    n_chunks = seq // c

    lb_raw = lb_ref[...]
    lb_e = jnp.exp(lb_raw - jnp.max(lb_raw, axis=0, keepdims=True))
    lb_all = lb_e[0:1] / jnp.sum(lb_e, axis=0, keepdims=True)

    for d in range(2):
        for h in range(H_A):
            s_scr[d, h] = s0_ref[0, 0, d, h].T if has_s0 else jnp.zeros((DV_A, DK_A), F32)
    acc[...] = jnp.zeros(acc.shape, F32)

    def body(n, carry):
        combos = [(d, h) for d in range(2) for h in range(H_A)]
        lanes = [slice(h * DK_A, (h + 1) * DK_A) for h in range(H_A)]
        rows = [pl.ds(pl.multiple_of((n if d == 0 else n_chunks - 1 - n) * c, c), c) for d in range(2)]
        f_all = [lb_all + (1.0 - lb_all) * _sigmoid((ff_ref, fb_ref)[d][rows[d], :]) for d in range(2)]
        e_all = [jnp.exp(_dot_const(mc_ref[d], jnp.log(f_all[d]))) for d in range(2)]
        q_all = [_silu(qa_ref[rows[d], :]) * DK_A ** -0.5 for d in range(2)]
        v_all = [ia_ref[rows[d], :].astype(BF16) for d in range(2)]
        st = [s_scr[d, h] for d, h in combos]
        qs, ks, vs, es = [], [], [], []
        for d, h in combos:
            qs.append(q_all[d][:, lanes[h]])
            ks.append(1.0 - f_all[d][:, lanes[h]])
            vs.append(v_all[d][:, lanes[h]])
            es.append(e_all[d][:, lanes[h]])
        lvl = [[e[i * c:(i + 1) * c] for i in range(HG_NL + 2)] for e in es]
        qst = [jnp.concatenate([q] + [q * l[i] for i in range(HG_NL)], axis=0).astype(BF16) for q, l in zip(qs, lvl)]
        kst = [jnp.concatenate([k] + [k * l[i] for i in range(HG_NL)], axis=0).astype(BF16) for k, l in zip(ks, lvl)]
        r = [(_dot_nt(qst[i], kst[i]) * mask_ref[d]).astype(BF16) for i, (d, h) in enumerate(combos)]
        ost = [_dot(r[i], jnp.concatenate([vs[i]] * (HG_NL + 1), axis=0)) for i in range(len(combos))]
        inter = [_dot_nt((qs[i] * lvl[i][HG_NL]).astype(BF16), st[i].astype(BF16)) for i in range(len(combos))]
        upd = [_dot_tn(vs[i], (ks[i] * lvl[i][HG_NL + 1]).astype(BF16)) for i in range(len(combos))]
        o = [functools.reduce(lambda a, b: a + b, [ost[i][j * c:(j + 1) * c] for j in range(HG_NL + 1)]) + inter[i]
             for i in range(len(combos))]
        for d in range(2):
            acc[rows[d], :] += jnp.concatenate(o[d * H_A:(d + 1) * H_A], axis=1)
        for i, (d, h) in enumerate(combos):
            e_tot = es[i][(HG_NL + 2) * c:(HG_NL + 2) * c + 1]
            s_scr[d, h] = st[i] * e_tot + upd[i]
        return carry

    lax.fori_loop(0, n_chunks, body, 0)

    for h in range(H_A):
        ln = slice(h * DV_A, (h + 1) * DV_A)
        o_ref[:, ln] = _rms_gate(acc[:, ln], gn_ref[...], ga_ref[:, ln])
    if emit_state:
        for d in range(2):
            for h in range(H_A):
                st_ref[0, 0, d, h] = s_scr[d, h].T


def _hgrn(proj, hgrn_lb, gn, consts, prompt, s0=None, layer=0, mixed=None):
    seq, nb, rb0 = _seq_layout(prompt)
    mc2, mask = consts
    wa = H_A * DK_A
    blk = lambda j: pl.BlockSpec((seq, wa), lambda b: (rb0 + b, j))
    const2 = lambda b: (0, 0)
    const3 = lambda b: (0, 0, 0)
    st_block = (1, 1, 2, H_A, DK_A, DV_A)
    in_specs = [blk(0), blk(1), blk(2), blk(3), blk(4),
                pl.BlockSpec(hgrn_lb.shape, const2), pl.BlockSpec((1, DV_A), const2),
                pl.BlockSpec(mc2.shape, const3), pl.BlockSpec(mask.shape, const3)]
    args = [proj] * 5 + [hgrn_lb, gn.reshape(1, DV_A), mc2, mask]
    if s0 is not None:
        in_specs.append(pl.BlockSpec(st_block, lambda b: (b, layer, 0, 0, 0, 0)))
        args.append(s0)
    aliases = {}
    if mixed is not None:
        aliases = {len(args): 0}
        in_specs.append(pl.BlockSpec(memory_space=pl.ANY))
        args.append(mixed)
    out_specs = [pl.BlockSpec((seq, wa), lambda b: (rb0 + b, 0))]
    out_shape = [jax.ShapeDtypeStruct((N_TOK, D_MODEL), F32)]
    if prompt:
        out_specs.append(pl.BlockSpec(st_block, lambda b: (b, 0, 0, 0, 0, 0)))
        out_shape.append(jax.ShapeDtypeStruct((nb, 1, 2, H_A, DK_A, DV_A), F32))
    return pl.pallas_call(
        functools.partial(_hgrn_kernel, seq=seq, has_s0=s0 is not None, emit_state=prompt, aliased=mixed is not None),
        grid=(nb,),
        in_specs=in_specs,
        out_specs=out_specs,
        out_shape=out_shape,
        input_output_aliases=aliases,
        scratch_shapes=[pltpu.VMEM((2, H_A, DV_A, DK_A), F32), pltpu.VMEM((seq, wa), F32)],
        compiler_params=_cparams("arbitrary"),
        name="hgrn_prompt" if prompt else "hgrn_sample",
    )(*args)


GD_SUB = GBLK // CHUNK
GD_ROWS = 2 * GBLK + TOT_ROWS


def _gdn_consts():
    n, c = GBLK, CHUNK
    same = (np.arange(n)[:, None] // c) == (np.arange(n)[None, :] // c)
    tri = (same & (np.arange(n)[None, :] <= np.arange(n)[:, None])).astype(np.float32)
    sup = (same & (np.arange(n)[None, :] > np.arange(n)[:, None])).astype(np.float32)
    tot = np.zeros((TOT_ROWS, n), np.float32)
    for s in range(GD_SUB):
        tot[s, s * c:(s + 1) * c] = 1.0
    mgs, tts, tris = [], [], []
    for reverse in (False, True):
        t = _flip_blocks(tri, c) if reverse else tri
        s = _flip_blocks(sup, c) if reverse else sup
        mg = np.concatenate([t, s, tot], axis=0)
        mgs.append(np.concatenate([mg, mg], axis=1))
        tts.append(np.concatenate([t.T, t.T], axis=0))
        tris.append(t)
    tris.append(same.astype(np.float32))
    return jnp.asarray(np.stack(mgs), BF16), jnp.asarray(np.stack(tts), BF16), jnp.asarray(np.stack(tris), F32)


def _softplus(x):
    return jnp.maximum(x, 0.0) + jnp.log(1.0 + jnp.exp(-jnp.abs(x)))


def _conv_silu(x, w, seq):
    t_idx = lax.broadcasted_iota(jnp.int32, (seq, 1), 0)
    half = SHORT_CONV // 2
    acc = x * w[half:half + 1]
    for j in range(SHORT_CONV):
        shift = half - j
        if shift == 0:
            continue
        src = t_idx - shift
        xr = pltpu.roll(x, shift % seq, axis=0)
        acc = acc + jnp.where((src >= 0) & (src < seq), xr, 0.0) * w[j:j + 1]
    return _silu(acc)


def _l2norm_heads(x, n_heads, width, scale):
    outs = []
    for h in range(n_heads):
        xh = x[:, h * width:(h + 1) * width]
        outs.append(xh * (lax.rsqrt(jnp.sum(xh * xh, axis=-1, keepdims=True) + EPS) * scale))
    return jnp.concatenate(outs, axis=-1)


def _gdn_kernel(*refs, seq, has_s0, emit_state, aliased):
    it = iter(refs)
    (q_ref, k_ref, v_ref, gb_ref, gate_ref, cw_ref, alog_ref, dt_ref, gn_ref,
     mg_ref, tt_ref, tri_ref) = [next(it) for _ in range(12)]
    s0_ref = next(it) if has_s0 else None
    if aliased:
        next(it)
    o_ref = next(it)
    st_ref = next(it) if emit_state else None
    qn, kn, vn, u_s, w_s, qg_s, kdt_s, at_s, et_s, s_scr, acc = [next(it) for _ in range(11)]
    c = CHUNK
    n_chunks = seq // c
    n_blocks = seq // GBLK
    wq = H_B * DK_B
    n_dh = 2 * H_B
    combos = [(d, h) for d in range(2) for h in range(H_B)]
    lanes = [slice(h * DK_B, (h + 1) * DK_B) for h in range(H_B)]

    qn[...] = _l2norm_heads(_conv_silu(q_ref[...], cw_ref[:, 0:wq], seq), H_B, DK_B, DK_B ** -0.5)
    kn[...] = _l2norm_heads(_conv_silu(k_ref[...], cw_ref[:, wq:2 * wq], seq), H_B, DK_B, 1.0)
    vn[...] = _conv_silu(v_ref[...], cw_ref[:, 2 * wq:3 * wq], seq)
    for i in range(2 * H_B):
        s_scr[i] = s0_ref[0, 0, i // H_B, i % H_B] if has_s0 else jnp.zeros((DK_B, DV_B), F32)
    acc[...] = jnp.zeros(acc.shape, F32)

    eye = (lax.broadcasted_iota(jnp.int32, (GBLK, GBLK), 0)
           == lax.broadcasted_iota(jnp.int32, (GBLK, GBLK), 1)).astype(F32)
    eye_pk = (lax.broadcasted_iota(jnp.int32, (c, GBLK), 0)
              == lax.broadcasted_iota(jnp.int32, (c, GBLK), 1) % c).astype(F32)
    bwd_lane = lax.broadcasted_iota(jnp.int32, (1, 128), 1) % n_dh >= H_B
    add = lambda a, b: a + b

    def expand(pk):
        return jnp.concatenate([pk] * GD_SUB, axis=0) * tri_ref[2]

    def pack(bd):
        return functools.reduce(add, [bd[s * c:(s + 1) * c] for s in range(GD_SUB)])

    def dot3_split(a, b_hi, b_lo):
        ah, al = _split2(a)
        t = _dot(jnp.concatenate([ah, al], axis=0), b_hi)
        return t[:a.shape[0]] + t[a.shape[0]:] + _dot(ah, b_lo)

    def block_body(blk, carry):
        rows = pl.ds(pl.multiple_of(blk * GBLK, GBLK), GBLK)
        gates = gate_ref[rows, :]
        glog_all = -jnp.exp(alog_ref[...]) * _softplus(gates + dt_ref[...])
        beta_all = _sigmoid(gates)
        g2 = jnp.concatenate(_split2(glog_all), axis=0)
        dg = [_dot(mg_ref[d], g2) for d in range(2)]
        dsel = jnp.where(bwd_lane, dg[1], dg[0])
        eg_all = jnp.exp(dsel)
        g_all = dsel[:GBLK]
        gt = [_dot_tn(g2, tt_ref[d]) for d in range(2)]
        qs = [qn[rows, ln] for ln in lanes]
        ks = [kn[rows, ln] for ln in lanes]
        vs = [vn[rows, ln] for ln in lanes]
        col = lambda x, j: jnp.broadcast_to(x[:, j:j + 1], (GBLK, DK_B))
        betas = [col(beta_all, n_dh + i) for i in range(n_dh)]
        kbs = [ks[h] * betas[i] for i, (d, h) in enumerate(combos)]
        kk = [_dot_nt(jnp.concatenate([qs[h], kbs[h], kbs[H_B + h]], axis=0).astype(BF16), ks[h].astype(BF16))
              for h in range(H_B)]
        decay = []
        for i, (d, h) in enumerate(combos):
            inside = tri_ref[d] > 0.0
            gd = col(g_all, i) - gt[d][i:i + 1, :]
            decay.append(jnp.where(inside, jnp.exp(jnp.where(inside, gd, 0.0)), 0.0))
        attn = [kk[h][:GBLK] * decay[i] for i, (d, h) in enumerate(combos)]
        p_pk = [pack(kk[h][(1 + d) * GBLK:(2 + d) * GBLK] * decay[i] * (1.0 - eye)) for i, (d, h) in enumerate(combos)]
        x_pk = [eye_pk - p for p in p_pk]
        p_bd = [_split2(expand(p)) for p in p_pk]
        for _ in range(CHUNK.bit_length() - 2):
            p_pk = [dot3_split(p, *b) for p, b in zip(p_pk, p_bd)]
            p_bd = [_split2(expand(p)) for p in p_pk]
            x_pk = [x + dot3_split(x, *b) for x, b in zip(x_pk, p_bd)]
        eg_col = [col(eg_all[:GBLK], i) for i in range(n_dh)]
        ekd_col = [col(eg_all[GBLK:2 * GBLK], i) for i in range(n_dh)]
        rhs = [jnp.concatenate([vs[h] * betas[i], kbs[i] * eg_col[i]], axis=1) for i, (d, h) in enumerate(combos)]
        uw = [_dot3(expand(x_pk[i]), rhs[i]) for i in range(n_dh)]
        for i, (d, h) in enumerate(combos):
            qg = (qs[h] * eg_col[i]).astype(BF16)
            kdt_s[i, blk] = (ks[h] * ekd_col[i]).T.astype(BF16)
            for s in range(GD_SUB):
                cn = blk * GD_SUB + s
                r = slice(s * c, (s + 1) * c)
                u_s[i, cn] = uw[i][r, :DV_B]
                w_s[i, cn] = uw[i][r, DV_B:].astype(BF16)
                qg_s[i, cn] = qg[r]
                at_s[i, cn] = attn[i][r].astype(BF16)
                et_s[i, cn] = jnp.broadcast_to(eg_all[2 * GBLK + s:2 * GBLK + s + 1, i:i + 1], (8, DV_B))
        return carry

    lax.fori_loop(0, n_blocks, block_body, 0)

    def chunk_body(n, carry):
        cns = [n, n_chunks - 1 - n]
        rows = [pl.ds(pl.multiple_of(cn * c, c), c) for cn in cns]
        sub_of_row = lax.broadcasted_iota(jnp.int32, (GBLK, 1), 0) // c
        in_chunk = [sub_of_row == cn % GD_SUB for cn in cns]
        st = [s_scr[i] for i in range(n_dh)]
        ws = [_dot(jnp.concatenate([w_s[i, cns[d]], qg_s[i, cns[d]]], axis=0), st[i].astype(BF16))
              for i, (d, h) in enumerate(combos)]
        vblk = [jnp.where(in_chunk[d], jnp.concatenate([u_s[i, cns[d]] - ws[i][:c]] * GD_SUB, axis=0), 0.0).astype(BF16)
                for i, (d, h) in enumerate(combos)]
        r = [_dot(jnp.concatenate([at_s[i, cns[d]], kdt_s[i, cns[d] // GD_SUB]], axis=0), vblk[i])
             for i, (d, h) in enumerate(combos)]
        for d in range(2):
            acc[rows[d], :] += jnp.concatenate([ws[i][c:] + r[i][:c] for i in range(d * H_B, (d + 1) * H_B)], axis=1)
        for i, (d, h) in enumerate(combos):
            s_scr[i] = st[i] * et_s[i, cns[d]][0:1] + r[i][c:]
        return carry

    lax.fori_loop(0, n_chunks, chunk_body, 0)

    for h in range(H_B):
        ln = slice(h * DV_B, (h + 1) * DV_B)
        o_ref[:, ln] = _rms_gate(acc[:, ln], gn_ref[...], gb_ref[:, ln])
    if emit_state:
        for i in range(2 * H_B):
            st_ref[0, 0, i // H_B, i % H_B] = s_scr[i]


def _gdn(proj, gates, conv_w, a_log, dt_bias, gn, consts, prompt, mixed, s0=None, layer=0):
    seq, nb, rb0 = _seq_layout(prompt)
    n_chunks = seq // CHUNK
    wq = H_B * DK_B
    blk = lambda j: pl.BlockSpec((seq, wq), lambda b: (rb0 + b, j))
    const2 = lambda b: (0, 0)
    const3 = lambda b: (0, 0, 0)
    st_block = (1, 1, 2, H_B, DK_B, DV_B)
    pad_row = lambda p: jnp.pad(p.reshape(1, -1).astype(F32), ((0, 0), (0, 128 - p.size)))
    in_specs = [blk(5), blk(6), blk(7), blk(8),
                pl.BlockSpec((seq, 128), lambda b: (rb0 + b, 0)),
                pl.BlockSpec((SHORT_CONV, 3 * wq), const2),
                pl.BlockSpec((1, 128), const2), pl.BlockSpec((1, 128), const2), pl.BlockSpec((1, DV_B), const2)]
    in_specs += [pl.BlockSpec(m.shape, const3) for m in consts]
    args = [proj] * 4 + [gates, conv_w.reshape(SHORT_CONV, 3 * wq), pad_row(a_log), pad_row(dt_bias),
                         gn.reshape(1, DV_B)] + list(consts)
    if s0 is not None:
        in_specs.append(pl.BlockSpec(st_block, lambda b: (b, layer, 0, 0, 0, 0)))
        args.append(s0)
    aliases = {len(args): 0}
    in_specs.append(pl.BlockSpec(memory_space=pl.ANY))
    args.append(mixed)
    out_specs = [pl.BlockSpec((seq, wq), lambda b: (rb0 + b, 1))]
    out_shape = [jax.ShapeDtypeStruct((N_TOK, D_MODEL), F32)]
    if prompt:
        out_specs.append(pl.BlockSpec(st_block, lambda b: (b, 0, 0, 0, 0, 0)))
        out_shape.append(jax.ShapeDtypeStruct((nb, 1, 2, H_B, DK_B, DV_B), F32))
    n_dh = 2 * H_B
    scratch = ([pltpu.VMEM((seq, wq), F32)] * 3
               + [pltpu.VMEM((n_dh, n_chunks, CHUNK, DV_B), F32)]
               + [pltpu.VMEM((n_dh, n_chunks, CHUNK, DK_B), BF16)] * 2
               + [pltpu.VMEM((n_dh, seq // GBLK, DK_B, GBLK), BF16),
                  pltpu.VMEM((n_dh, n_chunks, CHUNK, GBLK), BF16),
                  pltpu.VMEM((n_dh, n_chunks, 8, DV_B), F32),
                  pltpu.VMEM((n_dh, DK_B, DV_B), F32),
                  pltpu.VMEM((seq, wq), F32)])
    return pl.pallas_call(
        functools.partial(_gdn_kernel, seq=seq, has_s0=s0 is not None, emit_state=prompt, aliased=True),
        grid=(nb,),
        in_specs=in_specs,
        out_specs=out_specs,
        out_shape=out_shape,
        input_output_aliases=aliases,
        scratch_shapes=scratch,
        compiler_params=_cparams("arbitrary"),
        name="gdn_prompt" if prompt else "gdn_sample",
    )(*args)


def kernel(x_prompt, x_sample, state_hgrn, state_gdn, cache_na_k, cache_na_v, c, c_ctx, ada_w, ada_b, norm_g, w_in_ab, w_out_ab, hgrn_lb, gdn_conv, gdn_a_log, gdn_dt_bias, gn_hgrn, gn_gdn, w_qkv_na, qn_na, kn_na, rpb_na, w_out_na, w_mlp1, w_mlp2):
    cond = jnp.concatenate([c_ctx[None, :], c, jnp.zeros((N_MOD_ROWS - 1 - DEC_BATCH, D_MODEL), F32)], axis=0)
    mods = _modulation(cond, ada_w, ada_b)
    xs = (x_prompt.reshape(N_PROMPT, D_MODEL), x_sample.reshape(N_SAMPLE, D_MODEL))

    w_in = w_in_ab[0]
    w_main = w_in[:, :D_MAIN_AB].astype(BF16)
    w_gate = jnp.pad(w_in[:, D_MAIN_AB:], ((0, 0), (0, 128 - N_GATE_AB))).astype(BF16)
    proj, gates = _norm_proj(xs, mods[0], norm_g[0, 0], [w_main, w_gate])
    hg_consts = _hgrn_consts()
    gd_consts = _gdn_consts()
    mixed, new_hgrn = _hgrn(proj, hgrn_lb, gn_hgrn[0], hg_consts, True)
    mixed, = _hgrn(proj, hgrn_lb, gn_hgrn[0], hg_consts, False, s0=state_hgrn, mixed=mixed)
    gd_args = (gdn_conv[0], gdn_a_log[0], gdn_dt_bias[0], gn_gdn[0], gd_consts)
    mixed, new_gdn = _gdn(proj, gates, *gd_args, True, mixed)
    mixed, = _gdn(proj, gates, *gd_args, False, mixed, s0=state_gdn)
    xs = _post_mixer(xs, mixed, mods[0], norm_g[0, 1], w_out_ab[0].astype(BF16),
                     w_mlp1[0].astype(BF16), w_mlp2[0].astype(BF16))

    qkv, = _norm_proj(xs, mods[1], norm_g[1, 0], [w_qkv_na[0].astype(BF16)])
    mixed, new_k, new_v = _ctx_attention(qkv, qn_na[0], kn_na[0])
    mixed = _na_attention(qkv, cache_na_k, cache_na_v, qn_na[0], kn_na[0], _na_bias_table(rpb_na[0]), mixed)
    y_prompt, y_sample = _post_mixer(xs, mixed, mods[1], norm_g[1, 1], w_out_na[0].astype(BF16),
                                     w_mlp1[1].astype(BF16), w_mlp2[1].astype(BF16), split_out=True)

    return (y_prompt.reshape(BATCH, SEQ, D_MODEL), y_sample.reshape(DEC_BATCH, DEC_SEQ, D_MODEL),
            new_hgrn, new_gdn, new_k, new_v)
```

```python
import functools

import numpy as np
import jax
import jax.numpy as jnp
from jax import lax
from jax.experimental import pallas as pl
from jax.experimental.pallas import tpu as pltpu

F32 = jnp.float32
BF16 = jnp.bfloat16

D_MODEL = 1024
BATCH = 16
SEQ = 256
DEC_BATCH = 4
DEC_SEQ = 1024
PAST_LEN = 256
N_PROMPT = BATCH * SEQ
N_SAMPLE = DEC_BATCH * DEC_SEQ
N_TOK = N_PROMPT + N_SAMPLE
GRID_W = 64
GRID_ROWS = DEC_SEQ // GRID_W
H_A = 4
DK_A = 128
DV_A = 128
H_B = 4
DK_B = 128
DV_B = 128
SHORT_CONV = 5
H_C = 16
HD_C = 64
KH = 8
KW = 16
D_FF = 4 * D_MODEL
EPS = 1e-6
NEG_INF = -1e30
N_MOD_ROWS = 8
D_MAIN_AB = 4608
N_GATE_AB = 16
CHUNK = 32
GBLK = 128
VMEM_LIMIT = 56 * 1024 * 1024


def _cparams(*sem):
    return pltpu.CompilerParams(dimension_semantics=sem, vmem_limit_bytes=VMEM_LIMIT)


def _sigmoid(x):
    return 1.0 / (1.0 + jnp.exp(-x))


def _silu(x):
    return x * _sigmoid(x)


def _dot(a, b):
    return jnp.dot(a, b, preferred_element_type=F32)


def _dot_nt(a, b):
    return lax.dot_general(a, b, (((1,), (1,)), ((), ())), preferred_element_type=F32)


def _dot_tn(a, b):
    return lax.dot_general(a, b, (((0,), (0,)), ((), ())), preferred_element_type=F32)


def _split2(x):
    hi = x.astype(BF16)
    lo = (x - hi.astype(F32)).astype(BF16)
    return hi, lo


def _dot_const(m2, x):
    hi, lo = _split2(x)
    return _dot(m2, jnp.concatenate([hi, lo], axis=0))


def _dot3(a, b):
    ah, al = _split2(a)
    bh, bl = _split2(b)
    return _dot(ah, bh) + (_dot(ah, bl) + _dot(al, bh))


def _mod_row(i, tm):
    start = i * tm
    return jnp.where(start < N_PROMPT, 0, 1 + (start - N_PROMPT) // DEC_SEQ)


def _mod_slice(mod_ref, row, k):
    return mod_ref[pl.ds(row, 1), k * D_MODEL:(k + 1) * D_MODEL]


def _norm_mod(x, g, sc, sh):
    ms = jnp.mean(x * x, axis=-1, keepdims=True)
    return (x * lax.rsqrt(ms + EPS) * g) * (1.0 + sc) + sh


def _mod_kernel(cond_ref, w_ref, b_ref, o_ref):
    s = _silu(cond_ref[...]).astype(BF16)
    o_ref[0] = _dot(s, w_ref[0].astype(BF16)) + b_ref[0]


def _modulation(cond8, ada_w, ada_b):
    depth = ada_w.shape[0]
    tn = 1024
    nj = ada_w.shape[2] // tn
    return pl.pallas_call(
        _mod_kernel,
        grid=(depth, nj),
        in_specs=[
            pl.BlockSpec((N_MOD_ROWS, D_MODEL), lambda l, j: (0, 0)),
            pl.BlockSpec((1, D_MODEL, tn), lambda l, j: (l, 0, j)),
            pl.BlockSpec((1, 1, tn), lambda l, j: (l, 0, j)),
        ],
        out_specs=pl.BlockSpec((1, N_MOD_ROWS, tn), lambda l, j: (l, 0, j)),
        out_shape=jax.ShapeDtypeStruct((depth, N_MOD_ROWS, ada_w.shape[2]), F32),
        compiler_params=_cparams("arbitrary", "arbitrary"),
        name="modulation",
    )(cond8, ada_w, ada_b.reshape(depth, 1, -1))


def _stream_specs(n_arrays, tm):
    if n_arrays == 1:
        return [pl.BlockSpec((tm, D_MODEL), lambda i: (i, 0))]
    npt = N_PROMPT // tm
    return [pl.BlockSpec((tm, D_MODEL), lambda i: (jnp.minimum(i, npt - 1), 0)),
            pl.BlockSpec((tm, D_MODEL), lambda i: (jnp.maximum(i - npt, 0), 0))]


def _stream_load(x_refs, tm):
    if len(x_refs) == 1:
        return x_refs[0][...]
    return jnp.where(pl.program_id(0) < N_PROMPT // tm, x_refs[0][...], x_refs[1][...])


def _norm_proj_kernel(*refs, tm, n_x, n_w):
    x_refs, (mod_ref, g_ref) = refs[:n_x], refs[n_x:n_x + 2]
    w_refs, o_refs = refs[n_x + 2:n_x + 2 + n_w], refs[n_x + 2 + n_w:]
    row = _mod_row(pl.program_id(0), tm)
    h = _norm_mod(_stream_load(x_refs, tm), g_ref[...], _mod_slice(mod_ref, row, 1), _mod_slice(mod_ref, row, 0)).astype(BF16)
    for w_ref, o_ref in zip(w_refs, o_refs):
        o_ref[...] = _dot(h, w_ref[...])


def _norm_proj(xs, mod, g, ws, widths=None, tm=256):
    n_w = len(ws)
    widths = widths or [w.shape[1] for w in ws]
    const = lambda i: (0, 0)
    return pl.pallas_call(
        functools.partial(_norm_proj_kernel, tm=tm, n_x=len(xs), n_w=n_w),
        grid=(N_TOK // tm,),
        in_specs=_stream_specs(len(xs), tm) + [
            pl.BlockSpec(mod.shape, const),
            pl.BlockSpec((1, D_MODEL), const),
        ] + [pl.BlockSpec((D_MODEL, n), const, pipeline_mode=pl.Buffered(1)) for n in widths],
        out_specs=[pl.BlockSpec((tm, n), lambda i: (i, 0)) for n in widths],
        out_shape=[jax.ShapeDtypeStruct((N_TOK, n), F32) for n in widths],
        compiler_params=_cparams("arbitrary"),
        name="norm_proj",
    )(*xs, mod, g.reshape(1, D_MODEL), *ws)


def _post_kernel(*refs, tm, ff_chunk, n_x, n_y):
    x_refs = refs[:n_x]
    m_ref, mod_ref, g_ref, wo_ref, w1_ref, w2_ref = refs[n_x:n_x + 6]
    y_refs = refs[n_x + 6:]
    row = _mod_row(pl.program_id(0), tm)
    mix = _dot(m_ref[...].astype(BF16), wo_ref[...])
    x1 = _stream_load(x_refs, tm) + _mod_slice(mod_ref, row, 2) * mix
    h = _norm_mod(x1, g_ref[...], _mod_slice(mod_ref, row, 4), _mod_slice(mod_ref, row, 3)).astype(BF16)
    acc = jnp.zeros((tm, D_MODEL), F32)
    for k in range(0, D_FF, ff_chunk):
        a = jnp.maximum(_dot(h, w1_ref[:, k:k + ff_chunk]), 0.0)
        acc = acc + _dot((a * a).astype(BF16), w2_ref[k:k + ff_chunk, :])
    y = x1 + _mod_slice(mod_ref, row, 5) * acc
    if n_y == 1:
        y_refs[0][...] = y
    else:
        is_prompt = pl.program_id(0) < N_PROMPT // tm

        @pl.when(is_prompt)
        def _():
            y_refs[0][...] = y

        @pl.when(jnp.logical_not(is_prompt))
        def _():
            y_refs[1][...] = y


def _post_mixer(xs, mixed, mod, g, wo, w1_all, w2_all, layer, split_out=False, tm=512, ff_chunk=1024):
    const = lambda i: (0, 0)
    of_layer = lambda w: pl.BlockSpec((None,) + w.shape[1:], lambda i: (layer, 0, 0), pipeline_mode=pl.Buffered(1))
    n_y = 2 if split_out else 1
    rows = (N_PROMPT, N_SAMPLE) if split_out else (N_TOK,)
    out = pl.pallas_call(
        functools.partial(_post_kernel, tm=tm, ff_chunk=ff_chunk, n_x=len(xs), n_y=n_y),
        grid=(N_TOK // tm,),
        in_specs=_stream_specs(len(xs), tm) + [
            pl.BlockSpec((tm, D_MODEL), lambda i: (i, 0)),
            pl.BlockSpec(mod.shape, const),
            pl.BlockSpec((1, D_MODEL), const),
            pl.BlockSpec(wo.shape, const, pipeline_mode=pl.Buffered(1)),
            of_layer(w1_all),
            of_layer(w2_all),
        ],
        out_specs=_stream_specs(n_y, tm),
        out_shape=[jax.ShapeDtypeStruct((r, D_MODEL), F32) for r in rows],
        compiler_params=_cparams("arbitrary"),
        name="post_mixer",
    )(*xs, mixed, mod, g.reshape(1, D_MODEL), wo, w1_all, w2_all)
    return tuple(out)


PAIR = 2 * HD_C


def _pair_consts():
    lane = lax.broadcasted_iota(jnp.int32, (1, PAIR), 1)
    first = lane < HD_C
    ones_col = [jnp.where(lane == HD_C, 1.0, 0.0), jnp.where(lane == 0, 1.0, 0.0)]
    r = lax.broadcasted_iota(jnp.int32, (2 * PAIR, PAIR), 0) % PAIR
    cidx = lax.broadcasted_iota(jnp.int32, (2 * PAIR, PAIR), 1)
    mean2 = jnp.where(r // HD_C == cidx // HD_C, 1.0 / HD_C, 0.0).astype(BF16)
    return first, ones_col, mean2


def _pair_norm(x, w2, mean2):
    hi, lo = _split2(x * x)
    ms = _dot(jnp.concatenate([hi, lo], axis=1), mean2)
    return x * lax.rsqrt(ms + EPS) * w2


def _pair_queries(q, first):
    return [jnp.where(first, q, 0.0).astype(BF16), jnp.where(first, 0.0, q).astype(BF16)]


def _pair_values(v, first, ones_col):
    return [jnp.where(first, v, ones_col[0]).astype(BF16), jnp.where(first, ones_col[1], v).astype(BF16)]


def _pair_output(o_aug, first):
    den = [o_aug[0][:, HD_C:HD_C + 1], o_aug[1][:, 0:1]]
    return jnp.where(first, o_aug[0] / den[0], o_aug[1] / den[1])


def _row_max(*pieces):
    tiles = [p[:, i:i + 128] for p in pieces for i in range(0, p.shape[1], 128)]
    return jnp.max(functools.reduce(jnp.maximum, tiles), axis=-1, keepdims=True)


CTX_PAIRS = 2


def _ctx_attn_kernel(q_ref, k_ref, v_ref, qn_ref, kn_ref, o_ref, kc_ref, vc_ref):
    first, ones_col, mean2 = _pair_consts()
    lanes = [slice(p * PAIR, (p + 1) * PAIR) for p in range(CTX_PAIRS)]
    qn = [_pair_norm(q_ref[:, ln], qn_ref[...], mean2) * HD_C ** -0.5 for ln in lanes]
    kn = [_pair_norm(k_ref[:, ln], kn_ref[...], mean2) for ln in lanes]
    v = [v_ref[:, ln] for ln in lanes]
    for p in range(CTX_PAIRS):
        for j in range(2):
            kc_ref[0, 0, 2 * p + j] = kn[p][:, j * HD_C:(j + 1) * HD_C]
            vc_ref[0, 0, 2 * p + j] = v[p][:, j * HD_C:(j + 1) * HD_C]
    q = [_pair_queries(x, first) for x in qn]
    va = [_pair_values(x, first, ones_col) for x in v]
    kb = [x.astype(BF16) for x in kn]
    s = [[_dot_nt(q[p][j], kb[p]) for j in range(2)] for p in range(CTX_PAIRS)]
    pr = [[jnp.exp(x - _row_max(x)).astype(BF16) for x in sp] for sp in s]
    for p in range(CTX_PAIRS):
        o_ref[:, lanes[p]] = _pair_output([_dot(pr[p][j], va[p][j]) for j in range(2)], first)


def _ctx_attention(qkv, qn, kn):
    heads = 2 * CTX_PAIRS
    ng = H_C // heads
    wide = CTX_PAIRS * PAIR
    blk = lambda off: pl.BlockSpec((SEQ, wide), lambda b, p: (b, off + p))
    cache_spec = pl.BlockSpec((1, 1, heads, SEQ, HD_C), lambda b, p: (b, 0, p, 0, 0))
    cache_shape = jax.ShapeDtypeStruct((BATCH, 1, H_C, SEQ, HD_C), F32)
    return pl.pallas_call(
        _ctx_attn_kernel,
        grid=(BATCH, ng),
        in_specs=[blk(0), blk(ng), blk(2 * ng),
                  pl.BlockSpec((1, PAIR), lambda b, p: (0, 0)),
                  pl.BlockSpec((1, PAIR), lambda b, p: (0, 0))],
        out_specs=[pl.BlockSpec((SEQ, wide), lambda b, p: (b, p)), cache_spec, cache_spec],
        out_shape=[jax.ShapeDtypeStruct((N_TOK, D_MODEL), F32), cache_shape, cache_shape],
        compiler_params=_cparams("arbitrary", "arbitrary"),
        name="ctx_attention",
    )(qkv, qkv, qkv, jnp.tile(qn.reshape(1, HD_C), (1, 2)), jnp.tile(kn.reshape(1, HD_C), (1, 2)))


def _na_row_start(r):
    return min(max(r - KH // 2, 0), GRID_ROWS - KH)


NA_ROW_GROUP = 4


def _na_attn_kernel(q_ref, k_ref, v_ref, kc_ref, vc_ref, qn_ref, kn_ref, bias_ref, mixed_in_ref, o_ref,
                    qs, ks, vs, bias_s):
    del mixed_in_ref
    first, ones_col, mean2 = _pair_consts()

    @pl.when(pl.program_id(1) == 0)
    def _():
        for j in range(2):
            bias_s[j, 0] = bias_ref[j]
            bias_s[j, 1] = pltpu.roll(bias_ref[j], NA_BIAS_LANES - GRID_W, axis=1)

    q2 = _pair_queries(_pair_norm(q_ref[...], qn_ref[...], mean2) * HD_C ** -0.5, first)
    v2 = _pair_values(v_ref[...], first, ones_col)
    ks[...] = _pair_norm(k_ref[...], kn_ref[...], mean2).astype(BF16)
    for j in range(2):
        qs[j] = q2[j]
        vs[j] = v2[j]
    k_ctx = jnp.concatenate([kc_ref[0, 0, 0], kc_ref[0, 0, 1]], axis=1).astype(BF16)
    v_ctx = _pair_values(jnp.concatenate([vc_ref[0, 0, 0], vc_ref[0, 0, 1]], axis=1), first, ones_col)
    for r0 in range(0, GRID_ROWS, NA_ROW_GROUP):
        units = [(r, j) for r in range(r0, r0 + NA_ROW_GROUP) for j in range(2)]
        rows = {r: slice(r * GRID_W, (r + 1) * GRID_W) for r, _ in units}
        wins = {r: slice(_na_row_start(r) * GRID_W, (_na_row_start(r) + KH) * GRID_W) for r, _ in units}
        s_ctx_all = [_dot_nt(qs[j, r0 * GRID_W:(r0 + NA_ROW_GROUP) * GRID_W, :], k_ctx) for j in range(2)]
        s_ctx = [s_ctx_all[j][(r - r0) * GRID_W:(r - r0 + 1) * GRID_W] for r, j in units]
        s_win = []
        for r, j in units:
            dr0 = KH - 1 - (r - _na_row_start(r))
            lane0 = (dr0 - dr0 % 2) * GRID_W
            s_win.append(_dot_nt(qs[j, rows[r], :], ks[wins[r], :])
                         + bias_s[j, dr0 % 2, :, lane0:lane0 + KH * GRID_W])
        m = [_row_max(a, b) for a, b in zip(s_win, s_ctx)]
        p_win = [jnp.exp(a - mm).astype(BF16) for a, mm in zip(s_win, m)]
        p_ctx = [jnp.exp(b - mm).astype(BF16) for b, mm in zip(s_ctx, m)]
        o_aug = [_dot(p_win[i], vs[j, wins[r], :]) + _dot(p_ctx[i], v_ctx[j]) for i, (r, j) in enumerate(units)]
        for i in range(0, len(units), 2):
            o_ref[rows[units[i][0]], :] = _pair_output(o_aug[i:i + 2], first)


NA_BIAS_LANES = 2 * KH * GRID_W


def _na_bias_table(rpb):
    qc = np.arange(GRID_W)[:, None]
    kc = np.arange(GRID_W)[None, :]
    cstart = np.clip(qc - KW // 2, 0, GRID_W - KW)
    valid = (kc >= cstart) & (kc < cstart + KW)
    onehot = ((kc - qc + KW - 1)[None] == np.arange(2 * KW - 1)[:, None, None]) & valid[None]
    outside = np.tile(np.where(valid, 0.0, NEG_INF).astype(np.float32), (1, 2 * KH))
    rpb16 = jnp.pad(rpb.astype(F32), ((0, 0), (0, 1), (0, 0)))
    t = jnp.einsum('hrd,dqk->hqrk', rpb16, jnp.asarray(onehot, F32), precision=lax.Precision.HIGHEST)
    return t.reshape(H_C, GRID_W, NA_BIAS_LANES) + outside[None]


def _na_attention(qkv, cache_k, cache_v, qn, kn, bias, mixed):
    nhp = H_C // 2
    row0 = N_PROMPT // DEC_SEQ
    blk = lambda off: pl.BlockSpec((DEC_SEQ, 2 * HD_C), lambda p, b: (row0 + b, off + p))
    cache_spec = pl.BlockSpec((1, 1, 2, PAST_LEN, HD_C), lambda p, b: (b, 0, p, 0, 0))
    return pl.pallas_call(
        _na_attn_kernel,
        grid=(nhp, DEC_BATCH),
        in_specs=[blk(0), blk(nhp), blk(2 * nhp), cache_spec, cache_spec,
                  pl.BlockSpec((1, PAIR), lambda p, b: (0, 0)),
                  pl.BlockSpec((1, PAIR), lambda p, b: (0, 0)),
                  pl.BlockSpec((2, GRID_W, NA_BIAS_LANES), lambda p, b: (p, 0, 0)),
                  pl.BlockSpec(memory_space=pl.ANY)],
        out_specs=pl.BlockSpec((DEC_SEQ, 2 * HD_C), lambda p, b: (row0 + b, p)),
        out_shape=jax.ShapeDtypeStruct((N_TOK, D_MODEL), F32),
        input_output_aliases={8: 0},
        scratch_shapes=[pltpu.VMEM((2, DEC_SEQ, PAIR), BF16), pltpu.VMEM((DEC_SEQ, PAIR), BF16),
                        pltpu.VMEM((2, DEC_SEQ, PAIR), BF16), pltpu.VMEM((2, 2, GRID_W, NA_BIAS_LANES), F32)],
        compiler_params=_cparams("arbitrary", "arbitrary"),
        name="na_attention",
    )(qkv, qkv, qkv, cache_k, cache_v, jnp.tile(qn.reshape(1, HD_C), (1, 2)), jnp.tile(kn.reshape(1, HD_C), (1, 2)),
      bias, mixed)


def _seq_layout(prompt):
    return (SEQ, BATCH, 0) if prompt else (DEC_SEQ, DEC_BATCH, N_PROMPT // DEC_SEQ)


def _flip_blocks(m, c):
    r, s = m.shape
    return m.reshape(r // c, c, s // c, c)[:, ::-1, :, ::-1].reshape(r, s)


def _rms_gate(x, gn, gate):
    ms = jnp.mean(x * x, axis=-1, keepdims=True)
    return x * lax.rsqrt(ms + EPS) * gn * _silu(gate)


HG_LEVELS = tuple(CHUNK >> (i + 1) for i in range(CHUNK.bit_length() - 1))
HG_NL = len(HG_LEVELS)
HG_STACK = (HG_NL + 1) * CHUNK
TOT_ROWS = 16
HG_ROWS = (HG_NL + 2) * CHUNK + TOT_ROWS


def _hgrn_consts():
    c = CHUNK
    level_rows = []
    mask = np.zeros((HG_STACK, HG_STACK), np.float32)
    mask[:c, :c] = np.eye(c)
    for li, b in enumerate(HG_LEVELS):
        m = np.zeros((c, c), np.float32)
        blk = np.zeros((c, c), np.float32)
        for t in range(c):
            mid = (t // (2 * b)) * 2 * b + b
            if t >= mid:
                m[t, mid:t + 1] = 1.0
                blk[t, mid - b:mid] = 1.0
            else:
                m[t, t + 1:mid] = 1.0
        level_rows.append(m)
        mask[(li + 1) * c:(li + 2) * c, (li + 1) * c:(li + 2) * c] = blk
    dq = np.tril(np.ones((c, c), np.float32))
    dk = np.triu(np.ones((c, c), np.float32), 1)
    body = np.concatenate(level_rows + [dq, dk], axis=0)
    tot = np.ones((TOT_ROWS, c), np.float32)
    mcs, masks = [], []
    for reverse in (False, True):
        bm = _flip_blocks(body, c) if reverse else body
        mk = _flip_blocks(mask, c) if reverse else mask
        mc = np.concatenate([bm, tot], axis=0)
        mcs.append(np.concatenate([mc, mc], axis=1))
        masks.append(mk)
    return jnp.asarray(np.stack(mcs), BF16), jnp.asarray(np.stack(masks), F32)


HG_FAST = 64
HG_HALF = HG_FAST // 2
HG_FAST_ROWS = 4 * HG_FAST + TOT_ROWS
HG_SAFE_EXP = 40.0


def _hgrn_fast_consts(seq):
    c, m = HG_FAST, HG_HALF
    aq = np.zeros((c, c), np.float32)
    for t in range(c):
        if t >= m:
            aq[t, m:t + 1] = 1.0
        else:
            aq[t, t + 1:m] = -1.0
    dq = np.tril(np.ones((c, c), np.float32))
    dk = np.triu(np.ones((c, c), np.float32), 1)
    body = np.concatenate([aq, -aq, dq, dk], axis=0)
    tot = np.ones((TOT_ROWS, c), np.float32)
    causal = np.tril(np.ones((c, c), np.float32))
    mfs, masks = [], []
    for reverse in (False, True):
        bm = _flip_blocks(body, c) if reverse else body
        mf = np.concatenate([bm, tot], axis=0)
        mfs.append(np.concatenate([mf, mf], axis=1))
        masks.append(causal.T if reverse else causal)
    n_half = seq // m
    half = np.zeros((max(n_half, 16), seq), np.float32)
    for i in range(n_half):
        half[i, i * m:(i + 1) * m] = 1.0
    return jnp.asarray(np.stack(mfs), BF16), jnp.asarray(np.stack(masks), F32), jnp.asarray(half, BF16)


def _hgrn_kernel(*refs, seq, has_s0, emit_state, aliased):
    it = iter(refs)
    qa_ref, ff_ref, fb_ref, ia_ref, ga_ref, lb_ref, gn_ref, mc_ref, mask_ref = [next(it) for _ in range(9)]
    mf_ref, causal_ref, half_ref = [next(it) for _ in range(3)]
    s0_ref = next(it) if has_s0 else None
    if aliased:
        next(it)
    o_ref = next(it)
    st_ref = next(it) if emit_state else None
    s_scr, acc, f_s, lf_s = [next(it) for _ in range(4)]
    c = CHUNK
    n_chunks = seq // c
    combos = [(d, h) for d in range(2) for h in range(H_A)]
    lanes = [slice(h * DK_A, (h + 1) * DK_A) for h in range(H_A)]
    add = lambda a, b: a + b

    lb_raw = lb_ref[...]
    lb_e = jnp.exp(lb_raw - jnp.max(lb_raw, axis=0, keepdims=True))
    lb_all = lb_e[0:1] / jnp.sum(lb_e, axis=0, keepdims=True)

    for d in range(2):
        for h in range(H_A):
            s_scr[d, h] = s0_ref[0, 0, d, h].T if has_s0 else jnp.zeros((DV_A, DK_A), F32)
    acc[...] = jnp.zeros(acc.shape, F32)

    worst = []
    for d, fr_ref in enumerate((ff_ref, fb_ref)):
        f = lb_all + (1.0 - lb_all) * _sigmoid(fr_ref[...])
        lf = jnp.log(f)
        f_s[d] = f
        lf_s[d] = lf
        worst.append(jnp.max(_dot(half_ref[...], (-lf).astype(BF16))))
    safe = jnp.maximum(worst[0], worst[1]) <= HG_SAFE_EXP

    def fast_body(n, carry):
        cf = HG_FAST
        n_fast = seq // cf
        rows = [pl.ds(pl.multiple_of((n if d == 0 else n_fast - 1 - n) * cf, cf), cf) for d in range(2)]
        e_all = [jnp.exp(_dot_const(mf_ref[d], lf_s[d, rows[d], :])) for d in range(2)]
        q_all = [_silu(qa_ref[rows[d], :]) * DK_A ** -0.5 for d in range(2)]
        k_all = [1.0 - f_s[d, rows[d], :] for d in range(2)]
        v_all = [ia_ref[rows[d], :].astype(BF16) for d in range(2)]
        st = [s_scr[d, h] for d, h in combos]
        qs = [q_all[d][:, lanes[h]] for d, h in combos]
        ks = [k_all[d][:, lanes[h]] for d, h in combos]
        vs = [v_all[d][:, lanes[h]] for d, h in combos]
        es = [[e_all[d][i * cf:(i + 1) * cf, lanes[h]] for i in range(4)] for d, h in combos]
        p = [jnp.where(causal_ref[d] > 0.0,
                       _dot_nt((qs[i] * es[i][0]).astype(BF16), (ks[i] * es[i][1]).astype(BF16)), 0.0).astype(BF16)
             for i, (d, h) in enumerate(combos)]
        o = [_dot(p[i], vs[i]) + _dot_nt((qs[i] * es[i][2]).astype(BF16), st[i].astype(BF16))
             for i in range(len(combos))]
        upd = [_dot_tn(vs[i], (ks[i] * es[i][3]).astype(BF16)) for i in range(len(combos))]
        for d in range(2):
            acc[rows[d], :] += jnp.concatenate(o[d * H_A:(d + 1) * H_A], axis=1)
        for i, (d, h) in enumerate(combos):
            s_scr[d, h] = st[i] * e_all[d][4 * cf:4 * cf + 1, lanes[h]] + upd[i]
        return carry

    def body(n, carry):
        rows = [pl.ds(pl.multiple_of((n if d == 0 else n_chunks - 1 - n) * c, c), c) for d in range(2)]
        f_all = [f_s[d, rows[d], :] for d in range(2)]
        e_all = [jnp.exp(_dot_const(mc_ref[d], lf_s[d, rows[d], :])) for d in range(2)]
        q_all = [_silu(qa_ref[rows[d], :]) * DK_A ** -0.5 for d in range(2)]
        v_all = [ia_ref[rows[d], :].astype(BF16) for d in range(2)]
        st = [s_scr[d, h] for d, h in combos]
        qs, ks, vs, es = [], [], [], []
        for d, h in combos:
            qs.append(q_all[d][:, lanes[h]])
            ks.append(1.0 - f_all[d][:, lanes[h]])
            vs.append(v_all[d][:, lanes[h]])
            es.append(e_all[d][:, lanes[h]])
        lvl = [[e[i * c:(i + 1) * c] for i in range(HG_NL + 2)] for e in es]
        qst = [jnp.concatenate([q] + [q * l[i] for i in range(HG_NL)], axis=0).astype(BF16) for q, l in zip(qs, lvl)]
        kst = [jnp.concatenate([k] + [k * l[i] for i in range(HG_NL)], axis=0).astype(BF16) for k, l in zip(ks, lvl)]
        r = [(_dot_nt(qst[i], kst[i]) * mask_ref[d]).astype(BF16) for i, (d, h) in enumerate(combos)]
        ost = [_dot(r[i], jnp.concatenate([vs[i]] * (HG_NL + 1), axis=0)) for i in range(len(combos))]
        inter = [_dot_nt((qs[i] * lvl[i][HG_NL]).astype(BF16), st[i].astype(BF16)) for i in range(len(combos))]
        upd = [_dot_tn(vs[i], (ks[i] * lvl[i][HG_NL + 1]).astype(BF16)) for i in range(len(combos))]
        o = [functools.reduce(lambda a, b: a + b, [ost[i][j * c:(j + 1) * c] for j in range(HG_NL + 1)]) + inter[i]
             for i in range(len(combos))]
        for d in range(2):
            acc[rows[d], :] += jnp.concatenate(o[d * H_A:(d + 1) * H_A], axis=1)
        for i, (d, h) in enumerate(combos):
            e_tot = es[i][(HG_NL + 2) * c:(HG_NL + 2) * c + 1]
            s_scr[d, h] = st[i] * e_tot + upd[i]
        return carry

    @pl.when(safe)
    def _():
        lax.fori_loop(0, seq // HG_FAST, fast_body, 0)

    @pl.when(jnp.logical_not(safe))
    def _():
        lax.fori_loop(0, n_chunks, body, 0)

    for h in range(H_A):
        ln = slice(h * DV_A, (h + 1) * DV_A)
        o_ref[:, ln] = _rms_gate(acc[:, ln], gn_ref[...], ga_ref[:, ln])
    if emit_state:
        for d in range(2):
            for h in range(H_A):
                st_ref[0, 0, d, h] = s_scr[d, h].T


def _hgrn(proj, hgrn_lb, gn, consts, prompt, s0=None, layer=0, mixed=None):
    seq, nb, rb0 = _seq_layout(prompt)
    consts = list(consts) + list(_hgrn_fast_consts(seq))
    wa = H_A * DK_A
    blk = lambda j: pl.BlockSpec((seq, wa), lambda b: (rb0 + b, j))
    const2 = lambda b: (0, 0)
    st_block = (1, 1, 2, H_A, DK_A, DV_A)
    in_specs = [blk(0), blk(1), blk(2), blk(3), blk(4),
                pl.BlockSpec(hgrn_lb.shape, const2), pl.BlockSpec((1, DV_A), const2)]
    in_specs += [pl.BlockSpec(m.shape, lambda b, nd=m.ndim: (0,) * nd) for m in consts]
    args = [proj] * 5 + [hgrn_lb, gn.reshape(1, DV_A)] + consts
    if s0 is not None:
        in_specs.append(pl.BlockSpec(st_block, lambda b: (b, layer, 0, 0, 0, 0)))
        args.append(s0)
    aliases = {}
    if mixed is not None:
        aliases = {len(args): 0}
        in_specs.append(pl.BlockSpec(memory_space=pl.ANY))
        args.append(mixed)
    out_specs = [pl.BlockSpec((seq, wa), lambda b: (rb0 + b, 0))]
    out_shape = [jax.ShapeDtypeStruct((N_TOK, D_MODEL), F32)]
    if prompt:
        out_specs.append(pl.BlockSpec(st_block, lambda b: (b, 0, 0, 0, 0, 0)))
        out_shape.append(jax.ShapeDtypeStruct((nb, 1, 2, H_A, DK_A, DV_A), F32))
    return pl.pallas_call(
        functools.partial(_hgrn_kernel, seq=seq, has_s0=s0 is not None, emit_state=prompt, aliased=mixed is not None),
        grid=(nb,),
        in_specs=in_specs,
        out_specs=out_specs,
        out_shape=out_shape,
        input_output_aliases=aliases,
        scratch_shapes=[pltpu.VMEM((2, H_A, DV_A, DK_A), F32), pltpu.VMEM((seq, wa), F32),
                        pltpu.VMEM((2, seq, wa), F32), pltpu.VMEM((2, seq, wa), F32)],
        compiler_params=_cparams("arbitrary"),
        name="hgrn_prompt" if prompt else "hgrn_sample",
    )(*args)


GD_SUB = GBLK // CHUNK
GD_ROWS = 2 * GBLK + TOT_ROWS


def _gdn_consts():
    n, c = GBLK, CHUNK
    same = (np.arange(n)[:, None] // c) == (np.arange(n)[None, :] // c)
    tri = (same & (np.arange(n)[None, :] <= np.arange(n)[:, None])).astype(np.float32)
    sup = (same & (np.arange(n)[None, :] > np.arange(n)[:, None])).astype(np.float32)
    tot = np.zeros((TOT_ROWS, n), np.float32)
    for s in range(GD_SUB):
        tot[s, s * c:(s + 1) * c] = 1.0
    mgs, tts, tris = [], [], []
    for reverse in (False, True):
        t = _flip_blocks(tri, c) if reverse else tri
        s = _flip_blocks(sup, c) if reverse else sup
        mg = np.concatenate([t, s, tot], axis=0)
        mgs.append(np.concatenate([mg, mg], axis=1))
        tts.append(np.concatenate([t.T, t.T], axis=0))
        tris.append(t)
    tris.append(same.astype(np.float32))
    return jnp.asarray(np.stack(mgs), BF16), jnp.asarray(np.stack(tts), BF16), jnp.asarray(np.stack(tris), F32)


def _softplus(x):
    return jnp.maximum(x, 0.0) + jnp.log(1.0 + jnp.exp(-jnp.abs(x)))


def _conv_silu(x, w, seq):
    t_idx = lax.broadcasted_iota(jnp.int32, (seq, 1), 0)
    half = SHORT_CONV // 2
    acc = x * w[half:half + 1]
    for j in range(SHORT_CONV):
        shift = half - j
        if shift == 0:
            continue
        src = t_idx - shift
        xr = pltpu.roll(x, shift % seq, axis=0)
        acc = acc + jnp.where((src >= 0) & (src < seq), xr, 0.0) * w[j:j + 1]
    return _silu(acc)


def _l2norm_heads(x, n_heads, width, scale):
    outs = []
    for h in range(n_heads):
        xh = x[:, h * width:(h + 1) * width]
        outs.append(xh * (lax.rsqrt(jnp.sum(xh * xh, axis=-1, keepdims=True) + EPS) * scale))
    return jnp.concatenate(outs, axis=-1)


def _gdn_kernel(*refs, seq, has_s0, emit_state, aliased):
    it = iter(refs)
    (q_ref, k_ref, v_ref, gb_ref, gate_ref, cw_ref, alog_ref, dt_ref, gn_ref,
     mg_ref, tt_ref, tri_ref) = [next(it) for _ in range(12)]
    s0_ref = next(it) if has_s0 else None
    if aliased:
        next(it)
    o_ref = next(it)
    st_ref = next(it) if emit_state else None
    qn, kn, vn, u_s, w_s, qg_s, kdt_s, at_s, et_s, s_scr, acc = [next(it) for _ in range(11)]
    c = CHUNK
    n_chunks = seq // c
    n_blocks = seq // GBLK
    wq = H_B * DK_B
    n_dh = 2 * H_B
    combos = [(d, h) for d in range(2) for h in range(H_B)]
    lanes = [slice(h * DK_B, (h + 1) * DK_B) for h in range(H_B)]

    qn[...] = _l2norm_heads(_conv_silu(q_ref[...], cw_ref[:, 0:wq], seq), H_B, DK_B, DK_B ** -0.5)
    kn[...] = _l2norm_heads(_conv_silu(k_ref[...], cw_ref[:, wq:2 * wq], seq), H_B, DK_B, 1.0)
    vn[...] = _conv_silu(v_ref[...], cw_ref[:, 2 * wq:3 * wq], seq)
    for i in range(2 * H_B):
        s_scr[i] = s0_ref[0, 0, i // H_B, i % H_B] if has_s0 else jnp.zeros((DK_B, DV_B), F32)
    acc[...] = jnp.zeros(acc.shape, F32)

    eye = (lax.broadcasted_iota(jnp.int32, (GBLK, GBLK), 0)
           == lax.broadcasted_iota(jnp.int32, (GBLK, GBLK), 1)).astype(F32)
    eye_pk = (lax.broadcasted_iota(jnp.int32, (c, GBLK), 0)
              == lax.broadcasted_iota(jnp.int32, (c, GBLK), 1) % c).astype(F32)
    bwd_lane = lax.broadcasted_iota(jnp.int32, (1, 128), 1) % n_dh >= H_B
    add = lambda a, b: a + b

    def expand(pk):
        return jnp.concatenate([pk] * GD_SUB, axis=0) * tri_ref[2]

    def pack(bd):
        return functools.reduce(add, [bd[s * c:(s + 1) * c] for s in range(GD_SUB)])

    def dot3_split(a, b_hi, b_lo):
        ah, al = _split2(a)
        t = _dot(jnp.concatenate([ah, al], axis=0), b_hi)
        return t[:a.shape[0]] + t[a.shape[0]:] + _dot(ah, b_lo)

    def block_body(blk, carry):
        rows = pl.ds(pl.multiple_of(blk * GBLK, GBLK), GBLK)
        gates = gate_ref[rows, :]
        glog_all = -jnp.exp(alog_ref[...]) * _softplus(gates + dt_ref[...])
        beta_all = _sigmoid(gates)
        g2 = jnp.concatenate(_split2(glog_all), axis=0)
        dg = [_dot(mg_ref[d], g2) for d in range(2)]
        dsel = jnp.where(bwd_lane, dg[1], dg[0])
        eg_all = jnp.exp(dsel)
        g_all = dsel[:GBLK]
        gt = [_dot_tn(g2, tt_ref[d]) for d in range(2)]
        qs = [qn[rows, ln] for ln in lanes]
        ks = [kn[rows, ln] for ln in lanes]
        vs = [vn[rows, ln] for ln in lanes]
        col = lambda x, j: jnp.broadcast_to(x[:, j:j + 1], (GBLK, DK_B))
        betas = [col(beta_all, n_dh + i) for i in range(n_dh)]
        kbs = [ks[h] * betas[i] for i, (d, h) in enumerate(combos)]
        kk = [_dot_nt(jnp.concatenate([qs[h], kbs[h], kbs[H_B + h]], axis=0).astype(BF16), ks[h].astype(BF16))
              for h in range(H_B)]
        decay = []
        for i, (d, h) in enumerate(combos):
            inside = tri_ref[d] > 0.0
            gd = col(g_all, i) - gt[d][i:i + 1, :]
            decay.append(jnp.where(inside, jnp.exp(jnp.where(inside, gd, 0.0)), 0.0))
        attn = [kk[h][:GBLK] * decay[i] for i, (d, h) in enumerate(combos)]
        p_pk = [pack(kk[h][(1 + d) * GBLK:(2 + d) * GBLK] * decay[i] * (1.0 - eye)) for i, (d, h) in enumerate(combos)]
        x_pk = [eye_pk - p for p in p_pk]
        p_bd = [_split2(expand(p)) for p in p_pk]
        for _ in range(CHUNK.bit_length() - 2):
            p_pk = [dot3_split(p, *b) for p, b in zip(p_pk, p_bd)]
            p_bd = [_split2(expand(p)) for p in p_pk]
            x_pk = [x + dot3_split(x, *b) for x, b in zip(x_pk, p_bd)]
        eg_col = [col(eg_all[:GBLK], i) for i in range(n_dh)]
        ekd_col = [col(eg_all[GBLK:2 * GBLK], i) for i in range(n_dh)]
        rhs = [jnp.concatenate([vs[h] * betas[i], kbs[i] * eg_col[i]], axis=1) for i, (d, h) in enumerate(combos)]
        uw = [_dot3(expand(x_pk[i]), rhs[i]) for i in range(n_dh)]
        for i, (d, h) in enumerate(combos):
            qg = (qs[h] * eg_col[i]).astype(BF16)
            kdt_s[i, blk] = (ks[h] * ekd_col[i]).T.astype(BF16)
            for s in range(GD_SUB):
                cn = blk * GD_SUB + s
                r = slice(s * c, (s + 1) * c)
                u_s[i, cn] = uw[i][r, :DV_B]
                w_s[i, cn] = uw[i][r, DV_B:].astype(BF16)
                qg_s[i, cn] = qg[r]
                at_s[i, cn] = attn[i][r].astype(BF16)
                et_s[i, cn] = jnp.broadcast_to(eg_all[2 * GBLK + s:2 * GBLK + s + 1, i:i + 1], (8, DV_B))
        return carry

    lax.fori_loop(0, n_blocks, block_body, 0)

    def chunk_body(n, carry):
        cns = [n, n_chunks - 1 - n]
        rows = [pl.ds(pl.multiple_of(cn * c, c), c) for cn in cns]
        sub_of_row = lax.broadcasted_iota(jnp.int32, (GBLK, 1), 0) // c
        in_chunk = [sub_of_row == cn % GD_SUB for cn in cns]
        st = [s_scr[i] for i in range(n_dh)]
        ws = [_dot(jnp.concatenate([w_s[i, cns[d]], qg_s[i, cns[d]]], axis=0), st[i].astype(BF16))
              for i, (d, h) in enumerate(combos)]
        vblk = [jnp.where(in_chunk[d], jnp.concatenate([u_s[i, cns[d]] - ws[i][:c]] * GD_SUB, axis=0), 0.0).astype(BF16)
                for i, (d, h) in enumerate(combos)]
        r = [_dot(jnp.concatenate([at_s[i, cns[d]], kdt_s[i, cns[d] // GD_SUB]], axis=0), vblk[i])
             for i, (d, h) in enumerate(combos)]
        for d in range(2):
            acc[rows[d], :] += jnp.concatenate([ws[i][c:] + r[i][:c] for i in range(d * H_B, (d + 1) * H_B)], axis=1)
        for i, (d, h) in enumerate(combos):
            s_scr[i] = st[i] * et_s[i, cns[d]][0:1] + r[i][c:]
        return carry

    lax.fori_loop(0, n_chunks, chunk_body, 0)

    for h in range(H_B):
        ln = slice(h * DV_B, (h + 1) * DV_B)
        o_ref[:, ln] = _rms_gate(acc[:, ln], gn_ref[...], gb_ref[:, ln])
    if emit_state:
        for i in range(2 * H_B):
            st_ref[0, 0, i // H_B, i % H_B] = s_scr[i]


def _gdn(proj, gates, conv_w, a_log, dt_bias, gn, consts, prompt, mixed, s0=None, layer=0):
    seq, nb, rb0 = _seq_layout(prompt)
    n_chunks = seq // CHUNK
    wq = H_B * DK_B
    blk = lambda j: pl.BlockSpec((seq, wq), lambda b: (rb0 + b, j))
    const2 = lambda b: (0, 0)
    const3 = lambda b: (0, 0, 0)
    st_block = (1, 1, 2, H_B, DK_B, DV_B)
    pad_row = lambda p: jnp.pad(p.reshape(1, -1).astype(F32), ((0, 0), (0, 128 - p.size)))
    in_specs = [blk(5), blk(6), blk(7), blk(8),
                pl.BlockSpec((seq, 128), lambda b: (rb0 + b, 0)),
                pl.BlockSpec((SHORT_CONV, 3 * wq), const2),
                pl.BlockSpec((1, 128), const2), pl.BlockSpec((1, 128), const2), pl.BlockSpec((1, DV_B), const2)]
    in_specs += [pl.BlockSpec(m.shape, const3) for m in consts]
    args = [proj] * 4 + [gates, conv_w.reshape(SHORT_CONV, 3 * wq), pad_row(a_log), pad_row(dt_bias),
                         gn.reshape(1, DV_B)] + list(consts)
    if s0 is not None:
        in_specs.append(pl.BlockSpec(st_block, lambda b: (b, layer, 0, 0, 0, 0)))
        args.append(s0)
    aliases = {len(args): 0}
    in_specs.append(pl.BlockSpec(memory_space=pl.ANY))
    args.append(mixed)
    out_specs = [pl.BlockSpec((seq, wq), lambda b: (rb0 + b, 1))]
    out_shape = [jax.ShapeDtypeStruct((N_TOK, D_MODEL), F32)]
    if prompt:
        out_specs.append(pl.BlockSpec(st_block, lambda b: (b, 0, 0, 0, 0, 0)))
        out_shape.append(jax.ShapeDtypeStruct((nb, 1, 2, H_B, DK_B, DV_B), F32))
    n_dh = 2 * H_B
    scratch = ([pltpu.VMEM((seq, wq), F32)] * 3
               + [pltpu.VMEM((n_dh, n_chunks, CHUNK, DV_B), F32)]
               + [pltpu.VMEM((n_dh, n_chunks, CHUNK, DK_B), BF16)] * 2
               + [pltpu.VMEM((n_dh, seq // GBLK, DK_B, GBLK), BF16),
                  pltpu.VMEM((n_dh, n_chunks, CHUNK, GBLK), BF16),
                  pltpu.VMEM((n_dh, n_chunks, 8, DV_B), F32),
                  pltpu.VMEM((n_dh, DK_B, DV_B), F32),
                  pltpu.VMEM((seq, wq), F32)])
    return pl.pallas_call(
        functools.partial(_gdn_kernel, seq=seq, has_s0=s0 is not None, emit_state=prompt, aliased=True),
        grid=(nb,),
        in_specs=in_specs,
        out_specs=out_specs,
        out_shape=out_shape,
        input_output_aliases=aliases,
        scratch_shapes=scratch,
        compiler_params=_cparams("arbitrary"),
        name="gdn_prompt" if prompt else "gdn_sample",
    )(*args)


def kernel(x_prompt, x_sample, state_hgrn, state_gdn, cache_na_k, cache_na_v, c, c_ctx, ada_w, ada_b, norm_g, w_in_ab, w_out_ab, hgrn_lb, gdn_conv, gdn_a_log, gdn_dt_bias, gn_hgrn, gn_gdn, w_qkv_na, qn_na, kn_na, rpb_na, w_out_na, w_mlp1, w_mlp2):
    cond = jnp.concatenate([c_ctx[None, :], c, jnp.zeros((N_MOD_ROWS - 1 - DEC_BATCH, D_MODEL), F32)], axis=0)
    mods = _modulation(cond, ada_w, ada_b)
    xs = (x_prompt.reshape(N_PROMPT, D_MODEL), x_sample.reshape(N_SAMPLE, D_MODEL))

    w_in = w_in_ab[0].astype(BF16)
    w_gate = jnp.pad(w_in[:, D_MAIN_AB:], ((0, 0), (0, 128 - N_GATE_AB)))
    proj, gates = _norm_proj(xs, mods[0], norm_g[0, 0], [w_in, w_gate], widths=[D_MAIN_AB, 128])
    w_mlp1_bf, w_mlp2_bf = w_mlp1.astype(BF16), w_mlp2.astype(BF16)
    hg_consts = _hgrn_consts()
    gd_consts = _gdn_consts()
    mixed, new_hgrn = _hgrn(proj, hgrn_lb, gn_hgrn[0], hg_consts, True)
    mixed, = _hgrn(proj, hgrn_lb, gn_hgrn[0], hg_consts, False, s0=state_hgrn, mixed=mixed)
    gd_args = (gdn_conv[0], gdn_a_log[0], gdn_dt_bias[0], gn_gdn[0], gd_consts)
    mixed, new_gdn = _gdn(proj, gates, *gd_args, True, mixed)
    mixed, = _gdn(proj, gates, *gd_args, False, mixed, s0=state_gdn)
    xs = _post_mixer(xs, mixed, mods[0], norm_g[0, 1], w_out_ab[0].astype(BF16), w_mlp1_bf, w_mlp2_bf, 0)

    qkv, = _norm_proj(xs, mods[1], norm_g[1, 0], [w_qkv_na[0].astype(BF16)])
    mixed, new_k, new_v = _ctx_attention(qkv, qn_na[0], kn_na[0])
    mixed = _na_attention(qkv, cache_na_k, cache_na_v, qn_na[0], kn_na[0], _na_bias_table(rpb_na[0]), mixed)
    y_prompt, y_sample = _post_mixer(xs, mixed, mods[1], norm_g[1, 1], w_out_na[0].astype(BF16),
                                     w_mlp1_bf, w_mlp2_bf, 1, split_out=True)

    return (y_prompt.reshape(BATCH, SEQ, D_MODEL), y_sample.reshape(DEC_BATCH, DEC_SEQ, D_MODEL),
            new_hgrn, new_gdn, new_k, new_v)
```

```python
import functools

import numpy as np
import jax
import jax.numpy as jnp
from jax import lax
from jax.experimental import pallas as pl
from jax.experimental.pallas import tpu as pltpu

F32 = jnp.float32
BF16 = jnp.bfloat16

D_MODEL = 1024
BATCH = 16
SEQ = 256
DEC_BATCH = 4
DEC_SEQ = 1024
PAST_LEN = 256
N_PROMPT = BATCH * SEQ
N_SAMPLE = DEC_BATCH * DEC_SEQ
N_TOK = N_PROMPT + N_SAMPLE
GRID_W = 64
GRID_ROWS = DEC_SEQ // GRID_W
H_A = 4
DK_A = 128
DV_A = 128
H_B = 4
DK_B = 128
DV_B = 128
SHORT_CONV = 5
H_C = 16
HD_C = 64
KH = 8
KW = 16
D_FF = 4 * D_MODEL
EPS = 1e-6
NEG_INF = -1e30
N_MOD_ROWS = 8
D_MAIN_AB = 4608
N_GATE_AB = 16
CHUNK = 32
GBLK = 128
VMEM_LIMIT = 56 * 1024 * 1024


def _cparams(*sem):
    return pltpu.CompilerParams(dimension_semantics=sem, vmem_limit_bytes=VMEM_LIMIT)


def _sigmoid(x):
    return 1.0 / (1.0 + jnp.exp(-x))


def _silu(x):
    return x * _sigmoid(x)


def _dot(a, b):
    return jnp.dot(a, b, preferred_element_type=F32)


def _dot_nt(a, b):
    return lax.dot_general(a, b, (((1,), (1,)), ((), ())), preferred_element_type=F32)


def _dot_tn(a, b):
    return lax.dot_general(a, b, (((0,), (0,)), ((), ())), preferred_element_type=F32)


def _split2(x):
    hi = x.astype(BF16)
    lo = (x - hi.astype(F32)).astype(BF16)
    return hi, lo


def _dot_const(m2, x):
    hi, lo = _split2(x)
    return _dot(m2, jnp.concatenate([hi, lo], axis=0))


def _dot3(a, b):
    ah, al = _split2(a)
    bh, bl = _split2(b)
    return _dot(ah, bh) + (_dot(ah, bl) + _dot(al, bh))


def _mod_row(i, tm):
    start = i * tm
    return jnp.where(start < N_PROMPT, 0, 1 + (start - N_PROMPT) // DEC_SEQ)


def _mod_slice(mod_ref, row, k):
    return mod_ref[pl.ds(row, 1), k * D_MODEL:(k + 1) * D_MODEL]


def _norm_mod(x, g, sc, sh):
    ms = jnp.mean(x * x, axis=-1, keepdims=True)
    return (x * lax.rsqrt(ms + EPS) * g) * (1.0 + sc) + sh


def _mod_kernel(cond_ref, w_ref, b_ref, o_ref):
    s = _silu(cond_ref[...]).astype(BF16)
    o_ref[0] = _dot(s, w_ref[0].astype(BF16)) + b_ref[0]


def _modulation(cond8, ada_w, ada_b):
    depth = ada_w.shape[0]
    tn = 1024
    nj = ada_w.shape[2] // tn
    return pl.pallas_call(
        _mod_kernel,
        grid=(depth, nj),
        in_specs=[
            pl.BlockSpec((N_MOD_ROWS, D_MODEL), lambda l, j: (0, 0)),
            pl.BlockSpec((1, D_MODEL, tn), lambda l, j: (l, 0, j)),
            pl.BlockSpec((1, 1, tn), lambda l, j: (l, 0, j)),
        ],
        out_specs=pl.BlockSpec((1, N_MOD_ROWS, tn), lambda l, j: (l, 0, j)),
        out_shape=jax.ShapeDtypeStruct((depth, N_MOD_ROWS, ada_w.shape[2]), F32),
        compiler_params=_cparams("arbitrary", "arbitrary"),
        name="modulation",
    )(cond8, ada_w, ada_b.reshape(depth, 1, -1))


def _stream_specs(n_arrays, tm):
    if n_arrays == 1:
        return [pl.BlockSpec((tm, D_MODEL), lambda i: (i, 0))]
    npt = N_PROMPT // tm
    return [pl.BlockSpec((tm, D_MODEL), lambda i: (jnp.minimum(i, npt - 1), 0)),
            pl.BlockSpec((tm, D_MODEL), lambda i: (jnp.maximum(i - npt, 0), 0))]


def _stream_load(x_refs, tm):
    if len(x_refs) == 1:
        return x_refs[0][...]
    return jnp.where(pl.program_id(0) < N_PROMPT // tm, x_refs[0][...], x_refs[1][...])


def _norm_proj_kernel(*refs, tm, n_x, n_w):
    x_refs, (mod_ref, g_ref) = refs[:n_x], refs[n_x:n_x + 2]
    w_refs, o_refs = refs[n_x + 2:n_x + 2 + n_w], refs[n_x + 2 + n_w:]
    row = _mod_row(pl.program_id(0), tm)
    h = _norm_mod(_stream_load(x_refs, tm), g_ref[...], _mod_slice(mod_ref, row, 1), _mod_slice(mod_ref, row, 0)).astype(BF16)
    for w_ref, o_ref in zip(w_refs, o_refs):
        o_ref[...] = _dot(h, w_ref[...])


def _norm_proj(xs, mod, g, ws, widths=None, tm=512):
    n_w = len(ws)
    widths = widths or [w.shape[1] for w in ws]
    const = lambda i: (0, 0)
    return pl.pallas_call(
        functools.partial(_norm_proj_kernel, tm=tm, n_x=len(xs), n_w=n_w),
        grid=(N_TOK // tm,),
        in_specs=_stream_specs(len(xs), tm) + [
            pl.BlockSpec(mod.shape, const),
            pl.BlockSpec((1, D_MODEL), const),
        ] + [pl.BlockSpec((D_MODEL, n), const, pipeline_mode=pl.Buffered(1)) for n in widths],
        out_specs=[pl.BlockSpec((tm, n), lambda i: (i, 0)) for n in widths],
        out_shape=[jax.ShapeDtypeStruct((N_TOK, n), F32) for n in widths],
        compiler_params=_cparams("arbitrary"),
        name="norm_proj",
    )(*xs, mod, g.reshape(1, D_MODEL), *ws)


def _post_kernel(*refs, tm, ff_chunk, n_x, n_y):
    x_refs = refs[:n_x]
    m_ref, mod_ref, g_ref, wo_ref, w1_ref, w2_ref = refs[n_x:n_x + 6]
    y_refs = refs[n_x + 6:]
    row = _mod_row(pl.program_id(0), tm)
    mix = _dot(m_ref[...].astype(BF16), wo_ref[...])
    x1 = _stream_load(x_refs, tm) + _mod_slice(mod_ref, row, 2) * mix
    h = _norm_mod(x1, g_ref[...], _mod_slice(mod_ref, row, 4), _mod_slice(mod_ref, row, 3)).astype(BF16)
    acc = jnp.zeros((tm, D_MODEL), F32)
    for k in range(0, D_FF, ff_chunk):
        a = jnp.maximum(_dot(h, w1_ref[:, k:k + ff_chunk]), 0.0)
        acc = acc + _dot((a * a).astype(BF16), w2_ref[k:k + ff_chunk, :])
    y = x1 + _mod_slice(mod_ref, row, 5) * acc
    if n_y == 1:
        y_refs[0][...] = y
    else:
        is_prompt = pl.program_id(0) < N_PROMPT // tm

        @pl.when(is_prompt)
        def _():
            y_refs[0][...] = y

        @pl.when(jnp.logical_not(is_prompt))
        def _():
            y_refs[1][...] = y


def _post_mixer(xs, mixed, mod, g, wo, w1_all, w2_all, layer, split_out=False, tm=512, ff_chunk=1024):
    const = lambda i: (0, 0)
    of_layer = lambda w: pl.BlockSpec((None,) + w.shape[1:], lambda i: (layer, 0, 0), pipeline_mode=pl.Buffered(1))
    n_y = 2 if split_out else 1
    rows = (N_PROMPT, N_SAMPLE) if split_out else (N_TOK,)
    out = pl.pallas_call(
        functools.partial(_post_kernel, tm=tm, ff_chunk=ff_chunk, n_x=len(xs), n_y=n_y),
        grid=(N_TOK // tm,),
        in_specs=_stream_specs(len(xs), tm) + [
            pl.BlockSpec((tm, D_MODEL), lambda i: (i, 0)),
            pl.BlockSpec(mod.shape, const),
            pl.BlockSpec((1, D_MODEL), const),
            pl.BlockSpec(wo.shape, const, pipeline_mode=pl.Buffered(1)),
            of_layer(w1_all),
            of_layer(w2_all),
        ],
        out_specs=_stream_specs(n_y, tm),
        out_shape=[jax.ShapeDtypeStruct((r, D_MODEL), F32) for r in rows],
        compiler_params=_cparams("arbitrary"),
        name="post_mixer",
    )(*xs, mixed, mod, g.reshape(1, D_MODEL), wo, w1_all, w2_all)
    return tuple(out)


PAIR = 2 * HD_C


def _pair_consts():
    lane = lax.broadcasted_iota(jnp.int32, (1, PAIR), 1)
    first = lane < HD_C
    ones_col = [jnp.where(lane == HD_C, 1.0, 0.0), jnp.where(lane == 0, 1.0, 0.0)]
    r = lax.broadcasted_iota(jnp.int32, (2 * PAIR, PAIR), 0) % PAIR
    cidx = lax.broadcasted_iota(jnp.int32, (2 * PAIR, PAIR), 1)
    mean2 = jnp.where(r // HD_C == cidx // HD_C, 1.0 / HD_C, 0.0).astype(BF16)
    return first, ones_col, mean2


def _pair_norm(x, w2, mean2):
    hi, lo = _split2(x * x)
    ms = _dot(jnp.concatenate([hi, lo], axis=1), mean2)
    return x * lax.rsqrt(ms + EPS) * w2


def _pair_queries(q, first):
    return [jnp.where(first, q, 0.0).astype(BF16), jnp.where(first, 0.0, q).astype(BF16)]


def _pair_values(v, first, ones_col):
    return [jnp.where(first, v, ones_col[0]).astype(BF16), jnp.where(first, ones_col[1], v).astype(BF16)]


def _pair_output(o_aug, first):
    den = [o_aug[0][:, HD_C:HD_C + 1], o_aug[1][:, 0:1]]
    return jnp.where(first, o_aug[0] / den[0], o_aug[1] / den[1])


def _row_max(*pieces):
    tiles = [p[:, i:i + 128] for p in pieces for i in range(0, p.shape[1], 128)]
    return jnp.max(functools.reduce(jnp.maximum, tiles), axis=-1, keepdims=True)


CTX_PAIRS = 2


def _ctx_attn_kernel(q_ref, k_ref, v_ref, qn_ref, kn_ref, o_ref, kc_ref, vc_ref):
    first, ones_col, mean2 = _pair_consts()
    lanes = [slice(p * PAIR, (p + 1) * PAIR) for p in range(CTX_PAIRS)]
    qn = [_pair_norm(q_ref[:, ln], qn_ref[...], mean2) * HD_C ** -0.5 for ln in lanes]
    kn = [_pair_norm(k_ref[:, ln], kn_ref[...], mean2) for ln in lanes]
    v = [v_ref[:, ln] for ln in lanes]
    kt = [x.T for x in kn]
    for p in range(CTX_PAIRS):
        kc_ref[0, 0, lanes[p], :] = kt[p]
        vc_ref[0, 0, lanes[p], :] = v[p].T
    q = [_pair_queries(x, first) for x in qn]
    va = [_pair_values(x, first, ones_col) for x in v]
    s = [[_dot(q[p][j], kt[p].astype(BF16)) for j in range(2)] for p in range(CTX_PAIRS)]
    pr = [[jnp.exp(x - _row_max(x)).astype(BF16) for x in sp] for sp in s]
    for p in range(CTX_PAIRS):
        o_ref[:, lanes[p]] = _pair_output([_dot(pr[p][j], va[p][j]) for j in range(2)], first)


def _ctx_attention(qkv, qn, kn):
    heads = 2 * CTX_PAIRS
    ng = H_C // heads
    wide = CTX_PAIRS * PAIR
    blk = lambda off: pl.BlockSpec((SEQ, wide), lambda b, p: (b, off + p))
    cache_spec = pl.BlockSpec((1, 1, wide, SEQ), lambda b, p: (b, 0, p, 0))
    cache_shape = jax.ShapeDtypeStruct((BATCH, 1, H_C * HD_C, SEQ), F32)
    return pl.pallas_call(
        _ctx_attn_kernel,
        grid=(BATCH, ng),
        in_specs=[blk(0), blk(ng), blk(2 * ng),
                  pl.BlockSpec((1, PAIR), lambda b, p: (0, 0)),
                  pl.BlockSpec((1, PAIR), lambda b, p: (0, 0))],
        out_specs=[pl.BlockSpec((SEQ, wide), lambda b, p: (b, p)), cache_spec, cache_spec],
        out_shape=[jax.ShapeDtypeStruct((N_TOK, D_MODEL), F32), cache_shape, cache_shape],
        compiler_params=_cparams("arbitrary", "arbitrary"),
        name="ctx_attention",
    )(qkv, qkv, qkv, jnp.tile(qn.reshape(1, HD_C), (1, 2)), jnp.tile(kn.reshape(1, HD_C), (1, 2)))


def _na_row_start(r):
    return min(max(r - KH // 2, 0), GRID_ROWS - KH)


NA_ROW_GROUP = 4


def _na_attn_kernel(q_ref, k_ref, v_ref, kc_ref, vc_ref, qn_ref, kn_ref, bias_ref, mixed_in_ref, o_ref,
                    qs, ks, vs, bias_s):
    del mixed_in_ref
    first, ones_col, mean2 = _pair_consts()

    @pl.when(pl.program_id(1) == 0)
    def _():
        q_col = lax.broadcasted_iota(jnp.int32, (GRID_W, PAIR), 0)
        lane = lax.broadcasted_iota(jnp.int32, (GRID_W, PAIR), 1)
        k_col = lane % GRID_W
        w0 = jnp.clip(q_col - KW // 2, 0, GRID_W - KW)
        outside = jnp.where((k_col >= w0) & (k_col < w0 + KW), 0.0, NEG_INF)
        n_dr = 2 * KH - 1
        for j in range(2):
            band = []
            for dr in range(n_dr):
                row = jnp.broadcast_to(bias_ref[j, dr:dr + 1, :], (GRID_W, PAIR))
                band.append([pltpu.roll(row, (half * GRID_W - (KW - 1)) % PAIR, axis=1, stride=1, stride_axis=0)
                             for half in range(2)])
            zero = jnp.zeros((GRID_W, PAIR), F32)
            for cp in range(2):
                for t in range(KH):
                    lo, hi = 2 * t + cp, 2 * t + cp + 1
                    tile = jnp.where(lane < GRID_W, band[lo][0] if lo < n_dr else zero,
                                     band[hi][1] if hi < n_dr else zero)
                    bias_s[j, cp, :, t * PAIR:(t + 1) * PAIR] = tile + outside

    q2 = _pair_queries(_pair_norm(q_ref[...], qn_ref[...], mean2) * HD_C ** -0.5, first)
    v2 = _pair_values(v_ref[...], first, ones_col)
    ks[...] = _pair_norm(k_ref[...], kn_ref[...], mean2).astype(BF16)
    for j in range(2):
        qs[j] = q2[j]
        vs[j] = v2[j]
    kt_ctx = kc_ref[0, 0].astype(BF16)
    vt = vc_ref[0, 0]
    ch = lax.broadcasted_iota(jnp.int32, vt.shape, 0)
    vt_ctx = [jnp.where(ch < HD_C, vt, jnp.where(ch == HD_C, 1.0, 0.0)).astype(BF16),
              jnp.where(ch < HD_C, jnp.where(ch == 0, 1.0, 0.0), vt).astype(BF16)]
    for r0 in range(0, GRID_ROWS, NA_ROW_GROUP):
        units = [(r, j) for r in range(r0, r0 + NA_ROW_GROUP) for j in range(2)]
        rows = {r: slice(r * GRID_W, (r + 1) * GRID_W) for r, _ in units}
        wins = {r: slice(_na_row_start(r) * GRID_W, (_na_row_start(r) + KH) * GRID_W) for r, _ in units}
        s_ctx_all = [_dot(qs[j, r0 * GRID_W:(r0 + NA_ROW_GROUP) * GRID_W, :], kt_ctx) for j in range(2)]
        s_ctx = [s_ctx_all[j][(r - r0) * GRID_W:(r - r0 + 1) * GRID_W] for r, j in units]
        s_win = []
        for r, j in units:
            dr0 = KH - 1 - (r - _na_row_start(r))
            lane0 = (dr0 - dr0 % 2) * GRID_W
            s_win.append(_dot_nt(qs[j, rows[r], :], ks[wins[r], :])
                         + bias_s[j, dr0 % 2, :, lane0:lane0 + KH * GRID_W])
        m = [_row_max(a, b) for a, b in zip(s_win, s_ctx)]
        p_win = [jnp.exp(a - mm).astype(BF16) for a, mm in zip(s_win, m)]
        p_ctx = [jnp.exp(b - mm).astype(BF16) for b, mm in zip(s_ctx, m)]
        o_aug = [_dot(p_win[i], vs[j, wins[r], :]) + _dot_nt(p_ctx[i], vt_ctx[j]) for i, (r, j) in enumerate(units)]
        for i in range(0, len(units), 2):
            o_ref[rows[units[i][0]], :] = _pair_output(o_aug[i:i + 2], first)


NA_BIAS_LANES = 2 * KH * GRID_W


def _na_attention(qkv, cache_kt, cache_vt, qn, kn, rpb, mixed):
    nhp = H_C // 2
    row0 = N_PROMPT // DEC_SEQ
    blk = lambda off: pl.BlockSpec((DEC_SEQ, 2 * HD_C), lambda p, b: (row0 + b, off + p))
    cache_spec = pl.BlockSpec((1, 1, PAIR, PAST_LEN), lambda p, b: (b, 0, p, 0))
    rpb_rows = 2 * KH
    bias = jnp.pad(rpb.astype(F32), ((0, 0), (0, rpb_rows - rpb.shape[1]), (0, PAIR - rpb.shape[2])))
    return pl.pallas_call(
        _na_attn_kernel,
        grid=(nhp, DEC_BATCH),
        in_specs=[blk(0), blk(nhp), blk(2 * nhp), cache_spec, cache_spec,
                  pl.BlockSpec((1, PAIR), lambda p, b: (0, 0)),
                  pl.BlockSpec((1, PAIR), lambda p, b: (0, 0)),
                  pl.BlockSpec((2, rpb_rows, PAIR), lambda p, b: (p, 0, 0)),
                  pl.BlockSpec(memory_space=pl.ANY)],
        out_specs=pl.BlockSpec((DEC_SEQ, 2 * HD_C), lambda p, b: (row0 + b, p)),
        out_shape=jax.ShapeDtypeStruct((N_TOK, D_MODEL), F32),
        input_output_aliases={8: 0},
        scratch_shapes=[pltpu.VMEM((2, DEC_SEQ, PAIR), BF16), pltpu.VMEM((DEC_SEQ, PAIR), BF16),
                        pltpu.VMEM((2, DEC_SEQ, PAIR), BF16), pltpu.VMEM((2, 2, GRID_W, NA_BIAS_LANES), F32)],
        compiler_params=_cparams("arbitrary", "arbitrary"),
        name="na_attention",
    )(qkv, qkv, qkv, cache_kt, cache_vt, jnp.tile(qn.reshape(1, HD_C), (1, 2)), jnp.tile(kn.reshape(1, HD_C), (1, 2)),
      bias, mixed)


def _seq_layout(prompt):
    return (SEQ, BATCH, 0) if prompt else (DEC_SEQ, DEC_BATCH, N_PROMPT // DEC_SEQ)


def _flip_blocks(m, c):
    r, s = m.shape
    return m.reshape(r // c, c, s // c, c)[:, ::-1, :, ::-1].reshape(r, s)


def _rms_gate(x, gn, gate):
    ms = jnp.mean(x * x, axis=-1, keepdims=True)
    return x * lax.rsqrt(ms + EPS) * gn * _silu(gate)


HG_LEVELS = tuple(CHUNK >> (i + 1) for i in range(CHUNK.bit_length() - 1))
HG_NL = len(HG_LEVELS)
HG_STACK = (HG_NL + 1) * CHUNK
TOT_ROWS = 16
HG_ROWS = (HG_NL + 2) * CHUNK + TOT_ROWS


def _hgrn_consts():
    c = CHUNK
    level_rows = []
    mask = np.zeros((HG_STACK, HG_STACK), np.float32)
    mask[:c, :c] = np.eye(c)
    for li, b in enumerate(HG_LEVELS):
        m = np.zeros((c, c), np.float32)
        blk = np.zeros((c, c), np.float32)
        for t in range(c):
            mid = (t // (2 * b)) * 2 * b + b
            if t >= mid:
                m[t, mid:t + 1] = 1.0
                blk[t, mid - b:mid] = 1.0
            else:
                m[t, t + 1:mid] = 1.0
        level_rows.append(m)
        mask[(li + 1) * c:(li + 2) * c, (li + 1) * c:(li + 2) * c] = blk
    dq = np.tril(np.ones((c, c), np.float32))
    dk = np.triu(np.ones((c, c), np.float32), 1)
    body = np.concatenate(level_rows + [dq, dk], axis=0)
    tot = np.ones((TOT_ROWS, c), np.float32)
    mcs, masks = [], []
    for reverse in (False, True):
        bm = _flip_blocks(body, c) if reverse else body
        mk = _flip_blocks(mask, c) if reverse else mask
        mc = np.concatenate([bm, tot], axis=0)
        mcs.append(np.concatenate([mc, mc], axis=1))
        masks.append(mk)
    return jnp.asarray(np.stack(mcs), BF16), jnp.asarray(np.stack(masks), F32)


HG_FAST = 64
HG_HALF = HG_FAST // 2
HG_FAST_ROWS = 4 * HG_FAST + TOT_ROWS
HG_SAFE_EXP = 40.0


def _hgrn_fast_consts(seq):
    c, m = HG_FAST, HG_HALF
    aq = np.zeros((c, c), np.float32)
    for t in range(c):
        if t >= m:
            aq[t, m:t + 1] = 1.0
        else:
            aq[t, t + 1:m] = -1.0
    dq = np.tril(np.ones((c, c), np.float32))
    dk = np.triu(np.ones((c, c), np.float32), 1)
    body = np.concatenate([aq, -aq, dq, dk], axis=0)
    tot = np.ones((TOT_ROWS, c), np.float32)
    causal = np.tril(np.ones((c, c), np.float32))
    mfs, masks = [], []
    for reverse in (False, True):
        bm = _flip_blocks(body, c) if reverse else body
        mf = np.concatenate([bm, tot], axis=0)
        mfs.append(np.concatenate([mf, mf], axis=1))
        masks.append(causal.T if reverse else causal)
    n_half = seq // m
    half = np.zeros((max(n_half, 16), seq), np.float32)
    for i in range(n_half):
        half[i, i * m:(i + 1) * m] = 1.0
    return jnp.asarray(np.stack(mfs), BF16), jnp.asarray(np.stack(masks), F32), jnp.asarray(half, BF16)


def _hgrn_kernel(*refs, seq, has_s0, emit_state, aliased):
    it = iter(refs)
    qa_ref, ff_ref, fb_ref, ia_ref, ga_ref, lb_ref, gn_ref, mc_ref, mask_ref = [next(it) for _ in range(9)]
    mf_ref, causal_ref, half_ref = [next(it) for _ in range(3)]
    s0_ref = next(it) if has_s0 else None
    if aliased:
        next(it)
    o_ref = next(it)
    st_ref = next(it) if emit_state else None
    s_scr, acc, f_s, lf_s = [next(it) for _ in range(4)]
    c = CHUNK
    n_chunks = seq // c
    combos = [(d, h) for d in range(2) for h in range(H_A)]
    lanes = [slice(h * DK_A, (h + 1) * DK_A) for h in range(H_A)]
    add = lambda a, b: a + b

    lb_raw = lb_ref[...]
    lb_e = jnp.exp(lb_raw - jnp.max(lb_raw, axis=0, keepdims=True))
    lb_all = lb_e[0:1] / jnp.sum(lb_e, axis=0, keepdims=True)

    for d in range(2):
        for h in range(H_A):
            s_scr[d, h] = s0_ref[0, 0, d, h].T if has_s0 else jnp.zeros((DV_A, DK_A), F32)
    acc[...] = jnp.zeros(acc.shape, F32)

    worst = []
    for d, fr_ref in enumerate((ff_ref, fb_ref)):
        f = lb_all + (1.0 - lb_all) * _sigmoid(fr_ref[...])
        lf = jnp.log(f)
        f_s[d] = f
        lf_s[d] = lf
        worst.append(jnp.max(_dot(half_ref[...], (-lf).astype(BF16))))
    safe = jnp.maximum(worst[0], worst[1]) <= HG_SAFE_EXP

    def fast_body(n, carry):
        cf = HG_FAST
        n_fast = seq // cf
        rows = [pl.ds(pl.multiple_of((n if d == 0 else n_fast - 1 - n) * cf, cf), cf) for d in range(2)]
        e_all = [jnp.exp(_dot_const(mf_ref[d], lf_s[d, rows[d], :])) for d in range(2)]
        q_all = [_silu(qa_ref[rows[d], :]) * DK_A ** -0.5 for d in range(2)]
        k_all = [1.0 - f_s[d, rows[d], :] for d in range(2)]
        v_all = [ia_ref[rows[d], :].astype(BF16) for d in range(2)]
        st = [s_scr[d, h] for d, h in combos]
        qs = [q_all[d][:, lanes[h]] for d, h in combos]
        ks = [k_all[d][:, lanes[h]] for d, h in combos]
        vs = [v_all[d][:, lanes[h]] for d, h in combos]
        es = [[e_all[d][i * cf:(i + 1) * cf, lanes[h]] for i in range(4)] for d, h in combos]
        p = [jnp.where(causal_ref[d] > 0.0,
                       _dot_nt((qs[i] * es[i][0]).astype(BF16), (ks[i] * es[i][1]).astype(BF16)), 0.0).astype(BF16)
             for i, (d, h) in enumerate(combos)]
        o = [_dot(p[i], vs[i]) + _dot_nt((qs[i] * es[i][2]).astype(BF16), st[i].astype(BF16))
             for i in range(len(combos))]
        upd = [_dot_tn(vs[i], (ks[i] * es[i][3]).astype(BF16)) for i in range(len(combos))]
        for d in range(2):
            acc[rows[d], :] += jnp.concatenate(o[d * H_A:(d + 1) * H_A], axis=1)
        for i, (d, h) in enumerate(combos):
            s_scr[d, h] = st[i] * e_all[d][4 * cf:4 * cf + 1, lanes[h]] + upd[i]
        return carry

    def body(n, carry):
        rows = [pl.ds(pl.multiple_of((n if d == 0 else n_chunks - 1 - n) * c, c), c) for d in range(2)]
        f_all = [f_s[d, rows[d], :] for d in range(2)]
        e_all = [jnp.exp(_dot_const(mc_ref[d], lf_s[d, rows[d], :])) for d in range(2)]
        q_all = [_silu(qa_ref[rows[d], :]) * DK_A ** -0.5 for d in range(2)]
        v_all = [ia_ref[rows[d], :].astype(BF16) for d in range(2)]
        st = [s_scr[d, h] for d, h in combos]
        qs, ks, vs, es = [], [], [], []
        for d, h in combos:
            qs.append(q_all[d][:, lanes[h]])
            ks.append(1.0 - f_all[d][:, lanes[h]])
            vs.append(v_all[d][:, lanes[h]])
            es.append(e_all[d][:, lanes[h]])
        lvl = [[e[i * c:(i + 1) * c] for i in range(HG_NL + 2)] for e in es]
        qst = [jnp.concatenate([q] + [q * l[i] for i in range(HG_NL)], axis=0).astype(BF16) for q, l in zip(qs, lvl)]
        kst = [jnp.concatenate([k] + [k * l[i] for i in range(HG_NL)], axis=0).astype(BF16) for k, l in zip(ks, lvl)]
        r = [(_dot_nt(qst[i], kst[i]) * mask_ref[d]).astype(BF16) for i, (d, h) in enumerate(combos)]
        ost = [_dot(r[i], jnp.concatenate([vs[i]] * (HG_NL + 1), axis=0)) for i in range(len(combos))]
        inter = [_dot_nt((qs[i] * lvl[i][HG_NL]).astype(BF16), st[i].astype(BF16)) for i in range(len(combos))]
        upd = [_dot_tn(vs[i], (ks[i] * lvl[i][HG_NL + 1]).astype(BF16)) for i in range(len(combos))]
        o = [functools.reduce(lambda a, b: a + b, [ost[i][j * c:(j + 1) * c] for j in range(HG_NL + 1)]) + inter[i]
             for i in range(len(combos))]
        for d in range(2):
            acc[rows[d], :] += jnp.concatenate(o[d * H_A:(d + 1) * H_A], axis=1)
        for i, (d, h) in enumerate(combos):
            e_tot = es[i][(HG_NL + 2) * c:(HG_NL + 2) * c + 1]
            s_scr[d, h] = st[i] * e_tot + upd[i]
        return carry

    @pl.when(safe)
    def _():
        lax.fori_loop(0, seq // HG_FAST, fast_body, 0)

    @pl.when(jnp.logical_not(safe))
    def _():
        lax.fori_loop(0, n_chunks, body, 0)

    for h in range(H_A):
        ln = slice(h * DV_A, (h + 1) * DV_A)
        o_ref[:, ln] = _rms_gate(acc[:, ln], gn_ref[...], ga_ref[:, ln])
    if emit_state:
        for d in range(2):
            for h in range(H_A):
                st_ref[0, 0, d, h] = s_scr[d, h].T


def _hgrn(proj, hgrn_lb, gn, consts, prompt, s0=None, layer=0, mixed=None):
    seq, nb, rb0 = _seq_layout(prompt)
    consts = list(consts) + list(_hgrn_fast_consts(seq))
    wa = H_A * DK_A
    blk = lambda j: pl.BlockSpec((seq, wa), lambda b: (rb0 + b, j))
    const2 = lambda b: (0, 0)
    st_block = (1, 1, 2, H_A, DK_A, DV_A)
    in_specs = [blk(0), blk(1), blk(2), blk(3), blk(4),
                pl.BlockSpec(hgrn_lb.shape, const2), pl.BlockSpec((1, DV_A), const2)]
    in_specs += [pl.BlockSpec(m.shape, lambda b, nd=m.ndim: (0,) * nd) for m in consts]
    args = [proj] * 5 + [hgrn_lb, gn.reshape(1, DV_A)] + consts
    if s0 is not None:
        in_specs.append(pl.BlockSpec(st_block, lambda b: (b, layer, 0, 0, 0, 0)))
        args.append(s0)
    aliases = {}
    if mixed is not None:
        aliases = {len(args): 0}
        in_specs.append(pl.BlockSpec(memory_space=pl.ANY))
        args.append(mixed)
    out_specs = [pl.BlockSpec((seq, wa), lambda b: (rb0 + b, 0))]
    out_shape = [jax.ShapeDtypeStruct((N_TOK, D_MODEL), F32)]
    if prompt:
        out_specs.append(pl.BlockSpec(st_block, lambda b: (b, 0, 0, 0, 0, 0)))
        out_shape.append(jax.ShapeDtypeStruct((nb, 1, 2, H_A, DK_A, DV_A), F32))
    return pl.pallas_call(
        functools.partial(_hgrn_kernel, seq=seq, has_s0=s0 is not None, emit_state=prompt, aliased=mixed is not None),
        grid=(nb,),
        in_specs=in_specs,
        out_specs=out_specs,
        out_shape=out_shape,
        input_output_aliases=aliases,
        scratch_shapes=[pltpu.VMEM((2, H_A, DV_A, DK_A), F32), pltpu.VMEM((seq, wa), F32),
                        pltpu.VMEM((2, seq, wa), F32), pltpu.VMEM((2, seq, wa), F32)],
        compiler_params=_cparams("arbitrary"),
        name="hgrn_prompt" if prompt else "hgrn_sample",
    )(*args)


GD_SUB = GBLK // CHUNK
GD_ROWS = 2 * GBLK + TOT_ROWS


def _gdn_consts():
    n, c = GBLK, CHUNK
    same = (np.arange(n)[:, None] // c) == (np.arange(n)[None, :] // c)
    tri = (same & (np.arange(n)[None, :] <= np.arange(n)[:, None])).astype(np.float32)
    sup = (same & (np.arange(n)[None, :] > np.arange(n)[:, None])).astype(np.float32)
    tot = np.zeros((TOT_ROWS, n), np.float32)
    for s in range(GD_SUB):
        tot[s, s * c:(s + 1) * c] = 1.0
    mgs, tts, tris = [], [], []
    for reverse in (False, True):
        t = _flip_blocks(tri, c) if reverse else tri
        s = _flip_blocks(sup, c) if reverse else sup
        mg = np.concatenate([t, s, tot], axis=0)
        mgs.append(np.concatenate([mg, mg], axis=1))
        tts.append(np.concatenate([t.T, t.T], axis=0))
        tris.append(t)
    tris.append(same.astype(np.float32))
    return jnp.asarray(np.stack(mgs), BF16), jnp.asarray(np.stack(tts), BF16), jnp.asarray(np.stack(tris), F32)


def _softplus(x):
    return jnp.maximum(x, 0.0) + jnp.log(1.0 + jnp.exp(-jnp.abs(x)))


def _conv_silu(x, w, seq):
    half = SHORT_CONV // 2
    pad = jnp.zeros((8, x.shape[1]), x.dtype)
    xe = jnp.concatenate([pad, x, pad], axis=0)
    acc = xe * w[half:half + 1]
    for j in range(SHORT_CONV):
        shift = half - j
        if shift != 0:
            acc = acc + pltpu.roll(xe, shift % (seq + 16), axis=0) * w[j:j + 1]
    return _silu(acc[8:seq + 8])


def _l2norm_heads(x, n_heads, width, scale):
    outs = []
    for h in range(n_heads):
        xh = x[:, h * width:(h + 1) * width]
        outs.append(xh * (lax.rsqrt(jnp.sum(xh * xh, axis=-1, keepdims=True) + EPS) * scale))
    return jnp.concatenate(outs, axis=-1)


def _gdn_kernel(*refs, seq, has_s0, emit_state, aliased):
    it = iter(refs)
    (q_ref, k_ref, v_ref, gb_ref, gate_ref, cw_ref, alog_ref, dt_ref, gn_ref,
     mg_ref, tt_ref, tri_ref) = [next(it) for _ in range(12)]
    s0_ref = next(it) if has_s0 else None
    if aliased:
        next(it)
    o_ref = next(it)
    st_ref = next(it) if emit_state else None
    qn, kn, vn, u_s, w_s, qg_s, kdt_s, at_s, et_s, s_scr, acc = [next(it) for _ in range(11)]
    c = CHUNK
    n_chunks = seq // c
    n_blocks = seq // GBLK
    wq = H_B * DK_B
    n_dh = 2 * H_B
    combos = [(d, h) for d in range(2) for h in range(H_B)]
    lanes = [slice(h * DK_B, (h + 1) * DK_B) for h in range(H_B)]

    qn[...] = _l2norm_heads(_conv_silu(q_ref[...], cw_ref[:, 0:wq], seq), H_B, DK_B, DK_B ** -0.5)
    kn[...] = _l2norm_heads(_conv_silu(k_ref[...], cw_ref[:, wq:2 * wq], seq), H_B, DK_B, 1.0)
    vn[...] = _conv_silu(v_ref[...], cw_ref[:, 2 * wq:3 * wq], seq)
    for i in range(2 * H_B):
        s_scr[i] = s0_ref[0, 0, i // H_B, i % H_B] if has_s0 else jnp.zeros((DK_B, DV_B), F32)
    acc[...] = jnp.zeros(acc.shape, F32)

    eye = (lax.broadcasted_iota(jnp.int32, (GBLK, GBLK), 0)
           == lax.broadcasted_iota(jnp.int32, (GBLK, GBLK), 1)).astype(F32)
    eye_pk = (lax.broadcasted_iota(jnp.int32, (c, GBLK), 0)
              == lax.broadcasted_iota(jnp.int32, (c, GBLK), 1) % c).astype(F32)
    bwd_lane = lax.broadcasted_iota(jnp.int32, (1, 128), 1) % n_dh >= H_B
    add = lambda a, b: a + b

    def expand(pk):
        return jnp.concatenate([pk] * GD_SUB, axis=0) * tri_ref[2]

    def pack(bd):
        return functools.reduce(add, [bd[s * c:(s + 1) * c] for s in range(GD_SUB)])

    def dot3_split(a, b_hi, b_lo):
        ah, al = _split2(a)
        t = _dot(jnp.concatenate([ah, al], axis=0), b_hi)
        return t[:a.shape[0]] + t[a.shape[0]:] + _dot(ah, b_lo)

    def block_body(blk, carry):
        rows = pl.ds(pl.multiple_of(blk * GBLK, GBLK), GBLK)
        gates = gate_ref[rows, :]
        glog_all = -jnp.exp(alog_ref[...]) * _softplus(gates + dt_ref[...])
        beta_all = _sigmoid(gates)
        g2 = jnp.concatenate(_split2(glog_all), axis=0)
        dg = [_dot(mg_ref[d], g2) for d in range(2)]
        dsel = jnp.where(bwd_lane, dg[1], dg[0])
        eg_all = jnp.exp(dsel)
        g_all = dsel[:GBLK]
        gt = [_dot_tn(g2, tt_ref[d]) for d in range(2)]
        qs = [qn[rows, ln] for ln in lanes]
        ks = [kn[rows, ln] for ln in lanes]
        vs = [vn[rows, ln] for ln in lanes]
        col = lambda x, j: jnp.broadcast_to(x[:, j:j + 1], (GBLK, DK_B))
        betas = [col(beta_all, n_dh + i) for i in range(n_dh)]
        kbs = [ks[h] * betas[i] for i, (d, h) in enumerate(combos)]
        kk = [_dot_nt(jnp.concatenate([qs[h], kbs[h], kbs[H_B + h]], axis=0).astype(BF16), ks[h].astype(BF16))
              for h in range(H_B)]
        decay = []
        for i, (d, h) in enumerate(combos):
            inside = tri_ref[d] > 0.0
            gd = col(g_all, i) - gt[d][i:i + 1, :]
            decay.append(jnp.where(inside, jnp.exp(jnp.where(inside, gd, 0.0)), 0.0))
        attn = [kk[h][:GBLK] * decay[i] for i, (d, h) in enumerate(combos)]
        p_pk = [pack(kk[h][(1 + d) * GBLK:(2 + d) * GBLK] * decay[i] * (1.0 - eye)) for i, (d, h) in enumerate(combos)]
        x_pk = [eye_pk - p for p in p_pk]
        p_bd = [_split2(expand(p)) for p in p_pk]
        for _ in range(CHUNK.bit_length() - 2):
            p_pk = [dot3_split(p, *b) for p, b in zip(p_pk, p_bd)]
            p_bd = [_split2(expand(p)) for p in p_pk]
            x_pk = [x + dot3_split(x, *b) for x, b in zip(x_pk, p_bd)]
        eg_col = [col(eg_all[:GBLK], i) for i in range(n_dh)]
        ekd_col = [col(eg_all[GBLK:2 * GBLK], i) for i in range(n_dh)]
        rhs = [jnp.concatenate([vs[h] * betas[i], kbs[i] * eg_col[i]], axis=1) for i, (d, h) in enumerate(combos)]
        uw = [_dot3(expand(x_pk[i]), rhs[i]) for i in range(n_dh)]
        for i, (d, h) in enumerate(combos):
            qg = (qs[h] * eg_col[i]).astype(BF16)
            kdt_s[i, blk] = (ks[h] * ekd_col[i]).T.astype(BF16)
            for s in range(GD_SUB):
                cn = blk * GD_SUB + s
                r = slice(s * c, (s + 1) * c)
                u_s[i, cn] = uw[i][r, :DV_B]
                w_s[i, cn] = uw[i][r, DV_B:].astype(BF16)
                qg_s[i, cn] = qg[r]
                at_s[i, cn] = attn[i][r].astype(BF16)
                et_s[i, cn] = jnp.broadcast_to(eg_all[2 * GBLK + s:2 * GBLK + s + 1, i:i + 1], (8, DV_B))
        return carry

    lax.fori_loop(0, n_blocks, block_body, 0)

    def chunk_body(n, carry):
        cns = [n, n_chunks - 1 - n]
        rows = [pl.ds(pl.multiple_of(cn * c, c), c) for cn in cns]
        sub_of_row = lax.broadcasted_iota(jnp.int32, (GBLK, 1), 0) // c
        in_chunk = [sub_of_row == cn % GD_SUB for cn in cns]
        st = [s_scr[i] for i in range(n_dh)]
        ws = [_dot(jnp.concatenate([w_s[i, cns[d]], qg_s[i, cns[d]]], axis=0), st[i].astype(BF16))
              for i, (d, h) in enumerate(combos)]
        vblk = [jnp.where(in_chunk[d], jnp.concatenate([u_s[i, cns[d]] - ws[i][:c]] * GD_SUB, axis=0), 0.0).astype(BF16)
                for i, (d, h) in enumerate(combos)]
        r = [_dot(jnp.concatenate([at_s[i, cns[d]], kdt_s[i, cns[d] // GD_SUB]], axis=0), vblk[i])
             for i, (d, h) in enumerate(combos)]
        for d in range(2):
            acc[rows[d], :] += jnp.concatenate([ws[i][c:] + r[i][:c] for i in range(d * H_B, (d + 1) * H_B)], axis=1)
        for i, (d, h) in enumerate(combos):
            s_scr[i] = st[i] * et_s[i, cns[d]][0:1] + r[i][c:]
        return carry

    lax.fori_loop(0, n_chunks, chunk_body, 0)

    for h in range(H_B):
        ln = slice(h * DV_B, (h + 1) * DV_B)
        o_ref[:, ln] = _rms_gate(acc[:, ln], gn_ref[...], gb_ref[:, ln])
    if emit_state:
        for i in range(2 * H_B):
            st_ref[0, 0, i // H_B, i % H_B] = s_scr[i]


def _gdn(proj, gates, conv_w, a_log, dt_bias, gn, consts, prompt, mixed, s0=None, layer=0):
    seq, nb, rb0 = _seq_layout(prompt)
    n_chunks = seq // CHUNK
    wq = H_B * DK_B
    blk = lambda j: pl.BlockSpec((seq, wq), lambda b: (rb0 + b, j))
    const2 = lambda b: (0, 0)
    const3 = lambda b: (0, 0, 0)
    st_block = (1, 1, 2, H_B, DK_B, DV_B)
    pad_row = lambda p: jnp.pad(p.reshape(1, -1).astype(F32), ((0, 0), (0, 128 - p.size)))
    in_specs = [blk(5), blk(6), blk(7), blk(8),
                pl.BlockSpec((seq, 128), lambda b: (rb0 + b, 0)),
                pl.BlockSpec((SHORT_CONV, 3 * wq), const2),
                pl.BlockSpec((1, 128), const2), pl.BlockSpec((1, 128), const2), pl.BlockSpec((1, DV_B), const2)]
    in_specs += [pl.BlockSpec(m.shape, const3) for m in consts]
    args = [proj] * 4 + [gates, conv_w.reshape(SHORT_CONV, 3 * wq), pad_row(a_log), pad_row(dt_bias),
                         gn.reshape(1, DV_B)] + list(consts)
    if s0 is not None:
        in_specs.append(pl.BlockSpec(st_block, lambda b: (b, layer, 0, 0, 0, 0)))
        args.append(s0)
    aliases = {len(args): 0}
    in_specs.append(pl.BlockSpec(memory_space=pl.ANY))
    args.append(mixed)
    out_specs = [pl.BlockSpec((seq, wq), lambda b: (rb0 + b, 1))]
    out_shape = [jax.ShapeDtypeStruct((N_TOK, D_MODEL), F32)]
    if prompt:
        out_specs.append(pl.BlockSpec(st_block, lambda b: (b, 0, 0, 0, 0, 0)))
        out_shape.append(jax.ShapeDtypeStruct((nb, 1, 2, H_B, DK_B, DV_B), F32))
    n_dh = 2 * H_B
    scratch = ([pltpu.VMEM((seq, wq), F32)] * 3
               + [pltpu.VMEM((n_dh, n_chunks, CHUNK, DV_B), F32)]
               + [pltpu.VMEM((n_dh, n_chunks, CHUNK, DK_B), BF16)] * 2
               + [pltpu.VMEM((n_dh, seq // GBLK, DK_B, GBLK), BF16),
                  pltpu.VMEM((n_dh, n_chunks, CHUNK, GBLK), BF16),
                  pltpu.VMEM((n_dh, n_chunks, 8, DV_B), F32),
                  pltpu.VMEM((n_dh, DK_B, DV_B), F32),
                  pltpu.VMEM((seq, wq), F32)])
    return pl.pallas_call(
        functools.partial(_gdn_kernel, seq=seq, has_s0=s0 is not None, emit_state=prompt, aliased=True),
        grid=(nb,),
        in_specs=in_specs,
        out_specs=out_specs,
        out_shape=out_shape,
        input_output_aliases=aliases,
        scratch_shapes=scratch,
        compiler_params=_cparams("arbitrary"),
        name="gdn_prompt" if prompt else "gdn_sample",
    )(*args)


def kernel(x_prompt, x_sample, state_hgrn, state_gdn, cache_na_k, cache_na_v, c, c_ctx, ada_w, ada_b, norm_g, w_in_ab, w_out_ab, hgrn_lb, gdn_conv, gdn_a_log, gdn_dt_bias, gn_hgrn, gn_gdn, w_qkv_na, qn_na, kn_na, rpb_na, w_out_na, w_mlp1, w_mlp2):
    cond = jnp.concatenate([c_ctx[None, :], c, jnp.zeros((N_MOD_ROWS - 1 - DEC_BATCH, D_MODEL), F32)], axis=0)
    mods = _modulation(cond, ada_w, ada_b)
    xs = (x_prompt.reshape(N_PROMPT, D_MODEL), x_sample.reshape(N_SAMPLE, D_MODEL))

    w_in = w_in_ab[0].astype(BF16)
    w_gate = jnp.pad(w_in[:, D_MAIN_AB:], ((0, 0), (0, 128 - N_GATE_AB)))
    proj, gates = _norm_proj(xs, mods[0], norm_g[0, 0], [w_in, w_gate], widths=[D_MAIN_AB, 128])
    w_mlp1_bf, w_mlp2_bf = w_mlp1.astype(BF16), w_mlp2.astype(BF16)
    hg_consts = _hgrn_consts()
    gd_consts = _gdn_consts()
    mixed, new_hgrn = _hgrn(proj, hgrn_lb, gn_hgrn[0], hg_consts, True)
    mixed, = _hgrn(proj, hgrn_lb, gn_hgrn[0], hg_consts, False, s0=state_hgrn, mixed=mixed)
    gd_args = (gdn_conv[0], gdn_a_log[0], gdn_dt_bias[0], gn_gdn[0], gd_consts)
    mixed, new_gdn = _gdn(proj, gates, *gd_args, True, mixed)
    mixed, = _gdn(proj, gates, *gd_args, False, mixed, s0=state_gdn)
    xs = _post_mixer(xs, mixed, mods[0], norm_g[0, 1], w_out_ab[0].astype(BF16), w_mlp1_bf, w_mlp2_bf, 0)

    qkv, = _norm_proj(xs, mods[1], norm_g[1, 0], [w_qkv_na[0].astype(BF16)])
    mixed, new_kt, new_vt = _ctx_attention(qkv, qn_na[0], kn_na[0])
    time_minor = lambda a: jnp.swapaxes(a, -1, -2).reshape(a.shape[0], 1, H_C * HD_C, a.shape[3])
    mixed = _na_attention(qkv, time_minor(cache_na_k), time_minor(cache_na_v), qn_na[0], kn_na[0], rpb_na[0], mixed)
    time_major = lambda a: jnp.swapaxes(a.reshape(BATCH, 1, H_C, HD_C, SEQ), -1, -2)
    new_k, new_v = time_major(new_kt), time_major(new_vt)
    y_prompt, y_sample = _post_mixer(xs, mixed, mods[1], norm_g[1, 1], w_out_na[0].astype(BF16),
                                     w_mlp1_bf, w_mlp2_bf, 1, split_out=True)

    return (y_prompt.reshape(BATCH, SEQ, D_MODEL), y_sample.reshape(DEC_BATCH, DEC_SEQ, D_MODEL),
            new_hgrn, new_gdn, new_k, new_v)
```

```python
import functools

import numpy as np
import jax
import jax.numpy as jnp
from jax import lax
from jax.experimental import pallas as pl
from jax.experimental.pallas import tpu as pltpu

F32 = jnp.float32
BF16 = jnp.bfloat16

D_MODEL = 1024
BATCH = 16
SEQ = 256
DEC_BATCH = 4
DEC_SEQ = 1024
PAST_LEN = 256
N_PROMPT = BATCH * SEQ
N_SAMPLE = DEC_BATCH * DEC_SEQ
N_TOK = N_PROMPT + N_SAMPLE
GRID_W = 64
GRID_ROWS = DEC_SEQ // GRID_W
H_A = 4
DK_A = 128
DV_A = 128
H_B = 4
DK_B = 128
DV_B = 128
SHORT_CONV = 5
H_C = 16
HD_C = 64
KH = 8
KW = 16
D_FF = 4 * D_MODEL
EPS = 1e-6
NEG_INF = -1e30
N_MOD_ROWS = 8
D_MAIN_AB = 4608
N_GATE_AB = 16
CHUNK = 32
GBLK = 128
VMEM_LIMIT = 56 * 1024 * 1024


def _cparams(*sem):
    return pltpu.CompilerParams(dimension_semantics=sem, vmem_limit_bytes=VMEM_LIMIT)


def _sigmoid(x):
    return 1.0 / (1.0 + jnp.exp(-x))


def _silu(x):
    return x * _sigmoid(x)


def _dot(a, b):
    return jnp.dot(a, b, preferred_element_type=F32)


def _dot_nt(a, b):
    return lax.dot_general(a, b, (((1,), (1,)), ((), ())), preferred_element_type=F32)


def _dot_tn(a, b):
    return lax.dot_general(a, b, (((0,), (0,)), ((), ())), preferred_element_type=F32)


def _split2(x):
    hi = x.astype(BF16)
    lo = (x - hi.astype(F32)).astype(BF16)
    return hi, lo


def _dot_const(m2, x):
    hi, lo = _split2(x)
    return _dot(m2, jnp.concatenate([hi, lo], axis=0))


def _dot3(a, b):
    ah, al = _split2(a)
    bh, bl = _split2(b)
    return _dot(ah, bh) + (_dot(ah, bl) + _dot(al, bh))


def _mod_row(i, tm):
    start = i * tm
    return jnp.where(start < N_PROMPT, 0, 1 + (start - N_PROMPT) // DEC_SEQ)


def _mod_slice(mod_ref, row, k):
    return mod_ref[pl.ds(row, 1), k * D_MODEL:(k + 1) * D_MODEL]


def _norm_mod(x, g, sc, sh):
    ms = jnp.mean(x * x, axis=-1, keepdims=True)
    return (x * lax.rsqrt(ms + EPS) * g) * (1.0 + sc) + sh


def _mod_kernel(cond_ref, w_ref, b_ref, o_ref):
    s = _silu(cond_ref[...]).astype(BF16)
    o_ref[0] = _dot(s, w_ref[0].astype(BF16)) + b_ref[0]


def _modulation(cond8, ada_w, ada_b):
    depth = ada_w.shape[0]
    tn = 1024
    nj = ada_w.shape[2] // tn
    return pl.pallas_call(
        _mod_kernel,
        grid=(depth, nj),
        in_specs=[
            pl.BlockSpec((N_MOD_ROWS, D_MODEL), lambda l, j: (0, 0)),
            pl.BlockSpec((1, D_MODEL, tn), lambda l, j: (l, 0, j)),
            pl.BlockSpec((1, 1, tn), lambda l, j: (l, 0, j)),
        ],
        out_specs=pl.BlockSpec((1, N_MOD_ROWS, tn), lambda l, j: (l, 0, j)),
        out_shape=jax.ShapeDtypeStruct((depth, N_MOD_ROWS, ada_w.shape[2]), F32),
        compiler_params=_cparams("arbitrary", "arbitrary"),
        name="modulation",
    )(cond8, ada_w, ada_b.reshape(depth, 1, -1))


def _stream_specs(n_arrays, tm, width=D_MODEL):
    if n_arrays == 1:
        return [pl.BlockSpec((tm, width), lambda i: (i, 0))]
    npt = N_PROMPT // tm
    return [pl.BlockSpec((tm, width), lambda i: (jnp.minimum(i, npt - 1), 0)),
            pl.BlockSpec((tm, width), lambda i: (jnp.maximum(i - npt, 0), 0))]


def _stream_load(x_refs, tm):
    if len(x_refs) == 1:
        return x_refs[0][...]
    return jnp.where(pl.program_id(0) < N_PROMPT // tm, x_refs[0][...], x_refs[1][...])


def _norm_proj_kernel(*refs, tm, n_x, n_w):
    x_refs, (mod_ref, g_ref) = refs[:n_x], refs[n_x:n_x + 2]
    w_refs, o_refs = refs[n_x + 2:n_x + 2 + n_w], refs[n_x + 2 + n_w:]
    row = _mod_row(pl.program_id(0), tm)
    h = _norm_mod(_stream_load(x_refs, tm), g_ref[...], _mod_slice(mod_ref, row, 1), _mod_slice(mod_ref, row, 0)).astype(BF16)
    for w_ref, o_ref in zip(w_refs, o_refs):
        o_ref[...] = _dot(h, w_ref[...])


def _norm_proj(xs, mod, g, ws, widths=None, tm=512):
    n_w = len(ws)
    widths = widths or [w.shape[1] for w in ws]
    const = lambda i: (0, 0)
    return pl.pallas_call(
        functools.partial(_norm_proj_kernel, tm=tm, n_x=len(xs), n_w=n_w),
        grid=(N_TOK // tm,),
        in_specs=_stream_specs(len(xs), tm) + [
            pl.BlockSpec(mod.shape, const),
            pl.BlockSpec((1, D_MODEL), const),
        ] + [pl.BlockSpec((D_MODEL, n), const, pipeline_mode=pl.Buffered(1)) for n in widths],
        out_specs=[pl.BlockSpec((tm, n), lambda i: (i, 0)) for n in widths],
        out_shape=[jax.ShapeDtypeStruct((N_TOK, n), F32) for n in widths],
        compiler_params=_cparams("arbitrary"),
        name="norm_proj",
    )(*xs, mod, g.reshape(1, D_MODEL), *ws)


def _post_kernel(*refs, tm, ff_chunk, n_x, n_y, groups):
    x_refs = refs[:n_x]
    n_m = sum(n for n, _ in groups)
    m_refs = refs[n_x:n_x + n_m]
    mod_ref, g_ref, wo_ref, w1_ref, w2_ref = refs[n_x + n_m:n_x + n_m + 5]
    y_refs = refs[n_x + n_m + 5:]
    row = _mod_row(pl.program_id(0), tm)
    mix, first_ref, first_col = None, 0, 0
    for n, width in groups:
        part = _stream_load(m_refs[first_ref:first_ref + n], tm).astype(BF16)
        term = _dot(part, wo_ref[first_col:first_col + width, :])
        mix = term if mix is None else mix + term
        first_ref, first_col = first_ref + n, first_col + width
    x1 = _stream_load(x_refs, tm) + _mod_slice(mod_ref, row, 2) * mix
    h = _norm_mod(x1, g_ref[...], _mod_slice(mod_ref, row, 4), _mod_slice(mod_ref, row, 3)).astype(BF16)
    acc = jnp.zeros((tm, D_MODEL), F32)
    for k in range(0, D_FF, ff_chunk):
        a = jnp.maximum(_dot(h, w1_ref[:, k:k + ff_chunk]), 0.0)
        acc = acc + _dot((a * a).astype(BF16), w2_ref[k:k + ff_chunk, :])
    y = x1 + _mod_slice(mod_ref, row, 5) * acc
    if n_y == 1:
        y_refs[0][...] = y
    else:
        is_prompt = pl.program_id(0) < N_PROMPT // tm

        @pl.when(is_prompt)
        def _():
            y_refs[0][...] = y

        @pl.when(jnp.logical_not(is_prompt))
        def _():
            y_refs[1][...] = y


def _post_mixer(xs, mixed, mod, g, wo, w1_all, w2_all, layer, split_out=False, tm=512, ff_chunk=1024):
    const = lambda i: (0, 0)
    of_layer = lambda w: pl.BlockSpec((None,) + w.shape[1:], lambda i: (layer, 0, 0), pipeline_mode=pl.Buffered(1))
    n_y = 2 if split_out else 1
    rows = (N_PROMPT, N_SAMPLE) if split_out else (N_TOK,)
    groups = tuple((len(grp), grp[0].shape[1]) for grp in mixed)
    mixed_specs = [spec for n, width in groups for spec in _stream_specs(n, tm, width)]
    out = pl.pallas_call(
        functools.partial(_post_kernel, tm=tm, ff_chunk=ff_chunk, n_x=len(xs), n_y=n_y, groups=groups),
        grid=(N_TOK // tm,),
        in_specs=_stream_specs(len(xs), tm) + mixed_specs + [
            pl.BlockSpec(mod.shape, const),
            pl.BlockSpec((1, D_MODEL), const),
            pl.BlockSpec(wo.shape, const, pipeline_mode=pl.Buffered(1)),
            of_layer(w1_all),
            of_layer(w2_all),
        ],
        out_specs=_stream_specs(n_y, tm),
        out_shape=[jax.ShapeDtypeStruct((r, D_MODEL), F32) for r in rows],
        compiler_params=_cparams("arbitrary"),
        name="post_mixer",
    )(*xs, *[a for grp in mixed for a in grp], mod, g.reshape(1, D_MODEL), wo, w1_all, w2_all)
    return tuple(out)


PAIR = 2 * HD_C


def _pair_consts():
    lane = lax.broadcasted_iota(jnp.int32, (1, PAIR), 1)
    first = lane < HD_C
    ones_col = [jnp.where(lane == HD_C, 1.0, 0.0), jnp.where(lane == 0, 1.0, 0.0)]
    r = lax.broadcasted_iota(jnp.int32, (2 * PAIR, PAIR), 0) % PAIR
    cidx = lax.broadcasted_iota(jnp.int32, (2 * PAIR, PAIR), 1)
    mean2 = jnp.where(r // HD_C == cidx // HD_C, 1.0 / HD_C, 0.0).astype(BF16)
    return first, ones_col, mean2


def _pair_norm(x, w2, mean2):
    hi, lo = _split2(x * x)
    ms = _dot(jnp.concatenate([hi, lo], axis=1), mean2)
    return x * lax.rsqrt(ms + EPS) * w2


def _pair_queries(q, first):
    return [jnp.where(first, q, 0.0).astype(BF16), jnp.where(first, 0.0, q).astype(BF16)]


def _pair_values(v, first, ones_col):
    return [jnp.where(first, v, ones_col[0]).astype(BF16), jnp.where(first, ones_col[1], v).astype(BF16)]


def _pair_output(o_aug, first):
    den = [o_aug[0][:, HD_C:HD_C + 1], o_aug[1][:, 0:1]]
    return jnp.where(first, o_aug[0] / den[0], o_aug[1] / den[1])


def _row_max(*pieces):
    tiles = [p[:, i:i + 128] for p in pieces for i in range(0, p.shape[1], 128)]
    return jnp.max(functools.reduce(jnp.maximum, tiles), axis=-1, keepdims=True)


CTX_PAIRS = 4


def _ctx_attn_kernel(q_ref, k_ref, v_ref, qn_ref, kn_ref, o_ref, kc_ref, vc_ref):
    first, ones_col, mean2 = _pair_consts()
    lanes = [slice(p * PAIR, (p + 1) * PAIR) for p in range(CTX_PAIRS)]
    qn = [_pair_norm(q_ref[:, ln], qn_ref[...], mean2) * HD_C ** -0.5 for ln in lanes]
    kn = [_pair_norm(k_ref[:, ln], kn_ref[...], mean2) for ln in lanes]
    v = [v_ref[:, ln] for ln in lanes]
    kt = [x.T for x in kn]
    for p in range(CTX_PAIRS):
        kc_ref[0, 0, lanes[p], :] = kt[p]
        vc_ref[0, 0, lanes[p], :] = v[p].T
    q = [_pair_queries(x, first) for x in qn]
    va = [_pair_values(x, first, ones_col) for x in v]
    s = [[_dot(q[p][j], kt[p].astype(BF16)) for j in range(2)] for p in range(CTX_PAIRS)]
    pr = [[jnp.exp(x - _row_max(x)).astype(BF16) for x in sp] for sp in s]
    for p in range(CTX_PAIRS):
        o_ref[:, lanes[p]] = _pair_output([_dot(pr[p][j], va[p][j]) for j in range(2)], first)


def _ctx_attention(qkv, qn, kn):
    heads = 2 * CTX_PAIRS
    ng = H_C // heads
    wide = CTX_PAIRS * PAIR
    blk = lambda off: pl.BlockSpec((SEQ, wide), lambda b, p: (b, off + p))
    cache_spec = pl.BlockSpec((1, 1, wide, SEQ), lambda b, p: (b, 0, p, 0))
    cache_shape = jax.ShapeDtypeStruct((BATCH, 1, H_C * HD_C, SEQ), F32)
    return pl.pallas_call(
        _ctx_attn_kernel,
        grid=(BATCH, ng),
        in_specs=[blk(0), blk(ng), blk(2 * ng),
                  pl.BlockSpec((1, PAIR), lambda b, p: (0, 0)),
                  pl.BlockSpec((1, PAIR), lambda b, p: (0, 0))],
        out_specs=[pl.BlockSpec((SEQ, wide), lambda b, p: (b, p)), cache_spec, cache_spec],
        out_shape=[jax.ShapeDtypeStruct((N_PROMPT, D_MODEL), F32), cache_shape, cache_shape],
        compiler_params=_cparams("arbitrary", "arbitrary"),
        name="ctx_attention",
    )(qkv, qkv, qkv, jnp.tile(qn.reshape(1, HD_C), (1, 2)), jnp.tile(kn.reshape(1, HD_C), (1, 2)))


def _na_row_start(r):
    return min(max(r - KH // 2, 0), GRID_ROWS - KH)


NA_ROW_GROUP = 4


def _na_attn_kernel(q_ref, k_ref, v_ref, kc_ref, vc_ref, qn_ref, kn_ref, bias_ref, o_ref, qs, ks, vs, bias_s):
    first, ones_col, mean2 = _pair_consts()

    @pl.when(pl.program_id(1) == 0)
    def _():
        q_col = lax.broadcasted_iota(jnp.int32, (GRID_W, PAIR), 0)
        lane = lax.broadcasted_iota(jnp.int32, (GRID_W, PAIR), 1)
        k_col = lane % GRID_W
        w0 = jnp.clip(q_col - KW // 2, 0, GRID_W - KW)
        outside = jnp.where((k_col >= w0) & (k_col < w0 + KW), 0.0, NEG_INF)
        n_dr = 2 * KH - 1
        for j in range(2):
            band = []
            for dr in range(n_dr):
                row = jnp.broadcast_to(bias_ref[j, dr:dr + 1, :], (GRID_W, PAIR))
                band.append([pltpu.roll(row, (half * GRID_W - (KW - 1)) % PAIR, axis=1, stride=1, stride_axis=0)
                             for half in range(2)])
            zero = jnp.zeros((GRID_W, PAIR), F32)
            for cp in range(2):
                for t in range(KH):
                    lo, hi = 2 * t + cp, 2 * t + cp + 1
                    tile = jnp.where(lane < GRID_W, band[lo][0] if lo < n_dr else zero,
                                     band[hi][1] if hi < n_dr else zero)
                    bias_s[j, cp, :, t * PAIR:(t + 1) * PAIR] = tile + outside

    q2 = _pair_queries(_pair_norm(q_ref[...], qn_ref[...], mean2) * HD_C ** -0.5, first)
    v2 = _pair_values(v_ref[...], first, ones_col)
    ks[...] = _pair_norm(k_ref[...], kn_ref[...], mean2).astype(BF16)
    for j in range(2):
        qs[j] = q2[j]
        vs[j] = v2[j]
    kt_ctx = kc_ref[0, 0].astype(BF16)
    vt = vc_ref[0, 0]
    ch = lax.broadcasted_iota(jnp.int32, vt.shape, 0)
    vt_ctx = [jnp.where(ch < HD_C, vt, jnp.where(ch == HD_C, 1.0, 0.0)).astype(BF16),
              jnp.where(ch < HD_C, jnp.where(ch == 0, 1.0, 0.0), vt).astype(BF16)]
    for r0 in range(0, GRID_ROWS, NA_ROW_GROUP):
        units = [(r, j) for r in range(r0, r0 + NA_ROW_GROUP) for j in range(2)]
        rows = {r: slice(r * GRID_W, (r + 1) * GRID_W) for r, _ in units}
        wins = {r: slice(_na_row_start(r) * GRID_W, (_na_row_start(r) + KH) * GRID_W) for r, _ in units}
        s_ctx_all = [_dot(qs[j, r0 * GRID_W:(r0 + NA_ROW_GROUP) * GRID_W, :], kt_ctx) for j in range(2)]
        s_ctx = [s_ctx_all[j][(r - r0) * GRID_W:(r - r0 + 1) * GRID_W] for r, j in units]
        s_win = []
        for r, j in units:
            dr0 = KH - 1 - (r - _na_row_start(r))
            lane0 = (dr0 - dr0 % 2) * GRID_W
            s_win.append(_dot_nt(qs[j, rows[r], :], ks[wins[r], :])
                         + bias_s[j, dr0 % 2, :, lane0:lane0 + KH * GRID_W])
        m = [_row_max(a, b) for a, b in zip(s_win, s_ctx)]
        p_win = [jnp.exp(a - mm).astype(BF16) for a, mm in zip(s_win, m)]
        p_ctx = [jnp.exp(b - mm).astype(BF16) for b, mm in zip(s_ctx, m)]
        o_aug = [_dot(p_win[i], vs[j, wins[r], :]) + _dot_nt(p_ctx[i], vt_ctx[j]) for i, (r, j) in enumerate(units)]
        for i in range(0, len(units), 2):
            o_ref[rows[units[i][0]], :] = _pair_output(o_aug[i:i + 2], first)


NA_BIAS_LANES = 2 * KH * GRID_W


def _na_attention(qkv, cache_kt, cache_vt, qn, kn, rpb):
    nhp = H_C // 2
    row0 = N_PROMPT // DEC_SEQ
    blk = lambda off: pl.BlockSpec((DEC_SEQ, 2 * HD_C), lambda p, b: (row0 + b, off + p))
    cache_spec = pl.BlockSpec((1, 1, PAIR, PAST_LEN), lambda p, b: (b, 0, p, 0))
    rpb_rows = 2 * KH
    bias = jnp.pad(rpb.astype(F32), ((0, 0), (0, rpb_rows - rpb.shape[1]), (0, PAIR - rpb.shape[2])))
    return pl.pallas_call(
        _na_attn_kernel,
        grid=(nhp, DEC_BATCH),
        in_specs=[blk(0), blk(nhp), blk(2 * nhp), cache_spec, cache_spec,
                  pl.BlockSpec((1, PAIR), lambda p, b: (0, 0)),
                  pl.BlockSpec((1, PAIR), lambda p, b: (0, 0)),
                  pl.BlockSpec((2, rpb_rows, PAIR), lambda p, b: (p, 0, 0))],
        out_specs=pl.BlockSpec((DEC_SEQ, 2 * HD_C), lambda p, b: (b, p)),
        out_shape=jax.ShapeDtypeStruct((N_SAMPLE, D_MODEL), F32),
        scratch_shapes=[pltpu.VMEM((2, DEC_SEQ, PAIR), BF16), pltpu.VMEM((DEC_SEQ, PAIR), BF16),
                        pltpu.VMEM((2, DEC_SEQ, PAIR), BF16), pltpu.VMEM((2, 2, GRID_W, NA_BIAS_LANES), F32)],
        compiler_params=_cparams("arbitrary", "arbitrary"),
        name="na_attention",
    )(qkv, qkv, qkv, cache_kt, cache_vt, jnp.tile(qn.reshape(1, HD_C), (1, 2)), jnp.tile(kn.reshape(1, HD_C), (1, 2)),
      bias)


def _seq_layout(prompt):
    return (SEQ, BATCH, 0) if prompt else (DEC_SEQ, DEC_BATCH, N_PROMPT // DEC_SEQ)


def _flip_blocks(m, c):
    r, s = m.shape
    return m.reshape(r // c, c, s // c, c)[:, ::-1, :, ::-1].reshape(r, s)


def _rms_gate(x, gn, gate):
    ms = jnp.mean(x * x, axis=-1, keepdims=True)
    return x * lax.rsqrt(ms + EPS) * gn * _silu(gate)


HG_LEVELS = tuple(CHUNK >> (i + 1) for i in range(CHUNK.bit_length() - 1))
HG_NL = len(HG_LEVELS)
HG_STACK = (HG_NL + 1) * CHUNK
TOT_ROWS = 16
HG_ROWS = (HG_NL + 2) * CHUNK + TOT_ROWS


def _hgrn_consts():
    c = CHUNK
    level_rows = []
    mask = np.zeros((HG_STACK, HG_STACK), np.float32)
    mask[:c, :c] = np.eye(c)
    for li, b in enumerate(HG_LEVELS):
        m = np.zeros((c, c), np.float32)
        blk = np.zeros((c, c), np.float32)
        for t in range(c):
            mid = (t // (2 * b)) * 2 * b + b
            if t >= mid:
                m[t, mid:t + 1] = 1.0
                blk[t, mid - b:mid] = 1.0
            else:
                m[t, t + 1:mid] = 1.0
        level_rows.append(m)
        mask[(li + 1) * c:(li + 2) * c, (li + 1) * c:(li + 2) * c] = blk
    dq = np.tril(np.ones((c, c), np.float32))
    dk = np.triu(np.ones((c, c), np.float32), 1)
    body = np.concatenate(level_rows + [dq, dk], axis=0)
    tot = np.ones((TOT_ROWS, c), np.float32)
    mcs, masks = [], []
    for reverse in (False, True):
        bm = _flip_blocks(body, c) if reverse else body
        mk = _flip_blocks(mask, c) if reverse else mask
        mc = np.concatenate([bm, tot], axis=0)
        mcs.append(np.concatenate([mc, mc], axis=1))
        masks.append(mk)
    return jnp.asarray(np.stack(mcs), BF16), jnp.asarray(np.stack(masks), F32)


HG_FAST = 64
HG_HALF = HG_FAST // 2
HG_FAST_ROWS = 4 * HG_FAST + TOT_ROWS
HG_SAFE_EXP = 40.0


def _hgrn_fast_consts(seq):
    c, m = HG_FAST, HG_HALF
    aq = np.zeros((c, c), np.float32)
    for t in range(c):
        if t >= m:
            aq[t, m:t + 1] = 1.0
        else:
            aq[t, t + 1:m] = -1.0
    dq = np.tril(np.ones((c, c), np.float32))
    dk = np.triu(np.ones((c, c), np.float32), 1)
    body = np.concatenate([aq, -aq, dq, dk], axis=0)
    tot = np.ones((TOT_ROWS, c), np.float32)
    causal = np.tril(np.ones((c, c), np.float32))
    mfs, masks = [], []
    for reverse in (False, True):
        bm = _flip_blocks(body, c) if reverse else body
        mf = np.concatenate([bm, tot], axis=0)
        mfs.append(np.concatenate([mf, mf], axis=1))
        masks.append(causal.T if reverse else causal)
    n_half = seq // m
    half = np.zeros((max(n_half, 16), seq), np.float32)
    for i in range(n_half):
        half[i, i * m:(i + 1) * m] = 1.0
    return jnp.asarray(np.stack(mfs), BF16), jnp.asarray(np.stack(masks), F32), jnp.asarray(half, BF16)


def _hgrn_kernel(*refs, seq, has_s0, emit_state):
    it = iter(refs)
    qa_ref, ff_ref, fb_ref, ia_ref, ga_ref, lb_ref, gn_ref, mc_ref, mask_ref = [next(it) for _ in range(9)]
    mf_ref, causal_ref, half_ref = [next(it) for _ in range(3)]
    s0_ref = next(it) if has_s0 else None
    o_ref = next(it)
    st_ref = next(it) if emit_state else None
    s_scr, acc, f_s, lf_s = [next(it) for _ in range(4)]
    c = CHUNK
    n_chunks = seq // c
    combos = [(d, h) for d in range(2) for h in range(H_A)]
    lanes = [slice(h * DK_A, (h + 1) * DK_A) for h in range(H_A)]
    add = lambda a, b: a + b

    lb_raw = lb_ref[...]
    lb_e = jnp.exp(lb_raw - jnp.max(lb_raw, axis=0, keepdims=True))
    lb_all = lb_e[0:1] / jnp.sum(lb_e, axis=0, keepdims=True)

    for d in range(2):
        for h in range(H_A):
            s_scr[d, h] = s0_ref[0, 0, d, h].T if has_s0 else jnp.zeros((DV_A, DK_A), F32)
    acc[...] = jnp.zeros(acc.shape, F32)

    worst = []
    for d, fr_ref in enumerate((ff_ref, fb_ref)):
        f = lb_all + (1.0 - lb_all) * _sigmoid(fr_ref[...])
        lf = jnp.log(f)
        f_s[d] = f
        lf_s[d] = lf
        worst.append(jnp.max(_dot(half_ref[...], (-lf).astype(BF16))))
    safe = jnp.maximum(worst[0], worst[1]) <= HG_SAFE_EXP

    def fast_body(n, carry):
        cf = HG_FAST
        n_fast = seq // cf
        rows = [pl.ds(pl.multiple_of((n if d == 0 else n_fast - 1 - n) * cf, cf), cf) for d in range(2)]
        e_all = [jnp.exp(_dot_const(mf_ref[d], lf_s[d, rows[d], :])) for d in range(2)]
        q_all = [_silu(qa_ref[rows[d], :]) * DK_A ** -0.5 for d in range(2)]
        k_all = [1.0 - f_s[d, rows[d], :] for d in range(2)]
        v_all = [ia_ref[rows[d], :].astype(BF16) for d in range(2)]
        st = [s_scr[d, h] for d, h in combos]
        qs = [q_all[d][:, lanes[h]] for d, h in combos]
        ks = [k_all[d][:, lanes[h]] for d, h in combos]
        vs = [v_all[d][:, lanes[h]] for d, h in combos]
        es = [[e_all[d][i * cf:(i + 1) * cf, lanes[h]] for i in range(4)] for d, h in combos]
        p = [jnp.where(causal_ref[d] > 0.0,
                       _dot_nt((qs[i] * es[i][0]).astype(BF16), (ks[i] * es[i][1]).astype(BF16)), 0.0).astype(BF16)
             for i, (d, h) in enumerate(combos)]
        o = [_dot(p[i], vs[i]) + _dot_nt((qs[i] * es[i][2]).astype(BF16), st[i].astype(BF16))
             for i in range(len(combos))]
        upd = [_dot_tn(vs[i], (ks[i] * es[i][3]).astype(BF16)) for i in range(len(combos))]
        for d in range(2):
            acc[rows[d], :] += jnp.concatenate(o[d * H_A:(d + 1) * H_A], axis=1)
        for i, (d, h) in enumerate(combos):
            s_scr[d, h] = st[i] * e_all[d][4 * cf:4 * cf + 1, lanes[h]] + upd[i]
        return carry

    def body(n, carry):
        rows = [pl.ds(pl.multiple_of((n if d == 0 else n_chunks - 1 - n) * c, c), c) for d in range(2)]
        f_all = [f_s[d, rows[d], :] for d in range(2)]
        e_all = [jnp.exp(_dot_const(mc_ref[d], lf_s[d, rows[d], :])) for d in range(2)]
        q_all = [_silu(qa_ref[rows[d], :]) * DK_A ** -0.5 for d in range(2)]
        v_all = [ia_ref[rows[d], :].astype(BF16) for d in range(2)]
        st = [s_scr[d, h] for d, h in combos]
        qs, ks, vs, es = [], [], [], []
        for d, h in combos:
            qs.append(q_all[d][:, lanes[h]])
            ks.append(1.0 - f_all[d][:, lanes[h]])
            vs.append(v_all[d][:, lanes[h]])
            es.append(e_all[d][:, lanes[h]])
        lvl = [[e[i * c:(i + 1) * c] for i in range(HG_NL + 2)] for e in es]
        qst = [jnp.concatenate([q] + [q * l[i] for i in range(HG_NL)], axis=0).astype(BF16) for q, l in zip(qs, lvl)]
        kst = [jnp.concatenate([k] + [k * l[i] for i in range(HG_NL)], axis=0).astype(BF16) for k, l in zip(ks, lvl)]
        r = [(_dot_nt(qst[i], kst[i]) * mask_ref[d]).astype(BF16) for i, (d, h) in enumerate(combos)]
        ost = [_dot(r[i], jnp.concatenate([vs[i]] * (HG_NL + 1), axis=0)) for i in range(len(combos))]
        inter = [_dot_nt((qs[i] * lvl[i][HG_NL]).astype(BF16), st[i].astype(BF16)) for i in range(len(combos))]
        upd = [_dot_tn(vs[i], (ks[i] * lvl[i][HG_NL + 1]).astype(BF16)) for i in range(len(combos))]
        o = [functools.reduce(lambda a, b: a + b, [ost[i][j * c:(j + 1) * c] for j in range(HG_NL + 1)]) + inter[i]
             for i in range(len(combos))]
        for d in range(2):
            acc[rows[d], :] += jnp.concatenate(o[d * H_A:(d + 1) * H_A], axis=1)
        for i, (d, h) in enumerate(combos):
            e_tot = es[i][(HG_NL + 2) * c:(HG_NL + 2) * c + 1]
            s_scr[d, h] = st[i] * e_tot + upd[i]
        return carry

    @pl.when(safe)
    def _():
        lax.fori_loop(0, seq // HG_FAST, fast_body, 0)

    @pl.when(jnp.logical_not(safe))
    def _():
        lax.fori_loop(0, n_chunks, body, 0)

    for h in range(H_A):
        ln = slice(h * DV_A, (h + 1) * DV_A)
        o_ref[:, ln] = _rms_gate(acc[:, ln], gn_ref[...], ga_ref[:, ln])
    if emit_state:
        for d in range(2):
            for h in range(H_A):
                st_ref[0, 0, d, h] = s_scr[d, h].T


def _hgrn(proj, hgrn_lb, gn, consts, prompt, s0=None, layer=0):
    seq, nb, rb0 = _seq_layout(prompt)
    consts = list(consts) + list(_hgrn_fast_consts(seq))
    wa = H_A * DK_A
    blk = lambda j: pl.BlockSpec((seq, wa), lambda b: (rb0 + b, j))
    const2 = lambda b: (0, 0)
    st_block = (1, 1, 2, H_A, DK_A, DV_A)
    in_specs = [blk(0), blk(1), blk(2), blk(3), blk(4),
                pl.BlockSpec(hgrn_lb.shape, const2), pl.BlockSpec((1, DV_A), const2)]
    in_specs += [pl.BlockSpec(m.shape, lambda b, nd=m.ndim: (0,) * nd) for m in consts]
    args = [proj] * 5 + [hgrn_lb, gn.reshape(1, DV_A)] + consts
    if s0 is not None:
        in_specs.append(pl.BlockSpec(st_block, lambda b: (b, layer, 0, 0, 0, 0)))
        args.append(s0)
    out_specs = [pl.BlockSpec((seq, wa), lambda b: (b, 0))]
    out_shape = [jax.ShapeDtypeStruct((nb * seq, wa), F32)]
    if prompt:
        out_specs.append(pl.BlockSpec(st_block, lambda b: (b, 0, 0, 0, 0, 0)))
        out_shape.append(jax.ShapeDtypeStruct((nb, 1, 2, H_A, DK_A, DV_A), F32))
    return pl.pallas_call(
        functools.partial(_hgrn_kernel, seq=seq, has_s0=s0 is not None, emit_state=prompt),
        grid=(nb,),
        in_specs=in_specs,
        out_specs=out_specs,
        out_shape=out_shape,
        scratch_shapes=[pltpu.VMEM((2, H_A, DV_A, DK_A), F32), pltpu.VMEM((seq, wa), F32),
                        pltpu.VMEM((2, seq, wa), F32), pltpu.VMEM((2, seq, wa), F32)],
        compiler_params=_cparams("arbitrary"),
        name="hgrn_prompt" if prompt else "hgrn_sample",
    )(*args)


GD_SUB = GBLK // CHUNK
GD_ROWS = 2 * GBLK + TOT_ROWS


def _gdn_consts():
    n, c = GBLK, CHUNK
    same = (np.arange(n)[:, None] // c) == (np.arange(n)[None, :] // c)
    tri = (same & (np.arange(n)[None, :] <= np.arange(n)[:, None])).astype(np.float32)
    sup = (same & (np.arange(n)[None, :] > np.arange(n)[:, None])).astype(np.float32)
    tot = np.zeros((TOT_ROWS, n), np.float32)
    for s in range(GD_SUB):
        tot[s, s * c:(s + 1) * c] = 1.0
    mgs, tts, tris = [], [], []
    for reverse in (False, True):
        t = _flip_blocks(tri, c) if reverse else tri
        s = _flip_blocks(sup, c) if reverse else sup
        mg = np.concatenate([t, s, tot], axis=0)
        mgs.append(np.concatenate([mg, mg], axis=1))
        tts.append(np.concatenate([t.T, t.T], axis=0))
        tris.append(t)
    tris.append(same.astype(np.float32))
    return jnp.asarray(np.stack(mgs), BF16), jnp.asarray(np.stack(tts), BF16), jnp.asarray(np.stack(tris), F32)


def _softplus(x):
    return jnp.maximum(x, 0.0) + jnp.log(1.0 + jnp.exp(-jnp.abs(x)))


def _conv_silu(x, w, seq):
    half = SHORT_CONV // 2
    pad = jnp.zeros((8, x.shape[1]), x.dtype)
    xe = jnp.concatenate([pad, x, pad], axis=0)
    acc = xe * w[half:half + 1]
    for j in range(SHORT_CONV):
        shift = half - j
        if shift != 0:
            acc = acc + pltpu.roll(xe, shift % (seq + 16), axis=0) * w[j:j + 1]
    return _silu(acc[8:seq + 8])


def _l2norm_heads(x, n_heads, width, scale):
    outs = []
    for h in range(n_heads):
        xh = x[:, h * width:(h + 1) * width]
        outs.append(xh * (lax.rsqrt(jnp.sum(xh * xh, axis=-1, keepdims=True) + EPS) * scale))
    return jnp.concatenate(outs, axis=-1)


def _gdn_kernel(*refs, seq, has_s0, emit_state):
    it = iter(refs)
    (q_ref, k_ref, v_ref, gb_ref, gate_ref, cw_ref, alog_ref, dt_ref, gn_ref,
     mg_ref, tt_ref, tri_ref) = [next(it) for _ in range(12)]
    s0_ref = next(it) if has_s0 else None
    o_ref = next(it)
    st_ref = next(it) if emit_state else None
    qn, kn, vn, u_s, w_s, qg_s, kdt_s, at_s, et_s, s_scr, acc = [next(it) for _ in range(11)]
    c = CHUNK
    n_chunks = seq // c
    n_blocks = seq // GBLK
    wq = H_B * DK_B
    n_dh = 2 * H_B
    combos = [(d, h) for d in range(2) for h in range(H_B)]
    lanes = [slice(h * DK_B, (h + 1) * DK_B) for h in range(H_B)]

    qn[...] = _l2norm_heads(_conv_silu(q_ref[...], cw_ref[:, 0:wq], seq), H_B, DK_B, DK_B ** -0.5)
    kn[...] = _l2norm_heads(_conv_silu(k_ref[...], cw_ref[:, wq:2 * wq], seq), H_B, DK_B, 1.0)
    vn[...] = _conv_silu(v_ref[...], cw_ref[:, 2 * wq:3 * wq], seq)
    for i in range(2 * H_B):
        s_scr[i] = s0_ref[0, 0, i // H_B, i % H_B] if has_s0 else jnp.zeros((DK_B, DV_B), F32)
    acc[...] = jnp.zeros(acc.shape, F32)

    eye = (lax.broadcasted_iota(jnp.int32, (GBLK, GBLK), 0)
           == lax.broadcasted_iota(jnp.int32, (GBLK, GBLK), 1)).astype(F32)
    eye_pk = (lax.broadcasted_iota(jnp.int32, (c, GBLK), 0)
              == lax.broadcasted_iota(jnp.int32, (c, GBLK), 1) % c).astype(F32)
    bwd_lane = lax.broadcasted_iota(jnp.int32, (1, 128), 1) % n_dh >= H_B
    add = lambda a, b: a + b

    def expand(pk):
        return jnp.concatenate([pk] * GD_SUB, axis=0) * tri_ref[2]

    def pack(bd):
        return functools.reduce(add, [bd[s * c:(s + 1) * c] for s in range(GD_SUB)])

    def dot3_split(a, b_hi, b_lo):
        ah, al = _split2(a)
        t = _dot(jnp.concatenate([ah, al], axis=0), b_hi)
        return t[:a.shape[0]] + t[a.shape[0]:] + _dot(ah, b_lo)

    def block_body(blk, carry):
        rows = pl.ds(pl.multiple_of(blk * GBLK, GBLK), GBLK)
        gates = gate_ref[rows, :]
        glog_all = -jnp.exp(alog_ref[...]) * _softplus(gates + dt_ref[...])
        beta_all = _sigmoid(gates)
        g2 = jnp.concatenate(_split2(glog_all), axis=0)
        dg = [_dot(mg_ref[d], g2) for d in range(2)]
        dsel = jnp.where(bwd_lane, dg[1], dg[0])
        eg_all = jnp.exp(dsel)
        g_all = dsel[:GBLK]
        gt = [_dot_tn(g2, tt_ref[d]) for d in range(2)]
        qs = [qn[rows, ln] for ln in lanes]
        ks = [kn[rows, ln] for ln in lanes]
        vs = [vn[rows, ln] for ln in lanes]
        col = lambda x, j: jnp.broadcast_to(x[:, j:j + 1], (GBLK, DK_B))
        betas = [col(beta_all, n_dh + i) for i in range(n_dh)]
        kbs = [ks[h] * betas[i] for i, (d, h) in enumerate(combos)]
        kk = [_dot_nt(jnp.concatenate([qs[h], kbs[h], kbs[H_B + h]], axis=0).astype(BF16), ks[h].astype(BF16))
              for h in range(H_B)]
        decay = []
        for i, (d, h) in enumerate(combos):
            inside = tri_ref[d] > 0.0
            gd = col(g_all, i) - gt[d][i:i + 1, :]
            decay.append(jnp.where(inside, jnp.exp(jnp.where(inside, gd, 0.0)), 0.0))
        attn = [kk[h][:GBLK] * decay[i] for i, (d, h) in enumerate(combos)]
        p_pk = [pack(kk[h][(1 + d) * GBLK:(2 + d) * GBLK] * decay[i] * (1.0 - eye)) for i, (d, h) in enumerate(combos)]
        x_pk = [eye_pk - p for p in p_pk]
        p_bd = [_split2(expand(p)) for p in p_pk]
        for _ in range(CHUNK.bit_length() - 2):
            p_pk = [dot3_split(p, *b) for p, b in zip(p_pk, p_bd)]
            p_bd = [_split2(expand(p)) for p in p_pk]
            x_pk = [x + dot3_split(x, *b) for x, b in zip(x_pk, p_bd)]
        eg_col = [col(eg_all[:GBLK], i) for i in range(n_dh)]
        ekd_col = [col(eg_all[GBLK:2 * GBLK], i) for i in range(n_dh)]
        rhs = [jnp.concatenate([vs[h] * betas[i], kbs[i] * eg_col[i]], axis=1) for i, (d, h) in enumerate(combos)]
        uw = [_dot3(expand(x_pk[i]), rhs[i]) for i in range(n_dh)]
        for i, (d, h) in enumerate(combos):
            qg = (qs[h] * eg_col[i]).astype(BF16)
            kdt_s[i, blk] = (ks[h] * ekd_col[i]).T.astype(BF16)
            for s in range(GD_SUB):
                cn = blk * GD_SUB + s
                r = slice(s * c, (s + 1) * c)
                u_s[i, cn] = uw[i][r, :DV_B]
                w_s[i, cn] = uw[i][r, DV_B:].astype(BF16)
                qg_s[i, cn] = qg[r]
                at_s[i, cn] = attn[i][r].astype(BF16)
                et_s[i, cn] = jnp.broadcast_to(eg_all[2 * GBLK + s:2 * GBLK + s + 1, i:i + 1], (8, DV_B))
        return carry

    lax.fori_loop(0, n_blocks, block_body, 0)

    def chunk_body(n, carry):
        cns = [n, n_chunks - 1 - n]
        rows = [pl.ds(pl.multiple_of(cn * c, c), c) for cn in cns]
        sub_of_row = lax.broadcasted_iota(jnp.int32, (GBLK, 1), 0) // c
        in_chunk = [sub_of_row == cn % GD_SUB for cn in cns]
        st = [s_scr[i] for i in range(n_dh)]
        ws = [_dot(jnp.concatenate([w_s[i, cns[d]], qg_s[i, cns[d]]], axis=0), st[i].astype(BF16))
              for i, (d, h) in enumerate(combos)]
        vblk = [jnp.where(in_chunk[d], jnp.concatenate([u_s[i, cns[d]] - ws[i][:c]] * GD_SUB, axis=0), 0.0).astype(BF16)
                for i, (d, h) in enumerate(combos)]
        r = [_dot(jnp.concatenate([at_s[i, cns[d]], kdt_s[i, cns[d] // GD_SUB]], axis=0), vblk[i])
             for i, (d, h) in enumerate(combos)]
        for d in range(2):
            acc[rows[d], :] += jnp.concatenate([ws[i][c:] + r[i][:c] for i in range(d * H_B, (d + 1) * H_B)], axis=1)
        for i, (d, h) in enumerate(combos):
            s_scr[i] = st[i] * et_s[i, cns[d]][0:1] + r[i][c:]
        return carry

    lax.fori_loop(0, n_chunks, chunk_body, 0)

    for h in range(H_B):
        ln = slice(h * DV_B, (h + 1) * DV_B)
        o_ref[:, ln] = _rms_gate(acc[:, ln], gn_ref[...], gb_ref[:, ln])
    if emit_state:
        for i in range(2 * H_B):
            st_ref[0, 0, i // H_B, i % H_B] = s_scr[i]


def _gdn(proj, gates, conv_w, a_log, dt_bias, gn, consts, prompt, s0=None, layer=0):
    seq, nb, rb0 = _seq_layout(prompt)
    n_chunks = seq // CHUNK
    wq = H_B * DK_B
    blk = lambda j: pl.BlockSpec((seq, wq), lambda b: (rb0 + b, j))
    const2 = lambda b: (0, 0)
    const3 = lambda b: (0, 0, 0)
    st_block = (1, 1, 2, H_B, DK_B, DV_B)
    pad_row = lambda p: jnp.pad(p.reshape(1, -1).astype(F32), ((0, 0), (0, 128 - p.size)))
    in_specs = [blk(5), blk(6), blk(7), blk(8),
                pl.BlockSpec((seq, 128), lambda b: (rb0 + b, 0)),
                pl.BlockSpec((SHORT_CONV, 3 * wq), const2),
                pl.BlockSpec((1, 128), const2), pl.BlockSpec((1, 128), const2), pl.BlockSpec((1, DV_B), const2)]
    in_specs += [pl.BlockSpec(m.shape, const3) for m in consts]
    args = [proj] * 4 + [gates, conv_w.reshape(SHORT_CONV, 3 * wq), pad_row(a_log), pad_row(dt_bias),
                         gn.reshape(1, DV_B)] + list(consts)
    if s0 is not None:
        in_specs.append(pl.BlockSpec(st_block, lambda b: (b, layer, 0, 0, 0, 0)))
        args.append(s0)
    out_specs = [pl.BlockSpec((seq, wq), lambda b: (b, 0))]
    out_shape = [jax.ShapeDtypeStruct((nb * seq, wq), F32)]
    if prompt:
        out_specs.append(pl.BlockSpec(st_block, lambda b: (b, 0, 0, 0, 0, 0)))
        out_shape.append(jax.ShapeDtypeStruct((nb, 1, 2, H_B, DK_B, DV_B), F32))
    n_dh = 2 * H_B
    scratch = ([pltpu.VMEM((seq, wq), F32)] * 3
               + [pltpu.VMEM((n_dh, n_chunks, CHUNK, DV_B), F32)]
               + [pltpu.VMEM((n_dh, n_chunks, CHUNK, DK_B), BF16)] * 2
               + [pltpu.VMEM((n_dh, seq // GBLK, DK_B, GBLK), BF16),
                  pltpu.VMEM((n_dh, n_chunks, CHUNK, GBLK), BF16),
                  pltpu.VMEM((n_dh, n_chunks, 8, DV_B), F32),
                  pltpu.VMEM((n_dh, DK_B, DV_B), F32),
                  pltpu.VMEM((seq, wq), F32)])
    return pl.pallas_call(
        functools.partial(_gdn_kernel, seq=seq, has_s0=s0 is not None, emit_state=prompt),
        grid=(nb,),
        in_specs=in_specs,
        out_specs=out_specs,
        out_shape=out_shape,
        scratch_shapes=scratch,
        compiler_params=_cparams("arbitrary"),
        name="gdn_prompt" if prompt else "gdn_sample",
    )(*args)


def kernel(x_prompt, x_sample, state_hgrn, state_gdn, cache_na_k, cache_na_v, c, c_ctx, ada_w, ada_b, norm_g, w_in_ab, w_out_ab, hgrn_lb, gdn_conv, gdn_a_log, gdn_dt_bias, gn_hgrn, gn_gdn, w_qkv_na, qn_na, kn_na, rpb_na, w_out_na, w_mlp1, w_mlp2):
    cond = jnp.concatenate([c_ctx[None, :], c, jnp.zeros((N_MOD_ROWS - 1 - DEC_BATCH, D_MODEL), F32)], axis=0)
    mods = _modulation(cond, ada_w, ada_b)
    xs = (x_prompt.reshape(N_PROMPT, D_MODEL), x_sample.reshape(N_SAMPLE, D_MODEL))

    w_in = w_in_ab[0].astype(BF16)
    w_gate = jnp.pad(w_in[:, D_MAIN_AB:], ((0, 0), (0, 128 - N_GATE_AB)))
    proj, gates = _norm_proj(xs, mods[0], norm_g[0, 0], [w_in, w_gate], widths=[D_MAIN_AB, 128])
    w_mlp1_bf, w_mlp2_bf = w_mlp1.astype(BF16), w_mlp2.astype(BF16)
    hg_consts = _hgrn_consts()
    gd_consts = _gdn_consts()
    hg_prompt, new_hgrn = _hgrn(proj, hgrn_lb, gn_hgrn[0], hg_consts, True)
    hg_sample, = _hgrn(proj, hgrn_lb, gn_hgrn[0], hg_consts, False, s0=state_hgrn)
    gd_args = (gdn_conv[0], gdn_a_log[0], gdn_dt_bias[0], gn_gdn[0], gd_consts)
    gd_prompt, new_gdn = _gdn(proj, gates, *gd_args, True)
    gd_sample, = _gdn(proj, gates, *gd_args, False, s0=state_gdn)
    xs = _post_mixer(xs, [(hg_prompt, hg_sample), (gd_prompt, gd_sample)], mods[0], norm_g[0, 1],
                     w_out_ab[0].astype(BF16), w_mlp1_bf, w_mlp2_bf, 0)

    qkv, = _norm_proj(xs, mods[1], norm_g[1, 0], [w_qkv_na[0].astype(BF16)])
    at_prompt, new_kt, new_vt = _ctx_attention(qkv, qn_na[0], kn_na[0])
    time_minor = lambda a: jnp.swapaxes(a, -1, -2).reshape(a.shape[0], 1, H_C * HD_C, a.shape[3])
    at_sample = _na_attention(qkv, time_minor(cache_na_k), time_minor(cache_na_v), qn_na[0], kn_na[0], rpb_na[0])
    time_major = lambda a: jnp.swapaxes(a.reshape(BATCH, 1, H_C, HD_C, SEQ), -1, -2)
    new_k, new_v = time_major(new_kt), time_major(new_vt)
    y_prompt, y_sample = _post_mixer(xs, [(at_prompt, at_sample)], mods[1], norm_g[1, 1], w_out_na[0].astype(BF16),
                                     w_mlp1_bf, w_mlp2_bf, 1, split_out=True)

    return (y_prompt.reshape(BATCH, SEQ, D_MODEL), y_sample.reshape(DEC_BATCH, DEC_SEQ, D_MODEL),
            new_hgrn, new_gdn, new_k, new_v)
```

```python
import functools

import numpy as np
import jax
import jax.numpy as jnp
from jax import lax
from jax.experimental import pallas as pl
from jax.experimental.pallas import tpu as pltpu

F32 = jnp.float32
BF16 = jnp.bfloat16

D_MODEL = 1024
BATCH = 16
SEQ = 256
DEC_BATCH = 4
DEC_SEQ = 1024
PAST_LEN = 256
N_PROMPT = BATCH * SEQ
N_SAMPLE = DEC_BATCH * DEC_SEQ
N_TOK = N_PROMPT + N_SAMPLE
GRID_W = 64
GRID_ROWS = DEC_SEQ // GRID_W
H_A = 4
DK_A = 128
DV_A = 128
H_B = 4
DK_B = 128
DV_B = 128
SHORT_CONV = 5
H_C = 16
HD_C = 64
KH = 8
KW = 16
D_FF = 4 * D_MODEL
EPS = 1e-6
NEG_INF = -1e30
N_MOD_ROWS = 8
D_MAIN_AB = 4608
N_GATE_AB = 16
CHUNK = 32
GBLK = 128
VMEM_LIMIT = 56 * 1024 * 1024


def _cparams(*sem):
    return pltpu.CompilerParams(dimension_semantics=sem, vmem_limit_bytes=VMEM_LIMIT)


def _sigmoid(x):
    return 1.0 / (1.0 + jnp.exp(-x))


def _silu(x):
    return x * _sigmoid(x)


def _dot(a, b):
    return jnp.dot(a, b, preferred_element_type=F32)


def _dot_nt(a, b):
    return lax.dot_general(a, b, (((1,), (1,)), ((), ())), preferred_element_type=F32)


def _dot_tn(a, b):
    return lax.dot_general(a, b, (((0,), (0,)), ((), ())), preferred_element_type=F32)


def _split2(x):
    hi = x.astype(BF16)
    lo = (x - hi.astype(F32)).astype(BF16)
    return hi, lo


def _dot_const(m2, x):
    hi, lo = _split2(x)
    return _dot(m2, jnp.concatenate([hi, lo], axis=0))


def _dot3(a, b):
    ah, al = _split2(a)
    bh, bl = _split2(b)
    return _dot(ah, bh) + (_dot(ah, bl) + _dot(al, bh))


def _mod_row(i, tm):
    start = i * tm
    return jnp.where(start < N_PROMPT, 0, 1 + (start - N_PROMPT) // DEC_SEQ)


def _mod_slice(mod_ref, row, k):
    return mod_ref[pl.ds(row, 1), k * D_MODEL:(k + 1) * D_MODEL]


def _norm_mod(x, g, sc, sh):
    ms = jnp.mean(x * x, axis=-1, keepdims=True)
    return (x * lax.rsqrt(ms + EPS) * g) * (1.0 + sc) + sh


def _mod_kernel(cond_ref, w_ref, b_ref, o_ref):
    s = _silu(cond_ref[...]).astype(BF16)
    o_ref[0] = _dot(s, w_ref[0].astype(BF16)) + b_ref[0]


def _modulation(cond8, ada_w, ada_b):
    depth = ada_w.shape[0]
    tn = 1024
    nj = ada_w.shape[2] // tn
    return pl.pallas_call(
        _mod_kernel,
        grid=(depth, nj),
        in_specs=[
            pl.BlockSpec((N_MOD_ROWS, D_MODEL), lambda l, j: (0, 0)),
            pl.BlockSpec((1, D_MODEL, tn), lambda l, j: (l, 0, j)),
            pl.BlockSpec((1, 1, tn), lambda l, j: (l, 0, j)),
        ],
        out_specs=pl.BlockSpec((1, N_MOD_ROWS, tn), lambda l, j: (l, 0, j)),
        out_shape=jax.ShapeDtypeStruct((depth, N_MOD_ROWS, ada_w.shape[2]), F32),
        compiler_params=_cparams("arbitrary", "arbitrary"),
        name="modulation",
    )(cond8, ada_w, ada_b.reshape(depth, 1, -1))


def _stream_specs(n_arrays, tm, width=D_MODEL):
    if n_arrays == 1:
        return [pl.BlockSpec((tm, width), lambda i: (i, 0))]
    npt = N_PROMPT // tm
    return [pl.BlockSpec((tm, width), lambda i: (jnp.minimum(i, npt - 1), 0)),
            pl.BlockSpec((tm, width), lambda i: (jnp.maximum(i - npt, 0), 0))]


def _stream_load(x_refs, tm):
    if len(x_refs) == 1:
        return x_refs[0][...]
    return jnp.where(pl.program_id(0) < N_PROMPT // tm, x_refs[0][...], x_refs[1][...])


def _norm_proj_kernel(*refs, tm, n_x, n_w):
    x_refs, (mod_ref, g_ref) = refs[:n_x], refs[n_x:n_x + 2]
    w_refs, o_refs = refs[n_x + 2:n_x + 2 + n_w], refs[n_x + 2 + n_w:]
    row = _mod_row(pl.program_id(0), tm)
    h = _norm_mod(_stream_load(x_refs, tm), g_ref[...], _mod_slice(mod_ref, row, 1), _mod_slice(mod_ref, row, 0)).astype(BF16)
    for w_ref, o_ref in zip(w_refs, o_refs):
        o_ref[...] = _dot(h, w_ref[...])


def _norm_proj(xs, mod, g, ws, widths=None, tm=512):
    n_w = len(ws)
    widths = widths or [w.shape[1] for w in ws]
    const = lambda i: (0, 0)
    return pl.pallas_call(
        functools.partial(_norm_proj_kernel, tm=tm, n_x=len(xs), n_w=n_w),
        grid=(N_TOK // tm,),
        in_specs=_stream_specs(len(xs), tm) + [
            pl.BlockSpec(mod.shape, const),
            pl.BlockSpec((1, D_MODEL), const),
        ] + [pl.BlockSpec((D_MODEL, n), const, pipeline_mode=pl.Buffered(1)) for n in widths],
        out_specs=[pl.BlockSpec((tm, n), lambda i: (i, 0)) for n in widths],
        out_shape=[jax.ShapeDtypeStruct((N_TOK, n), F32) for n in widths],
        compiler_params=_cparams("arbitrary"),
        name="norm_proj",
    )(*xs, mod, g.reshape(1, D_MODEL), *ws)


def _post_kernel(*refs, tm, ff_chunk, n_x, n_y, groups):
    x_refs = refs[:n_x]
    n_m = sum(n for n, _ in groups)
    m_refs = refs[n_x:n_x + n_m]
    mod_ref, g_ref, wo_ref, w1_ref, w2_ref = refs[n_x + n_m:n_x + n_m + 5]
    y_refs = refs[n_x + n_m + 5:]
    row = _mod_row(pl.program_id(0), tm)
    mix, first_ref, first_col = None, 0, 0
    for n, width in groups:
        part = _stream_load(m_refs[first_ref:first_ref + n], tm).astype(BF16)
        term = _dot(part, wo_ref[first_col:first_col + width, :])
        mix = term if mix is None else mix + term
        first_ref, first_col = first_ref + n, first_col + width
    x1 = _stream_load(x_refs, tm) + _mod_slice(mod_ref, row, 2) * mix
    h = _norm_mod(x1, g_ref[...], _mod_slice(mod_ref, row, 4), _mod_slice(mod_ref, row, 3)).astype(BF16)
    acc = jnp.zeros((tm, D_MODEL), F32)
    for k in range(0, D_FF, ff_chunk):
        a = jnp.maximum(_dot(h, w1_ref[:, k:k + ff_chunk]), 0.0)
        acc = acc + _dot((a * a).astype(BF16), w2_ref[k:k + ff_chunk, :])
    y = x1 + _mod_slice(mod_ref, row, 5) * acc
    if n_y == 1:
        y_refs[0][...] = y
    else:
        is_prompt = pl.program_id(0) < N_PROMPT // tm

        @pl.when(is_prompt)
        def _():
            y_refs[0][...] = y

        @pl.when(jnp.logical_not(is_prompt))
        def _():
            y_refs[1][...] = y


def _post_mixer(xs, mixed, mod, g, wo, w1_all, w2_all, layer, split_out=False, tm=512, ff_chunk=1024):
    const = lambda i: (0, 0)
    of_layer = lambda w: pl.BlockSpec((None,) + w.shape[1:], lambda i: (layer, 0, 0), pipeline_mode=pl.Buffered(1))
    n_y = 2 if split_out else 1
    rows = (N_PROMPT, N_SAMPLE) if split_out else (N_TOK,)
    groups = tuple((len(grp), grp[0].shape[1]) for grp in mixed)
    mixed_specs = [spec for n, width in groups for spec in _stream_specs(n, tm, width)]
    out = pl.pallas_call(
        functools.partial(_post_kernel, tm=tm, ff_chunk=ff_chunk, n_x=len(xs), n_y=n_y, groups=groups),
        grid=(N_TOK // tm,),
        in_specs=_stream_specs(len(xs), tm) + mixed_specs + [
            pl.BlockSpec(mod.shape, const),
            pl.BlockSpec((1, D_MODEL), const),
            pl.BlockSpec(wo.shape, const, pipeline_mode=pl.Buffered(1)),
            of_layer(w1_all),
            of_layer(w2_all),
        ],
        out_specs=_stream_specs(n_y, tm),
        out_shape=[jax.ShapeDtypeStruct((r, D_MODEL), F32) for r in rows],
        compiler_params=_cparams("arbitrary"),
        name="post_mixer",
    )(*xs, *[a for grp in mixed for a in grp], mod, g.reshape(1, D_MODEL), wo, w1_all, w2_all)
    return tuple(out)


PAIR = 2 * HD_C


def _pair_consts():
    lane = lax.broadcasted_iota(jnp.int32, (1, PAIR), 1)
    first = lane < HD_C
    ones_col = [jnp.where(lane == HD_C, 1.0, 0.0), jnp.where(lane == 0, 1.0, 0.0)]
    r = lax.broadcasted_iota(jnp.int32, (2 * PAIR, PAIR), 0) % PAIR
    cidx = lax.broadcasted_iota(jnp.int32, (2 * PAIR, PAIR), 1)
    mean2 = jnp.where(r // HD_C == cidx // HD_C, 1.0 / HD_C, 0.0).astype(BF16)
    return first, ones_col, mean2


def _pair_norm(x, w2, mean2):
    hi, lo = _split2(x * x)
    ms = _dot(jnp.concatenate([hi, lo], axis=1), mean2)
    return x * lax.rsqrt(ms + EPS) * w2


def _pair_queries(q, first):
    return [jnp.where(first, q, 0.0).astype(BF16), jnp.where(first, 0.0, q).astype(BF16)]


def _pair_values(v, first, ones_col):
    return [jnp.where(first, v, ones_col[0]).astype(BF16), jnp.where(first, ones_col[1], v).astype(BF16)]


def _pair_output(o_aug, first):
    den = [o_aug[0][:, HD_C:HD_C + 1], o_aug[1][:, 0:1]]
    return jnp.where(first, o_aug[0] / den[0], o_aug[1] / den[1])


def _row_max(*pieces):
    tiles = [p[:, i:i + 128] for p in pieces for i in range(0, p.shape[1], 128)]
    return jnp.max(functools.reduce(jnp.maximum, tiles), axis=-1, keepdims=True)


CTX_PAIRS = 4


def _ctx_attn_kernel(q_ref, k_ref, v_ref, qn_ref, kn_ref, o_ref, kc_ref, vc_ref):
    first, ones_col, mean2 = _pair_consts()
    lanes = [slice(p * PAIR, (p + 1) * PAIR) for p in range(CTX_PAIRS)]
    qn = [_pair_norm(q_ref[:, ln], qn_ref[...], mean2) * HD_C ** -0.5 for ln in lanes]
    kn = [_pair_norm(k_ref[:, ln], kn_ref[...], mean2) for ln in lanes]
    v = [v_ref[:, ln] for ln in lanes]
    kt = [x.T for x in kn]
    for p in range(CTX_PAIRS):
        kc_ref[0, 0, lanes[p], :] = kt[p]
        vc_ref[0, 0, lanes[p], :] = v[p].T
    q = [_pair_queries(x, first) for x in qn]
    va = [_pair_values(x, first, ones_col) for x in v]
    s = [[_dot(q[p][j], kt[p].astype(BF16)) for j in range(2)] for p in range(CTX_PAIRS)]
    pr = [[jnp.exp(x - _row_max(x)).astype(BF16) for x in sp] for sp in s]
    for p in range(CTX_PAIRS):
        o_ref[:, lanes[p]] = _pair_output([_dot(pr[p][j], va[p][j]) for j in range(2)], first)


def _ctx_attention(qkv, qn, kn):
    heads = 2 * CTX_PAIRS
    ng = H_C // heads
    wide = CTX_PAIRS * PAIR
    blk = lambda off: pl.BlockSpec((SEQ, wide), lambda b, p: (b, off + p))
    cache_spec = pl.BlockSpec((1, 1, wide, SEQ), lambda b, p: (b, 0, p, 0))
    cache_shape = jax.ShapeDtypeStruct((BATCH, 1, H_C * HD_C, SEQ), F32)
    return pl.pallas_call(
        _ctx_attn_kernel,
        grid=(BATCH, ng),
        in_specs=[blk(0), blk(ng), blk(2 * ng),
                  pl.BlockSpec((1, PAIR), lambda b, p: (0, 0)),
                  pl.BlockSpec((1, PAIR), lambda b, p: (0, 0))],
        out_specs=[pl.BlockSpec((SEQ, wide), lambda b, p: (b, p)), cache_spec, cache_spec],
        out_shape=[jax.ShapeDtypeStruct((N_PROMPT, D_MODEL), F32), cache_shape, cache_shape],
        compiler_params=_cparams("arbitrary", "arbitrary"),
        name="ctx_attention",
    )(qkv, qkv, qkv, jnp.tile(qn.reshape(1, HD_C), (1, 2)), jnp.tile(kn.reshape(1, HD_C), (1, 2)))


def _na_row_start(r):
    return min(max(r - KH // 2, 0), GRID_ROWS - KH)


NA_ROW_GROUP = 4


def _na_attn_kernel(q_ref, k_ref, v_ref, kc_ref, vc_ref, qn_ref, kn_ref, bias_ref, o_ref, qs, ks, vs, bias_s):
    first, ones_col, mean2 = _pair_consts()

    @pl.when(pl.program_id(1) == 0)
    def _():
        q_col = lax.broadcasted_iota(jnp.int32, (GRID_W, PAIR), 0)
        lane = lax.broadcasted_iota(jnp.int32, (GRID_W, PAIR), 1)
        k_col = lane % GRID_W
        w0 = jnp.clip(q_col - KW // 2, 0, GRID_W - KW)
        outside = jnp.where((k_col >= w0) & (k_col < w0 + KW), 0.0, NEG_INF)
        n_dr = 2 * KH - 1
        for j in range(2):
            band = []
            for dr in range(n_dr):
                row = jnp.broadcast_to(bias_ref[j, dr:dr + 1, :], (GRID_W, PAIR))
                band.append([pltpu.roll(row, (half * GRID_W - (KW - 1)) % PAIR, axis=1, stride=1, stride_axis=0)
                             for half in range(2)])
            zero = jnp.zeros((GRID_W, PAIR), F32)
            for cp in range(2):
                for t in range(KH):
                    lo, hi = 2 * t + cp, 2 * t + cp + 1
                    tile = jnp.where(lane < GRID_W, band[lo][0] if lo < n_dr else zero,
                                     band[hi][1] if hi < n_dr else zero)
                    bias_s[j, cp, :, t * PAIR:(t + 1) * PAIR] = tile + outside

    q2 = _pair_queries(_pair_norm(q_ref[...], qn_ref[...], mean2) * HD_C ** -0.5, first)
    v2 = _pair_values(v_ref[...], first, ones_col)
    ks[...] = _pair_norm(k_ref[...], kn_ref[...], mean2).astype(BF16)
    for j in range(2):
        qs[j] = q2[j]
        vs[j] = v2[j]
    kt_ctx = kc_ref[0, 0].astype(BF16)
    vt = vc_ref[0, 0]
    ch = lax.broadcasted_iota(jnp.int32, vt.shape, 0)
    vt_ctx = [jnp.where(ch < HD_C, vt, jnp.where(ch == HD_C, 1.0, 0.0)).astype(BF16),
              jnp.where(ch < HD_C, jnp.where(ch == 0, 1.0, 0.0), vt).astype(BF16)]
    for r0 in range(0, GRID_ROWS, NA_ROW_GROUP):
        units = [(r, j) for r in range(r0, r0 + NA_ROW_GROUP) for j in range(2)]
        rows = {r: slice(r * GRID_W, (r + 1) * GRID_W) for r, _ in units}
        wins = {r: slice(_na_row_start(r) * GRID_W, (_na_row_start(r) + KH) * GRID_W) for r, _ in units}
        s_ctx_all = [_dot(qs[j, r0 * GRID_W:(r0 + NA_ROW_GROUP) * GRID_W, :], kt_ctx) for j in range(2)]
        s_ctx = [s_ctx_all[j][(r - r0) * GRID_W:(r - r0 + 1) * GRID_W] for r, j in units]
        s_win = []
        for r, j in units:
            dr0 = KH - 1 - (r - _na_row_start(r))
            lane0 = (dr0 - dr0 % 2) * GRID_W
            s_win.append(_dot_nt(qs[j, rows[r], :], ks[wins[r], :])
                         + bias_s[j, dr0 % 2, :, lane0:lane0 + KH * GRID_W])
        m = [_row_max(a, b) for a, b in zip(s_win, s_ctx)]
        p_win = [jnp.exp(a - mm).astype(BF16) for a, mm in zip(s_win, m)]
        p_ctx = [jnp.exp(b - mm).astype(BF16) for b, mm in zip(s_ctx, m)]
        o_aug = [_dot(p_win[i], vs[j, wins[r], :]) + _dot_nt(p_ctx[i], vt_ctx[j]) for i, (r, j) in enumerate(units)]
        for i in range(0, len(units), 2):
            o_ref[rows[units[i][0]], :] = _pair_output(o_aug[i:i + 2], first)


NA_BIAS_LANES = 2 * KH * GRID_W


def _na_attention(qkv, cache_kt, cache_vt, qn, kn, rpb):
    nhp = H_C // 2
    row0 = N_PROMPT // DEC_SEQ
    blk = lambda off: pl.BlockSpec((DEC_SEQ, 2 * HD_C), lambda p, b: (row0 + b, off + p))
    cache_spec = pl.BlockSpec((1, 1, PAIR, PAST_LEN), lambda p, b: (b, 0, p, 0))
    rpb_rows = 2 * KH
    bias = jnp.pad(rpb.astype(F32), ((0, 0), (0, rpb_rows - rpb.shape[1]), (0, PAIR - rpb.shape[2])))
    return pl.pallas_call(
        _na_attn_kernel,
        grid=(nhp, DEC_BATCH),
        in_specs=[blk(0), blk(nhp), blk(2 * nhp), cache_spec, cache_spec,
                  pl.BlockSpec((1, PAIR), lambda p, b: (0, 0)),
                  pl.BlockSpec((1, PAIR), lambda p, b: (0, 0)),
                  pl.BlockSpec((2, rpb_rows, PAIR), lambda p, b: (p, 0, 0))],
        out_specs=pl.BlockSpec((DEC_SEQ, 2 * HD_C), lambda p, b: (b, p)),
        out_shape=jax.ShapeDtypeStruct((N_SAMPLE, D_MODEL), F32),
        scratch_shapes=[pltpu.VMEM((2, DEC_SEQ, PAIR), BF16), pltpu.VMEM((DEC_SEQ, PAIR), BF16),
                        pltpu.VMEM((2, DEC_SEQ, PAIR), BF16), pltpu.VMEM((2, 2, GRID_W, NA_BIAS_LANES), F32)],
        compiler_params=_cparams("arbitrary", "arbitrary"),
        name="na_attention",
    )(qkv, qkv, qkv, cache_kt, cache_vt, jnp.tile(qn.reshape(1, HD_C), (1, 2)), jnp.tile(kn.reshape(1, HD_C), (1, 2)),
      bias)


def _seq_layout(prompt):
    return (SEQ, BATCH, 0) if prompt else (DEC_SEQ, DEC_BATCH, N_PROMPT // DEC_SEQ)


def _flip_blocks(m, c):
    r, s = m.shape
    return m.reshape(r // c, c, s // c, c)[:, ::-1, :, ::-1].reshape(r, s)


def _rms_gate(x, gn, gate):
    ms = jnp.mean(x * x, axis=-1, keepdims=True)
    return x * lax.rsqrt(ms + EPS) * gn * _silu(gate)


HG_LEVELS = tuple(CHUNK >> (i + 1) for i in range(CHUNK.bit_length() - 1))
HG_NL = len(HG_LEVELS)
HG_STACK = (HG_NL + 1) * CHUNK
TOT_ROWS = 16
HG_ROWS = (HG_NL + 2) * CHUNK + TOT_ROWS


def _hgrn_consts():
    c = CHUNK
    level_rows = []
    mask = np.zeros((HG_STACK, HG_STACK), np.float32)
    mask[:c, :c] = np.eye(c)
    for li, b in enumerate(HG_LEVELS):
        m = np.zeros((c, c), np.float32)
        blk = np.zeros((c, c), np.float32)
        for t in range(c):
            mid = (t // (2 * b)) * 2 * b + b
            if t >= mid:
                m[t, mid:t + 1] = 1.0
                blk[t, mid - b:mid] = 1.0
            else:
                m[t, t + 1:mid] = 1.0
        level_rows.append(m)
        mask[(li + 1) * c:(li + 2) * c, (li + 1) * c:(li + 2) * c] = blk
    dq = np.tril(np.ones((c, c), np.float32))
    dk = np.triu(np.ones((c, c), np.float32), 1)
    body = np.concatenate(level_rows + [dq, dk], axis=0)
    tot = np.ones((TOT_ROWS, c), np.float32)
    mcs, masks = [], []
    for reverse in (False, True):
        bm = _flip_blocks(body, c) if reverse else body
        mk = _flip_blocks(mask, c) if reverse else mask
        mc = np.concatenate([bm, tot], axis=0)
        mcs.append(np.concatenate([mc, mc], axis=1))
        masks.append(mk)
    return jnp.asarray(np.stack(mcs), BF16), jnp.asarray(np.stack(masks), F32)


HG_FAST = 64
HG_HALF = HG_FAST // 2
HG_FAST_ROWS = 4 * HG_FAST + TOT_ROWS
HG_SAFE_EXP = 40.0


def _hgrn_fast_consts(seq):
    c, m = HG_FAST, HG_HALF
    aq = np.zeros((c, c), np.float32)
    for t in range(c):
        if t >= m:
            aq[t, m:t + 1] = 1.0
        else:
            aq[t, t + 1:m] = -1.0
    dq = np.tril(np.ones((c, c), np.float32))
    dk = np.triu(np.ones((c, c), np.float32), 1)
    body = np.concatenate([aq, -aq, dq, dk], axis=0)
    tot = np.ones((TOT_ROWS, c), np.float32)
    causal = np.tril(np.ones((c, c), np.float32))
    mfs, masks = [], []
    for reverse in (False, True):
        bm = _flip_blocks(body, c) if reverse else body
        mf = np.concatenate([bm, tot], axis=0)
        mfs.append(np.concatenate([mf, mf], axis=1))
        masks.append(causal.T if reverse else causal)
    n_half = seq // m
    half = np.zeros((max(n_half, 16), seq), np.float32)
    for i in range(n_half):
        half[i, i * m:(i + 1) * m] = 1.0
    return jnp.asarray(np.stack(mfs), BF16), jnp.asarray(np.stack(masks), F32), jnp.asarray(half, BF16)


def _hgrn_kernel(*refs, seq, has_s0, emit_state):
    it = iter(refs)
    qa_ref, ff_ref, fb_ref, ia_ref, ga_ref, lb_ref, gn_ref, mc_ref, mask_ref = [next(it) for _ in range(9)]
    mf_ref, causal_ref, half_ref = [next(it) for _ in range(3)]
    s0_ref = next(it) if has_s0 else None
    o_ref = next(it)
    st_ref = next(it) if emit_state else None
    s_scr, acc, f_s, lf_s = [next(it) for _ in range(4)]
    c = CHUNK
    n_chunks = seq // c
    combos = [(d, h) for d in range(2) for h in range(H_A)]
    lanes = [slice(h * DK_A, (h + 1) * DK_A) for h in range(H_A)]
    add = lambda a, b: a + b

    lb_raw = lb_ref[...]
    lb_e = jnp.exp(lb_raw - jnp.max(lb_raw, axis=0, keepdims=True))
    lb_all = lb_e[0:1] / jnp.sum(lb_e, axis=0, keepdims=True)

    for d in range(2):
        for h in range(H_A):
            s_scr[d, h] = s0_ref[0, 0, d, h].T if has_s0 else jnp.zeros((DV_A, DK_A), F32)
    acc[...] = jnp.zeros(acc.shape, F32)

    worst = []
    for d, fr_ref in enumerate((ff_ref, fb_ref)):
        f = lb_all + (1.0 - lb_all) * _sigmoid(fr_ref[...])
        lf = jnp.log(f)
        f_s[d] = f
        lf_s[d] = lf
        worst.append(jnp.max(_dot(half_ref[...], (-lf).astype(BF16))))
    safe = jnp.maximum(worst[0], worst[1]) <= HG_SAFE_EXP

    def fast_body(n, carry):
        cf = HG_FAST
        n_fast = seq // cf
        rows = [pl.ds(pl.multiple_of((n if d == 0 else n_fast - 1 - n) * cf, cf), cf) for d in range(2)]
        e_all = [jnp.exp(_dot_const(mf_ref[d], lf_s[d, rows[d], :])) for d in range(2)]
        q_all = [_silu(qa_ref[rows[d], :]) * DK_A ** -0.5 for d in range(2)]
        k_all = [1.0 - f_s[d, rows[d], :] for d in range(2)]
        v_all = [ia_ref[rows[d], :].astype(BF16) for d in range(2)]
        st = [s_scr[d, h] for d, h in combos]
        qs = [q_all[d][:, lanes[h]] for d, h in combos]
        ks = [k_all[d][:, lanes[h]] for d, h in combos]
        vs = [v_all[d][:, lanes[h]] for d, h in combos]
        es = [[e_all[d][i * cf:(i + 1) * cf, lanes[h]] for i in range(4)] for d, h in combos]
        p = [jnp.where(causal_ref[d] > 0.0,
                       _dot_nt((qs[i] * es[i][0]).astype(BF16), (ks[i] * es[i][1]).astype(BF16)), 0.0).astype(BF16)
             for i, (d, h) in enumerate(combos)]
        o = [_dot(p[i], vs[i]) + _dot_nt((qs[i] * es[i][2]).astype(BF16), st[i].astype(BF16))
             for i in range(len(combos))]
        upd = [_dot_tn(vs[i], (ks[i] * es[i][3]).astype(BF16)) for i in range(len(combos))]
        for d in range(2):
            acc[rows[d], :] += jnp.concatenate(o[d * H_A:(d + 1) * H_A], axis=1)
        for i, (d, h) in enumerate(combos):
            s_scr[d, h] = st[i] * e_all[d][4 * cf:4 * cf + 1, lanes[h]] + upd[i]
        return carry

    def body(n, carry):
        rows = [pl.ds(pl.multiple_of((n if d == 0 else n_chunks - 1 - n) * c, c), c) for d in range(2)]
        f_all = [f_s[d, rows[d], :] for d in range(2)]
        e_all = [jnp.exp(_dot_const(mc_ref[d], lf_s[d, rows[d], :])) for d in range(2)]
        q_all = [_silu(qa_ref[rows[d], :]) * DK_A ** -0.5 for d in range(2)]
        v_all = [ia_ref[rows[d], :].astype(BF16) for d in range(2)]
        st = [s_scr[d, h] for d, h in combos]
        qs, ks, vs, es = [], [], [], []
        for d, h in combos:
            qs.append(q_all[d][:, lanes[h]])
            ks.append(1.0 - f_all[d][:, lanes[h]])
            vs.append(v_all[d][:, lanes[h]])
            es.append(e_all[d][:, lanes[h]])
        lvl = [[e[i * c:(i + 1) * c] for i in range(HG_NL + 2)] for e in es]
        qst = [jnp.concatenate([q] + [q * l[i] for i in range(HG_NL)], axis=0).astype(BF16) for q, l in zip(qs, lvl)]
        kst = [jnp.concatenate([k] + [k * l[i] for i in range(HG_NL)], axis=0).astype(BF16) for k, l in zip(ks, lvl)]
        r = [(_dot_nt(qst[i], kst[i]) * mask_ref[d]).astype(BF16) for i, (d, h) in enumerate(combos)]
        ost = [_dot(r[i], jnp.concatenate([vs[i]] * (HG_NL + 1), axis=0)) for i in range(len(combos))]
        inter = [_dot_nt((qs[i] * lvl[i][HG_NL]).astype(BF16), st[i].astype(BF16)) for i in range(len(combos))]
        upd = [_dot_tn(vs[i], (ks[i] * lvl[i][HG_NL + 1]).astype(BF16)) for i in range(len(combos))]
        o = [functools.reduce(lambda a, b: a + b, [ost[i][j * c:(j + 1) * c] for j in range(HG_NL + 1)]) + inter[i]
             for i in range(len(combos))]
        for d in range(2):
            acc[rows[d], :] += jnp.concatenate(o[d * H_A:(d + 1) * H_A], axis=1)
        for i, (d, h) in enumerate(combos):
            e_tot = es[i][(HG_NL + 2) * c:(HG_NL + 2) * c + 1]
            s_scr[d, h] = st[i] * e_tot + upd[i]
        return carry

    @pl.when(safe)
    def _():
        lax.fori_loop(0, seq // HG_FAST, fast_body, 0)

    @pl.when(jnp.logical_not(safe))
    def _():
        lax.fori_loop(0, n_chunks, body, 0)

    for h in range(H_A):
        ln = slice(h * DV_A, (h + 1) * DV_A)
        o_ref[:, ln] = _rms_gate(acc[:, ln], gn_ref[...], ga_ref[:, ln])
    if emit_state:
        for d in range(2):
            for h in range(H_A):
                st_ref[0, 0, d, h] = s_scr[d, h].T


def _hgrn(proj, hgrn_lb, gn, consts, prompt, s0=None, layer=0):
    seq, nb, rb0 = _seq_layout(prompt)
    consts = list(consts) + list(_hgrn_fast_consts(seq))
    wa = H_A * DK_A
    blk = lambda j: pl.BlockSpec((seq, wa), lambda b: (rb0 + b, j))
    const2 = lambda b: (0, 0)
    st_block = (1, 1, 2, H_A, DK_A, DV_A)
    in_specs = [blk(0), blk(1), blk(2), blk(3), blk(4),
                pl.BlockSpec(hgrn_lb.shape, const2), pl.BlockSpec((1, DV_A), const2)]
    in_specs += [pl.BlockSpec(m.shape, lambda b, nd=m.ndim: (0,) * nd) for m in consts]
    args = [proj] * 5 + [hgrn_lb, gn.reshape(1, DV_A)] + consts
    if s0 is not None:
        in_specs.append(pl.BlockSpec(st_block, lambda b: (b, layer, 0, 0, 0, 0)))
        args.append(s0)
    out_specs = [pl.BlockSpec((seq, wa), lambda b: (b, 0))]
    out_shape = [jax.ShapeDtypeStruct((nb * seq, wa), F32)]
    if prompt:
        out_specs.append(pl.BlockSpec(st_block, lambda b: (b, 0, 0, 0, 0, 0)))
        out_shape.append(jax.ShapeDtypeStruct((nb, 1, 2, H_A, DK_A, DV_A), F32))
    return pl.pallas_call(
        functools.partial(_hgrn_kernel, seq=seq, has_s0=s0 is not None, emit_state=prompt),
        grid=(nb,),
        in_specs=in_specs,
        out_specs=out_specs,
        out_shape=out_shape,
        scratch_shapes=[pltpu.VMEM((2, H_A, DV_A, DK_A), F32), pltpu.VMEM((seq, wa), F32),
                        pltpu.VMEM((2, seq, wa), F32), pltpu.VMEM((2, seq, wa), F32)],
        compiler_params=_cparams("arbitrary"),
        name="hgrn_prompt" if prompt else "hgrn_sample",
    )(*args)


GD_SUB = GBLK // CHUNK
GD_BLOCKS_PER_ITER = 2
GD_ROWS = 2 * GBLK + TOT_ROWS


def _gdn_consts():
    n, c = GBLK, CHUNK
    same = (np.arange(n)[:, None] // c) == (np.arange(n)[None, :] // c)
    tri = (same & (np.arange(n)[None, :] <= np.arange(n)[:, None])).astype(np.float32)
    sup = (same & (np.arange(n)[None, :] > np.arange(n)[:, None])).astype(np.float32)
    tot = np.zeros((TOT_ROWS, n), np.float32)
    for s in range(GD_SUB):
        tot[s, s * c:(s + 1) * c] = 1.0
    mgs, tts, tris = [], [], []
    for reverse in (False, True):
        t = _flip_blocks(tri, c) if reverse else tri
        s = _flip_blocks(sup, c) if reverse else sup
        mg = np.concatenate([t, s, tot], axis=0)
        mgs.append(np.concatenate([mg, mg], axis=1))
        tts.append(np.concatenate([t.T, t.T], axis=0))
        tris.append(t)
    tris.append(same.astype(np.float32))
    return jnp.asarray(np.stack(mgs), BF16), jnp.asarray(np.stack(tts), BF16), jnp.asarray(np.stack(tris), F32)


def _softplus(x):
    return jnp.maximum(x, 0.0) + jnp.log(1.0 + jnp.exp(-jnp.abs(x)))


def _conv_silu(x, w, seq):
    half = SHORT_CONV // 2
    pad = jnp.zeros((8, x.shape[1]), x.dtype)
    xe = jnp.concatenate([pad, x, pad], axis=0)
    acc = xe * w[half:half + 1]
    for j in range(SHORT_CONV):
        shift = half - j
        if shift != 0:
            acc = acc + pltpu.roll(xe, shift % (seq + 16), axis=0) * w[j:j + 1]
    return _silu(acc[8:seq + 8])


def _l2norm_heads(x, n_heads, width, scale):
    outs = []
    for h in range(n_heads):
        xh = x[:, h * width:(h + 1) * width]
        outs.append(xh * (lax.rsqrt(jnp.sum(xh * xh, axis=-1, keepdims=True) + EPS) * scale))
    return jnp.concatenate(outs, axis=-1)


def _gdn_kernel(*refs, seq, has_s0, emit_state):
    it = iter(refs)
    (q_ref, k_ref, v_ref, gb_ref, gate_ref, cw_ref, alog_ref, dt_ref, gn_ref,
     mg_ref, tt_ref, tri_ref) = [next(it) for _ in range(12)]
    s0_ref = next(it) if has_s0 else None
    o_ref = next(it)
    st_ref = next(it) if emit_state else None
    qn, kn, vn, u_s, w_s, qg_s, kdt_s, at_s, et_s, s_scr, acc = [next(it) for _ in range(11)]
    c = CHUNK
    n_chunks = seq // c
    n_blocks = seq // GBLK
    wq = H_B * DK_B
    n_dh = 2 * H_B
    combos = [(d, h) for d in range(2) for h in range(H_B)]
    lanes = [slice(h * DK_B, (h + 1) * DK_B) for h in range(H_B)]

    qn[...] = _l2norm_heads(_conv_silu(q_ref[...], cw_ref[:, 0:wq], seq), H_B, DK_B, DK_B ** -0.5)
    kn[...] = _l2norm_heads(_conv_silu(k_ref[...], cw_ref[:, wq:2 * wq], seq), H_B, DK_B, 1.0)
    vn[...] = _conv_silu(v_ref[...], cw_ref[:, 2 * wq:3 * wq], seq)
    for i in range(2 * H_B):
        s_scr[i] = s0_ref[0, 0, i // H_B, i % H_B] if has_s0 else jnp.zeros((DK_B, DV_B), F32)
    acc[...] = jnp.zeros(acc.shape, F32)

    eye = (lax.broadcasted_iota(jnp.int32, (GBLK, GBLK), 0)
           == lax.broadcasted_iota(jnp.int32, (GBLK, GBLK), 1)).astype(F32)
    eye_pk = (lax.broadcasted_iota(jnp.int32, (c, GBLK), 0)
              == lax.broadcasted_iota(jnp.int32, (c, GBLK), 1) % c).astype(F32)
    bwd_lane = lax.broadcasted_iota(jnp.int32, (1, 128), 1) % n_dh >= H_B
    add = lambda a, b: a + b

    same_chunk = tri_ref[2].astype(BF16)

    def expand(pk):
        return jnp.concatenate([pk] * GD_SUB, axis=0) * same_chunk

    def pack(bd):
        return functools.reduce(add, [bd[s * c:(s + 1) * c] for s in range(GD_SUB)])

    def weights(hi, lo):
        return jnp.concatenate([expand(hi), expand(lo)], axis=1)

    def dot3_split(a_hi, a_lo, w2):
        m, n = a_hi.shape[0], w2.shape[1] // 2
        t = _dot(jnp.concatenate([a_hi, a_lo], axis=0), w2)
        return t[:m, :n] + t[m:, :n] + t[:m, n:]

    def block_body(it, carry):
        blks = [it * GD_BLOCKS_PER_ITER + o for o in range(GD_BLOCKS_PER_ITER)]
        units = [(o, d, h) for o in range(GD_BLOCKS_PER_ITER) for d, h in combos]
        idx = lambda d, h: d * H_B + h
        col = lambda x, j: jnp.broadcast_to(x[:, j:j + 1], (GBLK, DK_B))
        rows = [pl.ds(pl.multiple_of(b * GBLK, GBLK), GBLK) for b in blks]
        gates = [gate_ref[r, :] for r in rows]
        glog_all = [-jnp.exp(alog_ref[...]) * _softplus(x + dt_ref[...]) for x in gates]
        beta_all = [_sigmoid(x) for x in gates]
        g2 = [jnp.concatenate(_split2(x), axis=0) for x in glog_all]
        dg = [[_dot(mg_ref[d], x) for d in range(2)] for x in g2]
        dsel = [jnp.where(bwd_lane, x[1], x[0]) for x in dg]
        eg_all = [jnp.exp(x) for x in dsel]
        gt = [[_dot_tn(x, tt_ref[d]) for d in range(2)] for x in g2]
        qs = [[qn[r, ln] for ln in lanes] for r in rows]
        ks = [[kn[r, ln] for ln in lanes] for r in rows]
        vs = [[vn[r, ln] for ln in lanes] for r in rows]
        betas = [col(beta_all[o], n_dh + idx(d, h)) for o, d, h in units]
        kbs = [ks[o][h] * betas[u] for u, (o, d, h) in enumerate(units)]
        kb_of = {unit: kbs[u] for u, unit in enumerate(units)}
        kk = {(o, h): _dot_nt(jnp.concatenate([qs[o][h], kb_of[o, 0, h], kb_of[o, 1, h]], axis=0).astype(BF16),
                              ks[o][h].astype(BF16))
              for o in range(GD_BLOCKS_PER_ITER) for h in range(H_B)}
        decay = []
        for o, d, h in units:
            inside = tri_ref[d] > 0.0
            gd = col(dsel[o][:GBLK], idx(d, h)) - gt[o][d][idx(d, h):idx(d, h) + 1, :]
            decay.append(jnp.where(inside, jnp.exp(jnp.where(inside, gd, 0.0)), 0.0))
        attn = [kk[o, h][:GBLK] * decay[u] for u, (o, d, h) in enumerate(units)]
        p_pk = [pack(kk[o, h][(1 + d) * GBLK:(2 + d) * GBLK] * decay[u] * (1.0 - eye))
                for u, (o, d, h) in enumerate(units)]
        x_pk = [eye_pk - p for p in p_pk]
        p_sp = [_split2(p) for p in p_pk]
        p_w = [weights(*s) for s in p_sp]
        for _ in range(CHUNK.bit_length() - 2):
            p_pk = [dot3_split(*s, w) for s, w in zip(p_sp, p_w)]
            p_sp = [_split2(p) for p in p_pk]
            p_w = [weights(*s) for s in p_sp]
            x_pk = [x + dot3_split(*_split2(x), w) for x, w in zip(x_pk, p_w)]
        eg_col = [col(eg_all[o][:GBLK], idx(d, h)) for o, d, h in units]
        ekd_col = [col(eg_all[o][GBLK:2 * GBLK], idx(d, h)) for o, d, h in units]
        rhs = [_split2(jnp.concatenate([vs[o][h] * betas[u], kbs[u] * eg_col[u]], axis=1))
               for u, (o, d, h) in enumerate(units)]
        t_sp = [[expand(part) for part in _split2(x)] for x in x_pk]
        uw = [_dot(t[0], r[0]) + (_dot(t[0], r[1]) + _dot(t[1], r[0])) for t, r in zip(t_sp, rhs)]
        for u, (o, d, h) in enumerate(units):
            i = idx(d, h)
            qg = (qs[o][h] * eg_col[u]).astype(BF16)
            kdt_s[i, blks[o]] = (ks[o][h] * ekd_col[u]).T.astype(BF16)
            for s in range(GD_SUB):
                cn = blks[o] * GD_SUB + s
                r = slice(s * c, (s + 1) * c)
                u_s[i, cn] = uw[u][r, :DV_B]
                w_s[i, cn] = uw[u][r, DV_B:].astype(BF16)
                qg_s[i, cn] = qg[r]
                at_s[i, cn] = attn[u][r].astype(BF16)
                et_s[i, cn] = jnp.broadcast_to(eg_all[o][2 * GBLK + s:2 * GBLK + s + 1, i:i + 1], (8, DV_B))
        return carry

    lax.fori_loop(0, n_blocks // GD_BLOCKS_PER_ITER, block_body, 0)

    def chunk_body(n, carry):
        cns = [n, n_chunks - 1 - n]
        rows = [pl.ds(pl.multiple_of(cn * c, c), c) for cn in cns]
        sub_of_row = lax.broadcasted_iota(jnp.int32, (GBLK, 1), 0) // c
        in_chunk = [sub_of_row == cn % GD_SUB for cn in cns]
        st = [s_scr[i] for i in range(n_dh)]
        ws = [_dot(jnp.concatenate([w_s[i, cns[d]], qg_s[i, cns[d]]], axis=0), st[i].astype(BF16))
              for i, (d, h) in enumerate(combos)]
        vblk = [jnp.where(in_chunk[d], jnp.concatenate([u_s[i, cns[d]] - ws[i][:c]] * GD_SUB, axis=0), 0.0).astype(BF16)
                for i, (d, h) in enumerate(combos)]
        r = [_dot(jnp.concatenate([at_s[i, cns[d]], kdt_s[i, cns[d] // GD_SUB]], axis=0), vblk[i])
             for i, (d, h) in enumerate(combos)]
        for d in range(2):
            acc[rows[d], :] += jnp.concatenate([ws[i][c:] + r[i][:c] for i in range(d * H_B, (d + 1) * H_B)], axis=1)
        for i, (d, h) in enumerate(combos):
            s_scr[i] = st[i] * et_s[i, cns[d]][0:1] + r[i][c:]
        return carry

    lax.fori_loop(0, n_chunks, chunk_body, 0)

    for h in range(H_B):
        ln = slice(h * DV_B, (h + 1) * DV_B)
        o_ref[:, ln] = _rms_gate(acc[:, ln], gn_ref[...], gb_ref[:, ln])
    if emit_state:
        for i in range(2 * H_B):
            st_ref[0, 0, i // H_B, i % H_B] = s_scr[i]


def _gdn(proj, gates, conv_w, a_log, dt_bias, gn, consts, prompt, s0=None, layer=0):
    seq, nb, rb0 = _seq_layout(prompt)
    n_chunks = seq // CHUNK
    wq = H_B * DK_B
    blk = lambda j: pl.BlockSpec((seq, wq), lambda b: (rb0 + b, j))
    const2 = lambda b: (0, 0)
    const3 = lambda b: (0, 0, 0)
    st_block = (1, 1, 2, H_B, DK_B, DV_B)
    pad_row = lambda p: jnp.pad(p.reshape(1, -1).astype(F32), ((0, 0), (0, 128 - p.size)))
    in_specs = [blk(5), blk(6), blk(7), blk(8),
                pl.BlockSpec((seq, 128), lambda b: (rb0 + b, 0)),
                pl.BlockSpec((SHORT_CONV, 3 * wq), const2),
                pl.BlockSpec((1, 128), const2), pl.BlockSpec((1, 128), const2), pl.BlockSpec((1, DV_B), const2)]
    in_specs += [pl.BlockSpec(m.shape, const3) for m in consts]
    args = [proj] * 4 + [gates, conv_w.reshape(SHORT_CONV, 3 * wq), pad_row(a_log), pad_row(dt_bias),
                         gn.reshape(1, DV_B)] + list(consts)
    if s0 is not None:
        in_specs.append(pl.BlockSpec(st_block, lambda b: (b, layer, 0, 0, 0, 0)))
        args.append(s0)
    out_specs = [pl.BlockSpec((seq, wq), lambda b: (b, 0))]
    out_shape = [jax.ShapeDtypeStruct((nb * seq, wq), F32)]
    if prompt:
        out_specs.append(pl.BlockSpec(st_block, lambda b: (b, 0, 0, 0, 0, 0)))
        out_shape.append(jax.ShapeDtypeStruct((nb, 1, 2, H_B, DK_B, DV_B), F32))
    n_dh = 2 * H_B
    scratch = ([pltpu.VMEM((seq, wq), F32)] * 3
               + [pltpu.VMEM((n_dh, n_chunks, CHUNK, DV_B), F32)]
               + [pltpu.VMEM((n_dh, n_chunks, CHUNK, DK_B), BF16)] * 2
               + [pltpu.VMEM((n_dh, seq // GBLK, DK_B, GBLK), BF16),
                  pltpu.VMEM((n_dh, n_chunks, CHUNK, GBLK), BF16),
                  pltpu.VMEM((n_dh, n_chunks, 8, DV_B), F32),
                  pltpu.VMEM((n_dh, DK_B, DV_B), F32),
                  pltpu.VMEM((seq, wq), F32)])
    return pl.pallas_call(
        functools.partial(_gdn_kernel, seq=seq, has_s0=s0 is not None, emit_state=prompt),
        grid=(nb,),
        in_specs=in_specs,
        out_specs=out_specs,
        out_shape=out_shape,
        scratch_shapes=scratch,
        compiler_params=_cparams("arbitrary"),
        name="gdn_prompt" if prompt else "gdn_sample",
    )(*args)


def kernel(x_prompt, x_sample, state_hgrn, state_gdn, cache_na_k, cache_na_v, c, c_ctx, ada_w, ada_b, norm_g, w_in_ab, w_out_ab, hgrn_lb, gdn_conv, gdn_a_log, gdn_dt_bias, gn_hgrn, gn_gdn, w_qkv_na, qn_na, kn_na, rpb_na, w_out_na, w_mlp1, w_mlp2):
    cond = jnp.concatenate([c_ctx[None, :], c, jnp.zeros((N_MOD_ROWS - 1 - DEC_BATCH, D_MODEL), F32)], axis=0)
    mods = _modulation(cond, ada_w, ada_b)
    xs = (x_prompt.reshape(N_PROMPT, D_MODEL), x_sample.reshape(N_SAMPLE, D_MODEL))

    w_in = w_in_ab[0].astype(BF16)
    w_gate = jnp.pad(w_in[:, D_MAIN_AB:], ((0, 0), (0, 128 - N_GATE_AB)))
    proj, gates = _norm_proj(xs, mods[0], norm_g[0, 0], [w_in, w_gate], widths=[D_MAIN_AB, 128])
    w_mlp1_bf, w_mlp2_bf = w_mlp1.astype(BF16), w_mlp2.astype(BF16)
    hg_consts = _hgrn_consts()
    gd_consts = _gdn_consts()
    hg_prompt, new_hgrn = _hgrn(proj, hgrn_lb, gn_hgrn[0], hg_consts, True)
    hg_sample, = _hgrn(proj, hgrn_lb, gn_hgrn[0], hg_consts, False, s0=state_hgrn)
    gd_args = (gdn_conv[0], gdn_a_log[0], gdn_dt_bias[0], gn_gdn[0], gd_consts)
    gd_prompt, new_gdn = _gdn(proj, gates, *gd_args, True)
    gd_sample, = _gdn(proj, gates, *gd_args, False, s0=state_gdn)
    xs = _post_mixer(xs, [(hg_prompt, hg_sample), (gd_prompt, gd_sample)], mods[0], norm_g[0, 1],
                     w_out_ab[0].astype(BF16), w_mlp1_bf, w_mlp2_bf, 0)

    qkv, = _norm_proj(xs, mods[1], norm_g[1, 0], [w_qkv_na[0].astype(BF16)])
    at_prompt, new_kt, new_vt = _ctx_attention(qkv, qn_na[0], kn_na[0])
    time_minor = lambda a: jnp.swapaxes(a, -1, -2).reshape(a.shape[0], 1, H_C * HD_C, a.shape[3])
    at_sample = _na_attention(qkv, time_minor(cache_na_k), time_minor(cache_na_v), qn_na[0], kn_na[0], rpb_na[0])
    time_major = lambda a: jnp.swapaxes(a.reshape(BATCH, 1, H_C, HD_C, SEQ), -1, -2)
    new_k, new_v = time_major(new_kt), time_major(new_vt)
    y_prompt, y_sample = _post_mixer(xs, [(at_prompt, at_sample)], mods[1], norm_g[1, 1], w_out_na[0].astype(BF16),
                                     w_mlp1_bf, w_mlp2_bf, 1, split_out=True)

    return (y_prompt.reshape(BATCH, SEQ, D_MODEL), y_sample.reshape(DEC_BATCH, DEC_SEQ, D_MODEL),
            new_hgrn, new_gdn, new_k, new_v)
```

```python
import functools

import numpy as np
import jax
import jax.numpy as jnp
from jax import lax
from jax.experimental import pallas as pl
from jax.experimental.pallas import tpu as pltpu

F32 = jnp.float32
BF16 = jnp.bfloat16

D_MODEL = 1024
BATCH = 16
SEQ = 256
DEC_BATCH = 4
DEC_SEQ = 1024
PAST_LEN = 256
N_PROMPT = BATCH * SEQ
N_SAMPLE = DEC_BATCH * DEC_SEQ
N_TOK = N_PROMPT + N_SAMPLE
GRID_W = 64
GRID_ROWS = DEC_SEQ // GRID_W
H_A = 4
DK_A = 128
DV_A = 128
H_B = 4
DK_B = 128
DV_B = 128
SHORT_CONV = 5
H_C = 16
HD_C = 64
KH = 8
KW = 16
D_FF = 4 * D_MODEL
EPS = 1e-6
NEG_INF = -1e30
N_MOD_ROWS = 8
D_MAIN_AB = 4608
N_GATE_AB = 16
CHUNK = 32
GBLK = 128
VMEM_LIMIT = 56 * 1024 * 1024


def _cparams(*sem):
    return pltpu.CompilerParams(dimension_semantics=sem, vmem_limit_bytes=VMEM_LIMIT)


def _sigmoid(x):
    return 0.5 * jnp.tanh(0.5 * x) + 0.5


def _silu(x):
    return x * _sigmoid(x)


def _dot(a, b):
    return jnp.dot(a, b, preferred_element_type=F32)


def _dot_nt(a, b):
    return lax.dot_general(a, b, (((1,), (1,)), ((), ())), preferred_element_type=F32)


def _dot_tn(a, b):
    return lax.dot_general(a, b, (((0,), (0,)), ((), ())), preferred_element_type=F32)


def _split2(x):
    hi = x.astype(BF16)
    lo = (x - hi.astype(F32)).astype(BF16)
    return hi, lo


def _dot_const(m2, x):
    hi, lo = _split2(x)
    return _dot(m2, jnp.concatenate([hi, lo], axis=0))


def _dot3(a, b):
    ah, al = _split2(a)
    bh, bl = _split2(b)
    return _dot(ah, bh) + (_dot(ah, bl) + _dot(al, bh))


def _mod_row(i, tm):
    start = i * tm
    return jnp.where(start < N_PROMPT, 0, 1 + (start - N_PROMPT) // DEC_SEQ)


def _mod_slice(mod_ref, row, k):
    return mod_ref[pl.ds(row, 1), k * D_MODEL:(k + 1) * D_MODEL]


def _norm_mod(x, g, sc, sh):
    ms = jnp.mean(x * x, axis=-1, keepdims=True)
    return (x * lax.rsqrt(ms + EPS) * g) * (1.0 + sc) + sh


def _mod_kernel(cond_ref, w_ref, b_ref, side_ref, o_ref, side_o_ref):
    s = _silu(cond_ref[...]).astype(BF16)
    o_ref[0] = _dot(s, w_ref[0].astype(BF16)) + b_ref[0]
    side_o_ref[...] = side_ref[...].astype(BF16)


MOD_TN = 768


def _modulation(cond8, ada_w, ada_b, side):
    depth = ada_w.shape[0]
    nj = ada_w.shape[2] // MOD_TN
    slab = side.shape[0] // (depth * nj)
    side_spec = pl.BlockSpec((slab, side.shape[1]), lambda l, j: (l * nj + j, 0))
    return pl.pallas_call(
        _mod_kernel,
        grid=(depth, nj),
        in_specs=[
            pl.BlockSpec((N_MOD_ROWS, D_MODEL), lambda l, j: (0, 0)),
            pl.BlockSpec((1, D_MODEL, MOD_TN), lambda l, j: (l, 0, j)),
            pl.BlockSpec((1, 1, MOD_TN), lambda l, j: (l, 0, j)),
            side_spec,
        ],
        out_specs=[pl.BlockSpec((1, N_MOD_ROWS, MOD_TN), lambda l, j: (l, 0, j)), side_spec],
        out_shape=[jax.ShapeDtypeStruct((depth, N_MOD_ROWS, ada_w.shape[2]), F32),
                   jax.ShapeDtypeStruct(side.shape, BF16)],
        compiler_params=_cparams("arbitrary", "arbitrary"),
        name="modulation",
    )(cond8, ada_w, ada_b.reshape(depth, 1, -1), side)


def _stream_specs(n_arrays, tm, width=D_MODEL):
    if n_arrays == 1:
        return [pl.BlockSpec((tm, width), lambda i: (i, 0))]
    npt = N_PROMPT // tm
    return [pl.BlockSpec((tm, width), lambda i: (jnp.minimum(i, npt - 1), 0)),
            pl.BlockSpec((tm, width), lambda i: (jnp.maximum(i - npt, 0), 0))]


def _stream_load(x_refs, tm):
    if len(x_refs) == 1:
        return x_refs[0][...]
    return jnp.where(pl.program_id(0) < N_PROMPT // tm, x_refs[0][...], x_refs[1][...])


def _side_specs(side, n_steps):
    specs = [pl.BlockSpec((w.shape[0] // n_steps, w.shape[1]), lambda i: (i, 0)) for w in side]
    shapes = [jax.ShapeDtypeStruct(w.shape, BF16) for w in side]
    return specs, shapes


def _side_cast(in_refs, out_refs):
    for i_ref, o_ref in zip(in_refs, out_refs):
        o_ref[...] = i_ref[...].astype(BF16)


def _norm_proj_kernel(*refs, tm, n_x, n_w, n_side):
    x_refs, (mod_ref, g_ref) = refs[:n_x], refs[n_x:n_x + 2]
    w_refs = refs[n_x + 2:n_x + 2 + n_w]
    side_in = refs[n_x + 2 + n_w:n_x + 2 + n_w + n_side]
    o_refs = refs[n_x + 2 + n_w + n_side:n_x + 2 + 2 * n_w + n_side]
    side_out = refs[n_x + 2 + 2 * n_w + n_side:]
    row = _mod_row(pl.program_id(0), tm)
    h = _norm_mod(_stream_load(x_refs, tm), g_ref[...], _mod_slice(mod_ref, row, 1), _mod_slice(mod_ref, row, 0)).astype(BF16)
    for w_ref, o_ref in zip(w_refs, o_refs):
        o_ref[...] = _dot(h, w_ref[...])
    _side_cast(side_in, side_out)


def _norm_proj(xs, mod, g, ws, widths=None, side=(), tm=512):
    n_w = len(ws)
    widths = widths or [w.shape[1] for w in ws]
    const = lambda i: (0, 0)
    side_specs, side_shapes = _side_specs(side, N_TOK // tm)
    return pl.pallas_call(
        functools.partial(_norm_proj_kernel, tm=tm, n_x=len(xs), n_w=n_w, n_side=len(side)),
        grid=(N_TOK // tm,),
        in_specs=_stream_specs(len(xs), tm) + [
            pl.BlockSpec(mod.shape, const),
            pl.BlockSpec((1, D_MODEL), const),
        ] + [pl.BlockSpec((D_MODEL, n), const, pipeline_mode=pl.Buffered(1)) for n in widths] + side_specs,
        out_specs=[pl.BlockSpec((tm, n), lambda i: (i, 0)) for n in widths] + side_specs,
        out_shape=[jax.ShapeDtypeStruct((N_TOK, n), F32) for n in widths] + side_shapes,
        compiler_params=_cparams("arbitrary"),
        name="norm_proj",
    )(*xs, mod, g.reshape(1, D_MODEL), *ws, *side)


def _post_kernel(*refs, tm, ff_chunk, n_x, n_y, groups, n_side):
    x_refs = refs[:n_x]
    n_m = sum(n for n, _ in groups)
    m_refs = refs[n_x:n_x + n_m]
    mod_ref, g_ref, wo_ref, w1_ref, w2_ref = refs[n_x + n_m:n_x + n_m + 5]
    side_in = refs[n_x + n_m + 5:n_x + n_m + 5 + n_side]
    y_refs = refs[n_x + n_m + 5 + n_side:n_x + n_m + 5 + n_side + n_y]
    _side_cast(side_in, refs[n_x + n_m + 5 + n_side + n_y:])
    row = _mod_row(pl.program_id(0), tm)
    mix, first_ref, first_col = None, 0, 0
    for n, width in groups:
        part = _stream_load(m_refs[first_ref:first_ref + n], tm).astype(BF16)
        term = _dot(part, wo_ref[first_col:first_col + width, :])
        mix = term if mix is None else mix + term
        first_ref, first_col = first_ref + n, first_col + width
    x1 = _stream_load(x_refs, tm) + _mod_slice(mod_ref, row, 2) * mix
    h = _norm_mod(x1, g_ref[...], _mod_slice(mod_ref, row, 4), _mod_slice(mod_ref, row, 3)).astype(BF16)
    acc = jnp.zeros((tm, D_MODEL), F32)
    for k in range(0, D_FF, ff_chunk):
        a = jnp.maximum(_dot(h, w1_ref[:, k:k + ff_chunk]), 0.0)
        acc = acc + _dot((a * a).astype(BF16), w2_ref[k:k + ff_chunk, :])
    y = x1 + _mod_slice(mod_ref, row, 5) * acc
    if n_y == 1:
        y_refs[0][...] = y
    else:
        is_prompt = pl.program_id(0) < N_PROMPT // tm

        @pl.when(is_prompt)
        def _():
            y_refs[0][...] = y

        @pl.when(jnp.logical_not(is_prompt))
        def _():
            y_refs[1][...] = y


def _post_mixer(xs, mixed, mod, g, wo, w1, w2, split_out=False, side=(), tm=512, ff_chunk=1024):
    const = lambda i: (0, 0)
    resident = lambda w: pl.BlockSpec(w.shape, const, pipeline_mode=pl.Buffered(1))
    n_y = 2 if split_out else 1
    rows = (N_PROMPT, N_SAMPLE) if split_out else (N_TOK,)
    groups = tuple((len(grp), grp[0].shape[1]) for grp in mixed)
    mixed_specs = [spec for n, width in groups for spec in _stream_specs(n, tm, width)]
    side_specs, side_shapes = _side_specs(side, N_TOK // tm)
    out = pl.pallas_call(
        functools.partial(_post_kernel, tm=tm, ff_chunk=ff_chunk, n_x=len(xs), n_y=n_y, groups=groups,
                          n_side=len(side)),
        grid=(N_TOK // tm,),
        in_specs=_stream_specs(len(xs), tm) + mixed_specs + [
            pl.BlockSpec(mod.shape, const),
            pl.BlockSpec((1, D_MODEL), const),
            resident(wo), resident(w1), resident(w2),
        ] + side_specs,
        out_specs=_stream_specs(n_y, tm) + side_specs,
        out_shape=[jax.ShapeDtypeStruct((r, D_MODEL), F32) for r in rows] + side_shapes,
        compiler_params=_cparams("arbitrary"),
        name="post_mixer",
    )(*xs, *[a for grp in mixed for a in grp], mod, g.reshape(1, D_MODEL), wo, w1, w2, *side)
    return tuple(out[:n_y]), tuple(out[n_y:])


PAIR = 2 * HD_C


def _pair_consts():
    lane = lax.broadcasted_iota(jnp.int32, (1, PAIR), 1)
    first = lane < HD_C
    ones_col = [jnp.where(lane == HD_C, 1.0, 0.0), jnp.where(lane == 0, 1.0, 0.0)]
    r = lax.broadcasted_iota(jnp.int32, (2 * PAIR, PAIR), 0) % PAIR
    cidx = lax.broadcasted_iota(jnp.int32, (2 * PAIR, PAIR), 1)
    mean2 = jnp.where(r // HD_C == cidx // HD_C, 1.0 / HD_C, 0.0).astype(BF16)
    return first, ones_col, mean2


def _pair_norm(x, w2, mean2):
    hi, lo = _split2(x * x)
    ms = _dot(jnp.concatenate([hi, lo], axis=1), mean2)
    return x * lax.rsqrt(ms + EPS) * w2


def _pair_queries(q, first):
    return [jnp.where(first, q, 0.0).astype(BF16), jnp.where(first, 0.0, q).astype(BF16)]


def _pair_values(v, first, ones_col):
    return [jnp.where(first, v, ones_col[0]).astype(BF16), jnp.where(first, ones_col[1], v).astype(BF16)]


def _pair_output(o_aug, first):
    den = [o_aug[0][:, HD_C:HD_C + 1], o_aug[1][:, 0:1]]
    return jnp.where(first, o_aug[0] / den[0], o_aug[1] / den[1])


def _row_max(*pieces):
    tiles = [p[:, i:i + 128] for p in pieces for i in range(0, p.shape[1], 128)]
    return jnp.max(functools.reduce(jnp.maximum, tiles), axis=-1, keepdims=True)


CTX_PAIRS = 4


def _ctx_attn_kernel(q_ref, k_ref, v_ref, qn_ref, kn_ref, o_ref, kc_ref, vc_ref):
    first, ones_col, mean2 = _pair_consts()
    lanes = [slice(p * PAIR, (p + 1) * PAIR) for p in range(CTX_PAIRS)]
    qn = [_pair_norm(q_ref[:, ln], qn_ref[...], mean2) * HD_C ** -0.5 for ln in lanes]
    kn = [_pair_norm(k_ref[:, ln], kn_ref[...], mean2) for ln in lanes]
    v = [v_ref[:, ln] for ln in lanes]
    kt = [x.T for x in kn]
    for p in range(CTX_PAIRS):
        kc_ref[0, 0, lanes[p], :] = kt[p]
        vc_ref[0, 0, lanes[p], :] = v[p].T
    q = [_pair_queries(x, first) for x in qn]
    va = [_pair_values(x, first, ones_col) for x in v]
    s = [[_dot(q[p][j], kt[p].astype(BF16)) for j in range(2)] for p in range(CTX_PAIRS)]
    pr = [[jnp.exp(x - _row_max(x)).astype(BF16) for x in sp] for sp in s]
    for p in range(CTX_PAIRS):
        o_ref[:, lanes[p]] = _pair_output([_dot(pr[p][j], va[p][j]) for j in range(2)], first)


def _ctx_attention(qkv, qn, kn):
    heads = 2 * CTX_PAIRS
    ng = H_C // heads
    wide = CTX_PAIRS * PAIR
    blk = lambda off: pl.BlockSpec((SEQ, wide), lambda b, p: (b, off + p))
    cache_spec = pl.BlockSpec((1, 1, wide, SEQ), lambda b, p: (b, 0, p, 0))
    cache_shape = jax.ShapeDtypeStruct((BATCH, 1, H_C * HD_C, SEQ), F32)
    return pl.pallas_call(
        _ctx_attn_kernel,
        grid=(BATCH, ng),
        in_specs=[blk(0), blk(ng), blk(2 * ng),
                  pl.BlockSpec((1, PAIR), lambda b, p: (0, 0)),
                  pl.BlockSpec((1, PAIR), lambda b, p: (0, 0))],
        out_specs=[pl.BlockSpec((SEQ, wide), lambda b, p: (b, p)), cache_spec, cache_spec],
        out_shape=[jax.ShapeDtypeStruct((N_PROMPT, D_MODEL), F32), cache_shape, cache_shape],
        compiler_params=_cparams("arbitrary", "arbitrary"),
        name="ctx_attention",
    )(qkv, qkv, qkv, jnp.tile(qn.reshape(1, HD_C), (1, 2)), jnp.tile(kn.reshape(1, HD_C), (1, 2)))


def _na_row_start(r):
    return min(max(r - KH // 2, 0), GRID_ROWS - KH)


NA_ROW_GROUP = 4


def _na_attn_kernel(q_ref, k_ref, v_ref, kc_ref, vc_ref, qn_ref, kn_ref, bias_ref, o_ref, qs, ks, vs, bias_s):
    first, ones_col, mean2 = _pair_consts()

    @pl.when(pl.program_id(1) == 0)
    def _():
        q_col = lax.broadcasted_iota(jnp.int32, (GRID_W, PAIR), 0)
        lane = lax.broadcasted_iota(jnp.int32, (GRID_W, PAIR), 1)
        k_col = lane % GRID_W
        w0 = jnp.clip(q_col - KW // 2, 0, GRID_W - KW)
        outside = jnp.where((k_col >= w0) & (k_col < w0 + KW), 0.0, NEG_INF)
        n_dr = 2 * KH - 1
        for j in range(2):
            band = []
            for dr in range(n_dr):
                row = jnp.broadcast_to(bias_ref[j, dr:dr + 1, :], (GRID_W, PAIR))
                band.append([pltpu.roll(row, (half * GRID_W - (KW - 1)) % PAIR, axis=1, stride=1, stride_axis=0)
                             for half in range(2)])
            zero = jnp.zeros((GRID_W, PAIR), F32)
            for cp in range(2):
                for t in range(KH):
                    lo, hi = 2 * t + cp, 2 * t + cp + 1
                    tile = jnp.where(lane < GRID_W, band[lo][0] if lo < n_dr else zero,
                                     band[hi][1] if hi < n_dr else zero)
                    bias_s[j, cp, :, t * PAIR:(t + 1) * PAIR] = tile + outside

    q2 = _pair_queries(_pair_norm(q_ref[...], qn_ref[...], mean2) * HD_C ** -0.5, first)
    v2 = _pair_values(v_ref[...], first, ones_col)
    ks[...] = _pair_norm(k_ref[...], kn_ref[...], mean2).astype(BF16)
    for j in range(2):
        qs[j] = q2[j]
        vs[j] = v2[j]
    kt_ctx = kc_ref[0, 0].astype(BF16)
    vt = vc_ref[0, 0]
    ch = lax.broadcasted_iota(jnp.int32, vt.shape, 0)
    vt_ctx = [jnp.where(ch < HD_C, vt, jnp.where(ch == HD_C, 1.0, 0.0)).astype(BF16),
              jnp.where(ch < HD_C, jnp.where(ch == 0, 1.0, 0.0), vt).astype(BF16)]
    for r0 in range(0, GRID_ROWS, NA_ROW_GROUP):
        units = [(r, j) for r in range(r0, r0 + NA_ROW_GROUP) for j in range(2)]
        rows = {r: slice(r * GRID_W, (r + 1) * GRID_W) for r, _ in units}
        wins = {r: slice(_na_row_start(r) * GRID_W, (_na_row_start(r) + KH) * GRID_W) for r, _ in units}
        s_ctx_all = [_dot(qs[j, r0 * GRID_W:(r0 + NA_ROW_GROUP) * GRID_W, :], kt_ctx) for j in range(2)]
        s_ctx = [s_ctx_all[j][(r - r0) * GRID_W:(r - r0 + 1) * GRID_W] for r, j in units]
        s_win = []
        for r, j in units:
            dr0 = KH - 1 - (r - _na_row_start(r))
            lane0 = (dr0 - dr0 % 2) * GRID_W
            s_win.append(_dot_nt(qs[j, rows[r], :], ks[wins[r], :])
                         + bias_s[j, dr0 % 2, :, lane0:lane0 + KH * GRID_W])
        m = [_row_max(a, b) for a, b in zip(s_win, s_ctx)]
        p_win = [jnp.exp(a - mm).astype(BF16) for a, mm in zip(s_win, m)]
        p_ctx = [jnp.exp(b - mm).astype(BF16) for b, mm in zip(s_ctx, m)]
        o_aug = [_dot(p_win[i], vs[j, wins[r], :]) + _dot_nt(p_ctx[i], vt_ctx[j]) for i, (r, j) in enumerate(units)]
        for i in range(0, len(units), 2):
            o_ref[rows[units[i][0]], :] = _pair_output(o_aug[i:i + 2], first)


NA_BIAS_LANES = 2 * KH * GRID_W


def _na_attention(qkv, cache_kt, cache_vt, qn, kn, rpb):
    nhp = H_C // 2
    row0 = N_PROMPT // DEC_SEQ
    blk = lambda off: pl.BlockSpec((DEC_SEQ, 2 * HD_C), lambda p, b: (row0 + b, off + p))
    cache_spec = pl.BlockSpec((1, 1, PAIR, PAST_LEN), lambda p, b: (b, 0, p, 0))
    rpb_rows = 2 * KH
    bias = jnp.pad(rpb.astype(F32), ((0, 0), (0, rpb_rows - rpb.shape[1]), (0, PAIR - rpb.shape[2])))
    return pl.pallas_call(
        _na_attn_kernel,
        grid=(nhp, DEC_BATCH),
        in_specs=[blk(0), blk(nhp), blk(2 * nhp), cache_spec, cache_spec,
                  pl.BlockSpec((1, PAIR), lambda p, b: (0, 0)),
                  pl.BlockSpec((1, PAIR), lambda p, b: (0, 0)),
                  pl.BlockSpec((2, rpb_rows, PAIR), lambda p, b: (p, 0, 0))],
        out_specs=pl.BlockSpec((DEC_SEQ, 2 * HD_C), lambda p, b: (b, p)),
        out_shape=jax.ShapeDtypeStruct((N_SAMPLE, D_MODEL), F32),
        scratch_shapes=[pltpu.VMEM((2, DEC_SEQ, PAIR), BF16), pltpu.VMEM((DEC_SEQ, PAIR), BF16),
                        pltpu.VMEM((2, DEC_SEQ, PAIR), BF16), pltpu.VMEM((2, 2, GRID_W, NA_BIAS_LANES), F32)],
        compiler_params=_cparams("arbitrary", "arbitrary"),
        name="na_attention",
    )(qkv, qkv, qkv, cache_kt, cache_vt, jnp.tile(qn.reshape(1, HD_C), (1, 2)), jnp.tile(kn.reshape(1, HD_C), (1, 2)),
      bias)


def _seq_layout(prompt):
    return (SEQ, BATCH, 0) if prompt else (DEC_SEQ, DEC_BATCH, N_PROMPT // DEC_SEQ)


def _flip_blocks(m, c):
    r, s = m.shape
    return m.reshape(r // c, c, s // c, c)[:, ::-1, :, ::-1].reshape(r, s)


def _rms_gate(x, gn, gate):
    ms = jnp.mean(x * x, axis=-1, keepdims=True)
    return x * lax.rsqrt(ms + EPS) * gn * _silu(gate)


HG_LEVELS = tuple(CHUNK >> (i + 1) for i in range(CHUNK.bit_length() - 1))
HG_NL = len(HG_LEVELS)
HG_STACK = (HG_NL + 1) * CHUNK
TOT_ROWS = 16
HG_ROWS = (HG_NL + 2) * CHUNK + TOT_ROWS


def _hgrn_consts():
    c = CHUNK
    level_rows = []
    mask = np.zeros((HG_STACK, HG_STACK), np.float32)
    mask[:c, :c] = np.eye(c)
    for li, b in enumerate(HG_LEVELS):
        m = np.zeros((c, c), np.float32)
        blk = np.zeros((c, c), np.float32)
        for t in range(c):
            mid = (t // (2 * b)) * 2 * b + b
            if t >= mid:
                m[t, mid:t + 1] = 1.0
                blk[t, mid - b:mid] = 1.0
            else:
                m[t, t + 1:mid] = 1.0
        level_rows.append(m)
        mask[(li + 1) * c:(li + 2) * c, (li + 1) * c:(li + 2) * c] = blk
    dq = np.tril(np.ones((c, c), np.float32))
    dk = np.triu(np.ones((c, c), np.float32), 1)
    body = np.concatenate(level_rows + [dq, dk], axis=0)
    tot = np.ones((TOT_ROWS, c), np.float32)
    mcs, masks = [], []
    for reverse in (False, True):
        bm = _flip_blocks(body, c) if reverse else body
        mk = _flip_blocks(mask, c) if reverse else mask
        mc = np.concatenate([bm, tot], axis=0)
        mcs.append(np.concatenate([mc, mc], axis=1))
        masks.append(mk)
    return jnp.asarray(np.stack(mcs), BF16), jnp.asarray(np.stack(masks), F32)


HG_FAST = 64
HG_HALF = HG_FAST // 2
HG_FAST_ROWS = 4 * HG_FAST + TOT_ROWS
HG_SAFE_EXP = 40.0
HG_FAST_STEPS = 4


def _hgrn_fast_consts(seq):
    c, m = HG_FAST, HG_HALF
    aq = np.zeros((c, c), np.float32)
    for t in range(c):
        if t >= m:
            aq[t, m:t + 1] = 1.0
        else:
            aq[t, t + 1:m] = -1.0
    dq = np.tril(np.ones((c, c), np.float32))
    dk = np.triu(np.ones((c, c), np.float32), 1)
    body = np.concatenate([aq, -aq, dq, dk], axis=0)
    tot = np.ones((TOT_ROWS, c), np.float32)
    causal = np.tril(np.ones((c, c), np.float32))
    mfs, masks = [], []
    for reverse in (False, True):
        bm = _flip_blocks(body, c) if reverse else body
        mf = np.concatenate([bm, tot], axis=0)
        mfs.append(np.concatenate([mf, mf], axis=1))
        masks.append(causal.T if reverse else causal)
    n_half = seq // m
    half = np.zeros((max(n_half, 16), seq), np.float32)
    for i in range(n_half):
        half[i, i * m:(i + 1) * m] = 1.0
    return jnp.asarray(np.stack(mfs), BF16), jnp.asarray(np.stack(masks), F32), jnp.asarray(half, BF16)


def _hgrn_kernel(*refs, seq, has_s0, emit_state):
    it = iter(refs)
    qa_ref, ff_ref, fb_ref, ia_ref, ga_ref, lb_ref, gn_ref, mc_ref, mask_ref = [next(it) for _ in range(9)]
    mf_ref, causal_ref, half_ref = [next(it) for _ in range(3)]
    s0_ref = next(it) if has_s0 else None
    o_ref = next(it)
    st_ref = next(it) if emit_state else None
    s_scr, acc, f_s, lf_s = [next(it) for _ in range(4)]
    c = CHUNK
    n_chunks = seq // c
    combos = [(d, h) for d in range(2) for h in range(H_A)]
    lanes = [slice(h * DK_A, (h + 1) * DK_A) for h in range(H_A)]
    add = lambda a, b: a + b

    lb_raw = lb_ref[...]
    lb_e = jnp.exp(lb_raw - jnp.max(lb_raw, axis=0, keepdims=True))
    lb_all = lb_e[0:1] / jnp.sum(lb_e, axis=0, keepdims=True)

    for d in range(2):
        for h in range(H_A):
            s_scr[d, h] = s0_ref[0, 0, d, h].T if has_s0 else jnp.zeros((DV_A, DK_A), F32)
    acc[...] = jnp.zeros(acc.shape, F32)

    worst = []
    for d, fr_ref in enumerate((ff_ref, fb_ref)):
        f = lb_all + (1.0 - lb_all) * _sigmoid(fr_ref[...])
        lf = jnp.log(f)
        f_s[d] = f
        lf_s[d] = lf
        worst.append(jnp.max(_dot(half_ref[...], (-lf).astype(BF16))))
    safe = jnp.maximum(worst[0], worst[1]) <= HG_SAFE_EXP

    def fast_body(n, carry):
        cf = HG_FAST
        n_fast = seq // cf
        steps = range(HG_FAST_STEPS)
        chunk = lambda d, t: (n * HG_FAST_STEPS + t) if d == 0 else (n_fast - 1 - n * HG_FAST_STEPS - t)
        rows = [[pl.ds(pl.multiple_of(chunk(d, t) * cf, cf), cf) for t in steps] for d in range(2)]
        units = [(t, d, h) for t in steps for d, h in combos]
        e_all = [[jnp.exp(_dot_const(mf_ref[d], lf_s[d, rows[d][t], :])) for t in steps] for d in range(2)]
        q_all = [[_silu(qa_ref[rows[d][t], :]) * DK_A ** -0.5 for t in steps] for d in range(2)]
        k_all = [[1.0 - f_s[d, rows[d][t], :] for t in steps] for d in range(2)]
        v_all = [[ia_ref[rows[d][t], :].astype(BF16) for t in steps] for d in range(2)]
        qs = {u: q_all[u[1]][u[0]][:, lanes[u[2]]] for u in units}
        ks = {u: k_all[u[1]][u[0]][:, lanes[u[2]]] for u in units}
        vs = {u: v_all[u[1]][u[0]][:, lanes[u[2]]] for u in units}
        es = {u: [e_all[u[1]][u[0]][i * cf:(i + 1) * cf, lanes[u[2]]] for i in range(4)] for u in units}
        p = {u: jnp.where(causal_ref[u[1]] > 0.0,
                          _dot_nt((qs[u] * es[u][0]).astype(BF16), (ks[u] * es[u][1]).astype(BF16)), 0.0).astype(BF16)
             for u in units}
        intra = {u: _dot(p[u], vs[u]) for u in units}
        upd = {u: _dot_tn(vs[u], (ks[u] * es[u][3]).astype(BF16)) for u in units}
        qdec = {u: (qs[u] * es[u][2]).astype(BF16) for u in units}
        st = {(d, h): s_scr[d, h] for d, h in combos}
        o = {}
        for t in steps:
            for d, h in combos:
                u = (t, d, h)
                o[u] = intra[u] + _dot_nt(qdec[u], st[d, h].astype(BF16))
                st[d, h] = st[d, h] * e_all[d][t][4 * cf:4 * cf + 1, lanes[h]] + upd[u]
        for t in steps:
            for d in range(2):
                acc[rows[d][t], :] += jnp.concatenate([o[t, d, h] for h in range(H_A)], axis=1)
        for d, h in combos:
            s_scr[d, h] = st[d, h]
        return carry

    def body(n, carry):
        rows = [pl.ds(pl.multiple_of((n if d == 0 else n_chunks - 1 - n) * c, c), c) for d in range(2)]
        f_all = [f_s[d, rows[d], :] for d in range(2)]
        e_all = [jnp.exp(_dot_const(mc_ref[d], lf_s[d, rows[d], :])) for d in range(2)]
        q_all = [_silu(qa_ref[rows[d], :]) * DK_A ** -0.5 for d in range(2)]
        v_all = [ia_ref[rows[d], :].astype(BF16) for d in range(2)]
        st = [s_scr[d, h] for d, h in combos]
        qs, ks, vs, es = [], [], [], []
        for d, h in combos:
            qs.append(q_all[d][:, lanes[h]])
            ks.append(1.0 - f_all[d][:, lanes[h]])
            vs.append(v_all[d][:, lanes[h]])
            es.append(e_all[d][:, lanes[h]])
        lvl = [[e[i * c:(i + 1) * c] for i in range(HG_NL + 2)] for e in es]
        qst = [jnp.concatenate([q] + [q * l[i] for i in range(HG_NL)], axis=0).astype(BF16) for q, l in zip(qs, lvl)]
        kst = [jnp.concatenate([k] + [k * l[i] for i in range(HG_NL)], axis=0).astype(BF16) for k, l in zip(ks, lvl)]
        r = [(_dot_nt(qst[i], kst[i]) * mask_ref[d]).astype(BF16) for i, (d, h) in enumerate(combos)]
        ost = [_dot(r[i], jnp.concatenate([vs[i]] * (HG_NL + 1), axis=0)) for i in range(len(combos))]
        inter = [_dot_nt((qs[i] * lvl[i][HG_NL]).astype(BF16), st[i].astype(BF16)) for i in range(len(combos))]
        upd = [_dot_tn(vs[i], (ks[i] * lvl[i][HG_NL + 1]).astype(BF16)) for i in range(len(combos))]
        o = [functools.reduce(lambda a, b: a + b, [ost[i][j * c:(j + 1) * c] for j in range(HG_NL + 1)]) + inter[i]
             for i in range(len(combos))]
        for d in range(2):
            acc[rows[d], :] += jnp.concatenate(o[d * H_A:(d + 1) * H_A], axis=1)
        for i, (d, h) in enumerate(combos):
            e_tot = es[i][(HG_NL + 2) * c:(HG_NL + 2) * c + 1]
            s_scr[d, h] = st[i] * e_tot + upd[i]
        return carry

    @pl.when(safe)
    def _():
        lax.fori_loop(0, seq // (HG_FAST * HG_FAST_STEPS), fast_body, 0)

    @pl.when(jnp.logical_not(safe))
    def _():
        lax.fori_loop(0, n_chunks, body, 0)

    for h in range(H_A):
        ln = slice(h * DV_A, (h + 1) * DV_A)
        o_ref[:, ln] = _rms_gate(acc[:, ln], gn_ref[...], ga_ref[:, ln])
    if emit_state:
        for d in range(2):
            for h in range(H_A):
                st_ref[0, 0, d, h] = s_scr[d, h].T


def _hgrn(proj, hgrn_lb, gn, consts, prompt, s0=None, layer=0):
    seq, nb, rb0 = _seq_layout(prompt)
    consts = list(consts) + list(_hgrn_fast_consts(seq))
    wa = H_A * DK_A
    blk = lambda j: pl.BlockSpec((seq, wa), lambda b: (rb0 + b, j))
    const2 = lambda b: (0, 0)
    st_block = (1, 1, 2, H_A, DK_A, DV_A)
    in_specs = [blk(0), blk(1), blk(2), blk(3), blk(4),
                pl.BlockSpec(hgrn_lb.shape, const2), pl.BlockSpec((1, DV_A), const2)]
    in_specs += [pl.BlockSpec(m.shape, lambda b, nd=m.ndim: (0,) * nd) for m in consts]
    args = [proj] * 5 + [hgrn_lb, gn.reshape(1, DV_A)] + consts
    if s0 is not None:
        in_specs.append(pl.BlockSpec(st_block, lambda b: (b, layer, 0, 0, 0, 0)))
        args.append(s0)
    out_specs = [pl.BlockSpec((seq, wa), lambda b: (b, 0))]
    out_shape = [jax.ShapeDtypeStruct((nb * seq, wa), F32)]
    if prompt:
        out_specs.append(pl.BlockSpec(st_block, lambda b: (b, 0, 0, 0, 0, 0)))
        out_shape.append(jax.ShapeDtypeStruct((nb, 1, 2, H_A, DK_A, DV_A), F32))
    return pl.pallas_call(
        functools.partial(_hgrn_kernel, seq=seq, has_s0=s0 is not None, emit_state=prompt),
        grid=(nb,),
        in_specs=in_specs,
        out_specs=out_specs,
        out_shape=out_shape,
        scratch_shapes=[pltpu.VMEM((2, H_A, DV_A, DK_A), F32), pltpu.VMEM((seq, wa), F32),
                        pltpu.VMEM((2, seq, wa), F32), pltpu.VMEM((2, seq, wa), F32)],
        compiler_params=_cparams("arbitrary"),
        name="hgrn_prompt" if prompt else "hgrn_sample",
    )(*args)


GD_SUB = GBLK // CHUNK
GD_BLOCKS_PER_ITER = 2
GD_ROWS = 2 * GBLK + TOT_ROWS


def _gdn_consts():
    n, c = GBLK, CHUNK
    same = (np.arange(n)[:, None] // c) == (np.arange(n)[None, :] // c)
    tri = (same & (np.arange(n)[None, :] <= np.arange(n)[:, None])).astype(np.float32)
    sup = (same & (np.arange(n)[None, :] > np.arange(n)[:, None])).astype(np.float32)
    tot = np.zeros((TOT_ROWS, n), np.float32)
    for s in range(GD_SUB):
        tot[s, s * c:(s + 1) * c] = 1.0
    mgs, tts, tris = [], [], []
    for reverse in (False, True):
        t = _flip_blocks(tri, c) if reverse else tri
        s = _flip_blocks(sup, c) if reverse else sup
        mg = np.concatenate([t, s, tot], axis=0)
        mgs.append(np.concatenate([mg, mg], axis=1))
        tts.append(np.concatenate([t.T, t.T], axis=0))
        tris.append(t)
    tris.append(same.astype(np.float32))
    return jnp.asarray(np.stack(mgs), BF16), jnp.asarray(np.stack(tts), BF16), jnp.asarray(np.stack(tris), F32)


def _softplus(x):
    return jnp.maximum(x, 0.0) + jnp.log(1.0 + jnp.exp(-jnp.abs(x)))


CONV_PAD = 8


def _conv_silu(x, w, seq):
    half = SHORT_CONV // 2
    pad = jnp.zeros((CONV_PAD, x.shape[1]), x.dtype)
    xe = jnp.concatenate([pad, x, pad], axis=0)
    acc = xe * w[half:half + 1]
    for j in range(SHORT_CONV):
        shift = half - j
        if shift != 0:
            acc = acc + pltpu.roll(xe, shift % (seq + 2 * CONV_PAD), axis=0) * w[j:j + 1]
    return _silu(acc[CONV_PAD:seq + CONV_PAD])


def _l2norm_heads(x, n_heads, width, scale):
    outs = []
    for h in range(n_heads):
        xh = x[:, h * width:(h + 1) * width]
        outs.append(xh * (lax.rsqrt(jnp.sum(xh * xh, axis=-1, keepdims=True) + EPS) * scale))
    return jnp.concatenate(outs, axis=-1)


def _gdn_kernel(*refs, seq, has_s0, emit_state):
    it = iter(refs)
    (q_ref, k_ref, v_ref, gb_ref, gate_ref, cw_ref, alog_ref, dt_ref, gn_ref,
     mg_ref, tt_ref, tri_ref) = [next(it) for _ in range(12)]
    s0_ref = next(it) if has_s0 else None
    o_ref = next(it)
    st_ref = next(it) if emit_state else None
    qn, kn, vn, u_s, w_s, qg_s, kdt_s, at_s, et_s, s_scr, acc = [next(it) for _ in range(11)]
    c = CHUNK
    n_chunks = seq // c
    n_blocks = seq // GBLK
    wq = H_B * DK_B
    n_dh = 2 * H_B
    combos = [(d, h) for d in range(2) for h in range(H_B)]
    lanes = [slice(h * DK_B, (h + 1) * DK_B) for h in range(H_B)]

    qn[...] = _l2norm_heads(_conv_silu(q_ref[...], cw_ref[:, 0:wq], seq), H_B, DK_B, DK_B ** -0.5)
    kn[...] = _l2norm_heads(_conv_silu(k_ref[...], cw_ref[:, wq:2 * wq], seq), H_B, DK_B, 1.0)
    vn[...] = _conv_silu(v_ref[...], cw_ref[:, 2 * wq:3 * wq], seq)
    for i in range(2 * H_B):
        s_scr[i] = s0_ref[0, 0, i // H_B, i % H_B] if has_s0 else jnp.zeros((DK_B, DV_B), F32)
    acc[...] = jnp.zeros(acc.shape, F32)

    eye = (lax.broadcasted_iota(jnp.int32, (GBLK, GBLK), 0)
           == lax.broadcasted_iota(jnp.int32, (GBLK, GBLK), 1)).astype(F32)
    eye_pk = (lax.broadcasted_iota(jnp.int32, (c, GBLK), 0)
              == lax.broadcasted_iota(jnp.int32, (c, GBLK), 1) % c).astype(F32)
    bwd_lane = lax.broadcasted_iota(jnp.int32, (1, 128), 1) % n_dh >= H_B
    add = lambda a, b: a + b

    same_chunk = tri_ref[2].astype(BF16)

    def expand(pk):
        return jnp.concatenate([pk] * GD_SUB, axis=0) * same_chunk

    def pack(bd):
        return functools.reduce(add, [bd[s * c:(s + 1) * c] for s in range(GD_SUB)])

    def weights(hi, lo):
        return jnp.concatenate([expand(hi), expand(lo)], axis=1)

    def dot3_split(a_hi, a_lo, w2):
        m, n = a_hi.shape[0], w2.shape[1] // 2
        t = _dot(jnp.concatenate([a_hi, a_lo], axis=0), w2)
        return t[:m, :n] + t[m:, :n] + t[:m, n:]

    def block_body(it, carry):
        blks = [it * GD_BLOCKS_PER_ITER + o for o in range(GD_BLOCKS_PER_ITER)]
        units = [(o, d, h) for o in range(GD_BLOCKS_PER_ITER) for d, h in combos]
        idx = lambda d, h: d * H_B + h
        col = lambda x, j: jnp.broadcast_to(x[:, j:j + 1], (GBLK, DK_B))
        rows = [pl.ds(pl.multiple_of(b * GBLK, GBLK), GBLK) for b in blks]
        gates = [gate_ref[r, :] for r in rows]
        glog_all = [-jnp.exp(alog_ref[...]) * _softplus(x + dt_ref[...]) for x in gates]
        beta_all = [_sigmoid(x) for x in gates]
        g2 = [jnp.concatenate(_split2(x), axis=0) for x in glog_all]
        dg = [[_dot(mg_ref[d], x) for d in range(2)] for x in g2]
        dsel = [jnp.where(bwd_lane, x[1], x[0]) for x in dg]
        eg_all = [jnp.exp(x) for x in dsel]
        gt = [[_dot_tn(x, tt_ref[d]) for d in range(2)] for x in g2]
        qs = [[qn[r, ln] for ln in lanes] for r in rows]
        ks = [[kn[r, ln] for ln in lanes] for r in rows]
        vs = [[vn[r, ln] for ln in lanes] for r in rows]
        betas = [col(beta_all[o], n_dh + idx(d, h)) for o, d, h in units]
        kbs = [ks[o][h] * betas[u] for u, (o, d, h) in enumerate(units)]
        kb_of = {unit: kbs[u] for u, unit in enumerate(units)}
        kk = {(o, h): _dot_nt(jnp.concatenate([qs[o][h], kb_of[o, 0, h], kb_of[o, 1, h]], axis=0).astype(BF16),
                              ks[o][h].astype(BF16))
              for o in range(GD_BLOCKS_PER_ITER) for h in range(H_B)}
        decay = []
        for o, d, h in units:
            inside = tri_ref[d] > 0.0
            gd = col(dsel[o][:GBLK], idx(d, h)) - gt[o][d][idx(d, h):idx(d, h) + 1, :]
            decay.append(jnp.where(inside, jnp.exp(jnp.where(inside, gd, 0.0)), 0.0))
        attn = [kk[o, h][:GBLK] * decay[u] for u, (o, d, h) in enumerate(units)]
        p_pk = [pack(kk[o, h][(1 + d) * GBLK:(2 + d) * GBLK] * decay[u] * (1.0 - eye))
                for u, (o, d, h) in enumerate(units)]
        x_pk = [eye_pk - p for p in p_pk]
        p_sp = [_split2(p) for p in p_pk]
        p_w = [weights(*s) for s in p_sp]
        for _ in range(CHUNK.bit_length() - 2):
            p_pk = [dot3_split(*s, w) for s, w in zip(p_sp, p_w)]
            p_sp = [_split2(p) for p in p_pk]
            p_w = [weights(*s) for s in p_sp]
            x_pk = [x + dot3_split(*_split2(x), w) for x, w in zip(x_pk, p_w)]
        eg_col = [col(eg_all[o][:GBLK], idx(d, h)) for o, d, h in units]
        ekd_col = [col(eg_all[o][GBLK:2 * GBLK], idx(d, h)) for o, d, h in units]
        rhs = [_split2(jnp.concatenate([vs[o][h] * betas[u], kbs[u] * eg_col[u]], axis=1))
               for u, (o, d, h) in enumerate(units)]
        t_sp = [[expand(part) for part in _split2(x)] for x in x_pk]
        uw = [_dot(t[0], r[0]) + (_dot(t[0], r[1]) + _dot(t[1], r[0])) for t, r in zip(t_sp, rhs)]
        for u, (o, d, h) in enumerate(units):
            i = idx(d, h)
            qg = (qs[o][h] * eg_col[u]).astype(BF16)
            kdt_s[i, blks[o]] = (ks[o][h] * ekd_col[u]).T.astype(BF16)
            for s in range(GD_SUB):
                cn = blks[o] * GD_SUB + s
                r = slice(s * c, (s + 1) * c)
                u_s[i, cn] = uw[u][r, :DV_B]
                w_s[i, cn] = uw[u][r, DV_B:].astype(BF16)
                qg_s[i, cn] = qg[r]
                at_s[i, cn] = attn[u][r].astype(BF16)
                et_s[i, cn] = jnp.broadcast_to(eg_all[o][2 * GBLK + s:2 * GBLK + s + 1, i:i + 1], (8, DV_B))
        return carry

    lax.fori_loop(0, n_blocks // GD_BLOCKS_PER_ITER, block_body, 0)

    def chunk_body(n, carry):
        cns = [n, n_chunks - 1 - n]
        rows = [pl.ds(pl.multiple_of(cn * c, c), c) for cn in cns]
        sub_of_row = lax.broadcasted_iota(jnp.int32, (GBLK, 1), 0) // c
        in_chunk = [sub_of_row == cn % GD_SUB for cn in cns]
        st = [s_scr[i] for i in range(n_dh)]
        ws = [_dot(jnp.concatenate([w_s[i, cns[d]], qg_s[i, cns[d]]], axis=0), st[i].astype(BF16))
              for i, (d, h) in enumerate(combos)]
        vblk = [jnp.where(in_chunk[d], jnp.concatenate([u_s[i, cns[d]] - ws[i][:c]] * GD_SUB, axis=0), 0.0).astype(BF16)
                for i, (d, h) in enumerate(combos)]
        r = [_dot(jnp.concatenate([at_s[i, cns[d]], kdt_s[i, cns[d] // GD_SUB]], axis=0), vblk[i])
             for i, (d, h) in enumerate(combos)]
        for d in range(2):
            acc[rows[d], :] += jnp.concatenate([ws[i][c:] + r[i][:c] for i in range(d * H_B, (d + 1) * H_B)], axis=1)
        for i, (d, h) in enumerate(combos):
            s_scr[i] = st[i] * et_s[i, cns[d]][0:1] + r[i][c:]
        return carry

    lax.fori_loop(0, n_chunks, chunk_body, 0)

    for h in range(H_B):
        ln = slice(h * DV_B, (h + 1) * DV_B)
        o_ref[:, ln] = _rms_gate(acc[:, ln], gn_ref[...], gb_ref[:, ln])
    if emit_state:
        for i in range(2 * H_B):
            st_ref[0, 0, i // H_B, i % H_B] = s_scr[i]


def _gdn(proj, gates, conv_w, a_log, dt_bias, gn, consts, prompt, s0=None, layer=0):
    seq, nb, rb0 = _seq_layout(prompt)
    n_chunks = seq // CHUNK
    wq = H_B * DK_B
    blk = lambda j: pl.BlockSpec((seq, wq), lambda b: (rb0 + b, j))
    const2 = lambda b: (0, 0)
    const3 = lambda b: (0, 0, 0)
    st_block = (1, 1, 2, H_B, DK_B, DV_B)
    pad_row = lambda p: jnp.pad(p.reshape(1, -1).astype(F32), ((0, 0), (0, 128 - p.size)))
    in_specs = [blk(5), blk(6), blk(7), blk(8),
                pl.BlockSpec((seq, 128), lambda b: (rb0 + b, 0)),
                pl.BlockSpec((SHORT_CONV, 3 * wq), const2),
                pl.BlockSpec((1, 128), const2), pl.BlockSpec((1, 128), const2), pl.BlockSpec((1, DV_B), const2)]
    in_specs += [pl.BlockSpec(m.shape, const3) for m in consts]
    args = [proj] * 4 + [gates, conv_w.reshape(SHORT_CONV, 3 * wq), pad_row(a_log), pad_row(dt_bias),
                         gn.reshape(1, DV_B)] + list(consts)
    if s0 is not None:
        in_specs.append(pl.BlockSpec(st_block, lambda b: (b, layer, 0, 0, 0, 0)))
        args.append(s0)
    out_specs = [pl.BlockSpec((seq, wq), lambda b: (b, 0))]
    out_shape = [jax.ShapeDtypeStruct((nb * seq, wq), F32)]
    if prompt:
        out_specs.append(pl.BlockSpec(st_block, lambda b: (b, 0, 0, 0, 0, 0)))
        out_shape.append(jax.ShapeDtypeStruct((nb, 1, 2, H_B, DK_B, DV_B), F32))
    n_dh = 2 * H_B
    scratch = ([pltpu.VMEM((seq, wq), F32)] * 3
               + [pltpu.VMEM((n_dh, n_chunks, CHUNK, DV_B), F32)]
               + [pltpu.VMEM((n_dh, n_chunks, CHUNK, DK_B), BF16)] * 2
               + [pltpu.VMEM((n_dh, seq // GBLK, DK_B, GBLK), BF16),
                  pltpu.VMEM((n_dh, n_chunks, CHUNK, GBLK), BF16),
                  pltpu.VMEM((n_dh, n_chunks, 8, DV_B), F32),
                  pltpu.VMEM((n_dh, DK_B, DV_B), F32),
                  pltpu.VMEM((seq, wq), F32)])
    return pl.pallas_call(
        functools.partial(_gdn_kernel, seq=seq, has_s0=s0 is not None, emit_state=prompt),
        grid=(nb,),
        in_specs=in_specs,
        out_specs=out_specs,
        out_shape=out_shape,
        scratch_shapes=scratch,
        compiler_params=_cparams("arbitrary"),
        name="gdn_prompt" if prompt else "gdn_sample",
    )(*args)


def kernel(x_prompt, x_sample, state_hgrn, state_gdn, cache_na_k, cache_na_v, c, c_ctx, ada_w, ada_b, norm_g, w_in_ab, w_out_ab, hgrn_lb, gdn_conv, gdn_a_log, gdn_dt_bias, gn_hgrn, gn_gdn, w_qkv_na, qn_na, kn_na, rpb_na, w_out_na, w_mlp1, w_mlp2):
    cond = jnp.concatenate([c_ctx[None, :], c, jnp.zeros((N_MOD_ROWS - 1 - DEC_BATCH, D_MODEL), F32)], axis=0)
    mods, w_in = _modulation(cond, ada_w, ada_b, w_in_ab[0])
    xs = (x_prompt.reshape(N_PROMPT, D_MODEL), x_sample.reshape(N_SAMPLE, D_MODEL))

    w_gate = jnp.pad(w_in[:, D_MAIN_AB:], ((0, 0), (0, 128 - N_GATE_AB)))
    proj, gates, wo0, w1_0, w2_0 = _norm_proj(xs, mods[0], norm_g[0, 0], [w_in, w_gate], widths=[D_MAIN_AB, 128],
                                              side=(w_out_ab[0], w_mlp1[0], w_mlp2[0]))
    hg_consts = _hgrn_consts()
    gd_consts = _gdn_consts()
    hg_prompt, new_hgrn = _hgrn(proj, hgrn_lb, gn_hgrn[0], hg_consts, True)
    hg_sample, = _hgrn(proj, hgrn_lb, gn_hgrn[0], hg_consts, False, s0=state_hgrn)
    gd_args = (gdn_conv[0], gdn_a_log[0], gdn_dt_bias[0], gn_gdn[0], gd_consts)
    gd_prompt, new_gdn = _gdn(proj, gates, *gd_args, True)
    gd_sample, = _gdn(proj, gates, *gd_args, False, s0=state_gdn)
    xs, (w_qkv, wo1, w1_1, w2_1) = _post_mixer(xs, [(hg_prompt, hg_sample), (gd_prompt, gd_sample)], mods[0],
                                               norm_g[0, 1], wo0, w1_0, w2_0,
                                               side=(w_qkv_na[0], w_out_na[0], w_mlp1[1], w_mlp2[1]))

    qkv, = _norm_proj(xs, mods[1], norm_g[1, 0], [w_qkv])
    at_prompt, new_kt, new_vt = _ctx_attention(qkv, qn_na[0], kn_na[0])
    time_minor = lambda a: jnp.swapaxes(a, -1, -2).reshape(a.shape[0], 1, H_C * HD_C, a.shape[3])
    at_sample = _na_attention(qkv, time_minor(cache_na_k), time_minor(cache_na_v), qn_na[0], kn_na[0], rpb_na[0])
    time_major = lambda a: jnp.swapaxes(a.reshape(BATCH, 1, H_C, HD_C, SEQ), -1, -2)
    new_k, new_v = time_major(new_kt), time_major(new_vt)
    (y_prompt, y_sample), _ = _post_mixer(xs, [(at_prompt, at_sample)], mods[1], norm_g[1, 1], wo1, w1_1, w2_1,
                                          split_out=True)

    return (y_prompt.reshape(BATCH, SEQ, D_MODEL), y_sample.reshape(DEC_BATCH, DEC_SEQ, D_MODEL),
            new_hgrn, new_gdn, new_k, new_v)
```

```python
import functools

import numpy as np
import jax
import jax.numpy as jnp
from jax import lax
from jax.experimental import pallas as pl
from jax.experimental.pallas import tpu as pltpu

F32 = jnp.float32
BF16 = jnp.bfloat16

D_MODEL = 1024
BATCH = 16
SEQ = 256
DEC_BATCH = 4
DEC_SEQ = 1024
PAST_LEN = 256
N_PROMPT = BATCH * SEQ
N_SAMPLE = DEC_BATCH * DEC_SEQ
N_TOK = N_PROMPT + N_SAMPLE
GRID_W = 64
GRID_ROWS = DEC_SEQ // GRID_W
H_A = 4
DK_A = 128
DV_A = 128
H_B = 4
DK_B = 128
DV_B = 128
SHORT_CONV = 5
H_C = 16
HD_C = 64
KH = 8
KW = 16
D_FF = 4 * D_MODEL
EPS = 1e-6
NEG_INF = -1e30
N_MOD_ROWS = 8
D_MAIN_AB = 4608
N_GATE_AB = 16
CHUNK = 32
GBLK = 128
VMEM_LIMIT = 56 * 1024 * 1024


def _cparams(*sem):
    return pltpu.CompilerParams(dimension_semantics=sem, vmem_limit_bytes=VMEM_LIMIT)


def _sigmoid(x):
    return 0.5 * jnp.tanh(0.5 * x) + 0.5


def _silu(x):
    return x * _sigmoid(x)


def _dot(a, b):
    return jnp.dot(a, b, preferred_element_type=F32)


def _dot_nt(a, b):
    return lax.dot_general(a, b, (((1,), (1,)), ((), ())), preferred_element_type=F32)


def _dot_tn(a, b):
    return lax.dot_general(a, b, (((0,), (0,)), ((), ())), preferred_element_type=F32)


def _split2(x):
    hi = x.astype(BF16)
    lo = (x - hi.astype(F32)).astype(BF16)
    return hi, lo


def _dot_const(m2, x):
    hi, lo = _split2(x)
    return _dot(m2, jnp.concatenate([hi, lo], axis=0))


def _dot3(a, b):
    ah, al = _split2(a)
    bh, bl = _split2(b)
    return _dot(ah, bh) + (_dot(ah, bl) + _dot(al, bh))


def _mod_row(i, tm):
    start = i * tm
    return jnp.where(start < N_PROMPT, 0, 1 + (start - N_PROMPT) // DEC_SEQ)


def _mod_slice(mod_ref, row, k):
    return mod_ref[pl.ds(row, 1), k * D_MODEL:(k + 1) * D_MODEL]


def _norm_mod(x, g, sc, sh):
    ms = jnp.mean(x * x, axis=-1, keepdims=True)
    return (x * lax.rsqrt(ms + EPS) * g) * (1.0 + sc) + sh


def _mod_kernel(cond_ref, w_ref, b_ref, side_ref, o_ref, side_o_ref):
    s = _silu(cond_ref[...]).astype(BF16)
    o_ref[0] = _dot(s, w_ref[0].astype(BF16)) + b_ref[0]
    side_o_ref[...] = side_ref[...].astype(BF16)


MOD_TN = 768


def _modulation(cond8, ada_w, ada_b, side):
    depth = ada_w.shape[0]
    nj = ada_w.shape[2] // MOD_TN
    slab = side.shape[1] // (depth * nj)
    return pl.pallas_call(
        _mod_kernel,
        grid=(depth, nj),
        in_specs=[
            pl.BlockSpec((N_MOD_ROWS, D_MODEL), lambda l, j: (0, 0)),
            pl.BlockSpec((1, D_MODEL, MOD_TN), lambda l, j: (l, 0, j)),
            pl.BlockSpec((1, 1, MOD_TN), lambda l, j: (l, 0, j)),
            pl.BlockSpec((None, slab, side.shape[2]), lambda l, j: (0, l * nj + j, 0)),
        ],
        out_specs=[pl.BlockSpec((1, N_MOD_ROWS, MOD_TN), lambda l, j: (l, 0, j)),
                   pl.BlockSpec((slab, side.shape[2]), lambda l, j: (l * nj + j, 0))],
        out_shape=[jax.ShapeDtypeStruct((depth, N_MOD_ROWS, ada_w.shape[2]), F32),
                   jax.ShapeDtypeStruct(side.shape[1:], BF16)],
        compiler_params=_cparams("arbitrary", "arbitrary"),
        name="modulation",
    )(cond8, ada_w, ada_b.reshape(depth, 1, -1), side)


def _stream_specs(n_arrays, tm, width=D_MODEL):
    if n_arrays == 1:
        return [pl.BlockSpec((tm, width), lambda i: (i, 0))]
    npt = N_PROMPT // tm
    return [pl.BlockSpec((tm, width), lambda i: (jnp.minimum(i, npt - 1), 0)),
            pl.BlockSpec((tm, width), lambda i: (jnp.maximum(i - npt, 0), 0))]


def _stream_load(x_refs, tm):
    if len(x_refs) == 1:
        return x_refs[0][...]
    return jnp.where(pl.program_id(0) < N_PROMPT // tm, x_refs[0][...], x_refs[1][...])


def _side_specs(side, n_steps):
    in_specs = [pl.BlockSpec((None, w.shape[1] // n_steps, w.shape[2]), lambda i, l=l: (l, i, 0)) for w, l in side]
    out_specs = [pl.BlockSpec((w.shape[1] // n_steps, w.shape[2]), lambda i: (i, 0)) for w, _ in side]
    shapes = [jax.ShapeDtypeStruct(w.shape[1:], BF16) for w, _ in side]
    return in_specs, out_specs, shapes


def _side_cast(in_refs, out_refs):
    for i_ref, o_ref in zip(in_refs, out_refs):
        o_ref[...] = i_ref[...].astype(BF16)


def _norm_proj_kernel(*refs, tm, n_x, n_w, n_side):
    x_refs, (mod_ref, g_ref) = refs[:n_x], refs[n_x:n_x + 2]
    w_refs = refs[n_x + 2:n_x + 2 + n_w]
    side_in = refs[n_x + 2 + n_w:n_x + 2 + n_w + n_side]
    o_refs = refs[n_x + 2 + n_w + n_side:n_x + 2 + 2 * n_w + n_side]
    side_out = refs[n_x + 2 + 2 * n_w + n_side:]
    row = _mod_row(pl.program_id(0), tm)
    h = _norm_mod(_stream_load(x_refs, tm), g_ref[...], _mod_slice(mod_ref, row, 1), _mod_slice(mod_ref, row, 0)).astype(BF16)
    for w_ref, o_ref in zip(w_refs, o_refs):
        o_ref[...] = _dot(h, w_ref[...])
    _side_cast(side_in, side_out)


def _norm_proj(xs, mod, g, ws, widths=None, side=(), tm=512):
    n_w = len(ws)
    widths = widths or [w.shape[1] for w in ws]
    const = lambda i: (0, 0)
    side_in_specs, side_out_specs, side_shapes = _side_specs(side, N_TOK // tm)
    return pl.pallas_call(
        functools.partial(_norm_proj_kernel, tm=tm, n_x=len(xs), n_w=n_w, n_side=len(side)),
        grid=(N_TOK // tm,),
        in_specs=_stream_specs(len(xs), tm) + [
            pl.BlockSpec(mod.shape, const),
            pl.BlockSpec((1, D_MODEL), const),
        ] + [pl.BlockSpec((D_MODEL, n), const, pipeline_mode=pl.Buffered(1)) for n in widths] + side_in_specs,
        out_specs=[pl.BlockSpec((tm, n), lambda i: (i, 0)) for n in widths] + side_out_specs,
        out_shape=[jax.ShapeDtypeStruct((N_TOK, n), F32) for n in widths] + side_shapes,
        compiler_params=_cparams("arbitrary"),
        name="norm_proj",
    )(*xs, mod, g.reshape(1, D_MODEL), *ws, *[w for w, _ in side])


def _post_kernel(*refs, tm, ff_chunk, n_x, n_y, groups, n_side):
    x_refs = refs[:n_x]
    n_m = sum(n for n, _ in groups)
    m_refs = refs[n_x:n_x + n_m]
    mod_ref, g_ref, wo_ref, w1_ref, w2_ref = refs[n_x + n_m:n_x + n_m + 5]
    side_in = refs[n_x + n_m + 5:n_x + n_m + 5 + n_side]
    y_refs = refs[n_x + n_m + 5 + n_side:n_x + n_m + 5 + n_side + n_y]
    _side_cast(side_in, refs[n_x + n_m + 5 + n_side + n_y:])
    row = _mod_row(pl.program_id(0), tm)
    mix, first_ref, first_col = None, 0, 0
    for n, width in groups:
        part = _stream_load(m_refs[first_ref:first_ref + n], tm).astype(BF16)
        term = _dot(part, wo_ref[first_col:first_col + width, :])
        mix = term if mix is None else mix + term
        first_ref, first_col = first_ref + n, first_col + width
    x1 = _stream_load(x_refs, tm) + _mod_slice(mod_ref, row, 2) * mix
    h = _norm_mod(x1, g_ref[...], _mod_slice(mod_ref, row, 4), _mod_slice(mod_ref, row, 3)).astype(BF16)
    acc = jnp.zeros((tm, D_MODEL), F32)
    for k in range(0, D_FF, ff_chunk):
        a = jnp.maximum(_dot(h, w1_ref[:, k:k + ff_chunk]), 0.0)
        acc = acc + _dot((a * a).astype(BF16), w2_ref[k:k + ff_chunk, :])
    y = x1 + _mod_slice(mod_ref, row, 5) * acc
    if n_y == 1:
        y_refs[0][...] = y
    else:
        is_prompt = pl.program_id(0) < N_PROMPT // tm

        @pl.when(is_prompt)
        def _():
            y_refs[0][...] = y

        @pl.when(jnp.logical_not(is_prompt))
        def _():
            y_refs[1][...] = y


def _post_mixer(xs, mixed, mod, g, wo, w1, w2, split_out=False, side=(), tm=512, ff_chunk=1024):
    const = lambda i: (0, 0)
    resident = lambda w: pl.BlockSpec(w.shape, const, pipeline_mode=pl.Buffered(1))
    n_y = 2 if split_out else 1
    rows = (N_PROMPT, N_SAMPLE) if split_out else (N_TOK,)
    groups = tuple((len(grp), grp[0].shape[1]) for grp in mixed)
    mixed_specs = [spec for n, width in groups for spec in _stream_specs(n, tm, width)]
    side_in_specs, side_out_specs, side_shapes = _side_specs(side, N_TOK // tm)
    out = pl.pallas_call(
        functools.partial(_post_kernel, tm=tm, ff_chunk=ff_chunk, n_x=len(xs), n_y=n_y, groups=groups,
                          n_side=len(side)),
        grid=(N_TOK // tm,),
        in_specs=_stream_specs(len(xs), tm) + mixed_specs + [
            pl.BlockSpec(mod.shape, const),
            pl.BlockSpec((1, D_MODEL), const),
            resident(wo), resident(w1), resident(w2),
        ] + side_in_specs,
        out_specs=_stream_specs(n_y, tm) + side_out_specs,
        out_shape=[jax.ShapeDtypeStruct((r, D_MODEL), F32) for r in rows] + side_shapes,
        compiler_params=_cparams("arbitrary"),
        name="post_mixer",
    )(*xs, *[a for grp in mixed for a in grp], mod, g.reshape(1, D_MODEL), wo, w1, w2, *[w for w, _ in side])
    return tuple(out[:n_y]), tuple(out[n_y:])


PAIR = 2 * HD_C


def _pair_consts():
    lane = lax.broadcasted_iota(jnp.int32, (1, PAIR), 1)
    first = lane < HD_C
    ones_col = [jnp.where(lane == HD_C, 1.0, 0.0), jnp.where(lane == 0, 1.0, 0.0)]
    r = lax.broadcasted_iota(jnp.int32, (2 * PAIR, PAIR), 0) % PAIR
    cidx = lax.broadcasted_iota(jnp.int32, (2 * PAIR, PAIR), 1)
    mean2 = jnp.where(r // HD_C == cidx // HD_C, 1.0 / HD_C, 0.0).astype(BF16)
    return first, ones_col, mean2


def _pair_norm(x, w2, mean2):
    hi, lo = _split2(x * x)
    ms = _dot(jnp.concatenate([hi, lo], axis=1), mean2)
    return x * lax.rsqrt(ms + EPS) * w2


def _pair_queries(q, first):
    return [jnp.where(first, q, 0.0).astype(BF16), jnp.where(first, 0.0, q).astype(BF16)]


def _pair_values(v, first, ones_col):
    return [jnp.where(first, v, ones_col[0]).astype(BF16), jnp.where(first, ones_col[1], v).astype(BF16)]


def _pair_output(o_aug, first):
    den = [o_aug[0][:, HD_C:HD_C + 1], o_aug[1][:, 0:1]]
    return jnp.where(first, o_aug[0] / den[0], o_aug[1] / den[1])


def _row_max(*pieces):
    tiles = [p[:, i:i + 128] for p in pieces for i in range(0, p.shape[1], 128)]
    return jnp.max(functools.reduce(jnp.maximum, tiles), axis=-1, keepdims=True)


CTX_PAIRS = 4


def _ctx_attn_kernel(q_ref, k_ref, v_ref, qn_ref, kn_ref, o_ref, kc_ref, vc_ref):
    first, ones_col, mean2 = _pair_consts()
    lanes = [slice(p * PAIR, (p + 1) * PAIR) for p in range(CTX_PAIRS)]
    qn = [_pair_norm(q_ref[:, ln], qn_ref[...], mean2) * HD_C ** -0.5 for ln in lanes]
    kn = [_pair_norm(k_ref[:, ln], kn_ref[...], mean2) for ln in lanes]
    v = [v_ref[:, ln] for ln in lanes]
    kt = [x.T for x in kn]
    for p in range(CTX_PAIRS):
        kc_ref[0, 0, lanes[p], :] = kt[p]
        vc_ref[0, 0, lanes[p], :] = v[p].T
    q = [_pair_queries(x, first) for x in qn]
    va = [_pair_values(x, first, ones_col) for x in v]
    s = [[_dot(q[p][j], kt[p].astype(BF16)) for j in range(2)] for p in range(CTX_PAIRS)]
    pr = [[jnp.exp(x - _row_max(x)).astype(BF16) for x in sp] for sp in s]
    for p in range(CTX_PAIRS):
        o_ref[:, lanes[p]] = _pair_output([_dot(pr[p][j], va[p][j]) for j in range(2)], first)


def _ctx_attention(qkv, qn, kn):
    heads = 2 * CTX_PAIRS
    ng = H_C // heads
    wide = CTX_PAIRS * PAIR
    blk = lambda off: pl.BlockSpec((SEQ, wide), lambda b, p: (b, off + p))
    cache_spec = pl.BlockSpec((1, 1, wide, SEQ), lambda b, p: (b, 0, p, 0))
    cache_shape = jax.ShapeDtypeStruct((BATCH, 1, H_C * HD_C, SEQ), F32)
    return pl.pallas_call(
        _ctx_attn_kernel,
        grid=(BATCH, ng),
        in_specs=[blk(0), blk(ng), blk(2 * ng),
                  pl.BlockSpec((1, PAIR), lambda b, p: (0, 0)),
                  pl.BlockSpec((1, PAIR), lambda b, p: (0, 0))],
        out_specs=[pl.BlockSpec((SEQ, wide), lambda b, p: (b, p)), cache_spec, cache_spec],
        out_shape=[jax.ShapeDtypeStruct((N_PROMPT, D_MODEL), F32), cache_shape, cache_shape],
        compiler_params=_cparams("arbitrary", "arbitrary"),
        name="ctx_attention",
    )(qkv, qkv, qkv, jnp.tile(qn.reshape(1, HD_C), (1, 2)), jnp.tile(kn.reshape(1, HD_C), (1, 2)))


def _na_row_start(r):
    return min(max(r - KH // 2, 0), GRID_ROWS - KH)


NA_ROW_GROUP = 4


def _na_attn_kernel(q_ref, k_ref, v_ref, kc_ref, vc_ref, qn_ref, kn_ref, bias_ref, o_ref, qs, ks, vs, bias_s):
    first, ones_col, mean2 = _pair_consts()

    @pl.when(pl.program_id(1) == 0)
    def _():
        q_col = lax.broadcasted_iota(jnp.int32, (GRID_W, PAIR), 0)
        lane = lax.broadcasted_iota(jnp.int32, (GRID_W, PAIR), 1)
        k_col = lane % GRID_W
        w0 = jnp.clip(q_col - KW // 2, 0, GRID_W - KW)
        outside = jnp.where((k_col >= w0) & (k_col < w0 + KW), 0.0, NEG_INF)
        n_dr = 2 * KH - 1
        for j in range(2):
            band = []
            for dr in range(n_dr):
                row = jnp.broadcast_to(bias_ref[j, dr:dr + 1, :], (GRID_W, PAIR))
                band.append([pltpu.roll(row, (half * GRID_W - (KW - 1)) % PAIR, axis=1, stride=1, stride_axis=0)
                             for half in range(2)])
            zero = jnp.zeros((GRID_W, PAIR), F32)
            for cp in range(2):
                for t in range(KH):
                    lo, hi = 2 * t + cp, 2 * t + cp + 1
                    tile = jnp.where(lane < GRID_W, band[lo][0] if lo < n_dr else zero,
                                     band[hi][1] if hi < n_dr else zero)
                    bias_s[j, cp, :, t * PAIR:(t + 1) * PAIR] = tile + outside

    q2 = _pair_queries(_pair_norm(q_ref[...], qn_ref[...], mean2) * HD_C ** -0.5, first)
    v2 = _pair_values(v_ref[...], first, ones_col)
    ks[...] = _pair_norm(k_ref[...], kn_ref[...], mean2).astype(BF16)
    for j in range(2):
        qs[j] = q2[j]
        vs[j] = v2[j]
    kt_ctx = kc_ref[0, 0].astype(BF16)
    vt = vc_ref[0, 0]
    ch = lax.broadcasted_iota(jnp.int32, vt.shape, 0)
    vt_ctx = [jnp.where(ch < HD_C, vt, jnp.where(ch == HD_C, 1.0, 0.0)).astype(BF16),
              jnp.where(ch < HD_C, jnp.where(ch == 0, 1.0, 0.0), vt).astype(BF16)]
    for r0 in range(0, GRID_ROWS, NA_ROW_GROUP):
        units = [(r, j) for r in range(r0, r0 + NA_ROW_GROUP) for j in range(2)]
        rows = {r: slice(r * GRID_W, (r + 1) * GRID_W) for r, _ in units}
        wins = {r: slice(_na_row_start(r) * GRID_W, (_na_row_start(r) + KH) * GRID_W) for r, _ in units}
        s_ctx_all = [_dot(qs[j, r0 * GRID_W:(r0 + NA_ROW_GROUP) * GRID_W, :], kt_ctx) for j in range(2)]
        s_ctx = [s_ctx_all[j][(r - r0) * GRID_W:(r - r0 + 1) * GRID_W] for r, j in units]
        s_win = []
        for r, j in units:
            dr0 = KH - 1 - (r - _na_row_start(r))
            lane0 = (dr0 - dr0 % 2) * GRID_W
            s_win.append(_dot_nt(qs[j, rows[r], :], ks[wins[r], :])
                         + bias_s[j, dr0 % 2, :, lane0:lane0 + KH * GRID_W])
        m = [_row_max(a, b) for a, b in zip(s_win, s_ctx)]
        p_win = [jnp.exp(a - mm).astype(BF16) for a, mm in zip(s_win, m)]
        p_ctx = [jnp.exp(b - mm).astype(BF16) for b, mm in zip(s_ctx, m)]
        o_aug = [_dot(p_win[i], vs[j, wins[r], :]) + _dot_nt(p_ctx[i], vt_ctx[j]) for i, (r, j) in enumerate(units)]
        for i in range(0, len(units), 2):
            o_ref[rows[units[i][0]], :] = _pair_output(o_aug[i:i + 2], first)


NA_BIAS_LANES = 2 * KH * GRID_W


def _na_attention(qkv, cache_kt, cache_vt, qn, kn, rpb):
    nhp = H_C // 2
    row0 = N_PROMPT // DEC_SEQ
    blk = lambda off: pl.BlockSpec((DEC_SEQ, 2 * HD_C), lambda p, b: (row0 + b, off + p))
    cache_spec = pl.BlockSpec((1, 1, PAIR, PAST_LEN), lambda p, b: (b, 0, p, 0))
    rpb_rows = 2 * KH
    bias = jnp.pad(rpb.astype(F32), ((0, 0), (0, rpb_rows - rpb.shape[1]), (0, PAIR - rpb.shape[2])))
    return pl.pallas_call(
        _na_attn_kernel,
        grid=(nhp, DEC_BATCH),
        in_specs=[blk(0), blk(nhp), blk(2 * nhp), cache_spec, cache_spec,
                  pl.BlockSpec((1, PAIR), lambda p, b: (0, 0)),
                  pl.BlockSpec((1, PAIR), lambda p, b: (0, 0)),
                  pl.BlockSpec((2, rpb_rows, PAIR), lambda p, b: (p, 0, 0))],
        out_specs=pl.BlockSpec((DEC_SEQ, 2 * HD_C), lambda p, b: (b, p)),
        out_shape=jax.ShapeDtypeStruct((N_SAMPLE, D_MODEL), F32),
        scratch_shapes=[pltpu.VMEM((2, DEC_SEQ, PAIR), BF16), pltpu.VMEM((DEC_SEQ, PAIR), BF16),
                        pltpu.VMEM((2, DEC_SEQ, PAIR), BF16), pltpu.VMEM((2, 2, GRID_W, NA_BIAS_LANES), F32)],
        compiler_params=_cparams("arbitrary", "arbitrary"),
        name="na_attention",
    )(qkv, qkv, qkv, cache_kt, cache_vt, jnp.tile(qn.reshape(1, HD_C), (1, 2)), jnp.tile(kn.reshape(1, HD_C), (1, 2)),
      bias)


def _seq_layout(prompt):
    return (SEQ, BATCH, 0) if prompt else (DEC_SEQ, DEC_BATCH, N_PROMPT // DEC_SEQ)


def _flip_blocks(m, c):
    r, s = m.shape
    return m.reshape(r // c, c, s // c, c)[:, ::-1, :, ::-1].reshape(r, s)


def _rms_gate(x, gn, gate):
    ms = jnp.mean(x * x, axis=-1, keepdims=True)
    return x * lax.rsqrt(ms + EPS) * gn * _silu(gate)


HG_LEVELS = tuple(CHUNK >> (i + 1) for i in range(CHUNK.bit_length() - 1))
HG_NL = len(HG_LEVELS)
HG_STACK = (HG_NL + 1) * CHUNK
TOT_ROWS = 16
HG_ROWS = (HG_NL + 2) * CHUNK + TOT_ROWS


def _hgrn_consts():
    c = CHUNK
    level_rows = []
    mask = np.zeros((HG_STACK, HG_STACK), np.float32)
    mask[:c, :c] = np.eye(c)
    for li, b in enumerate(HG_LEVELS):
        m = np.zeros((c, c), np.float32)
        blk = np.zeros((c, c), np.float32)
        for t in range(c):
            mid = (t // (2 * b)) * 2 * b + b
            if t >= mid:
                m[t, mid:t + 1] = 1.0
                blk[t, mid - b:mid] = 1.0
            else:
                m[t, t + 1:mid] = 1.0
        level_rows.append(m)
        mask[(li + 1) * c:(li + 2) * c, (li + 1) * c:(li + 2) * c] = blk
    dq = np.tril(np.ones((c, c), np.float32))
    dk = np.triu(np.ones((c, c), np.float32), 1)
    body = np.concatenate(level_rows + [dq, dk], axis=0)
    tot = np.ones((TOT_ROWS, c), np.float32)
    mcs, masks = [], []
    for reverse in (False, True):
        bm = _flip_blocks(body, c) if reverse else body
        mk = _flip_blocks(mask, c) if reverse else mask
        mc = np.concatenate([bm, tot], axis=0)
        mcs.append(np.concatenate([mc, mc], axis=1))
        masks.append(mk)
    return jnp.asarray(np.stack(mcs), BF16), jnp.asarray(np.stack(masks), F32)


HG_FAST = 64
HG_HALF = HG_FAST // 2
HG_FAST_ROWS = 4 * HG_FAST + TOT_ROWS
HG_SAFE_EXP = 40.0
HG_FAST_STEPS = 4


def _hgrn_fast_consts(seq):
    c, m = HG_FAST, HG_HALF
    aq = np.zeros((c, c), np.float32)
    for t in range(c):
        if t >= m:
            aq[t, m:t + 1] = 1.0
        else:
            aq[t, t + 1:m] = -1.0
    dq = np.tril(np.ones((c, c), np.float32))
    dk = np.triu(np.ones((c, c), np.float32), 1)
    body = np.concatenate([aq, -aq, dq, dk], axis=0)
    tot = np.ones((TOT_ROWS, c), np.float32)
    causal = np.tril(np.ones((c, c), np.float32))
    mfs, masks = [], []
    for reverse in (False, True):
        bm = _flip_blocks(body, c) if reverse else body
        mf = np.concatenate([bm, tot], axis=0)
        mfs.append(np.concatenate([mf, mf], axis=1))
        masks.append(causal.T if reverse else causal)
    n_half = seq // m
    half = np.zeros((max(n_half, 16), seq), np.float32)
    for i in range(n_half):
        half[i, i * m:(i + 1) * m] = 1.0
    return jnp.asarray(np.stack(mfs), BF16), jnp.asarray(np.stack(masks), F32), jnp.asarray(half, BF16)


def _hgrn_kernel(*refs, seq, has_s0, emit_state):
    it = iter(refs)
    qa_ref, ff_ref, fb_ref, ia_ref, ga_ref, lb_ref, gn_ref, mc_ref, mask_ref = [next(it) for _ in range(9)]
    mf_ref, causal_ref, half_ref = [next(it) for _ in range(3)]
    s0_ref = next(it) if has_s0 else None
    o_ref = next(it)
    st_ref = next(it) if emit_state else None
    s_scr, acc, f_s, lf_s = [next(it) for _ in range(4)]
    c = CHUNK
    n_chunks = seq // c
    combos = [(d, h) for d in range(2) for h in range(H_A)]
    lanes = [slice(h * DK_A, (h + 1) * DK_A) for h in range(H_A)]
    add = lambda a, b: a + b

    lb_raw = lb_ref[...]
    lb_e = jnp.exp(lb_raw - jnp.max(lb_raw, axis=0, keepdims=True))
    lb_all = lb_e[0:1] / jnp.sum(lb_e, axis=0, keepdims=True)

    for d in range(2):
        for h in range(H_A):
            s_scr[d, h] = s0_ref[0, 0, d, h].T if has_s0 else jnp.zeros((DV_A, DK_A), F32)
    acc[...] = jnp.zeros(acc.shape, F32)

    worst = []
    for d, fr_ref in enumerate((ff_ref, fb_ref)):
        f = lb_all + (1.0 - lb_all) * _sigmoid(fr_ref[...])
        lf = jnp.log(f)
        f_s[d] = f
        lf_s[d] = lf
        worst.append(jnp.max(_dot(half_ref[...], (-lf).astype(BF16))))
    safe = jnp.maximum(worst[0], worst[1]) <= HG_SAFE_EXP

    def fast_body(n, carry):
        cf = HG_FAST
        n_fast = seq // cf
        steps = range(HG_FAST_STEPS)
        chunk = lambda d, t: (n * HG_FAST_STEPS + t) if d == 0 else (n_fast - 1 - n * HG_FAST_STEPS - t)
        rows = [[pl.ds(pl.multiple_of(chunk(d, t) * cf, cf), cf) for t in steps] for d in range(2)]
        units = [(t, d, h) for t in steps for d, h in combos]
        e_all = [[jnp.exp(_dot_const(mf_ref[d], lf_s[d, rows[d][t], :])) for t in steps] for d in range(2)]
        q_all = [[_silu(qa_ref[rows[d][t], :]) * DK_A ** -0.5 for t in steps] for d in range(2)]
        k_all = [[1.0 - f_s[d, rows[d][t], :] for t in steps] for d in range(2)]
        v_all = [[ia_ref[rows[d][t], :].astype(BF16) for t in steps] for d in range(2)]
        qs = {u: q_all[u[1]][u[0]][:, lanes[u[2]]] for u in units}
        ks = {u: k_all[u[1]][u[0]][:, lanes[u[2]]] for u in units}
        vs = {u: v_all[u[1]][u[0]][:, lanes[u[2]]] for u in units}
        es = {u: [e_all[u[1]][u[0]][i * cf:(i + 1) * cf, lanes[u[2]]] for i in range(4)] for u in units}
        p = {u: jnp.where(causal_ref[u[1]] > 0.0,
                          _dot_nt((qs[u] * es[u][0]).astype(BF16), (ks[u] * es[u][1]).astype(BF16)), 0.0).astype(BF16)
             for u in units}
        intra = {u: _dot(p[u], vs[u]) for u in units}
        upd = {u: _dot_tn(vs[u], (ks[u] * es[u][3]).astype(BF16)) for u in units}
        qdec = {u: (qs[u] * es[u][2]).astype(BF16) for u in units}
        st = {(d, h): s_scr[d, h] for d, h in combos}
        o = {}
        for t in steps:
            for d, h in combos:
                u = (t, d, h)
                o[u] = intra[u] + _dot_nt(qdec[u], st[d, h].astype(BF16))
                st[d, h] = st[d, h] * e_all[d][t][4 * cf:4 * cf + 1, lanes[h]] + upd[u]
        for t in steps:
            for d in range(2):
                acc[rows[d][t], :] += jnp.concatenate([o[t, d, h] for h in range(H_A)], axis=1)
        for d, h in combos:
            s_scr[d, h] = st[d, h]
        return carry

    def body(n, carry):
        rows = [pl.ds(pl.multiple_of((n if d == 0 else n_chunks - 1 - n) * c, c), c) for d in range(2)]
        f_all = [f_s[d, rows[d], :] for d in range(2)]
        e_all = [jnp.exp(_dot_const(mc_ref[d], lf_s[d, rows[d], :])) for d in range(2)]
        q_all = [_silu(qa_ref[rows[d], :]) * DK_A ** -0.5 for d in range(2)]
        v_all = [ia_ref[rows[d], :].astype(BF16) for d in range(2)]
        st = [s_scr[d, h] for d, h in combos]
        qs, ks, vs, es = [], [], [], []
        for d, h in combos:
            qs.append(q_all[d][:, lanes[h]])
            ks.append(1.0 - f_all[d][:, lanes[h]])
            vs.append(v_all[d][:, lanes[h]])
            es.append(e_all[d][:, lanes[h]])
        lvl = [[e[i * c:(i + 1) * c] for i in range(HG_NL + 2)] for e in es]
        qst = [jnp.concatenate([q] + [q * l[i] for i in range(HG_NL)], axis=0).astype(BF16) for q, l in zip(qs, lvl)]
        kst = [jnp.concatenate([k] + [k * l[i] for i in range(HG_NL)], axis=0).astype(BF16) for k, l in zip(ks, lvl)]
        r = [(_dot_nt(qst[i], kst[i]) * mask_ref[d]).astype(BF16) for i, (d, h) in enumerate(combos)]
        ost = [_dot(r[i], jnp.concatenate([vs[i]] * (HG_NL + 1), axis=0)) for i in range(len(combos))]
        inter = [_dot_nt((qs[i] * lvl[i][HG_NL]).astype(BF16), st[i].astype(BF16)) for i in range(len(combos))]
        upd = [_dot_tn(vs[i], (ks[i] * lvl[i][HG_NL + 1]).astype(BF16)) for i in range(len(combos))]
        o = [functools.reduce(lambda a, b: a + b, [ost[i][j * c:(j + 1) * c] for j in range(HG_NL + 1)]) + inter[i]
             for i in range(len(combos))]
        for d in range(2):
            acc[rows[d], :] += jnp.concatenate(o[d * H_A:(d + 1) * H_A], axis=1)
        for i, (d, h) in enumerate(combos):
            e_tot = es[i][(HG_NL + 2) * c:(HG_NL + 2) * c + 1]
            s_scr[d, h] = st[i] * e_tot + upd[i]
        return carry

    @pl.when(safe)
    def _():
        lax.fori_loop(0, seq // (HG_FAST * HG_FAST_STEPS), fast_body, 0)

    @pl.when(jnp.logical_not(safe))
    def _():
        lax.fori_loop(0, n_chunks, body, 0)

    for h in range(H_A):
        ln = slice(h * DV_A, (h + 1) * DV_A)
        o_ref[:, ln] = _rms_gate(acc[:, ln], gn_ref[...], ga_ref[:, ln])
    if emit_state:
        for d in range(2):
            for h in range(H_A):
                st_ref[0, 0, d, h] = s_scr[d, h].T


def _hgrn(proj, hgrn_lb, gn, consts, prompt, s0=None, layer=0):
    seq, nb, rb0 = _seq_layout(prompt)
    consts = list(consts) + list(_hgrn_fast_consts(seq))
    wa = H_A * DK_A
    blk = lambda j: pl.BlockSpec((seq, wa), lambda b: (rb0 + b, j))
    const2 = lambda b: (0, 0)
    st_block = (1, 1, 2, H_A, DK_A, DV_A)
    in_specs = [blk(0), blk(1), blk(2), blk(3), blk(4),
                pl.BlockSpec(hgrn_lb.shape, const2), pl.BlockSpec((1, DV_A), const2)]
    in_specs += [pl.BlockSpec(m.shape, lambda b, nd=m.ndim: (0,) * nd) for m in consts]
    args = [proj] * 5 + [hgrn_lb, gn.reshape(1, DV_A)] + consts
    if s0 is not None:
        in_specs.append(pl.BlockSpec(st_block, lambda b: (b, layer, 0, 0, 0, 0)))
        args.append(s0)
    out_specs = [pl.BlockSpec((seq, wa), lambda b: (b, 0))]
    out_shape = [jax.ShapeDtypeStruct((nb * seq, wa), F32)]
    if prompt:
        out_specs.append(pl.BlockSpec(st_block, lambda b: (b, 0, 0, 0, 0, 0)))
        out_shape.append(jax.ShapeDtypeStruct((nb, 1, 2, H_A, DK_A, DV_A), F32))
    return pl.pallas_call(
        functools.partial(_hgrn_kernel, seq=seq, has_s0=s0 is not None, emit_state=prompt),
        grid=(nb,),
        in_specs=in_specs,
        out_specs=out_specs,
        out_shape=out_shape,
        scratch_shapes=[pltpu.VMEM((2, H_A, DV_A, DK_A), F32), pltpu.VMEM((seq, wa), F32),
                        pltpu.VMEM((2, seq, wa), F32), pltpu.VMEM((2, seq, wa), F32)],
        compiler_params=_cparams("arbitrary"),
        name="hgrn_prompt" if prompt else "hgrn_sample",
    )(*args)


GD_SUB = GBLK // CHUNK
GD_BLOCKS_PER_ITER = 2
GD_ROWS = 2 * GBLK + TOT_ROWS


def _gdn_consts():
    n, c = GBLK, CHUNK
    same = (np.arange(n)[:, None] // c) == (np.arange(n)[None, :] // c)
    tri = (same & (np.arange(n)[None, :] <= np.arange(n)[:, None])).astype(np.float32)
    sup = (same & (np.arange(n)[None, :] > np.arange(n)[:, None])).astype(np.float32)
    tot = np.zeros((TOT_ROWS, n), np.float32)
    for s in range(GD_SUB):
        tot[s, s * c:(s + 1) * c] = 1.0
    mgs, tts, tris = [], [], []
    for reverse in (False, True):
        t = _flip_blocks(tri, c) if reverse else tri
        s = _flip_blocks(sup, c) if reverse else sup
        mg = np.concatenate([t, s, tot], axis=0)
        mgs.append(np.concatenate([mg, mg], axis=1))
        tts.append(np.concatenate([t.T, t.T], axis=0))
        tris.append(t)
    tris.append(same.astype(np.float32))
    return jnp.asarray(np.stack(mgs), BF16), jnp.asarray(np.stack(tts), BF16), jnp.asarray(np.stack(tris), F32)


def _softplus(x):
    return jnp.maximum(x, 0.0) + jnp.log(1.0 + jnp.exp(-jnp.abs(x)))


CONV_PAD = 8


def _conv_silu(x, w, seq):
    half = SHORT_CONV // 2
    pad = jnp.zeros((CONV_PAD, x.shape[1]), x.dtype)
    xe = jnp.concatenate([pad, x, pad], axis=0)
    acc = xe * w[half:half + 1]
    for j in range(SHORT_CONV):
        shift = half - j
        if shift != 0:
            acc = acc + pltpu.roll(xe, shift % (seq + 2 * CONV_PAD), axis=0) * w[j:j + 1]
    return _silu(acc[CONV_PAD:seq + CONV_PAD])


def _l2norm_heads(x, n_heads, width, scale):
    outs = []
    for h in range(n_heads):
        xh = x[:, h * width:(h + 1) * width]
        outs.append(xh * (lax.rsqrt(jnp.sum(xh * xh, axis=-1, keepdims=True) + EPS) * scale))
    return jnp.concatenate(outs, axis=-1)


def _gdn_kernel(*refs, seq, has_s0, emit_state):
    it = iter(refs)
    (q_ref, k_ref, v_ref, gb_ref, gate_ref, cw_ref, alog_ref, dt_ref, gn_ref,
     mg_ref, tt_ref, tri_ref) = [next(it) for _ in range(12)]
    s0_ref = next(it) if has_s0 else None
    o_ref = next(it)
    st_ref = next(it) if emit_state else None
    qn, kn, vn, u_s, w_s, qg_s, kdt_s, at_s, et_s, s_scr, acc = [next(it) for _ in range(11)]
    c = CHUNK
    n_chunks = seq // c
    n_blocks = seq // GBLK
    wq = H_B * DK_B
    n_dh = 2 * H_B
    combos = [(d, h) for d in range(2) for h in range(H_B)]
    lanes = [slice(h * DK_B, (h + 1) * DK_B) for h in range(H_B)]

    qn[...] = _l2norm_heads(_conv_silu(q_ref[...], cw_ref[:, 0:wq], seq), H_B, DK_B, DK_B ** -0.5)
    kn[...] = _l2norm_heads(_conv_silu(k_ref[...], cw_ref[:, wq:2 * wq], seq), H_B, DK_B, 1.0)
    vn[...] = _conv_silu(v_ref[...], cw_ref[:, 2 * wq:3 * wq], seq)
    for i in range(2 * H_B):
        s_scr[i] = s0_ref[0, 0, i // H_B, i % H_B] if has_s0 else jnp.zeros((DK_B, DV_B), F32)
    acc[...] = jnp.zeros(acc.shape, F32)

    eye = (lax.broadcasted_iota(jnp.int32, (GBLK, GBLK), 0)
           == lax.broadcasted_iota(jnp.int32, (GBLK, GBLK), 1)).astype(F32)
    eye_pk = (lax.broadcasted_iota(jnp.int32, (c, GBLK), 0)
              == lax.broadcasted_iota(jnp.int32, (c, GBLK), 1) % c).astype(F32)
    bwd_lane = lax.broadcasted_iota(jnp.int32, (1, 128), 1) % n_dh >= H_B
    add = lambda a, b: a + b

    same_chunk = tri_ref[2].astype(BF16)

    def expand(pk):
        return jnp.concatenate([pk] * GD_SUB, axis=0) * same_chunk

    def pack(bd):
        return functools.reduce(add, [bd[s * c:(s + 1) * c] for s in range(GD_SUB)])

    def weights(hi, lo):
        return jnp.concatenate([expand(hi), expand(lo)], axis=1)

    def dot3_split(a_hi, a_lo, w2):
        m, n = a_hi.shape[0], w2.shape[1] // 2
        t = _dot(jnp.concatenate([a_hi, a_lo], axis=0), w2)
        return t[:m, :n] + t[m:, :n] + t[:m, n:]

    def block_body(it, carry):
        blks = [it * GD_BLOCKS_PER_ITER + o for o in range(GD_BLOCKS_PER_ITER)]
        units = [(o, d, h) for o in range(GD_BLOCKS_PER_ITER) for d, h in combos]
        idx = lambda d, h: d * H_B + h
        col = lambda x, j: jnp.broadcast_to(x[:, j:j + 1], (GBLK, DK_B))
        rows = [pl.ds(pl.multiple_of(b * GBLK, GBLK), GBLK) for b in blks]
        gates = [gate_ref[r, :] for r in rows]
        glog_all = [-jnp.exp(alog_ref[...]) * _softplus(x + dt_ref[...]) for x in gates]
        beta_all = [_sigmoid(x) for x in gates]
        g2 = [jnp.concatenate(_split2(x), axis=0) for x in glog_all]
        dg = [[_dot(mg_ref[d], x) for d in range(2)] for x in g2]
        dsel = [jnp.where(bwd_lane, x[1], x[0]) for x in dg]
        eg_all = [jnp.exp(x) for x in dsel]
        gt = [[_dot_tn(x, tt_ref[d]) for d in range(2)] for x in g2]
        qs = [[qn[r, ln] for ln in lanes] for r in rows]
        ks = [[kn[r, ln] for ln in lanes] for r in rows]
        vs = [[vn[r, ln] for ln in lanes] for r in rows]
        betas = [col(beta_all[o], n_dh + idx(d, h)) for o, d, h in units]
        kbs = [ks[o][h] * betas[u] for u, (o, d, h) in enumerate(units)]
        kb_of = {unit: kbs[u] for u, unit in enumerate(units)}
        kk = {(o, h): _dot_nt(jnp.concatenate([qs[o][h], kb_of[o, 0, h], kb_of[o, 1, h]], axis=0).astype(BF16),
                              ks[o][h].astype(BF16))
              for o in range(GD_BLOCKS_PER_ITER) for h in range(H_B)}
        decay = []
        for o, d, h in units:
            inside = tri_ref[d] > 0.0
            gd = col(dsel[o][:GBLK], idx(d, h)) - gt[o][d][idx(d, h):idx(d, h) + 1, :]
            decay.append(jnp.where(inside, jnp.exp(jnp.where(inside, gd, 0.0)), 0.0))
        attn = [kk[o, h][:GBLK] * decay[u] for u, (o, d, h) in enumerate(units)]
        p_pk = [pack(kk[o, h][(1 + d) * GBLK:(2 + d) * GBLK] * decay[u] * (1.0 - eye))
                for u, (o, d, h) in enumerate(units)]
        x_pk = [eye_pk - p for p in p_pk]
        p_sp = [_split2(p) for p in p_pk]
        p_w = [weights(*s) for s in p_sp]
        for _ in range(CHUNK.bit_length() - 2):
            p_pk = [dot3_split(*s, w) for s, w in zip(p_sp, p_w)]
            p_sp = [_split2(p) for p in p_pk]
            p_w = [weights(*s) for s in p_sp]
            x_pk = [x + dot3_split(*_split2(x), w) for x, w in zip(x_pk, p_w)]
        eg_col = [col(eg_all[o][:GBLK], idx(d, h)) for o, d, h in units]
        ekd_col = [col(eg_all[o][GBLK:2 * GBLK], idx(d, h)) for o, d, h in units]
        rhs = [_split2(jnp.concatenate([vs[o][h] * betas[u], kbs[u] * eg_col[u]], axis=1))
               for u, (o, d, h) in enumerate(units)]
        t_sp = [[expand(part) for part in _split2(x)] for x in x_pk]
        uw = [_dot(t[0], r[0]) + (_dot(t[0], r[1]) + _dot(t[1], r[0])) for t, r in zip(t_sp, rhs)]
        for u, (o, d, h) in enumerate(units):
            i = idx(d, h)
            qg = (qs[o][h] * eg_col[u]).astype(BF16)
            kdt_s[i, blks[o]] = (ks[o][h] * ekd_col[u]).T.astype(BF16)
            for s in range(GD_SUB):
                cn = blks[o] * GD_SUB + s
                r = slice(s * c, (s + 1) * c)
                u_s[i, cn] = uw[u][r, :DV_B]
                w_s[i, cn] = uw[u][r, DV_B:].astype(BF16)
                qg_s[i, cn] = qg[r]
                at_s[i, cn] = attn[u][r].astype(BF16)
                et_s[i, cn] = jnp.broadcast_to(eg_all[o][2 * GBLK + s:2 * GBLK + s + 1, i:i + 1], (8, DV_B))
        return carry

    lax.fori_loop(0, n_blocks // GD_BLOCKS_PER_ITER, block_body, 0)

    def chunk_body(n, carry):
        cns = [n, n_chunks - 1 - n]
        rows = [pl.ds(pl.multiple_of(cn * c, c), c) for cn in cns]
        sub_of_row = lax.broadcasted_iota(jnp.int32, (GBLK, 1), 0) // c
        in_chunk = [sub_of_row == cn % GD_SUB for cn in cns]
        st = [s_scr[i] for i in range(n_dh)]
        ws = [_dot(jnp.concatenate([w_s[i, cns[d]], qg_s[i, cns[d]]], axis=0), st[i].astype(BF16))
              for i, (d, h) in enumerate(combos)]
        vblk = [jnp.where(in_chunk[d], jnp.concatenate([u_s[i, cns[d]] - ws[i][:c]] * GD_SUB, axis=0), 0.0).astype(BF16)
                for i, (d, h) in enumerate(combos)]
        r = [_dot(jnp.concatenate([at_s[i, cns[d]], kdt_s[i, cns[d] // GD_SUB]], axis=0), vblk[i])
             for i, (d, h) in enumerate(combos)]
        for d in range(2):
            acc[rows[d], :] += jnp.concatenate([ws[i][c:] + r[i][:c] for i in range(d * H_B, (d + 1) * H_B)], axis=1)
        for i, (d, h) in enumerate(combos):
            s_scr[i] = st[i] * et_s[i, cns[d]][0:1] + r[i][c:]
        return carry

    lax.fori_loop(0, n_chunks, chunk_body, 0)

    for h in range(H_B):
        ln = slice(h * DV_B, (h + 1) * DV_B)
        o_ref[:, ln] = _rms_gate(acc[:, ln], gn_ref[...], gb_ref[:, ln])
    if emit_state:
        for i in range(2 * H_B):
            st_ref[0, 0, i // H_B, i % H_B] = s_scr[i]


def _gdn(proj, gates, conv_w, a_log, dt_bias, gn, consts, prompt, s0=None, layer=0):
    seq, nb, rb0 = _seq_layout(prompt)
    n_chunks = seq // CHUNK
    wq = H_B * DK_B
    blk = lambda j: pl.BlockSpec((seq, wq), lambda b: (rb0 + b, j))
    const2 = lambda b: (0, 0)
    const3 = lambda b: (0, 0, 0)
    st_block = (1, 1, 2, H_B, DK_B, DV_B)
    pad_row = lambda p: jnp.pad(p.reshape(1, -1).astype(F32), ((0, 0), (0, 128 - p.size)))
    in_specs = [blk(5), blk(6), blk(7), blk(8),
                pl.BlockSpec((seq, 128), lambda b: (rb0 + b, 0)),
                pl.BlockSpec((SHORT_CONV, 3 * wq), const2),
                pl.BlockSpec((1, 128), const2), pl.BlockSpec((1, 128), const2), pl.BlockSpec((1, DV_B), const2)]
    in_specs += [pl.BlockSpec(m.shape, const3) for m in consts]
    args = [proj] * 4 + [gates, conv_w.reshape(SHORT_CONV, 3 * wq), pad_row(a_log), pad_row(dt_bias),
                         gn.reshape(1, DV_B)] + list(consts)
    if s0 is not None:
        in_specs.append(pl.BlockSpec(st_block, lambda b: (b, layer, 0, 0, 0, 0)))
        args.append(s0)
    out_specs = [pl.BlockSpec((seq, wq), lambda b: (b, 0))]
    out_shape = [jax.ShapeDtypeStruct((nb * seq, wq), F32)]
    if prompt:
        out_specs.append(pl.BlockSpec(st_block, lambda b: (b, 0, 0, 0, 0, 0)))
        out_shape.append(jax.ShapeDtypeStruct((nb, 1, 2, H_B, DK_B, DV_B), F32))
    n_dh = 2 * H_B
    scratch = ([pltpu.VMEM((seq, wq), F32)] * 3
               + [pltpu.VMEM((n_dh, n_chunks, CHUNK, DV_B), F32)]
               + [pltpu.VMEM((n_dh, n_chunks, CHUNK, DK_B), BF16)] * 2
               + [pltpu.VMEM((n_dh, seq // GBLK, DK_B, GBLK), BF16),
                  pltpu.VMEM((n_dh, n_chunks, CHUNK, GBLK), BF16),
                  pltpu.VMEM((n_dh, n_chunks, 8, DV_B), F32),
                  pltpu.VMEM((n_dh, DK_B, DV_B), F32),
                  pltpu.VMEM((seq, wq), F32)])
    return pl.pallas_call(
        functools.partial(_gdn_kernel, seq=seq, has_s0=s0 is not None, emit_state=prompt),
        grid=(nb,),
        in_specs=in_specs,
        out_specs=out_specs,
        out_shape=out_shape,
        scratch_shapes=scratch,
        compiler_params=_cparams("arbitrary"),
        name="gdn_prompt" if prompt else "gdn_sample",
    )(*args)


def kernel(x_prompt, x_sample, state_hgrn, state_gdn, cache_na_k, cache_na_v, c, c_ctx, ada_w, ada_b, norm_g, w_in_ab, w_out_ab, hgrn_lb, gdn_conv, gdn_a_log, gdn_dt_bias, gn_hgrn, gn_gdn, w_qkv_na, qn_na, kn_na, rpb_na, w_out_na, w_mlp1, w_mlp2):
    cond = jnp.concatenate([c_ctx[None, :], c, jnp.zeros((N_MOD_ROWS - 1 - DEC_BATCH, D_MODEL), F32)], axis=0)
    mods, w_in = _modulation(cond, ada_w, ada_b, w_in_ab)
    xs = (x_prompt.reshape(N_PROMPT, D_MODEL), x_sample.reshape(N_SAMPLE, D_MODEL))

    w_gate = jnp.pad(w_in[:, D_MAIN_AB:], ((0, 0), (0, 128 - N_GATE_AB)))
    proj, gates, wo0, w1_0, w2_0 = _norm_proj(xs, mods[0], norm_g[0, 0], [w_in, w_gate], widths=[D_MAIN_AB, 128],
                                              side=((w_out_ab, 0), (w_mlp1, 0), (w_mlp2, 0)))
    hg_consts = _hgrn_consts()
    gd_consts = _gdn_consts()
    hg_prompt, new_hgrn = _hgrn(proj, hgrn_lb, gn_hgrn[0], hg_consts, True)
    hg_sample, = _hgrn(proj, hgrn_lb, gn_hgrn[0], hg_consts, False, s0=state_hgrn)
    gd_args = (gdn_conv[0], gdn_a_log[0], gdn_dt_bias[0], gn_gdn[0], gd_consts)
    gd_prompt, new_gdn = _gdn(proj, gates, *gd_args, True)
    gd_sample, = _gdn(proj, gates, *gd_args, False, s0=state_gdn)
    xs, (w_qkv, wo1, w1_1, w2_1) = _post_mixer(xs, [(hg_prompt, hg_sample), (gd_prompt, gd_sample)], mods[0],
                                               norm_g[0, 1], wo0, w1_0, w2_0,
                                               side=((w_qkv_na, 0), (w_out_na, 0), (w_mlp1, 1), (w_mlp2, 1)))

    qkv, = _norm_proj(xs, mods[1], norm_g[1, 0], [w_qkv])
    at_prompt, new_kt, new_vt = _ctx_attention(qkv, qn_na[0], kn_na[0])
    time_minor = lambda a: jnp.swapaxes(a, -1, -2).reshape(a.shape[0], 1, H_C * HD_C, a.shape[3])
    at_sample = _na_attention(qkv, time_minor(cache_na_k), time_minor(cache_na_v), qn_na[0], kn_na[0], rpb_na[0])
    time_major = lambda a: jnp.swapaxes(a.reshape(BATCH, 1, H_C, HD_C, SEQ), -1, -2)
    new_k, new_v = time_major(new_kt), time_major(new_vt)
    (y_prompt, y_sample), _ = _post_mixer(xs, [(at_prompt, at_sample)], mods[1], norm_g[1, 1], wo1, w1_1, w2_1,
                                          split_out=True)

    return (y_prompt.reshape(BATCH, SEQ, D_MODEL), y_sample.reshape(DEC_BATCH, DEC_SEQ, D_MODEL),
            new_hgrn, new_gdn, new_k, new_v)
```

```python
import functools

import numpy as np
import jax
import jax.numpy as jnp
from jax import lax
from jax.experimental import pallas as pl
from jax.experimental.pallas import tpu as pltpu

F32 = jnp.float32
BF16 = jnp.bfloat16

D_MODEL = 1024
BATCH = 16
SEQ = 256
DEC_BATCH = 4
DEC_SEQ = 1024
PAST_LEN = 256
N_PROMPT = BATCH * SEQ
N_SAMPLE = DEC_BATCH * DEC_SEQ
N_TOK = N_PROMPT + N_SAMPLE
GRID_W = 64
GRID_ROWS = DEC_SEQ // GRID_W
H_A = 4
DK_A = 128
DV_A = 128
H_B = 4
DK_B = 128
DV_B = 128
SHORT_CONV = 5
H_C = 16
HD_C = 64
KH = 8
KW = 16
D_FF = 4 * D_MODEL
EPS = 1e-6
NEG_INF = -1e30
N_MOD_ROWS = 8
D_MAIN_AB = 4608
N_GATE_AB = 16
CHUNK = 32
GBLK = 128
VMEM_LIMIT = 56 * 1024 * 1024


def _cparams(*sem):
    return pltpu.CompilerParams(dimension_semantics=sem, vmem_limit_bytes=VMEM_LIMIT)


def _sigmoid(x):
    return 0.5 * jnp.tanh(0.5 * x) + 0.5


def _silu(x):
    return x * _sigmoid(x)


def _dot(a, b):
    return jnp.dot(a, b, preferred_element_type=F32)


def _dot_nt(a, b):
    return lax.dot_general(a, b, (((1,), (1,)), ((), ())), preferred_element_type=F32)


def _dot_tn(a, b):
    return lax.dot_general(a, b, (((0,), (0,)), ((), ())), preferred_element_type=F32)


def _split2(x):
    hi = x.astype(BF16)
    lo = (x - hi.astype(F32)).astype(BF16)
    return hi, lo


def _dot_const(m2, x):
    hi, lo = _split2(x)
    return _dot(m2, jnp.concatenate([hi, lo], axis=0))


def _dot3(a, b):
    ah, al = _split2(a)
    bh, bl = _split2(b)
    return _dot(ah, bh) + (_dot(ah, bl) + _dot(al, bh))


def _mod_row(i, tm):
    start = i * tm
    return jnp.where(start < N_PROMPT, 0, 1 + (start - N_PROMPT) // DEC_SEQ)


def _mod_slice(mod_ref, row, k):
    return mod_ref[pl.ds(row, 1), k * D_MODEL:(k + 1) * D_MODEL]


def _norm_mod(x, g, sc, sh):
    ms = jnp.mean(x * x, axis=-1, keepdims=True)
    return (x * lax.rsqrt(ms + EPS) * g) * (1.0 + sc) + sh


def _mod_kernel(cond_ref, w_ref, b_ref, side_ref, o_ref, side_o_ref):
    s = _silu(cond_ref[...]).astype(BF16)
    o_ref[0] = _dot(s, w_ref[0].astype(BF16)) + b_ref[0]
    side_o_ref[...] = side_ref[...].astype(BF16)


MOD_TN = 768


def _modulation(cond8, ada_w, ada_b, side):
    depth = ada_w.shape[0]
    nj = ada_w.shape[2] // MOD_TN
    slab = -(-side.shape[1] // (depth * nj * 16)) * 16
    return pl.pallas_call(
        _mod_kernel,
        grid=(depth, nj),
        in_specs=[
            pl.BlockSpec((N_MOD_ROWS, D_MODEL), lambda l, j: (0, 0)),
            pl.BlockSpec((1, D_MODEL, MOD_TN), lambda l, j: (l, 0, j)),
            pl.BlockSpec((1, 1, MOD_TN), lambda l, j: (l, 0, j)),
            pl.BlockSpec((None, slab, side.shape[2]), lambda l, j: (0, l * nj + j, 0)),
        ],
        out_specs=[pl.BlockSpec((1, N_MOD_ROWS, MOD_TN), lambda l, j: (l, 0, j)),
                   pl.BlockSpec((slab, side.shape[2]), lambda l, j: (l * nj + j, 0))],
        out_shape=[jax.ShapeDtypeStruct((depth, N_MOD_ROWS, ada_w.shape[2]), F32),
                   jax.ShapeDtypeStruct(side.shape[1:], BF16)],
        compiler_params=_cparams("arbitrary", "arbitrary"),
        name="modulation",
    )(cond8, ada_w, ada_b.reshape(depth, 1, -1), side)


def _stream_specs(n_arrays, tm, width=D_MODEL):
    if n_arrays == 1:
        return [pl.BlockSpec((tm, width), lambda i: (i, 0))]
    npt = N_PROMPT // tm
    return [pl.BlockSpec((tm, width), lambda i: (jnp.minimum(i, npt - 1), 0)),
            pl.BlockSpec((tm, width), lambda i: (jnp.maximum(i - npt, 0), 0))]


def _stream_load(x_refs, tm):
    if len(x_refs) == 1:
        return x_refs[0][...]
    return jnp.where(pl.program_id(0) < N_PROMPT // tm, x_refs[0][...], x_refs[1][...])


def _side_specs(side, n_steps):
    in_specs = [pl.BlockSpec((None, w.shape[1] // n_steps, w.shape[2]), lambda i, l=l: (l, i, 0)) for w, l in side]
    out_specs = [pl.BlockSpec((w.shape[1] // n_steps, w.shape[2]), lambda i: (i, 0)) for w, _ in side]
    shapes = [jax.ShapeDtypeStruct(w.shape[1:], BF16) for w, _ in side]
    return in_specs, out_specs, shapes


def _side_cast(in_refs, out_refs):
    for i_ref, o_ref in zip(in_refs, out_refs):
        o_ref[...] = i_ref[...].astype(BF16)


def _norm_proj_kernel(*refs, tm, n_x, n_w, n_side, w_transposed):
    x_refs, (mod_ref, g_ref) = refs[:n_x], refs[n_x:n_x + 2]
    w_refs = refs[n_x + 2:n_x + 2 + n_w]
    side_in = refs[n_x + 2 + n_w:n_x + 2 + n_w + n_side]
    o_refs = refs[n_x + 2 + n_w + n_side:n_x + 2 + 2 * n_w + n_side]
    side_out = refs[n_x + 2 + 2 * n_w + n_side:]
    row = _mod_row(pl.program_id(0), tm)
    h = _norm_mod(_stream_load(x_refs, tm), g_ref[...], _mod_slice(mod_ref, row, 1), _mod_slice(mod_ref, row, 0)).astype(BF16)
    for w_ref, o_ref in zip(w_refs, o_refs):
        o_ref[...] = _dot_nt(h, w_ref[...]) if w_transposed else _dot(h, w_ref[...])
    _side_cast(side_in, side_out)


def _norm_proj(xs, mod, g, ws, widths=None, side=(), w_transposed=False, tm=512):
    n_w = len(ws)
    widths = widths or [w.shape[0 if w_transposed else 1] for w in ws]
    const = lambda i: (0, 0)
    w_block = (lambda n: (n, D_MODEL)) if w_transposed else (lambda n: (D_MODEL, n))
    side_in_specs, side_out_specs, side_shapes = _side_specs(side, N_TOK // tm)
    return pl.pallas_call(
        functools.partial(_norm_proj_kernel, tm=tm, n_x=len(xs), n_w=n_w, n_side=len(side), w_transposed=w_transposed),
        grid=(N_TOK // tm,),
        in_specs=_stream_specs(len(xs), tm) + [
            pl.BlockSpec(mod.shape, const),
            pl.BlockSpec((1, D_MODEL), const),
        ] + [pl.BlockSpec(w_block(n), const, pipeline_mode=pl.Buffered(1)) for n in widths] + side_in_specs,
        out_specs=[pl.BlockSpec((tm, n), lambda i: (i, 0)) for n in widths] + side_out_specs,
        out_shape=[jax.ShapeDtypeStruct((N_TOK, n), F32) for n in widths] + side_shapes,
        compiler_params=_cparams("arbitrary"),
        name="norm_proj",
    )(*xs, mod, g.reshape(1, D_MODEL), *ws, *[w for w, _ in side])


def _post_kernel(*refs, tm, ff_chunk, n_x, n_y, groups, n_side):
    x_refs = refs[:n_x]
    n_m = sum(n for n, _ in groups)
    m_refs = refs[n_x:n_x + n_m]
    mod_ref, g_ref, wo_ref, w1_ref, w2_ref = refs[n_x + n_m:n_x + n_m + 5]
    side_in = refs[n_x + n_m + 5:n_x + n_m + 5 + n_side]
    y_refs = refs[n_x + n_m + 5 + n_side:n_x + n_m + 5 + n_side + n_y]
    _side_cast(side_in, refs[n_x + n_m + 5 + n_side + n_y:])
    row = _mod_row(pl.program_id(0), tm)
    mix, first_ref, first_col = None, 0, 0
    for n, width in groups:
        part = _stream_load(m_refs[first_ref:first_ref + n], tm).astype(BF16)
        term = _dot(part, wo_ref[first_col:first_col + width, :])
        mix = term if mix is None else mix + term
        first_ref, first_col = first_ref + n, first_col + width
    x1 = _stream_load(x_refs, tm) + _mod_slice(mod_ref, row, 2) * mix
    h = _norm_mod(x1, g_ref[...], _mod_slice(mod_ref, row, 4), _mod_slice(mod_ref, row, 3)).astype(BF16)
    acc = jnp.zeros((tm, D_MODEL), F32)
    for k in range(0, D_FF, ff_chunk):
        a = jnp.maximum(_dot(h, w1_ref[:, k:k + ff_chunk]), 0.0)
        acc = acc + _dot((a * a).astype(BF16), w2_ref[k:k + ff_chunk, :])
    y = x1 + _mod_slice(mod_ref, row, 5) * acc
    if n_y == 1:
        y_refs[0][...] = y
    else:
        is_prompt = pl.program_id(0) < N_PROMPT // tm

        @pl.when(is_prompt)
        def _():
            y_refs[0][...] = y

        @pl.when(jnp.logical_not(is_prompt))
        def _():
            y_refs[1][...] = y


def _post_mixer(xs, mixed, mod, g, wo, w1, w2, split_out=False, side=(), tm=512, ff_chunk=1024):
    const = lambda i: (0, 0)
    resident = lambda w: pl.BlockSpec(w.shape, const, pipeline_mode=pl.Buffered(1))
    n_y = 2 if split_out else 1
    rows = (N_PROMPT, N_SAMPLE) if split_out else (N_TOK,)
    groups = tuple((len(grp), grp[0].shape[1]) for grp in mixed)
    mixed_specs = [spec for n, width in groups for spec in _stream_specs(n, tm, width)]
    side_in_specs, side_out_specs, side_shapes = _side_specs(side, N_TOK // tm)
    out = pl.pallas_call(
        functools.partial(_post_kernel, tm=tm, ff_chunk=ff_chunk, n_x=len(xs), n_y=n_y, groups=groups,
                          n_side=len(side)),
        grid=(N_TOK // tm,),
        in_specs=_stream_specs(len(xs), tm) + mixed_specs + [
            pl.BlockSpec(mod.shape, const),
            pl.BlockSpec((1, D_MODEL), const),
            resident(wo), resident(w1), resident(w2),
        ] + side_in_specs,
        out_specs=_stream_specs(n_y, tm) + side_out_specs,
        out_shape=[jax.ShapeDtypeStruct((r, D_MODEL), F32) for r in rows] + side_shapes,
        compiler_params=_cparams("arbitrary"),
        name="post_mixer",
    )(*xs, *[a for grp in mixed for a in grp], mod, g.reshape(1, D_MODEL), wo, w1, w2, *[w for w, _ in side])
    return tuple(out[:n_y]), tuple(out[n_y:])


PAIR = 2 * HD_C


def _pair_consts():
    lane = lax.broadcasted_iota(jnp.int32, (1, PAIR), 1)
    first = lane < HD_C
    ones_col = [jnp.where(lane == HD_C, 1.0, 0.0), jnp.where(lane == 0, 1.0, 0.0)]
    r = lax.broadcasted_iota(jnp.int32, (2 * PAIR, PAIR), 0) % PAIR
    cidx = lax.broadcasted_iota(jnp.int32, (2 * PAIR, PAIR), 1)
    mean2 = jnp.where(r // HD_C == cidx // HD_C, 1.0 / HD_C, 0.0).astype(BF16)
    return first, ones_col, mean2


def _pair_norm(x, w2, mean2):
    hi, lo = _split2(x * x)
    ms = _dot(jnp.concatenate([hi, lo], axis=1), mean2)
    return x * lax.rsqrt(ms + EPS) * w2


def _pair_queries(q, first):
    return [jnp.where(first, q, 0.0).astype(BF16), jnp.where(first, 0.0, q).astype(BF16)]


def _pair_values(v, first, ones_col):
    return [jnp.where(first, v, ones_col[0]).astype(BF16), jnp.where(first, ones_col[1], v).astype(BF16)]


def _pair_output(o_aug, first):
    den = [o_aug[0][:, HD_C:HD_C + 1], o_aug[1][:, 0:1]]
    return jnp.where(first, o_aug[0] / den[0], o_aug[1] / den[1])


def _row_max(*pieces):
    tiles = [p[:, i:i + 128] for p in pieces for i in range(0, p.shape[1], 128)]
    return jnp.max(functools.reduce(jnp.maximum, tiles), axis=-1, keepdims=True)


CTX_PAIRS = 4


def _ctx_attn_kernel(q_ref, k_ref, v_ref, qn_ref, kn_ref, o_ref, kc_ref, vc_ref):
    first, ones_col, mean2 = _pair_consts()
    lanes = [slice(p * PAIR, (p + 1) * PAIR) for p in range(CTX_PAIRS)]
    qn = [_pair_norm(q_ref[:, ln], qn_ref[...], mean2) * HD_C ** -0.5 for ln in lanes]
    kn = [_pair_norm(k_ref[:, ln], kn_ref[...], mean2) for ln in lanes]
    v = [v_ref[:, ln] for ln in lanes]
    kt = [x.T for x in kn]
    for p in range(CTX_PAIRS):
        kc_ref[0, 0, lanes[p], :] = kt[p]
        vc_ref[0, 0, lanes[p], :] = v[p].T
    q = [_pair_queries(x, first) for x in qn]
    va = [_pair_values(x, first, ones_col) for x in v]
    s = [[_dot(q[p][j], kt[p].astype(BF16)) for j in range(2)] for p in range(CTX_PAIRS)]
    pr = [[jnp.exp(x - _row_max(x)).astype(BF16) for x in sp] for sp in s]
    for p in range(CTX_PAIRS):
        o_ref[:, lanes[p]] = _pair_output([_dot(pr[p][j], va[p][j]) for j in range(2)], first)


def _ctx_attention(qkv, qn, kn):
    heads = 2 * CTX_PAIRS
    ng = H_C // heads
    wide = CTX_PAIRS * PAIR
    blk = lambda off: pl.BlockSpec((SEQ, wide), lambda b, p: (b, off + p))
    cache_spec = pl.BlockSpec((1, 1, wide, SEQ), lambda b, p: (b, 0, p, 0))
    cache_shape = jax.ShapeDtypeStruct((BATCH, 1, H_C * HD_C, SEQ), F32)
    return pl.pallas_call(
        _ctx_attn_kernel,
        grid=(BATCH, ng),
        in_specs=[blk(0), blk(ng), blk(2 * ng),
                  pl.BlockSpec((1, PAIR), lambda b, p: (0, 0)),
                  pl.BlockSpec((1, PAIR), lambda b, p: (0, 0))],
        out_specs=[pl.BlockSpec((SEQ, wide), lambda b, p: (b, p)), cache_spec, cache_spec],
        out_shape=[jax.ShapeDtypeStruct((N_PROMPT, D_MODEL), F32), cache_shape, cache_shape],
        compiler_params=_cparams("arbitrary", "arbitrary"),
        name="ctx_attention",
    )(qkv, qkv, qkv, jnp.tile(qn.reshape(1, HD_C), (1, 2)), jnp.tile(kn.reshape(1, HD_C), (1, 2)))


def _na_row_start(r):
    return min(max(r - KH // 2, 0), GRID_ROWS - KH)


NA_ROW_GROUP = 4


def _na_attn_kernel(q_ref, k_ref, v_ref, kc_ref, vc_ref, qn_ref, kn_ref, bias_ref, o_ref, qs, ks, vs, bias_s):
    first, ones_col, mean2 = _pair_consts()

    @pl.when(pl.program_id(1) == 0)
    def _():
        q_col = lax.broadcasted_iota(jnp.int32, (GRID_W, PAIR), 0)
        lane = lax.broadcasted_iota(jnp.int32, (GRID_W, PAIR), 1)
        k_col = lane % GRID_W
        w0 = jnp.clip(q_col - KW // 2, 0, GRID_W - KW)
        outside = jnp.where((k_col >= w0) & (k_col < w0 + KW), 0.0, NEG_INF)
        n_dr = 2 * KH - 1
        for j in range(2):
            band = []
            for dr in range(n_dr):
                row = jnp.broadcast_to(bias_ref[j, dr:dr + 1, :], (GRID_W, PAIR))
                band.append([pltpu.roll(row, (half * GRID_W - (KW - 1)) % PAIR, axis=1, stride=1, stride_axis=0)
                             for half in range(2)])
            zero = jnp.zeros((GRID_W, PAIR), F32)
            for cp in range(2):
                for t in range(KH):
                    lo, hi = 2 * t + cp, 2 * t + cp + 1
                    tile = jnp.where(lane < GRID_W, band[lo][0] if lo < n_dr else zero,
                                     band[hi][1] if hi < n_dr else zero)
                    bias_s[j, cp, :, t * PAIR:(t + 1) * PAIR] = tile + outside

    q2 = _pair_queries(_pair_norm(q_ref[...], qn_ref[...], mean2) * HD_C ** -0.5, first)
    v2 = _pair_values(v_ref[...], first, ones_col)
    ks[...] = _pair_norm(k_ref[...], kn_ref[...], mean2).astype(BF16)
    for j in range(2):
        qs[j] = q2[j]
        vs[j] = v2[j]
    kt_ctx = kc_ref[0, 0].astype(BF16)
    vt = vc_ref[0, 0]
    ch = lax.broadcasted_iota(jnp.int32, vt.shape, 0)
    vt_ctx = [jnp.where(ch < HD_C, vt, jnp.where(ch == HD_C, 1.0, 0.0)).astype(BF16),
              jnp.where(ch < HD_C, jnp.where(ch == 0, 1.0, 0.0), vt).astype(BF16)]
    for r0 in range(0, GRID_ROWS, NA_ROW_GROUP):
        units = [(r, j) for r in range(r0, r0 + NA_ROW_GROUP) for j in range(2)]
        rows = {r: slice(r * GRID_W, (r + 1) * GRID_W) for r, _ in units}
        wins = {r: slice(_na_row_start(r) * GRID_W, (_na_row_start(r) + KH) * GRID_W) for r, _ in units}
        s_ctx_all = [_dot(qs[j, r0 * GRID_W:(r0 + NA_ROW_GROUP) * GRID_W, :], kt_ctx) for j in range(2)]
        s_ctx = [s_ctx_all[j][(r - r0) * GRID_W:(r - r0 + 1) * GRID_W] for r, j in units]
        s_win = []
        for r, j in units:
            dr0 = KH - 1 - (r - _na_row_start(r))
            lane0 = (dr0 - dr0 % 2) * GRID_W
            s_win.append(_dot_nt(qs[j, rows[r], :], ks[wins[r], :])
                         + bias_s[j, dr0 % 2, :, lane0:lane0 + KH * GRID_W])
        m = [_row_max(a, b) for a, b in zip(s_win, s_ctx)]
        p_win = [jnp.exp(a - mm).astype(BF16) for a, mm in zip(s_win, m)]
        p_ctx = [jnp.exp(b - mm).astype(BF16) for b, mm in zip(s_ctx, m)]
        o_aug = [_dot(p_win[i], vs[j, wins[r], :]) + _dot_nt(p_ctx[i], vt_ctx[j]) for i, (r, j) in enumerate(units)]
        for i in range(0, len(units), 2):
            o_ref[rows[units[i][0]], :] = _pair_output(o_aug[i:i + 2], first)


NA_BIAS_LANES = 2 * KH * GRID_W


def _na_attention(qkv, cache_kt, cache_vt, qn, kn, rpb):
    nhp = H_C // 2
    row0 = N_PROMPT // DEC_SEQ
    blk = lambda off: pl.BlockSpec((DEC_SEQ, 2 * HD_C), lambda p, b: (row0 + b, off + p))
    cache_spec = pl.BlockSpec((1, 1, PAIR, PAST_LEN), lambda p, b: (b, 0, p, 0))
    rpb_rows = 2 * KH
    bias = jnp.pad(rpb.astype(F32), ((0, 0), (0, rpb_rows - rpb.shape[1]), (0, PAIR - rpb.shape[2])))
    return pl.pallas_call(
        _na_attn_kernel,
        grid=(nhp, DEC_BATCH),
        in_specs=[blk(0), blk(nhp), blk(2 * nhp), cache_spec, cache_spec,
                  pl.BlockSpec((1, PAIR), lambda p, b: (0, 0)),
                  pl.BlockSpec((1, PAIR), lambda p, b: (0, 0)),
                  pl.BlockSpec((2, rpb_rows, PAIR), lambda p, b: (p, 0, 0))],
        out_specs=pl.BlockSpec((DEC_SEQ, 2 * HD_C), lambda p, b: (b, p)),
        out_shape=jax.ShapeDtypeStruct((N_SAMPLE, D_MODEL), F32),
        scratch_shapes=[pltpu.VMEM((2, DEC_SEQ, PAIR), BF16), pltpu.VMEM((DEC_SEQ, PAIR), BF16),
                        pltpu.VMEM((2, DEC_SEQ, PAIR), BF16), pltpu.VMEM((2, 2, GRID_W, NA_BIAS_LANES), F32)],
        compiler_params=_cparams("arbitrary", "arbitrary"),
        name="na_attention",
    )(qkv, qkv, qkv, cache_kt, cache_vt, jnp.tile(qn.reshape(1, HD_C), (1, 2)), jnp.tile(kn.reshape(1, HD_C), (1, 2)),
      bias)


def _seq_layout(prompt):
    return (SEQ, BATCH, 0) if prompt else (DEC_SEQ, DEC_BATCH, N_PROMPT // DEC_SEQ)


def _flip_blocks(m, c):
    r, s = m.shape
    return m.reshape(r // c, c, s // c, c)[:, ::-1, :, ::-1].reshape(r, s)


def _rms_gate(x, gn, gate):
    ms = jnp.mean(x * x, axis=-1, keepdims=True)
    return x * lax.rsqrt(ms + EPS) * gn * _silu(gate)


HG_LEVELS = tuple(CHUNK >> (i + 1) for i in range(CHUNK.bit_length() - 1))
HG_NL = len(HG_LEVELS)
HG_STACK = (HG_NL + 1) * CHUNK
TOT_ROWS = 16
HG_ROWS = (HG_NL + 2) * CHUNK + TOT_ROWS


def _hgrn_consts():
    c = CHUNK
    level_rows = []
    mask = np.zeros((HG_STACK, HG_STACK), np.float32)
    mask[:c, :c] = np.eye(c)
    for li, b in enumerate(HG_LEVELS):
        m = np.zeros((c, c), np.float32)
        blk = np.zeros((c, c), np.float32)
        for t in range(c):
            mid = (t // (2 * b)) * 2 * b + b
            if t >= mid:
                m[t, mid:t + 1] = 1.0
                blk[t, mid - b:mid] = 1.0
            else:
                m[t, t + 1:mid] = 1.0
        level_rows.append(m)
        mask[(li + 1) * c:(li + 2) * c, (li + 1) * c:(li + 2) * c] = blk
    dq = np.tril(np.ones((c, c), np.float32))
    dk = np.triu(np.ones((c, c), np.float32), 1)
    body = np.concatenate(level_rows + [dq, dk], axis=0)
    tot = np.ones((TOT_ROWS, c), np.float32)
    mcs, masks = [], []
    for reverse in (False, True):
        bm = _flip_blocks(body, c) if reverse else body
        mk = _flip_blocks(mask, c) if reverse else mask
        mc = np.concatenate([bm, tot], axis=0)
        mcs.append(np.concatenate([mc, mc], axis=1))
        masks.append(mk)
    return jnp.asarray(np.stack(mcs), BF16), jnp.asarray(np.stack(masks), F32)


HG_FAST = 64
HG_HALF = HG_FAST // 2
HG_FAST_ROWS = 4 * HG_FAST + TOT_ROWS
HG_SAFE_EXP = 40.0
HG_FAST_STEPS = 4


def _hgrn_fast_consts(seq):
    c, m = HG_FAST, HG_HALF
    aq = np.zeros((c, c), np.float32)
    for t in range(c):
        if t >= m:
            aq[t, m:t + 1] = 1.0
        else:
            aq[t, t + 1:m] = -1.0
    dq = np.tril(np.ones((c, c), np.float32))
    dk = np.triu(np.ones((c, c), np.float32), 1)
    body = np.concatenate([aq, -aq, dq, dk], axis=0)
    tot = np.ones((TOT_ROWS, c), np.float32)
    causal = np.tril(np.ones((c, c), np.float32))
    mfs, masks = [], []
    for reverse in (False, True):
        bm = _flip_blocks(body, c) if reverse else body
        mf = np.concatenate([bm, tot], axis=0)
        mfs.append(np.concatenate([mf, mf], axis=1))
        masks.append(causal.T if reverse else causal)
    n_half = seq // m
    half = np.zeros((max(n_half, 16), seq), np.float32)
    for i in range(n_half):
        half[i, i * m:(i + 1) * m] = 1.0
    return jnp.asarray(np.stack(mfs), BF16), jnp.asarray(np.stack(masks), F32), jnp.asarray(half, BF16)


def _hgrn_kernel(*refs, seq, has_s0, emit_state):
    it = iter(refs)
    qa_ref, ff_ref, fb_ref, ia_ref, ga_ref, lb_ref, gn_ref, mc_ref, mask_ref = [next(it) for _ in range(9)]
    mf_ref, causal_ref, half_ref = [next(it) for _ in range(3)]
    s0_ref = next(it) if has_s0 else None
    o_ref = next(it)
    st_ref = next(it) if emit_state else None
    s_scr, acc, f_s, lf_s = [next(it) for _ in range(4)]
    c = CHUNK
    n_chunks = seq // c
    combos = [(d, h) for d in range(2) for h in range(H_A)]
    lanes = [slice(h * DK_A, (h + 1) * DK_A) for h in range(H_A)]
    add = lambda a, b: a + b

    lb_raw = lb_ref[...]
    lb_e = jnp.exp(lb_raw - jnp.max(lb_raw, axis=0, keepdims=True))
    lb_all = lb_e[0:1] / jnp.sum(lb_e, axis=0, keepdims=True)

    for d in range(2):
        for h in range(H_A):
            s_scr[d, h] = s0_ref[0, 0, d, h].T if has_s0 else jnp.zeros((DV_A, DK_A), F32)
    acc[...] = jnp.zeros(acc.shape, F32)

    worst = []
    for d, fr_ref in enumerate((ff_ref, fb_ref)):
        f = lb_all + (1.0 - lb_all) * _sigmoid(fr_ref[...])
        lf = jnp.log(f)
        f_s[d] = f
        lf_s[d] = lf
        worst.append(jnp.max(_dot(half_ref[...], (-lf).astype(BF16))))
    safe = jnp.maximum(worst[0], worst[1]) <= HG_SAFE_EXP

    def fast_body(n, carry):
        cf = HG_FAST
        n_fast = seq // cf
        steps = range(HG_FAST_STEPS)
        chunk = lambda d, t: (n * HG_FAST_STEPS + t) if d == 0 else (n_fast - 1 - n * HG_FAST_STEPS - t)
        rows = [[pl.ds(pl.multiple_of(chunk(d, t) * cf, cf), cf) for t in steps] for d in range(2)]
        units = [(t, d, h) for t in steps for d, h in combos]
        e_all = [[jnp.exp(_dot_const(mf_ref[d], lf_s[d, rows[d][t], :])) for t in steps] for d in range(2)]
        q_all = [[_silu(qa_ref[rows[d][t], :]) * DK_A ** -0.5 for t in steps] for d in range(2)]
        k_all = [[1.0 - f_s[d, rows[d][t], :] for t in steps] for d in range(2)]
        v_all = [[ia_ref[rows[d][t], :].astype(BF16) for t in steps] for d in range(2)]
        qs = {u: q_all[u[1]][u[0]][:, lanes[u[2]]] for u in units}
        ks = {u: k_all[u[1]][u[0]][:, lanes[u[2]]] for u in units}
        vs = {u: v_all[u[1]][u[0]][:, lanes[u[2]]] for u in units}
        es = {u: [e_all[u[1]][u[0]][i * cf:(i + 1) * cf, lanes[u[2]]] for i in range(4)] for u in units}
        p = {u: jnp.where(causal_ref[u[1]] > 0.0,
                          _dot_nt((qs[u] * es[u][0]).astype(BF16), (ks[u] * es[u][1]).astype(BF16)), 0.0).astype(BF16)
             for u in units}
        intra = {u: _dot(p[u], vs[u]) for u in units}
        upd = {u: _dot_tn(vs[u], (ks[u] * es[u][3]).astype(BF16)) for u in units}
        qdec = {u: (qs[u] * es[u][2]).astype(BF16) for u in units}
        st = {(d, h): s_scr[d, h] for d, h in combos}
        o = {}
        for t in steps:
            for d, h in combos:
                u = (t, d, h)
                o[u] = intra[u] + _dot_nt(qdec[u], st[d, h].astype(BF16))
                st[d, h] = st[d, h] * e_all[d][t][4 * cf:4 * cf + 1, lanes[h]] + upd[u]
        for t in steps:
            for d in range(2):
                acc[rows[d][t], :] += jnp.concatenate([o[t, d, h] for h in range(H_A)], axis=1)
        for d, h in combos:
            s_scr[d, h] = st[d, h]
        return carry

    def body(n, carry):
        rows = [pl.ds(pl.multiple_of((n if d == 0 else n_chunks - 1 - n) * c, c), c) for d in range(2)]
        f_all = [f_s[d, rows[d], :] for d in range(2)]
        e_all = [jnp.exp(_dot_const(mc_ref[d], lf_s[d, rows[d], :])) for d in range(2)]
        q_all = [_silu(qa_ref[rows[d], :]) * DK_A ** -0.5 for d in range(2)]
        v_all = [ia_ref[rows[d], :].astype(BF16) for d in range(2)]
        st = [s_scr[d, h] for d, h in combos]
        qs, ks, vs, es = [], [], [], []
        for d, h in combos:
            qs.append(q_all[d][:, lanes[h]])
            ks.append(1.0 - f_all[d][:, lanes[h]])
            vs.append(v_all[d][:, lanes[h]])
            es.append(e_all[d][:, lanes[h]])
        lvl = [[e[i * c:(i + 1) * c] for i in range(HG_NL + 2)] for e in es]
        qst = [jnp.concatenate([q] + [q * l[i] for i in range(HG_NL)], axis=0).astype(BF16) for q, l in zip(qs, lvl)]
        kst = [jnp.concatenate([k] + [k * l[i] for i in range(HG_NL)], axis=0).astype(BF16) for k, l in zip(ks, lvl)]
        r = [(_dot_nt(qst[i], kst[i]) * mask_ref[d]).astype(BF16) for i, (d, h) in enumerate(combos)]
        ost = [_dot(r[i], jnp.concatenate([vs[i]] * (HG_NL + 1), axis=0)) for i in range(len(combos))]
        inter = [_dot_nt((qs[i] * lvl[i][HG_NL]).astype(BF16), st[i].astype(BF16)) for i in range(len(combos))]
        upd = [_dot_tn(vs[i], (ks[i] * lvl[i][HG_NL + 1]).astype(BF16)) for i in range(len(combos))]
        o = [functools.reduce(lambda a, b: a + b, [ost[i][j * c:(j + 1) * c] for j in range(HG_NL + 1)]) + inter[i]
             for i in range(len(combos))]
        for d in range(2):
            acc[rows[d], :] += jnp.concatenate(o[d * H_A:(d + 1) * H_A], axis=1)
        for i, (d, h) in enumerate(combos):
            e_tot = es[i][(HG_NL + 2) * c:(HG_NL + 2) * c + 1]
            s_scr[d, h] = st[i] * e_tot + upd[i]
        return carry

    @pl.when(safe)
    def _():
        lax.fori_loop(0, seq // (HG_FAST * HG_FAST_STEPS), fast_body, 0)

    @pl.when(jnp.logical_not(safe))
    def _():
        lax.fori_loop(0, n_chunks, body, 0)

    for h in range(H_A):
        ln = slice(h * DV_A, (h + 1) * DV_A)
        o_ref[:, ln] = _rms_gate(acc[:, ln], gn_ref[...], ga_ref[:, ln])
    if emit_state:
        for d in range(2):
            for h in range(H_A):
                st_ref[0, 0, d, h] = s_scr[d, h].T


def _hgrn(proj, hgrn_lb, gn, consts, prompt, s0=None, layer=0):
    seq, nb, rb0 = _seq_layout(prompt)
    consts = list(consts) + list(_hgrn_fast_consts(seq))
    wa = H_A * DK_A
    blk = lambda j: pl.BlockSpec((seq, wa), lambda b: (rb0 + b, j))
    const2 = lambda b: (0, 0)
    st_block = (1, 1, 2, H_A, DK_A, DV_A)
    in_specs = [blk(0), blk(1), blk(2), blk(3), blk(4),
                pl.BlockSpec(hgrn_lb.shape, const2), pl.BlockSpec((1, DV_A), const2)]
    in_specs += [pl.BlockSpec(m.shape, lambda b, nd=m.ndim: (0,) * nd) for m in consts]
    args = [proj] * 5 + [hgrn_lb, gn.reshape(1, DV_A)] + consts
    if s0 is not None:
        in_specs.append(pl.BlockSpec(st_block, lambda b: (b, layer, 0, 0, 0, 0)))
        args.append(s0)
    out_specs = [pl.BlockSpec((seq, wa), lambda b: (b, 0))]
    out_shape = [jax.ShapeDtypeStruct((nb * seq, wa), F32)]
    if prompt:
        out_specs.append(pl.BlockSpec(st_block, lambda b: (b, 0, 0, 0, 0, 0)))
        out_shape.append(jax.ShapeDtypeStruct((nb, 1, 2, H_A, DK_A, DV_A), F32))
    return pl.pallas_call(
        functools.partial(_hgrn_kernel, seq=seq, has_s0=s0 is not None, emit_state=prompt),
        grid=(nb,),
        in_specs=in_specs,
        out_specs=out_specs,
        out_shape=out_shape,
        scratch_shapes=[pltpu.VMEM((2, H_A, DV_A, DK_A), F32), pltpu.VMEM((seq, wa), F32),
                        pltpu.VMEM((2, seq, wa), F32), pltpu.VMEM((2, seq, wa), F32)],
        compiler_params=_cparams("arbitrary"),
        name="hgrn_prompt" if prompt else "hgrn_sample",
    )(*args)


GD_SUB = GBLK // CHUNK
GD_BLOCKS_PER_ITER = 2
GD_ROWS = 2 * GBLK + TOT_ROWS


def _gdn_consts():
    n, c = GBLK, CHUNK
    same = (np.arange(n)[:, None] // c) == (np.arange(n)[None, :] // c)
    tri = (same & (np.arange(n)[None, :] <= np.arange(n)[:, None])).astype(np.float32)
    sup = (same & (np.arange(n)[None, :] > np.arange(n)[:, None])).astype(np.float32)
    tot = np.zeros((TOT_ROWS, n), np.float32)
    for s in range(GD_SUB):
        tot[s, s * c:(s + 1) * c] = 1.0
    mgs, tts, tris = [], [], []
    for reverse in (False, True):
        t = _flip_blocks(tri, c) if reverse else tri
        s = _flip_blocks(sup, c) if reverse else sup
        mg = np.concatenate([t, s, tot], axis=0)
        mgs.append(np.concatenate([mg, mg], axis=1))
        tts.append(np.concatenate([t.T, t.T], axis=0))
        tris.append(t)
    tris.append(same.astype(np.float32))
    return jnp.asarray(np.stack(mgs), BF16), jnp.asarray(np.stack(tts), BF16), jnp.asarray(np.stack(tris), F32)


def _softplus(x):
    return jnp.maximum(x, 0.0) + jnp.log(1.0 + jnp.exp(-jnp.abs(x)))


CONV_PAD = 8


def _conv_silu(x, w, seq):
    half = SHORT_CONV // 2
    pad = jnp.zeros((CONV_PAD, x.shape[1]), x.dtype)
    xe = jnp.concatenate([pad, x, pad], axis=0)
    acc = xe * w[half:half + 1]
    for j in range(SHORT_CONV):
        shift = half - j
        if shift != 0:
            acc = acc + pltpu.roll(xe, shift % (seq + 2 * CONV_PAD), axis=0) * w[j:j + 1]
    return _silu(acc[CONV_PAD:seq + CONV_PAD])


def _l2norm_heads(x, n_heads, width, scale):
    outs = []
    for h in range(n_heads):
        xh = x[:, h * width:(h + 1) * width]
        outs.append(xh * (lax.rsqrt(jnp.sum(xh * xh, axis=-1, keepdims=True) + EPS) * scale))
    return jnp.concatenate(outs, axis=-1)


def _gdn_kernel(*refs, seq, has_s0, emit_state):
    it = iter(refs)
    (q_ref, k_ref, v_ref, gb_ref, gate_ref, cw_ref, alog_ref, dt_ref, gn_ref,
     mg_ref, tt_ref, tri_ref) = [next(it) for _ in range(12)]
    s0_ref = next(it) if has_s0 else None
    o_ref = next(it)
    st_ref = next(it) if emit_state else None
    qn, kn, vn, u_s, w_s, qg_s, kdt_s, at_s, et_s, s_scr, acc = [next(it) for _ in range(11)]
    c = CHUNK
    n_chunks = seq // c
    n_blocks = seq // GBLK
    wq = H_B * DK_B
    n_dh = 2 * H_B
    combos = [(d, h) for d in range(2) for h in range(H_B)]
    lanes = [slice(h * DK_B, (h + 1) * DK_B) for h in range(H_B)]

    qn[...] = _l2norm_heads(_conv_silu(q_ref[...], cw_ref[:, 0:wq], seq), H_B, DK_B, DK_B ** -0.5)
    kn[...] = _l2norm_heads(_conv_silu(k_ref[...], cw_ref[:, wq:2 * wq], seq), H_B, DK_B, 1.0)
    vn[...] = _conv_silu(v_ref[...], cw_ref[:, 2 * wq:3 * wq], seq)
    for i in range(2 * H_B):
        s_scr[i] = s0_ref[0, 0, i // H_B, i % H_B] if has_s0 else jnp.zeros((DK_B, DV_B), F32)
    acc[...] = jnp.zeros(acc.shape, F32)

    eye = (lax.broadcasted_iota(jnp.int32, (GBLK, GBLK), 0)
           == lax.broadcasted_iota(jnp.int32, (GBLK, GBLK), 1)).astype(F32)
    eye_pk = (lax.broadcasted_iota(jnp.int32, (c, GBLK), 0)
              == lax.broadcasted_iota(jnp.int32, (c, GBLK), 1) % c).astype(F32)
    bwd_lane = lax.broadcasted_iota(jnp.int32, (1, 128), 1) % n_dh >= H_B
    add = lambda a, b: a + b

    same_chunk = tri_ref[2].astype(BF16)

    def expand(pk):
        return jnp.concatenate([pk] * GD_SUB, axis=0) * same_chunk

    def pack(bd):
        return functools.reduce(add, [bd[s * c:(s + 1) * c] for s in range(GD_SUB)])

    def weights(hi, lo):
        return jnp.concatenate([expand(hi), expand(lo)], axis=1)

    def dot3_split(a_hi, a_lo, w2):
        m, n = a_hi.shape[0], w2.shape[1] // 2
        t = _dot(jnp.concatenate([a_hi, a_lo], axis=0), w2)
        return t[:m, :n] + t[m:, :n] + t[:m, n:]

    def block_body(it, carry):
        blks = [it * GD_BLOCKS_PER_ITER + o for o in range(GD_BLOCKS_PER_ITER)]
        units = [(o, d, h) for o in range(GD_BLOCKS_PER_ITER) for d, h in combos]
        idx = lambda d, h: d * H_B + h
        col = lambda x, j: jnp.broadcast_to(x[:, j:j + 1], (GBLK, DK_B))
        rows = [pl.ds(pl.multiple_of(b * GBLK, GBLK), GBLK) for b in blks]
        gates = [gate_ref[r, :] for r in rows]
        glog_all = [-jnp.exp(alog_ref[...]) * _softplus(x + dt_ref[...]) for x in gates]
        beta_all = [_sigmoid(x) for x in gates]
        g2 = [jnp.concatenate(_split2(x), axis=0) for x in glog_all]
        dg = [[_dot(mg_ref[d], x) for d in range(2)] for x in g2]
        dsel = [jnp.where(bwd_lane, x[1], x[0]) for x in dg]
        eg_all = [jnp.exp(x) for x in dsel]
        gt = [[_dot_tn(x, tt_ref[d]) for d in range(2)] for x in g2]
        qs = [[qn[r, ln] for ln in lanes] for r in rows]
        ks = [[kn[r, ln] for ln in lanes] for r in rows]
        vs = [[vn[r, ln] for ln in lanes] for r in rows]
        betas = [col(beta_all[o], n_dh + idx(d, h)) for o, d, h in units]
        kbs = [ks[o][h] * betas[u] for u, (o, d, h) in enumerate(units)]
        kb_of = {unit: kbs[u] for u, unit in enumerate(units)}
        kk = {(o, h): _dot_nt(jnp.concatenate([qs[o][h], kb_of[o, 0, h], kb_of[o, 1, h]], axis=0).astype(BF16),
                              ks[o][h].astype(BF16))
              for o in range(GD_BLOCKS_PER_ITER) for h in range(H_B)}
        decay = []
        for o, d, h in units:
            inside = tri_ref[d] > 0.0
            gd = col(dsel[o][:GBLK], idx(d, h)) - gt[o][d][idx(d, h):idx(d, h) + 1, :]
            decay.append(jnp.where(inside, jnp.exp(jnp.where(inside, gd, 0.0)), 0.0))
        attn = [kk[o, h][:GBLK] * decay[u] for u, (o, d, h) in enumerate(units)]
        p_pk = [pack(kk[o, h][(1 + d) * GBLK:(2 + d) * GBLK] * decay[u] * (1.0 - eye))
                for u, (o, d, h) in enumerate(units)]
        x_pk = [eye_pk - p for p in p_pk]
        p_sp = [_split2(p) for p in p_pk]
        p_w = [weights(*s) for s in p_sp]
        for _ in range(CHUNK.bit_length() - 2):
            p_pk = [dot3_split(*s, w) for s, w in zip(p_sp, p_w)]
            p_sp = [_split2(p) for p in p_pk]
            p_w = [weights(*s) for s in p_sp]
            x_pk = [x + dot3_split(*_split2(x), w) for x, w in zip(x_pk, p_w)]
        eg_col = [col(eg_all[o][:GBLK], idx(d, h)) for o, d, h in units]
        ekd_col = [col(eg_all[o][GBLK:2 * GBLK], idx(d, h)) for o, d, h in units]
        rhs = [_split2(jnp.concatenate([vs[o][h] * betas[u], kbs[u] * eg_col[u]], axis=1))
               for u, (o, d, h) in enumerate(units)]
        t_sp = [[expand(part) for part in _split2(x)] for x in x_pk]
        uw = [_dot(t[0], r[0]) + (_dot(t[0], r[1]) + _dot(t[1], r[0])) for t, r in zip(t_sp, rhs)]
        for u, (o, d, h) in enumerate(units):
            i = idx(d, h)
            qg = (qs[o][h] * eg_col[u]).astype(BF16)
            kdt_s[i, blks[o]] = (ks[o][h] * ekd_col[u]).T.astype(BF16)
            for s in range(GD_SUB):
                cn = blks[o] * GD_SUB + s
                r = slice(s * c, (s + 1) * c)
                u_s[i, cn] = uw[u][r, :DV_B]
                w_s[i, cn] = uw[u][r, DV_B:].astype(BF16)
                qg_s[i, cn] = qg[r]
                at_s[i, cn] = attn[u][r].astype(BF16)
                et_s[i, cn] = jnp.broadcast_to(eg_all[o][2 * GBLK + s:2 * GBLK + s + 1, i:i + 1], (8, DV_B))
        return carry

    lax.fori_loop(0, n_blocks // GD_BLOCKS_PER_ITER, block_body, 0)

    def chunk_body(n, carry):
        cns = [n, n_chunks - 1 - n]
        rows = [pl.ds(pl.multiple_of(cn * c, c), c) for cn in cns]
        sub_of_row = lax.broadcasted_iota(jnp.int32, (GBLK, 1), 0) // c
        in_chunk = [sub_of_row == cn % GD_SUB for cn in cns]
        st = [s_scr[i] for i in range(n_dh)]
        ws = [_dot(jnp.concatenate([w_s[i, cns[d]], qg_s[i, cns[d]]], axis=0), st[i].astype(BF16))
              for i, (d, h) in enumerate(combos)]
        vblk = [jnp.where(in_chunk[d], jnp.concatenate([u_s[i, cns[d]] - ws[i][:c]] * GD_SUB, axis=0), 0.0).astype(BF16)
                for i, (d, h) in enumerate(combos)]
        r = [_dot(jnp.concatenate([at_s[i, cns[d]], kdt_s[i, cns[d] // GD_SUB]], axis=0), vblk[i])
             for i, (d, h) in enumerate(combos)]
        for d in range(2):
            acc[rows[d], :] += jnp.concatenate([ws[i][c:] + r[i][:c] for i in range(d * H_B, (d + 1) * H_B)], axis=1)
        for i, (d, h) in enumerate(combos):
            s_scr[i] = st[i] * et_s[i, cns[d]][0:1] + r[i][c:]
        return carry

    lax.fori_loop(0, n_chunks, chunk_body, 0)

    for h in range(H_B):
        ln = slice(h * DV_B, (h + 1) * DV_B)
        o_ref[:, ln] = _rms_gate(acc[:, ln], gn_ref[...], gb_ref[:, ln])
    if emit_state:
        for i in range(2 * H_B):
            st_ref[0, 0, i // H_B, i % H_B] = s_scr[i]


def _gdn(proj, gates, conv_w, a_log, dt_bias, gn, consts, prompt, s0=None, layer=0):
    seq, nb, rb0 = _seq_layout(prompt)
    n_chunks = seq // CHUNK
    wq = H_B * DK_B
    blk = lambda j: pl.BlockSpec((seq, wq), lambda b: (rb0 + b, j))
    const2 = lambda b: (0, 0)
    const3 = lambda b: (0, 0, 0)
    st_block = (1, 1, 2, H_B, DK_B, DV_B)
    pad_row = lambda p: jnp.pad(p.reshape(1, -1).astype(F32), ((0, 0), (0, 128 - p.size)))
    in_specs = [blk(5), blk(6), blk(7), blk(8),
                pl.BlockSpec((seq, 128), lambda b: (rb0 + b, 0)),
                pl.BlockSpec((SHORT_CONV, 3 * wq), const2),
                pl.BlockSpec((1, 128), const2), pl.BlockSpec((1, 128), const2), pl.BlockSpec((1, DV_B), const2)]
    in_specs += [pl.BlockSpec(m.shape, const3) for m in consts]
    args = [proj] * 4 + [gates, conv_w.reshape(SHORT_CONV, 3 * wq), pad_row(a_log), pad_row(dt_bias),
                         gn.reshape(1, DV_B)] + list(consts)
    if s0 is not None:
        in_specs.append(pl.BlockSpec(st_block, lambda b: (b, layer, 0, 0, 0, 0)))
        args.append(s0)
    out_specs = [pl.BlockSpec((seq, wq), lambda b: (b, 0))]
    out_shape = [jax.ShapeDtypeStruct((nb * seq, wq), F32)]
    if prompt:
        out_specs.append(pl.BlockSpec(st_block, lambda b: (b, 0, 0, 0, 0, 0)))
        out_shape.append(jax.ShapeDtypeStruct((nb, 1, 2, H_B, DK_B, DV_B), F32))
    n_dh = 2 * H_B
    scratch = ([pltpu.VMEM((seq, wq), F32)] * 3
               + [pltpu.VMEM((n_dh, n_chunks, CHUNK, DV_B), F32)]
               + [pltpu.VMEM((n_dh, n_chunks, CHUNK, DK_B), BF16)] * 2
               + [pltpu.VMEM((n_dh, seq // GBLK, DK_B, GBLK), BF16),
                  pltpu.VMEM((n_dh, n_chunks, CHUNK, GBLK), BF16),
                  pltpu.VMEM((n_dh, n_chunks, 8, DV_B), F32),
                  pltpu.VMEM((n_dh, DK_B, DV_B), F32),
                  pltpu.VMEM((seq, wq), F32)])
    return pl.pallas_call(
        functools.partial(_gdn_kernel, seq=seq, has_s0=s0 is not None, emit_state=prompt),
        grid=(nb,),
        in_specs=in_specs,
        out_specs=out_specs,
        out_shape=out_shape,
        scratch_shapes=scratch,
        compiler_params=_cparams("arbitrary"),
        name="gdn_prompt" if prompt else "gdn_sample",
    )(*args)


def kernel(x_prompt, x_sample, state_hgrn, state_gdn, cache_na_k, cache_na_v, c, c_ctx, ada_w, ada_b, norm_g, w_in_ab, w_out_ab, hgrn_lb, gdn_conv, gdn_a_log, gdn_dt_bias, gn_hgrn, gn_gdn, w_qkv_na, qn_na, kn_na, rpb_na, w_out_na, w_mlp1, w_mlp2):
    cond = jnp.concatenate([c_ctx[None, :], c, jnp.zeros((N_MOD_ROWS - 1 - DEC_BATCH, D_MODEL), F32)], axis=0)
    mods, w_in_t = _modulation(cond, ada_w, ada_b, jnp.swapaxes(w_in_ab, 1, 2))
    xs = (x_prompt.reshape(N_PROMPT, D_MODEL), x_sample.reshape(N_SAMPLE, D_MODEL))

    w_gate_t = jnp.pad(w_in_t[D_MAIN_AB:], ((0, 128 - N_GATE_AB), (0, 0)))
    proj, gates, wo0, w1_0, w2_0 = _norm_proj(xs, mods[0], norm_g[0, 0], [w_in_t, w_gate_t], widths=[D_MAIN_AB, 128],
                                              side=((w_out_ab, 0), (w_mlp1, 0), (w_mlp2, 0)), w_transposed=True)
    hg_consts = _hgrn_consts()
    gd_consts = _gdn_consts()
    hg_prompt, new_hgrn = _hgrn(proj, hgrn_lb, gn_hgrn[0], hg_consts, True)
    hg_sample, = _hgrn(proj, hgrn_lb, gn_hgrn[0], hg_consts, False, s0=state_hgrn)
    gd_args = (gdn_conv[0], gdn_a_log[0], gdn_dt_bias[0], gn_gdn[0], gd_consts)
    gd_prompt, new_gdn = _gdn(proj, gates, *gd_args, True)
    gd_sample, = _gdn(proj, gates, *gd_args, False, s0=state_gdn)
    xs, (w_qkv, wo1, w1_1, w2_1) = _post_mixer(xs, [(hg_prompt, hg_sample), (gd_prompt, gd_sample)], mods[0],
                                               norm_g[0, 1], wo0, w1_0, w2_0,
                                               side=((w_qkv_na, 0), (w_out_na, 0), (w_mlp1, 1), (w_mlp2, 1)))

    qkv, = _norm_proj(xs, mods[1], norm_g[1, 0], [w_qkv])
    at_prompt, new_kt, new_vt = _ctx_attention(qkv, qn_na[0], kn_na[0])
    time_minor = lambda a: jnp.swapaxes(a, -1, -2).reshape(a.shape[0], 1, H_C * HD_C, a.shape[3])
    at_sample = _na_attention(qkv, time_minor(cache_na_k), time_minor(cache_na_v), qn_na[0], kn_na[0], rpb_na[0])
    time_major = lambda a: jnp.swapaxes(a.reshape(BATCH, 1, H_C, HD_C, SEQ), -1, -2)
    new_k, new_v = time_major(new_kt), time_major(new_vt)
    (y_prompt, y_sample), _ = _post_mixer(xs, [(at_prompt, at_sample)], mods[1], norm_g[1, 1], wo1, w1_1, w2_1,
                                          split_out=True)

    return (y_prompt.reshape(BATCH, SEQ, D_MODEL), y_sample.reshape(DEC_BATCH, DEC_SEQ, D_MODEL),
            new_hgrn, new_gdn, new_k, new_v)
```

```python
import functools

import numpy as np
import jax
import jax.numpy as jnp
from jax import lax
from jax.experimental import pallas as pl
from jax.experimental.pallas import tpu as pltpu

F32 = jnp.float32
BF16 = jnp.bfloat16

D_MODEL = 1024
BATCH = 16
SEQ = 256
DEC_BATCH = 4
DEC_SEQ = 1024
PAST_LEN = 256
N_PROMPT = BATCH * SEQ
N_SAMPLE = DEC_BATCH * DEC_SEQ
N_TOK = N_PROMPT + N_SAMPLE
GRID_W = 64
GRID_ROWS = DEC_SEQ // GRID_W
H_A = 4
DK_A = 128
DV_A = 128
H_B = 4
DK_B = 128
DV_B = 128
SHORT_CONV = 5
H_C = 16
HD_C = 64
KH = 8
KW = 16
D_FF = 4 * D_MODEL
EPS = 1e-6
NEG_INF = -1e30
N_MOD_ROWS = 8
D_MAIN_AB = 4608
N_GATE_AB = 16
CHUNK = 32
GBLK = 128
VMEM_LIMIT = 56 * 1024 * 1024


def _cparams(*sem):
    return pltpu.CompilerParams(dimension_semantics=sem, vmem_limit_bytes=VMEM_LIMIT)


def _sigmoid(x):
    return 0.5 * jnp.tanh(0.5 * x) + 0.5


def _silu(x):
    return x * _sigmoid(x)


def _dot(a, b):
    return jnp.dot(a, b, preferred_element_type=F32)


def _dot_nt(a, b):
    return lax.dot_general(a, b, (((1,), (1,)), ((), ())), preferred_element_type=F32)


def _dot_tn(a, b):
    return lax.dot_general(a, b, (((0,), (0,)), ((), ())), preferred_element_type=F32)


def _split2(x):
    hi = x.astype(BF16)
    lo = (x - hi.astype(F32)).astype(BF16)
    return hi, lo


def _dot_const(m2, x):
    hi, lo = _split2(x)
    return _dot(m2, jnp.concatenate([hi, lo], axis=0))


def _dot3(a, b):
    ah, al = _split2(a)
    bh, bl = _split2(b)
    return _dot(ah, bh) + (_dot(ah, bl) + _dot(al, bh))


def _mod_row(i, tm):
    start = i * tm
    return jnp.where(start < N_PROMPT, 0, 1 + (start - N_PROMPT) // DEC_SEQ)


def _mod_slice(mod_ref, row, k):
    return mod_ref[pl.ds(row, 1), k * D_MODEL:(k + 1) * D_MODEL]


def _norm_mod(x, g, sc, sh):
    ms = jnp.mean(x * x, axis=-1, keepdims=True)
    return (x * lax.rsqrt(ms + EPS) * g) * (1.0 + sc) + sh


def _mod_kernel(cond_ref, w_ref, b_ref, side_ref, o_ref, side_o_ref):
    s = _silu(cond_ref[...]).astype(BF16)
    o_ref[0] = _dot(s, w_ref[0].astype(BF16)) + b_ref[0]
    side_o_ref[...] = side_ref[...].astype(BF16)


MOD_TN = 768


def _modulation(cond8, ada_w, ada_b, side):
    depth = ada_w.shape[0]
    nj = ada_w.shape[2] // MOD_TN
    slab = -(-side.shape[1] // (depth * nj * 16)) * 16
    return pl.pallas_call(
        _mod_kernel,
        grid=(depth, nj),
        in_specs=[
            pl.BlockSpec((N_MOD_ROWS, D_MODEL), lambda l, j: (0, 0)),
            pl.BlockSpec((1, D_MODEL, MOD_TN), lambda l, j: (l, 0, j)),
            pl.BlockSpec((1, 1, MOD_TN), lambda l, j: (l, 0, j)),
            pl.BlockSpec((None, slab, side.shape[2]), lambda l, j: (0, l * nj + j, 0)),
        ],
        out_specs=[pl.BlockSpec((1, N_MOD_ROWS, MOD_TN), lambda l, j: (l, 0, j)),
                   pl.BlockSpec((slab, side.shape[2]), lambda l, j: (l * nj + j, 0))],
        out_shape=[jax.ShapeDtypeStruct((depth, N_MOD_ROWS, ada_w.shape[2]), F32),
                   jax.ShapeDtypeStruct(side.shape[1:], BF16)],
        compiler_params=_cparams("arbitrary", "arbitrary"),
        name="modulation",
    )(cond8, ada_w, ada_b.reshape(depth, 1, -1), side)


def _stream_specs(n_arrays, tm, width=D_MODEL):
    if n_arrays == 1:
        return [pl.BlockSpec((tm, width), lambda i: (i, 0))]
    npt = N_PROMPT // tm
    return [pl.BlockSpec((tm, width), lambda i: (jnp.minimum(i, npt - 1), 0)),
            pl.BlockSpec((tm, width), lambda i: (jnp.maximum(i - npt, 0), 0))]


def _stream_load(x_refs, tm):
    if len(x_refs) == 1:
        return x_refs[0][...]
    return jnp.where(pl.program_id(0) < N_PROMPT // tm, x_refs[0][...], x_refs[1][...])


def _side_specs(side, n_steps):
    in_specs = [pl.BlockSpec((None, w.shape[1] // n_steps, w.shape[2]), lambda i, l=l: (l, i, 0)) for w, l in side]
    out_specs = [pl.BlockSpec((w.shape[1] // n_steps, w.shape[2]), lambda i: (i, 0)) for w, _ in side]
    shapes = [jax.ShapeDtypeStruct(w.shape[1:], BF16) for w, _ in side]
    return in_specs, out_specs, shapes


def _side_cast(in_refs, out_refs):
    for i_ref, o_ref in zip(in_refs, out_refs):
        o_ref[...] = i_ref[...].astype(BF16)


def _norm_proj_kernel(*refs, tm, n_x, n_w, n_side, w_transposed):
    x_refs, (mod_ref, g_ref) = refs[:n_x], refs[n_x:n_x + 2]
    w_refs = refs[n_x + 2:n_x + 2 + n_w]
    side_in = refs[n_x + 2 + n_w:n_x + 2 + n_w + n_side]
    o_refs = refs[n_x + 2 + n_w + n_side:n_x + 2 + 2 * n_w + n_side]
    side_out = refs[n_x + 2 + 2 * n_w + n_side:]
    row = _mod_row(pl.program_id(0), tm)
    h = _norm_mod(_stream_load(x_refs, tm), g_ref[...], _mod_slice(mod_ref, row, 1), _mod_slice(mod_ref, row, 0)).astype(BF16)
    for w_ref, o_ref in zip(w_refs, o_refs):
        o_ref[...] = _dot_nt(h, w_ref[...]) if w_transposed else _dot(h, w_ref[...])
    _side_cast(side_in, side_out)


def _norm_proj(xs, mod, g, ws, widths=None, side=(), w_transposed=False, tm=512):
    n_w = len(ws)
    widths = widths or [w.shape[0 if w_transposed else 1] for w in ws]
    const = lambda i: (0, 0)
    w_block = (lambda n: (n, D_MODEL)) if w_transposed else (lambda n: (D_MODEL, n))
    side_in_specs, side_out_specs, side_shapes = _side_specs(side, N_TOK // tm)
    return pl.pallas_call(
        functools.partial(_norm_proj_kernel, tm=tm, n_x=len(xs), n_w=n_w, n_side=len(side), w_transposed=w_transposed),
        grid=(N_TOK // tm,),
        in_specs=_stream_specs(len(xs), tm) + [
            pl.BlockSpec(mod.shape, const),
            pl.BlockSpec((1, D_MODEL), const),
        ] + [pl.BlockSpec(w_block(n), const, pipeline_mode=pl.Buffered(1)) for n in widths] + side_in_specs,
        out_specs=[pl.BlockSpec((tm, n), lambda i: (i, 0)) for n in widths] + side_out_specs,
        out_shape=[jax.ShapeDtypeStruct((N_TOK, n), F32) for n in widths] + side_shapes,
        compiler_params=_cparams("arbitrary"),
        name="norm_proj",
    )(*xs, mod, g.reshape(1, D_MODEL), *ws, *[w for w, _ in side])


def _post_kernel(*refs, tm, ff_chunk, n_x, n_y, groups, n_side):
    x_refs = refs[:n_x]
    n_m = sum(n for n, _ in groups)
    m_refs = refs[n_x:n_x + n_m]
    mod_ref, g_ref, wo_ref, w1_ref, w2_ref = refs[n_x + n_m:n_x + n_m + 5]
    side_in = refs[n_x + n_m + 5:n_x + n_m + 5 + n_side]
    y_refs = refs[n_x + n_m + 5 + n_side:n_x + n_m + 5 + n_side + n_y]
    _side_cast(side_in, refs[n_x + n_m + 5 + n_side + n_y:])
    row = _mod_row(pl.program_id(0), tm)
    mix, first_ref, first_col = None, 0, 0
    for n, width in groups:
        part = _stream_load(m_refs[first_ref:first_ref + n], tm).astype(BF16)
        term = _dot(part, wo_ref[first_col:first_col + width, :])
        mix = term if mix is None else mix + term
        first_ref, first_col = first_ref + n, first_col + width
    x1 = _stream_load(x_refs, tm) + _mod_slice(mod_ref, row, 2) * mix
    h = _norm_mod(x1, g_ref[...], _mod_slice(mod_ref, row, 4), _mod_slice(mod_ref, row, 3)).astype(BF16)
    acc = jnp.zeros((tm, D_MODEL), F32)
    for k in range(0, D_FF, ff_chunk):
        a = jnp.maximum(_dot(h, w1_ref[:, k:k + ff_chunk]), 0.0)
        acc = acc + _dot((a * a).astype(BF16), w2_ref[k:k + ff_chunk, :])
    y = x1 + _mod_slice(mod_ref, row, 5) * acc
    if n_y == 1:
        y_refs[0][...] = y
    else:
        is_prompt = pl.program_id(0) < N_PROMPT // tm

        @pl.when(is_prompt)
        def _():
            y_refs[0][...] = y

        @pl.when(jnp.logical_not(is_prompt))
        def _():
            y_refs[1][...] = y


def _post_mixer(xs, mixed, mod, g, wo, w1, w2, split_out=False, side=(), tm=512, ff_chunk=1024):
    const = lambda i: (0, 0)
    resident = lambda w: pl.BlockSpec(w.shape, const, pipeline_mode=pl.Buffered(1))
    n_y = 2 if split_out else 1
    rows = (N_PROMPT, N_SAMPLE) if split_out else (N_TOK,)
    groups = tuple((len(grp), grp[0].shape[1]) for grp in mixed)
    mixed_specs = [spec for n, width in groups for spec in _stream_specs(n, tm, width)]
    side_in_specs, side_out_specs, side_shapes = _side_specs(side, N_TOK // tm)
    out = pl.pallas_call(
        functools.partial(_post_kernel, tm=tm, ff_chunk=ff_chunk, n_x=len(xs), n_y=n_y, groups=groups,
                          n_side=len(side)),
        grid=(N_TOK // tm,),
        in_specs=_stream_specs(len(xs), tm) + mixed_specs + [
            pl.BlockSpec(mod.shape, const),
            pl.BlockSpec((1, D_MODEL), const),
            resident(wo), resident(w1), resident(w2),
        ] + side_in_specs,
        out_specs=_stream_specs(n_y, tm) + side_out_specs,
        out_shape=[jax.ShapeDtypeStruct((r, D_MODEL), F32) for r in rows] + side_shapes,
        compiler_params=_cparams("arbitrary"),
        name="post_mixer",
    )(*xs, *[a for grp in mixed for a in grp], mod, g.reshape(1, D_MODEL), wo, w1, w2, *[w for w, _ in side])
    return tuple(out[:n_y]), tuple(out[n_y:])


PAIR = 2 * HD_C


def _pair_consts():
    lane = lax.broadcasted_iota(jnp.int32, (1, PAIR), 1)
    first = lane < HD_C
    ones_col = [jnp.where(lane == HD_C, 1.0, 0.0), jnp.where(lane == 0, 1.0, 0.0)]
    r = lax.broadcasted_iota(jnp.int32, (2 * PAIR, PAIR), 0) % PAIR
    cidx = lax.broadcasted_iota(jnp.int32, (2 * PAIR, PAIR), 1)
    mean2 = jnp.where(r // HD_C == cidx // HD_C, 1.0 / HD_C, 0.0).astype(BF16)
    return first, ones_col, mean2


def _pair_norm(x, w2, mean2):
    hi, lo = _split2(x * x)
    ms = _dot(jnp.concatenate([hi, lo], axis=1), mean2)
    return x * lax.rsqrt(ms + EPS) * w2


def _pair_queries(q, first):
    return [jnp.where(first, q, 0.0).astype(BF16), jnp.where(first, 0.0, q).astype(BF16)]


def _pair_values(v, first, ones_col):
    return [jnp.where(first, v, ones_col[0]).astype(BF16), jnp.where(first, ones_col[1], v).astype(BF16)]


def _pair_output(o_aug, first):
    den = [o_aug[0][:, HD_C:HD_C + 1], o_aug[1][:, 0:1]]
    return jnp.where(first, o_aug[0] / den[0], o_aug[1] / den[1])


def _row_max(*pieces):
    tiles = [p[:, i:i + 128] for p in pieces for i in range(0, p.shape[1], 128)]
    return jnp.max(functools.reduce(jnp.maximum, tiles), axis=-1, keepdims=True)


CTX_PAIRS = 4


def _ctx_attn_kernel(q_ref, k_ref, v_ref, qn_ref, kn_ref, o_ref, kc_ref, vc_ref):
    first, ones_col, mean2 = _pair_consts()
    lanes = [slice(p * PAIR, (p + 1) * PAIR) for p in range(CTX_PAIRS)]
    qn = [_pair_norm(q_ref[:, ln], qn_ref[...], mean2) * HD_C ** -0.5 for ln in lanes]
    kn = [_pair_norm(k_ref[:, ln], kn_ref[...], mean2) for ln in lanes]
    v = [v_ref[:, ln] for ln in lanes]
    kt = [x.T for x in kn]
    for p in range(CTX_PAIRS):
        kc_ref[0, 0, lanes[p], :] = kt[p]
        vc_ref[0, 0, lanes[p], :] = v[p].T
    q = [_pair_queries(x, first) for x in qn]
    va = [_pair_values(x, first, ones_col) for x in v]
    s = [[_dot(q[p][j], kt[p].astype(BF16)) for j in range(2)] for p in range(CTX_PAIRS)]
    pr = [[jnp.exp(x - _row_max(x)).astype(BF16) for x in sp] for sp in s]
    for p in range(CTX_PAIRS):
        o_ref[:, lanes[p]] = _pair_output([_dot(pr[p][j], va[p][j]) for j in range(2)], first)


def _ctx_attention(qkv, qn, kn):
    heads = 2 * CTX_PAIRS
    ng = H_C // heads
    wide = CTX_PAIRS * PAIR
    blk = lambda off: pl.BlockSpec((SEQ, wide), lambda b, p: (b, off + p))
    cache_spec = pl.BlockSpec((1, 1, wide, SEQ), lambda b, p: (b, 0, p, 0))
    cache_shape = jax.ShapeDtypeStruct((BATCH, 1, H_C * HD_C, SEQ), F32)
    return pl.pallas_call(
        _ctx_attn_kernel,
        grid=(BATCH, ng),
        in_specs=[blk(0), blk(ng), blk(2 * ng),
                  pl.BlockSpec((1, PAIR), lambda b, p: (0, 0)),
                  pl.BlockSpec((1, PAIR), lambda b, p: (0, 0))],
        out_specs=[pl.BlockSpec((SEQ, wide), lambda b, p: (b, p)), cache_spec, cache_spec],
        out_shape=[jax.ShapeDtypeStruct((N_PROMPT, D_MODEL), F32), cache_shape, cache_shape],
        compiler_params=_cparams("arbitrary", "arbitrary"),
        name="ctx_attention",
    )(qkv, qkv, qkv, jnp.tile(qn.reshape(1, HD_C), (1, 2)), jnp.tile(kn.reshape(1, HD_C), (1, 2)))


def _na_row_start(r):
    return min(max(r - KH // 2, 0), GRID_ROWS - KH)


NA_ROW_GROUP = 4


def _na_attn_kernel(q_ref, k_ref, v_ref, kc_ref, vc_ref, qn_ref, kn_ref, bias_ref, o_ref, qs, ks, vs, bias_s):
    first, ones_col, mean2 = _pair_consts()

    @pl.when(pl.program_id(1) == 0)
    def _():
        q_col = lax.broadcasted_iota(jnp.int32, (GRID_W, PAIR), 0)
        lane = lax.broadcasted_iota(jnp.int32, (GRID_W, PAIR), 1)
        k_col = lane % GRID_W
        w0 = jnp.clip(q_col - KW // 2, 0, GRID_W - KW)
        outside = jnp.where((k_col >= w0) & (k_col < w0 + KW), 0.0, NEG_INF)
        n_dr = 2 * KH - 1
        for j in range(2):
            band = []
            for dr in range(n_dr):
                row = jnp.broadcast_to(bias_ref[j, dr:dr + 1, :], (GRID_W, PAIR))
                band.append([pltpu.roll(row, (half * GRID_W - (KW - 1)) % PAIR, axis=1, stride=1, stride_axis=0)
                             for half in range(2)])
            zero = jnp.zeros((GRID_W, PAIR), F32)
            for cp in range(2):
                for t in range(KH):
                    lo, hi = 2 * t + cp, 2 * t + cp + 1
                    tile = jnp.where(lane < GRID_W, band[lo][0] if lo < n_dr else zero,
                                     band[hi][1] if hi < n_dr else zero)
                    bias_s[j, cp, :, t * PAIR:(t + 1) * PAIR] = tile + outside

    q2 = _pair_queries(_pair_norm(q_ref[...], qn_ref[...], mean2) * HD_C ** -0.5, first)
    v2 = _pair_values(v_ref[...], first, ones_col)
    ks[...] = _pair_norm(k_ref[...], kn_ref[...], mean2).astype(BF16)
    for j in range(2):
        qs[j] = q2[j]
        vs[j] = v2[j]
    kt_ctx = kc_ref[0, 0].astype(BF16)
    vt = vc_ref[0, 0]
    ch = lax.broadcasted_iota(jnp.int32, vt.shape, 0)
    vt_ctx = [jnp.where(ch < HD_C, vt, jnp.where(ch == HD_C, 1.0, 0.0)).astype(BF16),
              jnp.where(ch < HD_C, jnp.where(ch == 0, 1.0, 0.0), vt).astype(BF16)]
    for r0 in range(0, GRID_ROWS, NA_ROW_GROUP):
        units = [(r, j) for r in range(r0, r0 + NA_ROW_GROUP) for j in range(2)]
        rows = {r: slice(r * GRID_W, (r + 1) * GRID_W) for r, _ in units}
        wins = {r: slice(_na_row_start(r) * GRID_W, (_na_row_start(r) + KH) * GRID_W) for r, _ in units}
        s_ctx_all = [_dot(qs[j, r0 * GRID_W:(r0 + NA_ROW_GROUP) * GRID_W, :], kt_ctx) for j in range(2)]
        s_ctx = [s_ctx_all[j][(r - r0) * GRID_W:(r - r0 + 1) * GRID_W] for r, j in units]
        s_win = []
        for r, j in units:
            dr0 = KH - 1 - (r - _na_row_start(r))
            lane0 = (dr0 - dr0 % 2) * GRID_W
            s_win.append(_dot_nt(qs[j, rows[r], :], ks[wins[r], :])
                         + bias_s[j, dr0 % 2, :, lane0:lane0 + KH * GRID_W])
        m = [_row_max(a, b) for a, b in zip(s_win, s_ctx)]
        p_win = [jnp.exp(a - mm).astype(BF16) for a, mm in zip(s_win, m)]
        p_ctx = [jnp.exp(b - mm).astype(BF16) for b, mm in zip(s_ctx, m)]
        o_aug = [_dot(p_win[i], vs[j, wins[r], :]) + _dot_nt(p_ctx[i], vt_ctx[j]) for i, (r, j) in enumerate(units)]
        for i in range(0, len(units), 2):
            o_ref[rows[units[i][0]], :] = _pair_output(o_aug[i:i + 2], first)


NA_BIAS_LANES = 2 * KH * GRID_W


def _na_attention(qkv, cache_kt, cache_vt, qn, kn, rpb):
    nhp = H_C // 2
    row0 = N_PROMPT // DEC_SEQ
    blk = lambda off: pl.BlockSpec((DEC_SEQ, 2 * HD_C), lambda p, b: (row0 + b, off + p))
    cache_spec = pl.BlockSpec((1, 1, PAIR, PAST_LEN), lambda p, b: (b, 0, p, 0))
    rpb_rows = 2 * KH
    bias = jnp.pad(rpb.astype(F32), ((0, 0), (0, rpb_rows - rpb.shape[1]), (0, PAIR - rpb.shape[2])))
    return pl.pallas_call(
        _na_attn_kernel,
        grid=(nhp, DEC_BATCH),
        in_specs=[blk(0), blk(nhp), blk(2 * nhp), cache_spec, cache_spec,
                  pl.BlockSpec((1, PAIR), lambda p, b: (0, 0)),
                  pl.BlockSpec((1, PAIR), lambda p, b: (0, 0)),
                  pl.BlockSpec((2, rpb_rows, PAIR), lambda p, b: (p, 0, 0))],
        out_specs=pl.BlockSpec((DEC_SEQ, 2 * HD_C), lambda p, b: (b, p)),
        out_shape=jax.ShapeDtypeStruct((N_SAMPLE, D_MODEL), F32),
        scratch_shapes=[pltpu.VMEM((2, DEC_SEQ, PAIR), BF16), pltpu.VMEM((DEC_SEQ, PAIR), BF16),
                        pltpu.VMEM((2, DEC_SEQ, PAIR), BF16), pltpu.VMEM((2, 2, GRID_W, NA_BIAS_LANES), F32)],
        compiler_params=_cparams("arbitrary", "arbitrary"),
        name="na_attention",
    )(qkv, qkv, qkv, cache_kt, cache_vt, jnp.tile(qn.reshape(1, HD_C), (1, 2)), jnp.tile(kn.reshape(1, HD_C), (1, 2)),
      bias)


def _seq_layout(prompt):
    return (SEQ, BATCH, 0) if prompt else (DEC_SEQ, DEC_BATCH, N_PROMPT // DEC_SEQ)


def _flip_blocks(m, c):
    r, s = m.shape
    return m.reshape(r // c, c, s // c, c)[:, ::-1, :, ::-1].reshape(r, s)


def _rms_gate(x, gn, gate):
    ms = jnp.mean(x * x, axis=-1, keepdims=True)
    return x * lax.rsqrt(ms + EPS) * gn * _silu(gate)


HG_LEVELS = tuple(CHUNK >> (i + 1) for i in range(CHUNK.bit_length() - 1))
HG_NL = len(HG_LEVELS)
HG_STACK = (HG_NL + 1) * CHUNK
TOT_ROWS = 16
HG_ROWS = (HG_NL + 2) * CHUNK + TOT_ROWS


def _hgrn_consts():
    c = CHUNK
    level_rows = []
    mask = np.zeros((HG_STACK, HG_STACK), np.float32)
    mask[:c, :c] = np.eye(c)
    for li, b in enumerate(HG_LEVELS):
        m = np.zeros((c, c), np.float32)
        blk = np.zeros((c, c), np.float32)
        for t in range(c):
            mid = (t // (2 * b)) * 2 * b + b
            if t >= mid:
                m[t, mid:t + 1] = 1.0
                blk[t, mid - b:mid] = 1.0
            else:
                m[t, t + 1:mid] = 1.0
        level_rows.append(m)
        mask[(li + 1) * c:(li + 2) * c, (li + 1) * c:(li + 2) * c] = blk
    dq = np.tril(np.ones((c, c), np.float32))
    dk = np.triu(np.ones((c, c), np.float32), 1)
    body = np.concatenate(level_rows + [dq, dk], axis=0)
    tot = np.ones((TOT_ROWS, c), np.float32)
    mcs, masks = [], []
    for reverse in (False, True):
        bm = _flip_blocks(body, c) if reverse else body
        mk = _flip_blocks(mask, c) if reverse else mask
        mc = np.concatenate([bm, tot], axis=0)
        mcs.append(np.concatenate([mc, mc], axis=1))
        masks.append(mk)
    return jnp.asarray(np.stack(mcs), BF16), jnp.asarray(np.stack(masks), F32)


HG_FAST = 64
HG_HALF = HG_FAST // 2
HG_FAST_ROWS = 4 * HG_FAST + TOT_ROWS
HG_SAFE_EXP = 40.0
HG_FAST_STEPS = 4


def _hgrn_fast_consts(seq):
    c, m = HG_FAST, HG_HALF
    aq = np.zeros((c, c), np.float32)
    for t in range(c):
        if t >= m:
            aq[t, m:t + 1] = 1.0
        else:
            aq[t, t + 1:m] = -1.0
    dq = np.tril(np.ones((c, c), np.float32))
    dk = np.triu(np.ones((c, c), np.float32), 1)
    body = np.concatenate([aq, -aq, dq, dk], axis=0)
    tot = np.ones((TOT_ROWS, c), np.float32)
    causal = np.tril(np.ones((c, c), np.float32))
    mfs, masks = [], []
    for reverse in (False, True):
        bm = _flip_blocks(body, c) if reverse else body
        mf = np.concatenate([bm, tot], axis=0)
        mfs.append(np.concatenate([mf, mf], axis=1))
        masks.append(causal.T if reverse else causal)
    n_half = seq // m
    half = np.zeros((max(n_half, 16), seq), np.float32)
    for i in range(n_half):
        half[i, i * m:(i + 1) * m] = 1.0
    return jnp.asarray(np.stack(mfs), BF16), jnp.asarray(np.stack(masks), F32), jnp.asarray(half, BF16)


def _hgrn_kernel(*refs, seq, has_s0, emit_state):
    it = iter(refs)
    qa_ref, ff_ref, fb_ref, ia_ref, ga_ref, lb_ref, gn_ref, mc_ref, mask_ref = [next(it) for _ in range(9)]
    mf_ref, causal_ref, half_ref = [next(it) for _ in range(3)]
    s0_ref = next(it) if has_s0 else None
    o_ref = next(it)
    st_ref = next(it) if emit_state else None
    s_scr, acc, f_s, lf_s = [next(it) for _ in range(4)]
    c = CHUNK
    n_chunks = seq // c
    combos = [(d, h) for d in range(2) for h in range(H_A)]
    lanes = [slice(h * DK_A, (h + 1) * DK_A) for h in range(H_A)]
    add = lambda a, b: a + b

    lb_raw = lb_ref[...]
    lb_e = jnp.exp(lb_raw - jnp.max(lb_raw, axis=0, keepdims=True))
    lb_all = lb_e[0:1] / jnp.sum(lb_e, axis=0, keepdims=True)

    for d in range(2):
        for h in range(H_A):
            s_scr[d, h] = s0_ref[0, 0, d, h].T if has_s0 else jnp.zeros((DV_A, DK_A), F32)
    acc[...] = jnp.zeros(acc.shape, F32)

    worst = []
    for d, fr_ref in enumerate((ff_ref, fb_ref)):
        f = lb_all + (1.0 - lb_all) * _sigmoid(fr_ref[...])
        lf = jnp.log(f)
        f_s[d] = f
        lf_s[d] = lf
        worst.append(jnp.max(_dot(half_ref[...], (-lf).astype(BF16))))
    safe = jnp.maximum(worst[0], worst[1]) <= HG_SAFE_EXP

    def fast_body(n, carry):
        cf = HG_FAST
        n_fast = seq // cf
        steps = range(HG_FAST_STEPS)
        chunk = lambda d, t: (n * HG_FAST_STEPS + t) if d == 0 else (n_fast - 1 - n * HG_FAST_STEPS - t)
        rows = [[pl.ds(pl.multiple_of(chunk(d, t) * cf, cf), cf) for t in steps] for d in range(2)]
        units = [(t, d, h) for t in steps for d, h in combos]
        e_all = [[jnp.exp(_dot_const(mf_ref[d], lf_s[d, rows[d][t], :])) for t in steps] for d in range(2)]
        q_all = [[_silu(qa_ref[rows[d][t], :]) * DK_A ** -0.5 for t in steps] for d in range(2)]
        k_all = [[1.0 - f_s[d, rows[d][t], :] for t in steps] for d in range(2)]
        v_all = [[ia_ref[rows[d][t], :].astype(BF16) for t in steps] for d in range(2)]
        qs = {u: q_all[u[1]][u[0]][:, lanes[u[2]]] for u in units}
        ks = {u: k_all[u[1]][u[0]][:, lanes[u[2]]] for u in units}
        vs = {u: v_all[u[1]][u[0]][:, lanes[u[2]]] for u in units}
        es = {u: [e_all[u[1]][u[0]][i * cf:(i + 1) * cf, lanes[u[2]]] for i in range(4)] for u in units}
        p = {u: jnp.where(causal_ref[u[1]] > 0.0,
                          _dot_nt((qs[u] * es[u][0]).astype(BF16), (ks[u] * es[u][1]).astype(BF16)), 0.0).astype(BF16)
             for u in units}
        intra = {u: _dot(p[u], vs[u]) for u in units}
        upd = {u: _dot_tn(vs[u], (ks[u] * es[u][3]).astype(BF16)) for u in units}
        qdec = {u: (qs[u] * es[u][2]).astype(BF16) for u in units}
        st = {(d, h): s_scr[d, h] for d, h in combos}
        o = {}
        for t in steps:
            for d, h in combos:
                u = (t, d, h)
                o[u] = intra[u] + _dot_nt(qdec[u], st[d, h].astype(BF16))
                st[d, h] = st[d, h] * e_all[d][t][4 * cf:4 * cf + 1, lanes[h]] + upd[u]
        for t in steps:
            for d in range(2):
                acc[rows[d][t], :] += jnp.concatenate([o[t, d, h] for h in range(H_A)], axis=1)
        for d, h in combos:
            s_scr[d, h] = st[d, h]
        return carry

    def body(n, carry):
        rows = [pl.ds(pl.multiple_of((n if d == 0 else n_chunks - 1 - n) * c, c), c) for d in range(2)]
        f_all = [f_s[d, rows[d], :] for d in range(2)]
        e_all = [jnp.exp(_dot_const(mc_ref[d], lf_s[d, rows[d], :])) for d in range(2)]
        q_all = [_silu(qa_ref[rows[d], :]) * DK_A ** -0.5 for d in range(2)]
        v_all = [ia_ref[rows[d], :].astype(BF16) for d in range(2)]
        st = [s_scr[d, h] for d, h in combos]
        qs, ks, vs, es = [], [], [], []
        for d, h in combos:
            qs.append(q_all[d][:, lanes[h]])
            ks.append(1.0 - f_all[d][:, lanes[h]])
            vs.append(v_all[d][:, lanes[h]])
            es.append(e_all[d][:, lanes[h]])
        lvl = [[e[i * c:(i + 1) * c] for i in range(HG_NL + 2)] for e in es]
        qst = [jnp.concatenate([q] + [q * l[i] for i in range(HG_NL)], axis=0).astype(BF16) for q, l in zip(qs, lvl)]
        kst = [jnp.concatenate([k] + [k * l[i] for i in range(HG_NL)], axis=0).astype(BF16) for k, l in zip(ks, lvl)]
        r = [(_dot_nt(qst[i], kst[i]) * mask_ref[d]).astype(BF16) for i, (d, h) in enumerate(combos)]
        ost = [_dot(r[i], jnp.concatenate([vs[i]] * (HG_NL + 1), axis=0)) for i in range(len(combos))]
        inter = [_dot_nt((qs[i] * lvl[i][HG_NL]).astype(BF16), st[i].astype(BF16)) for i in range(len(combos))]
        upd = [_dot_tn(vs[i], (ks[i] * lvl[i][HG_NL + 1]).astype(BF16)) for i in range(len(combos))]
        o = [functools.reduce(lambda a, b: a + b, [ost[i][j * c:(j + 1) * c] for j in range(HG_NL + 1)]) + inter[i]
             for i in range(len(combos))]
        for d in range(2):
            acc[rows[d], :] += jnp.concatenate(o[d * H_A:(d + 1) * H_A], axis=1)
        for i, (d, h) in enumerate(combos):
            e_tot = es[i][(HG_NL + 2) * c:(HG_NL + 2) * c + 1]
            s_scr[d, h] = st[i] * e_tot + upd[i]
        return carry

    @pl.when(safe)
    def _():
        lax.fori_loop(0, seq // (HG_FAST * HG_FAST_STEPS), fast_body, 0)

    @pl.when(jnp.logical_not(safe))
    def _():
        lax.fori_loop(0, n_chunks, body, 0)

    for h in range(H_A):
        ln = slice(h * DV_A, (h + 1) * DV_A)
        o_ref[:, ln] = _rms_gate(acc[:, ln], gn_ref[...], ga_ref[:, ln])
    if emit_state:
        for d in range(2):
            for h in range(H_A):
                st_ref[0, 0, d, h] = s_scr[d, h].T


def _hgrn(proj, hgrn_lb, gn, consts, prompt, s0=None, layer=0):
    seq, nb, rb0 = _seq_layout(prompt)
    consts = list(consts) + list(_hgrn_fast_consts(seq))
    wa = H_A * DK_A
    blk = lambda j: pl.BlockSpec((seq, wa), lambda b: (rb0 + b, j))
    const2 = lambda b: (0, 0)
    st_block = (1, 1, 2, H_A, DK_A, DV_A)
    in_specs = [blk(0), blk(1), blk(2), blk(3), blk(4),
                pl.BlockSpec(hgrn_lb.shape, const2), pl.BlockSpec((1, DV_A), const2)]
    in_specs += [pl.BlockSpec(m.shape, lambda b, nd=m.ndim: (0,) * nd) for m in consts]
    args = [proj] * 5 + [hgrn_lb, gn.reshape(1, DV_A)] + consts
    if s0 is not None:
        in_specs.append(pl.BlockSpec(st_block, lambda b: (b, layer, 0, 0, 0, 0)))
        args.append(s0)
    out_specs = [pl.BlockSpec((seq, wa), lambda b: (b, 0))]
    out_shape = [jax.ShapeDtypeStruct((nb * seq, wa), F32)]
    if prompt:
        out_specs.append(pl.BlockSpec(st_block, lambda b: (b, 0, 0, 0, 0, 0)))
        out_shape.append(jax.ShapeDtypeStruct((nb, 1, 2, H_A, DK_A, DV_A), F32))
    return pl.pallas_call(
        functools.partial(_hgrn_kernel, seq=seq, has_s0=s0 is not None, emit_state=prompt),
        grid=(nb,),
        in_specs=in_specs,
        out_specs=out_specs,
        out_shape=out_shape,
        scratch_shapes=[pltpu.VMEM((2, H_A, DV_A, DK_A), F32), pltpu.VMEM((seq, wa), F32),
                        pltpu.VMEM((2, seq, wa), F32), pltpu.VMEM((2, seq, wa), F32)],
        compiler_params=_cparams("arbitrary"),
        name="hgrn_prompt" if prompt else "hgrn_sample",
    )(*args)


GD_SUB = GBLK // CHUNK
GD_BLOCKS_PER_ITER = 2
GD_PROMPT_SEQS = 2
GD_ROWS = 2 * GBLK + TOT_ROWS


def _gdn_consts():
    n, c = GBLK, CHUNK
    same = (np.arange(n)[:, None] // c) == (np.arange(n)[None, :] // c)
    tri = (same & (np.arange(n)[None, :] <= np.arange(n)[:, None])).astype(np.float32)
    sup = (same & (np.arange(n)[None, :] > np.arange(n)[:, None])).astype(np.float32)
    tot = np.zeros((TOT_ROWS, n), np.float32)
    for s in range(GD_SUB):
        tot[s, s * c:(s + 1) * c] = 1.0
    mgs, tts, tris = [], [], []
    for reverse in (False, True):
        t = _flip_blocks(tri, c) if reverse else tri
        s = _flip_blocks(sup, c) if reverse else sup
        mg = np.concatenate([t, s, tot], axis=0)
        mgs.append(np.concatenate([mg, mg], axis=1))
        tts.append(np.concatenate([t.T, t.T], axis=0))
        tris.append(t)
    tris.append(same.astype(np.float32))
    return jnp.asarray(np.stack(mgs), BF16), jnp.asarray(np.stack(tts), BF16), jnp.asarray(np.stack(tris), F32)


def _softplus(x):
    return jnp.maximum(x, 0.0) + jnp.log(1.0 + jnp.exp(-jnp.abs(x)))


CONV_PAD = 8


def _conv_silu(x, w, seq):
    half = SHORT_CONV // 2
    pad = jnp.zeros((CONV_PAD, x.shape[1]), x.dtype)
    xe = jnp.concatenate([pad, x, pad], axis=0)
    acc = xe * w[half:half + 1]
    for j in range(SHORT_CONV):
        shift = half - j
        if shift != 0:
            acc = acc + pltpu.roll(xe, shift % (seq + 2 * CONV_PAD), axis=0) * w[j:j + 1]
    return _silu(acc[CONV_PAD:seq + CONV_PAD])


def _l2norm_heads(x, n_heads, width, scale):
    outs = []
    for h in range(n_heads):
        xh = x[:, h * width:(h + 1) * width]
        outs.append(xh * (lax.rsqrt(jnp.sum(xh * xh, axis=-1, keepdims=True) + EPS) * scale))
    return jnp.concatenate(outs, axis=-1)


def _gdn_kernel(*refs, seq, n_seq, has_s0, emit_state):
    it = iter(refs)
    (q_ref, k_ref, v_ref, gb_ref, gate_ref, cw_ref, alog_ref, dt_ref, gn_ref,
     mg_ref, tt_ref, tri_ref) = [next(it) for _ in range(12)]
    s0_ref = next(it) if has_s0 else None
    o_ref = next(it)
    st_ref = next(it) if emit_state else None
    qn, kn, vn, u_s, w_s, qg_s, kdt_s, at_s, et_s, s_scr, acc = [next(it) for _ in range(11)]
    c = CHUNK
    n_chunks = seq // c
    n_blocks = n_seq * seq // GBLK
    wq = H_B * DK_B
    n_dh = 2 * H_B
    combos = [(d, h) for d in range(2) for h in range(H_B)]
    lanes = [slice(h * DK_B, (h + 1) * DK_B) for h in range(H_B)]

    for s in range(n_seq):
        sr = slice(s * seq, (s + 1) * seq)
        qn[sr, :] = _l2norm_heads(_conv_silu(q_ref[sr, :], cw_ref[:, 0:wq], seq), H_B, DK_B, DK_B ** -0.5)
        kn[sr, :] = _l2norm_heads(_conv_silu(k_ref[sr, :], cw_ref[:, wq:2 * wq], seq), H_B, DK_B, 1.0)
        vn[sr, :] = _conv_silu(v_ref[sr, :], cw_ref[:, 2 * wq:3 * wq], seq)
        for i in range(n_dh):
            s_scr[s * n_dh + i] = (s0_ref[s, 0, i // H_B, i % H_B] if has_s0 else jnp.zeros((DK_B, DV_B), F32))
    acc[...] = jnp.zeros(acc.shape, F32)

    eye = (lax.broadcasted_iota(jnp.int32, (GBLK, GBLK), 0)
           == lax.broadcasted_iota(jnp.int32, (GBLK, GBLK), 1)).astype(F32)
    eye_pk = (lax.broadcasted_iota(jnp.int32, (c, GBLK), 0)
              == lax.broadcasted_iota(jnp.int32, (c, GBLK), 1) % c).astype(F32)
    bwd_lane = lax.broadcasted_iota(jnp.int32, (1, 128), 1) % n_dh >= H_B
    add = lambda a, b: a + b

    same_chunk = tri_ref[2].astype(BF16)

    def expand(pk):
        return jnp.concatenate([pk] * GD_SUB, axis=0) * same_chunk

    def pack(bd):
        return functools.reduce(add, [bd[s * c:(s + 1) * c] for s in range(GD_SUB)])

    def weights(hi, lo):
        return jnp.concatenate([expand(hi), expand(lo)], axis=1)

    def dot3_split(a_hi, a_lo, w2):
        m, n = a_hi.shape[0], w2.shape[1] // 2
        t = _dot(jnp.concatenate([a_hi, a_lo], axis=0), w2)
        return t[:m, :n] + t[m:, :n] + t[:m, n:]

    def block_body(it, carry):
        blks = [it * GD_BLOCKS_PER_ITER + o for o in range(GD_BLOCKS_PER_ITER)]
        units = [(o, d, h) for o in range(GD_BLOCKS_PER_ITER) for d, h in combos]
        idx = lambda d, h: d * H_B + h
        col = lambda x, j: jnp.broadcast_to(x[:, j:j + 1], (GBLK, DK_B))
        rows = [pl.ds(pl.multiple_of(b * GBLK, GBLK), GBLK) for b in blks]
        gates = [gate_ref[r, :] for r in rows]
        glog_all = [-jnp.exp(alog_ref[...]) * _softplus(x + dt_ref[...]) for x in gates]
        beta_all = [_sigmoid(x) for x in gates]
        g2 = [jnp.concatenate(_split2(x), axis=0) for x in glog_all]
        dg = [[_dot(mg_ref[d], x) for d in range(2)] for x in g2]
        dsel = [jnp.where(bwd_lane, x[1], x[0]) for x in dg]
        eg_all = [jnp.exp(x) for x in dsel]
        gt = [[_dot_tn(x, tt_ref[d]) for d in range(2)] for x in g2]
        qs = [[qn[r, ln] for ln in lanes] for r in rows]
        ks = [[kn[r, ln] for ln in lanes] for r in rows]
        vs = [[vn[r, ln] for ln in lanes] for r in rows]
        betas = [col(beta_all[o], n_dh + idx(d, h)) for o, d, h in units]
        kbs = [ks[o][h] * betas[u] for u, (o, d, h) in enumerate(units)]
        kb_of = {unit: kbs[u] for u, unit in enumerate(units)}
        kk = {(o, h): _dot_nt(jnp.concatenate([qs[o][h], kb_of[o, 0, h], kb_of[o, 1, h]], axis=0).astype(BF16),
                              ks[o][h].astype(BF16))
              for o in range(GD_BLOCKS_PER_ITER) for h in range(H_B)}
        decay = []
        for o, d, h in units:
            inside = tri_ref[d] > 0.0
            gd = col(dsel[o][:GBLK], idx(d, h)) - gt[o][d][idx(d, h):idx(d, h) + 1, :]
            decay.append(jnp.where(inside, jnp.exp(jnp.where(inside, gd, 0.0)), 0.0))
        attn = [kk[o, h][:GBLK] * decay[u] for u, (o, d, h) in enumerate(units)]
        p_pk = [pack(kk[o, h][(1 + d) * GBLK:(2 + d) * GBLK] * decay[u] * (1.0 - eye))
                for u, (o, d, h) in enumerate(units)]
        x_pk = [eye_pk - p for p in p_pk]
        p_sp = [_split2(p) for p in p_pk]
        p_w = [weights(*s) for s in p_sp]
        for _ in range(CHUNK.bit_length() - 2):
            p_pk = [dot3_split(*s, w) for s, w in zip(p_sp, p_w)]
            p_sp = [_split2(p) for p in p_pk]
            p_w = [weights(*s) for s in p_sp]
            x_pk = [x + dot3_split(*_split2(x), w) for x, w in zip(x_pk, p_w)]
        eg_col = [col(eg_all[o][:GBLK], idx(d, h)) for o, d, h in units]
        ekd_col = [col(eg_all[o][GBLK:2 * GBLK], idx(d, h)) for o, d, h in units]
        rhs = [_split2(jnp.concatenate([vs[o][h] * betas[u], kbs[u] * eg_col[u]], axis=1))
               for u, (o, d, h) in enumerate(units)]
        t_sp = [[expand(part) for part in _split2(x)] for x in x_pk]
        uw = [_dot(t[0], r[0]) + (_dot(t[0], r[1]) + _dot(t[1], r[0])) for t, r in zip(t_sp, rhs)]
        for u, (o, d, h) in enumerate(units):
            i = idx(d, h)
            qg = (qs[o][h] * eg_col[u]).astype(BF16)
            kdt_s[i, blks[o]] = (ks[o][h] * ekd_col[u]).T.astype(BF16)
            for s in range(GD_SUB):
                cn = blks[o] * GD_SUB + s
                r = slice(s * c, (s + 1) * c)
                u_s[i, cn] = uw[u][r, :DV_B]
                w_s[i, cn] = uw[u][r, DV_B:].astype(BF16)
                qg_s[i, cn] = qg[r]
                at_s[i, cn] = attn[u][r].astype(BF16)
                et_s[i, cn] = jnp.broadcast_to(eg_all[o][2 * GBLK + s:2 * GBLK + s + 1, i:i + 1], (8, DV_B))
        return carry

    n_iter = n_blocks // GD_BLOCKS_PER_ITER
    lax.fori_loop(0, n_iter, block_body, 0, unroll=True)

    def chunk_body(n, carry):
        chains = [(s, d, h) for s in range(n_seq) for d, h in combos]
        idx = lambda d, h: d * H_B + h
        cn = {(s, d): s * n_chunks + (n if d == 0 else n_chunks - 1 - n) for s in range(n_seq) for d in range(2)}
        rows = {key: pl.ds(pl.multiple_of(v * c, c), c) for key, v in cn.items()}
        sub_of_row = lax.broadcasted_iota(jnp.int32, (GBLK, 1), 0) // c
        in_chunk = {key: sub_of_row == v % GD_SUB for key, v in cn.items()}
        st = {ch: s_scr[ch[0] * n_dh + idx(ch[1], ch[2])] for ch in chains}
        ws = {(s, d, h): _dot(jnp.concatenate([w_s[idx(d, h), cn[s, d]], qg_s[idx(d, h), cn[s, d]]], axis=0),
                              st[s, d, h].astype(BF16)) for s, d, h in chains}
        vblk = {(s, d, h): jnp.where(in_chunk[s, d],
                                     jnp.concatenate([u_s[idx(d, h), cn[s, d]] - ws[s, d, h][:c]] * GD_SUB, axis=0),
                                     0.0).astype(BF16) for s, d, h in chains}
        r = {(s, d, h): _dot(jnp.concatenate([at_s[idx(d, h), cn[s, d]], kdt_s[idx(d, h), cn[s, d] // GD_SUB]], axis=0),
                             vblk[s, d, h]) for s, d, h in chains}
        for s in range(n_seq):
            for d in range(2):
                acc[rows[s, d], :] += jnp.concatenate([ws[s, d, h][c:] + r[s, d, h][:c] for h in range(H_B)], axis=1)
        for s, d, h in chains:
            s_scr[s * n_dh + idx(d, h)] = st[s, d, h] * et_s[idx(d, h), cn[s, d]][0:1] + r[s, d, h][c:]
        return carry

    lax.fori_loop(0, n_chunks, chunk_body, 0)

    for h in range(H_B):
        ln = slice(h * DV_B, (h + 1) * DV_B)
        o_ref[:, ln] = _rms_gate(acc[:, ln], gn_ref[...], gb_ref[:, ln])
    if emit_state:
        for s in range(n_seq):
            for i in range(n_dh):
                st_ref[s, 0, i // H_B, i % H_B] = s_scr[s * n_dh + i]


def _gdn(proj, gates, conv_w, a_log, dt_bias, gn, consts, prompt, s0=None, layer=0):
    seq, nb, rb0 = _seq_layout(prompt)
    n_seq = GD_PROMPT_SEQS if prompt else 1
    rows = n_seq * seq
    n_chunks = rows // CHUNK
    wq = H_B * DK_B
    blk = lambda j: pl.BlockSpec((rows, wq), lambda b: (rb0 + b, j))
    const2 = lambda b: (0, 0)
    const3 = lambda b: (0, 0, 0)
    st_block = (n_seq, 1, 2, H_B, DK_B, DV_B)
    pad_row = lambda p: jnp.pad(p.reshape(1, -1).astype(F32), ((0, 0), (0, 128 - p.size)))
    in_specs = [blk(5), blk(6), blk(7), blk(8),
                pl.BlockSpec((rows, 128), lambda b: (rb0 + b, 0)),
                pl.BlockSpec((SHORT_CONV, 3 * wq), const2),
                pl.BlockSpec((1, 128), const2), pl.BlockSpec((1, 128), const2), pl.BlockSpec((1, DV_B), const2)]
    in_specs += [pl.BlockSpec(m.shape, const3) for m in consts]
    args = [proj] * 4 + [gates, conv_w.reshape(SHORT_CONV, 3 * wq), pad_row(a_log), pad_row(dt_bias),
                         gn.reshape(1, DV_B)] + list(consts)
    if s0 is not None:
        in_specs.append(pl.BlockSpec(st_block, lambda b: (b, layer, 0, 0, 0, 0)))
        args.append(s0)
    out_specs = [pl.BlockSpec((rows, wq), lambda b: (b, 0))]
    out_shape = [jax.ShapeDtypeStruct((nb * seq, wq), F32)]
    if prompt:
        out_specs.append(pl.BlockSpec(st_block, lambda b: (b, 0, 0, 0, 0, 0)))
        out_shape.append(jax.ShapeDtypeStruct((nb, 1, 2, H_B, DK_B, DV_B), F32))
    n_dh = 2 * H_B
    scratch = ([pltpu.VMEM((rows, wq), F32)] * 3
               + [pltpu.VMEM((n_dh, n_chunks, CHUNK, DV_B), F32)]
               + [pltpu.VMEM((n_dh, n_chunks, CHUNK, DK_B), BF16)] * 2
               + [pltpu.VMEM((n_dh, rows // GBLK, DK_B, GBLK), BF16),
                  pltpu.VMEM((n_dh, n_chunks, CHUNK, GBLK), BF16),
                  pltpu.VMEM((n_dh, n_chunks, 8, DV_B), F32),
                  pltpu.VMEM((n_seq * n_dh, DK_B, DV_B), F32),
                  pltpu.VMEM((rows, wq), F32)])
    return pl.pallas_call(
        functools.partial(_gdn_kernel, seq=seq, n_seq=n_seq, has_s0=s0 is not None, emit_state=prompt),
        grid=(nb // n_seq,),
        in_specs=in_specs,
        out_specs=out_specs,
        out_shape=out_shape,
        scratch_shapes=scratch,
        compiler_params=_cparams("arbitrary"),
        name="gdn_prompt" if prompt else "gdn_sample",
    )(*args)


def kernel(x_prompt, x_sample, state_hgrn, state_gdn, cache_na_k, cache_na_v, c, c_ctx, ada_w, ada_b, norm_g, w_in_ab, w_out_ab, hgrn_lb, gdn_conv, gdn_a_log, gdn_dt_bias, gn_hgrn, gn_gdn, w_qkv_na, qn_na, kn_na, rpb_na, w_out_na, w_mlp1, w_mlp2):
    cond = jnp.concatenate([c_ctx[None, :], c, jnp.zeros((N_MOD_ROWS - 1 - DEC_BATCH, D_MODEL), F32)], axis=0)
    mods, w_in_t = _modulation(cond, ada_w, ada_b, jnp.swapaxes(w_in_ab, 1, 2))
    xs = (x_prompt.reshape(N_PROMPT, D_MODEL), x_sample.reshape(N_SAMPLE, D_MODEL))

    w_gate_t = jnp.pad(w_in_t[D_MAIN_AB:], ((0, 128 - N_GATE_AB), (0, 0)))
    proj, gates, wo0, w1_0, w2_0 = _norm_proj(xs, mods[0], norm_g[0, 0], [w_in_t, w_gate_t], widths=[D_MAIN_AB, 128],
                                              side=((w_out_ab, 0), (w_mlp1, 0), (w_mlp2, 0)), w_transposed=True)
    hg_consts = _hgrn_consts()
    gd_consts = _gdn_consts()
    hg_prompt, new_hgrn = _hgrn(proj, hgrn_lb, gn_hgrn[0], hg_consts, True)
    hg_sample, = _hgrn(proj, hgrn_lb, gn_hgrn[0], hg_consts, False, s0=state_hgrn)
    gd_args = (gdn_conv[0], gdn_a_log[0], gdn_dt_bias[0], gn_gdn[0], gd_consts)
    gd_prompt, new_gdn = _gdn(proj, gates, *gd_args, True)
    gd_sample, = _gdn(proj, gates, *gd_args, False, s0=state_gdn)
    xs, (w_qkv, wo1, w1_1, w2_1) = _post_mixer(xs, [(hg_prompt, hg_sample), (gd_prompt, gd_sample)], mods[0],
                                               norm_g[0, 1], wo0, w1_0, w2_0,
                                               side=((w_qkv_na, 0), (w_out_na, 0), (w_mlp1, 1), (w_mlp2, 1)))

    qkv, = _norm_proj(xs, mods[1], norm_g[1, 0], [w_qkv])
    at_prompt, new_kt, new_vt = _ctx_attention(qkv, qn_na[0], kn_na[0])
    time_minor = lambda a: jnp.swapaxes(a, -1, -2).reshape(a.shape[0], 1, H_C * HD_C, a.shape[3])
    at_sample = _na_attention(qkv, time_minor(cache_na_k), time_minor(cache_na_v), qn_na[0], kn_na[0], rpb_na[0])
    time_major = lambda a: jnp.swapaxes(a.reshape(BATCH, 1, H_C, HD_C, SEQ), -1, -2)
    new_k, new_v = time_major(new_kt), time_major(new_vt)
    (y_prompt, y_sample), _ = _post_mixer(xs, [(at_prompt, at_sample)], mods[1], norm_g[1, 1], wo1, w1_1, w2_1,
                                          split_out=True)

    return (y_prompt.reshape(BATCH, SEQ, D_MODEL), y_sample.reshape(DEC_BATCH, DEC_SEQ, D_MODEL),
            new_hgrn, new_gdn, new_k, new_v)
```

```python
import functools

import numpy as np
import jax
import jax.numpy as jnp
from jax import lax
from jax.experimental import pallas as pl
from jax.experimental.pallas import tpu as pltpu

F32 = jnp.float32
BF16 = jnp.bfloat16

D_MODEL = 1024
BATCH = 16
SEQ = 256
DEC_BATCH = 4
DEC_SEQ = 1024
PAST_LEN = 256
N_PROMPT = BATCH * SEQ
N_SAMPLE = DEC_BATCH * DEC_SEQ
N_TOK = N_PROMPT + N_SAMPLE
GRID_W = 64
GRID_ROWS = DEC_SEQ // GRID_W
H_A = 4
DK_A = 128
DV_A = 128
H_B = 4
DK_B = 128
DV_B = 128
SHORT_CONV = 5
H_C = 16
HD_C = 64
KH = 8
KW = 16
D_FF = 4 * D_MODEL
EPS = 1e-6
NEG_INF = -1e30
LANES = 128
SUBLANES = 8
BF16_ROWS = 16
VMEM_LIMIT = 56 * 1024 * 1024

N_MOD_ROWS = SUBLANES
N_GATE_AB = 16
D_MAIN_AB = 3 * H_A * DK_A + 2 * H_A * DV_A + H_B * (2 * DK_B + DV_B) + H_B * DV_B
CHUNK = 32
GBLK = LANES


def _cparams(*sem):
    return pltpu.CompilerParams(dimension_semantics=sem, vmem_limit_bytes=VMEM_LIMIT)


def _sigmoid(x):
    return 0.5 * jnp.tanh(0.5 * x) + 0.5


def _silu(x):
    return x * _sigmoid(x)


def _dot(a, b):
    return jnp.dot(a, b, preferred_element_type=F32)


def _dot_nt(a, b):
    return lax.dot_general(a, b, (((1,), (1,)), ((), ())), preferred_element_type=F32)


def _dot_tn(a, b):
    return lax.dot_general(a, b, (((0,), (0,)), ((), ())), preferred_element_type=F32)


def _split2(x):
    hi = x.astype(BF16)
    lo = (x - hi.astype(F32)).astype(BF16)
    return hi, lo


def _dot_const(m2, x):
    hi, lo = _split2(x)
    return _dot(m2, jnp.concatenate([hi, lo], axis=0))


def _mod_row(i, tm):
    start = i * tm
    return jnp.where(start < N_PROMPT, 0, 1 + (start - N_PROMPT) // DEC_SEQ)


def _mod_slice(mod_ref, row, k):
    return mod_ref[pl.ds(row, 1), k * D_MODEL:(k + 1) * D_MODEL]


def _norm_mod(x, g, sc, sh):
    ms = jnp.mean(x * x, axis=-1, keepdims=True)
    return (x * lax.rsqrt(ms + EPS) * g) * (1.0 + sc) + sh


def _mod_kernel(cond_ref, w_ref, b_ref, side_ref, o_ref, side_o_ref):
    s = _silu(cond_ref[...]).astype(BF16)
    o_ref[0] = _dot(s, w_ref[0].astype(BF16)) + b_ref[0]
    side_o_ref[...] = side_ref[...].astype(BF16)


MOD_TN = 768


def _modulation(cond8, ada_w, ada_b, side):
    depth = ada_w.shape[0]
    nj = ada_w.shape[2] // MOD_TN
    slab = pl.cdiv(side.shape[1], depth * nj * BF16_ROWS) * BF16_ROWS
    return pl.pallas_call(
        _mod_kernel,
        grid=(depth, nj),
        in_specs=[
            pl.BlockSpec((N_MOD_ROWS, D_MODEL), lambda l, j: (0, 0)),
            pl.BlockSpec((1, D_MODEL, MOD_TN), lambda l, j: (l, 0, j)),
            pl.BlockSpec((1, 1, MOD_TN), lambda l, j: (l, 0, j)),
            pl.BlockSpec((None, slab, side.shape[2]), lambda l, j: (0, l * nj + j, 0)),
        ],
        out_specs=[pl.BlockSpec((1, N_MOD_ROWS, MOD_TN), lambda l, j: (l, 0, j)),
                   pl.BlockSpec((slab, side.shape[2]), lambda l, j: (l * nj + j, 0))],
        out_shape=[jax.ShapeDtypeStruct((depth, N_MOD_ROWS, ada_w.shape[2]), F32),
                   jax.ShapeDtypeStruct(side.shape[1:], BF16)],
        compiler_params=_cparams("arbitrary", "arbitrary"),
        name="modulation",
    )(cond8, ada_w, ada_b.reshape(depth, 1, -1), side)


def _stream_specs(n_arrays, tm, width=D_MODEL):
    if n_arrays == 1:
        return [pl.BlockSpec((tm, width), lambda i: (i, 0))]
    npt = N_PROMPT // tm
    return [pl.BlockSpec((tm, width), lambda i: (jnp.minimum(i, npt - 1), 0)),
            pl.BlockSpec((tm, width), lambda i: (jnp.maximum(i - npt, 0), 0))]


def _stream_load(x_refs, tm):
    if len(x_refs) == 1:
        return x_refs[0][...]
    return jnp.where(pl.program_id(0) < N_PROMPT // tm, x_refs[0][...], x_refs[1][...])


def _side_specs(side, n_steps):
    in_specs = [pl.BlockSpec((None, w.shape[1] // n_steps, w.shape[2]), lambda i, l=l: (l, i, 0)) for w, l in side]
    out_specs = [pl.BlockSpec((w.shape[1] // n_steps, w.shape[2]), lambda i: (i, 0)) for w, _ in side]
    shapes = [jax.ShapeDtypeStruct(w.shape[1:], BF16) for w, _ in side]
    return in_specs, out_specs, shapes


def _side_cast(in_refs, out_refs):
    for i_ref, o_ref in zip(in_refs, out_refs):
        o_ref[...] = i_ref[...].astype(BF16)


def _norm_proj_kernel(*refs, tm, n_x, n_w, n_side, w_transposed):
    x_refs, (mod_ref, g_ref) = refs[:n_x], refs[n_x:n_x + 2]
    w_refs = refs[n_x + 2:n_x + 2 + n_w]
    side_in = refs[n_x + 2 + n_w:n_x + 2 + n_w + n_side]
    o_refs = refs[n_x + 2 + n_w + n_side:n_x + 2 + 2 * n_w + n_side]
    side_out = refs[n_x + 2 + 2 * n_w + n_side:]
    row = _mod_row(pl.program_id(0), tm)
    h = _norm_mod(_stream_load(x_refs, tm), g_ref[...], _mod_slice(mod_ref, row, 1), _mod_slice(mod_ref, row, 0)).astype(BF16)
    for w_ref, o_ref in zip(w_refs, o_refs):
        o_ref[...] = _dot_nt(h, w_ref[...]) if w_transposed else _dot(h, w_ref[...])
    _side_cast(side_in, side_out)


def _norm_proj(xs, mod, g, ws, widths=None, side=(), w_transposed=False, tm=512):
    n_w = len(ws)
    widths = widths or [w.shape[0 if w_transposed else 1] for w in ws]
    const = lambda i: (0, 0)
    w_block = (lambda n: (n, D_MODEL)) if w_transposed else (lambda n: (D_MODEL, n))
    side_in_specs, side_out_specs, side_shapes = _side_specs(side, N_TOK // tm)
    return pl.pallas_call(
        functools.partial(_norm_proj_kernel, tm=tm, n_x=len(xs), n_w=n_w, n_side=len(side), w_transposed=w_transposed),
        grid=(N_TOK // tm,),
        in_specs=_stream_specs(len(xs), tm) + [
            pl.BlockSpec(mod.shape, const),
            pl.BlockSpec((1, D_MODEL), const),
        ] + [pl.BlockSpec(w_block(n), const, pipeline_mode=pl.Buffered(1)) for n in widths] + side_in_specs,
        out_specs=[pl.BlockSpec((tm, n), lambda i: (i, 0)) for n in widths] + side_out_specs,
        out_shape=[jax.ShapeDtypeStruct((N_TOK, n), F32) for n in widths] + side_shapes,
        compiler_params=_cparams("arbitrary"),
        name="norm_proj",
    )(*xs, mod, g.reshape(1, D_MODEL), *ws, *[w for w, _ in side])


def _post_kernel(*refs, tm, ff_chunk, n_x, n_y, groups, n_side):
    x_refs = refs[:n_x]
    n_m = sum(n for n, _ in groups)
    m_refs = refs[n_x:n_x + n_m]
    mod_ref, g_ref, wo_ref, w1_ref, w2_ref = refs[n_x + n_m:n_x + n_m + 5]
    side_in = refs[n_x + n_m + 5:n_x + n_m + 5 + n_side]
    y_refs = refs[n_x + n_m + 5 + n_side:n_x + n_m + 5 + n_side + n_y]
    _side_cast(side_in, refs[n_x + n_m + 5 + n_side + n_y:])
    row = _mod_row(pl.program_id(0), tm)
    mix, first_ref, first_col = None, 0, 0
    for n, width in groups:
        part = _stream_load(m_refs[first_ref:first_ref + n], tm).astype(BF16)
        term = _dot(part, wo_ref[first_col:first_col + width, :])
        mix = term if mix is None else mix + term
        first_ref, first_col = first_ref + n, first_col + width
    x1 = _stream_load(x_refs, tm) + _mod_slice(mod_ref, row, 2) * mix
    h = _norm_mod(x1, g_ref[...], _mod_slice(mod_ref, row, 4), _mod_slice(mod_ref, row, 3)).astype(BF16)
    acc = jnp.zeros((tm, D_MODEL), F32)
    for k in range(0, D_FF, ff_chunk):
        a = jnp.maximum(_dot(h, w1_ref[:, k:k + ff_chunk]), 0.0)
        acc = acc + _dot((a * a).astype(BF16), w2_ref[k:k + ff_chunk, :])
    y = x1 + _mod_slice(mod_ref, row, 5) * acc
    if n_y == 1:
        y_refs[0][...] = y
    else:
        is_prompt = pl.program_id(0) < N_PROMPT // tm

        @pl.when(is_prompt)
        def _():
            y_refs[0][...] = y

        @pl.when(jnp.logical_not(is_prompt))
        def _():
            y_refs[1][...] = y


def _post_mixer(xs, mixed, mod, g, wo, w1, w2, split_out=False, side=(), tm=512, ff_chunk=1024):
    const = lambda i: (0, 0)
    resident = lambda w: pl.BlockSpec(w.shape, const, pipeline_mode=pl.Buffered(1))
    n_y = 2 if split_out else 1
    rows = (N_PROMPT, N_SAMPLE) if split_out else (N_TOK,)
    groups = tuple((len(grp), grp[0].shape[1]) for grp in mixed)
    mixed_specs = [spec for n, width in groups for spec in _stream_specs(n, tm, width)]
    side_in_specs, side_out_specs, side_shapes = _side_specs(side, N_TOK // tm)
    out = pl.pallas_call(
        functools.partial(_post_kernel, tm=tm, ff_chunk=ff_chunk, n_x=len(xs), n_y=n_y, groups=groups,
                          n_side=len(side)),
        grid=(N_TOK // tm,),
        in_specs=_stream_specs(len(xs), tm) + mixed_specs + [
            pl.BlockSpec(mod.shape, const),
            pl.BlockSpec((1, D_MODEL), const),
            resident(wo), resident(w1), resident(w2),
        ] + side_in_specs,
        out_specs=_stream_specs(n_y, tm) + side_out_specs,
        out_shape=[jax.ShapeDtypeStruct((r, D_MODEL), F32) for r in rows] + side_shapes,
        compiler_params=_cparams("arbitrary"),
        name="post_mixer",
    )(*xs, *[a for grp in mixed for a in grp], mod, g.reshape(1, D_MODEL), wo, w1, w2, *[w for w, _ in side])
    return tuple(out[:n_y]), tuple(out[n_y:])


PAIR = 2 * HD_C


def _pair_consts():
    lane = lax.broadcasted_iota(jnp.int32, (1, PAIR), 1)
    first = lane < HD_C
    ones_col = [jnp.where(lane == HD_C, 1.0, 0.0), jnp.where(lane == 0, 1.0, 0.0)]
    r = lax.broadcasted_iota(jnp.int32, (2 * PAIR, PAIR), 0) % PAIR
    cidx = lax.broadcasted_iota(jnp.int32, (2 * PAIR, PAIR), 1)
    mean2 = jnp.where(r // HD_C == cidx // HD_C, 1.0 / HD_C, 0.0).astype(BF16)
    return first, ones_col, mean2


def _pair_norm(x, w2, mean2):
    hi, lo = _split2(x * x)
    ms = _dot(jnp.concatenate([hi, lo], axis=1), mean2)
    return x * lax.rsqrt(ms + EPS) * w2


def _pair_queries(q, first):
    return [jnp.where(first, q, 0.0).astype(BF16), jnp.where(first, 0.0, q).astype(BF16)]


def _pair_values(v, first, ones_col):
    return [jnp.where(first, v, ones_col[0]).astype(BF16), jnp.where(first, ones_col[1], v).astype(BF16)]


def _pair_output(o_aug, first):
    den = [o_aug[0][:, HD_C:HD_C + 1], o_aug[1][:, 0:1]]
    return jnp.where(first, o_aug[0] / den[0], o_aug[1] / den[1])


def _row_max(*pieces):
    tiles = [p[:, i:i + LANES] for p in pieces for i in range(0, p.shape[1], LANES)]
    return jnp.max(functools.reduce(jnp.maximum, tiles), axis=-1, keepdims=True)


CTX_PAIRS = 4


def _ctx_attn_kernel(q_ref, k_ref, v_ref, qn_ref, kn_ref, o_ref, kc_ref, vc_ref):
    first, ones_col, mean2 = _pair_consts()
    lanes = [slice(p * PAIR, (p + 1) * PAIR) for p in range(CTX_PAIRS)]
    qn = [_pair_norm(q_ref[:, ln], qn_ref[...], mean2) * HD_C ** -0.5 for ln in lanes]
    kn = [_pair_norm(k_ref[:, ln], kn_ref[...], mean2) for ln in lanes]
    v = [v_ref[:, ln] for ln in lanes]
    kt = [x.T for x in kn]
    for p in range(CTX_PAIRS):
        kc_ref[0, 0, lanes[p], :] = kt[p]
        vc_ref[0, 0, lanes[p], :] = v[p].T
    q = [_pair_queries(x, first) for x in qn]
    va = [_pair_values(x, first, ones_col) for x in v]
    s = [[_dot(q[p][j], kt[p].astype(BF16)) for j in range(2)] for p in range(CTX_PAIRS)]
    pr = [[jnp.exp(x - _row_max(x)).astype(BF16) for x in sp] for sp in s]
    for p in range(CTX_PAIRS):
        o_ref[:, lanes[p]] = _pair_output([_dot(pr[p][j], va[p][j]) for j in range(2)], first)


def _ctx_attention(qkv, qn, kn):
    heads = 2 * CTX_PAIRS
    ng = H_C // heads
    wide = CTX_PAIRS * PAIR
    blk = lambda off: pl.BlockSpec((SEQ, wide), lambda b, p: (b, off + p))
    cache_spec = pl.BlockSpec((1, 1, wide, SEQ), lambda b, p: (b, 0, p, 0))
    cache_shape = jax.ShapeDtypeStruct((BATCH, 1, H_C * HD_C, SEQ), F32)
    return pl.pallas_call(
        _ctx_attn_kernel,
        grid=(BATCH, ng),
        in_specs=[blk(0), blk(ng), blk(2 * ng),
                  pl.BlockSpec((1, PAIR), lambda b, p: (0, 0)),
                  pl.BlockSpec((1, PAIR), lambda b, p: (0, 0))],
        out_specs=[pl.BlockSpec((SEQ, wide), lambda b, p: (b, p)), cache_spec, cache_spec],
        out_shape=[jax.ShapeDtypeStruct((N_PROMPT, D_MODEL), F32), cache_shape, cache_shape],
        compiler_params=_cparams("arbitrary", "arbitrary"),
        name="ctx_attention",
    )(qkv, qkv, qkv, jnp.tile(qn.reshape(1, HD_C), (1, 2)), jnp.tile(kn.reshape(1, HD_C), (1, 2)))


def _na_row_start(r):
    return min(max(r - KH // 2, 0), GRID_ROWS - KH)


NA_ROW_GROUP = 4


def _na_attn_kernel(q_ref, k_ref, v_ref, kc_ref, vc_ref, qn_ref, kn_ref, bias_ref, o_ref, qs, ks, vs, bias_s):
    first, ones_col, mean2 = _pair_consts()

    @pl.when(pl.program_id(1) == 0)
    def _():
        q_col = lax.broadcasted_iota(jnp.int32, (GRID_W, PAIR), 0)
        lane = lax.broadcasted_iota(jnp.int32, (GRID_W, PAIR), 1)
        k_col = lane % GRID_W
        w0 = jnp.clip(q_col - KW // 2, 0, GRID_W - KW)
        outside = jnp.where((k_col >= w0) & (k_col < w0 + KW), 0.0, NEG_INF)
        n_dr = 2 * KH - 1
        for j in range(2):
            band = []
            for dr in range(n_dr):
                row = jnp.broadcast_to(bias_ref[j, dr:dr + 1, :], (GRID_W, PAIR))
                band.append([pltpu.roll(row, (half * GRID_W - (KW - 1)) % PAIR, axis=1, stride=1, stride_axis=0)
                             for half in range(2)])
            zero = jnp.zeros((GRID_W, PAIR), F32)
            for cp in range(2):
                for t in range(KH):
                    lo, hi = 2 * t + cp, 2 * t + cp + 1
                    tile = jnp.where(lane < GRID_W, band[lo][0] if lo < n_dr else zero,
                                     band[hi][1] if hi < n_dr else zero)
                    bias_s[j, cp, :, t * PAIR:(t + 1) * PAIR] = tile + outside

    q2 = _pair_queries(_pair_norm(q_ref[...], qn_ref[...], mean2) * HD_C ** -0.5, first)
    v2 = _pair_values(v_ref[...], first, ones_col)
    ks[...] = _pair_norm(k_ref[...], kn_ref[...], mean2).astype(BF16)
    for j in range(2):
        qs[j] = q2[j]
        vs[j] = v2[j]
    kt_ctx = kc_ref[0, 0].astype(BF16)
    vt = vc_ref[0, 0]
    ch = lax.broadcasted_iota(jnp.int32, vt.shape, 0)
    vt_ctx = [jnp.where(ch < HD_C, vt, jnp.where(ch == HD_C, 1.0, 0.0)).astype(BF16),
              jnp.where(ch < HD_C, jnp.where(ch == 0, 1.0, 0.0), vt).astype(BF16)]
    for r0 in range(0, GRID_ROWS, NA_ROW_GROUP):
        units = [(r, j) for r in range(r0, r0 + NA_ROW_GROUP) for j in range(2)]
        rows = {r: slice(r * GRID_W, (r + 1) * GRID_W) for r, _ in units}
        wins = {r: slice(_na_row_start(r) * GRID_W, (_na_row_start(r) + KH) * GRID_W) for r, _ in units}
        s_ctx_all = [_dot(qs[j, r0 * GRID_W:(r0 + NA_ROW_GROUP) * GRID_W, :], kt_ctx) for j in range(2)]
        s_ctx = [s_ctx_all[j][(r - r0) * GRID_W:(r - r0 + 1) * GRID_W] for r, j in units]
        s_win = []
        for r, j in units:
            dr0 = KH - 1 - (r - _na_row_start(r))
            lane0 = (dr0 - dr0 % 2) * GRID_W
            s_win.append(_dot_nt(qs[j, rows[r], :], ks[wins[r], :])
                         + bias_s[j, dr0 % 2, :, lane0:lane0 + KH * GRID_W])
        m = [_row_max(a, b) for a, b in zip(s_win, s_ctx)]
        p_win = [jnp.exp(a - mm).astype(BF16) for a, mm in zip(s_win, m)]
        p_ctx = [jnp.exp(b - mm).astype(BF16) for b, mm in zip(s_ctx, m)]
        o_aug = [_dot(p_win[i], vs[j, wins[r], :]) + _dot_nt(p_ctx[i], vt_ctx[j]) for i, (r, j) in enumerate(units)]
        for i in range(0, len(units), 2):
            o_ref[rows[units[i][0]], :] = _pair_output(o_aug[i:i + 2], first)


NA_BIAS_LANES = 2 * KH * GRID_W


def _na_attention(qkv, cache_kt, cache_vt, qn, kn, rpb):
    nhp = H_C // 2
    row0 = N_PROMPT // DEC_SEQ
    blk = lambda off: pl.BlockSpec((DEC_SEQ, 2 * HD_C), lambda p, b: (row0 + b, off + p))
    cache_spec = pl.BlockSpec((1, 1, PAIR, PAST_LEN), lambda p, b: (b, 0, p, 0))
    rpb_rows = 2 * KH
    bias = jnp.pad(rpb.astype(F32), ((0, 0), (0, rpb_rows - rpb.shape[1]), (0, PAIR - rpb.shape[2])))
    return pl.pallas_call(
        _na_attn_kernel,
        grid=(nhp, DEC_BATCH),
        in_specs=[blk(0), blk(nhp), blk(2 * nhp), cache_spec, cache_spec,
                  pl.BlockSpec((1, PAIR), lambda p, b: (0, 0)),
                  pl.BlockSpec((1, PAIR), lambda p, b: (0, 0)),
                  pl.BlockSpec((2, rpb_rows, PAIR), lambda p, b: (p, 0, 0))],
        out_specs=pl.BlockSpec((DEC_SEQ, 2 * HD_C), lambda p, b: (b, p)),
        out_shape=jax.ShapeDtypeStruct((N_SAMPLE, D_MODEL), F32),
        scratch_shapes=[pltpu.VMEM((2, DEC_SEQ, PAIR), BF16), pltpu.VMEM((DEC_SEQ, PAIR), BF16),
                        pltpu.VMEM((2, DEC_SEQ, PAIR), BF16), pltpu.VMEM((2, 2, GRID_W, NA_BIAS_LANES), F32)],
        compiler_params=_cparams("arbitrary", "arbitrary"),
        name="na_attention",
    )(qkv, qkv, qkv, cache_kt, cache_vt, jnp.tile(qn.reshape(1, HD_C), (1, 2)), jnp.tile(kn.reshape(1, HD_C), (1, 2)),
      bias)


def _seq_layout(prompt):
    return (SEQ, BATCH, 0) if prompt else (DEC_SEQ, DEC_BATCH, N_PROMPT // DEC_SEQ)


def _flip_blocks(m, c):
    r, s = m.shape
    return m.reshape(r // c, c, s // c, c)[:, ::-1, :, ::-1].reshape(r, s)


def _rms_gate(x, gn, gate):
    ms = jnp.mean(x * x, axis=-1, keepdims=True)
    return x * lax.rsqrt(ms + EPS) * gn * _silu(gate)


HG_LEVELS = tuple(CHUNK >> (i + 1) for i in range(CHUNK.bit_length() - 1))
HG_NL = len(HG_LEVELS)
HG_STACK = (HG_NL + 1) * CHUNK
TOT_ROWS = BF16_ROWS
HG_ROWS = (HG_NL + 2) * CHUNK + TOT_ROWS


def _hgrn_consts():
    c = CHUNK
    level_rows = []
    mask = np.zeros((HG_STACK, HG_STACK), np.float32)
    mask[:c, :c] = np.eye(c)
    for li, b in enumerate(HG_LEVELS):
        m = np.zeros((c, c), np.float32)
        blk = np.zeros((c, c), np.float32)
        for t in range(c):
            mid = (t // (2 * b)) * 2 * b + b
            if t >= mid:
                m[t, mid:t + 1] = 1.0
                blk[t, mid - b:mid] = 1.0
            else:
                m[t, t + 1:mid] = 1.0
        level_rows.append(m)
        mask[(li + 1) * c:(li + 2) * c, (li + 1) * c:(li + 2) * c] = blk
    dq = np.tril(np.ones((c, c), np.float32))
    dk = np.triu(np.ones((c, c), np.float32), 1)
    body = np.concatenate(level_rows + [dq, dk], axis=0)
    tot = np.ones((TOT_ROWS, c), np.float32)
    mcs, masks = [], []
    for reverse in (False, True):
        bm = _flip_blocks(body, c) if reverse else body
        mk = _flip_blocks(mask, c) if reverse else mask
        mc = np.concatenate([bm, tot], axis=0)
        mcs.append(np.concatenate([mc, mc], axis=1))
        masks.append(mk)
    return jnp.asarray(np.stack(mcs), BF16), jnp.asarray(np.stack(masks), F32)


HG_FAST = 64
HG_HALF = HG_FAST // 2
HG_FAST_ROWS = 4 * HG_FAST + TOT_ROWS
HG_SAFE_EXP = 40.0
HG_FAST_STEPS = 4


def _hgrn_fast_consts(seq):
    c, m = HG_FAST, HG_HALF
    aq = np.zeros((c, c), np.float32)
    for t in range(c):
        if t >= m:
            aq[t, m:t + 1] = 1.0
        else:
            aq[t, t + 1:m] = -1.0
    dq = np.tril(np.ones((c, c), np.float32))
    dk = np.triu(np.ones((c, c), np.float32), 1)
    body = np.concatenate([aq, -aq, dq, dk], axis=0)
    tot = np.ones((TOT_ROWS, c), np.float32)
    causal = np.tril(np.ones((c, c), np.float32))
    mfs, masks = [], []
    for reverse in (False, True):
        bm = _flip_blocks(body, c) if reverse else body
        mf = np.concatenate([bm, tot], axis=0)
        mfs.append(np.concatenate([mf, mf], axis=1))
        masks.append(causal.T if reverse else causal)
    n_half = seq // m
    half = np.zeros((max(n_half, BF16_ROWS), seq), np.float32)
    for i in range(n_half):
        half[i, i * m:(i + 1) * m] = 1.0
    return jnp.asarray(np.stack(mfs), BF16), jnp.asarray(np.stack(masks), F32), jnp.asarray(half, BF16)


def _hgrn_kernel(*refs, seq, has_s0, emit_state):
    it = iter(refs)
    qa_ref, ff_ref, fb_ref, ia_ref, ga_ref, lb_ref, gn_ref, mc_ref, mask_ref = [next(it) for _ in range(9)]
    mf_ref, causal_ref, half_ref = [next(it) for _ in range(3)]
    s0_ref = next(it) if has_s0 else None
    o_ref = next(it)
    st_ref = next(it) if emit_state else None
    s_scr, acc, f_s, lf_s = [next(it) for _ in range(4)]
    c = CHUNK
    n_chunks = seq // c
    combos = [(d, h) for d in range(2) for h in range(H_A)]
    lanes = [slice(h * DK_A, (h + 1) * DK_A) for h in range(H_A)]
    add = lambda a, b: a + b

    lb_raw = lb_ref[...]
    lb_e = jnp.exp(lb_raw - jnp.max(lb_raw, axis=0, keepdims=True))
    lb_all = lb_e[0:1] / jnp.sum(lb_e, axis=0, keepdims=True)

    for d in range(2):
        for h in range(H_A):
            s_scr[d, h] = s0_ref[0, 0, d, h].T if has_s0 else jnp.zeros((DV_A, DK_A), F32)
    acc[...] = jnp.zeros(acc.shape, F32)

    worst = []
    for d, fr_ref in enumerate((ff_ref, fb_ref)):
        f = lb_all + (1.0 - lb_all) * _sigmoid(fr_ref[...])
        lf = jnp.log(f)
        f_s[d] = f
        lf_s[d] = lf
        worst.append(jnp.max(_dot(half_ref[...], (-lf).astype(BF16))))
    safe = jnp.maximum(worst[0], worst[1]) <= HG_SAFE_EXP

    def fast_body(n, carry):
        cf = HG_FAST
        n_fast = seq // cf
        steps = range(HG_FAST_STEPS)
        chunk = lambda d, t: (n * HG_FAST_STEPS + t) if d == 0 else (n_fast - 1 - n * HG_FAST_STEPS - t)
        rows = [[pl.ds(pl.multiple_of(chunk(d, t) * cf, cf), cf) for t in steps] for d in range(2)]
        units = [(t, d, h) for t in steps for d, h in combos]
        e_all = [[jnp.exp(_dot_const(mf_ref[d], lf_s[d, rows[d][t], :])) for t in steps] for d in range(2)]
        q_all = [[_silu(qa_ref[rows[d][t], :]) * DK_A ** -0.5 for t in steps] for d in range(2)]
        k_all = [[1.0 - f_s[d, rows[d][t], :] for t in steps] for d in range(2)]
        v_all = [[ia_ref[rows[d][t], :].astype(BF16) for t in steps] for d in range(2)]
        qs = {u: q_all[u[1]][u[0]][:, lanes[u[2]]] for u in units}
        ks = {u: k_all[u[1]][u[0]][:, lanes[u[2]]] for u in units}
        vs = {u: v_all[u[1]][u[0]][:, lanes[u[2]]] for u in units}
        es = {u: [e_all[u[1]][u[0]][i * cf:(i + 1) * cf, lanes[u[2]]] for i in range(4)] for u in units}
        p = {u: jnp.where(causal_ref[u[1]] > 0.0,
                          _dot_nt((qs[u] * es[u][0]).astype(BF16), (ks[u] * es[u][1]).astype(BF16)), 0.0).astype(BF16)
             for u in units}
        intra = {u: _dot(p[u], vs[u]) for u in units}
        upd = {u: _dot_tn(vs[u], (ks[u] * es[u][3]).astype(BF16)) for u in units}
        qdec = {u: (qs[u] * es[u][2]).astype(BF16) for u in units}
        st = {(d, h): s_scr[d, h] for d, h in combos}
        o = {}
        for t in steps:
            for d, h in combos:
                u = (t, d, h)
                o[u] = intra[u] + _dot_nt(qdec[u], st[d, h].astype(BF16))
                st[d, h] = st[d, h] * e_all[d][t][4 * cf:4 * cf + 1, lanes[h]] + upd[u]
        for t in steps:
            for d in range(2):
                acc[rows[d][t], :] += jnp.concatenate([o[t, d, h] for h in range(H_A)], axis=1)
        for d, h in combos:
            s_scr[d, h] = st[d, h]
        return carry

    def body(n, carry):
        rows = [pl.ds(pl.multiple_of((n if d == 0 else n_chunks - 1 - n) * c, c), c) for d in range(2)]
        f_all = [f_s[d, rows[d], :] for d in range(2)]
        e_all = [jnp.exp(_dot_const(mc_ref[d], lf_s[d, rows[d], :])) for d in range(2)]
        q_all = [_silu(qa_ref[rows[d], :]) * DK_A ** -0.5 for d in range(2)]
        v_all = [ia_ref[rows[d], :].astype(BF16) for d in range(2)]
        st = [s_scr[d, h] for d, h in combos]
        qs, ks, vs, es = [], [], [], []
        for d, h in combos:
            qs.append(q_all[d][:, lanes[h]])
            ks.append(1.0 - f_all[d][:, lanes[h]])
            vs.append(v_all[d][:, lanes[h]])
            es.append(e_all[d][:, lanes[h]])
        lvl = [[e[i * c:(i + 1) * c] for i in range(HG_NL + 2)] for e in es]
        qst = [jnp.concatenate([q] + [q * l[i] for i in range(HG_NL)], axis=0).astype(BF16) for q, l in zip(qs, lvl)]
        kst = [jnp.concatenate([k] + [k * l[i] for i in range(HG_NL)], axis=0).astype(BF16) for k, l in zip(ks, lvl)]
        r = [(_dot_nt(qst[i], kst[i]) * mask_ref[d]).astype(BF16) for i, (d, h) in enumerate(combos)]
        ost = [_dot(r[i], jnp.concatenate([vs[i]] * (HG_NL + 1), axis=0)) for i in range(len(combos))]
        inter = [_dot_nt((qs[i] * lvl[i][HG_NL]).astype(BF16), st[i].astype(BF16)) for i in range(len(combos))]
        upd = [_dot_tn(vs[i], (ks[i] * lvl[i][HG_NL + 1]).astype(BF16)) for i in range(len(combos))]
        o = [functools.reduce(lambda a, b: a + b, [ost[i][j * c:(j + 1) * c] for j in range(HG_NL + 1)]) + inter[i]
             for i in range(len(combos))]
        for d in range(2):
            acc[rows[d], :] += jnp.concatenate(o[d * H_A:(d + 1) * H_A], axis=1)
        for i, (d, h) in enumerate(combos):
            e_tot = es[i][(HG_NL + 2) * c:(HG_NL + 2) * c + 1]
            s_scr[d, h] = st[i] * e_tot + upd[i]
        return carry

    @pl.when(safe)
    def _():
        lax.fori_loop(0, seq // (HG_FAST * HG_FAST_STEPS), fast_body, 0)

    @pl.when(jnp.logical_not(safe))
    def _():
        lax.fori_loop(0, n_chunks, body, 0)

    for h in range(H_A):
        ln = slice(h * DV_A, (h + 1) * DV_A)
        o_ref[:, ln] = _rms_gate(acc[:, ln], gn_ref[...], ga_ref[:, ln])
    if emit_state:
        for d in range(2):
            for h in range(H_A):
                st_ref[0, 0, d, h] = s_scr[d, h].T


def _hgrn(proj, hgrn_lb, gn, consts, prompt, s0=None, layer=0):
    seq, nb, rb0 = _seq_layout(prompt)
    consts = list(consts) + list(_hgrn_fast_consts(seq))
    wa = H_A * DK_A
    blk = lambda j: pl.BlockSpec((seq, wa), lambda b: (rb0 + b, j))
    const2 = lambda b: (0, 0)
    st_block = (1, 1, 2, H_A, DK_A, DV_A)
    in_specs = [blk(0), blk(1), blk(2), blk(3), blk(4),
                pl.BlockSpec(hgrn_lb.shape, const2), pl.BlockSpec((1, DV_A), const2)]
    in_specs += [pl.BlockSpec(m.shape, lambda b, nd=m.ndim: (0,) * nd) for m in consts]
    args = [proj] * 5 + [hgrn_lb, gn.reshape(1, DV_A)] + consts
    if s0 is not None:
        in_specs.append(pl.BlockSpec(st_block, lambda b: (b, layer, 0, 0, 0, 0)))
        args.append(s0)
    out_specs = [pl.BlockSpec((seq, wa), lambda b: (b, 0))]
    out_shape = [jax.ShapeDtypeStruct((nb * seq, wa), F32)]
    if prompt:
        out_specs.append(pl.BlockSpec(st_block, lambda b: (b, 0, 0, 0, 0, 0)))
        out_shape.append(jax.ShapeDtypeStruct((nb, 1, 2, H_A, DK_A, DV_A), F32))
    return pl.pallas_call(
        functools.partial(_hgrn_kernel, seq=seq, has_s0=s0 is not None, emit_state=prompt),
        grid=(nb,),
        in_specs=in_specs,
        out_specs=out_specs,
        out_shape=out_shape,
        scratch_shapes=[pltpu.VMEM((2, H_A, DV_A, DK_A), F32), pltpu.VMEM((seq, wa), F32),
                        pltpu.VMEM((2, seq, wa), F32), pltpu.VMEM((2, seq, wa), F32)],
        compiler_params=_cparams("arbitrary"),
        name="hgrn_prompt" if prompt else "hgrn_sample",
    )(*args)


GD_SUB = GBLK // CHUNK
GD_BLOCKS_PER_ITER = 2
GD_PROMPT_SEQS = 2
GD_ROWS = 2 * GBLK + TOT_ROWS


def _gdn_consts():
    n, c = GBLK, CHUNK
    same = (np.arange(n)[:, None] // c) == (np.arange(n)[None, :] // c)
    tri = (same & (np.arange(n)[None, :] <= np.arange(n)[:, None])).astype(np.float32)
    sup = (same & (np.arange(n)[None, :] > np.arange(n)[:, None])).astype(np.float32)
    tot = np.zeros((TOT_ROWS, n), np.float32)
    for s in range(GD_SUB):
        tot[s, s * c:(s + 1) * c] = 1.0
    mgs, tts, tris = [], [], []
    for reverse in (False, True):
        t = _flip_blocks(tri, c) if reverse else tri
        s = _flip_blocks(sup, c) if reverse else sup
        mg = np.concatenate([t, s, tot], axis=0)
        mgs.append(np.concatenate([mg, mg], axis=1))
        tts.append(np.concatenate([t.T, t.T], axis=0))
        tris.append(t)
    tris.append(same.astype(np.float32))
    return jnp.asarray(np.stack(mgs), BF16), jnp.asarray(np.stack(tts), BF16), jnp.asarray(np.stack(tris), F32)


def _softplus(x):
    return jnp.maximum(x, 0.0) + jnp.log(1.0 + jnp.exp(-jnp.abs(x)))


CONV_PAD = SUBLANES


def _conv_silu(x, w, seq):
    half = SHORT_CONV // 2
    pad = jnp.zeros((CONV_PAD, x.shape[1]), x.dtype)
    xe = jnp.concatenate([pad, x, pad], axis=0)
    acc = xe * w[half:half + 1]
    for j in range(SHORT_CONV):
        shift = half - j
        if shift != 0:
            acc = acc + pltpu.roll(xe, shift % (seq + 2 * CONV_PAD), axis=0) * w[j:j + 1]
    return _silu(acc[CONV_PAD:seq + CONV_PAD])


def _l2norm_heads(x, n_heads, width, scale):
    outs = []
    for h in range(n_heads):
        xh = x[:, h * width:(h + 1) * width]
        outs.append(xh * (lax.rsqrt(jnp.sum(xh * xh, axis=-1, keepdims=True) + EPS) * scale))
    return jnp.concatenate(outs, axis=-1)


def _gdn_kernel(*refs, seq, n_seq, has_s0, emit_state):
    it = iter(refs)
    (q_ref, k_ref, v_ref, gb_ref, gate_ref, cw_ref, alog_ref, dt_ref, gn_ref,
     mg_ref, tt_ref, tri_ref) = [next(it) for _ in range(12)]
    s0_ref = next(it) if has_s0 else None
    o_ref = next(it)
    st_ref = next(it) if emit_state else None
    qn, kn, vn, u_s, w_s, qg_s, kdt_s, at_s, et_s, s_scr, acc = [next(it) for _ in range(11)]
    c = CHUNK
    n_chunks = seq // c
    n_blocks = n_seq * seq // GBLK
    wq = H_B * DK_B
    n_dh = 2 * H_B
    combos = [(d, h) for d in range(2) for h in range(H_B)]
    lanes = [slice(h * DK_B, (h + 1) * DK_B) for h in range(H_B)]

    for s in range(n_seq):
        sr = slice(s * seq, (s + 1) * seq)
        qn[sr, :] = _l2norm_heads(_conv_silu(q_ref[sr, :], cw_ref[:, 0:wq], seq), H_B, DK_B, DK_B ** -0.5)
        kn[sr, :] = _l2norm_heads(_conv_silu(k_ref[sr, :], cw_ref[:, wq:2 * wq], seq), H_B, DK_B, 1.0)
        vn[sr, :] = _conv_silu(v_ref[sr, :], cw_ref[:, 2 * wq:3 * wq], seq)
        for i in range(n_dh):
            s_scr[s * n_dh + i] = (s0_ref[s, 0, i // H_B, i % H_B] if has_s0 else jnp.zeros((DK_B, DV_B), F32))
    acc[...] = jnp.zeros(acc.shape, F32)

    eye = (lax.broadcasted_iota(jnp.int32, (GBLK, GBLK), 0)
           == lax.broadcasted_iota(jnp.int32, (GBLK, GBLK), 1)).astype(F32)
    eye_pk = (lax.broadcasted_iota(jnp.int32, (c, GBLK), 0)
              == lax.broadcasted_iota(jnp.int32, (c, GBLK), 1) % c).astype(F32)
    bwd_lane = lax.broadcasted_iota(jnp.int32, (1, LANES), 1) % n_dh >= H_B
    add = lambda a, b: a + b

    same_chunk = tri_ref[2].astype(BF16)

    def expand(pk):
        return jnp.concatenate([pk] * GD_SUB, axis=0) * same_chunk

    def pack(bd):
        return functools.reduce(add, [bd[s * c:(s + 1) * c] for s in range(GD_SUB)])

    def weights(hi, lo):
        return jnp.concatenate([expand(hi), expand(lo)], axis=1)

    def dot3_split(a_hi, a_lo, w2):
        m, n = a_hi.shape[0], w2.shape[1] // 2
        t = _dot(jnp.concatenate([a_hi, a_lo], axis=0), w2)
        return t[:m, :n] + t[m:, :n] + t[:m, n:]

    def block_body(it, carry):
        blks = [it * GD_BLOCKS_PER_ITER + o for o in range(GD_BLOCKS_PER_ITER)]
        units = [(o, d, h) for o in range(GD_BLOCKS_PER_ITER) for d, h in combos]
        idx = lambda d, h: d * H_B + h
        col = lambda x, j: jnp.broadcast_to(x[:, j:j + 1], (GBLK, DK_B))
        rows = [pl.ds(pl.multiple_of(b * GBLK, GBLK), GBLK) for b in blks]
        gates = [gate_ref[r, :] for r in rows]
        glog_all = [-jnp.exp(alog_ref[...]) * _softplus(x + dt_ref[...]) for x in gates]
        beta_all = [_sigmoid(x) for x in gates]
        g2 = [jnp.concatenate(_split2(x), axis=0) for x in glog_all]
        dg = [[_dot(mg_ref[d], x) for d in range(2)] for x in g2]
        dsel = [jnp.where(bwd_lane, x[1], x[0]) for x in dg]
        eg_all = [jnp.exp(x) for x in dsel]
        gt = [[_dot_tn(x, tt_ref[d]) for d in range(2)] for x in g2]
        qs = [[qn[r, ln] for ln in lanes] for r in rows]
        ks = [[kn[r, ln] for ln in lanes] for r in rows]
        vs = [[vn[r, ln] for ln in lanes] for r in rows]
        betas = [col(beta_all[o], n_dh + idx(d, h)) for o, d, h in units]
        kbs = [ks[o][h] * betas[u] for u, (o, d, h) in enumerate(units)]
        kb_of = {unit: kbs[u] for u, unit in enumerate(units)}
        kk = {(o, h): _dot_nt(jnp.concatenate([qs[o][h], kb_of[o, 0, h], kb_of[o, 1, h]], axis=0).astype(BF16),
                              ks[o][h].astype(BF16))
              for o in range(GD_BLOCKS_PER_ITER) for h in range(H_B)}
        decay = []
        for o, d, h in units:
            inside = tri_ref[d] > 0.0
            gd = col(dsel[o][:GBLK], idx(d, h)) - gt[o][d][idx(d, h):idx(d, h) + 1, :]
            decay.append(jnp.where(inside, jnp.exp(jnp.where(inside, gd, 0.0)), 0.0))
        attn = [kk[o, h][:GBLK] * decay[u] for u, (o, d, h) in enumerate(units)]
        p_pk = [pack(kk[o, h][(1 + d) * GBLK:(2 + d) * GBLK] * decay[u] * (1.0 - eye))
                for u, (o, d, h) in enumerate(units)]
        x_pk = [eye_pk - p for p in p_pk]
        p_sp = [_split2(p) for p in p_pk]
        p_w = [weights(*s) for s in p_sp]
        for _ in range(CHUNK.bit_length() - 2):
            p_pk = [dot3_split(*s, w) for s, w in zip(p_sp, p_w)]
            p_sp = [_split2(p) for p in p_pk]
            p_w = [weights(*s) for s in p_sp]
            x_pk = [x + dot3_split(*_split2(x), w) for x, w in zip(x_pk, p_w)]
        eg_col = [col(eg_all[o][:GBLK], idx(d, h)) for o, d, h in units]
        ekd_col = [col(eg_all[o][GBLK:2 * GBLK], idx(d, h)) for o, d, h in units]
        rhs = [_split2(jnp.concatenate([vs[o][h] * betas[u], kbs[u] * eg_col[u]], axis=1))
               for u, (o, d, h) in enumerate(units)]
        t_sp = [[expand(part) for part in _split2(x)] for x in x_pk]
        uw = [_dot(t[0], r[0]) + (_dot(t[0], r[1]) + _dot(t[1], r[0])) for t, r in zip(t_sp, rhs)]
        for u, (o, d, h) in enumerate(units):
            i = idx(d, h)
            qg = (qs[o][h] * eg_col[u]).astype(BF16)
            kdt_s[i, blks[o]] = (ks[o][h] * ekd_col[u]).T.astype(BF16)
            for s in range(GD_SUB):
                cn = blks[o] * GD_SUB + s
                r = slice(s * c, (s + 1) * c)
                u_s[i, cn] = uw[u][r, :DV_B]
                w_s[i, cn] = uw[u][r, DV_B:].astype(BF16)
                qg_s[i, cn] = qg[r]
                at_s[i, cn] = attn[u][r].astype(BF16)
                et_s[i, cn] = jnp.broadcast_to(eg_all[o][2 * GBLK + s:2 * GBLK + s + 1, i:i + 1], (SUBLANES, DV_B))
        return carry

    n_iter = n_blocks // GD_BLOCKS_PER_ITER
    lax.fori_loop(0, n_iter, block_body, 0, unroll=True)

    def chunk_body(n, carry):
        chains = [(s, d, h) for s in range(n_seq) for d, h in combos]
        idx = lambda d, h: d * H_B + h
        cn = {(s, d): s * n_chunks + (n if d == 0 else n_chunks - 1 - n) for s in range(n_seq) for d in range(2)}
        rows = {key: pl.ds(pl.multiple_of(v * c, c), c) for key, v in cn.items()}
        sub_of_row = lax.broadcasted_iota(jnp.int32, (GBLK, 1), 0) // c
        in_chunk = {key: sub_of_row == v % GD_SUB for key, v in cn.items()}
        st = {ch: s_scr[ch[0] * n_dh + idx(ch[1], ch[2])] for ch in chains}
        ws = {(s, d, h): _dot(jnp.concatenate([w_s[idx(d, h), cn[s, d]], qg_s[idx(d, h), cn[s, d]]], axis=0),
                              st[s, d, h].astype(BF16)) for s, d, h in chains}
        vblk = {(s, d, h): jnp.where(in_chunk[s, d],
                                     jnp.concatenate([u_s[idx(d, h), cn[s, d]] - ws[s, d, h][:c]] * GD_SUB, axis=0),
                                     0.0).astype(BF16) for s, d, h in chains}
        r = {(s, d, h): _dot(jnp.concatenate([at_s[idx(d, h), cn[s, d]], kdt_s[idx(d, h), cn[s, d] // GD_SUB]], axis=0),
                             vblk[s, d, h]) for s, d, h in chains}
        for s in range(n_seq):
            for d in range(2):
                acc[rows[s, d], :] += jnp.concatenate([ws[s, d, h][c:] + r[s, d, h][:c] for h in range(H_B)], axis=1)
        for s, d, h in chains:
            s_scr[s * n_dh + idx(d, h)] = st[s, d, h] * et_s[idx(d, h), cn[s, d]][0:1] + r[s, d, h][c:]
        return carry

    lax.fori_loop(0, n_chunks, chunk_body, 0)

    for h in range(H_B):
        ln = slice(h * DV_B, (h + 1) * DV_B)
        o_ref[:, ln] = _rms_gate(acc[:, ln], gn_ref[...], gb_ref[:, ln])
    if emit_state:
        for s in range(n_seq):
            for i in range(n_dh):
                st_ref[s, 0, i // H_B, i % H_B] = s_scr[s * n_dh + i]


def _gdn(proj, gates, conv_w, a_log, dt_bias, gn, consts, prompt, s0=None, layer=0):
    seq, nb, rb0 = _seq_layout(prompt)
    n_seq = GD_PROMPT_SEQS if prompt else 1
    rows = n_seq * seq
    n_chunks = rows // CHUNK
    wq = H_B * DK_B
    blk = lambda j: pl.BlockSpec((rows, wq), lambda b: (rb0 + b, j))
    const2 = lambda b: (0, 0)
    const3 = lambda b: (0, 0, 0)
    st_block = (n_seq, 1, 2, H_B, DK_B, DV_B)
    pad_row = lambda p: jnp.pad(p.reshape(1, -1).astype(F32), ((0, 0), (0, LANES - p.size)))
    in_specs = [blk(5), blk(6), blk(7), blk(8),
                pl.BlockSpec((rows, LANES), lambda b: (rb0 + b, 0)),
                pl.BlockSpec((SHORT_CONV, 3 * wq), const2),
                pl.BlockSpec((1, LANES), const2), pl.BlockSpec((1, LANES), const2), pl.BlockSpec((1, DV_B), const2)]
    in_specs += [pl.BlockSpec(m.shape, const3) for m in consts]
    args = [proj] * 4 + [gates, conv_w.reshape(SHORT_CONV, 3 * wq), pad_row(a_log), pad_row(dt_bias),
                         gn.reshape(1, DV_B)] + list(consts)
    if s0 is not None:
        in_specs.append(pl.BlockSpec(st_block, lambda b: (b, layer, 0, 0, 0, 0)))
        args.append(s0)
    out_specs = [pl.BlockSpec((rows, wq), lambda b: (b, 0))]
    out_shape = [jax.ShapeDtypeStruct((nb * seq, wq), F32)]
    if prompt:
        out_specs.append(pl.BlockSpec(st_block, lambda b: (b, 0, 0, 0, 0, 0)))
        out_shape.append(jax.ShapeDtypeStruct((nb, 1, 2, H_B, DK_B, DV_B), F32))
    n_dh = 2 * H_B
    scratch = ([pltpu.VMEM((rows, wq), F32)] * 3
               + [pltpu.VMEM((n_dh, n_chunks, CHUNK, DV_B), F32)]
               + [pltpu.VMEM((n_dh, n_chunks, CHUNK, DK_B), BF16)] * 2
               + [pltpu.VMEM((n_dh, rows // GBLK, DK_B, GBLK), BF16),
                  pltpu.VMEM((n_dh, n_chunks, CHUNK, GBLK), BF16),
                  pltpu.VMEM((n_dh, n_chunks, SUBLANES, DV_B), F32),
                  pltpu.VMEM((n_seq * n_dh, DK_B, DV_B), F32),
                  pltpu.VMEM((rows, wq), F32)])
    return pl.pallas_call(
        functools.partial(_gdn_kernel, seq=seq, n_seq=n_seq, has_s0=s0 is not None, emit_state=prompt),
        grid=(nb // n_seq,),
        in_specs=in_specs,
        out_specs=out_specs,
        out_shape=out_shape,
        scratch_shapes=scratch,
        compiler_params=_cparams("arbitrary"),
        name="gdn_prompt" if prompt else "gdn_sample",
    )(*args)


def kernel(x_prompt, x_sample, state_hgrn, state_gdn, cache_na_k, cache_na_v, c, c_ctx, ada_w, ada_b, norm_g, w_in_ab, w_out_ab, hgrn_lb, gdn_conv, gdn_a_log, gdn_dt_bias, gn_hgrn, gn_gdn, w_qkv_na, qn_na, kn_na, rpb_na, w_out_na, w_mlp1, w_mlp2):
    cond = jnp.concatenate([c_ctx[None, :], c, jnp.zeros((N_MOD_ROWS - 1 - DEC_BATCH, D_MODEL), F32)], axis=0)
    mods, w_in_t = _modulation(cond, ada_w, ada_b, jnp.swapaxes(w_in_ab, 1, 2))
    xs = (x_prompt.reshape(N_PROMPT, D_MODEL), x_sample.reshape(N_SAMPLE, D_MODEL))

    w_gate_t = jnp.pad(w_in_t[D_MAIN_AB:], ((0, LANES - N_GATE_AB), (0, 0)))
    proj, gates, wo0, w1_0, w2_0 = _norm_proj(xs, mods[0], norm_g[0, 0], [w_in_t, w_gate_t], widths=[D_MAIN_AB, LANES],
                                              side=((w_out_ab, 0), (w_mlp1, 0), (w_mlp2, 0)), w_transposed=True)
    hg_consts = _hgrn_consts()
    gd_consts = _gdn_consts()
    hg_prompt, new_hgrn = _hgrn(proj, hgrn_lb, gn_hgrn[0], hg_consts, True)
    hg_sample, = _hgrn(proj, hgrn_lb, gn_hgrn[0], hg_consts, False, s0=state_hgrn)
    gd_args = (gdn_conv[0], gdn_a_log[0], gdn_dt_bias[0], gn_gdn[0], gd_consts)
    gd_prompt, new_gdn = _gdn(proj, gates, *gd_args, True)
    gd_sample, = _gdn(proj, gates, *gd_args, False, s0=state_gdn)
    xs, (w_qkv, wo1, w1_1, w2_1) = _post_mixer(xs, [(hg_prompt, hg_sample), (gd_prompt, gd_sample)], mods[0],
                                               norm_g[0, 1], wo0, w1_0, w2_0,
                                               side=((w_qkv_na, 0), (w_out_na, 0), (w_mlp1, 1), (w_mlp2, 1)))

    qkv, = _norm_proj(xs, mods[1], norm_g[1, 0], [w_qkv])
    at_prompt, new_kt, new_vt = _ctx_attention(qkv, qn_na[0], kn_na[0])
    time_minor = lambda a: jnp.swapaxes(a, -1, -2).reshape(a.shape[0], 1, H_C * HD_C, a.shape[3])
    at_sample = _na_attention(qkv, time_minor(cache_na_k), time_minor(cache_na_v), qn_na[0], kn_na[0], rpb_na[0])
    time_major = lambda a: jnp.swapaxes(a.reshape(BATCH, 1, H_C, HD_C, SEQ), -1, -2)
    new_k, new_v = time_major(new_kt), time_major(new_vt)
    (y_prompt, y_sample), _ = _post_mixer(xs, [(at_prompt, at_sample)], mods[1], norm_g[1, 1], wo1, w1_1, w2_1,
                                          split_out=True)

    return (y_prompt.reshape(BATCH, SEQ, D_MODEL), y_sample.reshape(DEC_BATCH, DEC_SEQ, D_MODEL),
            new_hgrn, new_gdn, new_k, new_v)
```

```python
import functools

import numpy as np
import jax
import jax.numpy as jnp
from jax import lax
from jax.experimental import pallas as pl
from jax.experimental.pallas import tpu as pltpu

F32 = jnp.float32
BF16 = jnp.bfloat16

D_MODEL = 1024
BATCH = 16
SEQ = 256
DEC_BATCH = 4
DEC_SEQ = 1024
PAST_LEN = 256
N_PROMPT = BATCH * SEQ
N_SAMPLE = DEC_BATCH * DEC_SEQ
N_TOK = N_PROMPT + N_SAMPLE
GRID_W = 64
GRID_ROWS = DEC_SEQ // GRID_W
H_A = 4
DK_A = 128
DV_A = 128
H_B = 4
DK_B = 128
DV_B = 128
SHORT_CONV = 5
H_C = 16
HD_C = 64
KH = 8
KW = 16
D_FF = 4 * D_MODEL
EPS = 1e-6
NEG_INF = -1e30
LANES = 128
SUBLANES = 8
BF16_ROWS = 16
VMEM_LIMIT = 56 * 1024 * 1024

N_MOD_ROWS = SUBLANES
N_GATE_AB = 16
D_MAIN_AB = 3 * H_A * DK_A + 2 * H_A * DV_A + H_B * (2 * DK_B + DV_B) + H_B * DV_B
CHUNK = 32
GBLK = LANES


def _cparams(*sem):
    return pltpu.CompilerParams(dimension_semantics=sem, vmem_limit_bytes=VMEM_LIMIT)


def _sigmoid(x):
    return 0.5 * jnp.tanh(0.5 * x) + 0.5


def _silu(x):
    return x * _sigmoid(x)


def _dot(a, b):
    return jnp.dot(a, b, preferred_element_type=F32)


def _dot_nt(a, b):
    return lax.dot_general(a, b, (((1,), (1,)), ((), ())), preferred_element_type=F32)


def _dot_tn(a, b):
    return lax.dot_general(a, b, (((0,), (0,)), ((), ())), preferred_element_type=F32)


def _split2(x):
    hi = x.astype(BF16)
    lo = (x - hi.astype(F32)).astype(BF16)
    return hi, lo


def _dot_const(m2, x):
    hi, lo = _split2(x)
    return _dot(m2, jnp.concatenate([hi, lo], axis=0))


def _mod_row(i, tm):
    start = i * tm
    return jnp.where(start < N_PROMPT, 0, 1 + (start - N_PROMPT) // DEC_SEQ)


def _mod_slice(mod_ref, row, k):
    return mod_ref[pl.ds(row, 1), k * D_MODEL:(k + 1) * D_MODEL]


def _norm_mod(x, g, sc, sh):
    ms = jnp.mean(x * x, axis=-1, keepdims=True)
    return (x * lax.rsqrt(ms + EPS) * g) * (1.0 + sc) + sh


def _mod_kernel(cond_ref, w_ref, b_ref, side_ref, o_ref, side_o_ref):
    s = _silu(cond_ref[...]).astype(BF16)
    o_ref[0] = _dot(s, w_ref[0].astype(BF16)) + b_ref[0]
    side_o_ref[...] = side_ref[...].astype(BF16)


MOD_TN = 768


def _modulation(cond8, ada_w, ada_b, side):
    depth = ada_w.shape[0]
    nj = ada_w.shape[2] // MOD_TN
    slab = pl.cdiv(side.shape[1], depth * nj * BF16_ROWS) * BF16_ROWS
    return pl.pallas_call(
        _mod_kernel,
        grid=(depth, nj),
        in_specs=[
            pl.BlockSpec((N_MOD_ROWS, D_MODEL), lambda l, j: (0, 0)),
            pl.BlockSpec((1, D_MODEL, MOD_TN), lambda l, j: (l, 0, j)),
            pl.BlockSpec((1, 1, MOD_TN), lambda l, j: (l, 0, j)),
            pl.BlockSpec((None, slab, side.shape[2]), lambda l, j: (0, l * nj + j, 0)),
        ],
        out_specs=[pl.BlockSpec((1, N_MOD_ROWS, MOD_TN), lambda l, j: (l, 0, j)),
                   pl.BlockSpec((slab, side.shape[2]), lambda l, j: (l * nj + j, 0))],
        out_shape=[jax.ShapeDtypeStruct((depth, N_MOD_ROWS, ada_w.shape[2]), F32),
                   jax.ShapeDtypeStruct(side.shape[1:], BF16)],
        compiler_params=_cparams("arbitrary", "arbitrary"),
        name="modulation",
    )(cond8, ada_w, ada_b.reshape(depth, 1, -1), side)


def _stream_specs(n_arrays, tm, width=D_MODEL):
    if n_arrays == 1:
        return [pl.BlockSpec((tm, width), lambda i: (i, 0))]
    npt = N_PROMPT // tm
    return [pl.BlockSpec((tm, width), lambda i: (jnp.minimum(i, npt - 1), 0)),
            pl.BlockSpec((tm, width), lambda i: (jnp.maximum(i - npt, 0), 0))]


def _stream_load(x_refs, tm):
    if len(x_refs) == 1:
        return x_refs[0][...]
    return jnp.where(pl.program_id(0) < N_PROMPT // tm, x_refs[0][...], x_refs[1][...])


def _side_specs(side, n_steps):
    in_specs = [pl.BlockSpec((None, w.shape[1] // n_steps, w.shape[2]), lambda i, l=l: (l, i, 0)) for w, l in side]
    out_specs = [pl.BlockSpec((w.shape[1] // n_steps, w.shape[2]), lambda i: (i, 0)) for w, _ in side]
    shapes = [jax.ShapeDtypeStruct(w.shape[1:], BF16) for w, _ in side]
    return in_specs, out_specs, shapes


def _side_cast(in_refs, out_refs):
    for i_ref, o_ref in zip(in_refs, out_refs):
        o_ref[...] = i_ref[...].astype(BF16)


def _norm_proj_kernel(*refs, tm, n_x, n_w, n_side, w_transposed):
    x_refs, (mod_ref, g_ref) = refs[:n_x], refs[n_x:n_x + 2]
    w_refs = refs[n_x + 2:n_x + 2 + n_w]
    side_in = refs[n_x + 2 + n_w:n_x + 2 + n_w + n_side]
    o_refs = refs[n_x + 2 + n_w + n_side:n_x + 2 + 2 * n_w + n_side]
    side_out = refs[n_x + 2 + 2 * n_w + n_side:]
    row = _mod_row(pl.program_id(0), tm)
    h = _norm_mod(_stream_load(x_refs, tm), g_ref[...], _mod_slice(mod_ref, row, 1), _mod_slice(mod_ref, row, 0)).astype(BF16)
    for w_ref, o_ref in zip(w_refs, o_refs):
        o_ref[...] = _dot_nt(h, w_ref[...]) if w_transposed else _dot(h, w_ref[...])
    _side_cast(side_in, side_out)


def _norm_proj(xs, mod, g, ws, widths=None, side=(), w_transposed=False, tm=512):
    n_w = len(ws)
    widths = widths or [w.shape[0 if w_transposed else 1] for w in ws]
    const = lambda i: (0, 0)
    w_block = (lambda n: (n, D_MODEL)) if w_transposed else (lambda n: (D_MODEL, n))
    side_in_specs, side_out_specs, side_shapes = _side_specs(side, N_TOK // tm)
    return pl.pallas_call(
        functools.partial(_norm_proj_kernel, tm=tm, n_x=len(xs), n_w=n_w, n_side=len(side), w_transposed=w_transposed),
        grid=(N_TOK // tm,),
        in_specs=_stream_specs(len(xs), tm) + [
            pl.BlockSpec(mod.shape, const),
            pl.BlockSpec((1, D_MODEL), const),
        ] + [pl.BlockSpec(w_block(n), const, pipeline_mode=pl.Buffered(1)) for n in widths] + side_in_specs,
        out_specs=[pl.BlockSpec((tm, n), lambda i: (i, 0)) for n in widths] + side_out_specs,
        out_shape=[jax.ShapeDtypeStruct((N_TOK, n), F32) for n in widths] + side_shapes,
        compiler_params=_cparams("arbitrary"),
        name="norm_proj",
    )(*xs, mod, g.reshape(1, D_MODEL), *ws, *[w for w, _ in side])


def _post_kernel(*refs, tm, ff_chunk, n_x, n_y, groups, n_side):
    x_refs = refs[:n_x]
    n_m = sum(n for n, _ in groups)
    m_refs = refs[n_x:n_x + n_m]
    mod_ref, g_ref, wo_ref, w1_ref, w2_ref = refs[n_x + n_m:n_x + n_m + 5]
    side_in = refs[n_x + n_m + 5:n_x + n_m + 5 + n_side]
    y_refs = refs[n_x + n_m + 5 + n_side:n_x + n_m + 5 + n_side + n_y]
    _side_cast(side_in, refs[n_x + n_m + 5 + n_side + n_y:])
    row = _mod_row(pl.program_id(0), tm)
    mix, first_ref, first_col = None, 0, 0
    for n, width in groups:
        part = _stream_load(m_refs[first_ref:first_ref + n], tm).astype(BF16)
        term = _dot(part, wo_ref[first_col:first_col + width, :])
        mix = term if mix is None else mix + term
        first_ref, first_col = first_ref + n, first_col + width
    x1 = _stream_load(x_refs, tm) + _mod_slice(mod_ref, row, 2) * mix
    h = _norm_mod(x1, g_ref[...], _mod_slice(mod_ref, row, 4), _mod_slice(mod_ref, row, 3)).astype(BF16)
    acc = jnp.zeros((tm, D_MODEL), F32)
    for k in range(0, D_FF, ff_chunk):
        a = jnp.maximum(_dot(h, w1_ref[:, k:k + ff_chunk]), 0.0)
        acc = acc + _dot((a * a).astype(BF16), w2_ref[k:k + ff_chunk, :])
    y = x1 + _mod_slice(mod_ref, row, 5) * acc
    if n_y == 1:
        y_refs[0][...] = y
    else:
        is_prompt = pl.program_id(0) < N_PROMPT // tm

        @pl.when(is_prompt)
        def _():
            y_refs[0][...] = y

        @pl.when(jnp.logical_not(is_prompt))
        def _():
            y_refs[1][...] = y


def _post_mixer(xs, mixed, mod, g, wo, w1, w2, split_out=False, side=(), tm=512, ff_chunk=1024):
    const = lambda i: (0, 0)
    resident = lambda w: pl.BlockSpec(w.shape, const, pipeline_mode=pl.Buffered(1))
    n_y = 2 if split_out else 1
    rows = (N_PROMPT, N_SAMPLE) if split_out else (N_TOK,)
    groups = tuple((len(grp), grp[0].shape[1]) for grp in mixed)
    mixed_specs = [spec for n, width in groups for spec in _stream_specs(n, tm, width)]
    side_in_specs, side_out_specs, side_shapes = _side_specs(side, N_TOK // tm)
    out = pl.pallas_call(
        functools.partial(_post_kernel, tm=tm, ff_chunk=ff_chunk, n_x=len(xs), n_y=n_y, groups=groups,
                          n_side=len(side)),
        grid=(N_TOK // tm,),
        in_specs=_stream_specs(len(xs), tm) + mixed_specs + [
            pl.BlockSpec(mod.shape, const),
            pl.BlockSpec((1, D_MODEL), const),
            resident(wo), resident(w1), resident(w2),
        ] + side_in_specs,
        out_specs=_stream_specs(n_y, tm) + side_out_specs,
        out_shape=[jax.ShapeDtypeStruct((r, D_MODEL), F32) for r in rows] + side_shapes,
        compiler_params=_cparams("arbitrary"),
        name="post_mixer",
    )(*xs, *[a for grp in mixed for a in grp], mod, g.reshape(1, D_MODEL), wo, w1, w2, *[w for w, _ in side])
    return tuple(out[:n_y]), tuple(out[n_y:])


PAIR = 2 * HD_C


def _pair_consts():
    lane = lax.broadcasted_iota(jnp.int32, (1, PAIR), 1)
    first = lane < HD_C
    ones_col = [jnp.where(lane == HD_C, 1.0, 0.0), jnp.where(lane == 0, 1.0, 0.0)]
    r = lax.broadcasted_iota(jnp.int32, (2 * PAIR, PAIR), 0) % PAIR
    cidx = lax.broadcasted_iota(jnp.int32, (2 * PAIR, PAIR), 1)
    mean2 = jnp.where(r // HD_C == cidx // HD_C, 1.0 / HD_C, 0.0).astype(BF16)
    return first, ones_col, mean2


def _pair_norm(x, w2, mean2):
    hi, lo = _split2(x * x)
    ms = _dot(jnp.concatenate([hi, lo], axis=1), mean2)
    return x * lax.rsqrt(ms + EPS) * w2


def _pair_queries(q, first):
    return [jnp.where(first, q, 0.0).astype(BF16), jnp.where(first, 0.0, q).astype(BF16)]


def _pair_values(v, first, ones_col):
    return [jnp.where(first, v, ones_col[0]).astype(BF16), jnp.where(first, ones_col[1], v).astype(BF16)]


def _pair_output(o_aug, first):
    den = [o_aug[0][:, HD_C:HD_C + 1], o_aug[1][:, 0:1]]
    return jnp.where(first, o_aug[0] / den[0], o_aug[1] / den[1])


def _row_max(*pieces):
    tiles = [p[:, i:i + LANES] for p in pieces for i in range(0, p.shape[1], LANES)]
    return jnp.max(functools.reduce(jnp.maximum, tiles), axis=-1, keepdims=True)


CTX_PAIRS = 8


def _ctx_attn_kernel(q_ref, k_ref, v_ref, qn_ref, kn_ref, o_ref, kc_ref, vc_ref):
    first, ones_col, mean2 = _pair_consts()
    lanes = [slice(p * PAIR, (p + 1) * PAIR) for p in range(CTX_PAIRS)]
    qn = [_pair_norm(q_ref[:, ln], qn_ref[...], mean2) * HD_C ** -0.5 for ln in lanes]
    kn = [_pair_norm(k_ref[:, ln], kn_ref[...], mean2) for ln in lanes]
    v = [v_ref[:, ln] for ln in lanes]
    kt = [x.T for x in kn]
    for p in range(CTX_PAIRS):
        kc_ref[0, 0, lanes[p], :] = kt[p]
        vc_ref[0, 0, lanes[p], :] = v[p].T
    q = [_pair_queries(x, first) for x in qn]
    va = [_pair_values(x, first, ones_col) for x in v]
    s = [[_dot(q[p][j], kt[p].astype(BF16)) for j in range(2)] for p in range(CTX_PAIRS)]
    pr = [[jnp.exp(x - _row_max(x)).astype(BF16) for x in sp] for sp in s]
    for p in range(CTX_PAIRS):
        o_ref[:, lanes[p]] = _pair_output([_dot(pr[p][j], va[p][j]) for j in range(2)], first)


def _ctx_attention(qkv, qn, kn):
    heads = 2 * CTX_PAIRS
    ng = H_C // heads
    wide = CTX_PAIRS * PAIR
    blk = lambda off: pl.BlockSpec((SEQ, wide), lambda b, p: (b, off + p))
    cache_spec = pl.BlockSpec((1, 1, wide, SEQ), lambda b, p: (b, 0, p, 0))
    cache_shape = jax.ShapeDtypeStruct((BATCH, 1, H_C * HD_C, SEQ), F32)
    return pl.pallas_call(
        _ctx_attn_kernel,
        grid=(BATCH, ng),
        in_specs=[blk(0), blk(ng), blk(2 * ng),
                  pl.BlockSpec((1, PAIR), lambda b, p: (0, 0)),
                  pl.BlockSpec((1, PAIR), lambda b, p: (0, 0))],
        out_specs=[pl.BlockSpec((SEQ, wide), lambda b, p: (b, p)), cache_spec, cache_spec],
        out_shape=[jax.ShapeDtypeStruct((N_PROMPT, D_MODEL), F32), cache_shape, cache_shape],
        compiler_params=_cparams("arbitrary", "arbitrary"),
        name="ctx_attention",
    )(qkv, qkv, qkv, jnp.tile(qn.reshape(1, HD_C), (1, 2)), jnp.tile(kn.reshape(1, HD_C), (1, 2)))


def _na_row_start(r):
    return min(max(r - KH // 2, 0), GRID_ROWS - KH)


NA_ROW_GROUP = 4


def _na_attn_kernel(q_ref, k_ref, v_ref, kc_ref, vc_ref, qn_ref, kn_ref, bias_ref, o_ref, qs, ks, vs, bias_s):
    first, ones_col, mean2 = _pair_consts()

    @pl.when(pl.program_id(1) == 0)
    def _():
        q_col = lax.broadcasted_iota(jnp.int32, (GRID_W, PAIR), 0)
        lane = lax.broadcasted_iota(jnp.int32, (GRID_W, PAIR), 1)
        k_col = lane % GRID_W
        w0 = jnp.clip(q_col - KW // 2, 0, GRID_W - KW)
        outside = jnp.where((k_col >= w0) & (k_col < w0 + KW), 0.0, NEG_INF)
        n_dr = 2 * KH - 1
        for j in range(2):
            band = []
            for dr in range(n_dr):
                row = jnp.broadcast_to(bias_ref[j, dr:dr + 1, :], (GRID_W, PAIR))
                band.append([pltpu.roll(row, (half * GRID_W - (KW - 1)) % PAIR, axis=1, stride=1, stride_axis=0)
                             for half in range(2)])
            zero = jnp.zeros((GRID_W, PAIR), F32)
            for cp in range(2):
                for t in range(KH):
                    lo, hi = 2 * t + cp, 2 * t + cp + 1
                    tile = jnp.where(lane < GRID_W, band[lo][0] if lo < n_dr else zero,
                                     band[hi][1] if hi < n_dr else zero)
                    bias_s[j, cp, :, t * PAIR:(t + 1) * PAIR] = tile + outside

    q2 = _pair_queries(_pair_norm(q_ref[...], qn_ref[...], mean2) * HD_C ** -0.5, first)
    v2 = _pair_values(v_ref[...], first, ones_col)
    ks[...] = _pair_norm(k_ref[...], kn_ref[...], mean2).astype(BF16)
    for j in range(2):
        qs[j] = q2[j]
        vs[j] = v2[j]
    kt_ctx = kc_ref[0, 0].astype(BF16)
    vt = vc_ref[0, 0]
    ch = lax.broadcasted_iota(jnp.int32, vt.shape, 0)
    vt_ctx = [jnp.where(ch < HD_C, vt, jnp.where(ch == HD_C, 1.0, 0.0)).astype(BF16),
              jnp.where(ch < HD_C, jnp.where(ch == 0, 1.0, 0.0), vt).astype(BF16)]
    for r0 in range(0, GRID_ROWS, NA_ROW_GROUP):
        units = [(r, j) for r in range(r0, r0 + NA_ROW_GROUP) for j in range(2)]
        rows = {r: slice(r * GRID_W, (r + 1) * GRID_W) for r, _ in units}
        wins = {r: slice(_na_row_start(r) * GRID_W, (_na_row_start(r) + KH) * GRID_W) for r, _ in units}
        s_ctx_all = [_dot(qs[j, r0 * GRID_W:(r0 + NA_ROW_GROUP) * GRID_W, :], kt_ctx) for j in range(2)]
        s_ctx = [s_ctx_all[j][(r - r0) * GRID_W:(r - r0 + 1) * GRID_W] for r, j in units]
        s_win = []
        for r, j in units:
            dr0 = KH - 1 - (r - _na_row_start(r))
            lane0 = (dr0 - dr0 % 2) * GRID_W
            s_win.append(_dot_nt(qs[j, rows[r], :], ks[wins[r], :])
                         + bias_s[j, dr0 % 2, :, lane0:lane0 + KH * GRID_W])
        m = [_row_max(a, b) for a, b in zip(s_win, s_ctx)]
        p_win = [jnp.exp(a - mm).astype(BF16) for a, mm in zip(s_win, m)]
        p_ctx = [jnp.exp(b - mm).astype(BF16) for b, mm in zip(s_ctx, m)]
        o_aug = [_dot(p_win[i], vs[j, wins[r], :]) + _dot_nt(p_ctx[i], vt_ctx[j]) for i, (r, j) in enumerate(units)]
        for i in range(0, len(units), 2):
            o_ref[rows[units[i][0]], :] = _pair_output(o_aug[i:i + 2], first)


NA_BIAS_LANES = 2 * KH * GRID_W


def _na_attention(qkv, cache_kt, cache_vt, qn, kn, rpb):
    nhp = H_C // 2
    row0 = N_PROMPT // DEC_SEQ
    blk = lambda off: pl.BlockSpec((DEC_SEQ, 2 * HD_C), lambda p, b: (row0 + b, off + p))
    cache_spec = pl.BlockSpec((1, 1, PAIR, PAST_LEN), lambda p, b: (b, 0, p, 0))
    rpb_rows = 2 * KH
    bias = jnp.pad(rpb.astype(F32), ((0, 0), (0, rpb_rows - rpb.shape[1]), (0, PAIR - rpb.shape[2])))
    return pl.pallas_call(
        _na_attn_kernel,
        grid=(nhp, DEC_BATCH),
        in_specs=[blk(0), blk(nhp), blk(2 * nhp), cache_spec, cache_spec,
                  pl.BlockSpec((1, PAIR), lambda p, b: (0, 0)),
                  pl.BlockSpec((1, PAIR), lambda p, b: (0, 0)),
                  pl.BlockSpec((2, rpb_rows, PAIR), lambda p, b: (p, 0, 0))],
        out_specs=pl.BlockSpec((DEC_SEQ, 2 * HD_C), lambda p, b: (b, p)),
        out_shape=jax.ShapeDtypeStruct((N_SAMPLE, D_MODEL), F32),
        scratch_shapes=[pltpu.VMEM((2, DEC_SEQ, PAIR), BF16), pltpu.VMEM((DEC_SEQ, PAIR), BF16),
                        pltpu.VMEM((2, DEC_SEQ, PAIR), BF16), pltpu.VMEM((2, 2, GRID_W, NA_BIAS_LANES), F32)],
        compiler_params=_cparams("arbitrary", "arbitrary"),
        name="na_attention",
    )(qkv, qkv, qkv, cache_kt, cache_vt, jnp.tile(qn.reshape(1, HD_C), (1, 2)), jnp.tile(kn.reshape(1, HD_C), (1, 2)),
      bias)


def _seq_layout(prompt):
    return (SEQ, BATCH, 0) if prompt else (DEC_SEQ, DEC_BATCH, N_PROMPT // DEC_SEQ)


def _flip_blocks(m, c):
    r, s = m.shape
    return m.reshape(r // c, c, s // c, c)[:, ::-1, :, ::-1].reshape(r, s)


def _rms_gate(x, gn, gate):
    ms = jnp.mean(x * x, axis=-1, keepdims=True)
    return x * lax.rsqrt(ms + EPS) * gn * _silu(gate)


HG_LEVELS = tuple(CHUNK >> (i + 1) for i in range(CHUNK.bit_length() - 1))
HG_NL = len(HG_LEVELS)
HG_STACK = (HG_NL + 1) * CHUNK
TOT_ROWS = BF16_ROWS
HG_ROWS = (HG_NL + 2) * CHUNK + TOT_ROWS


def _hgrn_consts():
    c = CHUNK
    level_rows = []
    mask = np.zeros((HG_STACK, HG_STACK), np.float32)
    mask[:c, :c] = np.eye(c)
    for li, b in enumerate(HG_LEVELS):
        m = np.zeros((c, c), np.float32)
        blk = np.zeros((c, c), np.float32)
        for t in range(c):
            mid = (t // (2 * b)) * 2 * b + b
            if t >= mid:
                m[t, mid:t + 1] = 1.0
                blk[t, mid - b:mid] = 1.0
            else:
                m[t, t + 1:mid] = 1.0
        level_rows.append(m)
        mask[(li + 1) * c:(li + 2) * c, (li + 1) * c:(li + 2) * c] = blk
    dq = np.tril(np.ones((c, c), np.float32))
    dk = np.triu(np.ones((c, c), np.float32), 1)
    body = np.concatenate(level_rows + [dq, dk], axis=0)
    tot = np.ones((TOT_ROWS, c), np.float32)
    mcs, masks = [], []
    for reverse in (False, True):
        bm = _flip_blocks(body, c) if reverse else body
        mk = _flip_blocks(mask, c) if reverse else mask
        mc = np.concatenate([bm, tot], axis=0)
        mcs.append(np.concatenate([mc, mc], axis=1))
        masks.append(mk)
    return jnp.asarray(np.stack(mcs), BF16), jnp.asarray(np.stack(masks), F32)


HG_FAST = 64
HG_HALF = HG_FAST // 2
HG_FAST_ROWS = 4 * HG_FAST + TOT_ROWS
HG_SAFE_EXP = 40.0
HG_FAST_STEPS = 4


def _hgrn_fast_consts(seq):
    c, m = HG_FAST, HG_HALF
    aq = np.zeros((c, c), np.float32)
    for t in range(c):
        if t >= m:
            aq[t, m:t + 1] = 1.0
        else:
            aq[t, t + 1:m] = -1.0
    dq = np.tril(np.ones((c, c), np.float32))
    dk = np.triu(np.ones((c, c), np.float32), 1)
    body = np.concatenate([aq, -aq, dq, dk], axis=0)
    tot = np.ones((TOT_ROWS, c), np.float32)
    causal = np.tril(np.ones((c, c), np.float32))
    mfs, masks = [], []
    for reverse in (False, True):
        bm = _flip_blocks(body, c) if reverse else body
        mf = np.concatenate([bm, tot], axis=0)
        mfs.append(np.concatenate([mf, mf], axis=1))
        masks.append(causal.T if reverse else causal)
    n_half = seq // m
    half = np.zeros((max(n_half, BF16_ROWS), seq), np.float32)
    for i in range(n_half):
        half[i, i * m:(i + 1) * m] = 1.0
    return jnp.asarray(np.stack(mfs), BF16), jnp.asarray(np.stack(masks), F32), jnp.asarray(half, BF16)


def _hgrn_kernel(*refs, seq, has_s0, emit_state):
    it = iter(refs)
    qa_ref, ff_ref, fb_ref, ia_ref, ga_ref, lb_ref, gn_ref, mc_ref, mask_ref = [next(it) for _ in range(9)]
    mf_ref, causal_ref, half_ref = [next(it) for _ in range(3)]
    s0_ref = next(it) if has_s0 else None
    o_ref = next(it)
    st_ref = next(it) if emit_state else None
    s_scr, acc, f_s, lf_s = [next(it) for _ in range(4)]
    c = CHUNK
    n_chunks = seq // c
    combos = [(d, h) for d in range(2) for h in range(H_A)]
    lanes = [slice(h * DK_A, (h + 1) * DK_A) for h in range(H_A)]
    add = lambda a, b: a + b

    lb_raw = lb_ref[...]
    lb_e = jnp.exp(lb_raw - jnp.max(lb_raw, axis=0, keepdims=True))
    lb_all = lb_e[0:1] / jnp.sum(lb_e, axis=0, keepdims=True)

    for d in range(2):
        for h in range(H_A):
            s_scr[d, h] = s0_ref[0, 0, d, h].T if has_s0 else jnp.zeros((DV_A, DK_A), F32)
    acc[...] = jnp.zeros(acc.shape, F32)

    worst = []
    for d, fr_ref in enumerate((ff_ref, fb_ref)):
        f = lb_all + (1.0 - lb_all) * _sigmoid(fr_ref[...])
        lf = jnp.log(f)
        f_s[d] = f
        lf_s[d] = lf
        worst.append(jnp.max(_dot(half_ref[...], (-lf).astype(BF16))))
    safe = jnp.maximum(worst[0], worst[1]) <= HG_SAFE_EXP

    def fast_body(n, carry):
        cf = HG_FAST
        n_fast = seq // cf
        steps = range(HG_FAST_STEPS)
        chunk = lambda d, t: (n * HG_FAST_STEPS + t) if d == 0 else (n_fast - 1 - n * HG_FAST_STEPS - t)
        rows = [[pl.ds(pl.multiple_of(chunk(d, t) * cf, cf), cf) for t in steps] for d in range(2)]
        units = [(t, d, h) for t in steps for d, h in combos]
        e_all = [[jnp.exp(_dot_const(mf_ref[d], lf_s[d, rows[d][t], :])) for t in steps] for d in range(2)]
        q_all = [[_silu(qa_ref[rows[d][t], :]) * DK_A ** -0.5 for t in steps] for d in range(2)]
        k_all = [[1.0 - f_s[d, rows[d][t], :] for t in steps] for d in range(2)]
        v_all = [[ia_ref[rows[d][t], :].astype(BF16) for t in steps] for d in range(2)]
        qs = {u: q_all[u[1]][u[0]][:, lanes[u[2]]] for u in units}
        ks = {u: k_all[u[1]][u[0]][:, lanes[u[2]]] for u in units}
        vs = {u: v_all[u[1]][u[0]][:, lanes[u[2]]] for u in units}
        es = {u: [e_all[u[1]][u[0]][i * cf:(i + 1) * cf, lanes[u[2]]] for i in range(4)] for u in units}
        p = {u: jnp.where(causal_ref[u[1]] > 0.0,
                          _dot_nt((qs[u] * es[u][0]).astype(BF16), (ks[u] * es[u][1]).astype(BF16)), 0.0).astype(BF16)
             for u in units}
        intra = {u: _dot(p[u], vs[u]) for u in units}
        upd = {u: _dot_tn(vs[u], (ks[u] * es[u][3]).astype(BF16)) for u in units}
        qdec = {u: (qs[u] * es[u][2]).astype(BF16) for u in units}
        st = {(d, h): s_scr[d, h] for d, h in combos}
        o = {}
        for t in steps:
            for d, h in combos:
                u = (t, d, h)
                o[u] = intra[u] + _dot_nt(qdec[u], st[d, h].astype(BF16))
                st[d, h] = st[d, h] * e_all[d][t][4 * cf:4 * cf + 1, lanes[h]] + upd[u]
        for t in steps:
            for d in range(2):
                acc[rows[d][t], :] += jnp.concatenate([o[t, d, h] for h in range(H_A)], axis=1)
        for d, h in combos:
            s_scr[d, h] = st[d, h]
        return carry

    def body(n, carry):
        rows = [pl.ds(pl.multiple_of((n if d == 0 else n_chunks - 1 - n) * c, c), c) for d in range(2)]
        f_all = [f_s[d, rows[d], :] for d in range(2)]
        e_all = [jnp.exp(_dot_const(mc_ref[d], lf_s[d, rows[d], :])) for d in range(2)]
        q_all = [_silu(qa_ref[rows[d], :]) * DK_A ** -0.5 for d in range(2)]
        v_all = [ia_ref[rows[d], :].astype(BF16) for d in range(2)]
        st = [s_scr[d, h] for d, h in combos]
        qs, ks, vs, es = [], [], [], []
        for d, h in combos:
            qs.append(q_all[d][:, lanes[h]])
            ks.append(1.0 - f_all[d][:, lanes[h]])
            vs.append(v_all[d][:, lanes[h]])
            es.append(e_all[d][:, lanes[h]])
        lvl = [[e[i * c:(i + 1) * c] for i in range(HG_NL + 2)] for e in es]
        qst = [jnp.concatenate([q] + [q * l[i] for i in range(HG_NL)], axis=0).astype(BF16) for q, l in zip(qs, lvl)]
        kst = [jnp.concatenate([k] + [k * l[i] for i in range(HG_NL)], axis=0).astype(BF16) for k, l in zip(ks, lvl)]
        r = [(_dot_nt(qst[i], kst[i]) * mask_ref[d]).astype(BF16) for i, (d, h) in enumerate(combos)]
        ost = [_dot(r[i], jnp.concatenate([vs[i]] * (HG_NL + 1), axis=0)) for i in range(len(combos))]
        inter = [_dot_nt((qs[i] * lvl[i][HG_NL]).astype(BF16), st[i].astype(BF16)) for i in range(len(combos))]
        upd = [_dot_tn(vs[i], (ks[i] * lvl[i][HG_NL + 1]).astype(BF16)) for i in range(len(combos))]
        o = [functools.reduce(lambda a, b: a + b, [ost[i][j * c:(j + 1) * c] for j in range(HG_NL + 1)]) + inter[i]
             for i in range(len(combos))]
        for d in range(2):
            acc[rows[d], :] += jnp.concatenate(o[d * H_A:(d + 1) * H_A], axis=1)
        for i, (d, h) in enumerate(combos):
            e_tot = es[i][(HG_NL + 2) * c:(HG_NL + 2) * c + 1]
            s_scr[d, h] = st[i] * e_tot + upd[i]
        return carry

    @pl.when(safe)
    def _():
        lax.fori_loop(0, seq // (HG_FAST * HG_FAST_STEPS), fast_body, 0)

    @pl.when(jnp.logical_not(safe))
    def _():
        lax.fori_loop(0, n_chunks, body, 0)

    for h in range(H_A):
        ln = slice(h * DV_A, (h + 1) * DV_A)
        o_ref[:, ln] = _rms_gate(acc[:, ln], gn_ref[...], ga_ref[:, ln])
    if emit_state:
        for d in range(2):
            for h in range(H_A):
                st_ref[0, 0, d, h] = s_scr[d, h].T


def _hgrn(proj, hgrn_lb, gn, consts, prompt, s0=None, layer=0):
    seq, nb, rb0 = _seq_layout(prompt)
    consts = list(consts) + list(_hgrn_fast_consts(seq))
    wa = H_A * DK_A
    blk = lambda j: pl.BlockSpec((seq, wa), lambda b: (rb0 + b, j))
    const2 = lambda b: (0, 0)
    st_block = (1, 1, 2, H_A, DK_A, DV_A)
    in_specs = [blk(0), blk(1), blk(2), blk(3), blk(4),
                pl.BlockSpec(hgrn_lb.shape, const2), pl.BlockSpec((1, DV_A), const2)]
    in_specs += [pl.BlockSpec(m.shape, lambda b, nd=m.ndim: (0,) * nd) for m in consts]
    args = [proj] * 5 + [hgrn_lb, gn.reshape(1, DV_A)] + consts
    if s0 is not None:
        in_specs.append(pl.BlockSpec(st_block, lambda b: (b, layer, 0, 0, 0, 0)))
        args.append(s0)
    out_specs = [pl.BlockSpec((seq, wa), lambda b: (b, 0))]
    out_shape = [jax.ShapeDtypeStruct((nb * seq, wa), F32)]
    if prompt:
        out_specs.append(pl.BlockSpec(st_block, lambda b: (b, 0, 0, 0, 0, 0)))
        out_shape.append(jax.ShapeDtypeStruct((nb, 1, 2, H_A, DK_A, DV_A), F32))
    return pl.pallas_call(
        functools.partial(_hgrn_kernel, seq=seq, has_s0=s0 is not None, emit_state=prompt),
        grid=(nb,),
        in_specs=in_specs,
        out_specs=out_specs,
        out_shape=out_shape,
        scratch_shapes=[pltpu.VMEM((2, H_A, DV_A, DK_A), F32), pltpu.VMEM((seq, wa), F32),
                        pltpu.VMEM((2, seq, wa), F32), pltpu.VMEM((2, seq, wa), F32)],
        compiler_params=_cparams("arbitrary"),
        name="hgrn_prompt" if prompt else "hgrn_sample",
    )(*args)


GD_SUB = GBLK // CHUNK
GD_BLOCKS_PER_ITER = 2
GD_PROMPT_SEQS = 2
GD_ROWS = 2 * GBLK + TOT_ROWS


def _gdn_consts():
    n, c = GBLK, CHUNK
    same = (np.arange(n)[:, None] // c) == (np.arange(n)[None, :] // c)
    tri = (same & (np.arange(n)[None, :] <= np.arange(n)[:, None])).astype(np.float32)
    sup = (same & (np.arange(n)[None, :] > np.arange(n)[:, None])).astype(np.float32)
    tot = np.zeros((TOT_ROWS, n), np.float32)
    for s in range(GD_SUB):
        tot[s, s * c:(s + 1) * c] = 1.0
    mgs, tts, tris = [], [], []
    for reverse in (False, True):
        t = _flip_blocks(tri, c) if reverse else tri
        s = _flip_blocks(sup, c) if reverse else sup
        mg = np.concatenate([t, s, tot], axis=0)
        mgs.append(np.concatenate([mg, mg], axis=1))
        tts.append(np.concatenate([t.T, t.T], axis=0))
        tris.append(t)
    tris.append(same.astype(np.float32))
    return jnp.asarray(np.stack(mgs), BF16), jnp.asarray(np.stack(tts), BF16), jnp.asarray(np.stack(tris), F32)


def _softplus(x):
    return jnp.maximum(x, 0.0) + jnp.log(1.0 + jnp.exp(-jnp.abs(x)))


CONV_PAD = SUBLANES


def _conv_silu(x, w, seq):
    half = SHORT_CONV // 2
    pad = jnp.zeros((CONV_PAD, x.shape[1]), x.dtype)
    xe = jnp.concatenate([pad, x, pad], axis=0)
    acc = xe * w[half:half + 1]
    for j in range(SHORT_CONV):
        shift = half - j
        if shift != 0:
            acc = acc + pltpu.roll(xe, shift % (seq + 2 * CONV_PAD), axis=0) * w[j:j + 1]
    return _silu(acc[CONV_PAD:seq + CONV_PAD])


def _l2norm_heads(x, n_heads, width, scale):
    outs = []
    for h in range(n_heads):
        xh = x[:, h * width:(h + 1) * width]
        outs.append(xh * (lax.rsqrt(jnp.sum(xh * xh, axis=-1, keepdims=True) + EPS) * scale))
    return jnp.concatenate(outs, axis=-1)


def _gdn_kernel(*refs, seq, n_seq, has_s0, emit_state):
    it = iter(refs)
    (q_ref, k_ref, v_ref, gb_ref, gate_ref, cw_ref, alog_ref, dt_ref, gn_ref,
     mg_ref, tt_ref, tri_ref) = [next(it) for _ in range(12)]
    s0_ref = next(it) if has_s0 else None
    o_ref = next(it)
    st_ref = next(it) if emit_state else None
    qn, kn, vn, u_s, w_s, qg_s, kdt_s, at_s, et_s, s_scr, acc = [next(it) for _ in range(11)]
    c = CHUNK
    n_chunks = seq // c
    n_blocks = n_seq * seq // GBLK
    wq = H_B * DK_B
    n_dh = 2 * H_B
    combos = [(d, h) for d in range(2) for h in range(H_B)]
    lanes = [slice(h * DK_B, (h + 1) * DK_B) for h in range(H_B)]

    for s in range(n_seq):
        sr = slice(s * seq, (s + 1) * seq)
        qn[sr, :] = _l2norm_heads(_conv_silu(q_ref[sr, :], cw_ref[:, 0:wq], seq), H_B, DK_B, DK_B ** -0.5)
        kn[sr, :] = _l2norm_heads(_conv_silu(k_ref[sr, :], cw_ref[:, wq:2 * wq], seq), H_B, DK_B, 1.0)
        vn[sr, :] = _conv_silu(v_ref[sr, :], cw_ref[:, 2 * wq:3 * wq], seq)
        for i in range(n_dh):
            s_scr[s * n_dh + i] = (s0_ref[s, 0, i // H_B, i % H_B] if has_s0 else jnp.zeros((DK_B, DV_B), F32))
    acc[...] = jnp.zeros(acc.shape, F32)

    eye = (lax.broadcasted_iota(jnp.int32, (GBLK, GBLK), 0)
           == lax.broadcasted_iota(jnp.int32, (GBLK, GBLK), 1)).astype(F32)
    eye_pk = (lax.broadcasted_iota(jnp.int32, (c, GBLK), 0)
              == lax.broadcasted_iota(jnp.int32, (c, GBLK), 1) % c).astype(F32)
    bwd_lane = lax.broadcasted_iota(jnp.int32, (1, LANES), 1) % n_dh >= H_B
    add = lambda a, b: a + b

    same_chunk = tri_ref[2].astype(BF16)

    def expand(pk):
        return jnp.concatenate([pk] * GD_SUB, axis=0) * same_chunk

    def pack(bd):
        return functools.reduce(add, [bd[s * c:(s + 1) * c] for s in range(GD_SUB)])

    def weights(hi, lo):
        return jnp.concatenate([expand(hi), expand(lo)], axis=1)

    def dot3_split(a_hi, a_lo, w2):
        m, n = a_hi.shape[0], w2.shape[1] // 2
        t = _dot(jnp.concatenate([a_hi, a_lo], axis=0), w2)
        return t[:m, :n] + t[m:, :n] + t[:m, n:]

    def block_body(it, carry):
        blks = [it * GD_BLOCKS_PER_ITER + o for o in range(GD_BLOCKS_PER_ITER)]
        units = [(o, d, h) for o in range(GD_BLOCKS_PER_ITER) for d, h in combos]
        idx = lambda d, h: d * H_B + h
        col = lambda x, j: jnp.broadcast_to(x[:, j:j + 1], (GBLK, DK_B))
        rows = [pl.ds(pl.multiple_of(b * GBLK, GBLK), GBLK) for b in blks]
        gates = [gate_ref[r, :] for r in rows]
        glog_all = [-jnp.exp(alog_ref[...]) * _softplus(x + dt_ref[...]) for x in gates]
        beta_all = [_sigmoid(x) for x in gates]
        g2 = [jnp.concatenate(_split2(x), axis=0) for x in glog_all]
        dg = [[_dot(mg_ref[d], x) for d in range(2)] for x in g2]
        dsel = [jnp.where(bwd_lane, x[1], x[0]) for x in dg]
        eg_all = [jnp.exp(x) for x in dsel]
        gt = [[_dot_tn(x, tt_ref[d]) for d in range(2)] for x in g2]
        qs = [[qn[r, ln] for ln in lanes] for r in rows]
        ks = [[kn[r, ln] for ln in lanes] for r in rows]
        vs = [[vn[r, ln] for ln in lanes] for r in rows]
        betas = [col(beta_all[o], n_dh + idx(d, h)) for o, d, h in units]
        kbs = [ks[o][h] * betas[u] for u, (o, d, h) in enumerate(units)]
        kb_of = {unit: kbs[u] for u, unit in enumerate(units)}
        kk = {(o, h): _dot_nt(jnp.concatenate([qs[o][h], kb_of[o, 0, h], kb_of[o, 1, h]], axis=0).astype(BF16),
                              ks[o][h].astype(BF16))
              for o in range(GD_BLOCKS_PER_ITER) for h in range(H_B)}
        decay = []
        for o, d, h in units:
            inside = tri_ref[d] > 0.0
            gd = col(dsel[o][:GBLK], idx(d, h)) - gt[o][d][idx(d, h):idx(d, h) + 1, :]
            decay.append(jnp.where(inside, jnp.exp(jnp.where(inside, gd, 0.0)), 0.0))
        attn = [kk[o, h][:GBLK] * decay[u] for u, (o, d, h) in enumerate(units)]
        p_pk = [pack(kk[o, h][(1 + d) * GBLK:(2 + d) * GBLK] * decay[u] * (1.0 - eye))
                for u, (o, d, h) in enumerate(units)]
        x_pk = [eye_pk - p for p in p_pk]
        p_sp = [_split2(p) for p in p_pk]
        p_w = [weights(*s) for s in p_sp]
        for _ in range(CHUNK.bit_length() - 2):
            p_pk = [dot3_split(*s, w) for s, w in zip(p_sp, p_w)]
            p_sp = [_split2(p) for p in p_pk]
            p_w = [weights(*s) for s in p_sp]
            x_pk = [x + dot3_split(*_split2(x), w) for x, w in zip(x_pk, p_w)]
        eg_col = [col(eg_all[o][:GBLK], idx(d, h)) for o, d, h in units]
        ekd_col = [col(eg_all[o][GBLK:2 * GBLK], idx(d, h)) for o, d, h in units]
        rhs = [_split2(jnp.concatenate([vs[o][h] * betas[u], kbs[u] * eg_col[u]], axis=1))
               for u, (o, d, h) in enumerate(units)]
        t_sp = [[expand(part) for part in _split2(x)] for x in x_pk]
        uw = [_dot(t[0], r[0]) + (_dot(t[0], r[1]) + _dot(t[1], r[0])) for t, r in zip(t_sp, rhs)]
        for u, (o, d, h) in enumerate(units):
            i = idx(d, h)
            qg = (qs[o][h] * eg_col[u]).astype(BF16)
            kdt_s[i, blks[o]] = (ks[o][h] * ekd_col[u]).T.astype(BF16)
            for s in range(GD_SUB):
                cn = blks[o] * GD_SUB + s
                r = slice(s * c, (s + 1) * c)
                u_s[i, cn] = uw[u][r, :DV_B]
                w_s[i, cn] = uw[u][r, DV_B:].astype(BF16)
                qg_s[i, cn] = qg[r]
                at_s[i, cn] = attn[u][r].astype(BF16)
                et_s[i, cn] = jnp.broadcast_to(eg_all[o][2 * GBLK + s:2 * GBLK + s + 1, i:i + 1], (SUBLANES, DV_B))
        return carry

    n_iter = n_blocks // GD_BLOCKS_PER_ITER
    lax.fori_loop(0, n_iter, block_body, 0, unroll=True)

    def chunk_body(n, carry):
        chains = [(s, d, h) for s in range(n_seq) for d, h in combos]
        idx = lambda d, h: d * H_B + h
        cn = {(s, d): s * n_chunks + (n if d == 0 else n_chunks - 1 - n) for s in range(n_seq) for d in range(2)}
        rows = {key: pl.ds(pl.multiple_of(v * c, c), c) for key, v in cn.items()}
        sub_of_row = lax.broadcasted_iota(jnp.int32, (GBLK, 1), 0) // c
        in_chunk = {key: sub_of_row == v % GD_SUB for key, v in cn.items()}
        st = {ch: s_scr[ch[0] * n_dh + idx(ch[1], ch[2])] for ch in chains}
        ws = {(s, d, h): _dot(jnp.concatenate([w_s[idx(d, h), cn[s, d]], qg_s[idx(d, h), cn[s, d]]], axis=0),
                              st[s, d, h].astype(BF16)) for s, d, h in chains}
        vblk = {(s, d, h): jnp.where(in_chunk[s, d],
                                     jnp.concatenate([u_s[idx(d, h), cn[s, d]] - ws[s, d, h][:c]] * GD_SUB, axis=0),
                                     0.0).astype(BF16) for s, d, h in chains}
        r = {(s, d, h): _dot(jnp.concatenate([at_s[idx(d, h), cn[s, d]], kdt_s[idx(d, h), cn[s, d] // GD_SUB]], axis=0),
                             vblk[s, d, h]) for s, d, h in chains}
        for s in range(n_seq):
            for d in range(2):
                acc[rows[s, d], :] += jnp.concatenate([ws[s, d, h][c:] + r[s, d, h][:c] for h in range(H_B)], axis=1)
        for s, d, h in chains:
            s_scr[s * n_dh + idx(d, h)] = st[s, d, h] * et_s[idx(d, h), cn[s, d]][0:1] + r[s, d, h][c:]
        return carry

    lax.fori_loop(0, n_chunks, chunk_body, 0)

    for h in range(H_B):
        ln = slice(h * DV_B, (h + 1) * DV_B)
        o_ref[:, ln] = _rms_gate(acc[:, ln], gn_ref[...], gb_ref[:, ln])
    if emit_state:
        for s in range(n_seq):
            for i in range(n_dh):
                st_ref[s, 0, i // H_B, i % H_B] = s_scr[s * n_dh + i]


def _gdn(proj, gates, conv_w, a_log, dt_bias, gn, consts, prompt, s0=None, layer=0):
    seq, nb, rb0 = _seq_layout(prompt)
    n_seq = GD_PROMPT_SEQS if prompt else 1
    rows = n_seq * seq
    n_chunks = rows // CHUNK
    wq = H_B * DK_B
    blk = lambda j: pl.BlockSpec((rows, wq), lambda b: (rb0 + b, j))
    const2 = lambda b: (0, 0)
    const3 = lambda b: (0, 0, 0)
    st_block = (n_seq, 1, 2, H_B, DK_B, DV_B)
    pad_row = lambda p: jnp.pad(p.reshape(1, -1).astype(F32), ((0, 0), (0, LANES - p.size)))
    in_specs = [blk(5), blk(6), blk(7), blk(8),
                pl.BlockSpec((rows, LANES), lambda b: (rb0 + b, 0)),
                pl.BlockSpec((SHORT_CONV, 3 * wq), const2),
                pl.BlockSpec((1, LANES), const2), pl.BlockSpec((1, LANES), const2), pl.BlockSpec((1, DV_B), const2)]
    in_specs += [pl.BlockSpec(m.shape, const3) for m in consts]
    args = [proj] * 4 + [gates, conv_w.reshape(SHORT_CONV, 3 * wq), pad_row(a_log), pad_row(dt_bias),
                         gn.reshape(1, DV_B)] + list(consts)
    if s0 is not None:
        in_specs.append(pl.BlockSpec(st_block, lambda b: (b, layer, 0, 0, 0, 0)))
        args.append(s0)
    out_specs = [pl.BlockSpec((rows, wq), lambda b: (b, 0))]
    out_shape = [jax.ShapeDtypeStruct((nb * seq, wq), F32)]
    if prompt:
        out_specs.append(pl.BlockSpec(st_block, lambda b: (b, 0, 0, 0, 0, 0)))
        out_shape.append(jax.ShapeDtypeStruct((nb, 1, 2, H_B, DK_B, DV_B), F32))
    n_dh = 2 * H_B
    scratch = ([pltpu.VMEM((rows, wq), F32)] * 3
               + [pltpu.VMEM((n_dh, n_chunks, CHUNK, DV_B), F32)]
               + [pltpu.VMEM((n_dh, n_chunks, CHUNK, DK_B), BF16)] * 2
               + [pltpu.VMEM((n_dh, rows // GBLK, DK_B, GBLK), BF16),
                  pltpu.VMEM((n_dh, n_chunks, CHUNK, GBLK), BF16),
                  pltpu.VMEM((n_dh, n_chunks, SUBLANES, DV_B), F32),
                  pltpu.VMEM((n_seq * n_dh, DK_B, DV_B), F32),
                  pltpu.VMEM((rows, wq), F32)])
    return pl.pallas_call(
        functools.partial(_gdn_kernel, seq=seq, n_seq=n_seq, has_s0=s0 is not None, emit_state=prompt),
        grid=(nb // n_seq,),
        in_specs=in_specs,
        out_specs=out_specs,
        out_shape=out_shape,
        scratch_shapes=scratch,
        compiler_params=_cparams("arbitrary"),
        name="gdn_prompt" if prompt else "gdn_sample",
    )(*args)


def kernel(x_prompt, x_sample, state_hgrn, state_gdn, cache_na_k, cache_na_v, c, c_ctx, ada_w, ada_b, norm_g, w_in_ab, w_out_ab, hgrn_lb, gdn_conv, gdn_a_log, gdn_dt_bias, gn_hgrn, gn_gdn, w_qkv_na, qn_na, kn_na, rpb_na, w_out_na, w_mlp1, w_mlp2):
    cond = jnp.concatenate([c_ctx[None, :], c, jnp.zeros((N_MOD_ROWS - 1 - DEC_BATCH, D_MODEL), F32)], axis=0)
    mods, w_in_t = _modulation(cond, ada_w, ada_b, jnp.swapaxes(w_in_ab, 1, 2))
    xs = (x_prompt.reshape(N_PROMPT, D_MODEL), x_sample.reshape(N_SAMPLE, D_MODEL))

    w_gate_t = jnp.pad(w_in_t[D_MAIN_AB:], ((0, LANES - N_GATE_AB), (0, 0)))
    proj, gates, wo0, w1_0, w2_0 = _norm_proj(xs, mods[0], norm_g[0, 0], [w_in_t, w_gate_t], widths=[D_MAIN_AB, LANES],
                                              side=((w_out_ab, 0), (w_mlp1, 0), (w_mlp2, 0)), w_transposed=True)
    hg_consts = _hgrn_consts()
    gd_consts = _gdn_consts()
    hg_prompt, new_hgrn = _hgrn(proj, hgrn_lb, gn_hgrn[0], hg_consts, True)
    hg_sample, = _hgrn(proj, hgrn_lb, gn_hgrn[0], hg_consts, False, s0=state_hgrn)
    gd_args = (gdn_conv[0], gdn_a_log[0], gdn_dt_bias[0], gn_gdn[0], gd_consts)
    gd_prompt, new_gdn = _gdn(proj, gates, *gd_args, True)
    gd_sample, = _gdn(proj, gates, *gd_args, False, s0=state_gdn)
    xs, (w_qkv, wo1, w1_1, w2_1) = _post_mixer(xs, [(hg_prompt, hg_sample), (gd_prompt, gd_sample)], mods[0],
                                               norm_g[0, 1], wo0, w1_0, w2_0,
                                               side=((w_qkv_na, 0), (w_out_na, 0), (w_mlp1, 1), (w_mlp2, 1)))

    qkv, = _norm_proj(xs, mods[1], norm_g[1, 0], [w_qkv])
    at_prompt, new_kt, new_vt = _ctx_attention(qkv, qn_na[0], kn_na[0])
    time_minor = lambda a: jnp.swapaxes(a, -1, -2).reshape(a.shape[0], 1, H_C * HD_C, a.shape[3])
    at_sample = _na_attention(qkv, time_minor(cache_na_k), time_minor(cache_na_v), qn_na[0], kn_na[0], rpb_na[0])
    time_major = lambda a: jnp.swapaxes(a.reshape(BATCH, 1, H_C, HD_C, SEQ), -1, -2)
    new_k, new_v = time_major(new_kt), time_major(new_vt)
    (y_prompt, y_sample), _ = _post_mixer(xs, [(at_prompt, at_sample)], mods[1], norm_g[1, 1], wo1, w1_1, w2_1,
                                          split_out=True)

    return (y_prompt.reshape(BATCH, SEQ, D_MODEL), y_sample.reshape(DEC_BATCH, DEC_SEQ, D_MODEL),
            new_hgrn, new_gdn, new_k, new_v)
```

```python
import functools

import numpy as np
import jax
import jax.numpy as jnp
from jax import lax
from jax.experimental import pallas as pl
from jax.experimental.pallas import tpu as pltpu

F32 = jnp.float32
BF16 = jnp.bfloat16

D_MODEL = 1024
BATCH = 16
SEQ = 256
DEC_BATCH = 4
DEC_SEQ = 1024
PAST_LEN = 256
N_PROMPT = BATCH * SEQ
N_SAMPLE = DEC_BATCH * DEC_SEQ
N_TOK = N_PROMPT + N_SAMPLE
GRID_W = 64
GRID_ROWS = DEC_SEQ // GRID_W
H_A = 4
DK_A = 128
DV_A = 128
H_B = 4
DK_B = 128
DV_B = 128
SHORT_CONV = 5
H_C = 16
HD_C = 64
KH = 8
KW = 16
D_FF = 4 * D_MODEL
EPS = 1e-6
NEG_INF = -1e30
LANES = 128
SUBLANES = 8
BF16_ROWS = 16
VMEM_LIMIT = 56 * 1024 * 1024

N_MOD_ROWS = SUBLANES
N_GATE_AB = 16
D_MAIN_AB = 3 * H_A * DK_A + 2 * H_A * DV_A + H_B * (2 * DK_B + DV_B) + H_B * DV_B
CHUNK = 32
GBLK = LANES


def _cparams(*sem):
    return pltpu.CompilerParams(dimension_semantics=sem, vmem_limit_bytes=VMEM_LIMIT)


def _sigmoid(x):
    return 0.5 * jnp.tanh(0.5 * x) + 0.5


def _silu(x):
    return x * _sigmoid(x)


def _dot(a, b):
    return jnp.dot(a, b, preferred_element_type=F32)


def _dot_nt(a, b):
    return lax.dot_general(a, b, (((1,), (1,)), ((), ())), preferred_element_type=F32)


def _dot_tn(a, b):
    return lax.dot_general(a, b, (((0,), (0,)), ((), ())), preferred_element_type=F32)


def _split2(x):
    hi = x.astype(BF16)
    lo = (x - hi.astype(F32)).astype(BF16)
    return hi, lo


def _dot_const(m2, x):
    hi, lo = _split2(x)
    return _dot(m2, jnp.concatenate([hi, lo], axis=0))


def _mod_row(i, tm):
    start = i * tm
    return jnp.where(start < N_PROMPT, 0, 1 + (start - N_PROMPT) // DEC_SEQ)


def _mod_slice(mod_ref, row, k):
    return mod_ref[pl.ds(row, 1), k * D_MODEL:(k + 1) * D_MODEL]


def _norm_mod(x, g, sc, sh):
    ms = jnp.mean(x * x, axis=-1, keepdims=True)
    return (x * lax.rsqrt(ms + EPS) * g) * (1.0 + sc) + sh


def _mod_kernel(cond_ref, w_ref, b_ref, side_ref, o_ref, side_o_ref):
    s = _silu(cond_ref[...]).astype(BF16)
    o_ref[0] = _dot(s, w_ref[0].astype(BF16)) + b_ref[0]
    side_o_ref[...] = side_ref[...].astype(BF16)


MOD_TN = 768


def _modulation(cond8, ada_w, ada_b, side):
    depth = ada_w.shape[0]
    nj = ada_w.shape[2] // MOD_TN
    slab = pl.cdiv(side.shape[1], depth * nj * BF16_ROWS) * BF16_ROWS
    return pl.pallas_call(
        _mod_kernel,
        grid=(depth, nj),
        in_specs=[
            pl.BlockSpec((N_MOD_ROWS, D_MODEL), lambda l, j: (0, 0)),
            pl.BlockSpec((1, D_MODEL, MOD_TN), lambda l, j: (l, 0, j)),
            pl.BlockSpec((1, 1, MOD_TN), lambda l, j: (l, 0, j)),
            pl.BlockSpec((None, slab, side.shape[2]), lambda l, j: (0, l * nj + j, 0)),
        ],
        out_specs=[pl.BlockSpec((1, N_MOD_ROWS, MOD_TN), lambda l, j: (l, 0, j)),
                   pl.BlockSpec((slab, side.shape[2]), lambda l, j: (l * nj + j, 0))],
        out_shape=[jax.ShapeDtypeStruct((depth, N_MOD_ROWS, ada_w.shape[2]), F32),
                   jax.ShapeDtypeStruct(side.shape[1:], BF16)],
        compiler_params=_cparams("arbitrary", "arbitrary"),
        name="modulation",
    )(cond8, ada_w, ada_b.reshape(depth, 1, -1), side)


def _stream_specs(n_arrays, tm, width=D_MODEL):
    if n_arrays == 1:
        return [pl.BlockSpec((tm, width), lambda i: (i, 0))]
    npt = N_PROMPT // tm
    return [pl.BlockSpec((tm, width), lambda i: (jnp.minimum(i, npt - 1), 0)),
            pl.BlockSpec((tm, width), lambda i: (jnp.maximum(i - npt, 0), 0))]


def _stream_load(x_refs, tm):
    if len(x_refs) == 1:
        return x_refs[0][...]
    return jnp.where(pl.program_id(0) < N_PROMPT // tm, x_refs[0][...], x_refs[1][...])


def _side_specs(side, n_steps):
    in_specs = [pl.BlockSpec((None, w.shape[1] // n_steps, w.shape[2]), lambda i, l=l: (l, i, 0)) for w, l in side]
    out_specs = [pl.BlockSpec((w.shape[1] // n_steps, w.shape[2]), lambda i: (i, 0)) for w, _ in side]
    shapes = [jax.ShapeDtypeStruct(w.shape[1:], BF16) for w, _ in side]
    return in_specs, out_specs, shapes


def _side_cast(in_refs, out_refs):
    for i_ref, o_ref in zip(in_refs, out_refs):
        o_ref[...] = i_ref[...].astype(BF16)


def _norm_proj_kernel(*refs, tm, n_x, n_w, n_side, w_transposed):
    x_refs, (mod_ref, g_ref) = refs[:n_x], refs[n_x:n_x + 2]
    w_refs = refs[n_x + 2:n_x + 2 + n_w]
    side_in = refs[n_x + 2 + n_w:n_x + 2 + n_w + n_side]
    o_refs = refs[n_x + 2 + n_w + n_side:n_x + 2 + 2 * n_w + n_side]
    side_out = refs[n_x + 2 + 2 * n_w + n_side:]
    row = _mod_row(pl.program_id(0), tm)
    h = _norm_mod(_stream_load(x_refs, tm), g_ref[...], _mod_slice(mod_ref, row, 1), _mod_slice(mod_ref, row, 0)).astype(BF16)
    for w_ref, o_ref in zip(w_refs, o_refs):
        o_ref[...] = _dot_nt(h, w_ref[...]) if w_transposed else _dot(h, w_ref[...])
    _side_cast(side_in, side_out)


def _norm_proj(xs, mod, g, ws, widths=None, side=(), w_transposed=False, tm=512):
    n_w = len(ws)
    widths = widths or [w.shape[0 if w_transposed else 1] for w in ws]
    const = lambda i: (0, 0)
    w_block = (lambda n: (n, D_MODEL)) if w_transposed else (lambda n: (D_MODEL, n))
    side_in_specs, side_out_specs, side_shapes = _side_specs(side, N_TOK // tm)
    return pl.pallas_call(
        functools.partial(_norm_proj_kernel, tm=tm, n_x=len(xs), n_w=n_w, n_side=len(side), w_transposed=w_transposed),
        grid=(N_TOK // tm,),
        in_specs=_stream_specs(len(xs), tm) + [
            pl.BlockSpec(mod.shape, const),
            pl.BlockSpec((1, D_MODEL), const),
        ] + [pl.BlockSpec(w_block(n), const, pipeline_mode=pl.Buffered(1)) for n in widths] + side_in_specs,
        out_specs=[pl.BlockSpec((tm, n), lambda i: (i, 0)) for n in widths] + side_out_specs,
        out_shape=[jax.ShapeDtypeStruct((N_TOK, n), F32) for n in widths] + side_shapes,
        compiler_params=_cparams("arbitrary"),
        name="norm_proj",
    )(*xs, mod, g.reshape(1, D_MODEL), *ws, *[w for w, _ in side])


def _post_kernel(*refs, tm, ff_chunk, n_x, n_y, groups, n_side):
    x_refs = refs[:n_x]
    n_m = sum(n for n, _ in groups)
    m_refs = refs[n_x:n_x + n_m]
    mod_ref, g_ref, wo_ref, w1_ref, w2_ref = refs[n_x + n_m:n_x + n_m + 5]
    side_in = refs[n_x + n_m + 5:n_x + n_m + 5 + n_side]
    y_refs = refs[n_x + n_m + 5 + n_side:n_x + n_m + 5 + n_side + n_y]
    _side_cast(side_in, refs[n_x + n_m + 5 + n_side + n_y:])
    row = _mod_row(pl.program_id(0), tm)
    mix, first_ref, first_col = None, 0, 0
    for n, width in groups:
        part = _stream_load(m_refs[first_ref:first_ref + n], tm).astype(BF16)
        term = _dot(part, wo_ref[first_col:first_col + width, :])
        mix = term if mix is None else mix + term
        first_ref, first_col = first_ref + n, first_col + width
    x1 = _stream_load(x_refs, tm) + _mod_slice(mod_ref, row, 2) * mix
    h = _norm_mod(x1, g_ref[...], _mod_slice(mod_ref, row, 4), _mod_slice(mod_ref, row, 3)).astype(BF16)
    acc = jnp.zeros((tm, D_MODEL), F32)
    for k in range(0, D_FF, ff_chunk):
        a = jnp.maximum(_dot(h, w1_ref[:, k:k + ff_chunk]), 0.0)
        acc = acc + _dot((a * a).astype(BF16), w2_ref[k:k + ff_chunk, :])
    y = x1 + _mod_slice(mod_ref, row, 5) * acc
    if n_y == 1:
        y_refs[0][...] = y
    else:
        is_prompt = pl.program_id(0) < N_PROMPT // tm

        @pl.when(is_prompt)
        def _():
            y_refs[0][...] = y

        @pl.when(jnp.logical_not(is_prompt))
        def _():
            y_refs[1][...] = y


def _post_mixer(xs, mixed, mod, g, wo, w1, w2, split_out=False, side=(), tm=512, ff_chunk=1024):
    const = lambda i: (0, 0)
    resident = lambda w: pl.BlockSpec(w.shape, const, pipeline_mode=pl.Buffered(1))
    n_y = 2 if split_out else 1
    rows = (N_PROMPT, N_SAMPLE) if split_out else (N_TOK,)
    groups = tuple((len(grp), grp[0].shape[1]) for grp in mixed)
    mixed_specs = [spec for n, width in groups for spec in _stream_specs(n, tm, width)]
    side_in_specs, side_out_specs, side_shapes = _side_specs(side, N_TOK // tm)
    out = pl.pallas_call(
        functools.partial(_post_kernel, tm=tm, ff_chunk=ff_chunk, n_x=len(xs), n_y=n_y, groups=groups,
                          n_side=len(side)),
        grid=(N_TOK // tm,),
        in_specs=_stream_specs(len(xs), tm) + mixed_specs + [
            pl.BlockSpec(mod.shape, const),
            pl.BlockSpec((1, D_MODEL), const),
            resident(wo), resident(w1), resident(w2),
        ] + side_in_specs,
        out_specs=_stream_specs(n_y, tm) + side_out_specs,
        out_shape=[jax.ShapeDtypeStruct((r, D_MODEL), F32) for r in rows] + side_shapes,
        compiler_params=_cparams("arbitrary"),
        name="post_mixer",
    )(*xs, *[a for grp in mixed for a in grp], mod, g.reshape(1, D_MODEL), wo, w1, w2, *[w for w, _ in side])
    return tuple(out[:n_y]), tuple(out[n_y:])


PAIR = 2 * HD_C


def _pair_consts():
    lane = lax.broadcasted_iota(jnp.int32, (1, PAIR), 1)
    first = lane < HD_C
    ones_col = [jnp.where(lane == HD_C, 1.0, 0.0), jnp.where(lane == 0, 1.0, 0.0)]
    r = lax.broadcasted_iota(jnp.int32, (2 * PAIR, PAIR), 0) % PAIR
    cidx = lax.broadcasted_iota(jnp.int32, (2 * PAIR, PAIR), 1)
    mean2 = jnp.where(r // HD_C == cidx // HD_C, 1.0 / HD_C, 0.0).astype(BF16)
    return first, ones_col, mean2


def _pair_norm(x, w2, mean2):
    hi, lo = _split2(x * x)
    ms = _dot(jnp.concatenate([hi, lo], axis=1), mean2)
    return x * lax.rsqrt(ms + EPS) * w2


def _pair_queries(q, first):
    return [jnp.where(first, q, 0.0).astype(BF16), jnp.where(first, 0.0, q).astype(BF16)]


def _pair_values(v, first, ones_col):
    return [jnp.where(first, v, ones_col[0]).astype(BF16), jnp.where(first, ones_col[1], v).astype(BF16)]


def _pair_output(o_aug, first):
    den = [o_aug[0][:, HD_C:HD_C + 1], o_aug[1][:, 0:1]]
    return jnp.where(first, o_aug[0] / den[0], o_aug[1] / den[1])


def _row_max(*pieces):
    tiles = [p[:, i:i + LANES] for p in pieces for i in range(0, p.shape[1], LANES)]
    return jnp.max(functools.reduce(jnp.maximum, tiles), axis=-1, keepdims=True)


CTX_PAIRS = 8


def _ctx_attn_kernel(q_ref, k_ref, v_ref, qn_ref, kn_ref, o_ref, kc_ref, vc_ref):
    first, ones_col, mean2 = _pair_consts()
    lanes = [slice(p * PAIR, (p + 1) * PAIR) for p in range(CTX_PAIRS)]
    qn = [_pair_norm(q_ref[:, ln], qn_ref[...], mean2) * HD_C ** -0.5 for ln in lanes]
    kn = [_pair_norm(k_ref[:, ln], kn_ref[...], mean2) for ln in lanes]
    v = [v_ref[:, ln] for ln in lanes]
    kt = [x.T for x in kn]
    for p in range(CTX_PAIRS):
        kc_ref[0, 0, lanes[p], :] = kt[p]
        vc_ref[0, 0, lanes[p], :] = v[p].T
    q = [_pair_queries(x, first) for x in qn]
    va = [_pair_values(x, first, ones_col) for x in v]
    s = [[_dot(q[p][j], kt[p].astype(BF16)) for j in range(2)] for p in range(CTX_PAIRS)]
    pr = [[jnp.exp(x - _row_max(x)).astype(BF16) for x in sp] for sp in s]
    for p in range(CTX_PAIRS):
        o_ref[:, lanes[p]] = _pair_output([_dot(pr[p][j], va[p][j]) for j in range(2)], first)


def _ctx_attention(qkv, qn, kn):
    heads = 2 * CTX_PAIRS
    ng = H_C // heads
    wide = CTX_PAIRS * PAIR
    blk = lambda off: pl.BlockSpec((SEQ, wide), lambda b, p: (b, off + p))
    cache_spec = pl.BlockSpec((1, 1, wide, SEQ), lambda b, p: (b, 0, p, 0))
    cache_shape = jax.ShapeDtypeStruct((BATCH, 1, H_C * HD_C, SEQ), F32)
    return pl.pallas_call(
        _ctx_attn_kernel,
        grid=(BATCH, ng),
        in_specs=[blk(0), blk(ng), blk(2 * ng),
                  pl.BlockSpec((1, PAIR), lambda b, p: (0, 0)),
                  pl.BlockSpec((1, PAIR), lambda b, p: (0, 0))],
        out_specs=[pl.BlockSpec((SEQ, wide), lambda b, p: (b, p)), cache_spec, cache_spec],
        out_shape=[jax.ShapeDtypeStruct((N_PROMPT, D_MODEL), F32), cache_shape, cache_shape],
        compiler_params=_cparams("arbitrary", "arbitrary"),
        name="ctx_attention",
    )(qkv, qkv, qkv, jnp.tile(qn.reshape(1, HD_C), (1, 2)), jnp.tile(kn.reshape(1, HD_C), (1, 2)))


def _na_row_start(r):
    return min(max(r - KH // 2, 0), GRID_ROWS - KH)


NA_ROW_GROUP = 4


def _na_attn_kernel(q_ref, k_ref, v_ref, kc_ref, vc_ref, qn_ref, kn_ref, bias_ref, o_ref, qs, ks, vs, bias_s):
    first, ones_col, mean2 = _pair_consts()

    @pl.when(pl.program_id(1) == 0)
    def _():
        q_col = lax.broadcasted_iota(jnp.int32, (GRID_W, PAIR), 0)
        lane = lax.broadcasted_iota(jnp.int32, (GRID_W, PAIR), 1)
        k_col = lane % GRID_W
        w0 = jnp.clip(q_col - KW // 2, 0, GRID_W - KW)
        outside = jnp.where((k_col >= w0) & (k_col < w0 + KW), 0.0, NEG_INF)
        n_dr = 2 * KH - 1
        for j in range(2):
            band = []
            for dr in range(n_dr):
                row = jnp.broadcast_to(bias_ref[j, dr:dr + 1, :], (GRID_W, PAIR))
                band.append([pltpu.roll(row, (half * GRID_W - (KW - 1)) % PAIR, axis=1, stride=1, stride_axis=0)
                             for half in range(2)])
            zero = jnp.zeros((GRID_W, PAIR), F32)
            for cp in range(2):
                for t in range(KH):
                    lo, hi = 2 * t + cp, 2 * t + cp + 1
                    tile = jnp.where(lane < GRID_W, band[lo][0] if lo < n_dr else zero,
                                     band[hi][1] if hi < n_dr else zero)
                    bias_s[j, cp, :, t * PAIR:(t + 1) * PAIR] = tile + outside

    q2 = _pair_queries(_pair_norm(q_ref[...], qn_ref[...], mean2) * HD_C ** -0.5, first)
    v2 = _pair_values(v_ref[...], first, ones_col)
    ks[...] = _pair_norm(k_ref[...], kn_ref[...], mean2).astype(BF16)
    for j in range(2):
        qs[:, j] = q2[j].reshape(GRID_ROWS, GRID_W, PAIR)
        vs[j] = v2[j]
    kt_ctx = kc_ref[0, 0].astype(BF16)
    vt = vc_ref[0, 0]
    ch = lax.broadcasted_iota(jnp.int32, vt.shape, 0)
    vt_ctx = [jnp.where(ch < HD_C, vt, jnp.where(ch == HD_C, 1.0, 0.0)).astype(BF16),
              jnp.where(ch < HD_C, jnp.where(ch == 0, 1.0, 0.0), vt).astype(BF16)]
    for r0 in range(0, GRID_ROWS, NA_ROW_GROUP):
        units = [(r, j) for r in range(r0, r0 + NA_ROW_GROUP) for j in range(2)]
        rows = {r: slice(r * GRID_W, (r + 1) * GRID_W) for r, _ in units}
        wins = {r: slice(_na_row_start(r) * GRID_W, (_na_row_start(r) + KH) * GRID_W) for r, _ in units}
        unit_rows = lambda k: slice(k * GRID_W, (k + 1) * GRID_W)
        s_ctx_all = _dot(qs[r0:r0 + NA_ROW_GROUP].reshape(NA_ROW_GROUP * PAIR, PAIR), kt_ctx)
        s_ctx = [s_ctx_all[unit_rows(2 * (r - r0) + j)] for r, j in units]
        s_row = {r: _dot_nt(qs[r].reshape(PAIR, PAIR), ks[wins[r], :]) for r in rows}
        s_win = []
        for r, j in units:
            dr0 = KH - 1 - (r - _na_row_start(r))
            lane0 = (dr0 - dr0 % 2) * GRID_W
            s_win.append(s_row[r][unit_rows(j)] + bias_s[j, dr0 % 2, :, lane0:lane0 + KH * GRID_W])
        m = [_row_max(a, b) for a, b in zip(s_win, s_ctx)]
        p_win = [jnp.exp(a - mm).astype(BF16) for a, mm in zip(s_win, m)]
        p_ctx = [jnp.exp(b - mm).astype(BF16) for b, mm in zip(s_ctx, m)]
        o_aug = [_dot(p_win[i], vs[j, wins[r], :]) + _dot_nt(p_ctx[i], vt_ctx[j]) for i, (r, j) in enumerate(units)]
        for i in range(0, len(units), 2):
            o_ref[rows[units[i][0]], :] = _pair_output(o_aug[i:i + 2], first)


NA_BIAS_LANES = 2 * KH * GRID_W


def _na_attention(qkv, cache_kt, cache_vt, qn, kn, rpb):
    nhp = H_C // 2
    row0 = N_PROMPT // DEC_SEQ
    blk = lambda off: pl.BlockSpec((DEC_SEQ, 2 * HD_C), lambda p, b: (row0 + b, off + p))
    cache_spec = pl.BlockSpec((1, 1, PAIR, PAST_LEN), lambda p, b: (b, 0, p, 0))
    rpb_rows = 2 * KH
    bias = jnp.pad(rpb.astype(F32), ((0, 0), (0, rpb_rows - rpb.shape[1]), (0, PAIR - rpb.shape[2])))
    return pl.pallas_call(
        _na_attn_kernel,
        grid=(nhp, DEC_BATCH),
        in_specs=[blk(0), blk(nhp), blk(2 * nhp), cache_spec, cache_spec,
                  pl.BlockSpec((1, PAIR), lambda p, b: (0, 0)),
                  pl.BlockSpec((1, PAIR), lambda p, b: (0, 0)),
                  pl.BlockSpec((2, rpb_rows, PAIR), lambda p, b: (p, 0, 0))],
        out_specs=pl.BlockSpec((DEC_SEQ, 2 * HD_C), lambda p, b: (b, p)),
        out_shape=jax.ShapeDtypeStruct((N_SAMPLE, D_MODEL), F32),
        scratch_shapes=[pltpu.VMEM((GRID_ROWS, 2, GRID_W, PAIR), BF16), pltpu.VMEM((DEC_SEQ, PAIR), BF16),
                        pltpu.VMEM((2, DEC_SEQ, PAIR), BF16), pltpu.VMEM((2, 2, GRID_W, NA_BIAS_LANES), F32)],
        compiler_params=_cparams("arbitrary", "arbitrary"),
        name="na_attention",
    )(qkv, qkv, qkv, cache_kt, cache_vt, jnp.tile(qn.reshape(1, HD_C), (1, 2)), jnp.tile(kn.reshape(1, HD_C), (1, 2)),
      bias)


def _seq_layout(prompt):
    return (SEQ, BATCH, 0) if prompt else (DEC_SEQ, DEC_BATCH, N_PROMPT // DEC_SEQ)


def _flip_blocks(m, c):
    r, s = m.shape
    return m.reshape(r // c, c, s // c, c)[:, ::-1, :, ::-1].reshape(r, s)


def _rms_gate(x, gn, gate):
    ms = jnp.mean(x * x, axis=-1, keepdims=True)
    return x * lax.rsqrt(ms + EPS) * gn * _silu(gate)


HG_LEVELS = tuple(CHUNK >> (i + 1) for i in range(CHUNK.bit_length() - 1))
HG_NL = len(HG_LEVELS)
HG_STACK = (HG_NL + 1) * CHUNK
TOT_ROWS = BF16_ROWS
HG_ROWS = (HG_NL + 2) * CHUNK + TOT_ROWS


def _hgrn_consts():
    c = CHUNK
    level_rows = []
    mask = np.zeros((HG_STACK, HG_STACK), np.float32)
    mask[:c, :c] = np.eye(c)
    for li, b in enumerate(HG_LEVELS):
        m = np.zeros((c, c), np.float32)
        blk = np.zeros((c, c), np.float32)
        for t in range(c):
            mid = (t // (2 * b)) * 2 * b + b
            if t >= mid:
                m[t, mid:t + 1] = 1.0
                blk[t, mid - b:mid] = 1.0
            else:
                m[t, t + 1:mid] = 1.0
        level_rows.append(m)
        mask[(li + 1) * c:(li + 2) * c, (li + 1) * c:(li + 2) * c] = blk
    dq = np.tril(np.ones((c, c), np.float32))
    dk = np.triu(np.ones((c, c), np.float32), 1)
    body = np.concatenate(level_rows + [dq, dk], axis=0)
    tot = np.ones((TOT_ROWS, c), np.float32)
    mcs, masks = [], []
    for reverse in (False, True):
        bm = _flip_blocks(body, c) if reverse else body
        mk = _flip_blocks(mask, c) if reverse else mask
        mc = np.concatenate([bm, tot], axis=0)
        mcs.append(np.concatenate([mc, mc], axis=1))
        masks.append(mk)
    return jnp.asarray(np.stack(mcs), BF16), jnp.asarray(np.stack(masks), F32)


HG_FAST = 64
HG_HALF = HG_FAST // 2
HG_FAST_ROWS = 4 * HG_FAST + TOT_ROWS
HG_SAFE_EXP = 40.0
HG_FAST_STEPS = 4


def _hgrn_fast_consts(seq):
    c, m = HG_FAST, HG_HALF
    aq = np.zeros((c, c), np.float32)
    for t in range(c):
        if t >= m:
            aq[t, m:t + 1] = 1.0
        else:
            aq[t, t + 1:m] = -1.0
    dq = np.tril(np.ones((c, c), np.float32))
    dk = np.triu(np.ones((c, c), np.float32), 1)
    body = np.concatenate([aq, -aq, dq, dk], axis=0)
    tot = np.ones((TOT_ROWS, c), np.float32)
    causal = np.tril(np.ones((c, c), np.float32))
    mfs, masks = [], []
    for reverse in (False, True):
        bm = _flip_blocks(body, c) if reverse else body
        mf = np.concatenate([bm, tot], axis=0)
        mfs.append(np.concatenate([mf, mf], axis=1))
        masks.append(causal.T if reverse else causal)
    n_half = seq // m
    half = np.zeros((max(n_half, BF16_ROWS), seq), np.float32)
    for i in range(n_half):
        half[i, i * m:(i + 1) * m] = 1.0
    return jnp.asarray(np.stack(mfs), BF16), jnp.asarray(np.stack(masks), F32), jnp.asarray(half, BF16)


def _hgrn_kernel(*refs, seq, has_s0, emit_state):
    it = iter(refs)
    qa_ref, ff_ref, fb_ref, ia_ref, ga_ref, lb_ref, gn_ref, mc_ref, mask_ref = [next(it) for _ in range(9)]
    mf_ref, causal_ref, half_ref = [next(it) for _ in range(3)]
    s0_ref = next(it) if has_s0 else None
    o_ref = next(it)
    st_ref = next(it) if emit_state else None
    s_scr, acc, f_s, lf_s = [next(it) for _ in range(4)]
    c = CHUNK
    n_chunks = seq // c
    combos = [(d, h) for d in range(2) for h in range(H_A)]
    lanes = [slice(h * DK_A, (h + 1) * DK_A) for h in range(H_A)]
    add = lambda a, b: a + b

    lb_raw = lb_ref[...]
    lb_e = jnp.exp(lb_raw - jnp.max(lb_raw, axis=0, keepdims=True))
    lb_all = lb_e[0:1] / jnp.sum(lb_e, axis=0, keepdims=True)

    for d in range(2):
        for h in range(H_A):
            s_scr[d, h] = s0_ref[0, 0, d, h].T if has_s0 else jnp.zeros((DV_A, DK_A), F32)
    acc[...] = jnp.zeros(acc.shape, F32)

    worst = []
    for d, fr_ref in enumerate((ff_ref, fb_ref)):
        f = lb_all + (1.0 - lb_all) * _sigmoid(fr_ref[...])
        lf = jnp.log(f)
        f_s[d] = f
        lf_s[d] = lf
        worst.append(jnp.max(_dot(half_ref[...], (-lf).astype(BF16))))
    safe = jnp.maximum(worst[0], worst[1]) <= HG_SAFE_EXP

    def fast_body(n, carry):
        cf = HG_FAST
        n_fast = seq // cf
        steps = range(HG_FAST_STEPS)
        chunk = lambda d, t: (n * HG_FAST_STEPS + t) if d == 0 else (n_fast - 1 - n * HG_FAST_STEPS - t)
        rows = [[pl.ds(pl.multiple_of(chunk(d, t) * cf, cf), cf) for t in steps] for d in range(2)]
        units = [(t, d, h) for t in steps for d, h in combos]
        e_all = [[jnp.exp(_dot_const(mf_ref[d], lf_s[d, rows[d][t], :])) for t in steps] for d in range(2)]
        q_all = [[_silu(qa_ref[rows[d][t], :]) * DK_A ** -0.5 for t in steps] for d in range(2)]
        k_all = [[1.0 - f_s[d, rows[d][t], :] for t in steps] for d in range(2)]
        v_all = [[ia_ref[rows[d][t], :].astype(BF16) for t in steps] for d in range(2)]
        qs = {u: q_all[u[1]][u[0]][:, lanes[u[2]]] for u in units}
        ks = {u: k_all[u[1]][u[0]][:, lanes[u[2]]] for u in units}
        vs = {u: v_all[u[1]][u[0]][:, lanes[u[2]]] for u in units}
        es = {u: [e_all[u[1]][u[0]][i * cf:(i + 1) * cf, lanes[u[2]]] for i in range(4)] for u in units}
        p = {u: jnp.where(causal_ref[u[1]] > 0.0,
                          _dot_nt((qs[u] * es[u][0]).astype(BF16), (ks[u] * es[u][1]).astype(BF16)), 0.0).astype(BF16)
             for u in units}
        intra = {u: _dot(p[u], vs[u]) for u in units}
        upd = {u: _dot_tn(vs[u], (ks[u] * es[u][3]).astype(BF16)) for u in units}
        qdec = {u: (qs[u] * es[u][2]).astype(BF16) for u in units}
        st = {(d, h): s_scr[d, h] for d, h in combos}
        o = {}
        for t in steps:
            for d, h in combos:
                u = (t, d, h)
                o[u] = intra[u] + _dot_nt(qdec[u], st[d, h].astype(BF16))
                st[d, h] = st[d, h] * e_all[d][t][4 * cf:4 * cf + 1, lanes[h]] + upd[u]
        for t in steps:
            for d in range(2):
                acc[rows[d][t], :] += jnp.concatenate([o[t, d, h] for h in range(H_A)], axis=1)
        for d, h in combos:
            s_scr[d, h] = st[d, h]
        return carry

    def body(n, carry):
        rows = [pl.ds(pl.multiple_of((n if d == 0 else n_chunks - 1 - n) * c, c), c) for d in range(2)]
        f_all = [f_s[d, rows[d], :] for d in range(2)]
        e_all = [jnp.exp(_dot_const(mc_ref[d], lf_s[d, rows[d], :])) for d in range(2)]
        q_all = [_silu(qa_ref[rows[d], :]) * DK_A ** -0.5 for d in range(2)]
        v_all = [ia_ref[rows[d], :].astype(BF16) for d in range(2)]
        st = [s_scr[d, h] for d, h in combos]
        qs, ks, vs, es = [], [], [], []
        for d, h in combos:
            qs.append(q_all[d][:, lanes[h]])
            ks.append(1.0 - f_all[d][:, lanes[h]])
            vs.append(v_all[d][:, lanes[h]])
            es.append(e_all[d][:, lanes[h]])
        lvl = [[e[i * c:(i + 1) * c] for i in range(HG_NL + 2)] for e in es]
        qst = [jnp.concatenate([q] + [q * l[i] for i in range(HG_NL)], axis=0).astype(BF16) for q, l in zip(qs, lvl)]
        kst = [jnp.concatenate([k] + [k * l[i] for i in range(HG_NL)], axis=0).astype(BF16) for k, l in zip(ks, lvl)]
        r = [(_dot_nt(qst[i], kst[i]) * mask_ref[d]).astype(BF16) for i, (d, h) in enumerate(combos)]
        ost = [_dot(r[i], jnp.concatenate([vs[i]] * (HG_NL + 1), axis=0)) for i in range(len(combos))]
        inter = [_dot_nt((qs[i] * lvl[i][HG_NL]).astype(BF16), st[i].astype(BF16)) for i in range(len(combos))]
        upd = [_dot_tn(vs[i], (ks[i] * lvl[i][HG_NL + 1]).astype(BF16)) for i in range(len(combos))]
        o = [functools.reduce(lambda a, b: a + b, [ost[i][j * c:(j + 1) * c] for j in range(HG_NL + 1)]) + inter[i]
             for i in range(len(combos))]
        for d in range(2):
            acc[rows[d], :] += jnp.concatenate(o[d * H_A:(d + 1) * H_A], axis=1)
        for i, (d, h) in enumerate(combos):
            e_tot = es[i][(HG_NL + 2) * c:(HG_NL + 2) * c + 1]
            s_scr[d, h] = st[i] * e_tot + upd[i]
        return carry

    @pl.when(safe)
    def _():
        lax.fori_loop(0, seq // (HG_FAST * HG_FAST_STEPS), fast_body, 0)

    @pl.when(jnp.logical_not(safe))
    def _():
        lax.fori_loop(0, n_chunks, body, 0)

    for h in range(H_A):
        ln = slice(h * DV_A, (h + 1) * DV_A)
        o_ref[:, ln] = _rms_gate(acc[:, ln], gn_ref[...], ga_ref[:, ln])
    if emit_state:
        for d in range(2):
            for h in range(H_A):
                st_ref[0, 0, d, h] = s_scr[d, h].T


def _hgrn(proj, hgrn_lb, gn, consts, prompt, s0=None, layer=0):
    seq, nb, rb0 = _seq_layout(prompt)
    consts = list(consts) + list(_hgrn_fast_consts(seq))
    wa = H_A * DK_A
    blk = lambda j: pl.BlockSpec((seq, wa), lambda b: (rb0 + b, j))
    const2 = lambda b: (0, 0)
    st_block = (1, 1, 2, H_A, DK_A, DV_A)
    in_specs = [blk(0), blk(1), blk(2), blk(3), blk(4),
                pl.BlockSpec(hgrn_lb.shape, const2), pl.BlockSpec((1, DV_A), const2)]
    in_specs += [pl.BlockSpec(m.shape, lambda b, nd=m.ndim: (0,) * nd) for m in consts]
    args = [proj] * 5 + [hgrn_lb, gn.reshape(1, DV_A)] + consts
    if s0 is not None:
        in_specs.append(pl.BlockSpec(st_block, lambda b: (b, layer, 0, 0, 0, 0)))
        args.append(s0)
    out_specs = [pl.BlockSpec((seq, wa), lambda b: (b, 0))]
    out_shape = [jax.ShapeDtypeStruct((nb * seq, wa), F32)]
    if prompt:
        out_specs.append(pl.BlockSpec(st_block, lambda b: (b, 0, 0, 0, 0, 0)))
        out_shape.append(jax.ShapeDtypeStruct((nb, 1, 2, H_A, DK_A, DV_A), F32))
    return pl.pallas_call(
        functools.partial(_hgrn_kernel, seq=seq, has_s0=s0 is not None, emit_state=prompt),
        grid=(nb,),
        in_specs=in_specs,
        out_specs=out_specs,
        out_shape=out_shape,
        scratch_shapes=[pltpu.VMEM((2, H_A, DV_A, DK_A), F32), pltpu.VMEM((seq, wa), F32),
                        pltpu.VMEM((2, seq, wa), F32), pltpu.VMEM((2, seq, wa), F32)],
        compiler_params=_cparams("arbitrary"),
        name="hgrn_prompt" if prompt else "hgrn_sample",
    )(*args)


GD_SUB = GBLK // CHUNK
GD_BLOCKS_PER_ITER = 2
GD_PROMPT_SEQS = 2
GD_ROWS = 2 * GBLK + TOT_ROWS


def _gdn_consts():
    n, c = GBLK, CHUNK
    same = (np.arange(n)[:, None] // c) == (np.arange(n)[None, :] // c)
    tri = (same & (np.arange(n)[None, :] <= np.arange(n)[:, None])).astype(np.float32)
    sup = (same & (np.arange(n)[None, :] > np.arange(n)[:, None])).astype(np.float32)
    tot = np.zeros((TOT_ROWS, n), np.float32)
    for s in range(GD_SUB):
        tot[s, s * c:(s + 1) * c] = 1.0
    mgs, tts, tris = [], [], []
    for reverse in (False, True):
        t = _flip_blocks(tri, c) if reverse else tri
        s = _flip_blocks(sup, c) if reverse else sup
        mg = np.concatenate([t, s, tot], axis=0)
        mgs.append(np.concatenate([mg, mg], axis=1))
        tts.append(np.concatenate([t.T, t.T], axis=0))
        tris.append(t)
    tris.append(same.astype(np.float32))
    return jnp.asarray(np.stack(mgs), BF16), jnp.asarray(np.stack(tts), BF16), jnp.asarray(np.stack(tris), F32)


def _softplus(x):
    return jnp.maximum(x, 0.0) + jnp.log(1.0 + jnp.exp(-jnp.abs(x)))


CONV_PAD = SUBLANES


def _conv_silu(x, w, seq):
    half = SHORT_CONV // 2
    pad = jnp.zeros((CONV_PAD, x.shape[1]), x.dtype)
    xe = jnp.concatenate([pad, x, pad], axis=0)
    acc = xe * w[half:half + 1]
    for j in range(SHORT_CONV):
        shift = half - j
        if shift != 0:
            acc = acc + pltpu.roll(xe, shift % (seq + 2 * CONV_PAD), axis=0) * w[j:j + 1]
    return _silu(acc[CONV_PAD:seq + CONV_PAD])


def _l2norm_heads(x, n_heads, width, scale):
    outs = []
    for h in range(n_heads):
        xh = x[:, h * width:(h + 1) * width]
        outs.append(xh * (lax.rsqrt(jnp.sum(xh * xh, axis=-1, keepdims=True) + EPS) * scale))
    return jnp.concatenate(outs, axis=-1)


def _gdn_kernel(*refs, seq, n_seq, has_s0, emit_state):
    it = iter(refs)
    (q_ref, k_ref, v_ref, gb_ref, gate_ref, cw_ref, alog_ref, dt_ref, gn_ref,
     mg_ref, tt_ref, tri_ref) = [next(it) for _ in range(12)]
    s0_ref = next(it) if has_s0 else None
    o_ref = next(it)
    st_ref = next(it) if emit_state else None
    qn, kn, vn, u_s, w_s, qg_s, kdt_s, at_s, et_s, s_scr, acc = [next(it) for _ in range(11)]
    c = CHUNK
    n_chunks = seq // c
    n_blocks = n_seq * seq // GBLK
    wq = H_B * DK_B
    n_dh = 2 * H_B
    combos = [(d, h) for d in range(2) for h in range(H_B)]
    lanes = [slice(h * DK_B, (h + 1) * DK_B) for h in range(H_B)]

    for s in range(n_seq):
        sr = slice(s * seq, (s + 1) * seq)
        qn[sr, :] = _l2norm_heads(_conv_silu(q_ref[sr, :], cw_ref[:, 0:wq], seq), H_B, DK_B, DK_B ** -0.5)
        kn[sr, :] = _l2norm_heads(_conv_silu(k_ref[sr, :], cw_ref[:, wq:2 * wq], seq), H_B, DK_B, 1.0)
        vn[sr, :] = _conv_silu(v_ref[sr, :], cw_ref[:, 2 * wq:3 * wq], seq)
        for i in range(n_dh):
            s_scr[s * n_dh + i] = (s0_ref[s, 0, i // H_B, i % H_B] if has_s0 else jnp.zeros((DK_B, DV_B), F32))
    acc[...] = jnp.zeros(acc.shape, F32)

    eye = (lax.broadcasted_iota(jnp.int32, (GBLK, GBLK), 0)
           == lax.broadcasted_iota(jnp.int32, (GBLK, GBLK), 1)).astype(F32)
    eye_pk = (lax.broadcasted_iota(jnp.int32, (c, GBLK), 0)
              == lax.broadcasted_iota(jnp.int32, (c, GBLK), 1) % c).astype(F32)
    bwd_lane = lax.broadcasted_iota(jnp.int32, (1, LANES), 1) % n_dh >= H_B
    add = lambda a, b: a + b

    same_chunk = tri_ref[2].astype(BF16)

    def expand(pk):
        return jnp.concatenate([pk] * GD_SUB, axis=0) * same_chunk

    def pack(bd):
        return functools.reduce(add, [bd[s * c:(s + 1) * c] for s in range(GD_SUB)])

    def weights(hi, lo):
        return jnp.concatenate([expand(hi), expand(lo)], axis=1)

    def dot3_split(a_hi, a_lo, w2):
        m, n = a_hi.shape[0], w2.shape[1] // 2
        t = _dot(jnp.concatenate([a_hi, a_lo], axis=0), w2)
        return t[:m, :n] + t[m:, :n] + t[:m, n:]

    def block_body(it, carry):
        blks = [it * GD_BLOCKS_PER_ITER + o for o in range(GD_BLOCKS_PER_ITER)]
        units = [(o, d, h) for o in range(GD_BLOCKS_PER_ITER) for d, h in combos]
        idx = lambda d, h: d * H_B + h
        col = lambda x, j: jnp.broadcast_to(x[:, j:j + 1], (GBLK, DK_B))
        rows = [pl.ds(pl.multiple_of(b * GBLK, GBLK), GBLK) for b in blks]
        gates = [gate_ref[r, :] for r in rows]
        glog_all = [-jnp.exp(alog_ref[...]) * _softplus(x + dt_ref[...]) for x in gates]
        beta_all = [_sigmoid(x) for x in gates]
        g2 = [jnp.concatenate(_split2(x), axis=0) for x in glog_all]
        dg = [[_dot(mg_ref[d], x) for d in range(2)] for x in g2]
        dsel = [jnp.where(bwd_lane, x[1], x[0]) for x in dg]
        eg_all = [jnp.exp(x) for x in dsel]
        gt = [[_dot_tn(x, tt_ref[d]) for d in range(2)] for x in g2]
        qs = [[qn[r, ln] for ln in lanes] for r in rows]
        ks = [[kn[r, ln] for ln in lanes] for r in rows]
        vs = [[vn[r, ln] for ln in lanes] for r in rows]
        betas = [col(beta_all[o], n_dh + idx(d, h)) for o, d, h in units]
        kbs = [ks[o][h] * betas[u] for u, (o, d, h) in enumerate(units)]
        kb_of = {unit: kbs[u] for u, unit in enumerate(units)}
        kk = {(o, h): _dot_nt(jnp.concatenate([qs[o][h], kb_of[o, 0, h], kb_of[o, 1, h]], axis=0).astype(BF16),
                              ks[o][h].astype(BF16))
              for o in range(GD_BLOCKS_PER_ITER) for h in range(H_B)}
        decay = []
        for o, d, h in units:
            inside = tri_ref[d] > 0.0
            gd = col(dsel[o][:GBLK], idx(d, h)) - gt[o][d][idx(d, h):idx(d, h) + 1, :]
            decay.append(jnp.where(inside, jnp.exp(jnp.where(inside, gd, 0.0)), 0.0))
        attn = [kk[o, h][:GBLK] * decay[u] for u, (o, d, h) in enumerate(units)]
        p_pk = [pack(kk[o, h][(1 + d) * GBLK:(2 + d) * GBLK] * decay[u] * (1.0 - eye))
                for u, (o, d, h) in enumerate(units)]
        x_pk = [eye_pk - p for p in p_pk]
        p_sp = [_split2(p) for p in p_pk]
        p_w = [weights(*s) for s in p_sp]
        for _ in range(CHUNK.bit_length() - 2):
            p_pk = [dot3_split(*s, w) for s, w in zip(p_sp, p_w)]
            p_sp = [_split2(p) for p in p_pk]
            p_w = [weights(*s) for s in p_sp]
            x_pk = [x + dot3_split(*_split2(x), w) for x, w in zip(x_pk, p_w)]
        eg_col = [col(eg_all[o][:GBLK], idx(d, h)) for o, d, h in units]
        ekd_col = [col(eg_all[o][GBLK:2 * GBLK], idx(d, h)) for o, d, h in units]
        rhs = [_split2(jnp.concatenate([vs[o][h] * betas[u], kbs[u] * eg_col[u]], axis=1))
               for u, (o, d, h) in enumerate(units)]
        t_sp = [[expand(part) for part in _split2(x)] for x in x_pk]
        uw = [_dot(t[0], r[0]) + (_dot(t[0], r[1]) + _dot(t[1], r[0])) for t, r in zip(t_sp, rhs)]
        for u, (o, d, h) in enumerate(units):
            i = idx(d, h)
            qg = (qs[o][h] * eg_col[u]).astype(BF16)
            kdt_s[i, blks[o]] = (ks[o][h] * ekd_col[u]).T.astype(BF16)
            for s in range(GD_SUB):
                cn = blks[o] * GD_SUB + s
                r = slice(s * c, (s + 1) * c)
                u_s[i, cn] = uw[u][r, :DV_B]
                w_s[i, cn] = uw[u][r, DV_B:].astype(BF16)
                qg_s[i, cn] = qg[r]
                at_s[i, cn] = attn[u][r].astype(BF16)
                et_s[i, cn] = jnp.broadcast_to(eg_all[o][2 * GBLK + s:2 * GBLK + s + 1, i:i + 1], (SUBLANES, DV_B))
        return carry

    n_iter = n_blocks // GD_BLOCKS_PER_ITER
    lax.fori_loop(0, n_iter, block_body, 0, unroll=True)

    def chunk_body(n, carry):
        chains = [(s, d, h) for s in range(n_seq) for d, h in combos]
        idx = lambda d, h: d * H_B + h
        cn = {(s, d): s * n_chunks + (n if d == 0 else n_chunks - 1 - n) for s in range(n_seq) for d in range(2)}
        rows = {key: pl.ds(pl.multiple_of(v * c, c), c) for key, v in cn.items()}
        sub_of_row = lax.broadcasted_iota(jnp.int32, (GBLK, 1), 0) // c
        in_chunk = {key: sub_of_row == v % GD_SUB for key, v in cn.items()}
        st = {ch: s_scr[ch[0] * n_dh + idx(ch[1], ch[2])] for ch in chains}
        ws = {(s, d, h): _dot(jnp.concatenate([w_s[idx(d, h), cn[s, d]], qg_s[idx(d, h), cn[s, d]]], axis=0),
                              st[s, d, h].astype(BF16)) for s, d, h in chains}
        vblk = {(s, d, h): jnp.where(in_chunk[s, d],
                                     jnp.concatenate([u_s[idx(d, h), cn[s, d]] - ws[s, d, h][:c]] * GD_SUB, axis=0),
                                     0.0).astype(BF16) for s, d, h in chains}
        r = {(s, d, h): _dot(jnp.concatenate([at_s[idx(d, h), cn[s, d]], kdt_s[idx(d, h), cn[s, d] // GD_SUB]], axis=0),
                             vblk[s, d, h]) for s, d, h in chains}
        for s in range(n_seq):
            for d in range(2):
                acc[rows[s, d], :] += jnp.concatenate([ws[s, d, h][c:] + r[s, d, h][:c] for h in range(H_B)], axis=1)
        for s, d, h in chains:
            s_scr[s * n_dh + idx(d, h)] = st[s, d, h] * et_s[idx(d, h), cn[s, d]][0:1] + r[s, d, h][c:]
        return carry

    lax.fori_loop(0, n_chunks, chunk_body, 0)

    for h in range(H_B):
        ln = slice(h * DV_B, (h + 1) * DV_B)
        o_ref[:, ln] = _rms_gate(acc[:, ln], gn_ref[...], gb_ref[:, ln])
    if emit_state:
        for s in range(n_seq):
            for i in range(n_dh):
                st_ref[s, 0, i // H_B, i % H_B] = s_scr[s * n_dh + i]


def _gdn(proj, gates, conv_w, a_log, dt_bias, gn, consts, prompt, s0=None, layer=0):
    seq, nb, rb0 = _seq_layout(prompt)
    n_seq = GD_PROMPT_SEQS if prompt else 1
    rows = n_seq * seq
    n_chunks = rows // CHUNK
    wq = H_B * DK_B
    blk = lambda j: pl.BlockSpec((rows, wq), lambda b: (rb0 + b, j))
    const2 = lambda b: (0, 0)
    const3 = lambda b: (0, 0, 0)
    st_block = (n_seq, 1, 2, H_B, DK_B, DV_B)
    pad_row = lambda p: jnp.pad(p.reshape(1, -1).astype(F32), ((0, 0), (0, LANES - p.size)))
    in_specs = [blk(5), blk(6), blk(7), blk(8),
                pl.BlockSpec((rows, LANES), lambda b: (rb0 + b, 0)),
                pl.BlockSpec((SHORT_CONV, 3 * wq), const2),
                pl.BlockSpec((1, LANES), const2), pl.BlockSpec((1, LANES), const2), pl.BlockSpec((1, DV_B), const2)]
    in_specs += [pl.BlockSpec(m.shape, const3) for m in consts]
    args = [proj] * 4 + [gates, conv_w.reshape(SHORT_CONV, 3 * wq), pad_row(a_log), pad_row(dt_bias),
                         gn.reshape(1, DV_B)] + list(consts)
    if s0 is not None:
        in_specs.append(pl.BlockSpec(st_block, lambda b: (b, layer, 0, 0, 0, 0)))
        args.append(s0)
    out_specs = [pl.BlockSpec((rows, wq), lambda b: (b, 0))]
    out_shape = [jax.ShapeDtypeStruct((nb * seq, wq), F32)]
    if prompt:
        out_specs.append(pl.BlockSpec(st_block, lambda b: (b, 0, 0, 0, 0, 0)))
        out_shape.append(jax.ShapeDtypeStruct((nb, 1, 2, H_B, DK_B, DV_B), F32))
    n_dh = 2 * H_B
    scratch = ([pltpu.VMEM((rows, wq), F32)] * 3
               + [pltpu.VMEM((n_dh, n_chunks, CHUNK, DV_B), F32)]
               + [pltpu.VMEM((n_dh, n_chunks, CHUNK, DK_B), BF16)] * 2
               + [pltpu.VMEM((n_dh, rows // GBLK, DK_B, GBLK), BF16),
                  pltpu.VMEM((n_dh, n_chunks, CHUNK, GBLK), BF16),
                  pltpu.VMEM((n_dh, n_chunks, SUBLANES, DV_B), F32),
                  pltpu.VMEM((n_seq * n_dh, DK_B, DV_B), F32),
                  pltpu.VMEM((rows, wq), F32)])
    return pl.pallas_call(
        functools.partial(_gdn_kernel, seq=seq, n_seq=n_seq, has_s0=s0 is not None, emit_state=prompt),
        grid=(nb // n_seq,),
        in_specs=in_specs,
        out_specs=out_specs,
        out_shape=out_shape,
        scratch_shapes=scratch,
        compiler_params=_cparams("arbitrary"),
        name="gdn_prompt" if prompt else "gdn_sample",
    )(*args)


def kernel(x_prompt, x_sample, state_hgrn, state_gdn, cache_na_k, cache_na_v, c, c_ctx, ada_w, ada_b, norm_g, w_in_ab, w_out_ab, hgrn_lb, gdn_conv, gdn_a_log, gdn_dt_bias, gn_hgrn, gn_gdn, w_qkv_na, qn_na, kn_na, rpb_na, w_out_na, w_mlp1, w_mlp2):
    cond = jnp.concatenate([c_ctx[None, :], c, jnp.zeros((N_MOD_ROWS - 1 - DEC_BATCH, D_MODEL), F32)], axis=0)
    mods, w_in_t = _modulation(cond, ada_w, ada_b, jnp.swapaxes(w_in_ab, 1, 2))
    xs = (x_prompt.reshape(N_PROMPT, D_MODEL), x_sample.reshape(N_SAMPLE, D_MODEL))

    w_gate_t = jnp.pad(w_in_t[D_MAIN_AB:], ((0, LANES - N_GATE_AB), (0, 0)))
    proj, gates, wo0, w1_0, w2_0 = _norm_proj(xs, mods[0], norm_g[0, 0], [w_in_t, w_gate_t], widths=[D_MAIN_AB, LANES],
                                              side=((w_out_ab, 0), (w_mlp1, 0), (w_mlp2, 0)), w_transposed=True)
    hg_consts = _hgrn_consts()
    gd_consts = _gdn_consts()
    hg_prompt, new_hgrn = _hgrn(proj, hgrn_lb, gn_hgrn[0], hg_consts, True)
    hg_sample, = _hgrn(proj, hgrn_lb, gn_hgrn[0], hg_consts, False, s0=state_hgrn)
    gd_args = (gdn_conv[0], gdn_a_log[0], gdn_dt_bias[0], gn_gdn[0], gd_consts)
    gd_prompt, new_gdn = _gdn(proj, gates, *gd_args, True)
    gd_sample, = _gdn(proj, gates, *gd_args, False, s0=state_gdn)
    xs, (w_qkv, wo1, w1_1, w2_1) = _post_mixer(xs, [(hg_prompt, hg_sample), (gd_prompt, gd_sample)], mods[0],
                                               norm_g[0, 1], wo0, w1_0, w2_0,
                                               side=((w_qkv_na, 0), (w_out_na, 0), (w_mlp1, 1), (w_mlp2, 1)))

    qkv, = _norm_proj(xs, mods[1], norm_g[1, 0], [w_qkv], tm=1024)
    at_prompt, new_kt, new_vt = _ctx_attention(qkv, qn_na[0], kn_na[0])
    time_minor = lambda a: jnp.swapaxes(a, -1, -2).reshape(a.shape[0], 1, H_C * HD_C, a.shape[3])
    at_sample = _na_attention(qkv, time_minor(cache_na_k), time_minor(cache_na_v), qn_na[0], kn_na[0], rpb_na[0])
    time_major = lambda a: jnp.swapaxes(a.reshape(BATCH, 1, H_C, HD_C, SEQ), -1, -2)
    new_k, new_v = time_major(new_kt), time_major(new_vt)
    (y_prompt, y_sample), _ = _post_mixer(xs, [(at_prompt, at_sample)], mods[1], norm_g[1, 1], wo1, w1_1, w2_1,
                                          split_out=True)

    return (y_prompt.reshape(BATCH, SEQ, D_MODEL), y_sample.reshape(DEC_BATCH, DEC_SEQ, D_MODEL),
            new_hgrn, new_gdn, new_k, new_v)
```

```python
import functools

import numpy as np
import jax
import jax.numpy as jnp
from jax import lax
from jax.experimental import pallas as pl
from jax.experimental.pallas import tpu as pltpu

F32 = jnp.float32
BF16 = jnp.bfloat16

D_MODEL = 1024
BATCH = 16
SEQ = 256
DEC_BATCH = 4
DEC_SEQ = 1024
PAST_LEN = 256
N_PROMPT = BATCH * SEQ
N_SAMPLE = DEC_BATCH * DEC_SEQ
N_TOK = N_PROMPT + N_SAMPLE
GRID_W = 64
GRID_ROWS = DEC_SEQ // GRID_W
H_A = 4
DK_A = 128
DV_A = 128
H_B = 4
DK_B = 128
DV_B = 128
SHORT_CONV = 5
H_C = 16
HD_C = 64
KH = 8
KW = 16
D_FF = 4 * D_MODEL
EPS = 1e-6
NEG_INF = -1e30
LANES = 128
SUBLANES = 8
BF16_ROWS = 16
VMEM_LIMIT = 56 * 1024 * 1024

N_MOD_ROWS = SUBLANES
N_GATE_AB = 16
D_MAIN_AB = 3 * H_A * DK_A + 2 * H_A * DV_A + H_B * (2 * DK_B + DV_B) + H_B * DV_B
CHUNK = 32
GBLK = LANES


def _cparams(*sem):
    return pltpu.CompilerParams(dimension_semantics=sem, vmem_limit_bytes=VMEM_LIMIT)


def _sigmoid(x):
    return 0.5 * jnp.tanh(0.5 * x) + 0.5


def _silu(x):
    return x * _sigmoid(x)


def _dot(a, b):
    return jnp.dot(a, b, preferred_element_type=F32)


def _dot_nt(a, b):
    return lax.dot_general(a, b, (((1,), (1,)), ((), ())), preferred_element_type=F32)


def _dot_tn(a, b):
    return lax.dot_general(a, b, (((0,), (0,)), ((), ())), preferred_element_type=F32)


def _split2(x):
    hi = x.astype(BF16)
    lo = (x - hi.astype(F32)).astype(BF16)
    return hi, lo


def _dot_const(m2, x):
    hi, lo = _split2(x)
    return _dot(m2, jnp.concatenate([hi, lo], axis=0))


def _mod_row(i, tm):
    start = i * tm
    return jnp.where(start < N_PROMPT, 0, 1 + (start - N_PROMPT) // DEC_SEQ)


def _mod_slice(mod_ref, row, k):
    return mod_ref[pl.ds(row, 1), k * D_MODEL:(k + 1) * D_MODEL]


def _norm_mod(x, g, sc, sh):
    ms = jnp.mean(x * x, axis=-1, keepdims=True)
    return (x * lax.rsqrt(ms + EPS) * g) * (1.0 + sc) + sh


def _mod_kernel(cond_ref, w_ref, b_ref, side_ref, o_ref, side_o_ref):
    s = _silu(cond_ref[...]).astype(BF16)
    o_ref[0] = _dot(s, w_ref[0].astype(BF16)) + b_ref[0]
    side_o_ref[...] = side_ref[...].astype(BF16)


MOD_TN = 768


def _modulation(cond8, ada_w, ada_b, side):
    depth = ada_w.shape[0]
    nj = ada_w.shape[2] // MOD_TN
    slab = pl.cdiv(side.shape[1], depth * nj * BF16_ROWS) * BF16_ROWS
    return pl.pallas_call(
        _mod_kernel,
        grid=(depth, nj),
        in_specs=[
            pl.BlockSpec((N_MOD_ROWS, D_MODEL), lambda l, j: (0, 0)),
            pl.BlockSpec((1, D_MODEL, MOD_TN), lambda l, j: (l, 0, j)),
            pl.BlockSpec((1, 1, MOD_TN), lambda l, j: (l, 0, j)),
            pl.BlockSpec((None, slab, side.shape[2]), lambda l, j: (0, l * nj + j, 0)),
        ],
        out_specs=[pl.BlockSpec((1, N_MOD_ROWS, MOD_TN), lambda l, j: (l, 0, j)),
                   pl.BlockSpec((slab, side.shape[2]), lambda l, j: (l * nj + j, 0))],
        out_shape=[jax.ShapeDtypeStruct((depth, N_MOD_ROWS, ada_w.shape[2]), F32),
                   jax.ShapeDtypeStruct(side.shape[1:], BF16)],
        compiler_params=_cparams("arbitrary", "arbitrary"),
        name="modulation",
    )(cond8, ada_w, ada_b.reshape(depth, 1, -1), side)


def _stream_specs(n_arrays, tm, width=D_MODEL):
    if n_arrays == 1:
        return [pl.BlockSpec((tm, width), lambda i: (i, 0))]
    npt = N_PROMPT // tm
    return [pl.BlockSpec((tm, width), lambda i: (jnp.minimum(i, npt - 1), 0)),
            pl.BlockSpec((tm, width), lambda i: (jnp.maximum(i - npt, 0), 0))]


def _stream_load(x_refs, tm):
    if len(x_refs) == 1:
        return x_refs[0][...]
    return jnp.where(pl.program_id(0) < N_PROMPT // tm, x_refs[0][...], x_refs[1][...])


def _side_specs(side, n_steps):
    in_specs = [pl.BlockSpec((None, w.shape[1] // n_steps, w.shape[2]), lambda i, l=l: (l, i, 0)) for w, l in side]
    out_specs = [pl.BlockSpec((w.shape[1] // n_steps, w.shape[2]), lambda i: (i, 0)) for w, _ in side]
    shapes = [jax.ShapeDtypeStruct(w.shape[1:], BF16) for w, _ in side]
    return in_specs, out_specs, shapes


def _side_cast(in_refs, out_refs):
    for i_ref, o_ref in zip(in_refs, out_refs):
        o_ref[...] = i_ref[...].astype(BF16)


def _norm_proj_kernel(*refs, tm, n_x, n_w, n_side, w_transposed):
    x_refs, (mod_ref, g_ref) = refs[:n_x], refs[n_x:n_x + 2]
    w_refs = refs[n_x + 2:n_x + 2 + n_w]
    side_in = refs[n_x + 2 + n_w:n_x + 2 + n_w + n_side]
    o_refs = refs[n_x + 2 + n_w + n_side:n_x + 2 + 2 * n_w + n_side]
    side_out = refs[n_x + 2 + 2 * n_w + n_side:]
    row = _mod_row(pl.program_id(0), tm)
    h = _norm_mod(_stream_load(x_refs, tm), g_ref[...], _mod_slice(mod_ref, row, 1), _mod_slice(mod_ref, row, 0)).astype(BF16)
    for w_ref, o_ref in zip(w_refs, o_refs):
        o_ref[...] = _dot_nt(h, w_ref[...]) if w_transposed else _dot(h, w_ref[...])
    _side_cast(side_in, side_out)


def _norm_proj(xs, mod, g, ws, widths=None, side=(), w_transposed=False, tm=512):
    n_w = len(ws)
    widths = widths or [w.shape[0 if w_transposed else 1] for w in ws]
    const = lambda i: (0, 0)
    w_block = (lambda n: (n, D_MODEL)) if w_transposed else (lambda n: (D_MODEL, n))
    side_in_specs, side_out_specs, side_shapes = _side_specs(side, N_TOK // tm)
    return pl.pallas_call(
        functools.partial(_norm_proj_kernel, tm=tm, n_x=len(xs), n_w=n_w, n_side=len(side), w_transposed=w_transposed),
        grid=(N_TOK // tm,),
        in_specs=_stream_specs(len(xs), tm) + [
            pl.BlockSpec(mod.shape, const),
            pl.BlockSpec((1, D_MODEL), const),
        ] + [pl.BlockSpec(w_block(n), const, pipeline_mode=pl.Buffered(1)) for n in widths] + side_in_specs,
        out_specs=[pl.BlockSpec((tm, n), lambda i: (i, 0)) for n in widths] + side_out_specs,
        out_shape=[jax.ShapeDtypeStruct((N_TOK, n), F32) for n in widths] + side_shapes,
        compiler_params=_cparams("arbitrary"),
        name="norm_proj",
    )(*xs, mod, g.reshape(1, D_MODEL), *ws, *[w for w, _ in side])


def _post_kernel(*refs, tm, ff_chunk, n_x, n_y, groups, n_side):
    x_refs = refs[:n_x]
    n_m = sum(n for n, _ in groups)
    m_refs = refs[n_x:n_x + n_m]
    mod_ref, g_ref, wo_ref, w1_ref, w2_ref = refs[n_x + n_m:n_x + n_m + 5]
    side_in = refs[n_x + n_m + 5:n_x + n_m + 5 + n_side]
    y_refs = refs[n_x + n_m + 5 + n_side:n_x + n_m + 5 + n_side + n_y]
    _side_cast(side_in, refs[n_x + n_m + 5 + n_side + n_y:])
    row = _mod_row(pl.program_id(0), tm)
    mix, first_ref, first_col = None, 0, 0
    for n, width in groups:
        part = _stream_load(m_refs[first_ref:first_ref + n], tm).astype(BF16)
        term = _dot(part, wo_ref[first_col:first_col + width, :])
        mix = term if mix is None else mix + term
        first_ref, first_col = first_ref + n, first_col + width
    x1 = _stream_load(x_refs, tm) + _mod_slice(mod_ref, row, 2) * mix
    h = _norm_mod(x1, g_ref[...], _mod_slice(mod_ref, row, 4), _mod_slice(mod_ref, row, 3)).astype(BF16)
    acc = jnp.zeros((tm, D_MODEL), F32)
    for k in range(0, D_FF, ff_chunk):
        a = jnp.maximum(_dot(h, w1_ref[:, k:k + ff_chunk]), 0.0)
        acc = acc + _dot((a * a).astype(BF16), w2_ref[k:k + ff_chunk, :])
    y = x1 + _mod_slice(mod_ref, row, 5) * acc
    if n_y == 1:
        y_refs[0][...] = y
    else:
        is_prompt = pl.program_id(0) < N_PROMPT // tm

        @pl.when(is_prompt)
        def _():
            y_refs[0][...] = y

        @pl.when(jnp.logical_not(is_prompt))
        def _():
            y_refs[1][...] = y


def _post_mixer(xs, mixed, mod, g, wo, w1, w2, split_out=False, side=(), tm=512, ff_chunk=1024):
    const = lambda i: (0, 0)
    resident = lambda w: pl.BlockSpec(w.shape, const, pipeline_mode=pl.Buffered(1))
    n_y = 2 if split_out else 1
    rows = (N_PROMPT, N_SAMPLE) if split_out else (N_TOK,)
    groups = tuple((len(grp), grp[0].shape[1]) for grp in mixed)
    mixed_specs = [spec for n, width in groups for spec in _stream_specs(n, tm, width)]
    side_in_specs, side_out_specs, side_shapes = _side_specs(side, N_TOK // tm)
    out = pl.pallas_call(
        functools.partial(_post_kernel, tm=tm, ff_chunk=ff_chunk, n_x=len(xs), n_y=n_y, groups=groups,
                          n_side=len(side)),
        grid=(N_TOK // tm,),
        in_specs=_stream_specs(len(xs), tm) + mixed_specs + [
            pl.BlockSpec(mod.shape, const),
            pl.BlockSpec((1, D_MODEL), const),
            resident(wo), resident(w1), resident(w2),
        ] + side_in_specs,
        out_specs=_stream_specs(n_y, tm) + side_out_specs,
        out_shape=[jax.ShapeDtypeStruct((r, D_MODEL), F32) for r in rows] + side_shapes,
        compiler_params=_cparams("arbitrary"),
        name="post_mixer",
    )(*xs, *[a for grp in mixed for a in grp], mod, g.reshape(1, D_MODEL), wo, w1, w2, *[w for w, _ in side])
    return tuple(out[:n_y]), tuple(out[n_y:])


PAIR = 2 * HD_C


def _pair_consts():
    lane = lax.broadcasted_iota(jnp.int32, (1, PAIR), 1)
    first = lane < HD_C
    ones_col = [jnp.where(lane == HD_C, 1.0, 0.0), jnp.where(lane == 0, 1.0, 0.0)]
    r = lax.broadcasted_iota(jnp.int32, (2 * PAIR, PAIR), 0) % PAIR
    cidx = lax.broadcasted_iota(jnp.int32, (2 * PAIR, PAIR), 1)
    mean2 = jnp.where(r // HD_C == cidx // HD_C, 1.0 / HD_C, 0.0).astype(BF16)
    return first, ones_col, mean2


def _pair_norm(x, w2, mean2):
    hi, lo = _split2(x * x)
    ms = _dot(jnp.concatenate([hi, lo], axis=1), mean2)
    return x * lax.rsqrt(ms + EPS) * w2


def _pair_queries(q, first):
    return [jnp.where(first, q, 0.0).astype(BF16), jnp.where(first, 0.0, q).astype(BF16)]


def _pair_values(v, first, ones_col):
    return [jnp.where(first, v, ones_col[0]).astype(BF16), jnp.where(first, ones_col[1], v).astype(BF16)]


def _pair_output(o_aug, first):
    den = [o_aug[0][:, HD_C:HD_C + 1], o_aug[1][:, 0:1]]
    return jnp.where(first, o_aug[0] / den[0], o_aug[1] / den[1])


def _row_max(*pieces):
    tiles = [p[:, i:i + LANES] for p in pieces for i in range(0, p.shape[1], LANES)]
    return jnp.max(functools.reduce(jnp.maximum, tiles), axis=-1, keepdims=True)


CTX_PAIRS = 8


def _ctx_attn_kernel(q_ref, k_ref, v_ref, qn_ref, kn_ref, o_ref, kc_ref, vc_ref):
    first, ones_col, mean2 = _pair_consts()
    lanes = [slice(p * PAIR, (p + 1) * PAIR) for p in range(CTX_PAIRS)]
    qn = [_pair_norm(q_ref[:, ln], qn_ref[...], mean2) * HD_C ** -0.5 for ln in lanes]
    kn = [_pair_norm(k_ref[:, ln], kn_ref[...], mean2) for ln in lanes]
    v = [v_ref[:, ln] for ln in lanes]
    kt = [x.T for x in kn]
    for p in range(CTX_PAIRS):
        kc_ref[0, 0, lanes[p], :] = kt[p]
        vc_ref[0, 0, lanes[p], :] = v[p].T
    q = [_pair_queries(x, first) for x in qn]
    va = [_pair_values(x, first, ones_col) for x in v]
    s = [[_dot(q[p][j], kt[p].astype(BF16)) for j in range(2)] for p in range(CTX_PAIRS)]
    pr = [[jnp.exp(x - _row_max(x)).astype(BF16) for x in sp] for sp in s]
    for p in range(CTX_PAIRS):
        o_ref[:, lanes[p]] = _pair_output([_dot(pr[p][j], va[p][j]) for j in range(2)], first)


def _ctx_attention(qkv, qn, kn):
    heads = 2 * CTX_PAIRS
    ng = H_C // heads
    wide = CTX_PAIRS * PAIR
    blk = lambda off: pl.BlockSpec((SEQ, wide), lambda b, p: (b, off + p))
    cache_spec = pl.BlockSpec((1, 1, wide, SEQ), lambda b, p: (b, 0, p, 0))
    cache_shape = jax.ShapeDtypeStruct((BATCH, 1, H_C * HD_C, SEQ), F32)
    return pl.pallas_call(
        _ctx_attn_kernel,
        grid=(BATCH, ng),
        in_specs=[blk(0), blk(ng), blk(2 * ng),
                  pl.BlockSpec((1, PAIR), lambda b, p: (0, 0)),
                  pl.BlockSpec((1, PAIR), lambda b, p: (0, 0))],
        out_specs=[pl.BlockSpec((SEQ, wide), lambda b, p: (b, p)), cache_spec, cache_spec],
        out_shape=[jax.ShapeDtypeStruct((N_PROMPT, D_MODEL), F32), cache_shape, cache_shape],
        compiler_params=_cparams("arbitrary", "arbitrary"),
        name="ctx_attention",
    )(qkv, qkv, qkv, jnp.tile(qn.reshape(1, HD_C), (1, 2)), jnp.tile(kn.reshape(1, HD_C), (1, 2)))


def _na_row_start(r):
    return min(max(r - KH // 2, 0), GRID_ROWS - KH)


NA_ROW_GROUP = 16


def _na_attn_kernel(q_ref, k_ref, v_ref, kc_ref, vc_ref, qn_ref, kn_ref, bias_ref, o_ref, qs, ks, vs, bias_s):
    first, ones_col, mean2 = _pair_consts()

    @pl.when(pl.program_id(1) == 0)
    def _():
        q_col = lax.broadcasted_iota(jnp.int32, (GRID_W, PAIR), 0)
        lane = lax.broadcasted_iota(jnp.int32, (GRID_W, PAIR), 1)
        k_col = lane % GRID_W
        w0 = jnp.clip(q_col - KW // 2, 0, GRID_W - KW)
        outside = jnp.where((k_col >= w0) & (k_col < w0 + KW), 0.0, NEG_INF)
        n_dr = 2 * KH - 1
        for j in range(2):
            band = []
            for dr in range(n_dr):
                row = jnp.broadcast_to(bias_ref[j, dr:dr + 1, :], (GRID_W, PAIR))
                band.append([pltpu.roll(row, (half * GRID_W - (KW - 1)) % PAIR, axis=1, stride=1, stride_axis=0)
                             for half in range(2)])
            zero = jnp.zeros((GRID_W, PAIR), F32)
            for cp in range(2):
                for t in range(KH):
                    lo, hi = 2 * t + cp, 2 * t + cp + 1
                    tile = jnp.where(lane < GRID_W, band[lo][0] if lo < n_dr else zero,
                                     band[hi][1] if hi < n_dr else zero)
                    bias_s[j, cp, :, t * PAIR:(t + 1) * PAIR] = tile + outside

    q2 = _pair_queries(_pair_norm(q_ref[...], qn_ref[...], mean2) * HD_C ** -0.5, first)
    v2 = _pair_values(v_ref[...], first, ones_col)
    ks[...] = _pair_norm(k_ref[...], kn_ref[...], mean2).astype(BF16)
    for j in range(2):
        qs[:, j] = q2[j].reshape(GRID_ROWS, GRID_W, PAIR)
        vs[j] = v2[j]
    kt_ctx = kc_ref[0, 0].astype(BF16)
    vt = vc_ref[0, 0]
    ch = lax.broadcasted_iota(jnp.int32, vt.shape, 0)
    vt_ctx = [jnp.where(ch < HD_C, vt, jnp.where(ch == HD_C, 1.0, 0.0)).astype(BF16),
              jnp.where(ch < HD_C, jnp.where(ch == 0, 1.0, 0.0), vt).astype(BF16)]
    for r0 in range(0, GRID_ROWS, NA_ROW_GROUP):
        units = [(r, j) for r in range(r0, r0 + NA_ROW_GROUP) for j in range(2)]
        rows = {r: slice(r * GRID_W, (r + 1) * GRID_W) for r, _ in units}
        wins = {r: slice(_na_row_start(r) * GRID_W, (_na_row_start(r) + KH) * GRID_W) for r, _ in units}
        unit_rows = lambda k: slice(k * GRID_W, (k + 1) * GRID_W)
        s_ctx_all = _dot(qs[r0:r0 + NA_ROW_GROUP].reshape(NA_ROW_GROUP * PAIR, PAIR), kt_ctx)
        s_ctx = [s_ctx_all[unit_rows(2 * (r - r0) + j)] for r, j in units]
        s_row = {r: _dot_nt(qs[r].reshape(PAIR, PAIR), ks[wins[r], :]) for r in rows}
        s_win = []
        for r, j in units:
            dr0 = KH - 1 - (r - _na_row_start(r))
            lane0 = (dr0 - dr0 % 2) * GRID_W
            s_win.append(s_row[r][unit_rows(j)] + bias_s[j, dr0 % 2, :, lane0:lane0 + KH * GRID_W])
        m = [_row_max(a, b) for a, b in zip(s_win, s_ctx)]
        p_win = [jnp.exp(a - mm).astype(BF16) for a, mm in zip(s_win, m)]
        p_ctx = [jnp.exp(b - mm).astype(BF16) for b, mm in zip(s_ctx, m)]
        o_aug = [_dot(p_win[i], vs[j, wins[r], :]) + _dot_nt(p_ctx[i], vt_ctx[j]) for i, (r, j) in enumerate(units)]
        for i in range(0, len(units), 2):
            o_ref[rows[units[i][0]], :] = _pair_output(o_aug[i:i + 2], first)


NA_BIAS_LANES = 2 * KH * GRID_W


def _na_attention(qkv, cache_kt, cache_vt, qn, kn, rpb):
    nhp = H_C // 2
    row0 = N_PROMPT // DEC_SEQ
    blk = lambda off: pl.BlockSpec((DEC_SEQ, 2 * HD_C), lambda p, b: (row0 + b, off + p))
    cache_spec = pl.BlockSpec((1, 1, PAIR, PAST_LEN), lambda p, b: (b, 0, p, 0))
    rpb_rows = 2 * KH
    bias = jnp.pad(rpb.astype(F32), ((0, 0), (0, rpb_rows - rpb.shape[1]), (0, PAIR - rpb.shape[2])))
    return pl.pallas_call(
        _na_attn_kernel,
        grid=(nhp, DEC_BATCH),
        in_specs=[blk(0), blk(nhp), blk(2 * nhp), cache_spec, cache_spec,
                  pl.BlockSpec((1, PAIR), lambda p, b: (0, 0)),
                  pl.BlockSpec((1, PAIR), lambda p, b: (0, 0)),
                  pl.BlockSpec((2, rpb_rows, PAIR), lambda p, b: (p, 0, 0))],
        out_specs=pl.BlockSpec((DEC_SEQ, 2 * HD_C), lambda p, b: (b, p)),
        out_shape=jax.ShapeDtypeStruct((N_SAMPLE, D_MODEL), F32),
        scratch_shapes=[pltpu.VMEM((GRID_ROWS, 2, GRID_W, PAIR), BF16), pltpu.VMEM((DEC_SEQ, PAIR), BF16),
                        pltpu.VMEM((2, DEC_SEQ, PAIR), BF16), pltpu.VMEM((2, 2, GRID_W, NA_BIAS_LANES), F32)],
        compiler_params=_cparams("arbitrary", "arbitrary"),
        name="na_attention",
    )(qkv, qkv, qkv, cache_kt, cache_vt, jnp.tile(qn.reshape(1, HD_C), (1, 2)), jnp.tile(kn.reshape(1, HD_C), (1, 2)),
      bias)


def _seq_layout(prompt):
    return (SEQ, BATCH, 0) if prompt else (DEC_SEQ, DEC_BATCH, N_PROMPT // DEC_SEQ)


def _flip_blocks(m, c):
    r, s = m.shape
    return m.reshape(r // c, c, s // c, c)[:, ::-1, :, ::-1].reshape(r, s)


def _rms_gate(x, gn, gate):
    ms = jnp.mean(x * x, axis=-1, keepdims=True)
    return x * lax.rsqrt(ms + EPS) * gn * _silu(gate)


HG_LEVELS = tuple(CHUNK >> (i + 1) for i in range(CHUNK.bit_length() - 1))
HG_NL = len(HG_LEVELS)
HG_STACK = (HG_NL + 1) * CHUNK
TOT_ROWS = BF16_ROWS
HG_ROWS = (HG_NL + 2) * CHUNK + TOT_ROWS


def _hgrn_consts():
    c = CHUNK
    level_rows = []
    mask = np.zeros((HG_STACK, HG_STACK), np.float32)
    mask[:c, :c] = np.eye(c)
    for li, b in enumerate(HG_LEVELS):
        m = np.zeros((c, c), np.float32)
        blk = np.zeros((c, c), np.float32)
        for t in range(c):
            mid = (t // (2 * b)) * 2 * b + b
            if t >= mid:
                m[t, mid:t + 1] = 1.0
                blk[t, mid - b:mid] = 1.0
            else:
                m[t, t + 1:mid] = 1.0
        level_rows.append(m)
        mask[(li + 1) * c:(li + 2) * c, (li + 1) * c:(li + 2) * c] = blk
    dq = np.tril(np.ones((c, c), np.float32))
    dk = np.triu(np.ones((c, c), np.float32), 1)
    body = np.concatenate(level_rows + [dq, dk], axis=0)
    tot = np.ones((TOT_ROWS, c), np.float32)
    mcs, masks = [], []
    for reverse in (False, True):
        bm = _flip_blocks(body, c) if reverse else body
        mk = _flip_blocks(mask, c) if reverse else mask
        mc = np.concatenate([bm, tot], axis=0)
        mcs.append(np.concatenate([mc, mc], axis=1))
        masks.append(mk)
    return jnp.asarray(np.stack(mcs), BF16), jnp.asarray(np.stack(masks), F32)


HG_FAST = 64
HG_HALF = HG_FAST // 2
HG_FAST_ROWS = 4 * HG_FAST + TOT_ROWS
HG_SAFE_EXP = 40.0
HG_FAST_STEPS = 4


def _hgrn_fast_consts(seq):
    c, m = HG_FAST, HG_HALF
    aq = np.zeros((c, c), np.float32)
    for t in range(c):
        if t >= m:
            aq[t, m:t + 1] = 1.0
        else:
            aq[t, t + 1:m] = -1.0
    dq = np.tril(np.ones((c, c), np.float32))
    dk = np.triu(np.ones((c, c), np.float32), 1)
    body = np.concatenate([aq, -aq, dq, dk], axis=0)
    tot = np.ones((TOT_ROWS, c), np.float32)
    causal = np.tril(np.ones((c, c), np.float32))
    mfs, masks = [], []
    for reverse in (False, True):
        bm = _flip_blocks(body, c) if reverse else body
        mf = np.concatenate([bm, tot], axis=0)
        mfs.append(np.concatenate([mf, mf], axis=1))
        masks.append(causal.T if reverse else causal)
    n_half = seq // m
    half = np.zeros((max(n_half, BF16_ROWS), seq), np.float32)
    for i in range(n_half):
        half[i, i * m:(i + 1) * m] = 1.0
    return jnp.asarray(np.stack(mfs), BF16), jnp.asarray(np.stack(masks), F32), jnp.asarray(half, BF16)


def _hgrn_kernel(*refs, seq, has_s0, emit_state):
    it = iter(refs)
    qa_ref, ff_ref, fb_ref, ia_ref, ga_ref, lb_ref, gn_ref, mc_ref, mask_ref = [next(it) for _ in range(9)]
    mf_ref, causal_ref, half_ref = [next(it) for _ in range(3)]
    s0_ref = next(it) if has_s0 else None
    o_ref = next(it)
    st_ref = next(it) if emit_state else None
    s_scr, acc, f_s, lf_s = [next(it) for _ in range(4)]
    c = CHUNK
    n_chunks = seq // c
    combos = [(d, h) for d in range(2) for h in range(H_A)]
    lanes = [slice(h * DK_A, (h + 1) * DK_A) for h in range(H_A)]
    add = lambda a, b: a + b

    lb_raw = lb_ref[...]
    lb_e = jnp.exp(lb_raw - jnp.max(lb_raw, axis=0, keepdims=True))
    lb_all = lb_e[0:1] / jnp.sum(lb_e, axis=0, keepdims=True)

    for d in range(2):
        for h in range(H_A):
            s_scr[d, h] = s0_ref[0, 0, d, h].T if has_s0 else jnp.zeros((DV_A, DK_A), F32)
    acc[...] = jnp.zeros(acc.shape, F32)

    worst = []
    for d, fr_ref in enumerate((ff_ref, fb_ref)):
        f = lb_all + (1.0 - lb_all) * _sigmoid(fr_ref[...])
        lf = jnp.log(f)
        f_s[d] = f
        lf_s[d] = lf
        worst.append(jnp.max(_dot(half_ref[...], (-lf).astype(BF16))))
    safe = jnp.maximum(worst[0], worst[1]) <= HG_SAFE_EXP

    def fast_body(n, carry):
        cf = HG_FAST
        n_fast = seq // cf
        steps = range(HG_FAST_STEPS)
        chunk = lambda d, t: (n * HG_FAST_STEPS + t) if d == 0 else (n_fast - 1 - n * HG_FAST_STEPS - t)
        rows = [[pl.ds(pl.multiple_of(chunk(d, t) * cf, cf), cf) for t in steps] for d in range(2)]
        units = [(t, d, h) for t in steps for d, h in combos]
        e_all = [[jnp.exp(_dot_const(mf_ref[d], lf_s[d, rows[d][t], :])) for t in steps] for d in range(2)]
        q_all = [[_silu(qa_ref[rows[d][t], :]) * DK_A ** -0.5 for t in steps] for d in range(2)]
        k_all = [[1.0 - f_s[d, rows[d][t], :] for t in steps] for d in range(2)]
        v_all = [[ia_ref[rows[d][t], :].astype(BF16) for t in steps] for d in range(2)]
        qs = {u: q_all[u[1]][u[0]][:, lanes[u[2]]] for u in units}
        ks = {u: k_all[u[1]][u[0]][:, lanes[u[2]]] for u in units}
        vs = {u: v_all[u[1]][u[0]][:, lanes[u[2]]] for u in units}
        es = {u: [e_all[u[1]][u[0]][i * cf:(i + 1) * cf, lanes[u[2]]] for i in range(4)] for u in units}
        p = {u: jnp.where(causal_ref[u[1]] > 0.0,
                          _dot_nt((qs[u] * es[u][0]).astype(BF16), (ks[u] * es[u][1]).astype(BF16)), 0.0).astype(BF16)
             for u in units}
        intra = {u: _dot(p[u], vs[u]) for u in units}
        upd = {u: _dot_tn(vs[u], (ks[u] * es[u][3]).astype(BF16)) for u in units}
        qdec = {u: (qs[u] * es[u][2]).astype(BF16) for u in units}
        st = {(d, h): s_scr[d, h] for d, h in combos}
        o = {}
        for t in steps:
            for d, h in combos:
                u = (t, d, h)
                o[u] = intra[u] + _dot_nt(qdec[u], st[d, h].astype(BF16))
                st[d, h] = st[d, h] * e_all[d][t][4 * cf:4 * cf + 1, lanes[h]] + upd[u]
        for t in steps:
            for d in range(2):
                acc[rows[d][t], :] += jnp.concatenate([o[t, d, h] for h in range(H_A)], axis=1)
        for d, h in combos:
            s_scr[d, h] = st[d, h]
        return carry

    def body(n, carry):
        rows = [pl.ds(pl.multiple_of((n if d == 0 else n_chunks - 1 - n) * c, c), c) for d in range(2)]
        f_all = [f_s[d, rows[d], :] for d in range(2)]
        e_all = [jnp.exp(_dot_const(mc_ref[d], lf_s[d, rows[d], :])) for d in range(2)]
        q_all = [_silu(qa_ref[rows[d], :]) * DK_A ** -0.5 for d in range(2)]
        v_all = [ia_ref[rows[d], :].astype(BF16) for d in range(2)]
        st = [s_scr[d, h] for d, h in combos]
        qs, ks, vs, es = [], [], [], []
        for d, h in combos:
            qs.append(q_all[d][:, lanes[h]])
            ks.append(1.0 - f_all[d][:, lanes[h]])
            vs.append(v_all[d][:, lanes[h]])
            es.append(e_all[d][:, lanes[h]])
        lvl = [[e[i * c:(i + 1) * c] for i in range(HG_NL + 2)] for e in es]
        qst = [jnp.concatenate([q] + [q * l[i] for i in range(HG_NL)], axis=0).astype(BF16) for q, l in zip(qs, lvl)]
        kst = [jnp.concatenate([k] + [k * l[i] for i in range(HG_NL)], axis=0).astype(BF16) for k, l in zip(ks, lvl)]
        r = [(_dot_nt(qst[i], kst[i]) * mask_ref[d]).astype(BF16) for i, (d, h) in enumerate(combos)]
        ost = [_dot(r[i], jnp.concatenate([vs[i]] * (HG_NL + 1), axis=0)) for i in range(len(combos))]
        inter = [_dot_nt((qs[i] * lvl[i][HG_NL]).astype(BF16), st[i].astype(BF16)) for i in range(len(combos))]
        upd = [_dot_tn(vs[i], (ks[i] * lvl[i][HG_NL + 1]).astype(BF16)) for i in range(len(combos))]
        o = [functools.reduce(lambda a, b: a + b, [ost[i][j * c:(j + 1) * c] for j in range(HG_NL + 1)]) + inter[i]
             for i in range(len(combos))]
        for d in range(2):
            acc[rows[d], :] += jnp.concatenate(o[d * H_A:(d + 1) * H_A], axis=1)
        for i, (d, h) in enumerate(combos):
            e_tot = es[i][(HG_NL + 2) * c:(HG_NL + 2) * c + 1]
            s_scr[d, h] = st[i] * e_tot + upd[i]
        return carry

    @pl.when(safe)
    def _():
        lax.fori_loop(0, seq // (HG_FAST * HG_FAST_STEPS), fast_body, 0)

    @pl.when(jnp.logical_not(safe))
    def _():
        lax.fori_loop(0, n_chunks, body, 0)

    for h in range(H_A):
        ln = slice(h * DV_A, (h + 1) * DV_A)
        o_ref[:, ln] = _rms_gate(acc[:, ln], gn_ref[...], ga_ref[:, ln])
    if emit_state:
        for d in range(2):
            for h in range(H_A):
                st_ref[0, 0, d, h] = s_scr[d, h].T


def _hgrn(proj, hgrn_lb, gn, consts, prompt, s0=None, layer=0):
    seq, nb, rb0 = _seq_layout(prompt)
    consts = list(consts) + list(_hgrn_fast_consts(seq))
    wa = H_A * DK_A
    blk = lambda j: pl.BlockSpec((seq, wa), lambda b: (rb0 + b, j))
    const2 = lambda b: (0, 0)
    st_block = (1, 1, 2, H_A, DK_A, DV_A)
    in_specs = [blk(0), blk(1), blk(2), blk(3), blk(4),
                pl.BlockSpec(hgrn_lb.shape, const2), pl.BlockSpec((1, DV_A), const2)]
    in_specs += [pl.BlockSpec(m.shape, lambda b, nd=m.ndim: (0,) * nd) for m in consts]
    args = [proj] * 5 + [hgrn_lb, gn.reshape(1, DV_A)] + consts
    if s0 is not None:
        in_specs.append(pl.BlockSpec(st_block, lambda b: (b, layer, 0, 0, 0, 0)))
        args.append(s0)
    out_specs = [pl.BlockSpec((seq, wa), lambda b: (b, 0))]
    out_shape = [jax.ShapeDtypeStruct((nb * seq, wa), F32)]
    if prompt:
        out_specs.append(pl.BlockSpec(st_block, lambda b: (b, 0, 0, 0, 0, 0)))
        out_shape.append(jax.ShapeDtypeStruct((nb, 1, 2, H_A, DK_A, DV_A), F32))
    return pl.pallas_call(
        functools.partial(_hgrn_kernel, seq=seq, has_s0=s0 is not None, emit_state=prompt),
        grid=(nb,),
        in_specs=in_specs,
        out_specs=out_specs,
        out_shape=out_shape,
        scratch_shapes=[pltpu.VMEM((2, H_A, DV_A, DK_A), F32), pltpu.VMEM((seq, wa), F32),
                        pltpu.VMEM((2, seq, wa), F32), pltpu.VMEM((2, seq, wa), F32)],
        compiler_params=_cparams("arbitrary"),
        name="hgrn_prompt" if prompt else "hgrn_sample",
    )(*args)


GD_SUB = GBLK // CHUNK
GD_BLOCKS_PER_ITER = 2
GD_PROMPT_SEQS = 2
GD_ROWS = 2 * GBLK + TOT_ROWS


def _gdn_consts():
    n, c = GBLK, CHUNK
    same = (np.arange(n)[:, None] // c) == (np.arange(n)[None, :] // c)
    tri = (same & (np.arange(n)[None, :] <= np.arange(n)[:, None])).astype(np.float32)
    sup = (same & (np.arange(n)[None, :] > np.arange(n)[:, None])).astype(np.float32)
    tot = np.zeros((TOT_ROWS, n), np.float32)
    for s in range(GD_SUB):
        tot[s, s * c:(s + 1) * c] = 1.0
    mgs, tts, tris = [], [], []
    for reverse in (False, True):
        t = _flip_blocks(tri, c) if reverse else tri
        s = _flip_blocks(sup, c) if reverse else sup
        mg = np.concatenate([t, s, tot], axis=0)
        mgs.append(np.concatenate([mg, mg], axis=1))
        tts.append(np.concatenate([t.T, t.T], axis=0))
        tris.append(t)
    tris.append(same.astype(np.float32))
    return jnp.asarray(np.stack(mgs), BF16), jnp.asarray(np.stack(tts), BF16), jnp.asarray(np.stack(tris), F32)


def _softplus(x):
    return jnp.maximum(x, 0.0) + jnp.log(1.0 + jnp.exp(-jnp.abs(x)))


CONV_PAD = SUBLANES


def _conv_silu(x, w, seq):
    half = SHORT_CONV // 2
    pad = jnp.zeros((CONV_PAD, x.shape[1]), x.dtype)
    xe = jnp.concatenate([pad, x, pad], axis=0)
    acc = xe * w[half:half + 1]
    for j in range(SHORT_CONV):
        shift = half - j
        if shift != 0:
            acc = acc + pltpu.roll(xe, shift % (seq + 2 * CONV_PAD), axis=0) * w[j:j + 1]
    return _silu(acc[CONV_PAD:seq + CONV_PAD])


def _l2norm_heads(x, n_heads, width, scale):
    outs = []
    for h in range(n_heads):
        xh = x[:, h * width:(h + 1) * width]
        outs.append(xh * (lax.rsqrt(jnp.sum(xh * xh, axis=-1, keepdims=True) + EPS) * scale))
    return jnp.concatenate(outs, axis=-1)


def _gdn_kernel(*refs, seq, n_seq, has_s0, emit_state):
    it = iter(refs)
    (q_ref, k_ref, v_ref, gb_ref, gate_ref, cw_ref, alog_ref, dt_ref, gn_ref,
     mg_ref, tt_ref, tri_ref) = [next(it) for _ in range(12)]
    s0_ref = next(it) if has_s0 else None
    o_ref = next(it)
    st_ref = next(it) if emit_state else None
    qn, kn, vn, u_s, w_s, qg_s, kdt_s, at_s, et_s, s_scr, acc = [next(it) for _ in range(11)]
    c = CHUNK
    n_chunks = seq // c
    n_blocks = n_seq * seq // GBLK
    wq = H_B * DK_B
    n_dh = 2 * H_B
    combos = [(d, h) for d in range(2) for h in range(H_B)]
    lanes = [slice(h * DK_B, (h + 1) * DK_B) for h in range(H_B)]

    for s in range(n_seq):
        sr = slice(s * seq, (s + 1) * seq)
        qn[sr, :] = _l2norm_heads(_conv_silu(q_ref[sr, :], cw_ref[:, 0:wq], seq), H_B, DK_B, DK_B ** -0.5)
        kn[sr, :] = _l2norm_heads(_conv_silu(k_ref[sr, :], cw_ref[:, wq:2 * wq], seq), H_B, DK_B, 1.0)
        vn[sr, :] = _conv_silu(v_ref[sr, :], cw_ref[:, 2 * wq:3 * wq], seq)
        for i in range(n_dh):
            s_scr[s * n_dh + i] = (s0_ref[s, 0, i // H_B, i % H_B] if has_s0 else jnp.zeros((DK_B, DV_B), F32))
    acc[...] = jnp.zeros(acc.shape, F32)

    eye = (lax.broadcasted_iota(jnp.int32, (GBLK, GBLK), 0)
           == lax.broadcasted_iota(jnp.int32, (GBLK, GBLK), 1)).astype(F32)
    eye_pk = (lax.broadcasted_iota(jnp.int32, (c, GBLK), 0)
              == lax.broadcasted_iota(jnp.int32, (c, GBLK), 1) % c).astype(F32)
    bwd_lane = lax.broadcasted_iota(jnp.int32, (1, LANES), 1) % n_dh >= H_B
    add = lambda a, b: a + b

    same_chunk = tri_ref[2].astype(BF16)

    def expand(pk):
        return jnp.concatenate([pk] * GD_SUB, axis=0) * same_chunk

    def pack(bd):
        return functools.reduce(add, [bd[s * c:(s + 1) * c] for s in range(GD_SUB)])

    def weights(hi, lo):
        return jnp.concatenate([expand(hi), expand(lo)], axis=1)

    def dot3_split(a_hi, a_lo, w2):
        m, n = a_hi.shape[0], w2.shape[1] // 2
        t = _dot(jnp.concatenate([a_hi, a_lo], axis=0), w2)
        return t[:m, :n] + t[m:, :n] + t[:m, n:]

    def block_body(it, carry):
        blks = [it * GD_BLOCKS_PER_ITER + o for o in range(GD_BLOCKS_PER_ITER)]
        units = [(o, d, h) for o in range(GD_BLOCKS_PER_ITER) for d, h in combos]
        idx = lambda d, h: d * H_B + h
        col = lambda x, j: jnp.broadcast_to(x[:, j:j + 1], (GBLK, DK_B))
        rows = [pl.ds(pl.multiple_of(b * GBLK, GBLK), GBLK) for b in blks]
        gates = [gate_ref[r, :] for r in rows]
        glog_all = [-jnp.exp(alog_ref[...]) * _softplus(x + dt_ref[...]) for x in gates]
        beta_all = [_sigmoid(x) for x in gates]
        g2 = [jnp.concatenate(_split2(x), axis=0) for x in glog_all]
        dg = [[_dot(mg_ref[d], x) for d in range(2)] for x in g2]
        dsel = [jnp.where(bwd_lane, x[1], x[0]) for x in dg]
        eg_all = [jnp.exp(x) for x in dsel]
        gt = [[_dot_tn(x, tt_ref[d]) for d in range(2)] for x in g2]
        qs = [[qn[r, ln] for ln in lanes] for r in rows]
        ks = [[kn[r, ln] for ln in lanes] for r in rows]
        vs = [[vn[r, ln] for ln in lanes] for r in rows]
        betas = [col(beta_all[o], n_dh + idx(d, h)) for o, d, h in units]
        kbs = [ks[o][h] * betas[u] for u, (o, d, h) in enumerate(units)]
        kb_of = {unit: kbs[u] for u, unit in enumerate(units)}
        kk = {(o, h): _dot_nt(jnp.concatenate([qs[o][h], kb_of[o, 0, h], kb_of[o, 1, h]], axis=0).astype(BF16),
                              ks[o][h].astype(BF16))
              for o in range(GD_BLOCKS_PER_ITER) for h in range(H_B)}
        decay = []
        for o, d, h in units:
            inside = tri_ref[d] > 0.0
            gd = col(dsel[o][:GBLK], idx(d, h)) - gt[o][d][idx(d, h):idx(d, h) + 1, :]
            decay.append(jnp.where(inside, jnp.exp(jnp.where(inside, gd, 0.0)), 0.0))
        attn = [kk[o, h][:GBLK] * decay[u] for u, (o, d, h) in enumerate(units)]
        p_pk = [pack(kk[o, h][(1 + d) * GBLK:(2 + d) * GBLK] * decay[u] * (1.0 - eye))
                for u, (o, d, h) in enumerate(units)]
        x_pk = [eye_pk - p for p in p_pk]
        p_sp = [_split2(p) for p in p_pk]
        p_w = [weights(*s) for s in p_sp]
        for _ in range(CHUNK.bit_length() - 2):
            p_pk = [dot3_split(*s, w) for s, w in zip(p_sp, p_w)]
            p_sp = [_split2(p) for p in p_pk]
            p_w = [weights(*s) for s in p_sp]
            x_pk = [x + dot3_split(*_split2(x), w) for x, w in zip(x_pk, p_w)]
        eg_col = [col(eg_all[o][:GBLK], idx(d, h)) for o, d, h in units]
        ekd_col = [col(eg_all[o][GBLK:2 * GBLK], idx(d, h)) for o, d, h in units]
        rhs = [_split2(jnp.concatenate([vs[o][h] * betas[u], kbs[u] * eg_col[u]], axis=1))
               for u, (o, d, h) in enumerate(units)]
        t_sp = [[expand(part) for part in _split2(x)] for x in x_pk]
        uw = [_dot(t[0], r[0]) + (_dot(t[0], r[1]) + _dot(t[1], r[0])) for t, r in zip(t_sp, rhs)]
        for u, (o, d, h) in enumerate(units):
            i = idx(d, h)
            qg = (qs[o][h] * eg_col[u]).astype(BF16)
            kdt_s[i, blks[o]] = (ks[o][h] * ekd_col[u]).T.astype(BF16)
            for s in range(GD_SUB):
                cn = blks[o] * GD_SUB + s
                r = slice(s * c, (s + 1) * c)
                u_s[i, cn] = uw[u][r, :DV_B]
                w_s[i, cn] = uw[u][r, DV_B:].astype(BF16)
                qg_s[i, cn] = qg[r]
                at_s[i, cn] = attn[u][r].astype(BF16)
                et_s[i, cn] = jnp.broadcast_to(eg_all[o][2 * GBLK + s:2 * GBLK + s + 1, i:i + 1], (SUBLANES, DV_B))
        return carry

    n_iter = n_blocks // GD_BLOCKS_PER_ITER
    lax.fori_loop(0, n_iter, block_body, 0, unroll=True)

    def chunk_body(n, carry):
        chains = [(s, d, h) for s in range(n_seq) for d, h in combos]
        idx = lambda d, h: d * H_B + h
        cn = {(s, d): s * n_chunks + (n if d == 0 else n_chunks - 1 - n) for s in range(n_seq) for d in range(2)}
        rows = {key: pl.ds(pl.multiple_of(v * c, c), c) for key, v in cn.items()}
        sub_of_row = lax.broadcasted_iota(jnp.int32, (GBLK, 1), 0) // c
        in_chunk = {key: sub_of_row == v % GD_SUB for key, v in cn.items()}
        st = {ch: s_scr[ch[0] * n_dh + idx(ch[1], ch[2])] for ch in chains}
        ws = {(s, d, h): _dot(jnp.concatenate([w_s[idx(d, h), cn[s, d]], qg_s[idx(d, h), cn[s, d]]], axis=0),
                              st[s, d, h].astype(BF16)) for s, d, h in chains}
        vblk = {(s, d, h): jnp.where(in_chunk[s, d],
                                     jnp.concatenate([u_s[idx(d, h), cn[s, d]] - ws[s, d, h][:c]] * GD_SUB, axis=0),
                                     0.0).astype(BF16) for s, d, h in chains}
        r = {(s, d, h): _dot(jnp.concatenate([at_s[idx(d, h), cn[s, d]], kdt_s[idx(d, h), cn[s, d] // GD_SUB]], axis=0),
                             vblk[s, d, h]) for s, d, h in chains}
        for s in range(n_seq):
            for d in range(2):
                acc[rows[s, d], :] += jnp.concatenate([ws[s, d, h][c:] + r[s, d, h][:c] for h in range(H_B)], axis=1)
        for s, d, h in chains:
            s_scr[s * n_dh + idx(d, h)] = st[s, d, h] * et_s[idx(d, h), cn[s, d]][0:1] + r[s, d, h][c:]
        return carry

    lax.fori_loop(0, n_chunks, chunk_body, 0)

    for h in range(H_B):
        ln = slice(h * DV_B, (h + 1) * DV_B)
        o_ref[:, ln] = _rms_gate(acc[:, ln], gn_ref[...], gb_ref[:, ln])
    if emit_state:
        for s in range(n_seq):
            for i in range(n_dh):
                st_ref[s, 0, i // H_B, i % H_B] = s_scr[s * n_dh + i]


def _gdn(proj, gates, conv_w, a_log, dt_bias, gn, consts, prompt, s0=None, layer=0):
    seq, nb, rb0 = _seq_layout(prompt)
    n_seq = GD_PROMPT_SEQS if prompt else 1
    rows = n_seq * seq
    n_chunks = rows // CHUNK
    wq = H_B * DK_B
    blk = lambda j: pl.BlockSpec((rows, wq), lambda b: (rb0 + b, j))
    const2 = lambda b: (0, 0)
    const3 = lambda b: (0, 0, 0)
    st_block = (n_seq, 1, 2, H_B, DK_B, DV_B)
    pad_row = lambda p: jnp.pad(p.reshape(1, -1).astype(F32), ((0, 0), (0, LANES - p.size)))
    in_specs = [blk(5), blk(6), blk(7), blk(8),
                pl.BlockSpec((rows, LANES), lambda b: (rb0 + b, 0)),
                pl.BlockSpec((SHORT_CONV, 3 * wq), const2),
                pl.BlockSpec((1, LANES), const2), pl.BlockSpec((1, LANES), const2), pl.BlockSpec((1, DV_B), const2)]
    in_specs += [pl.BlockSpec(m.shape, const3) for m in consts]
    args = [proj] * 4 + [gates, conv_w.reshape(SHORT_CONV, 3 * wq), pad_row(a_log), pad_row(dt_bias),
                         gn.reshape(1, DV_B)] + list(consts)
    if s0 is not None:
        in_specs.append(pl.BlockSpec(st_block, lambda b: (b, layer, 0, 0, 0, 0)))
        args.append(s0)
    out_specs = [pl.BlockSpec((rows, wq), lambda b: (b, 0))]
    out_shape = [jax.ShapeDtypeStruct((nb * seq, wq), F32)]
    if prompt:
        out_specs.append(pl.BlockSpec(st_block, lambda b: (b, 0, 0, 0, 0, 0)))
        out_shape.append(jax.ShapeDtypeStruct((nb, 1, 2, H_B, DK_B, DV_B), F32))
    n_dh = 2 * H_B
    scratch = ([pltpu.VMEM((rows, wq), F32)] * 3
               + [pltpu.VMEM((n_dh, n_chunks, CHUNK, DV_B), F32)]
               + [pltpu.VMEM((n_dh, n_chunks, CHUNK, DK_B), BF16)] * 2
               + [pltpu.VMEM((n_dh, rows // GBLK, DK_B, GBLK), BF16),
                  pltpu.VMEM((n_dh, n_chunks, CHUNK, GBLK), BF16),
                  pltpu.VMEM((n_dh, n_chunks, SUBLANES, DV_B), F32),
                  pltpu.VMEM((n_seq * n_dh, DK_B, DV_B), F32),
                  pltpu.VMEM((rows, wq), F32)])
    return pl.pallas_call(
        functools.partial(_gdn_kernel, seq=seq, n_seq=n_seq, has_s0=s0 is not None, emit_state=prompt),
        grid=(nb // n_seq,),
        in_specs=in_specs,
        out_specs=out_specs,
        out_shape=out_shape,
        scratch_shapes=scratch,
        compiler_params=_cparams("arbitrary"),
        name="gdn_prompt" if prompt else "gdn_sample",
    )(*args)


def kernel(x_prompt, x_sample, state_hgrn, state_gdn, cache_na_k, cache_na_v, c, c_ctx, ada_w, ada_b, norm_g, w_in_ab, w_out_ab, hgrn_lb, gdn_conv, gdn_a_log, gdn_dt_bias, gn_hgrn, gn_gdn, w_qkv_na, qn_na, kn_na, rpb_na, w_out_na, w_mlp1, w_mlp2):
    cond = jnp.concatenate([c_ctx[None, :], c, jnp.zeros((N_MOD_ROWS - 1 - DEC_BATCH, D_MODEL), F32)], axis=0)
    mods, w_in_t = _modulation(cond, ada_w, ada_b, jnp.swapaxes(w_in_ab, 1, 2))
    xs = (x_prompt.reshape(N_PROMPT, D_MODEL), x_sample.reshape(N_SAMPLE, D_MODEL))

    w_gate_t = jnp.pad(w_in_t[D_MAIN_AB:], ((0, LANES - N_GATE_AB), (0, 0)))
    proj, gates, wo0, w1_0, w2_0 = _norm_proj(xs, mods[0], norm_g[0, 0], [w_in_t, w_gate_t], widths=[D_MAIN_AB, LANES],
                                              side=((w_out_ab, 0), (w_mlp1, 0), (w_mlp2, 0)), w_transposed=True)
    hg_consts = _hgrn_consts()
    gd_consts = _gdn_consts()
    hg_prompt, new_hgrn = _hgrn(proj, hgrn_lb, gn_hgrn[0], hg_consts, True)
    hg_sample, = _hgrn(proj, hgrn_lb, gn_hgrn[0], hg_consts, False, s0=state_hgrn)
    gd_args = (gdn_conv[0], gdn_a_log[0], gdn_dt_bias[0], gn_gdn[0], gd_consts)
    gd_prompt, new_gdn = _gdn(proj, gates, *gd_args, True)
    gd_sample, = _gdn(proj, gates, *gd_args, False, s0=state_gdn)
    xs, (w_qkv, wo1, w1_1, w2_1) = _post_mixer(xs, [(hg_prompt, hg_sample), (gd_prompt, gd_sample)], mods[0],
                                               norm_g[0, 1], wo0, w1_0, w2_0,
                                               side=((w_qkv_na, 0), (w_out_na, 0), (w_mlp1, 1), (w_mlp2, 1)))

    qkv, = _norm_proj(xs, mods[1], norm_g[1, 0], [w_qkv], tm=1024)
    at_prompt, new_kt, new_vt = _ctx_attention(qkv, qn_na[0], kn_na[0])
    time_minor = lambda a: jnp.swapaxes(a, -1, -2).reshape(a.shape[0], 1, H_C * HD_C, a.shape[3])
    at_sample = _na_attention(qkv, time_minor(cache_na_k), time_minor(cache_na_v), qn_na[0], kn_na[0], rpb_na[0])
    time_major = lambda a: jnp.swapaxes(a.reshape(BATCH, 1, H_C, HD_C, SEQ), -1, -2)
    new_k, new_v = time_major(new_kt), time_major(new_vt)
    (y_prompt, y_sample), _ = _post_mixer(xs, [(at_prompt, at_sample)], mods[1], norm_g[1, 1], wo1, w1_1, w2_1,
                                          split_out=True)

    return (y_prompt.reshape(BATCH, SEQ, D_MODEL), y_sample.reshape(DEC_BATCH, DEC_SEQ, D_MODEL),
            new_hgrn, new_gdn, new_k, new_v)
```

```python
import functools

import numpy as np
import jax
import jax.numpy as jnp
from jax import lax
from jax.experimental import pallas as pl
from jax.experimental.pallas import tpu as pltpu

F32 = jnp.float32
BF16 = jnp.bfloat16

D_MODEL = 1024
BATCH = 16
SEQ = 256
DEC_BATCH = 4
DEC_SEQ = 1024
PAST_LEN = 256
N_PROMPT = BATCH * SEQ
N_SAMPLE = DEC_BATCH * DEC_SEQ
N_TOK = N_PROMPT + N_SAMPLE
GRID_W = 64
GRID_ROWS = DEC_SEQ // GRID_W
H_A = 4
DK_A = 128
DV_A = 128
H_B = 4
DK_B = 128
DV_B = 128
SHORT_CONV = 5
H_C = 16
HD_C = 64
KH = 8
KW = 16
D_FF = 4 * D_MODEL
EPS = 1e-6
NEG_INF = -1e30
LANES = 128
SUBLANES = 8
BF16_ROWS = 16
VMEM_LIMIT = 56 * 1024 * 1024

N_MOD_ROWS = SUBLANES
N_GATE_AB = 16
D_MAIN_AB = 3 * H_A * DK_A + 2 * H_A * DV_A + H_B * (2 * DK_B + DV_B) + H_B * DV_B
CHUNK = 32
GBLK = LANES


def _cparams(*sem):
    return pltpu.CompilerParams(dimension_semantics=sem, vmem_limit_bytes=VMEM_LIMIT)


def _sigmoid(x):
    return 0.5 * jnp.tanh(0.5 * x) + 0.5


def _silu(x):
    return x * _sigmoid(x)


def _dot(a, b):
    return jnp.dot(a, b, preferred_element_type=F32)


def _dot_nt(a, b):
    return lax.dot_general(a, b, (((1,), (1,)), ((), ())), preferred_element_type=F32)


def _dot_tn(a, b):
    return lax.dot_general(a, b, (((0,), (0,)), ((), ())), preferred_element_type=F32)


def _split2(x):
    hi = x.astype(BF16)
    lo = (x - hi.astype(F32)).astype(BF16)
    return hi, lo


def _dot_const(m2, x):
    hi, lo = _split2(x)
    return _dot(m2, jnp.concatenate([hi, lo], axis=0))


def _mod_row(i, tm):
    start = i * tm
    return jnp.where(start < N_PROMPT, 0, 1 + (start - N_PROMPT) // DEC_SEQ)


def _mod_slice(mod_ref, row, k):
    return mod_ref[pl.ds(row, 1), k * D_MODEL:(k + 1) * D_MODEL]


def _norm_mod(x, g, sc, sh):
    ms = jnp.mean(x * x, axis=-1, keepdims=True)
    return (x * lax.rsqrt(ms + EPS) * g) * (1.0 + sc) + sh


def _mod_kernel(cond_ref, w_ref, b_ref, side_ref, o_ref, side_o_ref):
    s = _silu(cond_ref[...]).astype(BF16)
    o_ref[0] = _dot(s, w_ref[0].astype(BF16)) + b_ref[0]
    side_o_ref[...] = side_ref[...].astype(BF16)


MOD_TN = 768


def _modulation(cond8, ada_w, ada_b, side):
    depth = ada_w.shape[0]
    nj = ada_w.shape[2] // MOD_TN
    slab = pl.cdiv(side.shape[1], depth * nj * BF16_ROWS) * BF16_ROWS
    return pl.pallas_call(
        _mod_kernel,
        grid=(depth, nj),
        in_specs=[
            pl.BlockSpec((N_MOD_ROWS, D_MODEL), lambda l, j: (0, 0)),
            pl.BlockSpec((1, D_MODEL, MOD_TN), lambda l, j: (l, 0, j)),
            pl.BlockSpec((1, 1, MOD_TN), lambda l, j: (l, 0, j)),
            pl.BlockSpec((None, slab, side.shape[2]), lambda l, j: (0, l * nj + j, 0)),
        ],
        out_specs=[pl.BlockSpec((1, N_MOD_ROWS, MOD_TN), lambda l, j: (l, 0, j)),
                   pl.BlockSpec((slab, side.shape[2]), lambda l, j: (l * nj + j, 0))],
        out_shape=[jax.ShapeDtypeStruct((depth, N_MOD_ROWS, ada_w.shape[2]), F32),
                   jax.ShapeDtypeStruct(side.shape[1:], BF16)],
        compiler_params=_cparams("arbitrary", "arbitrary"),
        name="modulation",
    )(cond8, ada_w, ada_b.reshape(depth, 1, -1), side)


def _stream_specs(n_arrays, tm, width=D_MODEL):
    if n_arrays == 1:
        return [pl.BlockSpec((tm, width), lambda i: (i, 0))]
    npt = N_PROMPT // tm
    return [pl.BlockSpec((tm, width), lambda i: (jnp.minimum(i, npt - 1), 0)),
            pl.BlockSpec((tm, width), lambda i: (jnp.maximum(i - npt, 0), 0))]


def _stream_load(x_refs, tm):
    if len(x_refs) == 1:
        return x_refs[0][...]
    return jnp.where(pl.program_id(0) < N_PROMPT // tm, x_refs[0][...], x_refs[1][...])


def _side_specs(side, n_steps):
    in_specs = [pl.BlockSpec((None, w.shape[1] // n_steps, w.shape[2]), lambda i, l=l: (l, i, 0)) for w, l in side]
    out_specs = [pl.BlockSpec((w.shape[1] // n_steps, w.shape[2]), lambda i: (i, 0)) for w, _ in side]
    shapes = [jax.ShapeDtypeStruct(w.shape[1:], BF16) for w, _ in side]
    return in_specs, out_specs, shapes


def _side_cast(in_refs, out_refs):
    for i_ref, o_ref in zip(in_refs, out_refs):
        o_ref[...] = i_ref[...].astype(BF16)


def _norm_proj_kernel(*refs, tm, n_x, n_w, n_side, w_transposed):
    x_refs, (mod_ref, g_ref) = refs[:n_x], refs[n_x:n_x + 2]
    w_refs = refs[n_x + 2:n_x + 2 + n_w]
    side_in = refs[n_x + 2 + n_w:n_x + 2 + n_w + n_side]
    o_refs = refs[n_x + 2 + n_w + n_side:n_x + 2 + 2 * n_w + n_side]
    side_out = refs[n_x + 2 + 2 * n_w + n_side:]
    row = _mod_row(pl.program_id(0), tm)
    h = _norm_mod(_stream_load(x_refs, tm), g_ref[...], _mod_slice(mod_ref, row, 1), _mod_slice(mod_ref, row, 0)).astype(BF16)
    for w_ref, o_ref in zip(w_refs, o_refs):
        o_ref[...] = _dot_nt(h, w_ref[...]) if w_transposed else _dot(h, w_ref[...])
    _side_cast(side_in, side_out)


def _norm_proj(xs, mod, g, ws, widths=None, side=(), w_transposed=False, tm=512):
    n_w = len(ws)
    widths = widths or [w.shape[0 if w_transposed else 1] for w in ws]
    const = lambda i: (0, 0)
    w_block = (lambda n: (n, D_MODEL)) if w_transposed else (lambda n: (D_MODEL, n))
    side_in_specs, side_out_specs, side_shapes = _side_specs(side, N_TOK // tm)
    return pl.pallas_call(
        functools.partial(_norm_proj_kernel, tm=tm, n_x=len(xs), n_w=n_w, n_side=len(side), w_transposed=w_transposed),
        grid=(N_TOK // tm,),
        in_specs=_stream_specs(len(xs), tm) + [
            pl.BlockSpec(mod.shape, const),
            pl.BlockSpec((1, D_MODEL), const),
        ] + [pl.BlockSpec(w_block(n), const, pipeline_mode=pl.Buffered(1)) for n in widths] + side_in_specs,
        out_specs=[pl.BlockSpec((tm, n), lambda i: (i, 0)) for n in widths] + side_out_specs,
        out_shape=[jax.ShapeDtypeStruct((N_TOK, n), F32) for n in widths] + side_shapes,
        compiler_params=_cparams("arbitrary"),
        name="norm_proj",
    )(*xs, mod, g.reshape(1, D_MODEL), *ws, *[w for w, _ in side])


def _post_kernel(*refs, tm, ff_chunk, n_x, n_y, groups, n_side):
    x_refs = refs[:n_x]
    n_m = sum(n for n, _ in groups)
    m_refs = refs[n_x:n_x + n_m]
    mod_ref, g_ref, wo_ref, w1_ref, w2_ref = refs[n_x + n_m:n_x + n_m + 5]
    side_in = refs[n_x + n_m + 5:n_x + n_m + 5 + n_side]
    y_refs = refs[n_x + n_m + 5 + n_side:n_x + n_m + 5 + n_side + n_y]
    _side_cast(side_in, refs[n_x + n_m + 5 + n_side + n_y:])
    row = _mod_row(pl.program_id(0), tm)
    mix, first_ref, first_col = None, 0, 0
    for n, width in groups:
        part = _stream_load(m_refs[first_ref:first_ref + n], tm).astype(BF16)
        term = _dot(part, wo_ref[first_col:first_col + width, :])
        mix = term if mix is None else mix + term
        first_ref, first_col = first_ref + n, first_col + width
    x1 = _stream_load(x_refs, tm) + _mod_slice(mod_ref, row, 2) * mix
    h = _norm_mod(x1, g_ref[...], _mod_slice(mod_ref, row, 4), _mod_slice(mod_ref, row, 3)).astype(BF16)
    acc = jnp.zeros((tm, D_MODEL), F32)
    for k in range(0, D_FF, ff_chunk):
        a = jnp.maximum(_dot(h, w1_ref[:, k:k + ff_chunk]), 0.0)
        acc = acc + _dot((a * a).astype(BF16), w2_ref[k:k + ff_chunk, :])
    y = x1 + _mod_slice(mod_ref, row, 5) * acc
    if n_y == 1:
        y_refs[0][...] = y
    else:
        is_prompt = pl.program_id(0) < N_PROMPT // tm

        @pl.when(is_prompt)
        def _():
            y_refs[0][...] = y

        @pl.when(jnp.logical_not(is_prompt))
        def _():
            y_refs[1][...] = y


def _post_mixer(xs, mixed, mod, g, wo, w1, w2, split_out=False, side=(), tm=512, ff_chunk=1024):
    const = lambda i: (0, 0)
    resident = lambda w: pl.BlockSpec(w.shape, const, pipeline_mode=pl.Buffered(1))
    n_y = 2 if split_out else 1
    rows = (N_PROMPT, N_SAMPLE) if split_out else (N_TOK,)
    groups = tuple((len(grp), grp[0].shape[1]) for grp in mixed)
    mixed_specs = [spec for n, width in groups for spec in _stream_specs(n, tm, width)]
    side_in_specs, side_out_specs, side_shapes = _side_specs(side, N_TOK // tm)
    out = pl.pallas_call(
        functools.partial(_post_kernel, tm=tm, ff_chunk=ff_chunk, n_x=len(xs), n_y=n_y, groups=groups,
                          n_side=len(side)),
        grid=(N_TOK // tm,),
        in_specs=_stream_specs(len(xs), tm) + mixed_specs + [
            pl.BlockSpec(mod.shape, const),
            pl.BlockSpec((1, D_MODEL), const),
            resident(wo), resident(w1), resident(w2),
        ] + side_in_specs,
        out_specs=_stream_specs(n_y, tm) + side_out_specs,
        out_shape=[jax.ShapeDtypeStruct((r, D_MODEL), F32) for r in rows] + side_shapes,
        compiler_params=_cparams("arbitrary"),
        name="post_mixer",
    )(*xs, *[a for grp in mixed for a in grp], mod, g.reshape(1, D_MODEL), wo, w1, w2, *[w for w, _ in side])
    return tuple(out[:n_y]), tuple(out[n_y:])


PAIR = 2 * HD_C


def _pair_consts():
    lane = lax.broadcasted_iota(jnp.int32, (1, PAIR), 1)
    first = lane < HD_C
    ones_col = [jnp.where(lane == HD_C, 1.0, 0.0), jnp.where(lane == 0, 1.0, 0.0)]
    r = lax.broadcasted_iota(jnp.int32, (2 * PAIR, PAIR), 0) % PAIR
    cidx = lax.broadcasted_iota(jnp.int32, (2 * PAIR, PAIR), 1)
    mean2 = jnp.where(r // HD_C == cidx // HD_C, 1.0 / HD_C, 0.0).astype(BF16)
    return first, ones_col, mean2


def _pair_norm(x, w2, mean2):
    hi, lo = _split2(x * x)
    ms = _dot(jnp.concatenate([hi, lo], axis=1), mean2)
    return x * lax.rsqrt(ms + EPS) * w2


def _pair_queries(q, first):
    return [jnp.where(first, q, 0.0).astype(BF16), jnp.where(first, 0.0, q).astype(BF16)]


def _pair_values(v, first, ones_col):
    return [jnp.where(first, v, ones_col[0]).astype(BF16), jnp.where(first, ones_col[1], v).astype(BF16)]


def _pair_output(o_aug, first):
    den = [o_aug[0][:, HD_C:HD_C + 1], o_aug[1][:, 0:1]]
    return jnp.where(first, o_aug[0] / den[0], o_aug[1] / den[1])


def _row_max(*pieces):
    tiles = [p[:, i:i + LANES] for p in pieces for i in range(0, p.shape[1], LANES)]
    return jnp.max(functools.reduce(jnp.maximum, tiles), axis=-1, keepdims=True)


CTX_PAIRS = 8


def _ctx_attn_kernel(q_ref, k_ref, v_ref, qn_ref, kn_ref, o_ref, kc_ref, vc_ref):
    first, ones_col, mean2 = _pair_consts()
    lanes = [slice(p * PAIR, (p + 1) * PAIR) for p in range(CTX_PAIRS)]
    qn = [_pair_norm(q_ref[:, ln], qn_ref[...], mean2) * HD_C ** -0.5 for ln in lanes]
    kn = [_pair_norm(k_ref[:, ln], kn_ref[...], mean2) for ln in lanes]
    v = [v_ref[:, ln] for ln in lanes]
    kt = [x.T for x in kn]
    for p in range(CTX_PAIRS):
        kc_ref[0, 0, lanes[p], :] = kt[p]
        vc_ref[0, 0, lanes[p], :] = v[p].T
    q = [_pair_queries(x, first) for x in qn]
    va = [_pair_values(x, first, ones_col) for x in v]
    s = [[_dot(q[p][j], kt[p].astype(BF16)) for j in range(2)] for p in range(CTX_PAIRS)]
    pr = [[jnp.exp(x - _row_max(x)).astype(BF16) for x in sp] for sp in s]
    for p in range(CTX_PAIRS):
        o_ref[:, lanes[p]] = _pair_output([_dot(pr[p][j], va[p][j]) for j in range(2)], first)


def _ctx_attention(qkv, qn, kn):
    heads = 2 * CTX_PAIRS
    ng = H_C // heads
    wide = CTX_PAIRS * PAIR
    blk = lambda off: pl.BlockSpec((SEQ, wide), lambda b, p: (b, off + p))
    cache_spec = pl.BlockSpec((1, 1, wide, SEQ), lambda b, p: (b, 0, p, 0))
    cache_shape = jax.ShapeDtypeStruct((BATCH, 1, H_C * HD_C, SEQ), F32)
    return pl.pallas_call(
        _ctx_attn_kernel,
        grid=(BATCH, ng),
        in_specs=[blk(0), blk(ng), blk(2 * ng),
                  pl.BlockSpec((1, PAIR), lambda b, p: (0, 0)),
                  pl.BlockSpec((1, PAIR), lambda b, p: (0, 0))],
        out_specs=[pl.BlockSpec((SEQ, wide), lambda b, p: (b, p)), cache_spec, cache_spec],
        out_shape=[jax.ShapeDtypeStruct((N_PROMPT, D_MODEL), F32), cache_shape, cache_shape],
        compiler_params=_cparams("arbitrary", "arbitrary"),
        name="ctx_attention",
    )(qkv, qkv, qkv, jnp.tile(qn.reshape(1, HD_C), (1, 2)), jnp.tile(kn.reshape(1, HD_C), (1, 2)))


def _na_row_start(r):
    return min(max(r - KH // 2, 0), GRID_ROWS - KH)


NA_ROW_GROUP = 16


def _na_attn_kernel(q_ref, k_ref, v_ref, kc_ref, vc_ref, qn_ref, kn_ref, bias_ref, o_ref, qs, ks, vs, bias_s):
    first, ones_col, mean2 = _pair_consts()

    @pl.when(pl.program_id(1) == 0)
    def _():
        q_col = lax.broadcasted_iota(jnp.int32, (GRID_W, PAIR), 0)
        lane = lax.broadcasted_iota(jnp.int32, (GRID_W, PAIR), 1)
        k_col = lane % GRID_W
        w0 = jnp.clip(q_col - KW // 2, 0, GRID_W - KW)
        outside = jnp.where((k_col >= w0) & (k_col < w0 + KW), 0.0, NEG_INF)
        n_dr = 2 * KH - 1
        for j in range(2):
            band = []
            for dr in range(n_dr):
                row = jnp.broadcast_to(bias_ref[j, dr:dr + 1, :], (GRID_W, PAIR))
                band.append([pltpu.roll(row, (half * GRID_W - (KW - 1)) % PAIR, axis=1, stride=1, stride_axis=0)
                             for half in range(2)])
            zero = jnp.zeros((GRID_W, PAIR), F32)
            for cp in range(2):
                for t in range(KH):
                    lo, hi = 2 * t + cp, 2 * t + cp + 1
                    tile = jnp.where(lane < GRID_W, band[lo][0] if lo < n_dr else zero,
                                     band[hi][1] if hi < n_dr else zero)
                    bias_s[j, cp, :, t * PAIR:(t + 1) * PAIR] = tile + outside

    q2 = _pair_queries(_pair_norm(q_ref[...], qn_ref[...], mean2) * HD_C ** -0.5, first)
    v2 = _pair_values(v_ref[...], first, ones_col)
    ks[...] = _pair_norm(k_ref[...], kn_ref[...], mean2).astype(BF16)
    for j in range(2):
        qs[:, j] = q2[j].reshape(GRID_ROWS, GRID_W, PAIR)
        vs[j] = v2[j]
    kt_ctx = kc_ref[0, 0].astype(BF16)
    vt = vc_ref[0, 0]
    ch = lax.broadcasted_iota(jnp.int32, vt.shape, 0)
    vt_ctx = [jnp.where(ch < HD_C, vt, jnp.where(ch == HD_C, 1.0, 0.0)).astype(BF16),
              jnp.where(ch < HD_C, jnp.where(ch == 0, 1.0, 0.0), vt).astype(BF16)]
    for r0 in range(0, GRID_ROWS, NA_ROW_GROUP):
        units = [(r, j) for r in range(r0, r0 + NA_ROW_GROUP) for j in range(2)]
        rows = {r: slice(r * GRID_W, (r + 1) * GRID_W) for r, _ in units}
        wins = {r: slice(_na_row_start(r) * GRID_W, (_na_row_start(r) + KH) * GRID_W) for r, _ in units}
        unit_rows = lambda k: slice(k * GRID_W, (k + 1) * GRID_W)
        s_ctx_all = _dot(qs[r0:r0 + NA_ROW_GROUP].reshape(NA_ROW_GROUP * PAIR, PAIR), kt_ctx)
        s_ctx = [s_ctx_all[unit_rows(2 * (r - r0) + j)] for r, j in units]
        s_row = {r: _dot_nt(qs[r].reshape(PAIR, PAIR), ks[wins[r], :]) for r in rows}
        s_win = []
        for r, j in units:
            dr0 = KH - 1 - (r - _na_row_start(r))
            lane0 = (dr0 - dr0 % 2) * GRID_W
            s_win.append(s_row[r][unit_rows(j)] + bias_s[j, dr0 % 2, :, lane0:lane0 + KH * GRID_W])
        m = [_row_max(a, b) for a, b in zip(s_win, s_ctx)]
        p_win = [jnp.exp(a - mm).astype(BF16) for a, mm in zip(s_win, m)]
        p_ctx = [jnp.exp(b - mm).astype(BF16) for b, mm in zip(s_ctx, m)]
        o_aug = [_dot(p_win[i], vs[j, wins[r], :]) + _dot_nt(p_ctx[i], vt_ctx[j]) for i, (r, j) in enumerate(units)]
        for i in range(0, len(units), 2):
            o_ref[rows[units[i][0]], :] = _pair_output(o_aug[i:i + 2], first)


NA_BIAS_LANES = 2 * KH * GRID_W


def _na_attention(qkv, cache_kt, cache_vt, qn, kn, rpb):
    nhp = H_C // 2
    row0 = N_PROMPT // DEC_SEQ
    blk = lambda off: pl.BlockSpec((DEC_SEQ, 2 * HD_C), lambda p, b: (row0 + b, off + p))
    cache_spec = pl.BlockSpec((1, 1, PAIR, PAST_LEN), lambda p, b: (b, 0, p, 0))
    rpb_rows = 2 * KH
    bias = jnp.pad(rpb.astype(F32), ((0, 0), (0, rpb_rows - rpb.shape[1]), (0, PAIR - rpb.shape[2])))
    return pl.pallas_call(
        _na_attn_kernel,
        grid=(nhp, DEC_BATCH),
        in_specs=[blk(0), blk(nhp), blk(2 * nhp), cache_spec, cache_spec,
                  pl.BlockSpec((1, PAIR), lambda p, b: (0, 0)),
                  pl.BlockSpec((1, PAIR), lambda p, b: (0, 0)),
                  pl.BlockSpec((2, rpb_rows, PAIR), lambda p, b: (p, 0, 0))],
        out_specs=pl.BlockSpec((DEC_SEQ, 2 * HD_C), lambda p, b: (b, p)),
        out_shape=jax.ShapeDtypeStruct((N_SAMPLE, D_MODEL), F32),
        scratch_shapes=[pltpu.VMEM((GRID_ROWS, 2, GRID_W, PAIR), BF16), pltpu.VMEM((DEC_SEQ, PAIR), BF16),
                        pltpu.VMEM((2, DEC_SEQ, PAIR), BF16), pltpu.VMEM((2, 2, GRID_W, NA_BIAS_LANES), F32)],
        compiler_params=_cparams("arbitrary", "arbitrary"),
        name="na_attention",
    )(qkv, qkv, qkv, cache_kt, cache_vt, jnp.tile(qn.reshape(1, HD_C), (1, 2)), jnp.tile(kn.reshape(1, HD_C), (1, 2)),
      bias)


def _seq_layout(prompt):
    return (SEQ, BATCH, 0) if prompt else (DEC_SEQ, DEC_BATCH, N_PROMPT // DEC_SEQ)


def _flip_blocks(m, c):
    r, s = m.shape
    return m.reshape(r // c, c, s // c, c)[:, ::-1, :, ::-1].reshape(r, s)


def _rms_gate(x, gn, gate):
    ms = jnp.mean(x * x, axis=-1, keepdims=True)
    return x * lax.rsqrt(ms + EPS) * gn * _silu(gate)


HG_LEVELS = tuple(CHUNK >> (i + 1) for i in range(CHUNK.bit_length() - 1))
HG_NL = len(HG_LEVELS)
HG_STACK = (HG_NL + 1) * CHUNK
TOT_ROWS = BF16_ROWS
HG_ROWS = (HG_NL + 2) * CHUNK + TOT_ROWS


def _hgrn_consts():
    c = CHUNK
    level_rows = []
    mask = np.zeros((HG_STACK, HG_STACK), np.float32)
    mask[:c, :c] = np.eye(c)
    for li, b in enumerate(HG_LEVELS):
        m = np.zeros((c, c), np.float32)
        blk = np.zeros((c, c), np.float32)
        for t in range(c):
            mid = (t // (2 * b)) * 2 * b + b
            if t >= mid:
                m[t, mid:t + 1] = 1.0
                blk[t, mid - b:mid] = 1.0
            else:
                m[t, t + 1:mid] = 1.0
        level_rows.append(m)
        mask[(li + 1) * c:(li + 2) * c, (li + 1) * c:(li + 2) * c] = blk
    dq = np.tril(np.ones((c, c), np.float32))
    dk = np.triu(np.ones((c, c), np.float32), 1)
    body = np.concatenate(level_rows + [dq, dk], axis=0)
    tot = np.ones((TOT_ROWS, c), np.float32)
    mcs, masks = [], []
    for reverse in (False, True):
        bm = _flip_blocks(body, c) if reverse else body
        mk = _flip_blocks(mask, c) if reverse else mask
        mc = np.concatenate([bm, tot], axis=0)
        mcs.append(np.concatenate([mc, mc], axis=1))
        masks.append(mk)
    return jnp.asarray(np.stack(mcs), BF16), jnp.asarray(np.stack(masks), F32)


HG_FAST = 64
HG_HALF = HG_FAST // 2
HG_FAST_ROWS = HG_FAST + TOT_ROWS
HG_SAFE_EXP = 40.0
HG_FAST_STEPS = 4


def _hgrn_fast_consts(seq):
    c, m = HG_FAST, HG_HALF
    body = np.tril(np.ones((c, c), np.float32))
    tot = np.ones((TOT_ROWS, c), np.float32)
    causal = np.tril(np.ones((c, c), np.float32))
    mfs, masks = [], []
    for reverse in (False, True):
        bm = _flip_blocks(body, c) if reverse else body
        mf = np.concatenate([bm, tot], axis=0)
        mfs.append(np.concatenate([mf, mf], axis=1))
        masks.append(causal.T if reverse else causal)
    n_half = seq // m
    half = np.zeros((max(n_half, BF16_ROWS), seq), np.float32)
    for i in range(n_half):
        half[i, i * m:(i + 1) * m] = 1.0
    return jnp.asarray(np.stack(mfs), BF16), jnp.asarray(np.stack(masks), F32), jnp.asarray(half, BF16)


def _hgrn_kernel(*refs, seq, has_s0, emit_state):
    it = iter(refs)
    qa_ref, ff_ref, fb_ref, ia_ref, ga_ref, lb_ref, gn_ref, mc_ref, mask_ref = [next(it) for _ in range(9)]
    mf_ref, causal_ref, half_ref = [next(it) for _ in range(3)]
    s0_ref = next(it) if has_s0 else None
    o_ref = next(it)
    st_ref = next(it) if emit_state else None
    s_scr, acc, f_s, lf_s = [next(it) for _ in range(4)]
    c = CHUNK
    n_chunks = seq // c
    combos = [(d, h) for d in range(2) for h in range(H_A)]
    lanes = [slice(h * DK_A, (h + 1) * DK_A) for h in range(H_A)]
    add = lambda a, b: a + b

    lb_raw = lb_ref[...]
    lb_e = jnp.exp(lb_raw - jnp.max(lb_raw, axis=0, keepdims=True))
    lb_all = lb_e[0:1] / jnp.sum(lb_e, axis=0, keepdims=True)

    for d in range(2):
        for h in range(H_A):
            s_scr[d, h] = s0_ref[0, 0, d, h].T if has_s0 else jnp.zeros((DV_A, DK_A), F32)
    acc[...] = jnp.zeros(acc.shape, F32)

    worst = []
    for d, fr_ref in enumerate((ff_ref, fb_ref)):
        f = lb_all + (1.0 - lb_all) * _sigmoid(fr_ref[...])
        lf = jnp.log(f)
        f_s[d] = f
        lf_s[d] = lf
        worst.append(jnp.max(_dot(half_ref[...], (-lf).astype(BF16))))
    safe = jnp.maximum(worst[0], worst[1]) <= HG_SAFE_EXP

    def fast_body(n, carry):
        cf = HG_FAST
        n_fast = seq // cf
        steps = range(HG_FAST_STEPS)
        chunk = lambda d, t: (n * HG_FAST_STEPS + t) if d == 0 else (n_fast - 1 - n * HG_FAST_STEPS - t)
        rows = [[pl.ds(pl.multiple_of(chunk(d, t) * cf, cf), cf) for t in steps] for d in range(2)]
        units = [(t, d, h) for t in steps for d, h in combos]
        def factors(d, t):
            g_all = _dot_const(mf_ref[d], lf_s[d, rows[d][t], :])
            g, tot = g_all[:cf], g_all[cf:cf + 1]
            ref = HG_HALF - 1 if d == 0 else HG_HALF
            rel = g - g[ref:ref + 1]
            return [jnp.exp(rel), jnp.exp(-rel), jnp.exp(g), jnp.exp(tot - g), jnp.exp(tot)]

        e_all = [[factors(d, t) for t in steps] for d in range(2)]
        q_all = [[_silu(qa_ref[rows[d][t], :]) * DK_A ** -0.5 for t in steps] for d in range(2)]
        k_all = [[1.0 - f_s[d, rows[d][t], :] for t in steps] for d in range(2)]
        v_all = [[ia_ref[rows[d][t], :].astype(BF16) for t in steps] for d in range(2)]
        qs = {u: q_all[u[1]][u[0]][:, lanes[u[2]]] for u in units}
        ks = {u: k_all[u[1]][u[0]][:, lanes[u[2]]] for u in units}
        vs = {u: v_all[u[1]][u[0]][:, lanes[u[2]]] for u in units}
        es = {u: [e_all[u[1]][u[0]][i][:, lanes[u[2]]] for i in range(4)] for u in units}
        p = {u: jnp.where(causal_ref[u[1]] > 0.0,
                          _dot_nt((qs[u] * es[u][0]).astype(BF16), (ks[u] * es[u][1]).astype(BF16)), 0.0).astype(BF16)
             for u in units}
        intra = {u: _dot(p[u], vs[u]) for u in units}
        upd = {u: _dot_tn(vs[u], (ks[u] * es[u][3]).astype(BF16)) for u in units}
        qdec = {u: (qs[u] * es[u][2]).astype(BF16) for u in units}
        st = {(d, h): s_scr[d, h] for d, h in combos}
        o = {}
        for t in steps:
            for d, h in combos:
                u = (t, d, h)
                o[u] = intra[u] + _dot_nt(qdec[u], st[d, h].astype(BF16))
                st[d, h] = st[d, h] * e_all[d][t][4][:, lanes[h]] + upd[u]
        for t in steps:
            for d in range(2):
                acc[rows[d][t], :] += jnp.concatenate([o[t, d, h] for h in range(H_A)], axis=1)
        for d, h in combos:
            s_scr[d, h] = st[d, h]
        return carry

    def body(n, carry):
        rows = [pl.ds(pl.multiple_of((n if d == 0 else n_chunks - 1 - n) * c, c), c) for d in range(2)]
        f_all = [f_s[d, rows[d], :] for d in range(2)]
        e_all = [jnp.exp(_dot_const(mc_ref[d], lf_s[d, rows[d], :])) for d in range(2)]
        q_all = [_silu(qa_ref[rows[d], :]) * DK_A ** -0.5 for d in range(2)]
        v_all = [ia_ref[rows[d], :].astype(BF16) for d in range(2)]
        st = [s_scr[d, h] for d, h in combos]
        qs, ks, vs, es = [], [], [], []
        for d, h in combos:
            qs.append(q_all[d][:, lanes[h]])
            ks.append(1.0 - f_all[d][:, lanes[h]])
            vs.append(v_all[d][:, lanes[h]])
            es.append(e_all[d][:, lanes[h]])
        lvl = [[e[i * c:(i + 1) * c] for i in range(HG_NL + 2)] for e in es]
        qst = [jnp.concatenate([q] + [q * l[i] for i in range(HG_NL)], axis=0).astype(BF16) for q, l in zip(qs, lvl)]
        kst = [jnp.concatenate([k] + [k * l[i] for i in range(HG_NL)], axis=0).astype(BF16) for k, l in zip(ks, lvl)]
        r = [(_dot_nt(qst[i], kst[i]) * mask_ref[d]).astype(BF16) for i, (d, h) in enumerate(combos)]
        ost = [_dot(r[i], jnp.concatenate([vs[i]] * (HG_NL + 1), axis=0)) for i in range(len(combos))]
        inter = [_dot_nt((qs[i] * lvl[i][HG_NL]).astype(BF16), st[i].astype(BF16)) for i in range(len(combos))]
        upd = [_dot_tn(vs[i], (ks[i] * lvl[i][HG_NL + 1]).astype(BF16)) for i in range(len(combos))]
        o = [functools.reduce(lambda a, b: a + b, [ost[i][j * c:(j + 1) * c] for j in range(HG_NL + 1)]) + inter[i]
             for i in range(len(combos))]
        for d in range(2):
            acc[rows[d], :] += jnp.concatenate(o[d * H_A:(d + 1) * H_A], axis=1)
        for i, (d, h) in enumerate(combos):
            e_tot = es[i][(HG_NL + 2) * c:(HG_NL + 2) * c + 1]
            s_scr[d, h] = st[i] * e_tot + upd[i]
        return carry

    @pl.when(safe)
    def _():
        lax.fori_loop(0, seq // (HG_FAST * HG_FAST_STEPS), fast_body, 0)

    @pl.when(jnp.logical_not(safe))
    def _():
        lax.fori_loop(0, n_chunks, body, 0)

    for h in range(H_A):
        ln = slice(h * DV_A, (h + 1) * DV_A)
        o_ref[:, ln] = _rms_gate(acc[:, ln], gn_ref[...], ga_ref[:, ln])
    if emit_state:
        for d in range(2):
            for h in range(H_A):
                st_ref[0, 0, d, h] = s_scr[d, h].T


def _hgrn(proj, hgrn_lb, gn, consts, prompt, s0=None, layer=0):
    seq, nb, rb0 = _seq_layout(prompt)
    consts = list(consts) + list(_hgrn_fast_consts(seq))
    wa = H_A * DK_A
    blk = lambda j: pl.BlockSpec((seq, wa), lambda b: (rb0 + b, j))
    const2 = lambda b: (0, 0)
    st_block = (1, 1, 2, H_A, DK_A, DV_A)
    in_specs = [blk(0), blk(1), blk(2), blk(3), blk(4),
                pl.BlockSpec(hgrn_lb.shape, const2), pl.BlockSpec((1, DV_A), const2)]
    in_specs += [pl.BlockSpec(m.shape, lambda b, nd=m.ndim: (0,) * nd) for m in consts]
    args = [proj] * 5 + [hgrn_lb, gn.reshape(1, DV_A)] + consts
    if s0 is not None:
        in_specs.append(pl.BlockSpec(st_block, lambda b: (b, layer, 0, 0, 0, 0)))
        args.append(s0)
    out_specs = [pl.BlockSpec((seq, wa), lambda b: (b, 0))]
    out_shape = [jax.ShapeDtypeStruct((nb * seq, wa), F32)]
    if prompt:
        out_specs.append(pl.BlockSpec(st_block, lambda b: (b, 0, 0, 0, 0, 0)))
        out_shape.append(jax.ShapeDtypeStruct((nb, 1, 2, H_A, DK_A, DV_A), F32))
    return pl.pallas_call(
        functools.partial(_hgrn_kernel, seq=seq, has_s0=s0 is not None, emit_state=prompt),
        grid=(nb,),
        in_specs=in_specs,
        out_specs=out_specs,
        out_shape=out_shape,
        scratch_shapes=[pltpu.VMEM((2, H_A, DV_A, DK_A), F32), pltpu.VMEM((seq, wa), F32),
                        pltpu.VMEM((2, seq, wa), F32), pltpu.VMEM((2, seq, wa), F32)],
        compiler_params=_cparams("arbitrary"),
        name="hgrn_prompt" if prompt else "hgrn_sample",
    )(*args)


GD_SUB = GBLK // CHUNK
GD_BLOCKS_PER_ITER = 2
GD_PROMPT_SEQS = 2
GD_ROWS = 2 * GBLK + TOT_ROWS


def _gdn_consts():
    n, c = GBLK, CHUNK
    same = (np.arange(n)[:, None] // c) == (np.arange(n)[None, :] // c)
    tri = (same & (np.arange(n)[None, :] <= np.arange(n)[:, None])).astype(np.float32)
    sup = (same & (np.arange(n)[None, :] > np.arange(n)[:, None])).astype(np.float32)
    tot = np.zeros((TOT_ROWS, n), np.float32)
    for s in range(GD_SUB):
        tot[s, s * c:(s + 1) * c] = 1.0
    mgs, tts, tris = [], [], []
    for reverse in (False, True):
        t = _flip_blocks(tri, c) if reverse else tri
        s = _flip_blocks(sup, c) if reverse else sup
        mg = np.concatenate([t, s, tot], axis=0)
        mgs.append(np.concatenate([mg, mg], axis=1))
        tts.append(np.concatenate([t.T, t.T], axis=0))
        tris.append(t)
    tris.append(same.astype(np.float32))
    return jnp.asarray(np.stack(mgs), BF16), jnp.asarray(np.stack(tts), BF16), jnp.asarray(np.stack(tris), F32)


def _softplus(x):
    return jnp.maximum(x, 0.0) + jnp.log(1.0 + jnp.exp(-jnp.abs(x)))


CONV_PAD = SUBLANES


def _conv_silu(x, w, seq):
    half = SHORT_CONV // 2
    pad = jnp.zeros((CONV_PAD, x.shape[1]), x.dtype)
    xe = jnp.concatenate([pad, x, pad], axis=0)
    acc = xe * w[half:half + 1]
    for j in range(SHORT_CONV):
        shift = half - j
        if shift != 0:
            acc = acc + pltpu.roll(xe, shift % (seq + 2 * CONV_PAD), axis=0) * w[j:j + 1]
    return _silu(acc[CONV_PAD:seq + CONV_PAD])


def _l2norm_heads(x, n_heads, width, scale):
    outs = []
    for h in range(n_heads):
        xh = x[:, h * width:(h + 1) * width]
        outs.append(xh * (lax.rsqrt(jnp.sum(xh * xh, axis=-1, keepdims=True) + EPS) * scale))
    return jnp.concatenate(outs, axis=-1)


def _gdn_kernel(*refs, seq, n_seq, has_s0, emit_state):
    it = iter(refs)
    (q_ref, k_ref, v_ref, gb_ref, gate_ref, cw_ref, alog_ref, dt_ref, gn_ref,
     mg_ref, tt_ref, tri_ref) = [next(it) for _ in range(12)]
    s0_ref = next(it) if has_s0 else None
    o_ref = next(it)
    st_ref = next(it) if emit_state else None
    qn, kn, vn, u_s, w_s, qg_s, kdt_s, at_s, et_s, s_scr, acc = [next(it) for _ in range(11)]
    c = CHUNK
    n_chunks = seq // c
    n_blocks = n_seq * seq // GBLK
    wq = H_B * DK_B
    n_dh = 2 * H_B
    combos = [(d, h) for d in range(2) for h in range(H_B)]
    lanes = [slice(h * DK_B, (h + 1) * DK_B) for h in range(H_B)]

    for s in range(n_seq):
        sr = slice(s * seq, (s + 1) * seq)
        qn[sr, :] = _l2norm_heads(_conv_silu(q_ref[sr, :], cw_ref[:, 0:wq], seq), H_B, DK_B, DK_B ** -0.5)
        kn[sr, :] = _l2norm_heads(_conv_silu(k_ref[sr, :], cw_ref[:, wq:2 * wq], seq), H_B, DK_B, 1.0)
        vn[sr, :] = _conv_silu(v_ref[sr, :], cw_ref[:, 2 * wq:3 * wq], seq)
        for i in range(n_dh):
            s_scr[s * n_dh + i] = (s0_ref[s, 0, i // H_B, i % H_B] if has_s0 else jnp.zeros((DK_B, DV_B), F32))
    acc[...] = jnp.zeros(acc.shape, F32)

    eye = (lax.broadcasted_iota(jnp.int32, (GBLK, GBLK), 0)
           == lax.broadcasted_iota(jnp.int32, (GBLK, GBLK), 1)).astype(F32)
    eye_pk = (lax.broadcasted_iota(jnp.int32, (c, GBLK), 0)
              == lax.broadcasted_iota(jnp.int32, (c, GBLK), 1) % c).astype(F32)
    bwd_lane = lax.broadcasted_iota(jnp.int32, (1, LANES), 1) % n_dh >= H_B
    add = lambda a, b: a + b

    same_chunk = tri_ref[2].astype(BF16)

    def expand(pk):
        return jnp.concatenate([pk] * GD_SUB, axis=0) * same_chunk

    def pack(bd):
        return functools.reduce(add, [bd[s * c:(s + 1) * c] for s in range(GD_SUB)])

    def weights(hi, lo):
        return jnp.concatenate([expand(hi), expand(lo)], axis=1)

    def dot3_split(a_hi, a_lo, w2):
        m, n = a_hi.shape[0], w2.shape[1] // 2
        t = _dot(jnp.concatenate([a_hi, a_lo], axis=0), w2)
        return t[:m, :n] + t[m:, :n] + t[:m, n:]

    def block_body(it, carry):
        blks = [it * GD_BLOCKS_PER_ITER + o for o in range(GD_BLOCKS_PER_ITER)]
        units = [(o, d, h) for o in range(GD_BLOCKS_PER_ITER) for d, h in combos]
        idx = lambda d, h: d * H_B + h
        col = lambda x, j: jnp.broadcast_to(x[:, j:j + 1], (GBLK, DK_B))
        rows = [pl.ds(pl.multiple_of(b * GBLK, GBLK), GBLK) for b in blks]
        gates = [gate_ref[r, :] for r in rows]
        glog_all = [-jnp.exp(alog_ref[...]) * _softplus(x + dt_ref[...]) for x in gates]
        beta_all = [_sigmoid(x) for x in gates]
        g2 = [jnp.concatenate(_split2(x), axis=0) for x in glog_all]
        dg = [[_dot(mg_ref[d], x) for d in range(2)] for x in g2]
        dsel = [jnp.where(bwd_lane, x[1], x[0]) for x in dg]
        eg_all = [jnp.exp(x) for x in dsel]
        gt = [[_dot_tn(x, tt_ref[d]) for d in range(2)] for x in g2]
        qs = [[qn[r, ln] for ln in lanes] for r in rows]
        ks = [[kn[r, ln] for ln in lanes] for r in rows]
        vs = [[vn[r, ln] for ln in lanes] for r in rows]
        betas = [col(beta_all[o], n_dh + idx(d, h)) for o, d, h in units]
        kbs = [ks[o][h] * betas[u] for u, (o, d, h) in enumerate(units)]
        kb_of = {unit: kbs[u] for u, unit in enumerate(units)}
        kk = {(o, h): _dot_nt(jnp.concatenate([qs[o][h], kb_of[o, 0, h], kb_of[o, 1, h]], axis=0).astype(BF16),
                              ks[o][h].astype(BF16))
              for o in range(GD_BLOCKS_PER_ITER) for h in range(H_B)}
        decay = []
        for o, d, h in units:
            inside = tri_ref[d] > 0.0
            gd = col(dsel[o][:GBLK], idx(d, h)) - gt[o][d][idx(d, h):idx(d, h) + 1, :]
            decay.append(jnp.where(inside, jnp.exp(jnp.where(inside, gd, 0.0)), 0.0))
        attn = [kk[o, h][:GBLK] * decay[u] for u, (o, d, h) in enumerate(units)]
        p_pk = [pack(kk[o, h][(1 + d) * GBLK:(2 + d) * GBLK] * decay[u] * (1.0 - eye))
                for u, (o, d, h) in enumerate(units)]
        x_pk = [eye_pk - p for p in p_pk]
        p_sp = [_split2(p) for p in p_pk]
        p_w = [weights(*s) for s in p_sp]
        for _ in range(CHUNK.bit_length() - 2):
            p_pk = [dot3_split(*s, w) for s, w in zip(p_sp, p_w)]
            p_sp = [_split2(p) for p in p_pk]
            p_w = [weights(*s) for s in p_sp]
            x_pk = [x + dot3_split(*_split2(x), w) for x, w in zip(x_pk, p_w)]
        eg_col = [col(eg_all[o][:GBLK], idx(d, h)) for o, d, h in units]
        ekd_col = [col(eg_all[o][GBLK:2 * GBLK], idx(d, h)) for o, d, h in units]
        rhs = [_split2(jnp.concatenate([vs[o][h] * betas[u], kbs[u] * eg_col[u]], axis=1))
               for u, (o, d, h) in enumerate(units)]
        t_sp = [[expand(part) for part in _split2(x)] for x in x_pk]
        uw = [_dot(t[0], r[0]) + (_dot(t[0], r[1]) + _dot(t[1], r[0])) for t, r in zip(t_sp, rhs)]
        for u, (o, d, h) in enumerate(units):
            i = idx(d, h)
            qg = (qs[o][h] * eg_col[u]).astype(BF16)
            kdt_s[i, blks[o]] = (ks[o][h] * ekd_col[u]).T.astype(BF16)
            for s in range(GD_SUB):
                cn = blks[o] * GD_SUB + s
                r = slice(s * c, (s + 1) * c)
                u_s[i, cn] = uw[u][r, :DV_B]
                w_s[i, cn] = uw[u][r, DV_B:].astype(BF16)
                qg_s[i, cn] = qg[r]
                at_s[i, cn] = attn[u][r].astype(BF16)
                et_s[i, cn] = jnp.broadcast_to(eg_all[o][2 * GBLK + s:2 * GBLK + s + 1, i:i + 1], (SUBLANES, DV_B))
        return carry

    n_iter = n_blocks // GD_BLOCKS_PER_ITER
    lax.fori_loop(0, n_iter, block_body, 0, unroll=True)

    def chunk_body(n, carry):
        chains = [(s, d, h) for s in range(n_seq) for d, h in combos]
        idx = lambda d, h: d * H_B + h
        cn = {(s, d): s * n_chunks + (n if d == 0 else n_chunks - 1 - n) for s in range(n_seq) for d in range(2)}
        rows = {key: pl.ds(pl.multiple_of(v * c, c), c) for key, v in cn.items()}
        sub_of_row = lax.broadcasted_iota(jnp.int32, (GBLK, 1), 0) // c
        in_chunk = {key: sub_of_row == v % GD_SUB for key, v in cn.items()}
        st = {ch: s_scr[ch[0] * n_dh + idx(ch[1], ch[2])] for ch in chains}
        ws = {(s, d, h): _dot(jnp.concatenate([w_s[idx(d, h), cn[s, d]], qg_s[idx(d, h), cn[s, d]]], axis=0),
                              st[s, d, h].astype(BF16)) for s, d, h in chains}
        vblk = {(s, d, h): jnp.where(in_chunk[s, d],
                                     jnp.concatenate([u_s[idx(d, h), cn[s, d]] - ws[s, d, h][:c]] * GD_SUB, axis=0),
                                     0.0).astype(BF16) for s, d, h in chains}
        r = {(s, d, h): _dot(jnp.concatenate([at_s[idx(d, h), cn[s, d]], kdt_s[idx(d, h), cn[s, d] // GD_SUB]], axis=0),
                             vblk[s, d, h]) for s, d, h in chains}
        for s in range(n_seq):
            for d in range(2):
                acc[rows[s, d], :] += jnp.concatenate([ws[s, d, h][c:] + r[s, d, h][:c] for h in range(H_B)], axis=1)
        for s, d, h in chains:
            s_scr[s * n_dh + idx(d, h)] = st[s, d, h] * et_s[idx(d, h), cn[s, d]][0:1] + r[s, d, h][c:]
        return carry

    lax.fori_loop(0, n_chunks, chunk_body, 0)

    for h in range(H_B):
        ln = slice(h * DV_B, (h + 1) * DV_B)
        o_ref[:, ln] = _rms_gate(acc[:, ln], gn_ref[...], gb_ref[:, ln])
    if emit_state:
        for s in range(n_seq):
            for i in range(n_dh):
                st_ref[s, 0, i // H_B, i % H_B] = s_scr[s * n_dh + i]


def _gdn(proj, gates, conv_w, a_log, dt_bias, gn, consts, prompt, s0=None, layer=0):
    seq, nb, rb0 = _seq_layout(prompt)
    n_seq = GD_PROMPT_SEQS if prompt else 1
    rows = n_seq * seq
    n_chunks = rows // CHUNK
    wq = H_B * DK_B
    blk = lambda j: pl.BlockSpec((rows, wq), lambda b: (rb0 + b, j))
    const2 = lambda b: (0, 0)
    const3 = lambda b: (0, 0, 0)
    st_block = (n_seq, 1, 2, H_B, DK_B, DV_B)
    pad_row = lambda p: jnp.pad(p.reshape(1, -1).astype(F32), ((0, 0), (0, LANES - p.size)))
    in_specs = [blk(5), blk(6), blk(7), blk(8),
                pl.BlockSpec((rows, LANES), lambda b: (rb0 + b, 0)),
                pl.BlockSpec((SHORT_CONV, 3 * wq), const2),
                pl.BlockSpec((1, LANES), const2), pl.BlockSpec((1, LANES), const2), pl.BlockSpec((1, DV_B), const2)]
    in_specs += [pl.BlockSpec(m.shape, const3) for m in consts]
    args = [proj] * 4 + [gates, conv_w.reshape(SHORT_CONV, 3 * wq), pad_row(a_log), pad_row(dt_bias),
                         gn.reshape(1, DV_B)] + list(consts)
    if s0 is not None:
        in_specs.append(pl.BlockSpec(st_block, lambda b: (b, layer, 0, 0, 0, 0)))
        args.append(s0)
    out_specs = [pl.BlockSpec((rows, wq), lambda b: (b, 0))]
    out_shape = [jax.ShapeDtypeStruct((nb * seq, wq), F32)]
    if prompt:
        out_specs.append(pl.BlockSpec(st_block, lambda b: (b, 0, 0, 0, 0, 0)))
        out_shape.append(jax.ShapeDtypeStruct((nb, 1, 2, H_B, DK_B, DV_B), F32))
    n_dh = 2 * H_B
    scratch = ([pltpu.VMEM((rows, wq), F32)] * 3
               + [pltpu.VMEM((n_dh, n_chunks, CHUNK, DV_B), F32)]
               + [pltpu.VMEM((n_dh, n_chunks, CHUNK, DK_B), BF16)] * 2
               + [pltpu.VMEM((n_dh, rows // GBLK, DK_B, GBLK), BF16),
                  pltpu.VMEM((n_dh, n_chunks, CHUNK, GBLK), BF16),
                  pltpu.VMEM((n_dh, n_chunks, SUBLANES, DV_B), F32),
                  pltpu.VMEM((n_seq * n_dh, DK_B, DV_B), F32),
                  pltpu.VMEM((rows, wq), F32)])
    return pl.pallas_call(
        functools.partial(_gdn_kernel, seq=seq, n_seq=n_seq, has_s0=s0 is not None, emit_state=prompt),
        grid=(nb // n_seq,),
        in_specs=in_specs,
        out_specs=out_specs,
        out_shape=out_shape,
        scratch_shapes=scratch,
        compiler_params=_cparams("arbitrary"),
        name="gdn_prompt" if prompt else "gdn_sample",
    )(*args)


def kernel(x_prompt, x_sample, state_hgrn, state_gdn, cache_na_k, cache_na_v, c, c_ctx, ada_w, ada_b, norm_g, w_in_ab, w_out_ab, hgrn_lb, gdn_conv, gdn_a_log, gdn_dt_bias, gn_hgrn, gn_gdn, w_qkv_na, qn_na, kn_na, rpb_na, w_out_na, w_mlp1, w_mlp2):
    cond = jnp.concatenate([c_ctx[None, :], c, jnp.zeros((N_MOD_ROWS - 1 - DEC_BATCH, D_MODEL), F32)], axis=0)
    mods, w_in_t = _modulation(cond, ada_w, ada_b, jnp.swapaxes(w_in_ab, 1, 2))
    xs = (x_prompt.reshape(N_PROMPT, D_MODEL), x_sample.reshape(N_SAMPLE, D_MODEL))

    w_gate_t = jnp.pad(w_in_t[D_MAIN_AB:], ((0, LANES - N_GATE_AB), (0, 0)))
    proj, gates, wo0, w1_0, w2_0 = _norm_proj(xs, mods[0], norm_g[0, 0], [w_in_t, w_gate_t], widths=[D_MAIN_AB, LANES],
                                              side=((w_out_ab, 0), (w_mlp1, 0), (w_mlp2, 0)), w_transposed=True)
    hg_consts = _hgrn_consts()
    gd_consts = _gdn_consts()
    hg_prompt, new_hgrn = _hgrn(proj, hgrn_lb, gn_hgrn[0], hg_consts, True)
    hg_sample, = _hgrn(proj, hgrn_lb, gn_hgrn[0], hg_consts, False, s0=state_hgrn)
    gd_args = (gdn_conv[0], gdn_a_log[0], gdn_dt_bias[0], gn_gdn[0], gd_consts)
    gd_prompt, new_gdn = _gdn(proj, gates, *gd_args, True)
    gd_sample, = _gdn(proj, gates, *gd_args, False, s0=state_gdn)
    xs, (w_qkv, wo1, w1_1, w2_1) = _post_mixer(xs, [(hg_prompt, hg_sample), (gd_prompt, gd_sample)], mods[0],
                                               norm_g[0, 1], wo0, w1_0, w2_0,
                                               side=((w_qkv_na, 0), (w_out_na, 0), (w_mlp1, 1), (w_mlp2, 1)))

    qkv, = _norm_proj(xs, mods[1], norm_g[1, 0], [w_qkv], tm=1024)
    at_prompt, new_kt, new_vt = _ctx_attention(qkv, qn_na[0], kn_na[0])
    time_minor = lambda a: jnp.swapaxes(a, -1, -2).reshape(a.shape[0], 1, H_C * HD_C, a.shape[3])
    at_sample = _na_attention(qkv, time_minor(cache_na_k), time_minor(cache_na_v), qn_na[0], kn_na[0], rpb_na[0])
    time_major = lambda a: jnp.swapaxes(a.reshape(BATCH, 1, H_C, HD_C, SEQ), -1, -2)
    new_k, new_v = time_major(new_kt), time_major(new_vt)
    (y_prompt, y_sample), _ = _post_mixer(xs, [(at_prompt, at_sample)], mods[1], norm_g[1, 1], wo1, w1_1, w2_1,
                                          split_out=True)

    return (y_prompt.reshape(BATCH, SEQ, D_MODEL), y_sample.reshape(DEC_BATCH, DEC_SEQ, D_MODEL),
            new_hgrn, new_gdn, new_k, new_v)
```
